```python
import jax, jax.numpy as jnp
from jax import lax
import numpy as np

D_MODEL = 1024
BATCH = 16
SEQ = 256
DEPTH = 1
DEC_BATCH = 8
DEC_SEQ = 4096
PAST_LEN = 256

GRID_W = 64
MLA_HEADS = 4
MLA_NOPE = 128
MLA_ROPE = 64
MLA_QK = MLA_NOPE + MLA_ROPE
MLA_V = 128
Q_RANK = 512
KV_RANK = 256
ROPE_AXIS = MLA_ROPE // 2
ROPE_BASE = 10000.0
Q_BLOCK = 128
RET_HEADS = 4
RET_DK = 128
RET_DV = 128
RET_CHUNK = 128
IN_SIZES = (Q_RANK, KV_RANK, MLA_ROPE, RET_HEADS * RET_DK, RET_HEADS * RET_DK, RET_HEADS * RET_DV, RET_HEADS * RET_DV)
IN_DIM = Q_RANK + KV_RANK + MLA_ROPE + 2 * RET_HEADS * RET_DK + 2 * RET_HEADS * RET_DV
MIX_DIM = MLA_HEADS * MLA_V + RET_HEADS * RET_DV
PEER_HEADS = 8
PEER_QDIM = 256
PEER_HALF = PEER_QDIM // 2
N_KEYS = 128
N_EXPERTS = N_KEYS * N_KEYS
PEER_TOPK = 16
PEER_BLOCK = 128
EPS = 1e-6

kernel_name = "hybrid_mla_retention_peer_diffusion_step"


def rms_norm(x, w):
    xf = x.astype(jnp.float32)
    y = xf * lax.rsqrt(jnp.mean(xf * xf, axis=-1, keepdims=True) + EPS)
    return (y * w.astype(jnp.float32)).astype(x.dtype)


def ada_split(cond, w_ada, b_ada):
    m = (jax.nn.silu(cond) @ w_ada + b_ada)[:, None, :]
    return jnp.split(m, 6, axis=-1)


def modulate(x, norm_w, shift, scale):
    return rms_norm(x, norm_w) * (1.0 + scale) + shift


def axial_angles(S):
    rows = S // GRID_W
    r = jnp.repeat(jnp.arange(rows, dtype=jnp.float32), GRID_W)
    col = jnp.tile(jnp.arange(GRID_W, dtype=jnp.float32), rows)
    nf = ROPE_AXIS // 2
    freqs = jnp.power(ROPE_BASE, -jnp.arange(nf, dtype=jnp.float32) / nf)
    return r[:, None] * freqs, col[:, None] * freqs


def rope_1d(x, ang):
    x1, x2 = jnp.split(x, 2, axis=-1)
    cos = jnp.cos(ang)[None, :, None, :].astype(x.dtype)
    sin = jnp.sin(ang)[None, :, None, :].astype(x.dtype)
    return jnp.concatenate([x1 * cos - x2 * sin, x2 * cos + x1 * sin], axis=-1)


def rope_2d(x, ang_r, ang_c):
    xn = x[..., :MLA_NOPE]
    xr = x[..., MLA_NOPE:MLA_NOPE + ROPE_AXIS]
    xc = x[..., MLA_NOPE + ROPE_AXIS:]
    return jnp.concatenate([xn, rope_1d(xr, ang_r), rope_1d(xc, ang_c)], axis=-1)


def mla_kv(ckv, krope, w_ukv, k_head_norm_w):
    B, S, _ = ckv.shape
    kv = (ckv @ w_ukv).reshape(B, S, MLA_HEADS, MLA_NOPE + MLA_V)
    k_nope, v = kv[..., :MLA_NOPE], kv[..., MLA_NOPE:]
    k_r = jnp.broadcast_to(krope[:, :, None, :], (B, S, MLA_HEADS, MLA_ROPE))
    k = rms_norm(jnp.concatenate([k_nope, k_r], axis=-1), k_head_norm_w)
    return k, v


def block_attention(q, k, v):
    B, S, H, Dq = q.shape
    nb = S // Q_BLOCK
    qb = q.reshape(B, nb, Q_BLOCK, H, Dq).transpose(1, 0, 2, 3, 4)
    scale = Dq ** -0.5

    def attend(qblk):
        s = jnp.einsum('bqhd,bkhd->bhqk', qblk, k).astype(jnp.float32) * scale
        p = jax.nn.softmax(s, axis=-1).astype(v.dtype)
        return jnp.einsum('bhqk,bkhv->bqhv', p, v)

    o = lax.map(attend, qb)
    return o.transpose(1, 0, 2, 3, 4).reshape(B, S, H * v.shape[-1])


def retention_chunkwise(q, k, v, log_gamma, s0):
    B, S, H, DK = q.shape
    DV = v.shape[-1]
    n = S // RET_CHUNK

    def chunks(a):
        return a.reshape(B, n, RET_CHUNK, H, a.shape[-1]).transpose(1, 0, 2, 3, 4)

    pos = jnp.arange(RET_CHUNK, dtype=jnp.float32)
    rel = pos[:, None] - pos[None, :]
    intra = jnp.where(rel[None] >= 0, jnp.exp(jnp.maximum(rel, 0.0)[None] * log_gamma[:, None, None]), 0.0)
    q_dec = jnp.exp((pos[:, None] + 1.0) * log_gamma[None, :])[None, :, :, None]
    k_dec = jnp.exp((RET_CHUNK - 1.0 - pos)[:, None] * log_gamma[None, :])[None, :, :, None]
    c_dec = jnp.exp(RET_CHUNK * log_gamma)[None, :, None, None]

    def step(s, blk):
        qc, kc, vc = blk
        qf, kf, vf = qc.astype(jnp.float32), kc.astype(jnp.float32), vc.astype(jnp.float32)
        a = jnp.einsum('bihd,bjhd->bhij', qf, kf) * intra
        o = jnp.einsum('bhij,bjhv->bihv', a, vf) + jnp.einsum('bihd,bhdv->bihv', qf, s) * q_dec
        s_new = s * c_dec + jnp.einsum('bjhd,bjhv->bhdv', kf * k_dec, vf)
        return s_new, o

    s_fin, o = lax.scan(step, s0.astype(jnp.float32), (chunks(q), chunks(k), chunks(v)))
    return o.transpose(1, 0, 2, 3, 4).reshape(B, S, H, DV), s_fin


def retention_out(o, g, gn_w):
    B, S, H, DV = o.shape
    mu = jnp.mean(o, axis=-1, keepdims=True)
    var = jnp.mean(jnp.square(o - mu), axis=-1, keepdims=True)
    o = ((o - mu) * lax.rsqrt(var + EPS)).reshape(B, S, H * DV) * gn_w.astype(jnp.float32)
    return jax.nn.silu(g) * o.astype(g.dtype)


def peer(h, w_pq, sub_keys1, sub_keys2, u_table, v_table):
    B, S, D = h.shape
    hb = h.reshape((B * S) // PEER_BLOCK, PEER_BLOCK, D)

    def block(xb):
        q = (xb @ w_pq).reshape(PEER_BLOCK, PEER_HEADS, PEER_QDIM)
        s1 = jnp.einsum('thd,kd->thk', q[..., :PEER_HALF], sub_keys1).astype(jnp.float32)
        s2 = jnp.einsum('thd,kd->thk', q[..., PEER_HALF:], sub_keys2).astype(jnp.float32)
        v1, i1 = lax.top_k(s1, PEER_TOPK)
        v2, i2 = lax.top_k(s2, PEER_TOPK)
        cand = (v1[..., :, None] + v2[..., None, :]).reshape(PEER_BLOCK, PEER_HEADS, PEER_TOPK * PEER_TOPK)
        cidx = (i1[..., :, None] * N_KEYS + i2[..., None, :]).reshape(PEER_BLOCK, PEER_HEADS, PEER_TOPK * PEER_TOPK)
        top_s, sel = lax.top_k(cand, PEER_TOPK)
        eidx = jnp.take_along_axis(cidx, sel, axis=-1)
        g = jax.nn.softmax(top_s, axis=-1).astype(xb.dtype)
        act = jax.nn.gelu(jnp.einsum('thkd,td->thk', u_table[eidx], xb), approximate=False)
        return jnp.einsum('thk,thkd->td', g * act, v_table[eidx])

    return lax.map(block, hb).reshape(B, S, D)


def trunk_layer(x, cond, p, ctx):
    B, S, _ = x.shape
    sh1, sc1, g1, sh2, sc2, g2 = ada_split(cond, p['w_ada'], p['b_ada'])
    h = modulate(x, p['norm1_w'], sh1, sc1)
    z = h @ p['w_in']
    split_at = [int(i) for i in np.cumsum(IN_SIZES)[:-1]]
    cq, ckv, krope, qr, kr, vr, gr = jnp.split(z, split_at, axis=-1)
    ckv = rms_norm(ckv, p['kv_norm_w'])
    q = (rms_norm(cq, p['q_norm_w']) @ p['w_uq']).reshape(B, S, MLA_HEADS, MLA_QK)
    q = rms_norm(q, p['q_head_norm_w'])
    k, v = mla_kv(ckv, krope, p['w_ukv'], p['k_head_norm_w'])
    qr = qr.reshape(B, S, RET_HEADS, RET_DK)
    kr = kr.reshape(B, S, RET_HEADS, RET_DK) * (RET_DK ** -0.5)
    vr = vr.reshape(B, S, RET_HEADS, RET_DV)
    log_gamma = jax.nn.log_sigmoid(p['ret_decay_logit'].astype(jnp.float32))
    if ctx is None:
        s0_f = jnp.zeros((B, RET_HEADS, RET_DK, RET_DV), jnp.float32)
        s0_b = s0_f
    else:
        ckv_c, krope_c, state_c = ctx
        ang_r, ang_c = axial_angles(S)
        q = rope_2d(q, ang_r, ang_c)
        k = rope_2d(k, ang_r, ang_c)
        k_c, v_c = mla_kv(ckv_c, krope_c, p['w_ukv'], p['k_head_norm_w'])
        k = jnp.concatenate([k_c, k], axis=1)
        v = jnp.concatenate([v_c, v], axis=1)
        s0_f, s0_b = state_c[:, 0], state_c[:, 1]
    att = block_attention(q, k, v)
    o_f, s_f = retention_chunkwise(qr, kr, vr, log_gamma[0], s0_f)
    o_b, s_b = retention_chunkwise(qr[:, ::-1], kr[:, ::-1], vr[:, ::-1], log_gamma[1], s0_b)
    ret = retention_out(o_f + o_b[:, ::-1], gr, p['ret_gn_w'])
    x = x + g1 * (jnp.concatenate([att, ret], axis=-1) @ p['w_out'])
    h2 = modulate(x, p['norm2_w'], sh2, sc2)
    x = x + g2 * peer(h2, p['w_pq'], p['sub_keys1'], p['sub_keys2'], p['u_table'], p['v_table'])
    return x, ckv, krope, jnp.stack([s_f, s_b], axis=1).astype(x.dtype)


def setup_inputs(seed: int = 0) -> dict:
    key = jax.random.key(seed)
    ks = jax.random.split(key, 32)
    D = D_MODEL

    def nrm(k, shape, scale):
        return jax.random.normal(k, shape, jnp.float32) * scale

    def gain(k, shape):
        return 1.0 + 0.02 * jax.random.normal(k, shape, jnp.float32)

    base_logit = jnp.log(jnp.power(2.0, 5.0 + jnp.arange(RET_HEADS, dtype=jnp.float32)) - 1.0)
    return {
        "x_prompt": nrm(ks[0], (BATCH, SEQ, D), 1.0),
        "x_sample": nrm(ks[1], (DEC_BATCH, DEC_SEQ, D), 1.0),
        "c": nrm(ks[2], (DEC_BATCH, D), 1.0),
        "cache_ckv": nrm(ks[3], (DEC_BATCH, DEPTH, PAST_LEN, KV_RANK), 1.0),
        "cache_krope": nrm(ks[4], (DEC_BATCH, DEPTH, PAST_LEN, MLA_ROPE), 1.0),
        "state_ret": nrm(ks[5], (DEC_BATCH, DEPTH, 2, RET_HEADS, RET_DK, RET_DV), 0.5),
        "c_ctx": nrm(ks[6], (D,), 1.0),
        "w_ada": nrm(ks[7], (DEPTH, D, 6 * D), 0.5 * D ** -0.5),
        "b_ada": nrm(ks[8], (DEPTH, 6 * D), 0.02),
        "norm1_w": gain(ks[9], (DEPTH, D)),
        "w_in": nrm(ks[10], (DEPTH, D, IN_DIM), D ** -0.5),
        "q_norm_w": gain(ks[11], (DEPTH, Q_RANK)),
        "w_uq": nrm(ks[12], (DEPTH, Q_RANK, MLA_HEADS * MLA_QK), Q_RANK ** -0.5),
        "kv_norm_w": gain(ks[13], (DEPTH, KV_RANK)),
        "w_ukv": nrm(ks[14], (DEPTH, KV_RANK, MLA_HEADS * (MLA_NOPE + MLA_V)), KV_RANK ** -0.5),
        "q_head_norm_w": gain(ks[15], (DEPTH, MLA_QK)),
        "k_head_norm_w": gain(ks[16], (DEPTH, MLA_QK)),
        "ret_decay_logit": base_logit[None, None, :] + 0.1 * jax.random.normal(ks[17], (DEPTH, 2, RET_HEADS), jnp.float32),
        "ret_gn_w": gain(ks[18], (DEPTH, RET_HEADS * RET_DV)),
        "w_out": nrm(ks[19], (DEPTH, MIX_DIM, D), MIX_DIM ** -0.5),
        "norm2_w": gain(ks[20], (DEPTH, D)),
        "w_pq": nrm(ks[21], (DEPTH, D, PEER_HEADS * PEER_QDIM), D ** -0.5),
        "sub_keys1": nrm(ks[22], (DEPTH, N_KEYS, PEER_HALF), PEER_HALF ** -0.5),
        "sub_keys2": nrm(ks[23], (DEPTH, N_KEYS, PEER_HALF), PEER_HALF ** -0.5),
        "u_table": nrm(ks[24], (DEPTH, N_EXPERTS, D), D ** -0.5),
        "v_table": nrm(ks[25], (DEPTH, N_EXPERTS, D), 0.5),
    }


def reference(x_prompt, x_sample, c, cache_ckv, cache_krope, state_ret, c_ctx, w_ada, b_ada, norm1_w, w_in,
              q_norm_w, w_uq, kv_norm_w, w_ukv, q_head_norm_w, k_head_norm_w, ret_decay_logit, ret_gn_w,
              w_out, norm2_w, w_pq, sub_keys1, sub_keys2, u_table, v_table):
    y_prompt, y_sample = x_prompt, x_sample
    ckv_list, krope_list, ret_list = [], [], []
    for l in range(DEPTH):
        p = {
            'w_ada': w_ada[l], 'b_ada': b_ada[l], 'norm1_w': norm1_w[l], 'w_in': w_in[l],
            'q_norm_w': q_norm_w[l], 'w_uq': w_uq[l], 'kv_norm_w': kv_norm_w[l], 'w_ukv': w_ukv[l],
            'q_head_norm_w': q_head_norm_w[l], 'k_head_norm_w': k_head_norm_w[l],
            'ret_decay_logit': ret_decay_logit[l], 'ret_gn_w': ret_gn_w[l], 'w_out': w_out[l],
            'norm2_w': norm2_w[l], 'w_pq': w_pq[l], 'sub_keys1': sub_keys1[l], 'sub_keys2': sub_keys2[l],
            'u_table': u_table[l], 'v_table': v_table[l],
        }
        y_prompt, ckv_l, krope_l, ret_l = trunk_layer(y_prompt, c_ctx[None, :], p, None)
        ckv_list.append(ckv_l)
        krope_list.append(krope_l)
        ret_list.append(ret_l)
        y_sample, _, _, _ = trunk_layer(y_sample, c, p, (cache_ckv[:, l], cache_krope[:, l], state_ret[:, l]))
    new_cache_ckv = jnp.stack(ckv_list, axis=1)
    new_cache_krope = jnp.stack(krope_list, axis=1)
    new_state_ret = jnp.stack(ret_list, axis=1)
    return (y_prompt, y_sample, new_cache_ckv, new_cache_krope, new_state_ret)
```

```python
import functools

import jax
import jax.numpy as jnp
from jax import lax
from jax.experimental import pallas as pl
from jax.experimental.pallas import tpu as pltpu

F32 = jnp.float32
BF16 = jnp.bfloat16

D_MODEL = 1024
GRID_W = 64
MLA_HEADS = 4
MLA_NOPE = 128
MLA_ROPE = 64
MLA_QK = MLA_NOPE + MLA_ROPE
MLA_V = 128
Q_RANK = 512
KV_RANK = 256
ROPE_AXIS = MLA_ROPE // 2
ROPE_BASE = 10000.0
RET_HEADS = 4
RET_DK = 128
RET_DV = 128
RET_CHUNK = 128
PEER_HEADS = 8
PEER_QDIM = 256
PEER_HALF = PEER_QDIM // 2
N_KEYS = 128
PEER_TOPK = 16
EPS = 1e-6

LANE = 128
HEAD_PAD = 2 * LANE
RET_W = 4 * RET_HEADS * RET_DK
IN_PAD = Q_RANK + KV_RANK + LANE + RET_W
VMEM_LIMIT = 48 * 1024 * 1024

ROW_TILE = 256
Q_TILE = 256
SCORE_TILE = 256
GATHER_TILE = 128
GATHER_SLOTS = 8
GATHER_AHEAD = GATHER_SLOTS - 1
EXPERTS_PER_TOKEN = PEER_HEADS * PEER_TOPK


def _params(*sem):
    return pltpu.CompilerParams(dimension_semantics=sem, vmem_limit_bytes=VMEM_LIMIT)


def _rms(x, w):
    return x * lax.rsqrt(jnp.mean(x * x, axis=-1, keepdims=True) + EPS) * w


def _mm(a, b):
    return jnp.dot(a.astype(BF16), b.astype(BF16), preferred_element_type=F32)


def _mm_nt(a, b):
    return lax.dot_general(a.astype(BF16), b.astype(BF16), (((1,), (1,)), ((), ())),
                           preferred_element_type=F32)


def _mm_tn(a, b):
    return lax.dot_general(a.astype(BF16), b.astype(BF16), (((0,), (0,)), ((), ())),
                           preferred_element_type=F32)


def _ada_kernel(c_ref, w_ref, b_ref, o_ref):
    c = c_ref[...]
    o_ref[...] = _mm(c * jax.nn.sigmoid(c), w_ref[...]) + b_ref[...]


def _ada(cond, w_ada, b_ada):
    rows, d = cond.shape
    n = w_ada.shape[1]
    tn = 1536
    return pl.pallas_call(
        _ada_kernel,
        grid=(n // tn,),
        in_specs=[pl.BlockSpec((rows, d), lambda j: (0, 0)),
                  pl.BlockSpec((d, tn), lambda j: (0, j)),
                  pl.BlockSpec((1, tn), lambda j: (0, j))],
        out_specs=pl.BlockSpec((rows, tn), lambda j: (0, j)),
        out_shape=jax.ShapeDtypeStruct((rows, n), F32),
        compiler_params=_params("arbitrary"),
        name="ada",
    )(cond, w_ada, b_ada.reshape(1, n))


def _rope_tile(x, cos, sin):
    lane = lax.broadcasted_iota(jnp.int32, x.shape, 1)
    partner = jnp.where((lane % 32) < 16, pltpu.roll(x, LANE - 16, 1), pltpu.roll(x, 16, 1))
    return x * cos + partner * sin


def _kv_heads(kv, kr, khw, cos, sin, use_rope):
    krw = kr * khw[:, LANE:]
    if use_rope:
        krw = _rope_tile(krw, cos, sin)
    ssq_r = jnp.sum(kr * kr, axis=-1, keepdims=True)
    ks, vs = [], []
    for hd in range(MLA_HEADS):
        kn = kv[:, hd * HEAD_PAD: hd * HEAD_PAD + LANE]
        r = lax.rsqrt((jnp.sum(kn * kn, axis=-1, keepdims=True) + ssq_r) / MLA_QK + EPS)
        ks += [kn * r * khw[:, :LANE], krw * r]
        vs.append(kv[:, hd * HEAD_PAD + LANE: (hd + 1) * HEAD_PAD])
    return jnp.concatenate(ks, axis=-1).astype(BF16), jnp.concatenate(vs, axis=-1).astype(BF16)


def _inproj_kernel(x_ref, mod_ref, n1w_ref, win_ref, qnw_ref, kvnw_ref, wuq_ref, wukv_ref, qhw_ref, khw_ref,
                   cos_ref, sin_ref, ckv_ref, krope_ref, q_ref, k_ref, v_ref, zr_ref, *, use_rope):
    m = mod_ref[0]
    h = _rms(x_ref[...], n1w_ref[...]) * (1.0 + m[1:2]) + m[0:1]
    z = _mm(h, win_ref[...])
    kr = z[:, Q_RANK + KV_RANK: Q_RANK + KV_RANK + LANE]
    zr_ref[...] = z[:, Q_RANK + KV_RANK + LANE:]
    ckvn = _rms(z[:, Q_RANK: Q_RANK + KV_RANK], kvnw_ref[...])
    ckv_ref[...] = ckvn
    krope_ref[...] = kr[:, :MLA_ROPE]
    cos, sin = cos_ref[...], sin_ref[...]

    q = _mm(_rms(z[:, :Q_RANK], qnw_ref[...]), wuq_ref[...])
    qhw = qhw_ref[...]
    qs = []
    for hd in range(MLA_HEADS):
        qn = q[:, hd * HEAD_PAD: hd * HEAD_PAD + LANE]
        qr = q[:, hd * HEAD_PAD + LANE: (hd + 1) * HEAD_PAD]
        ssq = jnp.sum(qn * qn, axis=-1, keepdims=True) + jnp.sum(qr * qr, axis=-1, keepdims=True)
        r = lax.rsqrt(ssq / MLA_QK + EPS)
        qrw = qr * qhw[:, LANE:]
        if use_rope:
            qrw = _rope_tile(qrw, cos, sin)
        qs += [qn * r * qhw[:, :LANE], qrw * r]
    q_ref[...] = jnp.concatenate(qs, axis=-1).astype(BF16)

    k, v = _kv_heads(_mm(ckvn, wukv_ref[...]), kr, khw_ref[...], cos, sin, use_rope)
    k_ref[...] = k
    v_ref[...] = v


def _in_proj(x, mod, w, cos, sin, seq, use_rope):
    t = x.shape[0]
    tm = ROW_TILE
    per_batch = seq // tm
    const = lambda i: (0, 0)
    row = lambda i: (i, 0)
    if use_rope:
        pos = lambda i: (i % per_batch, 0)
    else:
        pos = const
    hp = MLA_HEADS * HEAD_PAD
    return pl.pallas_call(
        functools.partial(_inproj_kernel, use_rope=use_rope),
        grid=(t // tm,),
        in_specs=[pl.BlockSpec((tm, D_MODEL), row),
                  pl.BlockSpec((1, 6, D_MODEL), lambda i: (i // per_batch, 0, 0)),
                  pl.BlockSpec((1, D_MODEL), const),
                  pl.BlockSpec((D_MODEL, IN_PAD), const),
                  pl.BlockSpec((1, Q_RANK), const),
                  pl.BlockSpec((1, KV_RANK), const),
                  pl.BlockSpec((Q_RANK, hp), const),
                  pl.BlockSpec((KV_RANK, hp), const),
                  pl.BlockSpec((1, HEAD_PAD), const),
                  pl.BlockSpec((1, HEAD_PAD), const),
                  pl.BlockSpec((tm, LANE), pos),
                  pl.BlockSpec((tm, LANE), pos)],
        out_specs=[pl.BlockSpec((tm, KV_RANK), row),
                   pl.BlockSpec((tm, MLA_ROPE), row),
                   pl.BlockSpec((tm, hp), row),
                   pl.BlockSpec((tm, hp), row),
                   pl.BlockSpec((tm, MLA_HEADS * MLA_V), row),
                   pl.BlockSpec((tm, RET_W), row)],
        out_shape=[jax.ShapeDtypeStruct((t, KV_RANK), F32),
                   jax.ShapeDtypeStruct((t, MLA_ROPE), F32),
                   jax.ShapeDtypeStruct((t, hp), BF16),
                   jax.ShapeDtypeStruct((t, hp), BF16),
                   jax.ShapeDtypeStruct((t, MLA_HEADS * MLA_V), BF16),
                   jax.ShapeDtypeStruct((t, RET_W), F32)],
        compiler_params=_params("arbitrary"),
        name="in_proj",
    )(x, mod, w["norm1_w"], w["w_in"], w["q_norm_w"], w["kv_norm_w"], w["w_uq"], w["w_ukv"],
      w["q_head_norm_w"], w["k_head_norm_w"], cos, sin)


def _kvup_kernel(ckv_ref, kr_ref, wukv_ref, khw_ref, k_ref, v_ref):
    k, v = _kv_heads(_mm(ckv_ref[...], wukv_ref[...]), kr_ref[...], khw_ref[...], None, None, False)
    k_ref[...] = k
    v_ref[...] = v


def _kv_up(ckv, kr, w):
    t = ckv.shape[0]
    tm = ROW_TILE
    hp = MLA_HEADS * HEAD_PAD
    const = lambda i: (0, 0)
    row = lambda i: (i, 0)
    return pl.pallas_call(
        _kvup_kernel,
        grid=(t // tm,),
        in_specs=[pl.BlockSpec((tm, KV_RANK), row),
                  pl.BlockSpec((tm, LANE), row),
                  pl.BlockSpec((KV_RANK, hp), const),
                  pl.BlockSpec((1, HEAD_PAD), const)],
        out_specs=[pl.BlockSpec((tm, hp), row),
                   pl.BlockSpec((tm, MLA_HEADS * MLA_V), row)],
        out_shape=[jax.ShapeDtypeStruct((t, hp), BF16),
                   jax.ShapeDtypeStruct((t, MLA_HEADS * MLA_V), BF16)],
        compiler_params=_params("arbitrary"),
        name="kv_up",
    )(ckv, kr, w["w_ukv"], w["k_head_norm_w"])


def _attn_kernel(*refs, has_ctx):
    if has_ctx:
        q_ref, k_ref, v_ref, kc_ref, vc_ref, o_ref = refs
    else:
        q_ref, k_ref, v_ref, o_ref = refs
    scale = MLA_QK ** -0.5
    q = q_ref[...]
    s = _mm_nt(q, k_ref[...]) * scale
    m = jnp.max(s, axis=-1, keepdims=True)
    if has_ctx:
        sc = _mm_nt(q, kc_ref[...]) * scale
        m = jnp.maximum(m, jnp.max(sc, axis=-1, keepdims=True))
    p = jnp.exp(s - m)
    den = jnp.sum(p, axis=-1, keepdims=True)
    o = _mm(p, v_ref[...])
    if has_ctx:
        pc = jnp.exp(sc - m)
        den = den + jnp.sum(pc, axis=-1, keepdims=True)
        o = o + _mm(pc, vc_ref[...])
    o_ref[...] = (o / den).astype(BF16)


def _attention(q, k, v, batch, seq, ctx_kv):
    tq = Q_TILE
    nq = seq // tq
    has_ctx = ctx_kv is not None
    qmap = lambda b, h, i: (b * nq + i, h)
    kvmap = lambda b, h, i: (b, h)
    in_specs = [pl.BlockSpec((tq, HEAD_PAD), qmap),
                pl.BlockSpec((seq, HEAD_PAD), kvmap),
                pl.BlockSpec((seq, MLA_V), kvmap)]
    args = [q, k, v]
    if has_ctx:
        kc, vc = ctx_kv
        past = kc.shape[0] // batch
        in_specs += [pl.BlockSpec((past, HEAD_PAD), kvmap), pl.BlockSpec((past, MLA_V), kvmap)]
        args += [kc, vc]
    return pl.pallas_call(
        functools.partial(_attn_kernel, has_ctx=has_ctx),
        grid=(batch, MLA_HEADS, nq),
        in_specs=in_specs,
        out_specs=pl.BlockSpec((tq, MLA_V), qmap),
        out_shape=jax.ShapeDtypeStruct((batch * seq, MLA_HEADS * MLA_V), BF16),
        compiler_params=_params("arbitrary", "arbitrary", "arbitrary"),
        name="attention",
    )(*args)


def _ret_kernel(*refs, has_s0):
    if has_s0:
        (lg_ref, qf_ref, kf_ref, vf_ref, qb_ref, kb_ref, vb_ref, s0f_ref, s0b_ref,
         of_ref, ob_ref, sf_out, sb_out, sf_scr, sb_scr) = refs
    else:
        (lg_ref, qf_ref, kf_ref, vf_ref, qb_ref, kb_ref, vb_ref,
         of_ref, ob_ref, sf_out, sb_out, sf_scr, sb_scr) = refs
    hd = pl.program_id(1)
    c = pl.program_id(2)
    cs = RET_CHUNK

    @pl.when(c == 0)
    def _():
        if has_s0:
            sf_scr[...] = s0f_ref[0]
            sb_scr[...] = s0b_ref[0]
        else:
            sf_scr[...] = jnp.zeros_like(sf_scr)
            sb_scr[...] = jnp.zeros_like(sb_scr)

    lgf = jax.nn.log_sigmoid(jnp.full((1, LANE), lg_ref[0, hd], F32))
    lgb = jax.nn.log_sigmoid(jnp.full((1, LANE), lg_ref[1, hd], F32))
    ii = lax.broadcasted_iota(jnp.int32, (cs, cs), 0).astype(F32)
    jj = lax.broadcasted_iota(jnp.int32, (cs, cs), 1).astype(F32)
    rel = ii - jj
    kscale = RET_DK ** -0.5

    def chunk(q, k, v, s, intra, qdec, kdec, cdec):
        a = _mm_nt(q, k) * intra
        o = _mm(a, v) + _mm(q, s) * qdec
        return o, s * cdec + _mm_tn(k * kdec, v)

    intra_f = jnp.where(rel >= 0, jnp.exp(jnp.maximum(rel, 0.0) * lgf), 0.0)
    o_f, s_f = chunk(qf_ref[...], kf_ref[...] * kscale, vf_ref[...], sf_scr[...], intra_f,
                     jnp.exp((ii + 1.0) * lgf), jnp.exp((cs - 1.0 - ii) * lgf), jnp.exp(cs * lgf))
    of_ref[...] = o_f
    sf_scr[...] = s_f
    sf_out[0] = s_f

    intra_b = jnp.where(rel <= 0, jnp.exp(jnp.maximum(-rel, 0.0) * lgb), 0.0)
    o_b, s_b = chunk(qb_ref[...], kb_ref[...] * kscale, vb_ref[...], sb_scr[...], intra_b,
                     jnp.exp((cs - ii) * lgb), jnp.exp(ii * lgb), jnp.exp(cs * lgb))
    ob_ref[...] = o_b
    sb_scr[...] = s_b
    sb_out[0] = s_b


def _retention(zr, decay_logit, batch, seq, s0):
    cs = RET_CHUNK
    nc = seq // cs
    nh = RET_HEADS
    has_s0 = s0 is not None

    def fwd(col):
        return pl.BlockSpec((cs, RET_DK), lambda b, h, c: (b * nc + c, col * nh + h))

    def bwd(col):
        return pl.BlockSpec((cs, RET_DK), lambda b, h, c: (b * nc + nc - 1 - c, col * nh + h))

    def state(d):
        return pl.BlockSpec((1, RET_DK, RET_DV), lambda b, h, c: (b * 2 * nh + d * nh + h, 0, 0))

    in_specs = [pl.BlockSpec(memory_space=pltpu.SMEM), fwd(0), fwd(1), fwd(2), bwd(0), bwd(1), bwd(2)]
    args = [decay_logit, zr, zr, zr, zr, zr, zr]
    if has_s0:
        in_specs += [state(0), state(1)]
        args += [s0, s0]
    t = batch * seq
    o_f, o_b, s_f, s_b = pl.pallas_call(
        functools.partial(_ret_kernel, has_s0=has_s0),
        grid=(batch, nh, nc),
        in_specs=in_specs,
        out_specs=[pl.BlockSpec((cs, RET_DV), lambda b, h, c: (b * nc + c, h)),
                   pl.BlockSpec((cs, RET_DV), lambda b, h, c: (b * nc + nc - 1 - c, h)),
                   pl.BlockSpec((1, RET_DK, RET_DV), lambda b, h, c: (b * nh + h, 0, 0)),
                   pl.BlockSpec((1, RET_DK, RET_DV), lambda b, h, c: (b * nh + h, 0, 0))],
        out_shape=[jax.ShapeDtypeStruct((t, nh * RET_DV), F32),
                   jax.ShapeDtypeStruct((t, nh * RET_DV), F32),
                   jax.ShapeDtypeStruct((batch * nh, RET_DK, RET_DV), F32),
                   jax.ShapeDtypeStruct((batch * nh, RET_DK, RET_DV), F32)],
        scratch_shapes=[pltpu.VMEM((RET_DK, RET_DV), F32), pltpu.VMEM((RET_DK, RET_DV), F32)],
        compiler_params=_params("arbitrary", "arbitrary", "arbitrary"),
        name="retention",
    )(*args)
    return o_f, o_b, s_f, s_b


def _mix_kernel(x_ref, att_ref, of_ref, ob_ref, gr_ref, mod_ref, gnw_ref, wout_ref, n2w_ref, x1_ref, h2_ref):
    o = of_ref[...] + ob_ref[...]
    parts = []
    for hd in range(RET_HEADS):
        oh = o[:, hd * RET_DV:(hd + 1) * RET_DV]
        d = oh - jnp.mean(oh, axis=-1, keepdims=True)
        parts.append(d * lax.rsqrt(jnp.mean(d * d, axis=-1, keepdims=True) + EPS))
    g = gr_ref[...]
    ret = (g * jax.nn.sigmoid(g)) * (jnp.concatenate(parts, axis=-1) * gnw_ref[...])
    na = MLA_HEADS * MLA_V
    mixed = _mm(att_ref[...], wout_ref[:na, :]) + _mm(ret, wout_ref[na:, :])
    m = mod_ref[0]
    x1 = x_ref[...] + m[2:3] * mixed
    x1_ref[...] = x1
    h2_ref[...] = _rms(x1, n2w_ref[...]) * (1.0 + m[4:5]) + m[3:4]


def _mix(x, att, o_f, o_b, zr, mod, w, seq):
    t = x.shape[0]
    tm = ROW_TILE
    per_batch = seq // tm
    const = lambda i: (0, 0)
    row = lambda i: (i, 0)
    half = RET_HEADS * RET_DV
    return pl.pallas_call(
        _mix_kernel,
        grid=(t // tm,),
        in_specs=[pl.BlockSpec((tm, D_MODEL), row),
                  pl.BlockSpec((tm, half), row),
                  pl.BlockSpec((tm, half), row),
                  pl.BlockSpec((tm, half), row),
                  pl.BlockSpec((tm, half), lambda i: (i, 3)),
                  pl.BlockSpec((1, 6, D_MODEL), lambda i: (i // per_batch, 0, 0)),
                  pl.BlockSpec((1, half), const),
                  pl.BlockSpec((D_MODEL, D_MODEL), const),
                  pl.BlockSpec((1, D_MODEL), const)],
        out_specs=[pl.BlockSpec((tm, D_MODEL), row), pl.BlockSpec((tm, D_MODEL), row)],
        out_shape=[jax.ShapeDtypeStruct((t, D_MODEL), F32), jax.ShapeDtypeStruct((t, D_MODEL), F32)],
        compiler_params=_params("arbitrary"),
        name="mix",
    )(x, att, o_f, o_b, zr, mod, w["ret_gn_w"], w["w_out"], w["norm2_w"])


def _top16(s_ref, vals_ref, pay_ref, payload):
    n, cols = s_ref.shape
    rows = lax.broadcasted_iota(jnp.int32, (n, cols), 0).astype(F32)

    def body(r, carry):
        s = s_ref[...]
        m = jnp.max(s, axis=0, keepdims=True)
        pos = jnp.min(jnp.where(s == m, rows, float(n)), axis=0, keepdims=True)
        hit = rows == pos
        vals_ref[pl.ds(r, 1), :] = m
        if payload is None:
            pay_ref[pl.ds(r, 1), :] = pos
        else:
            pay_ref[pl.ds(r, 1), :] = jnp.sum(jnp.where(hit, payload, 0.0), axis=0, keepdims=True)
        s_ref[...] = jnp.where(hit, -jnp.inf, s)
        return carry

    lax.fori_loop(0, PEER_TOPK, body, 0)


def _score_kernel(h2_ref, wpq_ref, k1_ref, k2_ref, g_ref, e_ref,
                  s_scr, c_scr, v1_scr, i1_scr, v2_scr, i2_scr, vt_scr, it_scr):
    q = _mm(h2_ref[...], wpq_ref[...])
    s_scr[...] = _mm_nt(k1_ref[...], q[:, :PEER_HALF])
    _top16(s_scr, v1_scr, i1_scr, None)
    s_scr[...] = _mm_nt(k2_ref[...], q[:, PEER_HALF:])
    _top16(s_scr, v2_scr, i2_scr, None)
    v2 = v2_scr[...]
    i2 = i2_scr[...]
    cidx = []
    for a in range(PEER_TOPK):
        c_scr[a * PEER_TOPK:(a + 1) * PEER_TOPK, :] = v1_scr[a:a + 1, :] + v2
        cidx.append(i1_scr[a:a + 1, :] * float(N_KEYS) + i2)
    _top16(c_scr, vt_scr, it_scr, jnp.concatenate(cidx, axis=0))
    top = vt_scr[...]
    p = jnp.exp(top - jnp.max(top, axis=0, keepdims=True))
    g_ref[...] = p / jnp.sum(p, axis=0, keepdims=True)
    e_ref[...] = it_scr[...].astype(jnp.int32)


def _peer_score(h2, w):
    t = h2.shape[0]
    tb = SCORE_TILE
    k = PEER_TOPK
    const = lambda i, h: (0, 0)
    sel = lambda i, h: (h, i)
    return pl.pallas_call(
        _score_kernel,
        grid=(t // tb, PEER_HEADS),
        in_specs=[pl.BlockSpec((tb, D_MODEL), lambda i, h: (i, 0)),
                  pl.BlockSpec((D_MODEL, PEER_QDIM), lambda i, h: (0, h)),
                  pl.BlockSpec((N_KEYS, PEER_HALF), const),
                  pl.BlockSpec((N_KEYS, PEER_HALF), const)],
        out_specs=[pl.BlockSpec((k, tb), sel), pl.BlockSpec((k, tb), sel)],
        out_shape=[jax.ShapeDtypeStruct((PEER_HEADS * k, t), F32),
                   jax.ShapeDtypeStruct((PEER_HEADS * k, t), jnp.int32)],
        scratch_shapes=[pltpu.VMEM((N_KEYS, tb), F32), pltpu.VMEM((k * k, tb), F32)]
                       + [pltpu.VMEM((k, tb), F32)] * 6,
        compiler_params=_params("arbitrary", "arbitrary"),
        name="peer_score",
    )(h2, w["w_pq"], w["sub_keys1"], w["sub_keys2"])


def _gather_kernel(e_hbm, tab_hbm, h2_ref, gt_ref, x1_ref, mod_ref, y_ref,
                   idx_smem, rows, sem_rows, sem_idx, *, ntiles):
    i = pl.program_id(0)
    tb = GATHER_TILE
    ns = GATHER_SLOTS
    ahead = GATHER_AHEAD
    ne = EXPERTS_PER_TOKEN
    cur = i % 2
    nxt = 1 - cur
    more = i + 1 < ntiles

    def idx_copy(tile, slot):
        return pltpu.make_async_copy(e_hbm.at[pl.ds(tile * tb, tb), :], idx_smem.at[slot], sem_idx.at[slot])

    def issue(islot, tok, rslot):
        def group(gi, carry):
            for u in range(8):
                kk = gi * 8 + u
                ex = idx_smem[islot, tok, kk]
                pltpu.make_async_copy(tab_hbm.at[pl.ds(ex, 1), :], rows.at[rslot, pl.ds(kk, 1), :],
                                      sem_rows.at[rslot]).start()
            return carry
        lax.fori_loop(0, ne // 8, group, 0)

    def wait_rows(rslot):
        pltpu.make_async_copy(tab_hbm.at[pl.ds(0, ne), :], rows.at[rslot], sem_rows.at[rslot]).wait()

    @pl.when(i == 0)
    def _():
        idx_copy(0, 0).start()
        idx_copy(0, 0).wait()
        for tok in range(ahead):
            issue(0, tok, tok)

    @pl.when(more)
    def _():
        idx_copy(i + 1, nxt).start()

    g2 = mod_ref[0][5:6]
    lane = lax.broadcasted_iota(jnp.int32, (ne, tb), 1)

    def compute(j):
        rs = j % ns
        wait_rows(rs)
        x_row = h2_ref[pl.ds(j, 1), :]
        pre = jnp.sum(rows[rs, :, pl.ds(0, D_MODEL)] * x_row, axis=-1, keepdims=True)
        act = 0.5 * pre * (1.0 + lax.erf(pre * (2.0 ** -0.5)))
        gate = jnp.sum(jnp.where(lane == j, gt_ref[...], 0.0), axis=-1, keepdims=True)
        out = jnp.sum(rows[rs, :, pl.ds(D_MODEL, D_MODEL)] * (gate * act), axis=0, keepdims=True)
        y_ref[pl.ds(j, 1), :] = x1_ref[pl.ds(j, 1), :] + g2 * out

    def body_cur(j, carry):
        issue(cur, j + ahead, (j + ahead) % ns)
        compute(j)
        return carry

    lax.fori_loop(0, tb - ahead, body_cur, 0)

    @pl.when(more)
    def _():
        idx_copy(i + 1, nxt).wait()

    def body_next(j, carry):
        @pl.when(more)
        def _():
            issue(nxt, j + ahead - tb, (j + ahead) % ns)
        compute(j)
        return carry

    lax.fori_loop(tb - ahead, tb, body_next, 0)


def _peer_gather(e, tab, h2, gt, x1, mod, seq):
    t = h2.shape[0]
    tb = GATHER_TILE
    ntiles = t // tb
    per_batch = seq // tb
    row = lambda i: (i, 0)
    return pl.pallas_call(
        functools.partial(_gather_kernel, ntiles=ntiles),
        grid=(ntiles,),
        in_specs=[pl.BlockSpec(memory_space=pl.ANY),
                  pl.BlockSpec(memory_space=pl.ANY),
                  pl.BlockSpec((tb, D_MODEL), row),
                  pl.BlockSpec((EXPERTS_PER_TOKEN, tb), lambda i: (0, i)),
                  pl.BlockSpec((tb, D_MODEL), row),
                  pl.BlockSpec((1, 6, D_MODEL), lambda i: (i // per_batch, 0, 0))],
        out_specs=pl.BlockSpec((tb, D_MODEL), row),
        out_shape=jax.ShapeDtypeStruct((t, D_MODEL), F32),
        scratch_shapes=[pltpu.SMEM((2, tb, EXPERTS_PER_TOKEN), jnp.int32),
                        pltpu.VMEM((GATHER_SLOTS, EXPERTS_PER_TOKEN, 2 * D_MODEL), F32),
                        pltpu.SemaphoreType.DMA((GATHER_SLOTS,)),
                        pltpu.SemaphoreType.DMA((2,))],
        compiler_params=pltpu.CompilerParams(dimension_semantics=("arbitrary",),
                                             vmem_limit_bytes=VMEM_LIMIT,
                                             disable_bounds_checks=True),
        name="peer_gather",
    )(e, tab, h2, gt, x1, mod)


def _rope_tables(seq):
    rows = seq // GRID_W
    r = jnp.repeat(jnp.arange(rows, dtype=F32), GRID_W)
    col = jnp.tile(jnp.arange(GRID_W, dtype=F32), rows)
    nf = ROPE_AXIS // 2
    freqs = jnp.power(ROPE_BASE, -jnp.arange(nf, dtype=F32) / nf)
    ar, ac = r[:, None] * freqs, col[:, None] * freqs
    pad = jnp.zeros((seq, LANE - MLA_ROPE), F32)
    cos = jnp.concatenate([jnp.cos(ar), jnp.cos(ar), jnp.cos(ac), jnp.cos(ac), pad], axis=-1)
    sin = jnp.concatenate([-jnp.sin(ar), jnp.sin(ar), -jnp.sin(ac), jnp.sin(ac), pad], axis=-1)
    return cos, sin


def _prep_weights(norm1_w, w_in, q_norm_w, w_uq, kv_norm_w, w_ukv, q_head_norm_w, k_head_norm_w,
                  ret_gn_w, w_out, norm2_w, w_pq, sub_keys1, sub_keys2):
    cut = Q_RANK + KV_RANK + MLA_ROPE
    w_in_p = jnp.concatenate([w_in[:, :cut], jnp.zeros((D_MODEL, LANE - MLA_ROPE), F32), w_in[:, cut:]], axis=1)
    wq = w_uq.reshape(Q_RANK, MLA_HEADS, MLA_QK)
    wq = jnp.pad(wq, ((0, 0), (0, 0), (0, HEAD_PAD - MLA_QK))).reshape(Q_RANK, MLA_HEADS * HEAD_PAD)

    def head_w(v):
        return jnp.pad(v, (0, HEAD_PAD - MLA_QK)).reshape(1, HEAD_PAD)

    return {
        "norm1_w": norm1_w.reshape(1, -1), "w_in": w_in_p.astype(BF16),
        "q_norm_w": q_norm_w.reshape(1, -1), "kv_norm_w": kv_norm_w.reshape(1, -1),
        "w_uq": wq.astype(BF16), "w_ukv": w_ukv.astype(BF16),
        "q_head_norm_w": head_w(q_head_norm_w), "k_head_norm_w": head_w(k_head_norm_w),
        "ret_gn_w": ret_gn_w.reshape(1, -1), "w_out": w_out.astype(BF16), "norm2_w": norm2_w.reshape(1, -1),
        "w_pq": w_pq.astype(BF16), "sub_keys1": sub_keys1.astype(BF16), "sub_keys2": sub_keys2.astype(BF16),
    }


def _trunk(x, mod, w, decay_logit, tab, ctx):
    batch, seq, _ = x.shape
    x2 = x.reshape(batch * seq, D_MODEL)
    use_rope = ctx is not None
    if use_rope:
        cos, sin = _rope_tables(seq)
    else:
        cos = sin = jnp.zeros((ROW_TILE, LANE), F32)
    ckv, krope, q, k, v, zr = _in_proj(x2, mod, w, cos, sin, seq, use_rope)
    if use_rope:
        ckv_c, krope_c, state_c = ctx
        past = ckv_c.shape[1]
        kr_c = jnp.pad(krope_c.reshape(batch * past, MLA_ROPE), ((0, 0), (0, LANE - MLA_ROPE)))
        ctx_kv = _kv_up(ckv_c.reshape(batch * past, KV_RANK), kr_c, w)
        s0 = state_c.reshape(batch * 2 * RET_HEADS, RET_DK, RET_DV)
    else:
        ctx_kv, s0 = None, None
    att = _attention(q, k, v, batch, seq, ctx_kv)
    o_f, o_b, s_f, s_b = _retention(zr, decay_logit, batch, seq, s0)
    x1, h2 = _mix(x2, att, o_f, o_b, zr, mod, w, seq)
    gt, et = _peer_score(h2, w)
    y = _peer_gather(et.T, tab, h2, gt, x1, mod, seq)
    states = jnp.stack([s_f.reshape(batch, RET_HEADS, RET_DK, RET_DV),
                        s_b.reshape(batch, RET_HEADS, RET_DK, RET_DV)], axis=1)
    return y.reshape(batch, seq, D_MODEL), ckv.reshape(batch, seq, KV_RANK), krope.reshape(batch, seq, MLA_ROPE), states


def kernel(x_prompt, x_sample, c, cache_ckv, cache_krope, state_ret, c_ctx, w_ada, b_ada, norm1_w, w_in,
           q_norm_w, w_uq, kv_norm_w, w_ukv, q_head_norm_w, k_head_norm_w, ret_decay_logit, ret_gn_w,
           w_out, norm2_w, w_pq, sub_keys1, sub_keys2, u_table, v_table):
    depth = w_ada.shape[0]
    nb_ctx = x_prompt.shape[0]
    nb_lat = x_sample.shape[0]
    y_prompt, y_sample = x_prompt, x_sample
    ckv_list, krope_list, ret_list = [], [], []
    for l in range(depth):
        cond_rows = -(-(nb_lat + 1) // 8) * 8
        cond = jnp.concatenate([c, c_ctx[None, :], jnp.zeros((cond_rows - nb_lat - 1, D_MODEL), F32)], axis=0)
        mod = _ada(cond, w_ada[l], b_ada[l])
        mod_lat = mod[:nb_lat].reshape(nb_lat, 6, D_MODEL)
        mod_ctx = jnp.broadcast_to(mod[nb_lat].reshape(1, 6, D_MODEL), (nb_ctx, 6, D_MODEL))
        w = _prep_weights(norm1_w[l], w_in[l], q_norm_w[l], w_uq[l], kv_norm_w[l], w_ukv[l], q_head_norm_w[l],
                          k_head_norm_w[l], ret_gn_w[l], w_out[l], norm2_w[l], w_pq[l], sub_keys1[l], sub_keys2[l])
        tab = jnp.concatenate([u_table[l], v_table[l]], axis=1)
        y_prompt, ckv_l, krope_l, ret_l = _trunk(y_prompt, mod_ctx, w, ret_decay_logit[l], tab, None)
        ckv_list.append(ckv_l)
        krope_list.append(krope_l)
        ret_list.append(ret_l)
        y_sample, _, _, _ = _trunk(y_sample, mod_lat, w, ret_decay_logit[l], tab,
                                   (cache_ckv[:, l], cache_krope[:, l], state_ret[:, l]))
    return (y_prompt, y_sample, jnp.stack(ckv_list, axis=1), jnp.stack(krope_list, axis=1),
            jnp.stack(ret_list, axis=1))
```

```python
import functools

import jax
import jax.numpy as jnp
from jax import lax
from jax.experimental import pallas as pl
from jax.experimental.pallas import tpu as pltpu

F32 = jnp.float32
BF16 = jnp.bfloat16

D_MODEL = 1024
GRID_W = 64
MLA_HEADS = 4
MLA_NOPE = 128
MLA_ROPE = 64
MLA_QK = MLA_NOPE + MLA_ROPE
MLA_V = 128
Q_RANK = 512
KV_RANK = 256
ROPE_AXIS = MLA_ROPE // 2
ROPE_BASE = 10000.0
RET_HEADS = 4
RET_DK = 128
RET_DV = 128
RET_CHUNK = 128
PEER_HEADS = 8
PEER_QDIM = 256
PEER_HALF = PEER_QDIM // 2
N_KEYS = 128
PEER_TOPK = 16
EPS = 1e-6

LANE = 128
HEAD_PAD = 2 * LANE
RET_W = 4 * RET_HEADS * RET_DK
IN_PAD = Q_RANK + KV_RANK + LANE + RET_W
VMEM_LIMIT = 48 * 1024 * 1024

ROW_TILE = 256
Q_TILE = 256
SCORE_TILE = 256
GATHER_TILE = 128
GATHER_SLOTS = 4
GATHER_AHEAD = GATHER_SLOTS - 1
EXPERTS_PER_TOKEN = PEER_HEADS * PEER_TOPK
SLAB_ROWS = D_MODEL // LANE


def _params(*sem):
    return pltpu.CompilerParams(dimension_semantics=sem, vmem_limit_bytes=VMEM_LIMIT)


def _rms(x, w):
    return x * lax.rsqrt(jnp.mean(x * x, axis=-1, keepdims=True) + EPS) * w


def _mm(a, b):
    return jnp.dot(a.astype(BF16), b.astype(BF16), preferred_element_type=F32)


def _mm_nt(a, b):
    return lax.dot_general(a.astype(BF16), b.astype(BF16), (((1,), (1,)), ((), ())),
                           preferred_element_type=F32)


def _mm_tn(a, b):
    return lax.dot_general(a.astype(BF16), b.astype(BF16), (((0,), (0,)), ((), ())),
                           preferred_element_type=F32)


def _ada_kernel(c_ref, w_ref, b_ref, o_ref):
    c = c_ref[...]
    o_ref[...] = _mm(c * jax.nn.sigmoid(c), w_ref[...]) + b_ref[...]


def _ada(cond, w_ada, b_ada):
    rows, d = cond.shape
    n = w_ada.shape[1]
    tn = 1536
    return pl.pallas_call(
        _ada_kernel,
        grid=(n // tn,),
        in_specs=[pl.BlockSpec((rows, d), lambda j: (0, 0)),
                  pl.BlockSpec((d, tn), lambda j: (0, j)),
                  pl.BlockSpec((1, tn), lambda j: (0, j))],
        out_specs=pl.BlockSpec((rows, tn), lambda j: (0, j)),
        out_shape=jax.ShapeDtypeStruct((rows, n), F32),
        compiler_params=_params("arbitrary"),
        name="ada",
    )(cond, w_ada, b_ada.reshape(1, n))


def _rope_tile(x, cos, sin):
    lane = lax.broadcasted_iota(jnp.int32, x.shape, 1)
    partner = jnp.where((lane % 32) < 16, pltpu.roll(x, LANE - 16, 1), pltpu.roll(x, 16, 1))
    return x * cos + partner * sin


def _kv_heads(kv, kr, khw, cos, sin, use_rope):
    krw = kr * khw[:, LANE:]
    if use_rope:
        krw = _rope_tile(krw, cos, sin)
    ssq_r = jnp.sum(kr * kr, axis=-1, keepdims=True)
    ks, vs = [], []
    for hd in range(MLA_HEADS):
        kn = kv[:, hd * HEAD_PAD: hd * HEAD_PAD + LANE]
        r = lax.rsqrt((jnp.sum(kn * kn, axis=-1, keepdims=True) + ssq_r) / MLA_QK + EPS)
        ks += [kn * r * khw[:, :LANE], krw * r]
        vs.append(kv[:, hd * HEAD_PAD + LANE: (hd + 1) * HEAD_PAD])
    return jnp.concatenate(ks, axis=-1).astype(BF16), jnp.concatenate(vs, axis=-1).astype(BF16)


def _inproj_kernel(x_ref, mod_ref, n1w_ref, win_ref, qnw_ref, kvnw_ref, wuq_ref, wukv_ref, qhw_ref, khw_ref,
                   cos_ref, sin_ref, ckv_ref, krope_ref, q_ref, k_ref, v_ref, zr_ref, *, use_rope):
    m = mod_ref[0]
    h = _rms(x_ref[...], n1w_ref[...]) * (1.0 + m[1:2]) + m[0:1]
    z = _mm(h, win_ref[...])
    kr = z[:, Q_RANK + KV_RANK: Q_RANK + KV_RANK + LANE]
    zr_ref[...] = z[:, Q_RANK + KV_RANK + LANE:]
    ckvn = _rms(z[:, Q_RANK: Q_RANK + KV_RANK], kvnw_ref[...])
    ckv_ref[...] = ckvn
    krope_ref[...] = kr[:, :MLA_ROPE]
    cos, sin = cos_ref[...], sin_ref[...]

    q = _mm(_rms(z[:, :Q_RANK], qnw_ref[...]), wuq_ref[...])
    qhw = qhw_ref[...]
    qs = []
    for hd in range(MLA_HEADS):
        qn = q[:, hd * HEAD_PAD: hd * HEAD_PAD + LANE]
        qr = q[:, hd * HEAD_PAD + LANE: (hd + 1) * HEAD_PAD]
        ssq = jnp.sum(qn * qn, axis=-1, keepdims=True) + jnp.sum(qr * qr, axis=-1, keepdims=True)
        r = lax.rsqrt(ssq / MLA_QK + EPS)
        qrw = qr * qhw[:, LANE:]
        if use_rope:
            qrw = _rope_tile(qrw, cos, sin)
        qs += [qn * r * qhw[:, :LANE], qrw * r]
    q_ref[...] = jnp.concatenate(qs, axis=-1).astype(BF16)

    k, v = _kv_heads(_mm(ckvn, wukv_ref[...]), kr, khw_ref[...], cos, sin, use_rope)
    k_ref[...] = k
    v_ref[...] = v


def _in_proj(x, mod, w, cos, sin, seq, use_rope):
    t = x.shape[0]
    tm = ROW_TILE
    per_batch = seq // tm
    const = lambda i: (0, 0)
    row = lambda i: (i, 0)
    if use_rope:
        pos = lambda i: (i % per_batch, 0)
    else:
        pos = const
    hp = MLA_HEADS * HEAD_PAD
    return pl.pallas_call(
        functools.partial(_inproj_kernel, use_rope=use_rope),
        grid=(t // tm,),
        in_specs=[pl.BlockSpec((tm, D_MODEL), row),
                  pl.BlockSpec((1, 6, D_MODEL), lambda i: (i // per_batch, 0, 0)),
                  pl.BlockSpec((1, D_MODEL), const),
                  pl.BlockSpec((D_MODEL, IN_PAD), const),
                  pl.BlockSpec((1, Q_RANK), const),
                  pl.BlockSpec((1, KV_RANK), const),
                  pl.BlockSpec((Q_RANK, hp), const),
                  pl.BlockSpec((KV_RANK, hp), const),
                  pl.BlockSpec((1, HEAD_PAD), const),
                  pl.BlockSpec((1, HEAD_PAD), const),
                  pl.BlockSpec((tm, LANE), pos),
                  pl.BlockSpec((tm, LANE), pos)],
        out_specs=[pl.BlockSpec((tm, KV_RANK), row),
                   pl.BlockSpec((tm, MLA_ROPE), row),
                   pl.BlockSpec((tm, hp), row),
                   pl.BlockSpec((tm, hp), row),
                   pl.BlockSpec((tm, MLA_HEADS * MLA_V), row),
                   pl.BlockSpec((tm, RET_W), row)],
        out_shape=[jax.ShapeDtypeStruct((t, KV_RANK), F32),
                   jax.ShapeDtypeStruct((t, MLA_ROPE), F32),
                   jax.ShapeDtypeStruct((t, hp), BF16),
                   jax.ShapeDtypeStruct((t, hp), BF16),
                   jax.ShapeDtypeStruct((t, MLA_HEADS * MLA_V), BF16),
                   jax.ShapeDtypeStruct((t, RET_W), F32)],
        compiler_params=_params("arbitrary"),
        name="in_proj",
    )(x, mod, w["norm1_w"], w["w_in"], w["q_norm_w"], w["kv_norm_w"], w["w_uq"], w["w_ukv"],
      w["q_head_norm_w"], w["k_head_norm_w"], cos, sin)


def _kvup_kernel(ckv_ref, kr_ref, wukv_ref, khw_ref, k_ref, v_ref):
    k, v = _kv_heads(_mm(ckv_ref[...], wukv_ref[...]), kr_ref[...], khw_ref[...], None, None, False)
    k_ref[...] = k
    v_ref[...] = v


def _kv_up(ckv, kr, w):
    t = ckv.shape[0]
    tm = ROW_TILE
    hp = MLA_HEADS * HEAD_PAD
    const = lambda i: (0, 0)
    row = lambda i: (i, 0)
    return pl.pallas_call(
        _kvup_kernel,
        grid=(t // tm,),
        in_specs=[pl.BlockSpec((tm, KV_RANK), row),
                  pl.BlockSpec((tm, LANE), row),
                  pl.BlockSpec((KV_RANK, hp), const),
                  pl.BlockSpec((1, HEAD_PAD), const)],
        out_specs=[pl.BlockSpec((tm, hp), row),
                   pl.BlockSpec((tm, MLA_HEADS * MLA_V), row)],
        out_shape=[jax.ShapeDtypeStruct((t, hp), BF16),
                   jax.ShapeDtypeStruct((t, MLA_HEADS * MLA_V), BF16)],
        compiler_params=_params("arbitrary"),
        name="kv_up",
    )(ckv, kr, w["w_ukv"], w["k_head_norm_w"])


def _attn_kernel(*refs, has_ctx):
    if has_ctx:
        q_ref, k_ref, v_ref, kc_ref, vc_ref, o_ref = refs
    else:
        q_ref, k_ref, v_ref, o_ref = refs
    scale = MLA_QK ** -0.5
    q = q_ref[...]
    s = _mm_nt(q, k_ref[...]) * scale
    m = jnp.max(s, axis=-1, keepdims=True)
    if has_ctx:
        sc = _mm_nt(q, kc_ref[...]) * scale
        m = jnp.maximum(m, jnp.max(sc, axis=-1, keepdims=True))
    p = jnp.exp(s - m)
    den = jnp.sum(p, axis=-1, keepdims=True)
    o = _mm(p, v_ref[...])
    if has_ctx:
        pc = jnp.exp(sc - m)
        den = den + jnp.sum(pc, axis=-1, keepdims=True)
        o = o + _mm(pc, vc_ref[...])
    o_ref[...] = (o / den).astype(BF16)


def _attention(q, k, v, batch, seq, ctx_kv):
    tq = Q_TILE
    nq = seq // tq
    has_ctx = ctx_kv is not None
    qmap = lambda b, h, i: (b * nq + i, h)
    kvmap = lambda b, h, i: (b, h)
    in_specs = [pl.BlockSpec((tq, HEAD_PAD), qmap),
                pl.BlockSpec((seq, HEAD_PAD), kvmap),
                pl.BlockSpec((seq, MLA_V), kvmap)]
    args = [q, k, v]
    if has_ctx:
        kc, vc = ctx_kv
        past = kc.shape[0] // batch
        in_specs += [pl.BlockSpec((past, HEAD_PAD), kvmap), pl.BlockSpec((past, MLA_V), kvmap)]
        args += [kc, vc]
    return pl.pallas_call(
        functools.partial(_attn_kernel, has_ctx=has_ctx),
        grid=(batch, MLA_HEADS, nq),
        in_specs=in_specs,
        out_specs=pl.BlockSpec((tq, MLA_V), qmap),
        out_shape=jax.ShapeDtypeStruct((batch * seq, MLA_HEADS * MLA_V), BF16),
        compiler_params=_params("arbitrary", "arbitrary", "arbitrary"),
        name="attention",
    )(*args)


def _ret_kernel(*refs, has_s0):
    if has_s0:
        (lg_ref, qf_ref, kf_ref, vf_ref, qb_ref, kb_ref, vb_ref, s0f_ref, s0b_ref,
         of_ref, ob_ref, sf_out, sb_out, sf_scr, sb_scr) = refs
    else:
        (lg_ref, qf_ref, kf_ref, vf_ref, qb_ref, kb_ref, vb_ref,
         of_ref, ob_ref, sf_out, sb_out, sf_scr, sb_scr) = refs
    hd = pl.program_id(1)
    c = pl.program_id(2)
    cs = RET_CHUNK

    @pl.when(c == 0)
    def _():
        if has_s0:
            sf_scr[...] = s0f_ref[0]
            sb_scr[...] = s0b_ref[0]
        else:
            sf_scr[...] = jnp.zeros_like(sf_scr)
            sb_scr[...] = jnp.zeros_like(sb_scr)

    lgf = jax.nn.log_sigmoid(jnp.full((1, LANE), lg_ref[0, hd], F32))
    lgb = jax.nn.log_sigmoid(jnp.full((1, LANE), lg_ref[1, hd], F32))
    ii = lax.broadcasted_iota(jnp.int32, (cs, cs), 0).astype(F32)
    jj = lax.broadcasted_iota(jnp.int32, (cs, cs), 1).astype(F32)
    rel = ii - jj
    kscale = RET_DK ** -0.5

    def chunk(q, k, v, s, intra, qdec, kdec, cdec):
        a = _mm_nt(q, k) * intra
        o = _mm(a, v) + _mm(q, s) * qdec
        return o, s * cdec + _mm_tn(k * kdec, v)

    intra_f = jnp.where(rel >= 0, jnp.exp(jnp.maximum(rel, 0.0) * lgf), 0.0)
    o_f, s_f = chunk(qf_ref[...], kf_ref[...] * kscale, vf_ref[...], sf_scr[...], intra_f,
                     jnp.exp((ii + 1.0) * lgf), jnp.exp((cs - 1.0 - ii) * lgf), jnp.exp(cs * lgf))
    of_ref[...] = o_f
    sf_scr[...] = s_f
    sf_out[0] = s_f

    intra_b = jnp.where(rel <= 0, jnp.exp(jnp.maximum(-rel, 0.0) * lgb), 0.0)
    o_b, s_b = chunk(qb_ref[...], kb_ref[...] * kscale, vb_ref[...], sb_scr[...], intra_b,
                     jnp.exp((cs - ii) * lgb), jnp.exp(ii * lgb), jnp.exp(cs * lgb))
    ob_ref[...] = o_b
    sb_scr[...] = s_b
    sb_out[0] = s_b


def _retention(zr, decay_logit, batch, seq, s0):
    cs = RET_CHUNK
    nc = seq // cs
    nh = RET_HEADS
    has_s0 = s0 is not None

    def fwd(col):
        return pl.BlockSpec((cs, RET_DK), lambda b, h, c: (b * nc + c, col * nh + h))

    def bwd(col):
        return pl.BlockSpec((cs, RET_DK), lambda b, h, c: (b * nc + nc - 1 - c, col * nh + h))

    def state(d):
        return pl.BlockSpec((1, RET_DK, RET_DV), lambda b, h, c: (b * 2 * nh + d * nh + h, 0, 0))

    in_specs = [pl.BlockSpec(memory_space=pltpu.SMEM), fwd(0), fwd(1), fwd(2), bwd(0), bwd(1), bwd(2)]
    args = [decay_logit, zr, zr, zr, zr, zr, zr]
    if has_s0:
        in_specs += [state(0), state(1)]
        args += [s0, s0]
    t = batch * seq
    o_f, o_b, s_f, s_b = pl.pallas_call(
        functools.partial(_ret_kernel, has_s0=has_s0),
        grid=(batch, nh, nc),
        in_specs=in_specs,
        out_specs=[pl.BlockSpec((cs, RET_DV), lambda b, h, c: (b * nc + c, h)),
                   pl.BlockSpec((cs, RET_DV), lambda b, h, c: (b * nc + nc - 1 - c, h)),
                   pl.BlockSpec((1, RET_DK, RET_DV), lambda b, h, c: (b * nh + h, 0, 0)),
                   pl.BlockSpec((1, RET_DK, RET_DV), lambda b, h, c: (b * nh + h, 0, 0))],
        out_shape=[jax.ShapeDtypeStruct((t, nh * RET_DV), F32),
                   jax.ShapeDtypeStruct((t, nh * RET_DV), F32),
                   jax.ShapeDtypeStruct((batch * nh, RET_DK, RET_DV), F32),
                   jax.ShapeDtypeStruct((batch * nh, RET_DK, RET_DV), F32)],
        scratch_shapes=[pltpu.VMEM((RET_DK, RET_DV), F32), pltpu.VMEM((RET_DK, RET_DV), F32)],
        compiler_params=_params("arbitrary", "arbitrary", "arbitrary"),
        name="retention",
    )(*args)
    return o_f, o_b, s_f, s_b


def _mix_kernel(x_ref, att_ref, of_ref, ob_ref, gr_ref, mod_ref, gnw_ref, wout_ref, n2w_ref, x1_ref, h2_ref):
    o = of_ref[...] + ob_ref[...]
    parts = []
    for hd in range(RET_HEADS):
        oh = o[:, hd * RET_DV:(hd + 1) * RET_DV]
        d = oh - jnp.mean(oh, axis=-1, keepdims=True)
        parts.append(d * lax.rsqrt(jnp.mean(d * d, axis=-1, keepdims=True) + EPS))
    g = gr_ref[...]
    ret = (g * jax.nn.sigmoid(g)) * (jnp.concatenate(parts, axis=-1) * gnw_ref[...])
    na = MLA_HEADS * MLA_V
    mixed = _mm(att_ref[...], wout_ref[:na, :]) + _mm(ret, wout_ref[na:, :])
    m = mod_ref[0]
    x1 = x_ref[...] + m[2:3] * mixed
    x1_ref[...] = x1
    h2_ref[...] = _rms(x1, n2w_ref[...]) * (1.0 + m[4:5]) + m[3:4]


def _mix(x, att, o_f, o_b, zr, mod, w, seq):
    t = x.shape[0]
    tm = ROW_TILE
    per_batch = seq // tm
    const = lambda i: (0, 0)
    row = lambda i: (i, 0)
    half = RET_HEADS * RET_DV
    return pl.pallas_call(
        _mix_kernel,
        grid=(t // tm,),
        in_specs=[pl.BlockSpec((tm, D_MODEL), row),
                  pl.BlockSpec((tm, half), row),
                  pl.BlockSpec((tm, half), row),
                  pl.BlockSpec((tm, half), row),
                  pl.BlockSpec((tm, half), lambda i: (i, 3)),
                  pl.BlockSpec((1, 6, D_MODEL), lambda i: (i // per_batch, 0, 0)),
                  pl.BlockSpec((1, half), const),
                  pl.BlockSpec((D_MODEL, D_MODEL), const),
                  pl.BlockSpec((1, D_MODEL), const)],
        out_specs=[pl.BlockSpec((tm, D_MODEL), row), pl.BlockSpec((tm, D_MODEL), row)],
        out_shape=[jax.ShapeDtypeStruct((t, D_MODEL), F32), jax.ShapeDtypeStruct((t, D_MODEL), F32)],
        compiler_params=_params("arbitrary"),
        name="mix",
    )(x, att, o_f, o_b, zr, mod, w["ret_gn_w"], w["w_out"], w["norm2_w"])


def _top16(s_ref, vals_ref, pay_ref, payload):
    n, cols = s_ref.shape
    rows = lax.broadcasted_iota(jnp.int32, (n, cols), 0).astype(F32)

    def body(r, carry):
        s = s_ref[...]
        m = jnp.max(s, axis=0, keepdims=True)
        pos = jnp.min(jnp.where(s == m, rows, float(n)), axis=0, keepdims=True)
        hit = rows == pos
        vals_ref[pl.ds(r, 1), :] = m
        if payload is None:
            pay_ref[pl.ds(r, 1), :] = pos
        else:
            pay_ref[pl.ds(r, 1), :] = jnp.sum(jnp.where(hit, payload, 0.0), axis=0, keepdims=True)
        s_ref[...] = jnp.where(hit, -jnp.inf, s)
        return carry

    lax.fori_loop(0, PEER_TOPK, body, 0)


PAIR_COUNTS = tuple(PEER_TOPK // (a + 1) for a in range(PEER_TOPK))
NUM_PAIRS = sum(PAIR_COUNTS)
PAIR_ROWS = -(-NUM_PAIRS // 8) * 8


def _score_kernel(h2_ref, wpq_ref, k1_ref, k2_ref, g_ref, e_ref,
                  s_scr, c_scr, p_scr, v1_scr, i1_scr, v2_scr, i2_scr, vt_scr, it_scr):
    q = _mm(h2_ref[...], wpq_ref[...])
    s_scr[...] = _mm_nt(k1_ref[...], q[:, :PEER_HALF])
    _top16(s_scr, v1_scr, i1_scr, None)
    s_scr[...] = _mm_nt(k2_ref[...], q[:, PEER_HALF:])
    _top16(s_scr, v2_scr, i2_scr, None)
    c_scr[...] = jnp.full(c_scr.shape, -jnp.inf, F32)
    p_scr[...] = jnp.zeros(p_scr.shape, F32)
    off = 0
    for a, nb in enumerate(PAIR_COUNTS):
        c_scr[off:off + nb, :] = v1_scr[a:a + 1, :] + v2_scr[0:nb, :]
        p_scr[off:off + nb, :] = i1_scr[a:a + 1, :] * float(N_KEYS) + i2_scr[0:nb, :]
        off += nb
    _top16(c_scr, vt_scr, it_scr, p_scr[...])
    top = vt_scr[...]
    p = jnp.exp(top - jnp.max(top, axis=0, keepdims=True))
    g_ref[...] = p / jnp.sum(p, axis=0, keepdims=True)
    e_ref[...] = it_scr[...].astype(jnp.int32)


def _peer_score(h2, w):
    t = h2.shape[0]
    tb = SCORE_TILE
    k = PEER_TOPK
    const = lambda i, h: (0, 0)
    sel = lambda i, h: (h, i)
    return pl.pallas_call(
        _score_kernel,
        grid=(t // tb, PEER_HEADS),
        in_specs=[pl.BlockSpec((tb, D_MODEL), lambda i, h: (i, 0)),
                  pl.BlockSpec((D_MODEL, PEER_QDIM), lambda i, h: (0, h)),
                  pl.BlockSpec((N_KEYS, PEER_HALF), const),
                  pl.BlockSpec((N_KEYS, PEER_HALF), const)],
        out_specs=[pl.BlockSpec((k, tb), sel), pl.BlockSpec((k, tb), sel)],
        out_shape=[jax.ShapeDtypeStruct((PEER_HEADS * k, t), F32),
                   jax.ShapeDtypeStruct((PEER_HEADS * k, t), jnp.int32)],
        scratch_shapes=[pltpu.VMEM((N_KEYS, tb), F32), pltpu.VMEM((PAIR_ROWS, tb), F32),
                        pltpu.VMEM((PAIR_ROWS, tb), F32)] + [pltpu.VMEM((k, tb), F32)] * 6,
        compiler_params=_params("arbitrary", "arbitrary"),
        name="peer_score",
    )(h2, w["w_pq"], w["sub_keys1"], w["sub_keys2"])


def _gather_kernel(e_hbm, tab_hbm, h2_ref, gt_ref, x1_ref, mod_ref, y_ref, idx_smem, *scratch, ntiles):
    bufs = scratch[:GATHER_SLOTS]
    sem_rows, sem_idx = scratch[GATHER_SLOTS:]
    i = pl.program_id(0)
    tb = GATHER_TILE
    ns = GATHER_SLOTS
    ahead = GATHER_AHEAD
    ne = EXPERTS_PER_TOKEN
    slab = SLAB_ROWS
    ngroups = tb // ns
    cur = i % 2
    nxt = 1 - cur
    more = i + 1 < ntiles

    def idx_copy(tile, slot):
        return pltpu.make_async_copy(e_hbm.at[pl.ds(tile * tb, tb), :], idx_smem.at[slot], sem_idx.at[slot])

    def issue(islot, tok, rslot):
        for kk in range(ne):
            ex = idx_smem[islot, tok, kk]
            pltpu.make_async_copy(tab_hbm.at[ex], bufs[rslot].at[:, pl.ds(kk * slab, slab), :],
                                  sem_rows.at[rslot]).start()

    def wait_rows(rslot):
        pltpu.make_async_copy(bufs[rslot], bufs[rslot], sem_rows.at[rslot]).wait()

    @pl.when(i == 0)
    def _():
        idx_copy(0, 0).start()
        idx_copy(0, 0).wait()
        for tok in range(ahead):
            issue(0, tok, tok)

    @pl.when(more)
    def _():
        idx_copy(i + 1, nxt).start()

    g2 = mod_ref[0][5:6]
    lane = lax.broadcasted_iota(jnp.int32, (ne, tb), 1)

    def compute(j, rslot):
        rows = bufs[rslot]
        x_row = h2_ref[pl.ds(j, 1), :]
        acc = None
        for c in range(slab):
            u_c = rows[0, pl.ds(c, ne, stride=slab), :]
            term = u_c * x_row[:, c * LANE:(c + 1) * LANE]
            acc = term if acc is None else acc + term
        pre = jnp.sum(acc, axis=-1, keepdims=True)
        act = 0.5 * pre * (1.0 + lax.erf(pre * (2.0 ** -0.5)))
        gate = jnp.sum(jnp.where(lane == j, gt_ref[...], 0.0), axis=-1, keepdims=True)
        wgt = gate * act
        outs = []
        for c in range(slab):
            v_c = rows[1, pl.ds(c, ne, stride=slab), :]
            outs.append(jnp.sum(v_c * wgt, axis=0, keepdims=True))
        out = jnp.concatenate(outs, axis=-1)
        y_ref[pl.ds(j, 1), :] = x1_ref[pl.ds(j, 1), :] + g2 * out

    def group(g, carry):
        for u in range(ns):
            j = g * ns + u
            wait_rows(u)
            issue(cur, j + ahead, (u + ahead) % ns)
            compute(j, u)
        return carry

    lax.fori_loop(0, ngroups - 1, group, 0)

    @pl.when(more)
    def _():
        idx_copy(i + 1, nxt).wait()

    for u in range(ns):
        j = (ngroups - 1) * ns + u
        wait_rows(u)
        if j + ahead < tb:
            issue(cur, j + ahead, (u + ahead) % ns)
        else:
            @pl.when(more)
            def _():
                issue(nxt, j + ahead - tb, (u + ahead) % ns)
        compute(j, u)


def _peer_gather(e, tab, h2, gt, x1, mod, seq):
    t = h2.shape[0]
    tb = GATHER_TILE
    ntiles = t // tb
    per_batch = seq // tb
    row = lambda i: (i, 0)
    return pl.pallas_call(
        functools.partial(_gather_kernel, ntiles=ntiles),
        grid=(ntiles,),
        in_specs=[pl.BlockSpec(memory_space=pl.ANY),
                  pl.BlockSpec(memory_space=pl.ANY),
                  pl.BlockSpec((tb, D_MODEL), row),
                  pl.BlockSpec((EXPERTS_PER_TOKEN, tb), lambda i: (0, i)),
                  pl.BlockSpec((tb, D_MODEL), row),
                  pl.BlockSpec((1, 6, D_MODEL), lambda i: (i // per_batch, 0, 0))],
        out_specs=pl.BlockSpec((tb, D_MODEL), row),
        out_shape=jax.ShapeDtypeStruct((t, D_MODEL), F32),
        scratch_shapes=[pltpu.SMEM((2, tb, EXPERTS_PER_TOKEN), jnp.int32)]
                       + [pltpu.VMEM((2, EXPERTS_PER_TOKEN * SLAB_ROWS, LANE), F32)] * GATHER_SLOTS
                       + [pltpu.SemaphoreType.DMA((GATHER_SLOTS,)), pltpu.SemaphoreType.DMA((2,))],
        compiler_params=pltpu.CompilerParams(dimension_semantics=("arbitrary",),
                                             vmem_limit_bytes=VMEM_LIMIT,
                                             disable_bounds_checks=True),
        name="peer_gather",
    )(e, tab, h2, gt, x1, mod)


def _expert_slabs(u_table, v_table):
    n = u_table.shape[0]
    u = u_table.reshape(n, 1, SLAB_ROWS, LANE)
    v = v_table.reshape(n, 1, SLAB_ROWS, LANE)
    return jnp.concatenate([u, v], axis=1)


def _rope_tables(seq):
    rows = seq // GRID_W
    r = jnp.repeat(jnp.arange(rows, dtype=F32), GRID_W)
    col = jnp.tile(jnp.arange(GRID_W, dtype=F32), rows)
    nf = ROPE_AXIS // 2
    freqs = jnp.power(ROPE_BASE, -jnp.arange(nf, dtype=F32) / nf)
    ar, ac = r[:, None] * freqs, col[:, None] * freqs
    pad = jnp.zeros((seq, LANE - MLA_ROPE), F32)
    cos = jnp.concatenate([jnp.cos(ar), jnp.cos(ar), jnp.cos(ac), jnp.cos(ac), pad], axis=-1)
    sin = jnp.concatenate([-jnp.sin(ar), jnp.sin(ar), -jnp.sin(ac), jnp.sin(ac), pad], axis=-1)
    return cos, sin


def _prep_weights(norm1_w, w_in, q_norm_w, w_uq, kv_norm_w, w_ukv, q_head_norm_w, k_head_norm_w,
                  ret_gn_w, w_out, norm2_w, w_pq, sub_keys1, sub_keys2):
    cut = Q_RANK + KV_RANK + MLA_ROPE
    w_in_p = jnp.concatenate([w_in[:, :cut], jnp.zeros((D_MODEL, LANE - MLA_ROPE), F32), w_in[:, cut:]], axis=1)
    wq = w_uq.reshape(Q_RANK, MLA_HEADS, MLA_QK)
    wq = jnp.pad(wq, ((0, 0), (0, 0), (0, HEAD_PAD - MLA_QK))).reshape(Q_RANK, MLA_HEADS * HEAD_PAD)

    def head_w(v):
        return jnp.pad(v, (0, HEAD_PAD - MLA_QK)).reshape(1, HEAD_PAD)

    return {
        "norm1_w": norm1_w.reshape(1, -1), "w_in": w_in_p.astype(BF16),
        "q_norm_w": q_norm_w.reshape(1, -1), "kv_norm_w": kv_norm_w.reshape(1, -1),
        "w_uq": wq.astype(BF16), "w_ukv": w_ukv.astype(BF16),
        "q_head_norm_w": head_w(q_head_norm_w), "k_head_norm_w": head_w(k_head_norm_w),
        "ret_gn_w": ret_gn_w.reshape(1, -1), "w_out": w_out.astype(BF16), "norm2_w": norm2_w.reshape(1, -1),
        "w_pq": w_pq.astype(BF16), "sub_keys1": sub_keys1.astype(BF16), "sub_keys2": sub_keys2.astype(BF16),
    }


def _trunk(x, mod, w, decay_logit, tab, ctx):
    batch, seq, _ = x.shape
    x2 = x.reshape(batch * seq, D_MODEL)
    use_rope = ctx is not None
    if use_rope:
        cos, sin = _rope_tables(seq)
    else:
        cos = sin = jnp.zeros((ROW_TILE, LANE), F32)
    ckv, krope, q, k, v, zr = _in_proj(x2, mod, w, cos, sin, seq, use_rope)
    if use_rope:
        ckv_c, krope_c, state_c = ctx
        past = ckv_c.shape[1]
        kr_c = jnp.pad(krope_c.reshape(batch * past, MLA_ROPE), ((0, 0), (0, LANE - MLA_ROPE)))
        ctx_kv = _kv_up(ckv_c.reshape(batch * past, KV_RANK), kr_c, w)
        s0 = state_c.reshape(batch * 2 * RET_HEADS, RET_DK, RET_DV)
    else:
        ctx_kv, s0 = None, None
    att = _attention(q, k, v, batch, seq, ctx_kv)
    o_f, o_b, s_f, s_b = _retention(zr, decay_logit, batch, seq, s0)
    x1, h2 = _mix(x2, att, o_f, o_b, zr, mod, w, seq)
    gt, et = _peer_score(h2, w)
    y = _peer_gather(et.T, tab, h2, gt, x1, mod, seq)
    states = jnp.stack([s_f.reshape(batch, RET_HEADS, RET_DK, RET_DV),
                        s_b.reshape(batch, RET_HEADS, RET_DK, RET_DV)], axis=1)
    return y.reshape(batch, seq, D_MODEL), ckv.reshape(batch, seq, KV_RANK), krope.reshape(batch, seq, MLA_ROPE), states


def kernel(x_prompt, x_sample, c, cache_ckv, cache_krope, state_ret, c_ctx, w_ada, b_ada, norm1_w, w_in,
           q_norm_w, w_uq, kv_norm_w, w_ukv, q_head_norm_w, k_head_norm_w, ret_decay_logit, ret_gn_w,
           w_out, norm2_w, w_pq, sub_keys1, sub_keys2, u_table, v_table):
    depth = w_ada.shape[0]
    nb_ctx = x_prompt.shape[0]
    nb_lat = x_sample.shape[0]
    y_prompt, y_sample = x_prompt, x_sample
    ckv_list, krope_list, ret_list = [], [], []
    for l in range(depth):
        cond_rows = -(-(nb_lat + 1) // 8) * 8
        cond = jnp.concatenate([c, c_ctx[None, :], jnp.zeros((cond_rows - nb_lat - 1, D_MODEL), F32)], axis=0)
        mod = _ada(cond, w_ada[l], b_ada[l])
        mod_lat = mod[:nb_lat].reshape(nb_lat, 6, D_MODEL)
        mod_ctx = jnp.broadcast_to(mod[nb_lat].reshape(1, 6, D_MODEL), (nb_ctx, 6, D_MODEL))
        w = _prep_weights(norm1_w[l], w_in[l], q_norm_w[l], w_uq[l], kv_norm_w[l], w_ukv[l], q_head_norm_w[l],
                          k_head_norm_w[l], ret_gn_w[l], w_out[l], norm2_w[l], w_pq[l], sub_keys1[l], sub_keys2[l])
        tab = _expert_slabs(u_table[l], v_table[l])
        y_prompt, ckv_l, krope_l, ret_l = _trunk(y_prompt, mod_ctx, w, ret_decay_logit[l], tab, None)
        ckv_list.append(ckv_l)
        krope_list.append(krope_l)
        ret_list.append(ret_l)
        y_sample, _, _, _ = _trunk(y_sample, mod_lat, w, ret_decay_logit[l], tab,
                                   (cache_ckv[:, l], cache_krope[:, l], state_ret[:, l]))
    return (y_prompt, y_sample, jnp.stack(ckv_list, axis=1), jnp.stack(krope_list, axis=1),
            jnp.stack(ret_list, axis=1))
```

```python
import functools

import jax
import jax.numpy as jnp
from jax import lax
from jax.experimental import pallas as pl
from jax.experimental.pallas import tpu as pltpu

F32 = jnp.float32
BF16 = jnp.bfloat16

D_MODEL = 1024
GRID_W = 64
MLA_HEADS = 4
MLA_NOPE = 128
MLA_ROPE = 64
MLA_QK = MLA_NOPE + MLA_ROPE
MLA_V = 128
Q_RANK = 512
KV_RANK = 256
ROPE_AXIS = MLA_ROPE // 2
ROPE_BASE = 10000.0
RET_HEADS = 4
RET_DK = 128
RET_DV = 128
RET_CHUNK = 128
PEER_HEADS = 8
PEER_QDIM = 256
PEER_HALF = PEER_QDIM // 2
N_KEYS = 128
PEER_TOPK = 16
EPS = 1e-6

LANE = 128
HEAD_PAD = 2 * LANE
RET_W = 4 * RET_HEADS * RET_DK
IN_PAD = Q_RANK + KV_RANK + LANE + RET_W
VMEM_LIMIT = 48 * 1024 * 1024

ROW_TILE = 256
Q_TILE = 512
GATHER_TILE = 128
GATHER_SLOTS = 4
GATHER_AHEAD = GATHER_SLOTS - 1
EXPERTS_PER_TOKEN = PEER_HEADS * PEER_TOPK
SLAB_ROWS = D_MODEL // LANE


def _params(*sem):
    return pltpu.CompilerParams(dimension_semantics=sem, vmem_limit_bytes=VMEM_LIMIT)


def _rms(x, w):
    return x * lax.rsqrt(jnp.mean(x * x, axis=-1, keepdims=True) + EPS) * w


def _mm(a, b):
    return jnp.dot(a.astype(BF16), b.astype(BF16), preferred_element_type=F32)


def _mm_nt(a, b):
    return lax.dot_general(a.astype(BF16), b.astype(BF16), (((1,), (1,)), ((), ())),
                           preferred_element_type=F32)


def _mm_tn(a, b):
    return lax.dot_general(a.astype(BF16), b.astype(BF16), (((0,), (0,)), ((), ())),
                           preferred_element_type=F32)


def _ada_kernel(c_ref, w_ref, b_ref, o_ref):
    c = c_ref[...]
    o_ref[...] = _mm(c * jax.nn.sigmoid(c), w_ref[...]) + b_ref[...]


def _ada(cond, w_ada, b_ada):
    rows, d = cond.shape
    n = w_ada.shape[1]
    tn = 1536
    return pl.pallas_call(
        _ada_kernel,
        grid=(n // tn,),
        in_specs=[pl.BlockSpec((rows, d), lambda j: (0, 0)),
                  pl.BlockSpec((d, tn), lambda j: (0, j)),
                  pl.BlockSpec((1, tn), lambda j: (0, j))],
        out_specs=pl.BlockSpec((rows, tn), lambda j: (0, j)),
        out_shape=jax.ShapeDtypeStruct((rows, n), F32),
        compiler_params=_params("arbitrary"),
        name="ada",
    )(cond, w_ada, b_ada.reshape(1, n))


def _rope_tile(x, cos, sin):
    lane = lax.broadcasted_iota(jnp.int32, x.shape, 1)
    partner = jnp.where((lane % 32) < 16, pltpu.roll(x, LANE - 16, 1), pltpu.roll(x, 16, 1))
    return x * cos + partner * sin


def _kv_heads(kv, kr, khw, cos, sin, use_rope):
    krw = kr * khw[:, LANE:]
    if use_rope:
        krw = _rope_tile(krw, cos, sin)
    ssq_r = jnp.sum(kr * kr, axis=-1, keepdims=True)
    ks, vs = [], []
    for hd in range(MLA_HEADS):
        kn = kv[:, hd * HEAD_PAD: hd * HEAD_PAD + LANE]
        r = lax.rsqrt((jnp.sum(kn * kn, axis=-1, keepdims=True) + ssq_r) / MLA_QK + EPS)
        ks += [kn * r * khw[:, :LANE], krw * r]
        vs.append(kv[:, hd * HEAD_PAD + LANE: (hd + 1) * HEAD_PAD])
    return jnp.concatenate(ks, axis=-1).astype(BF16), jnp.concatenate(vs, axis=-1).astype(BF16)


def _inproj_kernel(x_ref, mod_ref, n1w_ref, win_ref, qnw_ref, kvnw_ref, wuq_ref, wukv_ref, qhw_ref, khw_ref,
                   cos_ref, sin_ref, ckv_ref, krope_ref, q_ref, k_ref, v_ref, zr_ref, *, use_rope):
    m = mod_ref[0]
    h = _rms(x_ref[...], n1w_ref[...]) * (1.0 + m[1:2]) + m[0:1]
    z = _mm(h, win_ref[...])
    kr = z[:, Q_RANK + KV_RANK: Q_RANK + KV_RANK + LANE]
    zr_ref[...] = z[:, Q_RANK + KV_RANK + LANE:]
    ckvn = _rms(z[:, Q_RANK: Q_RANK + KV_RANK], kvnw_ref[...])
    ckv_ref[...] = ckvn
    krope_ref[...] = kr[:, :MLA_ROPE]
    cos, sin = cos_ref[...], sin_ref[...]

    q = _mm(_rms(z[:, :Q_RANK], qnw_ref[...]), wuq_ref[...])
    qhw = qhw_ref[...]
    qs = []
    for hd in range(MLA_HEADS):
        qn = q[:, hd * HEAD_PAD: hd * HEAD_PAD + LANE]
        qr = q[:, hd * HEAD_PAD + LANE: (hd + 1) * HEAD_PAD]
        ssq = jnp.sum(qn * qn, axis=-1, keepdims=True) + jnp.sum(qr * qr, axis=-1, keepdims=True)
        r = lax.rsqrt(ssq / MLA_QK + EPS)
        qrw = qr * qhw[:, LANE:]
        if use_rope:
            qrw = _rope_tile(qrw, cos, sin)
        qs += [qn * r * qhw[:, :LANE], qrw * r]
    q_ref[...] = jnp.concatenate(qs, axis=-1).astype(BF16)

    k, v = _kv_heads(_mm(ckvn, wukv_ref[...]), kr, khw_ref[...], cos, sin, use_rope)
    k_ref[...] = k
    v_ref[...] = v


def _in_proj(x, mod, w, cos, sin, seq, use_rope):
    t = x.shape[0]
    tm = ROW_TILE
    per_batch = seq // tm
    const = lambda i: (0, 0)
    row = lambda i: (i, 0)
    if use_rope:
        pos = lambda i: (i % per_batch, 0)
    else:
        pos = const
    hp = MLA_HEADS * HEAD_PAD
    return pl.pallas_call(
        functools.partial(_inproj_kernel, use_rope=use_rope),
        grid=(t // tm,),
        in_specs=[pl.BlockSpec((tm, D_MODEL), row),
                  pl.BlockSpec((1, 6, D_MODEL), lambda i: (i // per_batch, 0, 0)),
                  pl.BlockSpec((1, D_MODEL), const),
                  pl.BlockSpec((D_MODEL, IN_PAD), const),
                  pl.BlockSpec((1, Q_RANK), const),
                  pl.BlockSpec((1, KV_RANK), const),
                  pl.BlockSpec((Q_RANK, hp), const),
                  pl.BlockSpec((KV_RANK, hp), const),
                  pl.BlockSpec((1, HEAD_PAD), const),
                  pl.BlockSpec((1, HEAD_PAD), const),
                  pl.BlockSpec((tm, LANE), pos),
                  pl.BlockSpec((tm, LANE), pos)],
        out_specs=[pl.BlockSpec((tm, KV_RANK), row),
                   pl.BlockSpec((tm, MLA_ROPE), row),
                   pl.BlockSpec((tm, hp), row),
                   pl.BlockSpec((tm, hp), row),
                   pl.BlockSpec((tm, MLA_HEADS * MLA_V), row),
                   pl.BlockSpec((tm, RET_W), row)],
        out_shape=[jax.ShapeDtypeStruct((t, KV_RANK), F32),
                   jax.ShapeDtypeStruct((t, MLA_ROPE), F32),
                   jax.ShapeDtypeStruct((t, hp), BF16),
                   jax.ShapeDtypeStruct((t, hp), BF16),
                   jax.ShapeDtypeStruct((t, MLA_HEADS * MLA_V), BF16),
                   jax.ShapeDtypeStruct((t, RET_W), F32)],
        compiler_params=_params("arbitrary"),
        name="in_proj",
    )(x, mod, w["norm1_w"], w["w_in"], w["q_norm_w"], w["kv_norm_w"], w["w_uq"], w["w_ukv"],
      w["q_head_norm_w"], w["k_head_norm_w"], cos, sin)


def _kvup_kernel(ckv_ref, kr_ref, wukv_ref, khw_ref, k_ref, v_ref):
    k, v = _kv_heads(_mm(ckv_ref[...], wukv_ref[...]), kr_ref[...], khw_ref[...], None, None, False)
    k_ref[...] = k
    v_ref[...] = v


def _kv_up(ckv, kr, w):
    t = ckv.shape[0]
    tm = ROW_TILE
    hp = MLA_HEADS * HEAD_PAD
    const = lambda i: (0, 0)
    row = lambda i: (i, 0)
    return pl.pallas_call(
        _kvup_kernel,
        grid=(t // tm,),
        in_specs=[pl.BlockSpec((tm, KV_RANK), row),
                  pl.BlockSpec((tm, LANE), row),
                  pl.BlockSpec((KV_RANK, hp), const),
                  pl.BlockSpec((1, HEAD_PAD), const)],
        out_specs=[pl.BlockSpec((tm, hp), row),
                   pl.BlockSpec((tm, MLA_HEADS * MLA_V), row)],
        out_shape=[jax.ShapeDtypeStruct((t, hp), BF16),
                   jax.ShapeDtypeStruct((t, MLA_HEADS * MLA_V), BF16)],
        compiler_params=_params("arbitrary"),
        name="kv_up",
    )(ckv, kr, w["w_ukv"], w["k_head_norm_w"])


def _attn_kernel(*refs, has_ctx):
    if has_ctx:
        q_ref, k_ref, v_ref, kc_ref, vc_ref, o_ref = refs
    else:
        q_ref, k_ref, v_ref, o_ref = refs
    scale = MLA_QK ** -0.5
    q = q_ref[...]
    s = _mm_nt(q, k_ref[...]) * scale
    m = jnp.max(s, axis=-1, keepdims=True)
    if has_ctx:
        sc = _mm_nt(q, kc_ref[...]) * scale
        m = jnp.maximum(m, jnp.max(sc, axis=-1, keepdims=True))
    p = jnp.exp(s - m)
    den = jnp.sum(p, axis=-1, keepdims=True)
    o = _mm(p, v_ref[...])
    if has_ctx:
        pc = jnp.exp(sc - m)
        den = den + jnp.sum(pc, axis=-1, keepdims=True)
        o = o + _mm(pc, vc_ref[...])
    o_ref[...] = (o / den).astype(BF16)


def _attention(q, k, v, batch, seq, ctx_kv):
    tq = min(Q_TILE, seq)
    nq = seq // tq
    has_ctx = ctx_kv is not None
    qmap = lambda b, h, i: (b * nq + i, h)
    kvmap = lambda b, h, i: (b, h)
    in_specs = [pl.BlockSpec((tq, HEAD_PAD), qmap),
                pl.BlockSpec((seq, HEAD_PAD), kvmap),
                pl.BlockSpec((seq, MLA_V), kvmap)]
    args = [q, k, v]
    if has_ctx:
        kc, vc = ctx_kv
        past = kc.shape[0] // batch
        in_specs += [pl.BlockSpec((past, HEAD_PAD), kvmap), pl.BlockSpec((past, MLA_V), kvmap)]
        args += [kc, vc]
    return pl.pallas_call(
        functools.partial(_attn_kernel, has_ctx=has_ctx),
        grid=(batch, MLA_HEADS, nq),
        in_specs=in_specs,
        out_specs=pl.BlockSpec((tq, MLA_V), qmap),
        out_shape=jax.ShapeDtypeStruct((batch * seq, MLA_HEADS * MLA_V), BF16),
        compiler_params=_params("arbitrary", "arbitrary", "arbitrary"),
        name="attention",
    )(*args)


def _ret_kernel(*refs, has_s0):
    if has_s0:
        (lg_ref, qf_ref, kf_ref, vf_ref, qb_ref, kb_ref, vb_ref, s0f_ref, s0b_ref,
         of_ref, ob_ref, sf_out, sb_out, sf_scr, sb_scr) = refs
    else:
        (lg_ref, qf_ref, kf_ref, vf_ref, qb_ref, kb_ref, vb_ref,
         of_ref, ob_ref, sf_out, sb_out, sf_scr, sb_scr) = refs
    c = pl.program_id(1)
    cs = RET_CHUNK

    @pl.when(c == 0)
    def _():
        if has_s0:
            sf_scr[...] = s0f_ref[0]
            sb_scr[...] = s0b_ref[0]
        else:
            sf_scr[...] = jnp.zeros_like(sf_scr)
            sb_scr[...] = jnp.zeros_like(sb_scr)

    ii = lax.broadcasted_iota(jnp.int32, (cs, cs), 0).astype(F32)
    jj = lax.broadcasted_iota(jnp.int32, (cs, cs), 1).astype(F32)
    rel = ii - jj
    kscale = RET_DK ** -0.5

    def chunk(q, k, v, s, intra, qdec, kdec, cdec):
        a = _mm_nt(q, k) * intra
        o = _mm(a, v) + _mm(q, s) * qdec
        return o, s * cdec + _mm_tn(k * kdec, v)

    for hd in range(RET_HEADS):
        cols = slice(hd * RET_DK, (hd + 1) * RET_DK)
        lgf = jax.nn.log_sigmoid(jnp.full((1, LANE), lg_ref[0, hd], F32))
        lgb = jax.nn.log_sigmoid(jnp.full((1, LANE), lg_ref[1, hd], F32))

        intra_f = jnp.where(rel >= 0, jnp.exp(jnp.maximum(rel, 0.0) * lgf), 0.0)
        o_f, s_f = chunk(qf_ref[:, cols], kf_ref[:, cols] * kscale, vf_ref[:, cols], sf_scr[hd], intra_f,
                         jnp.exp((ii + 1.0) * lgf), jnp.exp((cs - 1.0 - ii) * lgf), jnp.exp(cs * lgf))
        of_ref[:, cols] = o_f
        sf_scr[hd] = s_f
        sf_out[0, hd] = s_f

        intra_b = jnp.where(rel <= 0, jnp.exp(jnp.maximum(-rel, 0.0) * lgb), 0.0)
        o_b, s_b = chunk(qb_ref[:, cols], kb_ref[:, cols] * kscale, vb_ref[:, cols], sb_scr[hd], intra_b,
                         jnp.exp((cs - ii) * lgb), jnp.exp(ii * lgb), jnp.exp(cs * lgb))
        ob_ref[:, cols] = o_b
        sb_scr[hd] = s_b
        sb_out[0, hd] = s_b


def _retention(zr, decay_logit, batch, seq, s0):
    cs = RET_CHUNK
    nc = seq // cs
    nh = RET_HEADS
    width = nh * RET_DK
    has_s0 = s0 is not None

    def fwd(col):
        return pl.BlockSpec((cs, width), lambda b, c: (b * nc + c, col))

    def bwd(col):
        return pl.BlockSpec((cs, width), lambda b, c: (b * nc + nc - 1 - c, col))

    def state(d):
        return pl.BlockSpec((1, nh, RET_DK, RET_DV), lambda b, c: (b * 2 + d, 0, 0, 0))

    state_out = pl.BlockSpec((1, nh, RET_DK, RET_DV), lambda b, c: (b, 0, 0, 0))
    in_specs = [pl.BlockSpec(memory_space=pltpu.SMEM), fwd(0), fwd(1), fwd(2), bwd(0), bwd(1), bwd(2)]
    args = [decay_logit, zr, zr, zr, zr, zr, zr]
    if has_s0:
        in_specs += [state(0), state(1)]
        args += [s0, s0]
    t = batch * seq
    o_f, o_b, s_f, s_b = pl.pallas_call(
        functools.partial(_ret_kernel, has_s0=has_s0),
        grid=(batch, nc),
        in_specs=in_specs,
        out_specs=[pl.BlockSpec((cs, width), lambda b, c: (b * nc + c, 0)),
                   pl.BlockSpec((cs, width), lambda b, c: (b * nc + nc - 1 - c, 0)),
                   state_out, state_out],
        out_shape=[jax.ShapeDtypeStruct((t, width), F32),
                   jax.ShapeDtypeStruct((t, width), F32),
                   jax.ShapeDtypeStruct((batch, nh, RET_DK, RET_DV), F32),
                   jax.ShapeDtypeStruct((batch, nh, RET_DK, RET_DV), F32)],
        scratch_shapes=[pltpu.VMEM((nh, RET_DK, RET_DV), F32), pltpu.VMEM((nh, RET_DK, RET_DV), F32)],
        compiler_params=_params("arbitrary", "arbitrary"),
        name="retention",
    )(*args)
    return o_f, o_b, s_f, s_b


def _mix_kernel(x_ref, att_ref, of_ref, ob_ref, gr_ref, mod_ref, gnw_ref, wout_ref, n2w_ref, x1_ref, h2_ref):
    o = of_ref[...] + ob_ref[...]
    parts = []
    for hd in range(RET_HEADS):
        oh = o[:, hd * RET_DV:(hd + 1) * RET_DV]
        d = oh - jnp.mean(oh, axis=-1, keepdims=True)
        parts.append(d * lax.rsqrt(jnp.mean(d * d, axis=-1, keepdims=True) + EPS))
    g = gr_ref[...]
    ret = (g * jax.nn.sigmoid(g)) * (jnp.concatenate(parts, axis=-1) * gnw_ref[...])
    na = MLA_HEADS * MLA_V
    mixed = _mm(att_ref[...], wout_ref[:na, :]) + _mm(ret, wout_ref[na:, :])
    m = mod_ref[0]
    x1 = x_ref[...] + m[2:3] * mixed
    x1_ref[...] = x1
    h2_ref[...] = _rms(x1, n2w_ref[...]) * (1.0 + m[4:5]) + m[3:4]


def _mix(x, att, o_f, o_b, zr, mod, w, seq):
    t = x.shape[0]
    tm = ROW_TILE
    per_batch = seq // tm
    const = lambda i: (0, 0)
    row = lambda i: (i, 0)
    half = RET_HEADS * RET_DV
    return pl.pallas_call(
        _mix_kernel,
        grid=(t // tm,),
        in_specs=[pl.BlockSpec((tm, D_MODEL), row),
                  pl.BlockSpec((tm, half), row),
                  pl.BlockSpec((tm, half), row),
                  pl.BlockSpec((tm, half), row),
                  pl.BlockSpec((tm, half), lambda i: (i, 3)),
                  pl.BlockSpec((1, 6, D_MODEL), lambda i: (i // per_batch, 0, 0)),
                  pl.BlockSpec((1, half), const),
                  pl.BlockSpec((D_MODEL, D_MODEL), const),
                  pl.BlockSpec((1, D_MODEL), const)],
        out_specs=[pl.BlockSpec((tm, D_MODEL), row), pl.BlockSpec((tm, D_MODEL), row)],
        out_shape=[jax.ShapeDtypeStruct((t, D_MODEL), F32), jax.ShapeDtypeStruct((t, D_MODEL), F32)],
        compiler_params=_params("arbitrary"),
        name="mix",
    )(x, att, o_f, o_b, zr, mod, w["ret_gn_w"], w["w_out"], w["norm2_w"])


def _top16(jobs):
    def body(r, carry):
        for s_ref, vals_ref, pay_ref, payload in jobs:
            n, cols = s_ref.shape
            rows = lax.broadcasted_iota(jnp.int32, (n, cols), 0).astype(F32)
            s = s_ref[...]
            m = jnp.max(s, axis=0, keepdims=True)
            pos = jnp.min(jnp.where(s == m, rows, float(n)), axis=0, keepdims=True)
            hit = rows == pos
            vals_ref[pl.ds(r, 1), :] = m
            if payload is None:
                pay_ref[pl.ds(r, 1), :] = pos
            else:
                pay_ref[pl.ds(r, 1), :] = jnp.sum(jnp.where(hit, payload, 0.0), axis=0, keepdims=True)
            s_ref[...] = jnp.where(hit, -jnp.inf, s)
        return carry

    lax.fori_loop(0, PEER_TOPK, body, 0)


PAIR_COUNTS = tuple(PEER_TOPK // (a + 1) for a in range(PEER_TOPK))
NUM_PAIRS = sum(PAIR_COUNTS)
PAIR_ROWS = -(-NUM_PAIRS // 8) * 8


def _score_head(hbt, wpqt, k1, k2, g_out, e_out, scr):
    s1_scr, s2_scr, c_scr, p_scr, v1_scr, i1_scr, v2_scr, i2_scr, vt_scr, it_scr = scr
    qt = jnp.dot(wpqt, hbt, preferred_element_type=F32)
    s1_scr[...] = _mm(k1, qt[:PEER_HALF, :])
    s2_scr[...] = _mm(k2, qt[PEER_HALF:, :])
    _top16([(s1_scr, v1_scr, i1_scr, None), (s2_scr, v2_scr, i2_scr, None)])
    c_scr[...] = jnp.full(c_scr.shape, -jnp.inf, F32)
    p_scr[...] = jnp.zeros(p_scr.shape, F32)
    off = 0
    for a, nb in enumerate(PAIR_COUNTS):
        c_scr[off:off + nb, :] = v1_scr[a:a + 1, :] + v2_scr[0:nb, :]
        p_scr[off:off + nb, :] = i1_scr[a:a + 1, :] * float(N_KEYS) + i2_scr[0:nb, :]
        off += nb
    _top16([(c_scr, vt_scr, it_scr, p_scr[...])])
    top = vt_scr[...]
    p = jnp.exp(top - jnp.max(top, axis=0, keepdims=True))
    g_out[...] = p / jnp.sum(p, axis=0, keepdims=True)
    e_out[...] = it_scr[...]


def _score_scratch(tb):
    k = PEER_TOPK
    return ([pltpu.VMEM((N_KEYS, tb), F32)] * 2 + [pltpu.VMEM((PAIR_ROWS, tb), F32)] * 2
            + [pltpu.VMEM((k, tb), F32)] * 6)


def _peer_kernel(tab_hbm, h2_ref, h2n_ref, wpq_ref, k1_ref, k2_ref, x1_ref, mod_ref, y_ref,
                 idx_smem, g_scr, e_scr, ei_scr, hb_scr, *scratch, ntiles):
    bufs = scratch[:GATHER_SLOTS]
    sem_rows, sem_idx = scratch[GATHER_SLOTS:GATHER_SLOTS + 2]
    score_scr = scratch[GATHER_SLOTS + 2:]
    i = pl.program_id(0)
    tb = GATHER_TILE
    ns = GATHER_SLOTS
    ahead = GATHER_AHEAD
    ne = EXPERTS_PER_TOKEN
    slab = SLAB_ROWS
    topk = PEER_TOPK
    groups_per_head = tb // (ns * PEER_HEADS)
    cur = i % 2
    nxt = 1 - cur
    more = i + 1 < ntiles

    def score(src_ref, hd, slot):
        @pl.when(hd == 0)
        def _():
            hb_scr[...] = src_ref[...].T.astype(BF16)
        head_rows = pl.ds(pl.multiple_of(hd * topk, topk), topk)
        _score_head(hb_scr[...], wpq_ref[hd], k1_ref[...], k2_ref[...],
                    g_scr.at[slot, head_rows], e_scr.at[head_rows], score_scr)

    def publish(slot):
        ei_scr[...] = e_scr[...].T.astype(jnp.int32)
        copy = pltpu.make_async_copy(ei_scr, idx_smem.at[slot], sem_idx.at[0])
        copy.start()
        copy.wait()

    def issue(islot, tok, rslot):
        for kk in range(ne):
            ex = idx_smem[islot, tok, kk]
            pltpu.make_async_copy(tab_hbm.at[ex], bufs[rslot].at[:, pl.ds(kk * slab, slab), :],
                                  sem_rows.at[rslot]).start(priority=kk % 2)

    def wait_rows(rslot):
        pltpu.make_async_copy(bufs[rslot], bufs[rslot], sem_rows.at[rslot]).wait()

    @pl.when(i == 0)
    def _():
        def first(hd, carry):
            score(h2_ref, hd, 0)
            return carry
        lax.fori_loop(0, PEER_HEADS, first, 0)
        publish(0)
        for tok in range(ahead):
            issue(0, tok, tok)

    g2 = mod_ref[0][5:6]
    lane = lax.broadcasted_iota(jnp.int32, (ne, tb), 1)

    def compute(j, rslot):
        rows = bufs[rslot]
        x_row = h2_ref[pl.ds(j, 1), :]
        acc = None
        for c in range(slab):
            u_c = rows[0, pl.ds(c, ne, stride=slab), :]
            term = u_c * x_row[:, c * LANE:(c + 1) * LANE]
            acc = term if acc is None else acc + term
        pre = jnp.sum(acc, axis=-1, keepdims=True)
        act = 0.5 * pre * (1.0 + lax.erf(pre * (2.0 ** -0.5)))
        gate = jnp.sum(jnp.where(lane == j, g_scr[cur], 0.0), axis=-1, keepdims=True)
        wgt = gate * act
        outs = []
        for c in range(slab):
            v_c = rows[1, pl.ds(c, ne, stride=slab), :]
            outs.append(jnp.sum(v_c * wgt, axis=0, keepdims=True))
        out = jnp.concatenate(outs, axis=-1)
        y_ref[pl.ds(j, 1), :] = x1_ref[pl.ds(j, 1), :] + g2 * out

    def phase(hd, carry):
        score(h2n_ref, hd, nxt)

        @pl.when(hd == PEER_HEADS - 1)
        def _():
            publish(nxt)

        def group(gq, inner):
            g = hd * groups_per_head + gq
            for u in range(ns):
                j = g * ns + u
                wait_rows(u)
                jj = j + ahead
                over = jj >= tb
                issue(jnp.where(over, nxt, cur), jnp.where(over, jj - tb, jj), (u + ahead) % ns)
                compute(j, u)
            return inner

        lax.fori_loop(0, groups_per_head, group, 0)
        return carry

    lax.fori_loop(0, PEER_HEADS, phase, 0)

    @pl.when(jnp.logical_not(more))
    def _():
        for u in range(ahead):
            wait_rows(u)


def _peer(tab, h2, x1, mod, w, seq):
    t = h2.shape[0]
    tb = GATHER_TILE
    ntiles = t // tb
    per_batch = seq // tb
    row = lambda i: (i, 0)
    const = lambda i: (0, 0)
    return pl.pallas_call(
        functools.partial(_peer_kernel, ntiles=ntiles),
        grid=(ntiles,),
        in_specs=[pl.BlockSpec(memory_space=pl.ANY),
                  pl.BlockSpec((tb, D_MODEL), row),
                  pl.BlockSpec((tb, D_MODEL), lambda i: (jnp.minimum(i + 1, ntiles - 1), 0)),
                  pl.BlockSpec((PEER_HEADS, PEER_QDIM, D_MODEL), lambda i: (0, 0, 0)),
                  pl.BlockSpec((N_KEYS, PEER_HALF), const),
                  pl.BlockSpec((N_KEYS, PEER_HALF), const),
                  pl.BlockSpec((tb, D_MODEL), row),
                  pl.BlockSpec((1, 6, D_MODEL), lambda i: (i // per_batch, 0, 0))],
        out_specs=pl.BlockSpec((tb, D_MODEL), row),
        out_shape=jax.ShapeDtypeStruct((t, D_MODEL), F32),
        scratch_shapes=[pltpu.SMEM((2, tb, EXPERTS_PER_TOKEN), jnp.int32),
                        pltpu.VMEM((2, EXPERTS_PER_TOKEN, tb), F32),
                        pltpu.VMEM((EXPERTS_PER_TOKEN, tb), F32),
                        pltpu.VMEM((tb, EXPERTS_PER_TOKEN), jnp.int32),
                        pltpu.VMEM((D_MODEL, tb), BF16)]
                       + [pltpu.VMEM((2, EXPERTS_PER_TOKEN * SLAB_ROWS, LANE), F32)] * GATHER_SLOTS
                       + [pltpu.SemaphoreType.DMA((GATHER_SLOTS,)), pltpu.SemaphoreType.DMA((1,))]
                       + _score_scratch(tb),
        compiler_params=pltpu.CompilerParams(dimension_semantics=("arbitrary",),
                                             vmem_limit_bytes=VMEM_LIMIT,
                                             disable_bounds_checks=True),
        name="peer",
    )(tab, h2, h2, w["w_pq"], w["sub_keys1"], w["sub_keys2"], x1, mod)


def _expert_slabs(u_table, v_table):
    n = u_table.shape[0]
    u = u_table.reshape(n, 1, SLAB_ROWS, LANE)
    v = v_table.reshape(n, 1, SLAB_ROWS, LANE)
    return jnp.concatenate([u, v], axis=1)


def _rope_tables(seq):
    rows = seq // GRID_W
    r = jnp.repeat(jnp.arange(rows, dtype=F32), GRID_W)
    col = jnp.tile(jnp.arange(GRID_W, dtype=F32), rows)
    nf = ROPE_AXIS // 2
    freqs = jnp.power(ROPE_BASE, -jnp.arange(nf, dtype=F32) / nf)
    ar, ac = r[:, None] * freqs, col[:, None] * freqs
    pad = jnp.zeros((seq, LANE - MLA_ROPE), F32)
    cos = jnp.concatenate([jnp.cos(ar), jnp.cos(ar), jnp.cos(ac), jnp.cos(ac), pad], axis=-1)
    sin = jnp.concatenate([-jnp.sin(ar), jnp.sin(ar), -jnp.sin(ac), jnp.sin(ac), pad], axis=-1)
    return cos, sin


def _prep_weights(norm1_w, w_in, q_norm_w, w_uq, kv_norm_w, w_ukv, q_head_norm_w, k_head_norm_w,
                  ret_gn_w, w_out, norm2_w, w_pq, sub_keys1, sub_keys2):
    cut = Q_RANK + KV_RANK + MLA_ROPE
    w_in_p = jnp.concatenate([w_in[:, :cut], jnp.zeros((D_MODEL, LANE - MLA_ROPE), F32), w_in[:, cut:]], axis=1)
    wq = w_uq.reshape(Q_RANK, MLA_HEADS, MLA_QK)
    wq = jnp.pad(wq, ((0, 0), (0, 0), (0, HEAD_PAD - MLA_QK))).reshape(Q_RANK, MLA_HEADS * HEAD_PAD)

    def head_w(v):
        return jnp.pad(v, (0, HEAD_PAD - MLA_QK)).reshape(1, HEAD_PAD)

    return {
        "norm1_w": norm1_w.reshape(1, -1), "w_in": w_in_p.astype(BF16),
        "q_norm_w": q_norm_w.reshape(1, -1), "kv_norm_w": kv_norm_w.reshape(1, -1),
        "w_uq": wq.astype(BF16), "w_ukv": w_ukv.astype(BF16),
        "q_head_norm_w": head_w(q_head_norm_w), "k_head_norm_w": head_w(k_head_norm_w),
        "ret_gn_w": ret_gn_w.reshape(1, -1), "w_out": w_out.astype(BF16), "norm2_w": norm2_w.reshape(1, -1),
        "w_pq": w_pq.astype(BF16).reshape(D_MODEL, PEER_HEADS, PEER_QDIM).transpose(1, 2, 0), "sub_keys1": sub_keys1.astype(BF16), "sub_keys2": sub_keys2.astype(BF16),
    }


def _trunk(x, mod, w, decay_logit, tab, ctx):
    batch, seq, _ = x.shape
    x2 = x.reshape(batch * seq, D_MODEL)
    use_rope = ctx is not None
    if use_rope:
        cos, sin = _rope_tables(seq)
    else:
        cos = sin = jnp.zeros((ROW_TILE, LANE), F32)
    ckv, krope, q, k, v, zr = _in_proj(x2, mod, w, cos, sin, seq, use_rope)
    if use_rope:
        ckv_c, krope_c, state_c = ctx
        past = ckv_c.shape[1]
        kr_c = jnp.pad(krope_c.reshape(batch * past, MLA_ROPE), ((0, 0), (0, LANE - MLA_ROPE)))
        ctx_kv = _kv_up(ckv_c.reshape(batch * past, KV_RANK), kr_c, w)
        s0 = state_c.reshape(batch * 2, RET_HEADS, RET_DK, RET_DV)
    else:
        ctx_kv, s0 = None, None
    att = _attention(q, k, v, batch, seq, ctx_kv)
    o_f, o_b, s_f, s_b = _retention(zr, decay_logit, batch, seq, s0)
    x1, h2 = _mix(x2, att, o_f, o_b, zr, mod, w, seq)
    y = _peer(tab, h2, x1, mod, w, seq)
    states = jnp.stack([s_f, s_b], axis=1)
    return y.reshape(batch, seq, D_MODEL), ckv.reshape(batch, seq, KV_RANK), krope.reshape(batch, seq, MLA_ROPE), states


def kernel(x_prompt, x_sample, c, cache_ckv, cache_krope, state_ret, c_ctx, w_ada, b_ada, norm1_w, w_in,
           q_norm_w, w_uq, kv_norm_w, w_ukv, q_head_norm_w, k_head_norm_w, ret_decay_logit, ret_gn_w,
           w_out, norm2_w, w_pq, sub_keys1, sub_keys2, u_table, v_table):
    depth = w_ada.shape[0]
    nb_ctx = x_prompt.shape[0]
    nb_lat = x_sample.shape[0]
    y_prompt, y_sample = x_prompt, x_sample
    ckv_list, krope_list, ret_list = [], [], []
    for l in range(depth):
        cond_rows = -(-(nb_lat + 1) // 8) * 8
        cond = jnp.concatenate([c, c_ctx[None, :], jnp.zeros((cond_rows - nb_lat - 1, D_MODEL), F32)], axis=0)
        mod = _ada(cond, w_ada[l], b_ada[l])
        mod_lat = mod[:nb_lat].reshape(nb_lat, 6, D_MODEL)
        mod_ctx = jnp.broadcast_to(mod[nb_lat].reshape(1, 6, D_MODEL), (nb_ctx, 6, D_MODEL))
        w = _prep_weights(norm1_w[l], w_in[l], q_norm_w[l], w_uq[l], kv_norm_w[l], w_ukv[l], q_head_norm_w[l],
                          k_head_norm_w[l], ret_gn_w[l], w_out[l], norm2_w[l], w_pq[l], sub_keys1[l], sub_keys2[l])
        tab = _expert_slabs(u_table[l], v_table[l])
        y_prompt, ckv_l, krope_l, ret_l = _trunk(y_prompt, mod_ctx, w, ret_decay_logit[l], tab, None)
        ckv_list.append(ckv_l)
        krope_list.append(krope_l)
        ret_list.append(ret_l)
        y_sample, _, _, _ = _trunk(y_sample, mod_lat, w, ret_decay_logit[l], tab,
                                   (cache_ckv[:, l], cache_krope[:, l], state_ret[:, l]))
    return (y_prompt, y_sample, jnp.stack(ckv_list, axis=1), jnp.stack(krope_list, axis=1),
            jnp.stack(ret_list, axis=1))
```

```python
import functools

import jax
import jax.numpy as jnp
from jax import lax
from jax.experimental import pallas as pl
from jax.experimental.pallas import tpu as pltpu
from jax.experimental.pallas import tpu_sc as plsc

F32 = jnp.float32
BF16 = jnp.bfloat16

D_MODEL = 1024
GRID_W = 64
MLA_HEADS = 4
MLA_NOPE = 128
MLA_ROPE = 64
MLA_QK = MLA_NOPE + MLA_ROPE
MLA_V = 128
Q_RANK = 512
KV_RANK = 256
ROPE_AXIS = MLA_ROPE // 2
ROPE_BASE = 10000.0
RET_HEADS = 4
RET_DK = 128
RET_DV = 128
RET_CHUNK = 128
PEER_HEADS = 8
PEER_QDIM = 256
PEER_HALF = PEER_QDIM // 2
N_KEYS = 128
PEER_TOPK = 16
EPS = 1e-6

LANE = 128
HEAD_PAD = 2 * LANE
RET_W = 4 * RET_HEADS * RET_DK
IN_PAD = Q_RANK + KV_RANK + LANE + RET_W
VMEM_LIMIT = 48 * 1024 * 1024

ROW_TILE = 256
Q_TILE = 256
GATHER_TILE = 128
GATHER_SLOTS = 4
GATHER_AHEAD = GATHER_SLOTS - 1
EXPERTS_PER_TOKEN = PEER_HEADS * PEER_TOPK
SLAB_ROWS = D_MODEL // LANE


def _params(*sem):
    return pltpu.CompilerParams(dimension_semantics=sem, vmem_limit_bytes=VMEM_LIMIT)


def _rms(x, w):
    return x * lax.rsqrt(jnp.mean(x * x, axis=-1, keepdims=True) + EPS) * w


def _mm(a, b):
    return jnp.dot(a.astype(BF16), b.astype(BF16), preferred_element_type=F32)


def _mm_nt(a, b):
    return lax.dot_general(a.astype(BF16), b.astype(BF16), (((1,), (1,)), ((), ())),
                           preferred_element_type=F32)


def _mm_tn(a, b):
    return lax.dot_general(a.astype(BF16), b.astype(BF16), (((0,), (0,)), ((), ())),
                           preferred_element_type=F32)


def _ada_kernel(c_ref, w_ref, b_ref, o_ref):
    c = c_ref[...]
    o_ref[...] = _mm(c * jax.nn.sigmoid(c), w_ref[...]) + b_ref[...]


def _ada(cond, w_ada, b_ada):
    rows, d = cond.shape
    n = w_ada.shape[1]
    tn = 1536
    return pl.pallas_call(
        _ada_kernel,
        grid=(n // tn,),
        in_specs=[pl.BlockSpec((rows, d), lambda j: (0, 0)),
                  pl.BlockSpec((d, tn), lambda j: (0, j)),
                  pl.BlockSpec((1, tn), lambda j: (0, j))],
        out_specs=pl.BlockSpec((rows, tn), lambda j: (0, j)),
        out_shape=jax.ShapeDtypeStruct((rows, n), F32),
        compiler_params=_params("arbitrary"),
        name="ada",
    )(cond, w_ada, b_ada.reshape(1, n))


def _rope_tile(x, cos, sin):
    lane = lax.broadcasted_iota(jnp.int32, x.shape, 1)
    partner = jnp.where((lane % 32) < 16, pltpu.roll(x, LANE - 16, 1), pltpu.roll(x, 16, 1))
    return x * cos + partner * sin


def _kv_heads(kv, kr, khw, cos, sin, use_rope):
    krw = kr * khw[:, LANE:]
    if use_rope:
        krw = _rope_tile(krw, cos, sin)
    ssq_r = jnp.sum(kr * kr, axis=-1, keepdims=True)
    ks, vs = [], []
    for hd in range(MLA_HEADS):
        kn = kv[:, hd * HEAD_PAD: hd * HEAD_PAD + LANE]
        r = lax.rsqrt((jnp.sum(kn * kn, axis=-1, keepdims=True) + ssq_r) / MLA_QK + EPS)
        ks += [kn * r * khw[:, :LANE], krw * r]
        vs.append(kv[:, hd * HEAD_PAD + LANE: (hd + 1) * HEAD_PAD])
    return jnp.concatenate(ks, axis=-1).astype(BF16), jnp.concatenate(vs, axis=-1).astype(BF16)


def _inproj_kernel(x_ref, mod_ref, n1w_ref, win_ref, qnw_ref, kvnw_ref, wuq_ref, wukv_ref, qhw_ref, khw_ref,
                   cos_ref, sin_ref, ckv_ref, krope_ref, q_ref, k_ref, v_ref, zr_ref, *, use_rope):
    m = mod_ref[0]
    h = _rms(x_ref[...], n1w_ref[...]) * (1.0 + m[1:2]) + m[0:1]
    z = _mm(h, win_ref[...])
    kr = z[:, Q_RANK + KV_RANK: Q_RANK + KV_RANK + LANE]
    zr_ref[...] = z[:, Q_RANK + KV_RANK + LANE:]
    ckvn = _rms(z[:, Q_RANK: Q_RANK + KV_RANK], kvnw_ref[...])
    ckv_ref[...] = ckvn
    krope_ref[...] = kr[:, :MLA_ROPE]
    cos, sin = cos_ref[...], sin_ref[...]

    q = _mm(_rms(z[:, :Q_RANK], qnw_ref[...]), wuq_ref[...])
    qhw = qhw_ref[...]
    qs = []
    for hd in range(MLA_HEADS):
        qn = q[:, hd * HEAD_PAD: hd * HEAD_PAD + LANE]
        qr = q[:, hd * HEAD_PAD + LANE: (hd + 1) * HEAD_PAD]
        ssq = jnp.sum(qn * qn, axis=-1, keepdims=True) + jnp.sum(qr * qr, axis=-1, keepdims=True)
        r = lax.rsqrt(ssq / MLA_QK + EPS)
        qrw = qr * qhw[:, LANE:]
        if use_rope:
            qrw = _rope_tile(qrw, cos, sin)
        qs += [qn * r * qhw[:, :LANE], qrw * r]
    q_ref[...] = jnp.concatenate(qs, axis=-1).astype(BF16)

    k, v = _kv_heads(_mm(ckvn, wukv_ref[...]), kr, khw_ref[...], cos, sin, use_rope)
    k_ref[...] = k
    v_ref[...] = v


def _in_proj(x, mod, w, cos, sin, seq, use_rope):
    t = x.shape[0]
    tm = ROW_TILE
    per_batch = seq // tm
    const = lambda i: (0, 0)
    row = lambda i: (i, 0)
    if use_rope:
        pos = lambda i: (i % per_batch, 0)
    else:
        pos = const
    hp = MLA_HEADS * HEAD_PAD
    return pl.pallas_call(
        functools.partial(_inproj_kernel, use_rope=use_rope),
        grid=(t // tm,),
        in_specs=[pl.BlockSpec((tm, D_MODEL), row),
                  pl.BlockSpec((1, 6, D_MODEL), lambda i: (i // per_batch, 0, 0)),
                  pl.BlockSpec((1, D_MODEL), const),
                  pl.BlockSpec((D_MODEL, IN_PAD), const),
                  pl.BlockSpec((1, Q_RANK), const),
                  pl.BlockSpec((1, KV_RANK), const),
                  pl.BlockSpec((Q_RANK, hp), const),
                  pl.BlockSpec((KV_RANK, hp), const),
                  pl.BlockSpec((1, HEAD_PAD), const),
                  pl.BlockSpec((1, HEAD_PAD), const),
                  pl.BlockSpec((tm, LANE), pos),
                  pl.BlockSpec((tm, LANE), pos)],
        out_specs=[pl.BlockSpec((tm, KV_RANK), row),
                   pl.BlockSpec((tm, MLA_ROPE), row),
                   pl.BlockSpec((tm, hp), row),
                   pl.BlockSpec((tm, hp), row),
                   pl.BlockSpec((tm, MLA_HEADS * MLA_V), row),
                   pl.BlockSpec((tm, RET_W), row)],
        out_shape=[jax.ShapeDtypeStruct((t, KV_RANK), F32),
                   jax.ShapeDtypeStruct((t, MLA_ROPE), F32),
                   jax.ShapeDtypeStruct((t, hp), BF16),
                   jax.ShapeDtypeStruct((t, hp), BF16),
                   jax.ShapeDtypeStruct((t, MLA_HEADS * MLA_V), BF16),
                   jax.ShapeDtypeStruct((t, RET_W), F32)],
        compiler_params=_params("arbitrary"),
        name="in_proj",
    )(x, mod, w["norm1_w"], w["w_in"], w["q_norm_w"], w["kv_norm_w"], w["w_uq"], w["w_ukv"],
      w["q_head_norm_w"], w["k_head_norm_w"], cos, sin)


def _kvup_kernel(ckv_ref, kr_ref, wukv_ref, khw_ref, k_ref, v_ref):
    k, v = _kv_heads(_mm(ckv_ref[...], wukv_ref[...]), kr_ref[...], khw_ref[...], None, None, False)
    k_ref[...] = k
    v_ref[...] = v


def _kv_up(ckv, kr, w):
    t = ckv.shape[0]
    tm = ROW_TILE
    hp = MLA_HEADS * HEAD_PAD
    const = lambda i: (0, 0)
    row = lambda i: (i, 0)
    return pl.pallas_call(
        _kvup_kernel,
        grid=(t // tm,),
        in_specs=[pl.BlockSpec((tm, KV_RANK), row),
                  pl.BlockSpec((tm, LANE), row),
                  pl.BlockSpec((KV_RANK, hp), const),
                  pl.BlockSpec((1, HEAD_PAD), const)],
        out_specs=[pl.BlockSpec((tm, hp), row),
                   pl.BlockSpec((tm, MLA_HEADS * MLA_V), row)],
        out_shape=[jax.ShapeDtypeStruct((t, hp), BF16),
                   jax.ShapeDtypeStruct((t, MLA_HEADS * MLA_V), BF16)],
        compiler_params=_params("arbitrary"),
        name="kv_up",
    )(ckv, kr, w["w_ukv"], w["k_head_norm_w"])


def _attn_kernel(*refs, has_ctx):
    if has_ctx:
        q_ref, k_ref, v_ref, kc_ref, vc_ref, o_ref = refs
    else:
        q_ref, k_ref, v_ref, o_ref = refs
    scale = MLA_QK ** -0.5
    q = q_ref[...]
    s = _mm_nt(q, k_ref[...]) * scale
    m = jnp.max(s, axis=-1, keepdims=True)
    if has_ctx:
        sc = _mm_nt(q, kc_ref[...]) * scale
        m = jnp.maximum(m, jnp.max(sc, axis=-1, keepdims=True))
    p = jnp.exp(s - m)
    den = jnp.sum(p, axis=-1, keepdims=True)
    o = _mm(p, v_ref[...])
    if has_ctx:
        pc = jnp.exp(sc - m)
        den = den + jnp.sum(pc, axis=-1, keepdims=True)
        o = o + _mm(pc, vc_ref[...])
    o_ref[...] = (o / den).astype(BF16)


def _attention(q, k, v, batch, seq, ctx_kv):
    tq = min(Q_TILE, seq)
    nq = seq // tq
    has_ctx = ctx_kv is not None
    qmap = lambda b, h, i: (b * nq + i, h)
    kvmap = lambda b, h, i: (b, h)
    in_specs = [pl.BlockSpec((tq, HEAD_PAD), qmap),
                pl.BlockSpec((seq, HEAD_PAD), kvmap),
                pl.BlockSpec((seq, MLA_V), kvmap)]
    args = [q, k, v]
    if has_ctx:
        kc, vc = ctx_kv
        past = kc.shape[0] // batch
        in_specs += [pl.BlockSpec((past, HEAD_PAD), kvmap), pl.BlockSpec((past, MLA_V), kvmap)]
        args += [kc, vc]
    return pl.pallas_call(
        functools.partial(_attn_kernel, has_ctx=has_ctx),
        grid=(batch, MLA_HEADS, nq),
        in_specs=in_specs,
        out_specs=pl.BlockSpec((tq, MLA_V), qmap),
        out_shape=jax.ShapeDtypeStruct((batch * seq, MLA_HEADS * MLA_V), BF16),
        compiler_params=_params("arbitrary", "arbitrary", "arbitrary"),
        name="attention",
    )(*args)


def _ret_kernel(*refs, has_s0):
    if has_s0:
        (lg_ref, qf_ref, kf_ref, vf_ref, qb_ref, kb_ref, vb_ref, s0f_ref, s0b_ref,
         of_ref, ob_ref, sf_out, sb_out, sf_scr, sb_scr) = refs
    else:
        (lg_ref, qf_ref, kf_ref, vf_ref, qb_ref, kb_ref, vb_ref,
         of_ref, ob_ref, sf_out, sb_out, sf_scr, sb_scr) = refs
    c = pl.program_id(1)
    cs = RET_CHUNK

    @pl.when(c == 0)
    def _():
        if has_s0:
            sf_scr[...] = s0f_ref[0]
            sb_scr[...] = s0b_ref[0]
        else:
            sf_scr[...] = jnp.zeros_like(sf_scr)
            sb_scr[...] = jnp.zeros_like(sb_scr)

    ii = lax.broadcasted_iota(jnp.int32, (cs, cs), 0).astype(F32)
    jj = lax.broadcasted_iota(jnp.int32, (cs, cs), 1).astype(F32)
    rel = ii - jj
    kscale = RET_DK ** -0.5

    def chunk(q, k, v, s, intra, qdec, kdec, cdec):
        a = _mm_nt(q, k) * intra
        o = _mm(a, v) + _mm(q, s) * qdec
        return o, s * cdec + _mm_tn(k * kdec, v)

    for hd in range(RET_HEADS):
        cols = slice(hd * RET_DK, (hd + 1) * RET_DK)
        lgf = jax.nn.log_sigmoid(jnp.full((1, LANE), lg_ref[0, hd], F32))
        lgb = jax.nn.log_sigmoid(jnp.full((1, LANE), lg_ref[1, hd], F32))

        intra_f = jnp.where(rel >= 0, jnp.exp(jnp.maximum(rel, 0.0) * lgf), 0.0)
        o_f, s_f = chunk(qf_ref[:, cols], kf_ref[:, cols] * kscale, vf_ref[:, cols], sf_scr[hd], intra_f,
                         jnp.exp((ii + 1.0) * lgf), jnp.exp((cs - 1.0 - ii) * lgf), jnp.exp(cs * lgf))
        of_ref[:, cols] = o_f
        sf_scr[hd] = s_f
        sf_out[0, hd] = s_f

        intra_b = jnp.where(rel <= 0, jnp.exp(jnp.maximum(-rel, 0.0) * lgb), 0.0)
        o_b, s_b = chunk(qb_ref[:, cols], kb_ref[:, cols] * kscale, vb_ref[:, cols], sb_scr[hd], intra_b,
                         jnp.exp((cs - ii) * lgb), jnp.exp(ii * lgb), jnp.exp(cs * lgb))
        ob_ref[:, cols] = o_b
        sb_scr[hd] = s_b
        sb_out[0, hd] = s_b


def _retention(zr, decay_logit, batch, seq, s0):
    cs = RET_CHUNK
    nc = seq // cs
    nh = RET_HEADS
    width = nh * RET_DK
    has_s0 = s0 is not None

    def fwd(col):
        return pl.BlockSpec((cs, width), lambda b, c: (b * nc + c, col))

    def bwd(col):
        return pl.BlockSpec((cs, width), lambda b, c: (b * nc + nc - 1 - c, col))

    def state(d):
        return pl.BlockSpec((1, nh, RET_DK, RET_DV), lambda b, c: (b * 2 + d, 0, 0, 0))

    state_out = pl.BlockSpec((1, nh, RET_DK, RET_DV), lambda b, c: (b, 0, 0, 0))
    in_specs = [pl.BlockSpec(memory_space=pltpu.SMEM), fwd(0), fwd(1), fwd(2), bwd(0), bwd(1), bwd(2)]
    args = [decay_logit, zr, zr, zr, zr, zr, zr]
    if has_s0:
        in_specs += [state(0), state(1)]
        args += [s0, s0]
    t = batch * seq
    o_f, o_b, s_f, s_b = pl.pallas_call(
        functools.partial(_ret_kernel, has_s0=has_s0),
        grid=(batch, nc),
        in_specs=in_specs,
        out_specs=[pl.BlockSpec((cs, width), lambda b, c: (b * nc + c, 0)),
                   pl.BlockSpec((cs, width), lambda b, c: (b * nc + nc - 1 - c, 0)),
                   state_out, state_out],
        out_shape=[jax.ShapeDtypeStruct((t, width), F32),
                   jax.ShapeDtypeStruct((t, width), F32),
                   jax.ShapeDtypeStruct((batch, nh, RET_DK, RET_DV), F32),
                   jax.ShapeDtypeStruct((batch, nh, RET_DK, RET_DV), F32)],
        scratch_shapes=[pltpu.VMEM((nh, RET_DK, RET_DV), F32), pltpu.VMEM((nh, RET_DK, RET_DV), F32)],
        compiler_params=_params("arbitrary", "arbitrary"),
        name="retention",
    )(*args)
    return o_f, o_b, s_f, s_b


def _mix_kernel(x_ref, att_ref, of_ref, ob_ref, gr_ref, mod_ref, gnw_ref, wout_ref, n2w_ref, x1_ref, h2_ref):
    o = of_ref[...] + ob_ref[...]
    parts = []
    for hd in range(RET_HEADS):
        oh = o[:, hd * RET_DV:(hd + 1) * RET_DV]
        d = oh - jnp.mean(oh, axis=-1, keepdims=True)
        parts.append(d * lax.rsqrt(jnp.mean(d * d, axis=-1, keepdims=True) + EPS))
    g = gr_ref[...]
    ret = (g * jax.nn.sigmoid(g)) * (jnp.concatenate(parts, axis=-1) * gnw_ref[...])
    na = MLA_HEADS * MLA_V
    mixed = _mm(att_ref[...], wout_ref[:na, :]) + _mm(ret, wout_ref[na:, :])
    m = mod_ref[0]
    x1 = x_ref[...] + m[2:3] * mixed
    x1_ref[...] = x1
    h2_ref[...] = _rms(x1, n2w_ref[...]) * (1.0 + m[4:5]) + m[3:4]


def _mix(x, att, o_f, o_b, zr, mod, w, seq):
    t = x.shape[0]
    tm = ROW_TILE
    per_batch = seq // tm
    const = lambda i: (0, 0)
    row = lambda i: (i, 0)
    half = RET_HEADS * RET_DV
    return pl.pallas_call(
        _mix_kernel,
        grid=(t // tm,),
        in_specs=[pl.BlockSpec((tm, D_MODEL), row),
                  pl.BlockSpec((tm, half), row),
                  pl.BlockSpec((tm, half), row),
                  pl.BlockSpec((tm, half), row),
                  pl.BlockSpec((tm, half), lambda i: (i, 3)),
                  pl.BlockSpec((1, 6, D_MODEL), lambda i: (i // per_batch, 0, 0)),
                  pl.BlockSpec((1, half), const),
                  pl.BlockSpec((D_MODEL, D_MODEL), const),
                  pl.BlockSpec((1, D_MODEL), const)],
        out_specs=[pl.BlockSpec((tm, D_MODEL), row), pl.BlockSpec((tm, D_MODEL), row)],
        out_shape=[jax.ShapeDtypeStruct((t, D_MODEL), F32), jax.ShapeDtypeStruct((t, D_MODEL), F32)],
        compiler_params=_params("arbitrary"),
        name="mix",
    )(x, att, o_f, o_b, zr, mod, w["ret_gn_w"], w["w_out"], w["norm2_w"])


def _top16(jobs):
    def body(r, carry):
        for s_ref, vals_ref, pay_ref, payload in jobs:
            n, cols = s_ref.shape
            rows = lax.broadcasted_iota(jnp.int32, (n, cols), 0).astype(F32)
            s = s_ref[...]
            m = jnp.max(s, axis=0, keepdims=True)
            pos = jnp.min(jnp.where(s == m, rows, float(n)), axis=0, keepdims=True)
            hit = rows == pos
            vals_ref[pl.ds(r, 1), :] = m
            if payload is None:
                pay_ref[pl.ds(r, 1), :] = pos
            else:
                pay_ref[pl.ds(r, 1), :] = jnp.sum(jnp.where(hit, payload, 0.0), axis=0, keepdims=True)
            s_ref[...] = jnp.where(hit, -jnp.inf, s)
        return carry

    lax.fori_loop(0, PEER_TOPK, body, 0)


PAIR_COUNTS = tuple(PEER_TOPK // (a + 1) for a in range(PEER_TOPK))
NUM_PAIRS = sum(PAIR_COUNTS)
PAIR_ROWS = -(-NUM_PAIRS // 8) * 8


def _score_head(hbt, wpqt, k1, k2, g_out, e_out, scr):
    s1_scr, s2_scr, c_scr, p_scr, v1_scr, i1_scr, v2_scr, i2_scr, vt_scr, it_scr = scr
    qt = jnp.dot(wpqt, hbt, preferred_element_type=F32)
    s1_scr[...] = _mm(k1, qt[:PEER_HALF, :])
    s2_scr[...] = _mm(k2, qt[PEER_HALF:, :])
    _top16([(s1_scr, v1_scr, i1_scr, None), (s2_scr, v2_scr, i2_scr, None)])
    c_scr[...] = jnp.full(c_scr.shape, -jnp.inf, F32)
    p_scr[...] = jnp.zeros(p_scr.shape, F32)
    off = 0
    for a, nb in enumerate(PAIR_COUNTS):
        c_scr[off:off + nb, :] = v1_scr[a:a + 1, :] + v2_scr[0:nb, :]
        p_scr[off:off + nb, :] = i1_scr[a:a + 1, :] * float(N_KEYS) + i2_scr[0:nb, :]
        off += nb
    _top16([(c_scr, vt_scr, it_scr, p_scr[...])])
    top = vt_scr[...]
    p = jnp.exp(top - jnp.max(top, axis=0, keepdims=True))
    g_out[...] = p / jnp.sum(p, axis=0, keepdims=True)
    e_out[...] = it_scr[...]


def _score_scratch(tb):
    k = PEER_TOPK
    return ([pltpu.VMEM((N_KEYS, tb), F32)] * 2 + [pltpu.VMEM((PAIR_ROWS, tb), F32)] * 2
            + [pltpu.VMEM((k, tb), F32)] * 6)


def _peer_kernel(tab_hbm, h2_ref, h2n_ref, wpq_ref, k1_ref, k2_ref, x1_ref, mod_ref, *rest, ntiles, aliased):
    y_ref, idx_smem, g_scr, e_scr, ei_scr, hb_scr = rest[int(aliased):int(aliased) + 6]
    scratch = rest[int(aliased) + 6:]
    _peer_body(tab_hbm, h2_ref, h2n_ref, wpq_ref, k1_ref, k2_ref, x1_ref, mod_ref, y_ref,
               idx_smem, g_scr, e_scr, ei_scr, hb_scr, *scratch, ntiles=ntiles)


def _peer_body(tab_hbm, h2_ref, h2n_ref, wpq_ref, k1_ref, k2_ref, x1_ref, mod_ref, y_ref,
               idx_smem, g_scr, e_scr, ei_scr, hb_scr, *scratch, ntiles):
    bufs = scratch[:GATHER_SLOTS]
    sem_rows, sem_idx = scratch[GATHER_SLOTS:GATHER_SLOTS + 2]
    score_scr = scratch[GATHER_SLOTS + 2:]
    i = pl.program_id(0)
    tb = GATHER_TILE
    ns = GATHER_SLOTS
    ahead = GATHER_AHEAD
    ne = EXPERTS_PER_TOKEN
    slab = SLAB_ROWS
    topk = PEER_TOPK
    groups_per_head = tb // (ns * PEER_HEADS)
    cur = i % 2
    nxt = 1 - cur
    more = i + 1 < ntiles

    def score(src_ref, hd, slot):
        @pl.when(hd == 0)
        def _():
            hb_scr[...] = src_ref[...].T.astype(BF16)
        head_rows = pl.ds(pl.multiple_of(hd * topk, topk), topk)
        _score_head(hb_scr[...], wpq_ref[hd], k1_ref[...], k2_ref[...],
                    g_scr.at[slot, head_rows], e_scr.at[head_rows], score_scr)

    def publish(slot):
        ei_scr[...] = e_scr[...].T.astype(jnp.int32)
        copy = pltpu.make_async_copy(ei_scr, idx_smem.at[slot], sem_idx.at[0])
        copy.start()
        copy.wait()

    def issue(islot, tok, rslot):
        for kk in range(ne):
            ex = idx_smem[islot, tok, kk]
            pltpu.make_async_copy(tab_hbm.at[ex], bufs[rslot].at[:, pl.ds(kk * slab, slab), :],
                                  sem_rows.at[rslot]).start(priority=kk % 2)

    def wait_rows(rslot):
        pltpu.make_async_copy(bufs[rslot], bufs[rslot], sem_rows.at[rslot]).wait()

    @pl.when(i == 0)
    def _():
        def first(hd, carry):
            score(h2_ref, hd, 0)
            return carry
        lax.fori_loop(0, PEER_HEADS, first, 0)
        publish(0)
        for tok in range(ahead):
            issue(0, tok, tok)

    g2 = mod_ref[0][5:6]
    lane = lax.broadcasted_iota(jnp.int32, (ne, tb), 1)

    def compute(j, rslot):
        rows = bufs[rslot]
        x_row = h2_ref[pl.ds(j, 1), :]
        acc = None
        for c in range(slab):
            u_c = rows[0, pl.ds(c, ne, stride=slab), :]
            term = u_c * x_row[:, c * LANE:(c + 1) * LANE]
            acc = term if acc is None else acc + term
        pre = jnp.sum(acc, axis=-1, keepdims=True)
        act = 0.5 * pre * (1.0 + lax.erf(pre * (2.0 ** -0.5)))
        gate = jnp.sum(jnp.where(lane == j, g_scr[cur], 0.0), axis=-1, keepdims=True)
        wgt = gate * act
        outs = []
        for c in range(slab):
            v_c = rows[1, pl.ds(c, ne, stride=slab), :]
            outs.append(jnp.sum(v_c * wgt, axis=0, keepdims=True))
        out = jnp.concatenate(outs, axis=-1)
        y_ref[pl.ds(j, 1), :] = x1_ref[pl.ds(j, 1), :] + g2 * out

    def phase(hd, carry):
        score(h2n_ref, hd, nxt)

        @pl.when(hd == PEER_HEADS - 1)
        def _():
            publish(nxt)

        def group(gq, inner):
            g = hd * groups_per_head + gq
            for u in range(ns):
                j = g * ns + u
                wait_rows(u)
                jj = j + ahead
                over = jj >= tb
                issue(jnp.where(over, nxt, cur), jnp.where(over, jj - tb, jj), (u + ahead) % ns)
                compute(j, u)
            return inner

        lax.fori_loop(0, groups_per_head, group, 0)
        return carry

    lax.fori_loop(0, PEER_HEADS, phase, 0)

    @pl.when(jnp.logical_not(more))
    def _():
        for u in range(ahead):
            wait_rows(u)


def _peer(tab, h2, x1, mod, w, seq, tile0, ntiles, y_prev=None):
    t = h2.shape[0]
    tb = GATHER_TILE
    per_batch = seq // tb
    row = lambda i: (tile0 + i, 0)
    const = lambda i: (0, 0)
    in_specs = [pl.BlockSpec(memory_space=pl.ANY),
                pl.BlockSpec((tb, D_MODEL), row),
                pl.BlockSpec((tb, D_MODEL), lambda i: (tile0 + jnp.minimum(i + 1, ntiles - 1), 0)),
                pl.BlockSpec((PEER_HEADS, PEER_QDIM, D_MODEL), lambda i: (0, 0, 0)),
                pl.BlockSpec((N_KEYS, PEER_HALF), const),
                pl.BlockSpec((N_KEYS, PEER_HALF), const),
                pl.BlockSpec((tb, D_MODEL), row),
                pl.BlockSpec((1, 6, D_MODEL), lambda i: ((tile0 + i) // per_batch, 0, 0))]
    args = [tab, h2, h2, w["w_pq"], w["sub_keys1"], w["sub_keys2"], x1, mod]
    aliases = {}
    if y_prev is not None:
        in_specs.append(pl.BlockSpec(memory_space=pl.ANY))
        args.append(y_prev)
        aliases = {len(args) - 1: 0}
    return pl.pallas_call(
        functools.partial(_peer_kernel, ntiles=ntiles, aliased=y_prev is not None),
        grid=(ntiles,),
        in_specs=in_specs,
        input_output_aliases=aliases,
        out_specs=pl.BlockSpec((tb, D_MODEL), row),
        out_shape=jax.ShapeDtypeStruct((t, D_MODEL), F32),
        scratch_shapes=[pltpu.SMEM((2, tb, EXPERTS_PER_TOKEN), jnp.int32),
                        pltpu.VMEM((2, EXPERTS_PER_TOKEN, tb), F32),
                        pltpu.VMEM((EXPERTS_PER_TOKEN, tb), F32),
                        pltpu.VMEM((tb, EXPERTS_PER_TOKEN), jnp.int32),
                        pltpu.VMEM((D_MODEL, tb), BF16)]
                       + [pltpu.VMEM((2, EXPERTS_PER_TOKEN * SLAB_ROWS, LANE), F32)] * GATHER_SLOTS
                       + [pltpu.SemaphoreType.DMA((GATHER_SLOTS,)), pltpu.SemaphoreType.DMA((1,))]
                       + _score_scratch(tb),
        compiler_params=pltpu.CompilerParams(dimension_semantics=("arbitrary",),
                                             vmem_limit_bytes=VMEM_LIMIT,
                                             disable_bounds_checks=True),
        name="peer",
    )(*args)


def _score_kernel(h2_ref, wpq_ref, k1_ref, k2_ref, g_ref, e_ref, hb_scr, *score_scr):
    hb_scr[...] = h2_ref[...].T.astype(BF16)

    def head(hd, carry):
        head_rows = pl.ds(pl.multiple_of(hd * PEER_TOPK, PEER_TOPK), PEER_TOPK)
        _score_head(hb_scr[...], wpq_ref[hd], k1_ref[...], k2_ref[...], g_ref.at[head_rows], e_ref.at[head_rows],
                    score_scr)
        return carry

    lax.fori_loop(0, PEER_HEADS, head, 0)


def _peer_score(h2, w, tile0, ntiles):
    tb = GATHER_TILE
    const = lambda i: (0, 0)
    out = pl.BlockSpec((EXPERTS_PER_TOKEN, tb), lambda i: (0, i))
    return pl.pallas_call(
        _score_kernel,
        grid=(ntiles,),
        in_specs=[pl.BlockSpec((tb, D_MODEL), lambda i: (tile0 + i, 0)),
                  pl.BlockSpec((PEER_HEADS, PEER_QDIM, D_MODEL), lambda i: (0, 0, 0)),
                  pl.BlockSpec((N_KEYS, PEER_HALF), const),
                  pl.BlockSpec((N_KEYS, PEER_HALF), const)],
        out_specs=[out, out],
        out_shape=[jax.ShapeDtypeStruct((EXPERTS_PER_TOKEN, ntiles * tb), F32)] * 2,
        scratch_shapes=[pltpu.VMEM((D_MODEL, tb), BF16)] + _score_scratch(tb),
        compiler_params=_params("arbitrary"),
        name="peer_score",
    )(h2, w["w_pq"], w["sub_keys1"], w["sub_keys2"])


def _act_kernel(pre_ref, g_ref, w_ref):
    pre = pre_ref[...]
    w_ref[...] = g_ref[...] * (0.5 * pre * (1.0 + lax.erf(pre * (2.0 ** -0.5))))


def _peer_act(pre, gate):
    t, ne = pre.shape
    tm = 1024
    row = lambda i: (i, 0)
    return pl.pallas_call(
        _act_kernel,
        grid=(t // tm,),
        in_specs=[pl.BlockSpec((tm, ne), row), pl.BlockSpec((tm, ne), row)],
        out_specs=pl.BlockSpec((tm, ne), row),
        out_shape=jax.ShapeDtypeStruct((t, ne), F32),
        compiler_params=_params("arbitrary"),
        name="peer_act",
    )(pre, gate)


def _residual_kernel(x1_ref, out_ref, mod_ref, prev_ref, y_ref):
    del prev_ref
    y_ref[...] = x1_ref[...] + mod_ref[0][5:6] * out_ref[...]


def _peer_residual(x1, out, mod, seq, tile0, y_prev):
    tb = GATHER_TILE
    per_batch = seq // tb
    row = lambda i: (tile0 + i, 0)
    return pl.pallas_call(
        _residual_kernel,
        grid=(out.shape[0] // tb,),
        in_specs=[pl.BlockSpec((tb, D_MODEL), row),
                  pl.BlockSpec((tb, D_MODEL), lambda i: (i, 0)),
                  pl.BlockSpec((1, 6, D_MODEL), lambda i: ((tile0 + i) // per_batch, 0, 0)),
                  pl.BlockSpec(memory_space=pl.ANY)],
        input_output_aliases={3: 0},
        out_specs=pl.BlockSpec((tb, D_MODEL), row),
        out_shape=jax.ShapeDtypeStruct(y_prev.shape, F32),
        compiler_params=_params("arbitrary"),
        name="peer_residual",
    )(x1, out, mod, y_prev)


SC_WORKERS = 32
SC_LANES = 16
SC_CHUNK = 16
SC_SHARE = (7, 16)


def _sc_udot(u_table, idx, x):
    t, ne = idx.shape
    per_w = t // SC_WORKERS
    nchunk = ne // SC_CHUNK
    nl = D_MODEL // SC_LANES
    mesh = plsc.VectorSubcoreMesh(core_axis_name="c", subcore_axis_name="s")

    @functools.partial(
        pl.kernel, out_type=jax.ShapeDtypeStruct((t, ne), F32), mesh=mesh,
        scratch_types=[pltpu.VMEM((ne,), jnp.int32), pltpu.VMEM((D_MODEL,), F32),
                       pltpu.VMEM((2, SC_CHUNK, D_MODEL), F32), pltpu.VMEM((SC_CHUNK, SC_LANES), F32),
                       pltpu.VMEM((ne,), F32), pltpu.SemaphoreType.DMA((2,))],
        compiler_params=pltpu.CompilerParams(needs_layout_passes=False),
        name="sc_udot")
    def run(u_hbm, idx_hbm, x_hbm, pre_hbm, idx_v, x_v, rows, accs, pre_v, sems):
        wid = lax.axis_index("s") * 2 + lax.axis_index("c")
        lanes = lax.iota(jnp.int32, SC_LANES)

        def gather(c, b):
            return pltpu.make_async_copy(u_hbm.at[idx_v.at[pl.ds(c * SC_CHUNK, SC_CHUNK)]], rows.at[b], sems.at[b])

        @pl.loop(0, per_w)
        def _(tt):
            tok = wid * per_w + tt
            pltpu.sync_copy(idx_hbm.at[tok], idx_v)
            pltpu.sync_copy(x_hbm.at[tok], x_v)
            gather(0, 0).start()
            for c in range(nchunk):
                b = c % 2
                if c + 1 < nchunk:
                    gather(c + 1, 1 - b).start()
                gather(c, b).wait()
                for g in range(SC_CHUNK // 4):
                    def body(j, acc):
                        xv = x_v[pl.ds(j * SC_LANES, SC_LANES)]
                        return tuple(acc[q] + rows[b, g * 4 + q, pl.ds(j * SC_LANES, SC_LANES)] * xv
                                     for q in range(4))
                    acc = plsc.parallel_loop(0, nl, unroll=4,
                                             carry=tuple(jnp.zeros((SC_LANES,), F32) for _ in range(4)))(body)
                    for q in range(4):
                        accs[g * 4 + q, :] = acc[q]
                tot = jnp.zeros((SC_LANES,), F32)
                for l in range(SC_LANES):
                    tot = tot + plsc.load_gather(accs, [lanes, jnp.full((SC_LANES,), l, jnp.int32)])
                pre_v[pl.ds(c * SC_CHUNK, SC_CHUNK)] = tot
            pltpu.sync_copy(pre_v, pre_hbm.at[tok])

    return run(u_table, idx, x)


def _sc_vsum(v_table, idx, wgt):
    t, ne = idx.shape
    per_w = t // SC_WORKERS
    nchunk = ne // SC_CHUNK
    block = 16 * SC_LANES
    mesh = plsc.VectorSubcoreMesh(core_axis_name="c", subcore_axis_name="s")

    @functools.partial(
        pl.kernel, out_type=jax.ShapeDtypeStruct((t, D_MODEL), F32), mesh=mesh,
        scratch_types=[pltpu.VMEM((ne,), jnp.int32), pltpu.VMEM((ne,), F32),
                       pltpu.VMEM((2, SC_CHUNK, D_MODEL), F32), pltpu.VMEM((D_MODEL,), F32),
                       pltpu.SemaphoreType.DMA((2,))],
        compiler_params=pltpu.CompilerParams(needs_layout_passes=False),
        name="sc_vsum")
    def run(v_hbm, idx_hbm, w_hbm, out_hbm, idx_v, w_v, rows, out_v, sems):
        wid = lax.axis_index("s") * 2 + lax.axis_index("c")

        def gather(c, b):
            return pltpu.make_async_copy(v_hbm.at[idx_v.at[pl.ds(c * SC_CHUNK, SC_CHUNK)]], rows.at[b], sems.at[b])

        @pl.loop(0, per_w)
        def _(tt):
            tok = wid * per_w + tt
            pltpu.sync_copy(idx_hbm.at[tok], idx_v)
            pltpu.sync_copy(w_hbm.at[tok], w_v)
            gather(0, 0).start()
            for c in range(nchunk):
                b = c % 2
                if c + 1 < nchunk:
                    gather(c + 1, 1 - b).start()
                gather(c, b).wait()
                for d in range(D_MODEL // block):
                    def body(k, acc):
                        wk = plsc.load_gather(w_v, [jnp.full((SC_LANES,), c * SC_CHUNK, jnp.int32) + k])
                        return tuple(acc[j] + rows[b, k, pl.ds(d * block + j * SC_LANES, SC_LANES)] * wk
                                     for j in range(16))
                    if c == 0:
                        init = tuple(jnp.zeros((SC_LANES,), F32) for _ in range(16))
                    else:
                        init = tuple(out_v[pl.ds(d * block + j * SC_LANES, SC_LANES)] for j in range(16))
                    acc = plsc.parallel_loop(0, SC_CHUNK, carry=init)(body)
                    for j in range(16):
                        out_v[pl.ds(d * block + j * SC_LANES, SC_LANES)] = acc[j]
            pltpu.sync_copy(out_v, out_hbm.at[tok])

    return run(v_table, idx, wgt)


def _expert_slabs(u_table, v_table):
    n = u_table.shape[0]
    u = u_table.reshape(n, 1, SLAB_ROWS, LANE)
    v = v_table.reshape(n, 1, SLAB_ROWS, LANE)
    return jnp.concatenate([u, v], axis=1)


def _rope_tables(seq):
    rows = seq // GRID_W
    r = jnp.repeat(jnp.arange(rows, dtype=F32), GRID_W)
    col = jnp.tile(jnp.arange(GRID_W, dtype=F32), rows)
    nf = ROPE_AXIS // 2
    freqs = jnp.power(ROPE_BASE, -jnp.arange(nf, dtype=F32) / nf)
    ar, ac = r[:, None] * freqs, col[:, None] * freqs
    pad = jnp.zeros((seq, LANE - MLA_ROPE), F32)
    cos = jnp.concatenate([jnp.cos(ar), jnp.cos(ar), jnp.cos(ac), jnp.cos(ac), pad], axis=-1)
    sin = jnp.concatenate([-jnp.sin(ar), jnp.sin(ar), -jnp.sin(ac), jnp.sin(ac), pad], axis=-1)
    return cos, sin


def _prep_weights(norm1_w, w_in, q_norm_w, w_uq, kv_norm_w, w_ukv, q_head_norm_w, k_head_norm_w,
                  ret_gn_w, w_out, norm2_w, w_pq, sub_keys1, sub_keys2):
    cut = Q_RANK + KV_RANK + MLA_ROPE
    w_in_p = jnp.concatenate([w_in[:, :cut], jnp.zeros((D_MODEL, LANE - MLA_ROPE), F32), w_in[:, cut:]], axis=1)
    wq = w_uq.reshape(Q_RANK, MLA_HEADS, MLA_QK)
    wq = jnp.pad(wq, ((0, 0), (0, 0), (0, HEAD_PAD - MLA_QK))).reshape(Q_RANK, MLA_HEADS * HEAD_PAD)

    def head_w(v):
        return jnp.pad(v, (0, HEAD_PAD - MLA_QK)).reshape(1, HEAD_PAD)

    return {
        "norm1_w": norm1_w.reshape(1, -1), "w_in": w_in_p.astype(BF16),
        "q_norm_w": q_norm_w.reshape(1, -1), "kv_norm_w": kv_norm_w.reshape(1, -1),
        "w_uq": wq.astype(BF16), "w_ukv": w_ukv.astype(BF16),
        "q_head_norm_w": head_w(q_head_norm_w), "k_head_norm_w": head_w(k_head_norm_w),
        "ret_gn_w": ret_gn_w.reshape(1, -1), "w_out": w_out.astype(BF16), "norm2_w": norm2_w.reshape(1, -1),
        "w_pq": w_pq.astype(BF16).reshape(D_MODEL, PEER_HEADS, PEER_QDIM).transpose(1, 2, 0), "sub_keys1": sub_keys1.astype(BF16), "sub_keys2": sub_keys2.astype(BF16),
    }


def _trunk(x, mod, w, decay_logit, tab, ctx):
    batch, seq, _ = x.shape
    x2 = x.reshape(batch * seq, D_MODEL)
    use_rope = ctx is not None
    if use_rope:
        cos, sin = _rope_tables(seq)
    else:
        cos = sin = jnp.zeros((ROW_TILE, LANE), F32)
    ckv, krope, q, k, v, zr = _in_proj(x2, mod, w, cos, sin, seq, use_rope)
    if use_rope:
        ckv_c, krope_c, state_c = ctx
        past = ckv_c.shape[1]
        kr_c = jnp.pad(krope_c.reshape(batch * past, MLA_ROPE), ((0, 0), (0, LANE - MLA_ROPE)))
        ctx_kv = _kv_up(ckv_c.reshape(batch * past, KV_RANK), kr_c, w)
        s0 = state_c.reshape(batch * 2, RET_HEADS, RET_DK, RET_DV)
    else:
        ctx_kv, s0 = None, None
    att = _attention(q, k, v, batch, seq, ctx_kv)
    o_f, o_b, s_f, s_b = _retention(zr, decay_logit, batch, seq, s0)
    x1, h2 = _mix(x2, att, o_f, o_b, zr, mod, w, seq)
    ntiles = (batch * seq) // GATHER_TILE
    sc_tiles = ntiles * SC_SHARE[0] // SC_SHARE[1] if use_rope else 0
    if sc_tiles == 0:
        y = _peer(tab, h2, x1, mod, w, seq, 0, ntiles)
    else:
        tc_tiles = ntiles - sc_tiles
        first = tc_tiles // 2
        gt, et = _peer_score(h2, w, tc_tiles, sc_tiles)
        idx = et.T.astype(jnp.int32)
        pre = _sc_udot(w["u_table"], idx, h2[tc_tiles * GATHER_TILE:])
        y = _peer(tab, h2, x1, mod, w, seq, 0, first)
        out = _sc_vsum(w["v_table"], idx, _peer_act(pre, gt.T))
        y = _peer(tab, h2, x1, mod, w, seq, first, tc_tiles - first, y_prev=y)
        y = _peer_residual(x1, out, mod, seq, tc_tiles, y)
    states = jnp.stack([s_f, s_b], axis=1)
    return y.reshape(batch, seq, D_MODEL), ckv.reshape(batch, seq, KV_RANK), krope.reshape(batch, seq, MLA_ROPE), states


def kernel(x_prompt, x_sample, c, cache_ckv, cache_krope, state_ret, c_ctx, w_ada, b_ada, norm1_w, w_in,
           q_norm_w, w_uq, kv_norm_w, w_ukv, q_head_norm_w, k_head_norm_w, ret_decay_logit, ret_gn_w,
           w_out, norm2_w, w_pq, sub_keys1, sub_keys2, u_table, v_table):
    depth = w_ada.shape[0]
    nb_ctx = x_prompt.shape[0]
    nb_lat = x_sample.shape[0]
    y_prompt, y_sample = x_prompt, x_sample
    ckv_list, krope_list, ret_list = [], [], []
    for l in range(depth):
        cond_rows = -(-(nb_lat + 1) // 8) * 8
        cond = jnp.concatenate([c, c_ctx[None, :], jnp.zeros((cond_rows - nb_lat - 1, D_MODEL), F32)], axis=0)
        mod = _ada(cond, w_ada[l], b_ada[l])
        mod_lat = mod[:nb_lat].reshape(nb_lat, 6, D_MODEL)
        mod_ctx = jnp.broadcast_to(mod[nb_lat].reshape(1, 6, D_MODEL), (nb_ctx, 6, D_MODEL))
        w = _prep_weights(norm1_w[l], w_in[l], q_norm_w[l], w_uq[l], kv_norm_w[l], w_ukv[l], q_head_norm_w[l],
                          k_head_norm_w[l], ret_gn_w[l], w_out[l], norm2_w[l], w_pq[l], sub_keys1[l], sub_keys2[l])
        tab = _expert_slabs(u_table[l], v_table[l])
        w["u_table"], w["v_table"] = u_table[l], v_table[l]
        y_prompt, ckv_l, krope_l, ret_l = _trunk(y_prompt, mod_ctx, w, ret_decay_logit[l], tab, None)
        ckv_list.append(ckv_l)
        krope_list.append(krope_l)
        ret_list.append(ret_l)
        y_sample, _, _, _ = _trunk(y_sample, mod_lat, w, ret_decay_logit[l], tab,
                                   (cache_ckv[:, l], cache_krope[:, l], state_ret[:, l]))
    return (y_prompt, y_sample, jnp.stack(ckv_list, axis=1), jnp.stack(krope_list, axis=1),
            jnp.stack(ret_list, axis=1))
```

```python
import functools

import jax
import jax.numpy as jnp
from jax import lax
from jax.experimental import pallas as pl
from jax.experimental.pallas import tpu as pltpu
from jax.experimental.pallas import tpu_sc as plsc

F32 = jnp.float32
BF16 = jnp.bfloat16

D_MODEL = 1024
GRID_W = 64
MLA_HEADS = 4
MLA_NOPE = 128
MLA_ROPE = 64
MLA_QK = MLA_NOPE + MLA_ROPE
MLA_V = 128
Q_RANK = 512
KV_RANK = 256
ROPE_AXIS = MLA_ROPE // 2
ROPE_BASE = 10000.0
RET_HEADS = 4
RET_DK = 128
RET_DV = 128
RET_CHUNK = 128
PEER_HEADS = 8
PEER_QDIM = 256
PEER_HALF = PEER_QDIM // 2
N_KEYS = 128
PEER_TOPK = 16
EPS = 1e-6

LANE = 128
HEAD_PAD = 2 * LANE
RET_W = 4 * RET_HEADS * RET_DK
IN_PAD = Q_RANK + KV_RANK + LANE + RET_W
VMEM_LIMIT = 48 * 1024 * 1024

ROW_TILE = 256
Q_TILE = 256
GATHER_TILE = 128
GATHER_SLOTS = 4
GATHER_AHEAD = GATHER_SLOTS - 1
EXPERTS_PER_TOKEN = PEER_HEADS * PEER_TOPK
SLAB_ROWS = D_MODEL // LANE


def _params(*sem):
    return pltpu.CompilerParams(dimension_semantics=sem, vmem_limit_bytes=VMEM_LIMIT)


def _rms(x, w):
    return x * lax.rsqrt(jnp.mean(x * x, axis=-1, keepdims=True) + EPS) * w


def _mm(a, b):
    return jnp.dot(a.astype(BF16), b.astype(BF16), preferred_element_type=F32)


def _mm_nt(a, b):
    return lax.dot_general(a.astype(BF16), b.astype(BF16), (((1,), (1,)), ((), ())),
                           preferred_element_type=F32)


def _mm_tn(a, b):
    return lax.dot_general(a.astype(BF16), b.astype(BF16), (((0,), (0,)), ((), ())),
                           preferred_element_type=F32)


def _ada_kernel(c_ref, w_ref, b_ref, o_ref):
    c = c_ref[...]
    o_ref[...] = _mm(c * jax.nn.sigmoid(c), w_ref[...]) + b_ref[...]


def _ada(cond, w_ada, b_ada):
    rows, d = cond.shape
    n = w_ada.shape[1]
    tn = 1536
    return pl.pallas_call(
        _ada_kernel,
        grid=(n // tn,),
        in_specs=[pl.BlockSpec((rows, d), lambda j: (0, 0)),
                  pl.BlockSpec((d, tn), lambda j: (0, j)),
                  pl.BlockSpec((1, tn), lambda j: (0, j))],
        out_specs=pl.BlockSpec((rows, tn), lambda j: (0, j)),
        out_shape=jax.ShapeDtypeStruct((rows, n), F32),
        compiler_params=_params("arbitrary"),
        name="ada",
    )(cond, w_ada, b_ada.reshape(1, n))


def _rope_tile(x, cos, sin):
    lane = lax.broadcasted_iota(jnp.int32, x.shape, 1)
    partner = jnp.where((lane % 32) < 16, pltpu.roll(x, LANE - 16, 1), pltpu.roll(x, 16, 1))
    return x * cos + partner * sin


def _kv_heads(kv, kr, khw, cos, sin, use_rope):
    krw = kr * khw[:, LANE:]
    if use_rope:
        krw = _rope_tile(krw, cos, sin)
    ssq_r = jnp.sum(kr * kr, axis=-1, keepdims=True)
    ks, vs = [], []
    for hd in range(MLA_HEADS):
        kn = kv[:, hd * HEAD_PAD: hd * HEAD_PAD + LANE]
        r = lax.rsqrt((jnp.sum(kn * kn, axis=-1, keepdims=True) + ssq_r) / MLA_QK + EPS)
        ks += [kn * r * khw[:, :LANE], krw * r]
        vs.append(kv[:, hd * HEAD_PAD + LANE: (hd + 1) * HEAD_PAD])
    return jnp.concatenate(ks, axis=-1).astype(BF16), jnp.concatenate(vs, axis=-1).astype(BF16)


def _inproj_kernel(x_ref, mod_ref, n1w_ref, win_ref, qnw_ref, kvnw_ref, wuq_ref, wukv_ref, qhw_ref, khw_ref,
                   cos_ref, sin_ref, ckv_ref, krope_ref, q_ref, k_ref, v_ref, zr_ref, *, use_rope):
    m = mod_ref[0]
    h = _rms(x_ref[...], n1w_ref[...]) * (1.0 + m[1:2]) + m[0:1]
    z = _mm(h, win_ref[...])
    kr = z[:, Q_RANK + KV_RANK: Q_RANK + KV_RANK + LANE]
    zr_ref[...] = z[:, Q_RANK + KV_RANK + LANE:]
    ckvn = _rms(z[:, Q_RANK: Q_RANK + KV_RANK], kvnw_ref[...])
    ckv_ref[...] = ckvn
    krope_ref[...] = kr[:, :MLA_ROPE]
    cos, sin = cos_ref[...], sin_ref[...]

    q = _mm(_rms(z[:, :Q_RANK], qnw_ref[...]), wuq_ref[...])
    qhw = qhw_ref[...]
    qs = []
    for hd in range(MLA_HEADS):
        qn = q[:, hd * HEAD_PAD: hd * HEAD_PAD + LANE]
        qr = q[:, hd * HEAD_PAD + LANE: (hd + 1) * HEAD_PAD]
        ssq = jnp.sum(qn * qn, axis=-1, keepdims=True) + jnp.sum(qr * qr, axis=-1, keepdims=True)
        r = lax.rsqrt(ssq / MLA_QK + EPS)
        qrw = qr * qhw[:, LANE:]
        if use_rope:
            qrw = _rope_tile(qrw, cos, sin)
        qs += [qn * r * qhw[:, :LANE], qrw * r]
    q_ref[...] = jnp.concatenate(qs, axis=-1).astype(BF16)

    k, v = _kv_heads(_mm(ckvn, wukv_ref[...]), kr, khw_ref[...], cos, sin, use_rope)
    k_ref[...] = k
    v_ref[...] = v


def _in_proj(x, mod, w, cos, sin, seq, use_rope):
    t = x.shape[0]
    tm = ROW_TILE
    per_batch = seq // tm
    const = lambda i: (0, 0)
    row = lambda i: (i, 0)
    if use_rope:
        pos = lambda i: (i % per_batch, 0)
    else:
        pos = const
    hp = MLA_HEADS * HEAD_PAD
    return pl.pallas_call(
        functools.partial(_inproj_kernel, use_rope=use_rope),
        grid=(t // tm,),
        in_specs=[pl.BlockSpec((tm, D_MODEL), row),
                  pl.BlockSpec((1, 6, D_MODEL), lambda i: (i // per_batch, 0, 0)),
                  pl.BlockSpec((1, D_MODEL), const),
                  pl.BlockSpec((D_MODEL, IN_PAD), const),
                  pl.BlockSpec((1, Q_RANK), const),
                  pl.BlockSpec((1, KV_RANK), const),
                  pl.BlockSpec((Q_RANK, hp), const),
                  pl.BlockSpec((KV_RANK, hp), const),
                  pl.BlockSpec((1, HEAD_PAD), const),
                  pl.BlockSpec((1, HEAD_PAD), const),
                  pl.BlockSpec((tm, LANE), pos),
                  pl.BlockSpec((tm, LANE), pos)],
        out_specs=[pl.BlockSpec((tm, KV_RANK), row),
                   pl.BlockSpec((tm, MLA_ROPE), row),
                   pl.BlockSpec((tm, hp), row),
                   pl.BlockSpec((tm, hp), row),
                   pl.BlockSpec((tm, MLA_HEADS * MLA_V), row),
                   pl.BlockSpec((tm, RET_W), row)],
        out_shape=[jax.ShapeDtypeStruct((t, KV_RANK), F32),
                   jax.ShapeDtypeStruct((t, MLA_ROPE), F32),
                   jax.ShapeDtypeStruct((t, hp), BF16),
                   jax.ShapeDtypeStruct((t, hp), BF16),
                   jax.ShapeDtypeStruct((t, MLA_HEADS * MLA_V), BF16),
                   jax.ShapeDtypeStruct((t, RET_W), F32)],
        compiler_params=_params("arbitrary"),
        name="in_proj",
    )(x, mod, w["norm1_w"], w["w_in"], w["q_norm_w"], w["kv_norm_w"], w["w_uq"], w["w_ukv"],
      w["q_head_norm_w"], w["k_head_norm_w"], cos, sin)


def _kvup_kernel(ckv_ref, kr_ref, wukv_ref, khw_ref, k_ref, v_ref):
    k, v = _kv_heads(_mm(ckv_ref[...], wukv_ref[...]), kr_ref[...], khw_ref[...], None, None, False)
    k_ref[...] = k
    v_ref[...] = v


def _kv_up(ckv, kr, w):
    t = ckv.shape[0]
    tm = ROW_TILE
    hp = MLA_HEADS * HEAD_PAD
    const = lambda i: (0, 0)
    row = lambda i: (i, 0)
    return pl.pallas_call(
        _kvup_kernel,
        grid=(t // tm,),
        in_specs=[pl.BlockSpec((tm, KV_RANK), row),
                  pl.BlockSpec((tm, LANE), row),
                  pl.BlockSpec((KV_RANK, hp), const),
                  pl.BlockSpec((1, HEAD_PAD), const)],
        out_specs=[pl.BlockSpec((tm, hp), row),
                   pl.BlockSpec((tm, MLA_HEADS * MLA_V), row)],
        out_shape=[jax.ShapeDtypeStruct((t, hp), BF16),
                   jax.ShapeDtypeStruct((t, MLA_HEADS * MLA_V), BF16)],
        compiler_params=_params("arbitrary"),
        name="kv_up",
    )(ckv, kr, w["w_ukv"], w["k_head_norm_w"])


def _attn_kernel(*refs, has_ctx):
    if has_ctx:
        q_ref, k_ref, v_ref, kc_ref, vc_ref, o_ref = refs
    else:
        q_ref, k_ref, v_ref, o_ref = refs
    scale = MLA_QK ** -0.5
    q = q_ref[...]
    s = _mm_nt(q, k_ref[...]) * scale
    m = jnp.max(s, axis=-1, keepdims=True)
    if has_ctx:
        sc = _mm_nt(q, kc_ref[...]) * scale
        m = jnp.maximum(m, jnp.max(sc, axis=-1, keepdims=True))
    p = jnp.exp(s - m)
    den = jnp.sum(p, axis=-1, keepdims=True)
    o = _mm(p, v_ref[...])
    if has_ctx:
        pc = jnp.exp(sc - m)
        den = den + jnp.sum(pc, axis=-1, keepdims=True)
        o = o + _mm(pc, vc_ref[...])
    o_ref[...] = (o / den).astype(BF16)


def _attention(q, k, v, batch, seq, ctx_kv):
    tq = min(Q_TILE, seq)
    nq = seq // tq
    has_ctx = ctx_kv is not None
    qmap = lambda b, h, i: (b * nq + i, h)
    kvmap = lambda b, h, i: (b, h)
    in_specs = [pl.BlockSpec((tq, HEAD_PAD), qmap),
                pl.BlockSpec((seq, HEAD_PAD), kvmap),
                pl.BlockSpec((seq, MLA_V), kvmap)]
    args = [q, k, v]
    if has_ctx:
        kc, vc = ctx_kv
        past = kc.shape[0] // batch
        in_specs += [pl.BlockSpec((past, HEAD_PAD), kvmap), pl.BlockSpec((past, MLA_V), kvmap)]
        args += [kc, vc]
    return pl.pallas_call(
        functools.partial(_attn_kernel, has_ctx=has_ctx),
        grid=(batch, MLA_HEADS, nq),
        in_specs=in_specs,
        out_specs=pl.BlockSpec((tq, MLA_V), qmap),
        out_shape=jax.ShapeDtypeStruct((batch * seq, MLA_HEADS * MLA_V), BF16),
        compiler_params=_params("arbitrary", "arbitrary", "arbitrary"),
        name="attention",
    )(*args)


def _ret_kernel(*refs, has_s0):
    if has_s0:
        (lg_ref, qf_ref, kf_ref, vf_ref, qb_ref, kb_ref, vb_ref, s0f_ref, s0b_ref,
         of_ref, ob_ref, sf_out, sb_out, sf_scr, sb_scr) = refs
    else:
        (lg_ref, qf_ref, kf_ref, vf_ref, qb_ref, kb_ref, vb_ref,
         of_ref, ob_ref, sf_out, sb_out, sf_scr, sb_scr) = refs
    c = pl.program_id(1)
    cs = RET_CHUNK

    @pl.when(c == 0)
    def _():
        if has_s0:
            sf_scr[...] = s0f_ref[0]
            sb_scr[...] = s0b_ref[0]
        else:
            sf_scr[...] = jnp.zeros_like(sf_scr)
            sb_scr[...] = jnp.zeros_like(sb_scr)

    ii = lax.broadcasted_iota(jnp.int32, (cs, cs), 0).astype(F32)
    jj = lax.broadcasted_iota(jnp.int32, (cs, cs), 1).astype(F32)
    rel = ii - jj
    kscale = RET_DK ** -0.5

    def chunk(q, k, v, s, intra, qdec, kdec, cdec):
        a = _mm_nt(q, k) * intra
        o = _mm(a, v) + _mm(q, s) * qdec
        return o, s * cdec + _mm_tn(k * kdec, v)

    for hd in range(RET_HEADS):
        cols = slice(hd * RET_DK, (hd + 1) * RET_DK)
        lgf = jax.nn.log_sigmoid(jnp.full((1, LANE), lg_ref[0, hd], F32))
        lgb = jax.nn.log_sigmoid(jnp.full((1, LANE), lg_ref[1, hd], F32))

        intra_f = jnp.where(rel >= 0, jnp.exp(jnp.maximum(rel, 0.0) * lgf), 0.0)
        o_f, s_f = chunk(qf_ref[:, cols], kf_ref[:, cols] * kscale, vf_ref[:, cols], sf_scr[hd], intra_f,
                         jnp.exp((ii + 1.0) * lgf), jnp.exp((cs - 1.0 - ii) * lgf), jnp.exp(cs * lgf))
        of_ref[:, cols] = o_f
        sf_scr[hd] = s_f
        sf_out[0, hd] = s_f

        intra_b = jnp.where(rel <= 0, jnp.exp(jnp.maximum(-rel, 0.0) * lgb), 0.0)
        o_b, s_b = chunk(qb_ref[:, cols], kb_ref[:, cols] * kscale, vb_ref[:, cols], sb_scr[hd], intra_b,
                         jnp.exp((cs - ii) * lgb), jnp.exp(ii * lgb), jnp.exp(cs * lgb))
        ob_ref[:, cols] = o_b
        sb_scr[hd] = s_b
        sb_out[0, hd] = s_b


def _retention(zr, decay_logit, batch, seq, s0):
    cs = RET_CHUNK
    nc = seq // cs
    nh = RET_HEADS
    width = nh * RET_DK
    has_s0 = s0 is not None

    def fwd(col):
        return pl.BlockSpec((cs, width), lambda b, c: (b * nc + c, col))

    def bwd(col):
        return pl.BlockSpec((cs, width), lambda b, c: (b * nc + nc - 1 - c, col))

    def state(d):
        return pl.BlockSpec((1, nh, RET_DK, RET_DV), lambda b, c: (b * 2 + d, 0, 0, 0))

    state_out = pl.BlockSpec((1, nh, RET_DK, RET_DV), lambda b, c: (b, 0, 0, 0))
    in_specs = [pl.BlockSpec(memory_space=pltpu.SMEM), fwd(0), fwd(1), fwd(2), bwd(0), bwd(1), bwd(2)]
    args = [decay_logit, zr, zr, zr, zr, zr, zr]
    if has_s0:
        in_specs += [state(0), state(1)]
        args += [s0, s0]
    t = batch * seq
    o_f, o_b, s_f, s_b = pl.pallas_call(
        functools.partial(_ret_kernel, has_s0=has_s0),
        grid=(batch, nc),
        in_specs=in_specs,
        out_specs=[pl.BlockSpec((cs, width), lambda b, c: (b * nc + c, 0)),
                   pl.BlockSpec((cs, width), lambda b, c: (b * nc + nc - 1 - c, 0)),
                   state_out, state_out],
        out_shape=[jax.ShapeDtypeStruct((t, width), F32),
                   jax.ShapeDtypeStruct((t, width), F32),
                   jax.ShapeDtypeStruct((batch, nh, RET_DK, RET_DV), F32),
                   jax.ShapeDtypeStruct((batch, nh, RET_DK, RET_DV), F32)],
        scratch_shapes=[pltpu.VMEM((nh, RET_DK, RET_DV), F32), pltpu.VMEM((nh, RET_DK, RET_DV), F32)],
        compiler_params=_params("arbitrary", "arbitrary"),
        name="retention",
    )(*args)
    return o_f, o_b, s_f, s_b


def _mix_kernel(x_ref, att_ref, of_ref, ob_ref, gr_ref, mod_ref, gnw_ref, wout_ref, n2w_ref, x1_ref, h2_ref):
    o = of_ref[...] + ob_ref[...]
    parts = []
    for hd in range(RET_HEADS):
        oh = o[:, hd * RET_DV:(hd + 1) * RET_DV]
        d = oh - jnp.mean(oh, axis=-1, keepdims=True)
        parts.append(d * lax.rsqrt(jnp.mean(d * d, axis=-1, keepdims=True) + EPS))
    g = gr_ref[...]
    ret = (g * jax.nn.sigmoid(g)) * (jnp.concatenate(parts, axis=-1) * gnw_ref[...])
    na = MLA_HEADS * MLA_V
    mixed = _mm(att_ref[...], wout_ref[:na, :]) + _mm(ret, wout_ref[na:, :])
    m = mod_ref[0]
    x1 = x_ref[...] + m[2:3] * mixed
    x1_ref[...] = x1
    h2_ref[...] = _rms(x1, n2w_ref[...]) * (1.0 + m[4:5]) + m[3:4]


def _mix(x, att, o_f, o_b, zr, mod, w, seq):
    t = x.shape[0]
    tm = ROW_TILE
    per_batch = seq // tm
    const = lambda i: (0, 0)
    row = lambda i: (i, 0)
    half = RET_HEADS * RET_DV
    return pl.pallas_call(
        _mix_kernel,
        grid=(t // tm,),
        in_specs=[pl.BlockSpec((tm, D_MODEL), row),
                  pl.BlockSpec((tm, half), row),
                  pl.BlockSpec((tm, half), row),
                  pl.BlockSpec((tm, half), row),
                  pl.BlockSpec((tm, half), lambda i: (i, 3)),
                  pl.BlockSpec((1, 6, D_MODEL), lambda i: (i // per_batch, 0, 0)),
                  pl.BlockSpec((1, half), const),
                  pl.BlockSpec((D_MODEL, D_MODEL), const),
                  pl.BlockSpec((1, D_MODEL), const)],
        out_specs=[pl.BlockSpec((tm, D_MODEL), row), pl.BlockSpec((tm, D_MODEL), row)],
        out_shape=[jax.ShapeDtypeStruct((t, D_MODEL), F32), jax.ShapeDtypeStruct((t, D_MODEL), F32)],
        compiler_params=_params("arbitrary"),
        name="mix",
    )(x, att, o_f, o_b, zr, mod, w["ret_gn_w"], w["w_out"], w["norm2_w"])


def _top16(jobs):
    def body(r, carry):
        for s_ref, vals_ref, pay_ref, payload in jobs:
            n, cols = s_ref.shape
            rows = lax.broadcasted_iota(jnp.int32, (n, cols), 0).astype(F32)
            s = s_ref[...]
            m = jnp.max(s, axis=0, keepdims=True)
            pos = jnp.min(jnp.where(s == m, rows, float(n)), axis=0, keepdims=True)
            hit = rows == pos
            vals_ref[pl.ds(r, 1), :] = m
            if payload is None:
                pay_ref[pl.ds(r, 1), :] = pos
            else:
                pay_ref[pl.ds(r, 1), :] = jnp.sum(jnp.where(hit, payload, 0.0), axis=0, keepdims=True)
            s_ref[...] = jnp.where(hit, -jnp.inf, s)
        return carry

    lax.fori_loop(0, PEER_TOPK, body, 0)


PAIR_COUNTS = tuple(PEER_TOPK // (a + 1) for a in range(PEER_TOPK))
NUM_PAIRS = sum(PAIR_COUNTS)
PAIR_ROWS = -(-NUM_PAIRS // 8) * 8


def _score_head(hbt, wpqt, k1, k2, g_out, e_out, scr):
    s1_scr, s2_scr, c_scr, p_scr, v1_scr, i1_scr, v2_scr, i2_scr, vt_scr, it_scr = scr
    qt = jnp.dot(wpqt, hbt, preferred_element_type=F32)
    s1_scr[...] = _mm(k1, qt[:PEER_HALF, :])
    s2_scr[...] = _mm(k2, qt[PEER_HALF:, :])
    _top16([(s1_scr, v1_scr, i1_scr, None), (s2_scr, v2_scr, i2_scr, None)])
    c_scr[...] = jnp.full(c_scr.shape, -jnp.inf, F32)
    p_scr[...] = jnp.zeros(p_scr.shape, F32)
    off = 0
    for a, nb in enumerate(PAIR_COUNTS):
        c_scr[off:off + nb, :] = v1_scr[a:a + 1, :] + v2_scr[0:nb, :]
        p_scr[off:off + nb, :] = i1_scr[a:a + 1, :] * float(N_KEYS) + i2_scr[0:nb, :]
        off += nb
    _top16([(c_scr, vt_scr, it_scr, p_scr[...])])
    top = vt_scr[...]
    p = jnp.exp(top - jnp.max(top, axis=0, keepdims=True))
    g_out[...] = p / jnp.sum(p, axis=0, keepdims=True)
    e_out[...] = it_scr[...]


def _score_scratch(tb):
    k = PEER_TOPK
    return ([pltpu.VMEM((N_KEYS, tb), F32)] * 2 + [pltpu.VMEM((PAIR_ROWS, tb), F32)] * 2
            + [pltpu.VMEM((k, tb), F32)] * 6)


def _peer_kernel(tab_hbm, h2_ref, h2n_ref, wpq_ref, k1_ref, k2_ref, x1_ref, mod_ref, *rest, ntiles, aliased):
    y_ref, idx_smem, g_scr, e_scr, ei_scr, hb_scr = rest[int(aliased):int(aliased) + 6]
    scratch = rest[int(aliased) + 6:]
    _peer_body(tab_hbm, h2_ref, h2n_ref, wpq_ref, k1_ref, k2_ref, x1_ref, mod_ref, y_ref,
               idx_smem, g_scr, e_scr, ei_scr, hb_scr, *scratch, ntiles=ntiles)


def _peer_body(tab_hbm, h2_ref, h2n_ref, wpq_ref, k1_ref, k2_ref, x1_ref, mod_ref, y_ref,
               idx_smem, g_scr, e_scr, ei_scr, hb_scr, *scratch, ntiles):
    bufs = scratch[:GATHER_SLOTS]
    sem_rows, sem_idx = scratch[GATHER_SLOTS:GATHER_SLOTS + 2]
    score_scr = scratch[GATHER_SLOTS + 2:]
    i = pl.program_id(0)
    tb = GATHER_TILE
    ns = GATHER_SLOTS
    ahead = GATHER_AHEAD
    ne = EXPERTS_PER_TOKEN
    slab = SLAB_ROWS
    topk = PEER_TOPK
    groups_per_head = tb // (ns * PEER_HEADS)
    cur = i % 2
    nxt = 1 - cur
    more = i + 1 < ntiles

    def score(src_ref, hd, slot):
        @pl.when(hd == 0)
        def _():
            hb_scr[...] = src_ref[...].T.astype(BF16)
        head_rows = pl.ds(pl.multiple_of(hd * topk, topk), topk)
        _score_head(hb_scr[...], wpq_ref[hd], k1_ref[...], k2_ref[...],
                    g_scr.at[slot, head_rows], e_scr.at[head_rows], score_scr)

    def publish(slot):
        ei_scr[...] = e_scr[...].T.astype(jnp.int32)
        copy = pltpu.make_async_copy(ei_scr, idx_smem.at[slot], sem_idx.at[0])
        copy.start()
        copy.wait()

    def issue(islot, tok, rslot):
        for kk in range(ne):
            ex = idx_smem[islot, tok, kk]
            pltpu.make_async_copy(tab_hbm.at[ex], bufs[rslot].at[:, pl.ds(kk * slab, slab), :],
                                  sem_rows.at[rslot]).start(priority=kk % 2)

    def wait_rows(rslot):
        pltpu.make_async_copy(bufs[rslot], bufs[rslot], sem_rows.at[rslot]).wait()

    @pl.when(i == 0)
    def _():
        def first(hd, carry):
            score(h2_ref, hd, 0)
            return carry
        lax.fori_loop(0, PEER_HEADS, first, 0)
        publish(0)
        for tok in range(ahead):
            issue(0, tok, tok)

    g2 = mod_ref[0][5:6]
    lane = lax.broadcasted_iota(jnp.int32, (ne, tb), 1)

    def compute(j, rslot):
        rows = bufs[rslot]
        x_row = h2_ref[pl.ds(j, 1), :]
        acc = None
        for c in range(slab):
            u_c = rows[0, pl.ds(c, ne, stride=slab), :]
            term = u_c * x_row[:, c * LANE:(c + 1) * LANE]
            acc = term if acc is None else acc + term
        pre = jnp.sum(acc, axis=-1, keepdims=True)
        act = 0.5 * pre * (1.0 + lax.erf(pre * (2.0 ** -0.5)))
        gate = jnp.sum(jnp.where(lane == j, g_scr[cur], 0.0), axis=-1, keepdims=True)
        wgt = gate * act
        outs = []
        for c in range(slab):
            v_c = rows[1, pl.ds(c, ne, stride=slab), :]
            outs.append(jnp.sum(v_c * wgt, axis=0, keepdims=True))
        out = jnp.concatenate(outs, axis=-1)
        y_ref[pl.ds(j, 1), :] = x1_ref[pl.ds(j, 1), :] + g2 * out

    def phase(hd, carry):
        score(h2n_ref, hd, nxt)

        @pl.when(hd == PEER_HEADS - 1)
        def _():
            publish(nxt)

        def group(gq, inner):
            g = hd * groups_per_head + gq
            for u in range(ns):
                j = g * ns + u
                wait_rows(u)
                jj = j + ahead
                over = jj >= tb
                issue(jnp.where(over, nxt, cur), jnp.where(over, jj - tb, jj), (u + ahead) % ns)
                compute(j, u)
            return inner

        lax.fori_loop(0, groups_per_head, group, 0)
        return carry

    lax.fori_loop(0, PEER_HEADS, phase, 0)

    @pl.when(jnp.logical_not(more))
    def _():
        for u in range(ahead):
            wait_rows(u)


def _peer(tab, h2, x1, mod, w, seq, tile0, ntiles, y_prev=None):
    t = h2.shape[0]
    tb = GATHER_TILE
    per_batch = seq // tb
    row = lambda i: (tile0 + i, 0)
    const = lambda i: (0, 0)
    in_specs = [pl.BlockSpec(memory_space=pl.ANY),
                pl.BlockSpec((tb, D_MODEL), row),
                pl.BlockSpec((tb, D_MODEL), lambda i: (tile0 + jnp.minimum(i + 1, ntiles - 1), 0)),
                pl.BlockSpec((PEER_HEADS, PEER_QDIM, D_MODEL), lambda i: (0, 0, 0)),
                pl.BlockSpec((N_KEYS, PEER_HALF), const),
                pl.BlockSpec((N_KEYS, PEER_HALF), const),
                pl.BlockSpec((tb, D_MODEL), row),
                pl.BlockSpec((1, 6, D_MODEL), lambda i: ((tile0 + i) // per_batch, 0, 0))]
    args = [tab, h2, h2, w["w_pq"], w["sub_keys1"], w["sub_keys2"], x1, mod]
    aliases = {}
    if y_prev is not None:
        in_specs.append(pl.BlockSpec(memory_space=pl.ANY))
        args.append(y_prev)
        aliases = {len(args) - 1: 0}
    return pl.pallas_call(
        functools.partial(_peer_kernel, ntiles=ntiles, aliased=y_prev is not None),
        grid=(ntiles,),
        in_specs=in_specs,
        input_output_aliases=aliases,
        out_specs=pl.BlockSpec((tb, D_MODEL), row),
        out_shape=jax.ShapeDtypeStruct((t, D_MODEL), F32),
        scratch_shapes=[pltpu.SMEM((2, tb, EXPERTS_PER_TOKEN), jnp.int32),
                        pltpu.VMEM((2, EXPERTS_PER_TOKEN, tb), F32),
                        pltpu.VMEM((EXPERTS_PER_TOKEN, tb), F32),
                        pltpu.VMEM((tb, EXPERTS_PER_TOKEN), jnp.int32),
                        pltpu.VMEM((D_MODEL, tb), BF16)]
                       + [pltpu.VMEM((2, EXPERTS_PER_TOKEN * SLAB_ROWS, LANE), F32)] * GATHER_SLOTS
                       + [pltpu.SemaphoreType.DMA((GATHER_SLOTS,)), pltpu.SemaphoreType.DMA((1,))]
                       + _score_scratch(tb),
        compiler_params=pltpu.CompilerParams(dimension_semantics=("arbitrary",),
                                             vmem_limit_bytes=VMEM_LIMIT,
                                             disable_bounds_checks=True),
        name="peer",
    )(*args)


def _score_kernel(h2_ref, wpq_ref, k1_ref, k2_ref, g_ref, e_ref, hb_scr, *score_scr):
    hb_scr[...] = h2_ref[...].T.astype(BF16)

    def head(hd, carry):
        head_rows = pl.ds(pl.multiple_of(hd * PEER_TOPK, PEER_TOPK), PEER_TOPK)
        _score_head(hb_scr[...], wpq_ref[hd], k1_ref[...], k2_ref[...], g_ref.at[head_rows], e_ref.at[head_rows],
                    score_scr)
        return carry

    lax.fori_loop(0, PEER_HEADS, head, 0)


def _peer_score(h2, w, tile0, ntiles):
    tb = GATHER_TILE
    const = lambda i: (0, 0)
    out = pl.BlockSpec((EXPERTS_PER_TOKEN, tb), lambda i: (0, i))
    return pl.pallas_call(
        _score_kernel,
        grid=(ntiles,),
        in_specs=[pl.BlockSpec((tb, D_MODEL), lambda i: (tile0 + i, 0)),
                  pl.BlockSpec((PEER_HEADS, PEER_QDIM, D_MODEL), lambda i: (0, 0, 0)),
                  pl.BlockSpec((N_KEYS, PEER_HALF), const),
                  pl.BlockSpec((N_KEYS, PEER_HALF), const)],
        out_specs=[out, out],
        out_shape=[jax.ShapeDtypeStruct((EXPERTS_PER_TOKEN, ntiles * tb), F32)] * 2,
        scratch_shapes=[pltpu.VMEM((D_MODEL, tb), BF16)] + _score_scratch(tb),
        compiler_params=_params("arbitrary"),
        name="peer_score",
    )(h2, w["w_pq"], w["sub_keys1"], w["sub_keys2"])


def _act_kernel(pre_ref, g_ref, after_ref, w_ref):
    del after_ref
    pre = pre_ref[...]
    w_ref[...] = g_ref[...] * (0.5 * pre * (1.0 + lax.erf(pre * (2.0 ** -0.5))))


def _peer_act(pre, gate, after):
    t, ne = pre.shape
    tm = 1024
    row = lambda i: (i, 0)
    return pl.pallas_call(
        _act_kernel,
        grid=(t // tm,),
        in_specs=[pl.BlockSpec((tm, ne), row), pl.BlockSpec((tm, ne), row), pl.BlockSpec(memory_space=pl.ANY)],
        out_specs=pl.BlockSpec((tm, ne), row),
        out_shape=jax.ShapeDtypeStruct((t, ne), F32),
        compiler_params=_params("arbitrary"),
        name="peer_act",
    )(pre, gate, after)


def _residual_kernel(x1_ref, out_ref, mod_ref, prev_ref, y_ref):
    del prev_ref
    y_ref[...] = x1_ref[...] + mod_ref[0][5:6] * out_ref[...]


def _peer_residual(x1, out, mod, seq, tile0, y_prev):
    tb = GATHER_TILE
    per_batch = seq // tb
    row = lambda i: (tile0 + i, 0)
    return pl.pallas_call(
        _residual_kernel,
        grid=(out.shape[0] // tb,),
        in_specs=[pl.BlockSpec((tb, D_MODEL), row),
                  pl.BlockSpec((tb, D_MODEL), lambda i: (i, 0)),
                  pl.BlockSpec((1, 6, D_MODEL), lambda i: ((tile0 + i) // per_batch, 0, 0)),
                  pl.BlockSpec(memory_space=pl.ANY)],
        input_output_aliases={3: 0},
        out_specs=pl.BlockSpec((tb, D_MODEL), row),
        out_shape=jax.ShapeDtypeStruct(y_prev.shape, F32),
        compiler_params=_params("arbitrary"),
        name="peer_residual",
    )(x1, out, mod, y_prev)


SC_WORKERS = 32
SC_LANES = 16
SC_CHUNK = 16
SC_SHARE = (7, 16)


def _sc_udot(u_table, idx, x):
    t, ne = idx.shape
    per_w = t // SC_WORKERS
    nchunk = ne // SC_CHUNK
    nl = D_MODEL // SC_LANES
    mesh = plsc.VectorSubcoreMesh(core_axis_name="c", subcore_axis_name="s")

    @functools.partial(
        pl.kernel, out_type=jax.ShapeDtypeStruct((t, ne), F32), mesh=mesh,
        scratch_types=[pltpu.VMEM((ne,), jnp.int32), pltpu.VMEM((D_MODEL,), F32),
                       pltpu.VMEM((2, SC_CHUNK, D_MODEL), F32), pltpu.VMEM((SC_CHUNK, SC_LANES), F32),
                       pltpu.VMEM((ne,), F32), pltpu.SemaphoreType.DMA((2,))],
        compiler_params=pltpu.CompilerParams(needs_layout_passes=False),
        name="sc_udot")
    def run(u_hbm, idx_hbm, x_hbm, pre_hbm, idx_v, x_v, rows, accs, pre_v, sems):
        wid = lax.axis_index("s") * 2 + lax.axis_index("c")
        lanes = lax.iota(jnp.int32, SC_LANES)

        def gather(c, b):
            return pltpu.make_async_copy(u_hbm.at[idx_v.at[pl.ds(c * SC_CHUNK, SC_CHUNK)]], rows.at[b], sems.at[b])

        @pl.loop(0, per_w)
        def _(tt):
            tok = wid * per_w + tt
            pltpu.sync_copy(idx_hbm.at[tok], idx_v)
            pltpu.sync_copy(x_hbm.at[tok], x_v)
            gather(0, 0).start()
            for c in range(nchunk):
                b = c % 2
                if c + 1 < nchunk:
                    gather(c + 1, 1 - b).start()
                gather(c, b).wait()
                for g in range(SC_CHUNK // 4):
                    def body(j, acc):
                        xv = x_v[pl.ds(j * SC_LANES, SC_LANES)]
                        return tuple(acc[q] + rows[b, g * 4 + q, pl.ds(j * SC_LANES, SC_LANES)] * xv
                                     for q in range(4))
                    acc = plsc.parallel_loop(0, nl, unroll=4,
                                             carry=tuple(jnp.zeros((SC_LANES,), F32) for _ in range(4)))(body)
                    for q in range(4):
                        accs[g * 4 + q, :] = acc[q]
                tot = jnp.zeros((SC_LANES,), F32)
                for l in range(SC_LANES):
                    tot = tot + plsc.load_gather(accs, [lanes, jnp.full((SC_LANES,), l, jnp.int32)])
                pre_v[pl.ds(c * SC_CHUNK, SC_CHUNK)] = tot
            pltpu.sync_copy(pre_v, pre_hbm.at[tok])

    return run(u_table, idx, x)


def _sc_vsum(v_table, idx, wgt):
    t, ne = idx.shape
    per_w = t // SC_WORKERS
    nchunk = ne // SC_CHUNK
    block = 16 * SC_LANES
    mesh = plsc.VectorSubcoreMesh(core_axis_name="c", subcore_axis_name="s")

    @functools.partial(
        pl.kernel, out_type=jax.ShapeDtypeStruct((t, D_MODEL), F32), mesh=mesh,
        scratch_types=[pltpu.VMEM((ne,), jnp.int32), pltpu.VMEM((ne,), F32),
                       pltpu.VMEM((2, SC_CHUNK, D_MODEL), F32), pltpu.VMEM((D_MODEL,), F32),
                       pltpu.SemaphoreType.DMA((2,))],
        compiler_params=pltpu.CompilerParams(needs_layout_passes=False),
        name="sc_vsum")
    def run(v_hbm, idx_hbm, w_hbm, out_hbm, idx_v, w_v, rows, out_v, sems):
        wid = lax.axis_index("s") * 2 + lax.axis_index("c")

        def gather(c, b):
            return pltpu.make_async_copy(v_hbm.at[idx_v.at[pl.ds(c * SC_CHUNK, SC_CHUNK)]], rows.at[b], sems.at[b])

        @pl.loop(0, per_w)
        def _(tt):
            tok = wid * per_w + tt
            pltpu.sync_copy(idx_hbm.at[tok], idx_v)
            pltpu.sync_copy(w_hbm.at[tok], w_v)
            gather(0, 0).start()
            for c in range(nchunk):
                b = c % 2
                if c + 1 < nchunk:
                    gather(c + 1, 1 - b).start()
                gather(c, b).wait()
                for d in range(D_MODEL // block):
                    def body(k, acc):
                        wk = plsc.load_gather(w_v, [jnp.full((SC_LANES,), c * SC_CHUNK, jnp.int32) + k])
                        return tuple(acc[j] + rows[b, k, pl.ds(d * block + j * SC_LANES, SC_LANES)] * wk
                                     for j in range(16))
                    if c == 0:
                        init = tuple(jnp.zeros((SC_LANES,), F32) for _ in range(16))
                    else:
                        init = tuple(out_v[pl.ds(d * block + j * SC_LANES, SC_LANES)] for j in range(16))
                    acc = plsc.parallel_loop(0, SC_CHUNK, carry=init)(body)
                    for j in range(16):
                        out_v[pl.ds(d * block + j * SC_LANES, SC_LANES)] = acc[j]
            pltpu.sync_copy(out_v, out_hbm.at[tok])

    return run(v_table, idx, wgt)


def _expert_slabs(u_table, v_table):
    n = u_table.shape[0]
    u = u_table.reshape(n, 1, SLAB_ROWS, LANE)
    v = v_table.reshape(n, 1, SLAB_ROWS, LANE)
    return jnp.concatenate([u, v], axis=1)


def _rope_tables(seq):
    rows = seq // GRID_W
    r = jnp.repeat(jnp.arange(rows, dtype=F32), GRID_W)
    col = jnp.tile(jnp.arange(GRID_W, dtype=F32), rows)
    nf = ROPE_AXIS // 2
    freqs = jnp.power(ROPE_BASE, -jnp.arange(nf, dtype=F32) / nf)
    ar, ac = r[:, None] * freqs, col[:, None] * freqs
    pad = jnp.zeros((seq, LANE - MLA_ROPE), F32)
    cos = jnp.concatenate([jnp.cos(ar), jnp.cos(ar), jnp.cos(ac), jnp.cos(ac), pad], axis=-1)
    sin = jnp.concatenate([-jnp.sin(ar), jnp.sin(ar), -jnp.sin(ac), jnp.sin(ac), pad], axis=-1)
    return cos, sin


def _prep_weights(norm1_w, w_in, q_norm_w, w_uq, kv_norm_w, w_ukv, q_head_norm_w, k_head_norm_w,
                  ret_gn_w, w_out, norm2_w, w_pq, sub_keys1, sub_keys2):
    cut = Q_RANK + KV_RANK + MLA_ROPE
    w_in_p = jnp.concatenate([w_in[:, :cut], jnp.zeros((D_MODEL, LANE - MLA_ROPE), F32), w_in[:, cut:]], axis=1)
    wq = w_uq.reshape(Q_RANK, MLA_HEADS, MLA_QK)
    wq = jnp.pad(wq, ((0, 0), (0, 0), (0, HEAD_PAD - MLA_QK))).reshape(Q_RANK, MLA_HEADS * HEAD_PAD)

    def head_w(v):
        return jnp.pad(v, (0, HEAD_PAD - MLA_QK)).reshape(1, HEAD_PAD)

    return {
        "norm1_w": norm1_w.reshape(1, -1), "w_in": w_in_p.astype(BF16),
        "q_norm_w": q_norm_w.reshape(1, -1), "kv_norm_w": kv_norm_w.reshape(1, -1),
        "w_uq": wq.astype(BF16), "w_ukv": w_ukv.astype(BF16),
        "q_head_norm_w": head_w(q_head_norm_w), "k_head_norm_w": head_w(k_head_norm_w),
        "ret_gn_w": ret_gn_w.reshape(1, -1), "w_out": w_out.astype(BF16), "norm2_w": norm2_w.reshape(1, -1),
        "w_pq": w_pq.astype(BF16).reshape(D_MODEL, PEER_HEADS, PEER_QDIM).transpose(1, 2, 0), "sub_keys1": sub_keys1.astype(BF16), "sub_keys2": sub_keys2.astype(BF16),
    }


def _trunk(x, mod, w, decay_logit, tab, ctx):
    batch, seq, _ = x.shape
    x2 = x.reshape(batch * seq, D_MODEL)
    use_rope = ctx is not None
    if use_rope:
        cos, sin = _rope_tables(seq)
    else:
        cos = sin = jnp.zeros((ROW_TILE, LANE), F32)
    ckv, krope, q, k, v, zr = _in_proj(x2, mod, w, cos, sin, seq, use_rope)
    if use_rope:
        ckv_c, krope_c, state_c = ctx
        past = ckv_c.shape[1]
        kr_c = jnp.pad(krope_c.reshape(batch * past, MLA_ROPE), ((0, 0), (0, LANE - MLA_ROPE)))
        ctx_kv = _kv_up(ckv_c.reshape(batch * past, KV_RANK), kr_c, w)
        s0 = state_c.reshape(batch * 2, RET_HEADS, RET_DK, RET_DV)
    else:
        ctx_kv, s0 = None, None
    att = _attention(q, k, v, batch, seq, ctx_kv)
    o_f, o_b, s_f, s_b = _retention(zr, decay_logit, batch, seq, s0)
    x1, h2 = _mix(x2, att, o_f, o_b, zr, mod, w, seq)
    ntiles = (batch * seq) // GATHER_TILE
    sc_tiles = ntiles * SC_SHARE[0] // SC_SHARE[1] if use_rope else 0
    if sc_tiles == 0:
        y = _peer(tab, h2, x1, mod, w, seq, 0, ntiles)
    else:
        tc_tiles = ntiles - sc_tiles
        first = tc_tiles // 2
        gt, et = _peer_score(h2, w, tc_tiles, sc_tiles)
        idx = et.T.astype(jnp.int32)
        pre = _sc_udot(w["u_table"], idx, h2[tc_tiles * GATHER_TILE:])
        y = _peer(tab, h2, x1, mod, w, seq, 0, first)
        out = _sc_vsum(w["v_table"], idx, _peer_act(pre, gt.T, y))
        y = _peer(tab, h2, x1, mod, w, seq, first, tc_tiles - first, y_prev=y)
        y = _peer_residual(x1, out, mod, seq, tc_tiles, y)
    states = jnp.stack([s_f, s_b], axis=1)
    return y.reshape(batch, seq, D_MODEL), ckv.reshape(batch, seq, KV_RANK), krope.reshape(batch, seq, MLA_ROPE), states


def kernel(x_prompt, x_sample, c, cache_ckv, cache_krope, state_ret, c_ctx, w_ada, b_ada, norm1_w, w_in,
           q_norm_w, w_uq, kv_norm_w, w_ukv, q_head_norm_w, k_head_norm_w, ret_decay_logit, ret_gn_w,
           w_out, norm2_w, w_pq, sub_keys1, sub_keys2, u_table, v_table):
    depth = w_ada.shape[0]
    nb_ctx = x_prompt.shape[0]
    nb_lat = x_sample.shape[0]
    y_prompt, y_sample = x_prompt, x_sample
    ckv_list, krope_list, ret_list = [], [], []
    for l in range(depth):
        cond_rows = -(-(nb_lat + 1) // 8) * 8
        cond = jnp.concatenate([c, c_ctx[None, :], jnp.zeros((cond_rows - nb_lat - 1, D_MODEL), F32)], axis=0)
        mod = _ada(cond, w_ada[l], b_ada[l])
        mod_lat = mod[:nb_lat].reshape(nb_lat, 6, D_MODEL)
        mod_ctx = jnp.broadcast_to(mod[nb_lat].reshape(1, 6, D_MODEL), (nb_ctx, 6, D_MODEL))
        w = _prep_weights(norm1_w[l], w_in[l], q_norm_w[l], w_uq[l], kv_norm_w[l], w_ukv[l], q_head_norm_w[l],
                          k_head_norm_w[l], ret_gn_w[l], w_out[l], norm2_w[l], w_pq[l], sub_keys1[l], sub_keys2[l])
        tab = _expert_slabs(u_table[l], v_table[l])
        w["u_table"], w["v_table"] = u_table[l], v_table[l]
        y_prompt, ckv_l, krope_l, ret_l = _trunk(y_prompt, mod_ctx, w, ret_decay_logit[l], tab, None)
        ckv_list.append(ckv_l)
        krope_list.append(krope_l)
        ret_list.append(ret_l)
        y_sample, _, _, _ = _trunk(y_sample, mod_lat, w, ret_decay_logit[l], tab,
                                   (cache_ckv[:, l], cache_krope[:, l], state_ret[:, l]))
    return (y_prompt, y_sample, jnp.stack(ckv_list, axis=1), jnp.stack(krope_list, axis=1),
            jnp.stack(ret_list, axis=1))
```

```python
import functools

import jax
import jax.numpy as jnp
from jax import lax
from jax.experimental import pallas as pl
from jax.experimental.pallas import tpu as pltpu
from jax.experimental.pallas import tpu_sc as plsc

F32 = jnp.float32
BF16 = jnp.bfloat16

D_MODEL = 1024
GRID_W = 64
MLA_HEADS = 4
MLA_NOPE = 128
MLA_ROPE = 64
MLA_QK = MLA_NOPE + MLA_ROPE
MLA_V = 128
Q_RANK = 512
KV_RANK = 256
ROPE_AXIS = MLA_ROPE // 2
ROPE_BASE = 10000.0
RET_HEADS = 4
RET_DK = 128
RET_DV = 128
RET_CHUNK = 128
PEER_HEADS = 8
PEER_QDIM = 256
PEER_HALF = PEER_QDIM // 2
N_KEYS = 128
PEER_TOPK = 16
EPS = 1e-6

LANE = 128
HEAD_PAD = 2 * LANE
RET_W = 4 * RET_HEADS * RET_DK
IN_PAD = Q_RANK + KV_RANK + LANE + RET_W
VMEM_LIMIT = 48 * 1024 * 1024

ROW_TILE = 256
Q_TILE = 256
GATHER_TILE = 128
GATHER_SLOTS = 4
GATHER_AHEAD = GATHER_SLOTS - 1
EXPERTS_PER_TOKEN = PEER_HEADS * PEER_TOPK
SLAB_ROWS = D_MODEL // LANE


def _params(*sem):
    return pltpu.CompilerParams(dimension_semantics=sem, vmem_limit_bytes=VMEM_LIMIT)


def _rms(x, w):
    return x * lax.rsqrt(jnp.mean(x * x, axis=-1, keepdims=True) + EPS) * w


def _mm(a, b):
    return jnp.dot(a.astype(BF16), b.astype(BF16), preferred_element_type=F32)


def _mm_nt(a, b):
    return lax.dot_general(a.astype(BF16), b.astype(BF16), (((1,), (1,)), ((), ())),
                           preferred_element_type=F32)


def _mm_tn(a, b):
    return lax.dot_general(a.astype(BF16), b.astype(BF16), (((0,), (0,)), ((), ())),
                           preferred_element_type=F32)


def _ada_kernel(c_ref, w_ref, b_ref, o_ref):
    c = c_ref[...]
    o_ref[...] = _mm(c * jax.nn.sigmoid(c), w_ref[...]) + b_ref[...]


def _ada(cond, w_ada, b_ada):
    rows, d = cond.shape
    n = w_ada.shape[1]
    tn = 1536
    return pl.pallas_call(
        _ada_kernel,
        grid=(n // tn,),
        in_specs=[pl.BlockSpec((rows, d), lambda j: (0, 0)),
                  pl.BlockSpec((d, tn), lambda j: (0, j)),
                  pl.BlockSpec((1, tn), lambda j: (0, j))],
        out_specs=pl.BlockSpec((rows, tn), lambda j: (0, j)),
        out_shape=jax.ShapeDtypeStruct((rows, n), F32),
        compiler_params=_params("arbitrary"),
        name="ada",
    )(cond, w_ada, b_ada.reshape(1, n))


def _rope_tile(x, cos, sin):
    lane = lax.broadcasted_iota(jnp.int32, x.shape, 1)
    partner = jnp.where((lane % 32) < 16, pltpu.roll(x, LANE - 16, 1), pltpu.roll(x, 16, 1))
    return x * cos + partner * sin


def _kv_heads(kv, kr, khw, cos, sin, use_rope):
    krw = kr * khw[:, LANE:]
    if use_rope:
        krw = _rope_tile(krw, cos, sin)
    ssq_r = jnp.sum(kr * kr, axis=-1, keepdims=True)
    ks, vs = [], []
    for hd in range(MLA_HEADS):
        kn = kv[:, hd * HEAD_PAD: hd * HEAD_PAD + LANE]
        r = lax.rsqrt((jnp.sum(kn * kn, axis=-1, keepdims=True) + ssq_r) / MLA_QK + EPS)
        ks += [kn * r * khw[:, :LANE], krw * r]
        vs.append(kv[:, hd * HEAD_PAD + LANE: (hd + 1) * HEAD_PAD])
    return jnp.concatenate(ks, axis=-1).astype(BF16), jnp.concatenate(vs, axis=-1).astype(BF16)


def _inproj_kernel(x_ref, mod_ref, n1w_ref, win_ref, qnw_ref, kvnw_ref, wuq_ref, wukv_ref, qhw_ref, khw_ref,
                   cos_ref, sin_ref, ckv_ref, krope_ref, q_ref, k_ref, v_ref, zr_ref, *, use_rope):
    m = mod_ref[0]
    h = _rms(x_ref[...], n1w_ref[...]) * (1.0 + m[1:2]) + m[0:1]
    z = _mm(h, win_ref[...])
    kr = z[:, Q_RANK + KV_RANK: Q_RANK + KV_RANK + LANE]
    zr_ref[...] = z[:, Q_RANK + KV_RANK + LANE:]
    ckvn = _rms(z[:, Q_RANK: Q_RANK + KV_RANK], kvnw_ref[...])
    ckv_ref[...] = ckvn
    krope_ref[...] = kr[:, :MLA_ROPE]
    cos, sin = cos_ref[...], sin_ref[...]

    q = _mm(_rms(z[:, :Q_RANK], qnw_ref[...]), wuq_ref[...])
    qhw = qhw_ref[...]
    qs = []
    for hd in range(MLA_HEADS):
        qn = q[:, hd * HEAD_PAD: hd * HEAD_PAD + LANE]
        qr = q[:, hd * HEAD_PAD + LANE: (hd + 1) * HEAD_PAD]
        ssq = jnp.sum(qn * qn, axis=-1, keepdims=True) + jnp.sum(qr * qr, axis=-1, keepdims=True)
        r = lax.rsqrt(ssq / MLA_QK + EPS)
        qrw = qr * qhw[:, LANE:]
        if use_rope:
            qrw = _rope_tile(qrw, cos, sin)
        qs += [qn * r * qhw[:, :LANE], qrw * r]
    q_ref[...] = jnp.concatenate(qs, axis=-1).astype(BF16)

    k, v = _kv_heads(_mm(ckvn, wukv_ref[...]), kr, khw_ref[...], cos, sin, use_rope)
    k_ref[...] = k
    v_ref[...] = v


def _in_proj(x, mod, w, cos, sin, seq, use_rope):
    t = x.shape[0]
    tm = ROW_TILE
    per_batch = seq // tm
    const = lambda i: (0, 0)
    row = lambda i: (i, 0)
    if use_rope:
        pos = lambda i: (i % per_batch, 0)
    else:
        pos = const
    hp = MLA_HEADS * HEAD_PAD
    return pl.pallas_call(
        functools.partial(_inproj_kernel, use_rope=use_rope),
        grid=(t // tm,),
        in_specs=[pl.BlockSpec((tm, D_MODEL), row),
                  pl.BlockSpec((1, 6, D_MODEL), lambda i: (i // per_batch, 0, 0)),
                  pl.BlockSpec((1, D_MODEL), const),
                  pl.BlockSpec((D_MODEL, IN_PAD), const),
                  pl.BlockSpec((1, Q_RANK), const),
                  pl.BlockSpec((1, KV_RANK), const),
                  pl.BlockSpec((Q_RANK, hp), const),
                  pl.BlockSpec((KV_RANK, hp), const),
                  pl.BlockSpec((1, HEAD_PAD), const),
                  pl.BlockSpec((1, HEAD_PAD), const),
                  pl.BlockSpec((tm, LANE), pos),
                  pl.BlockSpec((tm, LANE), pos)],
        out_specs=[pl.BlockSpec((tm, KV_RANK), row),
                   pl.BlockSpec((tm, MLA_ROPE), row),
                   pl.BlockSpec((tm, hp), row),
                   pl.BlockSpec((tm, hp), row),
                   pl.BlockSpec((tm, MLA_HEADS * MLA_V), row),
                   pl.BlockSpec((tm, RET_W), row)],
        out_shape=[jax.ShapeDtypeStruct((t, KV_RANK), F32),
                   jax.ShapeDtypeStruct((t, MLA_ROPE), F32),
                   jax.ShapeDtypeStruct((t, hp), BF16),
                   jax.ShapeDtypeStruct((t, hp), BF16),
                   jax.ShapeDtypeStruct((t, MLA_HEADS * MLA_V), BF16),
                   jax.ShapeDtypeStruct((t, RET_W), F32)],
        compiler_params=_params("arbitrary"),
        name="in_proj",
    )(x, mod, w["norm1_w"], w["w_in"], w["q_norm_w"], w["kv_norm_w"], w["w_uq"], w["w_ukv"],
      w["q_head_norm_w"], w["k_head_norm_w"], cos, sin)


def _kvup_kernel(ckv_ref, kr_ref, wukv_ref, khw_ref, k_ref, v_ref):
    k, v = _kv_heads(_mm(ckv_ref[...], wukv_ref[...]), kr_ref[...], khw_ref[...], None, None, False)
    k_ref[...] = k
    v_ref[...] = v


def _kv_up(ckv, kr, w):
    t = ckv.shape[0]
    tm = ROW_TILE
    hp = MLA_HEADS * HEAD_PAD
    const = lambda i: (0, 0)
    row = lambda i: (i, 0)
    return pl.pallas_call(
        _kvup_kernel,
        grid=(t // tm,),
        in_specs=[pl.BlockSpec((tm, KV_RANK), row),
                  pl.BlockSpec((tm, LANE), row),
                  pl.BlockSpec((KV_RANK, hp), const),
                  pl.BlockSpec((1, HEAD_PAD), const)],
        out_specs=[pl.BlockSpec((tm, hp), row),
                   pl.BlockSpec((tm, MLA_HEADS * MLA_V), row)],
        out_shape=[jax.ShapeDtypeStruct((t, hp), BF16),
                   jax.ShapeDtypeStruct((t, MLA_HEADS * MLA_V), BF16)],
        compiler_params=_params("arbitrary"),
        name="kv_up",
    )(ckv, kr, w["w_ukv"], w["k_head_norm_w"])


def _attn_kernel(*refs, has_ctx):
    if has_ctx:
        q_ref, k_ref, v_ref, kc_ref, vc_ref, o_ref = refs
    else:
        q_ref, k_ref, v_ref, o_ref = refs
    scale = MLA_QK ** -0.5
    q = q_ref[...]
    s = _mm_nt(q, k_ref[...]) * scale
    m = jnp.max(s, axis=-1, keepdims=True)
    if has_ctx:
        sc = _mm_nt(q, kc_ref[...]) * scale
        m = jnp.maximum(m, jnp.max(sc, axis=-1, keepdims=True))
    p = jnp.exp(s - m)
    den = jnp.sum(p, axis=-1, keepdims=True)
    o = _mm(p, v_ref[...])
    if has_ctx:
        pc = jnp.exp(sc - m)
        den = den + jnp.sum(pc, axis=-1, keepdims=True)
        o = o + _mm(pc, vc_ref[...])
    o_ref[...] = (o / den).astype(BF16)


def _attention(q, k, v, batch, seq, ctx_kv):
    tq = min(Q_TILE, seq)
    nq = seq // tq
    has_ctx = ctx_kv is not None
    qmap = lambda b, h, i: (b * nq + i, h)
    kvmap = lambda b, h, i: (b, h)
    in_specs = [pl.BlockSpec((tq, HEAD_PAD), qmap),
                pl.BlockSpec((seq, HEAD_PAD), kvmap),
                pl.BlockSpec((seq, MLA_V), kvmap)]
    args = [q, k, v]
    if has_ctx:
        kc, vc = ctx_kv
        past = kc.shape[0] // batch
        in_specs += [pl.BlockSpec((past, HEAD_PAD), kvmap), pl.BlockSpec((past, MLA_V), kvmap)]
        args += [kc, vc]
    return pl.pallas_call(
        functools.partial(_attn_kernel, has_ctx=has_ctx),
        grid=(batch, MLA_HEADS, nq),
        in_specs=in_specs,
        out_specs=pl.BlockSpec((tq, MLA_V), qmap),
        out_shape=jax.ShapeDtypeStruct((batch * seq, MLA_HEADS * MLA_V), BF16),
        compiler_params=_params("arbitrary", "arbitrary", "arbitrary"),
        name="attention",
    )(*args)


def _ret_kernel(*refs, has_s0):
    if has_s0:
        (lg_ref, qf_ref, kf_ref, vf_ref, qb_ref, kb_ref, vb_ref, s0f_ref, s0b_ref,
         of_ref, ob_ref, sf_out, sb_out, sf_scr, sb_scr) = refs
    else:
        (lg_ref, qf_ref, kf_ref, vf_ref, qb_ref, kb_ref, vb_ref,
         of_ref, ob_ref, sf_out, sb_out, sf_scr, sb_scr) = refs
    c = pl.program_id(1)
    cs = RET_CHUNK

    @pl.when(c == 0)
    def _():
        if has_s0:
            sf_scr[...] = s0f_ref[0]
            sb_scr[...] = s0b_ref[0]
        else:
            sf_scr[...] = jnp.zeros_like(sf_scr)
            sb_scr[...] = jnp.zeros_like(sb_scr)

    ii = lax.broadcasted_iota(jnp.int32, (cs, cs), 0).astype(F32)
    jj = lax.broadcasted_iota(jnp.int32, (cs, cs), 1).astype(F32)
    rel = ii - jj
    kscale = RET_DK ** -0.5

    def chunk(q, k, v, s, intra, qdec, kdec, cdec):
        a = _mm_nt(q, k) * intra
        o = _mm(a, v) + _mm(q, s) * qdec
        return o, s * cdec + _mm_tn(k * kdec, v)

    for hd in range(RET_HEADS):
        cols = slice(hd * RET_DK, (hd + 1) * RET_DK)
        lgf = jax.nn.log_sigmoid(jnp.full((1, LANE), lg_ref[0, hd], F32))
        lgb = jax.nn.log_sigmoid(jnp.full((1, LANE), lg_ref[1, hd], F32))

        intra_f = jnp.where(rel >= 0, jnp.exp(jnp.maximum(rel, 0.0) * lgf), 0.0)
        o_f, s_f = chunk(qf_ref[:, cols], kf_ref[:, cols] * kscale, vf_ref[:, cols], sf_scr[hd], intra_f,
                         jnp.exp((ii + 1.0) * lgf), jnp.exp((cs - 1.0 - ii) * lgf), jnp.exp(cs * lgf))
        of_ref[:, cols] = o_f
        sf_scr[hd] = s_f
        sf_out[0, hd] = s_f

        intra_b = jnp.where(rel <= 0, jnp.exp(jnp.maximum(-rel, 0.0) * lgb), 0.0)
        o_b, s_b = chunk(qb_ref[:, cols], kb_ref[:, cols] * kscale, vb_ref[:, cols], sb_scr[hd], intra_b,
                         jnp.exp((cs - ii) * lgb), jnp.exp(ii * lgb), jnp.exp(cs * lgb))
        ob_ref[:, cols] = o_b
        sb_scr[hd] = s_b
        sb_out[0, hd] = s_b


def _retention(zr, decay_logit, batch, seq, s0):
    cs = RET_CHUNK
    nc = seq // cs
    nh = RET_HEADS
    width = nh * RET_DK
    has_s0 = s0 is not None

    def fwd(col):
        return pl.BlockSpec((cs, width), lambda b, c: (b * nc + c, col))

    def bwd(col):
        return pl.BlockSpec((cs, width), lambda b, c: (b * nc + nc - 1 - c, col))

    def state(d):
        return pl.BlockSpec((1, nh, RET_DK, RET_DV), lambda b, c: (b * 2 + d, 0, 0, 0))

    state_out = pl.BlockSpec((1, nh, RET_DK, RET_DV), lambda b, c: (b, 0, 0, 0))
    in_specs = [pl.BlockSpec(memory_space=pltpu.SMEM), fwd(0), fwd(1), fwd(2), bwd(0), bwd(1), bwd(2)]
    args = [decay_logit, zr, zr, zr, zr, zr, zr]
    if has_s0:
        in_specs += [state(0), state(1)]
        args += [s0, s0]
    t = batch * seq
    o_f, o_b, s_f, s_b = pl.pallas_call(
        functools.partial(_ret_kernel, has_s0=has_s0),
        grid=(batch, nc),
        in_specs=in_specs,
        out_specs=[pl.BlockSpec((cs, width), lambda b, c: (b * nc + c, 0)),
                   pl.BlockSpec((cs, width), lambda b, c: (b * nc + nc - 1 - c, 0)),
                   state_out, state_out],
        out_shape=[jax.ShapeDtypeStruct((t, width), F32),
                   jax.ShapeDtypeStruct((t, width), F32),
                   jax.ShapeDtypeStruct((batch, nh, RET_DK, RET_DV), F32),
                   jax.ShapeDtypeStruct((batch, nh, RET_DK, RET_DV), F32)],
        scratch_shapes=[pltpu.VMEM((nh, RET_DK, RET_DV), F32), pltpu.VMEM((nh, RET_DK, RET_DV), F32)],
        compiler_params=_params("arbitrary", "arbitrary"),
        name="retention",
    )(*args)
    return o_f, o_b, s_f, s_b


def _mix_kernel(x_ref, att_ref, of_ref, ob_ref, gr_ref, mod_ref, gnw_ref, wout_ref, n2w_ref, x1_ref, h2_ref):
    o = of_ref[...] + ob_ref[...]
    parts = []
    for hd in range(RET_HEADS):
        oh = o[:, hd * RET_DV:(hd + 1) * RET_DV]
        d = oh - jnp.mean(oh, axis=-1, keepdims=True)
        parts.append(d * lax.rsqrt(jnp.mean(d * d, axis=-1, keepdims=True) + EPS))
    g = gr_ref[...]
    ret = (g * jax.nn.sigmoid(g)) * (jnp.concatenate(parts, axis=-1) * gnw_ref[...])
    na = MLA_HEADS * MLA_V
    mixed = _mm(att_ref[...], wout_ref[:na, :]) + _mm(ret, wout_ref[na:, :])
    m = mod_ref[0]
    x1 = x_ref[...] + m[2:3] * mixed
    x1_ref[...] = x1
    h2_ref[...] = _rms(x1, n2w_ref[...]) * (1.0 + m[4:5]) + m[3:4]


def _mix(x, att, o_f, o_b, zr, mod, w, seq):
    t = x.shape[0]
    tm = ROW_TILE
    per_batch = seq // tm
    const = lambda i: (0, 0)
    row = lambda i: (i, 0)
    half = RET_HEADS * RET_DV
    return pl.pallas_call(
        _mix_kernel,
        grid=(t // tm,),
        in_specs=[pl.BlockSpec((tm, D_MODEL), row),
                  pl.BlockSpec((tm, half), row),
                  pl.BlockSpec((tm, half), row),
                  pl.BlockSpec((tm, half), row),
                  pl.BlockSpec((tm, half), lambda i: (i, 3)),
                  pl.BlockSpec((1, 6, D_MODEL), lambda i: (i // per_batch, 0, 0)),
                  pl.BlockSpec((1, half), const),
                  pl.BlockSpec((D_MODEL, D_MODEL), const),
                  pl.BlockSpec((1, D_MODEL), const)],
        out_specs=[pl.BlockSpec((tm, D_MODEL), row), pl.BlockSpec((tm, D_MODEL), row)],
        out_shape=[jax.ShapeDtypeStruct((t, D_MODEL), F32), jax.ShapeDtypeStruct((t, D_MODEL), F32)],
        compiler_params=_params("arbitrary"),
        name="mix",
    )(x, att, o_f, o_b, zr, mod, w["ret_gn_w"], w["w_out"], w["norm2_w"])


def _top16(jobs):
    def body(r, carry):
        for s_ref, vals_ref, pay_ref, payload in jobs:
            n, cols = s_ref.shape
            rows = lax.broadcasted_iota(jnp.int32, (n, cols), 0).astype(F32)
            s = s_ref[...]
            m = jnp.max(s, axis=0, keepdims=True)
            pos = jnp.min(jnp.where(s == m, rows, float(n)), axis=0, keepdims=True)
            hit = rows == pos
            vals_ref[pl.ds(r, 1), :] = m
            if payload is None:
                pay_ref[pl.ds(r, 1), :] = pos
            else:
                pay_ref[pl.ds(r, 1), :] = jnp.sum(jnp.where(hit, payload, 0.0), axis=0, keepdims=True)
            s_ref[...] = jnp.where(hit, -jnp.inf, s)
        return carry

    lax.fori_loop(0, PEER_TOPK, body, 0)


PAIR_COUNTS = tuple(PEER_TOPK // (a + 1) for a in range(PEER_TOPK))
NUM_PAIRS = sum(PAIR_COUNTS)
PAIR_ROWS = -(-NUM_PAIRS // 8) * 8


def _score_head(hbt, wpqt, k1, k2, g_out, e_out, scr):
    s1_scr, s2_scr, c_scr, p_scr, v1_scr, i1_scr, v2_scr, i2_scr, vt_scr, it_scr = scr
    qt = jnp.dot(wpqt, hbt, preferred_element_type=F32)
    s1_scr[...] = _mm(k1, qt[:PEER_HALF, :])
    s2_scr[...] = _mm(k2, qt[PEER_HALF:, :])
    _top16([(s1_scr, v1_scr, i1_scr, None), (s2_scr, v2_scr, i2_scr, None)])
    c_scr[...] = jnp.full(c_scr.shape, -jnp.inf, F32)
    p_scr[...] = jnp.zeros(p_scr.shape, F32)
    off = 0
    for a, nb in enumerate(PAIR_COUNTS):
        c_scr[off:off + nb, :] = v1_scr[a:a + 1, :] + v2_scr[0:nb, :]
        p_scr[off:off + nb, :] = i1_scr[a:a + 1, :] * float(N_KEYS) + i2_scr[0:nb, :]
        off += nb
    _top16([(c_scr, vt_scr, it_scr, p_scr[...])])
    top = vt_scr[...]
    p = jnp.exp(top - jnp.max(top, axis=0, keepdims=True))
    g_out[...] = p / jnp.sum(p, axis=0, keepdims=True)
    e_out[...] = it_scr[...]


def _score_scratch(tb):
    k = PEER_TOPK
    return ([pltpu.VMEM((N_KEYS, tb), F32)] * 2 + [pltpu.VMEM((PAIR_ROWS, tb), F32)] * 2
            + [pltpu.VMEM((k, tb), F32)] * 6)


def _peer_kernel(tab_hbm, h2_ref, h2n_ref, wpq_ref, k1_ref, k2_ref, x1_ref, mod_ref, *rest, ntiles, aliased):
    y_ref, idx_smem, g_scr, e_scr, ei_scr, hb_scr = rest[int(aliased):int(aliased) + 6]
    scratch = rest[int(aliased) + 6:]
    _peer_body(tab_hbm, h2_ref, h2n_ref, wpq_ref, k1_ref, k2_ref, x1_ref, mod_ref, y_ref,
               idx_smem, g_scr, e_scr, ei_scr, hb_scr, *scratch, ntiles=ntiles)


def _peer_body(tab_hbm, h2_ref, h2n_ref, wpq_ref, k1_ref, k2_ref, x1_ref, mod_ref, y_ref,
               idx_smem, g_scr, e_scr, ei_scr, hb_scr, *scratch, ntiles):
    bufs = scratch[:GATHER_SLOTS]
    sem_rows, sem_idx = scratch[GATHER_SLOTS:GATHER_SLOTS + 2]
    score_scr = scratch[GATHER_SLOTS + 2:]
    i = pl.program_id(0)
    tb = GATHER_TILE
    ns = GATHER_SLOTS
    ahead = GATHER_AHEAD
    ne = EXPERTS_PER_TOKEN
    slab = SLAB_ROWS
    topk = PEER_TOPK
    groups_per_head = tb // (ns * PEER_HEADS)
    cur = i % 2
    nxt = 1 - cur
    more = i + 1 < ntiles

    def score(src_ref, hd, slot):
        @pl.when(hd == 0)
        def _():
            hb_scr[...] = src_ref[...].T.astype(BF16)
        head_rows = pl.ds(pl.multiple_of(hd * topk, topk), topk)
        _score_head(hb_scr[...], wpq_ref[hd], k1_ref[...], k2_ref[...],
                    g_scr.at[slot, head_rows], e_scr.at[head_rows], score_scr)

    def publish(slot):
        ei_scr[...] = e_scr[...].T.astype(jnp.int32)
        copy = pltpu.make_async_copy(ei_scr, idx_smem.at[slot], sem_idx.at[0])
        copy.start()
        copy.wait()

    def issue(islot, tok, rslot):
        for kk in range(ne):
            ex = idx_smem[islot, tok, kk]
            pltpu.make_async_copy(tab_hbm.at[ex], bufs[rslot].at[:, pl.ds(kk * slab, slab), :],
                                  sem_rows.at[rslot]).start(priority=kk % 2)

    def wait_rows(rslot):
        pltpu.make_async_copy(bufs[rslot], bufs[rslot], sem_rows.at[rslot]).wait()

    @pl.when(i == 0)
    def _():
        def first(hd, carry):
            score(h2_ref, hd, 0)
            return carry
        lax.fori_loop(0, PEER_HEADS, first, 0)
        publish(0)
        for tok in range(ahead):
            issue(0, tok, tok)

    g2 = mod_ref[0][5:6]
    lane = lax.broadcasted_iota(jnp.int32, (ne, tb), 1)

    def compute(j, rslot):
        rows = bufs[rslot]
        x_row = h2_ref[pl.ds(j, 1), :]
        acc = None
        for c in range(slab):
            u_c = rows[0, pl.ds(c, ne, stride=slab), :]
            term = u_c * x_row[:, c * LANE:(c + 1) * LANE]
            acc = term if acc is None else acc + term
        pre = jnp.sum(acc, axis=-1, keepdims=True)
        act = 0.5 * pre * (1.0 + lax.erf(pre * (2.0 ** -0.5)))
        gate = jnp.sum(jnp.where(lane == j, g_scr[cur], 0.0), axis=-1, keepdims=True)
        wgt = gate * act
        outs = []
        for c in range(slab):
            v_c = rows[1, pl.ds(c, ne, stride=slab), :]
            outs.append(jnp.sum(v_c * wgt, axis=0, keepdims=True))
        out = jnp.concatenate(outs, axis=-1)
        y_ref[pl.ds(j, 1), :] = x1_ref[pl.ds(j, 1), :] + g2 * out

    def phase(hd, carry):
        score(h2n_ref, hd, nxt)

        @pl.when(hd == PEER_HEADS - 1)
        def _():
            publish(nxt)

        def group(gq, inner):
            g = hd * groups_per_head + gq
            for u in range(ns):
                j = g * ns + u
                wait_rows(u)
                jj = j + ahead
                over = jj >= tb
                issue(jnp.where(over, nxt, cur), jnp.where(over, jj - tb, jj), (u + ahead) % ns)
                compute(j, u)
            return inner

        lax.fori_loop(0, groups_per_head, group, 0)
        return carry

    lax.fori_loop(0, PEER_HEADS, phase, 0)

    @pl.when(jnp.logical_not(more))
    def _():
        for u in range(ahead):
            wait_rows(u)


def _peer(tab, h2, x1, mod, w, seq, tile0, ntiles, y_prev=None):
    t = h2.shape[0]
    tb = GATHER_TILE
    per_batch = seq // tb
    row = lambda i: (tile0 + i, 0)
    const = lambda i: (0, 0)
    in_specs = [pl.BlockSpec(memory_space=pl.ANY),
                pl.BlockSpec((tb, D_MODEL), row),
                pl.BlockSpec((tb, D_MODEL), lambda i: (tile0 + jnp.minimum(i + 1, ntiles - 1), 0)),
                pl.BlockSpec((PEER_HEADS, PEER_QDIM, D_MODEL), lambda i: (0, 0, 0)),
                pl.BlockSpec((N_KEYS, PEER_HALF), const),
                pl.BlockSpec((N_KEYS, PEER_HALF), const),
                pl.BlockSpec((tb, D_MODEL), row),
                pl.BlockSpec((1, 6, D_MODEL), lambda i: ((tile0 + i) // per_batch, 0, 0))]
    args = [tab, h2, h2, w["w_pq"], w["sub_keys1"], w["sub_keys2"], x1, mod]
    aliases = {}
    if y_prev is not None:
        in_specs.append(pl.BlockSpec(memory_space=pl.ANY))
        args.append(y_prev)
        aliases = {len(args) - 1: 0}
    return pl.pallas_call(
        functools.partial(_peer_kernel, ntiles=ntiles, aliased=y_prev is not None),
        grid=(ntiles,),
        in_specs=in_specs,
        input_output_aliases=aliases,
        out_specs=pl.BlockSpec((tb, D_MODEL), row),
        out_shape=jax.ShapeDtypeStruct((t, D_MODEL), F32),
        scratch_shapes=[pltpu.SMEM((2, tb, EXPERTS_PER_TOKEN), jnp.int32),
                        pltpu.VMEM((2, EXPERTS_PER_TOKEN, tb), F32),
                        pltpu.VMEM((EXPERTS_PER_TOKEN, tb), F32),
                        pltpu.VMEM((tb, EXPERTS_PER_TOKEN), jnp.int32),
                        pltpu.VMEM((D_MODEL, tb), BF16)]
                       + [pltpu.VMEM((2, EXPERTS_PER_TOKEN * SLAB_ROWS, LANE), F32)] * GATHER_SLOTS
                       + [pltpu.SemaphoreType.DMA((GATHER_SLOTS,)), pltpu.SemaphoreType.DMA((1,))]
                       + _score_scratch(tb),
        compiler_params=pltpu.CompilerParams(dimension_semantics=("arbitrary",),
                                             vmem_limit_bytes=VMEM_LIMIT,
                                             disable_bounds_checks=True),
        name="peer",
    )(*args)


def _score_kernel(h2_ref, wpq_ref, k1_ref, k2_ref, g_ref, e_ref, hb_scr, *score_scr):
    hb_scr[...] = h2_ref[...].T.astype(BF16)

    def head(hd, carry):
        head_rows = pl.ds(pl.multiple_of(hd * PEER_TOPK, PEER_TOPK), PEER_TOPK)
        _score_head(hb_scr[...], wpq_ref[hd], k1_ref[...], k2_ref[...], g_ref.at[head_rows], e_ref.at[head_rows],
                    score_scr)
        return carry

    lax.fori_loop(0, PEER_HEADS, head, 0)


def _peer_score(h2, w, tile0, ntiles):
    tb = GATHER_TILE
    const = lambda i: (0, 0)
    out = pl.BlockSpec((EXPERTS_PER_TOKEN, tb), lambda i: (0, i))
    return pl.pallas_call(
        _score_kernel,
        grid=(ntiles,),
        in_specs=[pl.BlockSpec((tb, D_MODEL), lambda i: (tile0 + i, 0)),
                  pl.BlockSpec((PEER_HEADS, PEER_QDIM, D_MODEL), lambda i: (0, 0, 0)),
                  pl.BlockSpec((N_KEYS, PEER_HALF), const),
                  pl.BlockSpec((N_KEYS, PEER_HALF), const)],
        out_specs=[out, out],
        out_shape=[jax.ShapeDtypeStruct((EXPERTS_PER_TOKEN, ntiles * tb), F32)] * 2,
        scratch_shapes=[pltpu.VMEM((D_MODEL, tb), BF16)] + _score_scratch(tb),
        compiler_params=_params("arbitrary"),
        name="peer_score",
    )(h2, w["w_pq"], w["sub_keys1"], w["sub_keys2"])


def _act_kernel(pre_ref, g_ref, after_ref, w_ref):
    del after_ref
    pre = pre_ref[...]
    w_ref[...] = g_ref[...] * (0.5 * pre * (1.0 + lax.erf(pre * (2.0 ** -0.5))))


def _peer_act(pre, gate, after):
    t, ne = pre.shape
    tm = 1024
    row = lambda i: (i, 0)
    return pl.pallas_call(
        _act_kernel,
        grid=(t // tm,),
        in_specs=[pl.BlockSpec((tm, ne), row), pl.BlockSpec((tm, ne), row), pl.BlockSpec(memory_space=pl.ANY)],
        out_specs=pl.BlockSpec((tm, ne), row),
        out_shape=jax.ShapeDtypeStruct((t, ne), F32),
        compiler_params=_params("arbitrary"),
        name="peer_act",
    )(pre, gate, after)


def _residual_kernel(x1_ref, out_ref, mod_ref, prev_ref, y_ref):
    del prev_ref
    y_ref[...] = x1_ref[...] + mod_ref[0][5:6] * out_ref[...]


def _peer_residual(x1, out, mod, seq, tile0, y_prev):
    tb = GATHER_TILE
    per_batch = seq // tb
    row = lambda i: (tile0 + i, 0)
    return pl.pallas_call(
        _residual_kernel,
        grid=(out.shape[0] // tb,),
        in_specs=[pl.BlockSpec((tb, D_MODEL), row),
                  pl.BlockSpec((tb, D_MODEL), lambda i: (i, 0)),
                  pl.BlockSpec((1, 6, D_MODEL), lambda i: ((tile0 + i) // per_batch, 0, 0)),
                  pl.BlockSpec(memory_space=pl.ANY)],
        input_output_aliases={3: 0},
        out_specs=pl.BlockSpec((tb, D_MODEL), row),
        out_shape=jax.ShapeDtypeStruct(y_prev.shape, F32),
        compiler_params=_params("arbitrary"),
        name="peer_residual",
    )(x1, out, mod, y_prev)


SC_WORKERS = 32
SC_LANES = 16
SC_CHUNK = 16
SC_SHARE = (17, 32)
TC_FIRST = (13, 20)


def _sc_udot(u_table, idx, x):
    t, ne = idx.shape
    per_w = t // SC_WORKERS
    nchunk = ne // SC_CHUNK
    nl = D_MODEL // SC_LANES
    mesh = plsc.VectorSubcoreMesh(core_axis_name="c", subcore_axis_name="s")

    @functools.partial(
        pl.kernel, out_type=jax.ShapeDtypeStruct((t, ne), F32), mesh=mesh,
        scratch_types=[pltpu.VMEM((ne,), jnp.int32), pltpu.VMEM((D_MODEL,), F32),
                       pltpu.VMEM((2, SC_CHUNK, D_MODEL), F32), pltpu.VMEM((SC_CHUNK, SC_LANES), F32),
                       pltpu.VMEM((ne,), F32), pltpu.SemaphoreType.DMA((2,))],
        compiler_params=pltpu.CompilerParams(needs_layout_passes=False),
        name="sc_udot")
    def run(u_hbm, idx_hbm, x_hbm, pre_hbm, idx_v, x_v, rows, accs, pre_v, sems):
        wid = lax.axis_index("s") * 2 + lax.axis_index("c")
        lanes = lax.iota(jnp.int32, SC_LANES)

        def gather(c, b):
            return pltpu.make_async_copy(u_hbm.at[idx_v.at[pl.ds(c * SC_CHUNK, SC_CHUNK)]], rows.at[b], sems.at[b])

        @pl.loop(0, per_w)
        def _(tt):
            tok = wid * per_w + tt
            pltpu.sync_copy(idx_hbm.at[tok], idx_v)
            pltpu.sync_copy(x_hbm.at[tok], x_v)
            gather(0, 0).start()
            for c in range(nchunk):
                b = c % 2
                if c + 1 < nchunk:
                    gather(c + 1, 1 - b).start()
                gather(c, b).wait()
                for g in range(SC_CHUNK // 4):
                    def body(j, acc):
                        xv = x_v[pl.ds(j * SC_LANES, SC_LANES)]
                        return tuple(acc[q] + rows[b, g * 4 + q, pl.ds(j * SC_LANES, SC_LANES)] * xv
                                     for q in range(4))
                    acc = plsc.parallel_loop(0, nl, unroll=4,
                                             carry=tuple(jnp.zeros((SC_LANES,), F32) for _ in range(4)))(body)
                    for q in range(4):
                        accs[g * 4 + q, :] = acc[q]
                tot = jnp.zeros((SC_LANES,), F32)
                for l in range(SC_LANES):
                    tot = tot + plsc.load_gather(accs, [lanes, jnp.full((SC_LANES,), l, jnp.int32)])
                pre_v[pl.ds(c * SC_CHUNK, SC_CHUNK)] = tot
            pltpu.sync_copy(pre_v, pre_hbm.at[tok])

    return run(u_table, idx, x)


def _sc_vsum(v_table, idx, wgt):
    t, ne = idx.shape
    per_w = t // SC_WORKERS
    nchunk = ne // SC_CHUNK
    block = 16 * SC_LANES
    mesh = plsc.VectorSubcoreMesh(core_axis_name="c", subcore_axis_name="s")

    @functools.partial(
        pl.kernel, out_type=jax.ShapeDtypeStruct((t, D_MODEL), F32), mesh=mesh,
        scratch_types=[pltpu.VMEM((ne,), jnp.int32), pltpu.VMEM((ne,), F32),
                       pltpu.VMEM((2, SC_CHUNK, D_MODEL), F32), pltpu.VMEM((D_MODEL,), F32),
                       pltpu.SemaphoreType.DMA((2,))],
        compiler_params=pltpu.CompilerParams(needs_layout_passes=False),
        name="sc_vsum")
    def run(v_hbm, idx_hbm, w_hbm, out_hbm, idx_v, w_v, rows, out_v, sems):
        wid = lax.axis_index("s") * 2 + lax.axis_index("c")

        def gather(c, b):
            return pltpu.make_async_copy(v_hbm.at[idx_v.at[pl.ds(c * SC_CHUNK, SC_CHUNK)]], rows.at[b], sems.at[b])

        @pl.loop(0, per_w)
        def _(tt):
            tok = wid * per_w + tt
            pltpu.sync_copy(idx_hbm.at[tok], idx_v)
            pltpu.sync_copy(w_hbm.at[tok], w_v)
            gather(0, 0).start()
            for c in range(nchunk):
                b = c % 2
                if c + 1 < nchunk:
                    gather(c + 1, 1 - b).start()
                gather(c, b).wait()
                for d in range(D_MODEL // block):
                    def body(k, acc):
                        wk = plsc.load_gather(w_v, [jnp.full((SC_LANES,), c * SC_CHUNK, jnp.int32) + k])
                        return tuple(acc[j] + rows[b, k, pl.ds(d * block + j * SC_LANES, SC_LANES)] * wk
                                     for j in range(16))
                    if c == 0:
                        init = tuple(jnp.zeros((SC_LANES,), F32) for _ in range(16))
                    else:
                        init = tuple(out_v[pl.ds(d * block + j * SC_LANES, SC_LANES)] for j in range(16))
                    acc = plsc.parallel_loop(0, SC_CHUNK, carry=init)(body)
                    for j in range(16):
                        out_v[pl.ds(d * block + j * SC_LANES, SC_LANES)] = acc[j]
            pltpu.sync_copy(out_v, out_hbm.at[tok])

    return run(v_table, idx, wgt)


def _expert_slabs(u_table, v_table):
    n = u_table.shape[0]
    u = u_table.reshape(n, 1, SLAB_ROWS, LANE)
    v = v_table.reshape(n, 1, SLAB_ROWS, LANE)
    return jnp.concatenate([u, v], axis=1)


def _rope_tables(seq):
    rows = seq // GRID_W
    r = jnp.repeat(jnp.arange(rows, dtype=F32), GRID_W)
    col = jnp.tile(jnp.arange(GRID_W, dtype=F32), rows)
    nf = ROPE_AXIS // 2
    freqs = jnp.power(ROPE_BASE, -jnp.arange(nf, dtype=F32) / nf)
    ar, ac = r[:, None] * freqs, col[:, None] * freqs
    pad = jnp.zeros((seq, LANE - MLA_ROPE), F32)
    cos = jnp.concatenate([jnp.cos(ar), jnp.cos(ar), jnp.cos(ac), jnp.cos(ac), pad], axis=-1)
    sin = jnp.concatenate([-jnp.sin(ar), jnp.sin(ar), -jnp.sin(ac), jnp.sin(ac), pad], axis=-1)
    return cos, sin


def _prep_weights(norm1_w, w_in, q_norm_w, w_uq, kv_norm_w, w_ukv, q_head_norm_w, k_head_norm_w,
                  ret_gn_w, w_out, norm2_w, w_pq, sub_keys1, sub_keys2):
    cut = Q_RANK + KV_RANK + MLA_ROPE
    w_in_p = jnp.concatenate([w_in[:, :cut], jnp.zeros((D_MODEL, LANE - MLA_ROPE), F32), w_in[:, cut:]], axis=1)
    wq = w_uq.reshape(Q_RANK, MLA_HEADS, MLA_QK)
    wq = jnp.pad(wq, ((0, 0), (0, 0), (0, HEAD_PAD - MLA_QK))).reshape(Q_RANK, MLA_HEADS * HEAD_PAD)

    def head_w(v):
        return jnp.pad(v, (0, HEAD_PAD - MLA_QK)).reshape(1, HEAD_PAD)

    return {
        "norm1_w": norm1_w.reshape(1, -1), "w_in": w_in_p.astype(BF16),
        "q_norm_w": q_norm_w.reshape(1, -1), "kv_norm_w": kv_norm_w.reshape(1, -1),
        "w_uq": wq.astype(BF16), "w_ukv": w_ukv.astype(BF16),
        "q_head_norm_w": head_w(q_head_norm_w), "k_head_norm_w": head_w(k_head_norm_w),
        "ret_gn_w": ret_gn_w.reshape(1, -1), "w_out": w_out.astype(BF16), "norm2_w": norm2_w.reshape(1, -1),
        "w_pq": w_pq.astype(BF16).reshape(D_MODEL, PEER_HEADS, PEER_QDIM).transpose(1, 2, 0), "sub_keys1": sub_keys1.astype(BF16), "sub_keys2": sub_keys2.astype(BF16),
    }


def _trunk(x, mod, w, decay_logit, tab, ctx):
    batch, seq, _ = x.shape
    x2 = x.reshape(batch * seq, D_MODEL)
    use_rope = ctx is not None
    if use_rope:
        cos, sin = _rope_tables(seq)
    else:
        cos = sin = jnp.zeros((ROW_TILE, LANE), F32)
    ckv, krope, q, k, v, zr = _in_proj(x2, mod, w, cos, sin, seq, use_rope)
    if use_rope:
        ckv_c, krope_c, state_c = ctx
        past = ckv_c.shape[1]
        kr_c = jnp.pad(krope_c.reshape(batch * past, MLA_ROPE), ((0, 0), (0, LANE - MLA_ROPE)))
        ctx_kv = _kv_up(ckv_c.reshape(batch * past, KV_RANK), kr_c, w)
        s0 = state_c.reshape(batch * 2, RET_HEADS, RET_DK, RET_DV)
    else:
        ctx_kv, s0 = None, None
    att = _attention(q, k, v, batch, seq, ctx_kv)
    o_f, o_b, s_f, s_b = _retention(zr, decay_logit, batch, seq, s0)
    x1, h2 = _mix(x2, att, o_f, o_b, zr, mod, w, seq)
    ntiles = (batch * seq) // GATHER_TILE
    sc_tiles = ntiles * SC_SHARE[0] // SC_SHARE[1] if use_rope else 0
    if sc_tiles == 0:
        y = _peer(tab, h2, x1, mod, w, seq, 0, ntiles)
    else:
        tc_tiles = ntiles - sc_tiles
        first = tc_tiles * TC_FIRST[0] // TC_FIRST[1]
        gt, et = _peer_score(h2, w, tc_tiles, sc_tiles)
        idx = et.T.astype(jnp.int32)
        pre = _sc_udot(w["u_table"], idx, h2[tc_tiles * GATHER_TILE:])
        y = _peer(tab, h2, x1, mod, w, seq, 0, first)
        out = _sc_vsum(w["v_table"], idx, _peer_act(pre, gt.T, y))
        y = _peer(tab, h2, x1, mod, w, seq, first, tc_tiles - first, y_prev=y)
        y = _peer_residual(x1, out, mod, seq, tc_tiles, y)
    states = jnp.stack([s_f, s_b], axis=1)
    return y.reshape(batch, seq, D_MODEL), ckv.reshape(batch, seq, KV_RANK), krope.reshape(batch, seq, MLA_ROPE), states


def kernel(x_prompt, x_sample, c, cache_ckv, cache_krope, state_ret, c_ctx, w_ada, b_ada, norm1_w, w_in,
           q_norm_w, w_uq, kv_norm_w, w_ukv, q_head_norm_w, k_head_norm_w, ret_decay_logit, ret_gn_w,
           w_out, norm2_w, w_pq, sub_keys1, sub_keys2, u_table, v_table):
    depth = w_ada.shape[0]
    nb_ctx = x_prompt.shape[0]
    nb_lat = x_sample.shape[0]
    y_prompt, y_sample = x_prompt, x_sample
    ckv_list, krope_list, ret_list = [], [], []
    for l in range(depth):
        cond_rows = -(-(nb_lat + 1) // 8) * 8
        cond = jnp.concatenate([c, c_ctx[None, :], jnp.zeros((cond_rows - nb_lat - 1, D_MODEL), F32)], axis=0)
        mod = _ada(cond, w_ada[l], b_ada[l])
        mod_lat = mod[:nb_lat].reshape(nb_lat, 6, D_MODEL)
        mod_ctx = jnp.broadcast_to(mod[nb_lat].reshape(1, 6, D_MODEL), (nb_ctx, 6, D_MODEL))
        w = _prep_weights(norm1_w[l], w_in[l], q_norm_w[l], w_uq[l], kv_norm_w[l], w_ukv[l], q_head_norm_w[l],
                          k_head_norm_w[l], ret_gn_w[l], w_out[l], norm2_w[l], w_pq[l], sub_keys1[l], sub_keys2[l])
        tab = _expert_slabs(u_table[l], v_table[l])
        w["u_table"], w["v_table"] = u_table[l], v_table[l]
        y_prompt, ckv_l, krope_l, ret_l = _trunk(y_prompt, mod_ctx, w, ret_decay_logit[l], tab, None)
        ckv_list.append(ckv_l)
        krope_list.append(krope_l)
        ret_list.append(ret_l)
        y_sample, _, _, _ = _trunk(y_sample, mod_lat, w, ret_decay_logit[l], tab,
                                   (cache_ckv[:, l], cache_krope[:, l], state_ret[:, l]))
    return (y_prompt, y_sample, jnp.stack(ckv_list, axis=1), jnp.stack(krope_list, axis=1),
            jnp.stack(ret_list, axis=1))
```

```python
import functools

import jax
import jax.numpy as jnp
from jax import lax
from jax.experimental import pallas as pl
from jax.experimental.pallas import tpu as pltpu
from jax.experimental.pallas import tpu_sc as plsc

F32 = jnp.float32
BF16 = jnp.bfloat16

D_MODEL = 1024
GRID_W = 64
MLA_HEADS = 4
MLA_NOPE = 128
MLA_ROPE = 64
MLA_QK = MLA_NOPE + MLA_ROPE
MLA_V = 128
Q_RANK = 512
KV_RANK = 256
ROPE_AXIS = MLA_ROPE // 2
ROPE_BASE = 10000.0
RET_HEADS = 4
RET_DK = 128
RET_DV = 128
RET_CHUNK = 128
PEER_HEADS = 8
PEER_QDIM = 256
PEER_HALF = PEER_QDIM // 2
N_KEYS = 128
PEER_TOPK = 16
EPS = 1e-6

LANE = 128
HEAD_PAD = 2 * LANE
RET_W = 4 * RET_HEADS * RET_DK
IN_PAD = Q_RANK + KV_RANK + LANE + RET_W
VMEM_LIMIT = 48 * 1024 * 1024

ROW_TILE = 256
Q_TILE = 256
GATHER_TILE = 128
GATHER_SLOTS = 4
GATHER_AHEAD = GATHER_SLOTS - 1
EXPERTS_PER_TOKEN = PEER_HEADS * PEER_TOPK
SLAB_ROWS = D_MODEL // LANE


def _params(*sem):
    return pltpu.CompilerParams(dimension_semantics=sem, vmem_limit_bytes=VMEM_LIMIT)


def _rms(x, w):
    return x * lax.rsqrt(jnp.mean(x * x, axis=-1, keepdims=True) + EPS) * w


def _mm(a, b):
    return jnp.dot(a.astype(BF16), b.astype(BF16), preferred_element_type=F32)


def _mm_nt(a, b):
    return lax.dot_general(a.astype(BF16), b.astype(BF16), (((1,), (1,)), ((), ())),
                           preferred_element_type=F32)


def _mm_tn(a, b):
    return lax.dot_general(a.astype(BF16), b.astype(BF16), (((0,), (0,)), ((), ())),
                           preferred_element_type=F32)


def _ada_kernel(c_ref, w_ref, b_ref, o_ref):
    c = c_ref[...]
    o_ref[...] = _mm(c * jax.nn.sigmoid(c), w_ref[...]) + b_ref[...]


def _ada(cond, w_ada, b_ada):
    rows, d = cond.shape
    n = w_ada.shape[1]
    tn = 1536
    return pl.pallas_call(
        _ada_kernel,
        grid=(n // tn,),
        in_specs=[pl.BlockSpec((rows, d), lambda j: (0, 0)),
                  pl.BlockSpec((d, tn), lambda j: (0, j)),
                  pl.BlockSpec((1, tn), lambda j: (0, j))],
        out_specs=pl.BlockSpec((rows, tn), lambda j: (0, j)),
        out_shape=jax.ShapeDtypeStruct((rows, n), F32),
        compiler_params=_params("arbitrary"),
        name="ada",
    )(cond, w_ada, b_ada.reshape(1, n))


def _rope_tile(x, cos, sin):
    lane = lax.broadcasted_iota(jnp.int32, x.shape, 1)
    partner = jnp.where((lane % 32) < 16, pltpu.roll(x, LANE - 16, 1), pltpu.roll(x, 16, 1))
    return x * cos + partner * sin


def _kv_heads(kv, kr, khw, cos, sin, use_rope):
    krw = kr * khw[:, LANE:]
    if use_rope:
        krw = _rope_tile(krw, cos, sin)
    ssq_r = jnp.sum(kr * kr, axis=-1, keepdims=True)
    ks, vs = [], []
    for hd in range(MLA_HEADS):
        kn = kv[:, hd * HEAD_PAD: hd * HEAD_PAD + LANE]
        r = lax.rsqrt((jnp.sum(kn * kn, axis=-1, keepdims=True) + ssq_r) / MLA_QK + EPS)
        ks += [kn * r * khw[:, :LANE], krw * r]
        vs.append(kv[:, hd * HEAD_PAD + LANE: (hd + 1) * HEAD_PAD])
    return jnp.concatenate(ks, axis=-1).astype(BF16), jnp.concatenate(vs, axis=-1).astype(BF16)


def _inproj_kernel(x_ref, mod_ref, n1w_ref, win_ref, qnw_ref, kvnw_ref, wuq_ref, wukv_ref, qhw_ref, khw_ref,
                   cos_ref, sin_ref, ckv_ref, krope_ref, q_ref, k_ref, v_ref, zr_ref, *, use_rope):
    m = mod_ref[0]
    h = _rms(x_ref[...], n1w_ref[...]) * (1.0 + m[1:2]) + m[0:1]
    z = _mm(h, win_ref[...])
    kr = z[:, Q_RANK + KV_RANK: Q_RANK + KV_RANK + LANE]
    zr_ref[...] = z[:, Q_RANK + KV_RANK + LANE:]
    ckvn = _rms(z[:, Q_RANK: Q_RANK + KV_RANK], kvnw_ref[...])
    ckv_ref[...] = ckvn
    krope_ref[...] = kr[:, :MLA_ROPE]
    cos, sin = cos_ref[...], sin_ref[...]

    q = _mm(_rms(z[:, :Q_RANK], qnw_ref[...]), wuq_ref[...])
    qhw = qhw_ref[...]
    qs = []
    for hd in range(MLA_HEADS):
        qn = q[:, hd * HEAD_PAD: hd * HEAD_PAD + LANE]
        qr = q[:, hd * HEAD_PAD + LANE: (hd + 1) * HEAD_PAD]
        ssq = jnp.sum(qn * qn, axis=-1, keepdims=True) + jnp.sum(qr * qr, axis=-1, keepdims=True)
        r = lax.rsqrt(ssq / MLA_QK + EPS)
        qrw = qr * qhw[:, LANE:]
        if use_rope:
            qrw = _rope_tile(qrw, cos, sin)
        qs += [qn * r * qhw[:, :LANE], qrw * r]
    q_ref[...] = jnp.concatenate(qs, axis=-1).astype(BF16)

    k, v = _kv_heads(_mm(ckvn, wukv_ref[...]), kr, khw_ref[...], cos, sin, use_rope)
    k_ref[...] = k
    v_ref[...] = v


def _in_proj(x, mod, w, cos, sin, seq, use_rope):
    t = x.shape[0]
    tm = ROW_TILE
    per_batch = seq // tm
    const = lambda i: (0, 0)
    row = lambda i: (i, 0)
    if use_rope:
        pos = lambda i: (i % per_batch, 0)
    else:
        pos = const
    hp = MLA_HEADS * HEAD_PAD
    return pl.pallas_call(
        functools.partial(_inproj_kernel, use_rope=use_rope),
        grid=(t // tm,),
        in_specs=[pl.BlockSpec((tm, D_MODEL), row),
                  pl.BlockSpec((1, 6, D_MODEL), lambda i: (i // per_batch, 0, 0)),
                  pl.BlockSpec((1, D_MODEL), const),
                  pl.BlockSpec((D_MODEL, IN_PAD), const),
                  pl.BlockSpec((1, Q_RANK), const),
                  pl.BlockSpec((1, KV_RANK), const),
                  pl.BlockSpec((Q_RANK, hp), const),
                  pl.BlockSpec((KV_RANK, hp), const),
                  pl.BlockSpec((1, HEAD_PAD), const),
                  pl.BlockSpec((1, HEAD_PAD), const),
                  pl.BlockSpec((tm, LANE), pos),
                  pl.BlockSpec((tm, LANE), pos)],
        out_specs=[pl.BlockSpec((tm, KV_RANK), row),
                   pl.BlockSpec((tm, MLA_ROPE), row),
                   pl.BlockSpec((tm, hp), row),
                   pl.BlockSpec((tm, hp), row),
                   pl.BlockSpec((tm, MLA_HEADS * MLA_V), row),
                   pl.BlockSpec((tm, RET_W), row)],
        out_shape=[jax.ShapeDtypeStruct((t, KV_RANK), F32),
                   jax.ShapeDtypeStruct((t, MLA_ROPE), F32),
                   jax.ShapeDtypeStruct((t, hp), BF16),
                   jax.ShapeDtypeStruct((t, hp), BF16),
                   jax.ShapeDtypeStruct((t, MLA_HEADS * MLA_V), BF16),
                   jax.ShapeDtypeStruct((t, RET_W), F32)],
        compiler_params=_params("arbitrary"),
        name="in_proj",
    )(x, mod, w["norm1_w"], w["w_in"], w["q_norm_w"], w["kv_norm_w"], w["w_uq"], w["w_ukv"],
      w["q_head_norm_w"], w["k_head_norm_w"], cos, sin)


def _kvup_kernel(ckv_ref, kr_ref, wukv_ref, khw_ref, k_ref, v_ref):
    k, v = _kv_heads(_mm(ckv_ref[...], wukv_ref[...]), kr_ref[...], khw_ref[...], None, None, False)
    k_ref[...] = k
    v_ref[...] = v


def _kv_up(ckv, kr, w):
    t = ckv.shape[0]
    tm = ROW_TILE
    hp = MLA_HEADS * HEAD_PAD
    const = lambda i: (0, 0)
    row = lambda i: (i, 0)
    return pl.pallas_call(
        _kvup_kernel,
        grid=(t // tm,),
        in_specs=[pl.BlockSpec((tm, KV_RANK), row),
                  pl.BlockSpec((tm, LANE), row),
                  pl.BlockSpec((KV_RANK, hp), const),
                  pl.BlockSpec((1, HEAD_PAD), const)],
        out_specs=[pl.BlockSpec((tm, hp), row),
                   pl.BlockSpec((tm, MLA_HEADS * MLA_V), row)],
        out_shape=[jax.ShapeDtypeStruct((t, hp), BF16),
                   jax.ShapeDtypeStruct((t, MLA_HEADS * MLA_V), BF16)],
        compiler_params=_params("arbitrary"),
        name="kv_up",
    )(ckv, kr, w["w_ukv"], w["k_head_norm_w"])


def _attn_kernel(*refs, has_ctx):
    if has_ctx:
        q_ref, k_ref, v_ref, kc_ref, vc_ref, o_ref = refs
    else:
        q_ref, k_ref, v_ref, o_ref = refs
    scale = MLA_QK ** -0.5
    q = q_ref[...]
    s = _mm_nt(q, k_ref[...]) * scale
    m = jnp.max(s, axis=-1, keepdims=True)
    if has_ctx:
        sc = _mm_nt(q, kc_ref[...]) * scale
        m = jnp.maximum(m, jnp.max(sc, axis=-1, keepdims=True))
    p = jnp.exp(s - m)
    den = jnp.sum(p, axis=-1, keepdims=True)
    o = _mm(p, v_ref[...])
    if has_ctx:
        pc = jnp.exp(sc - m)
        den = den + jnp.sum(pc, axis=-1, keepdims=True)
        o = o + _mm(pc, vc_ref[...])
    o_ref[...] = (o / den).astype(BF16)


def _attention(q, k, v, batch, seq, ctx_kv):
    tq = min(Q_TILE, seq)
    nq = seq // tq
    has_ctx = ctx_kv is not None
    qmap = lambda b, h, i: (b * nq + i, h)
    kvmap = lambda b, h, i: (b, h)
    in_specs = [pl.BlockSpec((tq, HEAD_PAD), qmap),
                pl.BlockSpec((seq, HEAD_PAD), kvmap),
                pl.BlockSpec((seq, MLA_V), kvmap)]
    args = [q, k, v]
    if has_ctx:
        kc, vc = ctx_kv
        past = kc.shape[0] // batch
        in_specs += [pl.BlockSpec((past, HEAD_PAD), kvmap), pl.BlockSpec((past, MLA_V), kvmap)]
        args += [kc, vc]
    return pl.pallas_call(
        functools.partial(_attn_kernel, has_ctx=has_ctx),
        grid=(batch, MLA_HEADS, nq),
        in_specs=in_specs,
        out_specs=pl.BlockSpec((tq, MLA_V), qmap),
        out_shape=jax.ShapeDtypeStruct((batch * seq, MLA_HEADS * MLA_V), BF16),
        compiler_params=_params("arbitrary", "arbitrary", "arbitrary"),
        name="attention",
    )(*args)


def _ret_kernel(*refs, has_s0):
    if has_s0:
        (lg_ref, qf_ref, kf_ref, vf_ref, qb_ref, kb_ref, vb_ref, s0f_ref, s0b_ref,
         of_ref, ob_ref, sf_out, sb_out, sf_scr, sb_scr) = refs
    else:
        (lg_ref, qf_ref, kf_ref, vf_ref, qb_ref, kb_ref, vb_ref,
         of_ref, ob_ref, sf_out, sb_out, sf_scr, sb_scr) = refs
    c = pl.program_id(1)
    cs = RET_CHUNK

    @pl.when(c == 0)
    def _():
        if has_s0:
            sf_scr[...] = s0f_ref[0]
            sb_scr[...] = s0b_ref[0]
        else:
            sf_scr[...] = jnp.zeros_like(sf_scr)
            sb_scr[...] = jnp.zeros_like(sb_scr)

    ii = lax.broadcasted_iota(jnp.int32, (cs, cs), 0).astype(F32)
    jj = lax.broadcasted_iota(jnp.int32, (cs, cs), 1).astype(F32)
    rel = ii - jj
    kscale = RET_DK ** -0.5

    def chunk(q, k, v, s, intra, qdec, kdec, cdec):
        a = _mm_nt(q, k) * intra
        o = _mm(a, v) + _mm(q, s) * qdec
        return o, s * cdec + _mm_tn(k * kdec, v)

    for hd in range(RET_HEADS):
        cols = slice(hd * RET_DK, (hd + 1) * RET_DK)
        lgf = jax.nn.log_sigmoid(jnp.full((1, LANE), lg_ref[0, hd], F32))
        lgb = jax.nn.log_sigmoid(jnp.full((1, LANE), lg_ref[1, hd], F32))

        intra_f = jnp.where(rel >= 0, jnp.exp(jnp.maximum(rel, 0.0) * lgf), 0.0)
        o_f, s_f = chunk(qf_ref[:, cols], kf_ref[:, cols] * kscale, vf_ref[:, cols], sf_scr[hd], intra_f,
                         jnp.exp((ii + 1.0) * lgf), jnp.exp((cs - 1.0 - ii) * lgf), jnp.exp(cs * lgf))
        of_ref[:, cols] = o_f
        sf_scr[hd] = s_f
        sf_out[0, hd] = s_f

        intra_b = jnp.where(rel <= 0, jnp.exp(jnp.maximum(-rel, 0.0) * lgb), 0.0)
        o_b, s_b = chunk(qb_ref[:, cols], kb_ref[:, cols] * kscale, vb_ref[:, cols], sb_scr[hd], intra_b,
                         jnp.exp((cs - ii) * lgb), jnp.exp(ii * lgb), jnp.exp(cs * lgb))
        ob_ref[:, cols] = o_b
        sb_scr[hd] = s_b
        sb_out[0, hd] = s_b


def _retention(zr, decay_logit, batch, seq, s0):
    cs = RET_CHUNK
    nc = seq // cs
    nh = RET_HEADS
    width = nh * RET_DK
    has_s0 = s0 is not None

    def fwd(col):
        return pl.BlockSpec((cs, width), lambda b, c: (b * nc + c, col))

    def bwd(col):
        return pl.BlockSpec((cs, width), lambda b, c: (b * nc + nc - 1 - c, col))

    def state(d):
        return pl.BlockSpec((1, nh, RET_DK, RET_DV), lambda b, c: (b * 2 + d, 0, 0, 0))

    state_out = pl.BlockSpec((1, nh, RET_DK, RET_DV), lambda b, c: (b, 0, 0, 0))
    in_specs = [pl.BlockSpec(memory_space=pltpu.SMEM), fwd(0), fwd(1), fwd(2), bwd(0), bwd(1), bwd(2)]
    args = [decay_logit, zr, zr, zr, zr, zr, zr]
    if has_s0:
        in_specs += [state(0), state(1)]
        args += [s0, s0]
    t = batch * seq
    o_f, o_b, s_f, s_b = pl.pallas_call(
        functools.partial(_ret_kernel, has_s0=has_s0),
        grid=(batch, nc),
        in_specs=in_specs,
        out_specs=[pl.BlockSpec((cs, width), lambda b, c: (b * nc + c, 0)),
                   pl.BlockSpec((cs, width), lambda b, c: (b * nc + nc - 1 - c, 0)),
                   state_out, state_out],
        out_shape=[jax.ShapeDtypeStruct((t, width), F32),
                   jax.ShapeDtypeStruct((t, width), F32),
                   jax.ShapeDtypeStruct((batch, nh, RET_DK, RET_DV), F32),
                   jax.ShapeDtypeStruct((batch, nh, RET_DK, RET_DV), F32)],
        scratch_shapes=[pltpu.VMEM((nh, RET_DK, RET_DV), F32), pltpu.VMEM((nh, RET_DK, RET_DV), F32)],
        compiler_params=_params("arbitrary", "arbitrary"),
        name="retention",
    )(*args)
    return o_f, o_b, s_f, s_b


def _mix_kernel(x_ref, att_ref, of_ref, ob_ref, gr_ref, mod_ref, gnw_ref, wout_ref, n2w_ref, x1_ref, h2_ref):
    o = of_ref[...] + ob_ref[...]
    parts = []
    for hd in range(RET_HEADS):
        oh = o[:, hd * RET_DV:(hd + 1) * RET_DV]
        d = oh - jnp.mean(oh, axis=-1, keepdims=True)
        parts.append(d * lax.rsqrt(jnp.mean(d * d, axis=-1, keepdims=True) + EPS))
    g = gr_ref[...]
    ret = (g * jax.nn.sigmoid(g)) * (jnp.concatenate(parts, axis=-1) * gnw_ref[...])
    na = MLA_HEADS * MLA_V
    mixed = _mm(att_ref[...], wout_ref[:na, :]) + _mm(ret, wout_ref[na:, :])
    m = mod_ref[0]
    x1 = x_ref[...] + m[2:3] * mixed
    x1_ref[...] = x1
    h2_ref[...] = _rms(x1, n2w_ref[...]) * (1.0 + m[4:5]) + m[3:4]


def _mix(x, att, o_f, o_b, zr, mod, w, seq):
    t = x.shape[0]
    tm = ROW_TILE
    per_batch = seq // tm
    const = lambda i: (0, 0)
    row = lambda i: (i, 0)
    half = RET_HEADS * RET_DV
    return pl.pallas_call(
        _mix_kernel,
        grid=(t // tm,),
        in_specs=[pl.BlockSpec((tm, D_MODEL), row),
                  pl.BlockSpec((tm, half), row),
                  pl.BlockSpec((tm, half), row),
                  pl.BlockSpec((tm, half), row),
                  pl.BlockSpec((tm, half), lambda i: (i, 3)),
                  pl.BlockSpec((1, 6, D_MODEL), lambda i: (i // per_batch, 0, 0)),
                  pl.BlockSpec((1, half), const),
                  pl.BlockSpec((D_MODEL, D_MODEL), const),
                  pl.BlockSpec((1, D_MODEL), const)],
        out_specs=[pl.BlockSpec((tm, D_MODEL), row), pl.BlockSpec((tm, D_MODEL), row)],
        out_shape=[jax.ShapeDtypeStruct((t, D_MODEL), F32), jax.ShapeDtypeStruct((t, D_MODEL), F32)],
        compiler_params=_params("arbitrary"),
        name="mix",
    )(x, att, o_f, o_b, zr, mod, w["ret_gn_w"], w["w_out"], w["norm2_w"])


def _top16(jobs):
    def body(r, carry):
        for s_ref, vals_ref, pay_ref, payload in jobs:
            n, cols = s_ref.shape
            rows = lax.broadcasted_iota(jnp.int32, (n, cols), 0).astype(F32)
            s = s_ref[...]
            m = jnp.max(s, axis=0, keepdims=True)
            pos = jnp.min(jnp.where(s == m, rows, float(n)), axis=0, keepdims=True)
            hit = rows == pos
            vals_ref[pl.ds(r, 1), :] = m
            if payload is None:
                pay_ref[pl.ds(r, 1), :] = pos
            else:
                pay_ref[pl.ds(r, 1), :] = jnp.sum(jnp.where(hit, payload, 0.0), axis=0, keepdims=True)
            s_ref[...] = jnp.where(hit, -jnp.inf, s)
        return carry

    lax.fori_loop(0, PEER_TOPK, body, 0)


PAIR_COUNTS = tuple(PEER_TOPK // (a + 1) for a in range(PEER_TOPK))
NUM_PAIRS = sum(PAIR_COUNTS)
PAIR_ROWS = -(-NUM_PAIRS // 8) * 8


def _score_head(hbt, wpqt, k1, k2, g_out, e_out, scr):
    s1_scr, s2_scr, c_scr, p_scr, v1_scr, i1_scr, v2_scr, i2_scr, vt_scr, it_scr = scr
    qt = jnp.dot(wpqt, hbt, preferred_element_type=F32)
    s1_scr[...] = _mm(k1, qt[:PEER_HALF, :])
    s2_scr[...] = _mm(k2, qt[PEER_HALF:, :])
    _top16([(s1_scr, v1_scr, i1_scr, None), (s2_scr, v2_scr, i2_scr, None)])
    c_scr[...] = jnp.full(c_scr.shape, -jnp.inf, F32)
    p_scr[...] = jnp.zeros(p_scr.shape, F32)
    off = 0
    for a, nb in enumerate(PAIR_COUNTS):
        c_scr[off:off + nb, :] = v1_scr[a:a + 1, :] + v2_scr[0:nb, :]
        p_scr[off:off + nb, :] = i1_scr[a:a + 1, :] * float(N_KEYS) + i2_scr[0:nb, :]
        off += nb
    _top16([(c_scr, vt_scr, it_scr, p_scr[...])])
    top = vt_scr[...]
    p = jnp.exp(top - jnp.max(top, axis=0, keepdims=True))
    g_out[...] = p / jnp.sum(p, axis=0, keepdims=True)
    e_out[...] = it_scr[...]


def _score_scratch(tb):
    k = PEER_TOPK
    return ([pltpu.VMEM((N_KEYS, tb), F32)] * 2 + [pltpu.VMEM((PAIR_ROWS, tb), F32)] * 2
            + [pltpu.VMEM((k, tb), F32)] * 6)


def _peer_kernel(tab_hbm, h2_ref, h2n_ref, wpq_ref, k1_ref, k2_ref, x1_ref, mod_ref, *rest, ntiles, aliased):
    y_ref, idx_smem, g_scr, e_scr, ei_scr, hb_scr = rest[int(aliased):int(aliased) + 6]
    scratch = rest[int(aliased) + 6:]
    _peer_body(tab_hbm, h2_ref, h2n_ref, wpq_ref, k1_ref, k2_ref, x1_ref, mod_ref, y_ref,
               idx_smem, g_scr, e_scr, ei_scr, hb_scr, *scratch, ntiles=ntiles)


def _peer_body(tab_hbm, h2_ref, h2n_ref, wpq_ref, k1_ref, k2_ref, x1_ref, mod_ref, y_ref,
               idx_smem, g_scr, e_scr, ei_scr, hb_scr, *scratch, ntiles):
    bufs = scratch[:GATHER_SLOTS]
    sem_rows, sem_idx = scratch[GATHER_SLOTS:GATHER_SLOTS + 2]
    score_scr = scratch[GATHER_SLOTS + 2:]
    i = pl.program_id(0)
    tb = GATHER_TILE
    ns = GATHER_SLOTS
    ahead = GATHER_AHEAD
    ne = EXPERTS_PER_TOKEN
    slab = SLAB_ROWS
    topk = PEER_TOPK
    groups_per_head = tb // (ns * PEER_HEADS)
    cur = i % 2
    nxt = 1 - cur
    more = i + 1 < ntiles

    def score(src_ref, hd, slot):
        @pl.when(hd == 0)
        def _():
            hb_scr[...] = src_ref[...].T.astype(BF16)
        head_rows = pl.ds(pl.multiple_of(hd * topk, topk), topk)
        _score_head(hb_scr[...], wpq_ref[hd], k1_ref[...], k2_ref[...],
                    g_scr.at[slot, head_rows], e_scr.at[head_rows], score_scr)

    def publish(slot):
        ei_scr[...] = e_scr[...].T.astype(jnp.int32)
        copy = pltpu.make_async_copy(ei_scr, idx_smem.at[slot], sem_idx.at[0])
        copy.start()
        copy.wait()

    def issue(islot, tok, rslot):
        for kk in range(ne):
            ex = idx_smem[islot, tok, kk]
            pltpu.make_async_copy(tab_hbm.at[ex], bufs[rslot].at[:, pl.ds(kk * slab, slab), :],
                                  sem_rows.at[rslot]).start(priority=kk % 2)

    def wait_rows(rslot):
        pltpu.make_async_copy(bufs[rslot], bufs[rslot], sem_rows.at[rslot]).wait()

    @pl.when(i == 0)
    def _():
        def first(hd, carry):
            score(h2_ref, hd, 0)
            return carry
        lax.fori_loop(0, PEER_HEADS, first, 0)
        publish(0)
        for tok in range(ahead):
            issue(0, tok, tok)

    g2 = mod_ref[0][5:6]
    lane = lax.broadcasted_iota(jnp.int32, (ne, tb), 1)

    def compute(j, rslot):
        rows = bufs[rslot]
        x_row = h2_ref[pl.ds(j, 1), :]
        acc = None
        for c in range(slab):
            u_c = rows[0, pl.ds(c, ne, stride=slab), :]
            term = u_c * x_row[:, c * LANE:(c + 1) * LANE]
            acc = term if acc is None else acc + term
        pre = jnp.sum(acc, axis=-1, keepdims=True)
        act = 0.5 * pre * (1.0 + lax.erf(pre * (2.0 ** -0.5)))
        gate = jnp.sum(jnp.where(lane == j, g_scr[cur], 0.0), axis=-1, keepdims=True)
        wgt = gate * act
        outs = []
        for c in range(slab):
            v_c = rows[1, pl.ds(c, ne, stride=slab), :]
            outs.append(jnp.sum(v_c * wgt, axis=0, keepdims=True))
        out = jnp.concatenate(outs, axis=-1)
        y_ref[pl.ds(j, 1), :] = x1_ref[pl.ds(j, 1), :] + g2 * out

    def phase(hd, carry):
        score(h2n_ref, hd, nxt)

        @pl.when(hd == PEER_HEADS - 1)
        def _():
            publish(nxt)

        def group(gq, inner):
            g = hd * groups_per_head + gq
            for u in range(ns):
                j = g * ns + u
                wait_rows(u)
                jj = j + ahead
                over = jj >= tb
                issue(jnp.where(over, nxt, cur), jnp.where(over, jj - tb, jj), (u + ahead) % ns)
                compute(j, u)
            return inner

        lax.fori_loop(0, groups_per_head, group, 0)
        return carry

    lax.fori_loop(0, PEER_HEADS, phase, 0)

    @pl.when(jnp.logical_not(more))
    def _():
        for u in range(ahead):
            wait_rows(u)


def _peer(tab, h2, x1, mod, w, seq, tile0, ntiles, y_prev=None):
    t = h2.shape[0]
    tb = GATHER_TILE
    per_batch = seq // tb
    row = lambda i: (tile0 + i, 0)
    const = lambda i: (0, 0)
    in_specs = [pl.BlockSpec(memory_space=pl.ANY),
                pl.BlockSpec((tb, D_MODEL), row),
                pl.BlockSpec((tb, D_MODEL), lambda i: (tile0 + jnp.minimum(i + 1, ntiles - 1), 0)),
                pl.BlockSpec((PEER_HEADS, PEER_QDIM, D_MODEL), lambda i: (0, 0, 0)),
                pl.BlockSpec((N_KEYS, PEER_HALF), const),
                pl.BlockSpec((N_KEYS, PEER_HALF), const),
                pl.BlockSpec((tb, D_MODEL), row),
                pl.BlockSpec((1, 6, D_MODEL), lambda i: ((tile0 + i) // per_batch, 0, 0))]
    args = [tab, h2, h2, w["w_pq"], w["sub_keys1"], w["sub_keys2"], x1, mod]
    aliases = {}
    if y_prev is not None:
        in_specs.append(pl.BlockSpec(memory_space=pl.ANY))
        args.append(y_prev)
        aliases = {len(args) - 1: 0}
    return pl.pallas_call(
        functools.partial(_peer_kernel, ntiles=ntiles, aliased=y_prev is not None),
        grid=(ntiles,),
        in_specs=in_specs,
        input_output_aliases=aliases,
        out_specs=pl.BlockSpec((tb, D_MODEL), row),
        out_shape=jax.ShapeDtypeStruct((t, D_MODEL), F32),
        scratch_shapes=[pltpu.SMEM((2, tb, EXPERTS_PER_TOKEN), jnp.int32),
                        pltpu.VMEM((2, EXPERTS_PER_TOKEN, tb), F32),
                        pltpu.VMEM((EXPERTS_PER_TOKEN, tb), F32),
                        pltpu.VMEM((tb, EXPERTS_PER_TOKEN), jnp.int32),
                        pltpu.VMEM((D_MODEL, tb), BF16)]
                       + [pltpu.VMEM((2, EXPERTS_PER_TOKEN * SLAB_ROWS, LANE), F32)] * GATHER_SLOTS
                       + [pltpu.SemaphoreType.DMA((GATHER_SLOTS,)), pltpu.SemaphoreType.DMA((1,))]
                       + _score_scratch(tb),
        compiler_params=pltpu.CompilerParams(dimension_semantics=("arbitrary",),
                                             vmem_limit_bytes=VMEM_LIMIT,
                                             disable_bounds_checks=True),
        name="peer",
    )(*args)


def _score_kernel(h2_ref, wpq_ref, k1_ref, k2_ref, g_ref, e_ref, hb_scr, *score_scr):
    hb_scr[...] = h2_ref[...].T.astype(BF16)

    def head(hd, carry):
        head_rows = pl.ds(pl.multiple_of(hd * PEER_TOPK, PEER_TOPK), PEER_TOPK)
        _score_head(hb_scr[...], wpq_ref[hd], k1_ref[...], k2_ref[...], g_ref.at[head_rows], e_ref.at[head_rows],
                    score_scr)
        return carry

    lax.fori_loop(0, PEER_HEADS, head, 0)


def _peer_score(h2, w, tile0, ntiles):
    tb = GATHER_TILE
    const = lambda i: (0, 0)
    out = pl.BlockSpec((EXPERTS_PER_TOKEN, tb), lambda i: (0, i))
    return pl.pallas_call(
        _score_kernel,
        grid=(ntiles,),
        in_specs=[pl.BlockSpec((tb, D_MODEL), lambda i: (tile0 + i, 0)),
                  pl.BlockSpec((PEER_HEADS, PEER_QDIM, D_MODEL), lambda i: (0, 0, 0)),
                  pl.BlockSpec((N_KEYS, PEER_HALF), const),
                  pl.BlockSpec((N_KEYS, PEER_HALF), const)],
        out_specs=[out, out],
        out_shape=[jax.ShapeDtypeStruct((EXPERTS_PER_TOKEN, ntiles * tb), F32)] * 2,
        scratch_shapes=[pltpu.VMEM((D_MODEL, tb), BF16)] + _score_scratch(tb),
        compiler_params=_params("arbitrary"),
        name="peer_score",
    )(h2, w["w_pq"], w["sub_keys1"], w["sub_keys2"])


def _act_kernel(pre_ref, g_ref, after_ref, w_ref):
    del after_ref
    pre = pre_ref[...]
    w_ref[...] = g_ref[...] * (0.5 * pre * (1.0 + lax.erf(pre * (2.0 ** -0.5))))


def _peer_act(pre, gate, after):
    t, ne = pre.shape
    tm = 1024
    row = lambda i: (i, 0)
    return pl.pallas_call(
        _act_kernel,
        grid=(t // tm,),
        in_specs=[pl.BlockSpec((tm, ne), row), pl.BlockSpec((tm, ne), row), pl.BlockSpec(memory_space=pl.ANY)],
        out_specs=pl.BlockSpec((tm, ne), row),
        out_shape=jax.ShapeDtypeStruct((t, ne), F32),
        compiler_params=_params("arbitrary"),
        name="peer_act",
    )(pre, gate, after)


def _residual_kernel(x1_ref, out_ref, mod_ref, *rest):
    rest[-1][...] = x1_ref[...] + mod_ref[0][5:6] * out_ref[...]


def _peer_residual(x1, out, mod, seq, tile0, y_prev):
    tb = GATHER_TILE
    per_batch = seq // tb
    row = lambda i: (tile0 + i, 0)
    in_specs = [pl.BlockSpec((tb, D_MODEL), row),
                pl.BlockSpec((tb, D_MODEL), lambda i: (i, 0)),
                pl.BlockSpec((1, 6, D_MODEL), lambda i: ((tile0 + i) // per_batch, 0, 0))]
    args = [x1, out, mod]
    aliases = {}
    if y_prev is not None:
        in_specs.append(pl.BlockSpec(memory_space=pl.ANY))
        args.append(y_prev)
        aliases = {3: 0}
    return pl.pallas_call(
        _residual_kernel,
        grid=(out.shape[0] // tb,),
        in_specs=in_specs,
        input_output_aliases=aliases,
        out_specs=pl.BlockSpec((tb, D_MODEL), row),
        out_shape=jax.ShapeDtypeStruct(x1.shape, F32),
        compiler_params=_params("arbitrary"),
        name="peer_residual",
    )(*args)


SC_WORKERS = 32
SC_LANES = 16
SC_CHUNK = 16
SC_SHARE = (15, 32)
TC_FIRST = (13, 25)


def _sc_udot(u_table, idx, x):
    t, ne = idx.shape
    per_w = t // SC_WORKERS
    nchunk = ne // SC_CHUNK
    nl = D_MODEL // SC_LANES
    mesh = plsc.VectorSubcoreMesh(core_axis_name="c", subcore_axis_name="s")

    @functools.partial(
        pl.kernel, out_type=jax.ShapeDtypeStruct((t, ne), F32), mesh=mesh,
        scratch_types=[pltpu.VMEM((ne,), jnp.int32), pltpu.VMEM((D_MODEL,), F32),
                       pltpu.VMEM((2, SC_CHUNK, D_MODEL), F32), pltpu.VMEM((SC_CHUNK, SC_LANES), F32),
                       pltpu.VMEM((ne,), F32), pltpu.SemaphoreType.DMA((2,))],
        compiler_params=pltpu.CompilerParams(needs_layout_passes=False),
        name="sc_udot")
    def run(u_hbm, idx_hbm, x_hbm, pre_hbm, idx_v, x_v, rows, accs, pre_v, sems):
        wid = lax.axis_index("s") * 2 + lax.axis_index("c")
        lanes = lax.iota(jnp.int32, SC_LANES)

        def gather(c, b):
            return pltpu.make_async_copy(u_hbm.at[idx_v.at[pl.ds(c * SC_CHUNK, SC_CHUNK)]], rows.at[b], sems.at[b])

        @pl.loop(0, per_w)
        def _(tt):
            tok = wid * per_w + tt
            pltpu.sync_copy(idx_hbm.at[tok], idx_v)
            pltpu.sync_copy(x_hbm.at[tok], x_v)
            gather(0, 0).start()
            for c in range(nchunk):
                b = c % 2
                if c + 1 < nchunk:
                    gather(c + 1, 1 - b).start()
                gather(c, b).wait()
                for g in range(SC_CHUNK // 4):
                    def body(j, acc):
                        xv = x_v[pl.ds(j * SC_LANES, SC_LANES)]
                        return tuple(acc[q] + rows[b, g * 4 + q, pl.ds(j * SC_LANES, SC_LANES)] * xv
                                     for q in range(4))
                    acc = plsc.parallel_loop(0, nl, unroll=4,
                                             carry=tuple(jnp.zeros((SC_LANES,), F32) for _ in range(4)))(body)
                    for q in range(4):
                        accs[g * 4 + q, :] = acc[q]
                tot = jnp.zeros((SC_LANES,), F32)
                for l in range(SC_LANES):
                    tot = tot + plsc.load_gather(accs, [lanes, jnp.full((SC_LANES,), l, jnp.int32)])
                pre_v[pl.ds(c * SC_CHUNK, SC_CHUNK)] = tot
            pltpu.sync_copy(pre_v, pre_hbm.at[tok])

    return run(u_table, idx, x)


def _sc_vsum(v_table, idx, wgt):
    t, ne = idx.shape
    per_w = t // SC_WORKERS
    nchunk = ne // SC_CHUNK
    block = 16 * SC_LANES
    mesh = plsc.VectorSubcoreMesh(core_axis_name="c", subcore_axis_name="s")

    @functools.partial(
        pl.kernel, out_type=jax.ShapeDtypeStruct((t, D_MODEL), F32), mesh=mesh,
        scratch_types=[pltpu.VMEM((ne,), jnp.int32), pltpu.VMEM((ne,), F32),
                       pltpu.VMEM((2, SC_CHUNK, D_MODEL), F32), pltpu.VMEM((D_MODEL,), F32),
                       pltpu.SemaphoreType.DMA((2,))],
        compiler_params=pltpu.CompilerParams(needs_layout_passes=False),
        name="sc_vsum")
    def run(v_hbm, idx_hbm, w_hbm, out_hbm, idx_v, w_v, rows, out_v, sems):
        wid = lax.axis_index("s") * 2 + lax.axis_index("c")

        def gather(c, b):
            return pltpu.make_async_copy(v_hbm.at[idx_v.at[pl.ds(c * SC_CHUNK, SC_CHUNK)]], rows.at[b], sems.at[b])

        @pl.loop(0, per_w)
        def _(tt):
            tok = wid * per_w + tt
            pltpu.sync_copy(idx_hbm.at[tok], idx_v)
            pltpu.sync_copy(w_hbm.at[tok], w_v)
            gather(0, 0).start()
            for c in range(nchunk):
                b = c % 2
                if c + 1 < nchunk:
                    gather(c + 1, 1 - b).start()
                gather(c, b).wait()
                for d in range(D_MODEL // block):
                    def body(k, acc):
                        wk = plsc.load_gather(w_v, [jnp.full((SC_LANES,), c * SC_CHUNK, jnp.int32) + k])
                        return tuple(acc[j] + rows[b, k, pl.ds(d * block + j * SC_LANES, SC_LANES)] * wk
                                     for j in range(16))
                    if c == 0:
                        init = tuple(jnp.zeros((SC_LANES,), F32) for _ in range(16))
                    else:
                        init = tuple(out_v[pl.ds(d * block + j * SC_LANES, SC_LANES)] for j in range(16))
                    acc = plsc.parallel_loop(0, SC_CHUNK, carry=init)(body)
                    for j in range(16):
                        out_v[pl.ds(d * block + j * SC_LANES, SC_LANES)] = acc[j]
            pltpu.sync_copy(out_v, out_hbm.at[tok])

    return run(v_table, idx, wgt)


def _expert_slabs(u_table, v_table):
    n = u_table.shape[0]
    u = u_table.reshape(n, 1, SLAB_ROWS, LANE)
    v = v_table.reshape(n, 1, SLAB_ROWS, LANE)
    return jnp.concatenate([u, v], axis=1)


def _rope_tables(seq):
    rows = seq // GRID_W
    r = jnp.repeat(jnp.arange(rows, dtype=F32), GRID_W)
    col = jnp.tile(jnp.arange(GRID_W, dtype=F32), rows)
    nf = ROPE_AXIS // 2
    freqs = jnp.power(ROPE_BASE, -jnp.arange(nf, dtype=F32) / nf)
    ar, ac = r[:, None] * freqs, col[:, None] * freqs
    pad = jnp.zeros((seq, LANE - MLA_ROPE), F32)
    cos = jnp.concatenate([jnp.cos(ar), jnp.cos(ar), jnp.cos(ac), jnp.cos(ac), pad], axis=-1)
    sin = jnp.concatenate([-jnp.sin(ar), jnp.sin(ar), -jnp.sin(ac), jnp.sin(ac), pad], axis=-1)
    return cos, sin


def _prep_weights(norm1_w, w_in, q_norm_w, w_uq, kv_norm_w, w_ukv, q_head_norm_w, k_head_norm_w,
                  ret_gn_w, w_out, norm2_w, w_pq, sub_keys1, sub_keys2):
    cut = Q_RANK + KV_RANK + MLA_ROPE
    w_in_p = jnp.concatenate([w_in[:, :cut], jnp.zeros((D_MODEL, LANE - MLA_ROPE), F32), w_in[:, cut:]], axis=1)
    wq = w_uq.reshape(Q_RANK, MLA_HEADS, MLA_QK)
    wq = jnp.pad(wq, ((0, 0), (0, 0), (0, HEAD_PAD - MLA_QK))).reshape(Q_RANK, MLA_HEADS * HEAD_PAD)

    def head_w(v):
        return jnp.pad(v, (0, HEAD_PAD - MLA_QK)).reshape(1, HEAD_PAD)

    return {
        "norm1_w": norm1_w.reshape(1, -1), "w_in": w_in_p.astype(BF16),
        "q_norm_w": q_norm_w.reshape(1, -1), "kv_norm_w": kv_norm_w.reshape(1, -1),
        "w_uq": wq.astype(BF16), "w_ukv": w_ukv.astype(BF16),
        "q_head_norm_w": head_w(q_head_norm_w), "k_head_norm_w": head_w(k_head_norm_w),
        "ret_gn_w": ret_gn_w.reshape(1, -1), "w_out": w_out.astype(BF16), "norm2_w": norm2_w.reshape(1, -1),
        "w_pq": w_pq.astype(BF16).reshape(D_MODEL, PEER_HEADS, PEER_QDIM).transpose(1, 2, 0), "sub_keys1": sub_keys1.astype(BF16), "sub_keys2": sub_keys2.astype(BF16),
    }


def _trunk(x, mod, w, decay_logit, ctx):
    batch, seq, _ = x.shape
    x2 = x.reshape(batch * seq, D_MODEL)
    use_rope = ctx is not None
    if use_rope:
        cos, sin = _rope_tables(seq)
    else:
        cos = sin = jnp.zeros((ROW_TILE, LANE), F32)
    ckv, krope, q, k, v, zr = _in_proj(x2, mod, w, cos, sin, seq, use_rope)
    if use_rope:
        ckv_c, krope_c, state_c = ctx
        past = ckv_c.shape[1]
        kr_c = jnp.pad(krope_c.reshape(batch * past, MLA_ROPE), ((0, 0), (0, LANE - MLA_ROPE)))
        ctx_kv = _kv_up(ckv_c.reshape(batch * past, KV_RANK), kr_c, w)
        s0 = state_c.reshape(batch * 2, RET_HEADS, RET_DK, RET_DV)
    else:
        ctx_kv, s0 = None, None
    att = _attention(q, k, v, batch, seq, ctx_kv)
    o_f, o_b, s_f, s_b = _retention(zr, decay_logit, batch, seq, s0)
    x1, h2 = _mix(x2, att, o_f, o_b, zr, mod, w, seq)
    states = jnp.stack([s_f, s_b], axis=1)
    return x1, h2, ckv.reshape(batch, seq, KV_RANK), krope.reshape(batch, seq, MLA_ROPE), states


def _sc_select(h2, w, tile0, ntiles):
    gt, et = _peer_score(h2, w, tile0, ntiles)
    return et.T.astype(jnp.int32), gt.T


def _peer_split(x_ctx, x_lat, mod_ctx, mod_lat, seq_ctx, seq_lat, w, tab):
    (x1c, h2c), (x1l, h2l) = x_ctx, x_lat
    tiles_c = h2c.shape[0] // GATHER_TILE
    tiles_l = h2l.shape[0] // GATHER_TILE
    sc_tiles = tiles_l * SC_SHARE[0] // SC_SHARE[1]
    tc_tiles = tiles_l - sc_tiles
    first = tc_tiles * TC_FIRST[0] // TC_FIRST[1]

    idx_c, gate_c = _sc_select(h2c, w, 0, tiles_c)
    pre_c = _sc_udot(w["u_table"], idx_c, h2c)
    out_c = _sc_vsum(w["v_table"], idx_c, _peer_act(pre_c, gate_c, h2l))

    idx_l, gate_l = _sc_select(h2l, w, tc_tiles, sc_tiles)
    pre_l = _sc_udot(w["u_table"], idx_l, h2l[tc_tiles * GATHER_TILE:])
    y = _peer(tab, h2l, x1l, mod_lat, w, seq_lat, 0, first)
    out_l = _sc_vsum(w["v_table"], idx_l, _peer_act(pre_l, gate_l, y))
    y = _peer(tab, h2l, x1l, mod_lat, w, seq_lat, first, tc_tiles - first, y_prev=y)
    y_lat = _peer_residual(x1l, out_l, mod_lat, seq_lat, tc_tiles, y)
    y_ctx = _peer_residual(x1c, out_c, mod_ctx, seq_ctx, 0, None)
    return y_ctx, y_lat


def kernel(x_prompt, x_sample, c, cache_ckv, cache_krope, state_ret, c_ctx, w_ada, b_ada, norm1_w, w_in,
           q_norm_w, w_uq, kv_norm_w, w_ukv, q_head_norm_w, k_head_norm_w, ret_decay_logit, ret_gn_w,
           w_out, norm2_w, w_pq, sub_keys1, sub_keys2, u_table, v_table):
    depth = w_ada.shape[0]
    nb_ctx = x_prompt.shape[0]
    nb_lat = x_sample.shape[0]
    y_prompt, y_sample = x_prompt, x_sample
    ckv_list, krope_list, ret_list = [], [], []
    for l in range(depth):
        cond_rows = -(-(nb_lat + 1) // 8) * 8
        cond = jnp.concatenate([c, c_ctx[None, :], jnp.zeros((cond_rows - nb_lat - 1, D_MODEL), F32)], axis=0)
        mod = _ada(cond, w_ada[l], b_ada[l])
        mod_lat = mod[:nb_lat].reshape(nb_lat, 6, D_MODEL)
        mod_ctx = jnp.broadcast_to(mod[nb_lat].reshape(1, 6, D_MODEL), (nb_ctx, 6, D_MODEL))
        w = _prep_weights(norm1_w[l], w_in[l], q_norm_w[l], w_uq[l], kv_norm_w[l], w_ukv[l], q_head_norm_w[l],
                          k_head_norm_w[l], ret_gn_w[l], w_out[l], norm2_w[l], w_pq[l], sub_keys1[l], sub_keys2[l])
        tab = _expert_slabs(u_table[l], v_table[l])
        w["u_table"], w["v_table"] = u_table[l], v_table[l]
        x1c, h2c, ckv_l, krope_l, ret_l = _trunk(y_prompt, mod_ctx, w, ret_decay_logit[l], None)
        ckv_list.append(ckv_l)
        krope_list.append(krope_l)
        ret_list.append(ret_l)
        x1l, h2l, _, _, _ = _trunk(y_sample, mod_lat, w, ret_decay_logit[l],
                                   (cache_ckv[:, l], cache_krope[:, l], state_ret[:, l]))
        y_ctx, y_lat = _peer_split((x1c, h2c), (x1l, h2l), mod_ctx, mod_lat, y_prompt.shape[1], y_sample.shape[1],
                                   w, tab)
        y_prompt = y_ctx.reshape(y_prompt.shape)
        y_sample = y_lat.reshape(y_sample.shape)
    return (y_prompt, y_sample, jnp.stack(ckv_list, axis=1), jnp.stack(krope_list, axis=1),
            jnp.stack(ret_list, axis=1))
```

```python
import functools

import jax
import jax.numpy as jnp
from jax import lax
from jax.experimental import pallas as pl
from jax.experimental.pallas import tpu as pltpu
from jax.experimental.pallas import tpu_sc as plsc

F32 = jnp.float32
BF16 = jnp.bfloat16

D_MODEL = 1024
GRID_W = 64
MLA_HEADS = 4
MLA_NOPE = 128
MLA_ROPE = 64
MLA_QK = MLA_NOPE + MLA_ROPE
MLA_V = 128
Q_RANK = 512
KV_RANK = 256
ROPE_AXIS = MLA_ROPE // 2
ROPE_BASE = 10000.0
RET_HEADS = 4
RET_DK = 128
RET_DV = 128
RET_CHUNK = 128
PEER_HEADS = 8
PEER_QDIM = 256
PEER_HALF = PEER_QDIM // 2
N_KEYS = 128
PEER_TOPK = 16
EPS = 1e-6

LANE = 128
HEAD_PAD = 2 * LANE
RET_W = 4 * RET_HEADS * RET_DK
IN_PAD = Q_RANK + KV_RANK + LANE + RET_W
VMEM_LIMIT = 48 * 1024 * 1024

ROW_TILE = 256
Q_TILE = 256
GATHER_TILE = 128
GATHER_SLOTS = 4
GATHER_AHEAD = GATHER_SLOTS - 1
EXPERTS_PER_TOKEN = PEER_HEADS * PEER_TOPK
SLAB_ROWS = D_MODEL // LANE


def _params(*sem):
    return pltpu.CompilerParams(dimension_semantics=sem, vmem_limit_bytes=VMEM_LIMIT)


def _rms(x, w):
    return x * lax.rsqrt(jnp.mean(x * x, axis=-1, keepdims=True) + EPS) * w


def _mm(a, b):
    return jnp.dot(a.astype(BF16), b.astype(BF16), preferred_element_type=F32)


def _mm_nt(a, b):
    return lax.dot_general(a.astype(BF16), b.astype(BF16), (((1,), (1,)), ((), ())),
                           preferred_element_type=F32)


def _mm_tn(a, b):
    return lax.dot_general(a.astype(BF16), b.astype(BF16), (((0,), (0,)), ((), ())),
                           preferred_element_type=F32)


def _ada_kernel(c_ref, w_ref, b_ref, o_ref):
    c = c_ref[...]
    o_ref[...] = _mm(c * jax.nn.sigmoid(c), w_ref[...]) + b_ref[...]


def _ada(cond, w_ada, b_ada):
    rows, d = cond.shape
    n = w_ada.shape[1]
    tn = 1536
    return pl.pallas_call(
        _ada_kernel,
        grid=(n // tn,),
        in_specs=[pl.BlockSpec((rows, d), lambda j: (0, 0)),
                  pl.BlockSpec((d, tn), lambda j: (0, j)),
                  pl.BlockSpec((1, tn), lambda j: (0, j))],
        out_specs=pl.BlockSpec((rows, tn), lambda j: (0, j)),
        out_shape=jax.ShapeDtypeStruct((rows, n), F32),
        compiler_params=_params("arbitrary"),
        name="ada",
    )(cond, w_ada, b_ada.reshape(1, n))


def _rope_tile(x, cos, sin):
    lane = lax.broadcasted_iota(jnp.int32, x.shape, 1)
    partner = jnp.where((lane % 32) < 16, pltpu.roll(x, LANE - 16, 1), pltpu.roll(x, 16, 1))
    return x * cos + partner * sin


def _kv_heads(kv, kr, khw, cos, sin, use_rope):
    krw = kr * khw[:, LANE:]
    if use_rope:
        krw = _rope_tile(krw, cos, sin)
    ssq_r = jnp.sum(kr * kr, axis=-1, keepdims=True)
    ks, vs = [], []
    for hd in range(MLA_HEADS):
        kn = kv[:, hd * HEAD_PAD: hd * HEAD_PAD + LANE]
        r = lax.rsqrt((jnp.sum(kn * kn, axis=-1, keepdims=True) + ssq_r) / MLA_QK + EPS)
        ks += [kn * r * khw[:, :LANE], krw * r]
        vs.append(kv[:, hd * HEAD_PAD + LANE: (hd + 1) * HEAD_PAD])
    return jnp.concatenate(ks, axis=-1).astype(BF16), jnp.concatenate(vs, axis=-1).astype(BF16)


def _inproj_kernel(x_ref, mod_ref, n1w_ref, win_ref, qnw_ref, kvnw_ref, wuq_ref, wukv_ref, qhw_ref, khw_ref,
                   cos_ref, sin_ref, *rest, use_rope):
    ckv_ref, krope_ref, q_ref, k_ref, v_ref, zr_ref = rest[-6:]
    m = mod_ref[0]
    h = _rms(x_ref[...], n1w_ref[...]) * (1.0 + m[1:2]) + m[0:1]
    z = _mm(h, win_ref[...])
    kr = z[:, Q_RANK + KV_RANK: Q_RANK + KV_RANK + LANE]
    zr_ref[...] = z[:, Q_RANK + KV_RANK + LANE:]
    ckvn = _rms(z[:, Q_RANK: Q_RANK + KV_RANK], kvnw_ref[...])
    ckv_ref[...] = ckvn
    krope_ref[...] = kr[:, :MLA_ROPE]
    cos, sin = cos_ref[...], sin_ref[...]

    q = _mm(_rms(z[:, :Q_RANK], qnw_ref[...]), wuq_ref[...])
    qhw = qhw_ref[...]
    qs = []
    for hd in range(MLA_HEADS):
        qn = q[:, hd * HEAD_PAD: hd * HEAD_PAD + LANE]
        qr = q[:, hd * HEAD_PAD + LANE: (hd + 1) * HEAD_PAD]
        ssq = jnp.sum(qn * qn, axis=-1, keepdims=True) + jnp.sum(qr * qr, axis=-1, keepdims=True)
        r = lax.rsqrt(ssq / MLA_QK + EPS)
        qrw = qr * qhw[:, LANE:]
        if use_rope:
            qrw = _rope_tile(qrw, cos, sin)
        qs += [qn * r * qhw[:, :LANE], qrw * r]
    q_ref[...] = jnp.concatenate(qs, axis=-1).astype(BF16)

    k, v = _kv_heads(_mm(ckvn, wukv_ref[...]), kr, khw_ref[...], cos, sin, use_rope)
    k_ref[...] = k
    v_ref[...] = v


def _in_proj(x, mod, w, cos, sin, seq, use_rope, after=None):
    t = x.shape[0]
    tm = ROW_TILE
    per_batch = seq // tm
    const = lambda i: (0, 0)
    row = lambda i: (i, 0)
    if use_rope:
        pos = lambda i: (i % per_batch, 0)
    else:
        pos = const
    hp = MLA_HEADS * HEAD_PAD
    ordering = [] if after is None else [after]
    return pl.pallas_call(
        functools.partial(_inproj_kernel, use_rope=use_rope),
        grid=(t // tm,),
        in_specs=[pl.BlockSpec((tm, D_MODEL), row),
                  pl.BlockSpec((1, 6, D_MODEL), lambda i: (i // per_batch, 0, 0)),
                  pl.BlockSpec((1, D_MODEL), const),
                  pl.BlockSpec((D_MODEL, IN_PAD), const),
                  pl.BlockSpec((1, Q_RANK), const),
                  pl.BlockSpec((1, KV_RANK), const),
                  pl.BlockSpec((Q_RANK, hp), const),
                  pl.BlockSpec((KV_RANK, hp), const),
                  pl.BlockSpec((1, HEAD_PAD), const),
                  pl.BlockSpec((1, HEAD_PAD), const),
                  pl.BlockSpec((tm, LANE), pos),
                  pl.BlockSpec((tm, LANE), pos)] + [pl.BlockSpec(memory_space=pl.ANY)] * len(ordering),
        out_specs=[pl.BlockSpec((tm, KV_RANK), row),
                   pl.BlockSpec((tm, MLA_ROPE), row),
                   pl.BlockSpec((tm, hp), row),
                   pl.BlockSpec((tm, hp), row),
                   pl.BlockSpec((tm, MLA_HEADS * MLA_V), row),
                   pl.BlockSpec((tm, RET_W), row)],
        out_shape=[jax.ShapeDtypeStruct((t, KV_RANK), F32),
                   jax.ShapeDtypeStruct((t, MLA_ROPE), F32),
                   jax.ShapeDtypeStruct((t, hp), BF16),
                   jax.ShapeDtypeStruct((t, hp), BF16),
                   jax.ShapeDtypeStruct((t, MLA_HEADS * MLA_V), BF16),
                   jax.ShapeDtypeStruct((t, RET_W), F32)],
        compiler_params=_params("arbitrary"),
        name="in_proj",
    )(x, mod, w["norm1_w"], w["w_in"], w["q_norm_w"], w["kv_norm_w"], w["w_uq"], w["w_ukv"],
      w["q_head_norm_w"], w["k_head_norm_w"], cos, sin, *ordering)


def _kvup_kernel(ckv_ref, kr_ref, wukv_ref, khw_ref, k_ref, v_ref):
    k, v = _kv_heads(_mm(ckv_ref[...], wukv_ref[...]), kr_ref[...], khw_ref[...], None, None, False)
    k_ref[...] = k
    v_ref[...] = v


def _kv_up(ckv, kr, w):
    t = ckv.shape[0]
    tm = ROW_TILE
    hp = MLA_HEADS * HEAD_PAD
    const = lambda i: (0, 0)
    row = lambda i: (i, 0)
    return pl.pallas_call(
        _kvup_kernel,
        grid=(t // tm,),
        in_specs=[pl.BlockSpec((tm, KV_RANK), row),
                  pl.BlockSpec((tm, LANE), row),
                  pl.BlockSpec((KV_RANK, hp), const),
                  pl.BlockSpec((1, HEAD_PAD), const)],
        out_specs=[pl.BlockSpec((tm, hp), row),
                   pl.BlockSpec((tm, MLA_HEADS * MLA_V), row)],
        out_shape=[jax.ShapeDtypeStruct((t, hp), BF16),
                   jax.ShapeDtypeStruct((t, MLA_HEADS * MLA_V), BF16)],
        compiler_params=_params("arbitrary"),
        name="kv_up",
    )(ckv, kr, w["w_ukv"], w["k_head_norm_w"])


def _attn_kernel(*refs, has_ctx):
    if has_ctx:
        q_ref, k_ref, v_ref, kc_ref, vc_ref, o_ref = refs
    else:
        q_ref, k_ref, v_ref, o_ref = refs
    scale = MLA_QK ** -0.5
    q = q_ref[...]
    s = _mm_nt(q, k_ref[...]) * scale
    m = jnp.max(s, axis=-1, keepdims=True)
    if has_ctx:
        sc = _mm_nt(q, kc_ref[...]) * scale
        m = jnp.maximum(m, jnp.max(sc, axis=-1, keepdims=True))
    p = jnp.exp(s - m)
    den = jnp.sum(p, axis=-1, keepdims=True)
    o = _mm(p, v_ref[...])
    if has_ctx:
        pc = jnp.exp(sc - m)
        den = den + jnp.sum(pc, axis=-1, keepdims=True)
        o = o + _mm(pc, vc_ref[...])
    o_ref[...] = (o / den).astype(BF16)


def _attention(q, k, v, batch, seq, ctx_kv):
    tq = min(Q_TILE, seq)
    nq = seq // tq
    has_ctx = ctx_kv is not None
    qmap = lambda b, h, i: (b * nq + i, h)
    kvmap = lambda b, h, i: (b, h)
    in_specs = [pl.BlockSpec((tq, HEAD_PAD), qmap),
                pl.BlockSpec((seq, HEAD_PAD), kvmap),
                pl.BlockSpec((seq, MLA_V), kvmap)]
    args = [q, k, v]
    if has_ctx:
        kc, vc = ctx_kv
        past = kc.shape[0] // batch
        in_specs += [pl.BlockSpec((past, HEAD_PAD), kvmap), pl.BlockSpec((past, MLA_V), kvmap)]
        args += [kc, vc]
    return pl.pallas_call(
        functools.partial(_attn_kernel, has_ctx=has_ctx),
        grid=(batch, MLA_HEADS, nq),
        in_specs=in_specs,
        out_specs=pl.BlockSpec((tq, MLA_V), qmap),
        out_shape=jax.ShapeDtypeStruct((batch * seq, MLA_HEADS * MLA_V), BF16),
        compiler_params=_params("arbitrary", "arbitrary", "arbitrary"),
        name="attention",
    )(*args)


def _ret_kernel(*refs, has_s0):
    if has_s0:
        (lg_ref, qf_ref, kf_ref, vf_ref, qb_ref, kb_ref, vb_ref, s0f_ref, s0b_ref,
         of_ref, ob_ref, sf_out, sb_out, sf_scr, sb_scr) = refs
    else:
        (lg_ref, qf_ref, kf_ref, vf_ref, qb_ref, kb_ref, vb_ref,
         of_ref, ob_ref, sf_out, sb_out, sf_scr, sb_scr) = refs
    c = pl.program_id(1)
    cs = RET_CHUNK

    @pl.when(c == 0)
    def _():
        if has_s0:
            sf_scr[...] = s0f_ref[0]
            sb_scr[...] = s0b_ref[0]
        else:
            sf_scr[...] = jnp.zeros_like(sf_scr)
            sb_scr[...] = jnp.zeros_like(sb_scr)

    ii = lax.broadcasted_iota(jnp.int32, (cs, cs), 0).astype(F32)
    jj = lax.broadcasted_iota(jnp.int32, (cs, cs), 1).astype(F32)
    rel = ii - jj
    kscale = RET_DK ** -0.5

    def chunk(q, k, v, s, intra, qdec, kdec, cdec):
        a = _mm_nt(q, k) * intra
        o = _mm(a, v) + _mm(q, s) * qdec
        return o, s * cdec + _mm_tn(k * kdec, v)

    for hd in range(RET_HEADS):
        cols = slice(hd * RET_DK, (hd + 1) * RET_DK)
        lgf = jax.nn.log_sigmoid(jnp.full((1, LANE), lg_ref[0, hd], F32))
        lgb = jax.nn.log_sigmoid(jnp.full((1, LANE), lg_ref[1, hd], F32))

        intra_f = jnp.where(rel >= 0, jnp.exp(jnp.maximum(rel, 0.0) * lgf), 0.0)
        o_f, s_f = chunk(qf_ref[:, cols], kf_ref[:, cols] * kscale, vf_ref[:, cols], sf_scr[hd], intra_f,
                         jnp.exp((ii + 1.0) * lgf), jnp.exp((cs - 1.0 - ii) * lgf), jnp.exp(cs * lgf))
        of_ref[:, cols] = o_f
        sf_scr[hd] = s_f
        sf_out[0, hd] = s_f

        intra_b = jnp.where(rel <= 0, jnp.exp(jnp.maximum(-rel, 0.0) * lgb), 0.0)
        o_b, s_b = chunk(qb_ref[:, cols], kb_ref[:, cols] * kscale, vb_ref[:, cols], sb_scr[hd], intra_b,
                         jnp.exp((cs - ii) * lgb), jnp.exp(ii * lgb), jnp.exp(cs * lgb))
        ob_ref[:, cols] = o_b
        sb_scr[hd] = s_b
        sb_out[0, hd] = s_b


def _retention(zr, decay_logit, batch, seq, s0):
    cs = RET_CHUNK
    nc = seq // cs
    nh = RET_HEADS
    width = nh * RET_DK
    has_s0 = s0 is not None

    def fwd(col):
        return pl.BlockSpec((cs, width), lambda b, c: (b * nc + c, col))

    def bwd(col):
        return pl.BlockSpec((cs, width), lambda b, c: (b * nc + nc - 1 - c, col))

    def state(d):
        return pl.BlockSpec((1, nh, RET_DK, RET_DV), lambda b, c: (b * 2 + d, 0, 0, 0))

    state_out = pl.BlockSpec((1, nh, RET_DK, RET_DV), lambda b, c: (b, 0, 0, 0))
    in_specs = [pl.BlockSpec(memory_space=pltpu.SMEM), fwd(0), fwd(1), fwd(2), bwd(0), bwd(1), bwd(2)]
    args = [decay_logit, zr, zr, zr, zr, zr, zr]
    if has_s0:
        in_specs += [state(0), state(1)]
        args += [s0, s0]
    t = batch * seq
    o_f, o_b, s_f, s_b = pl.pallas_call(
        functools.partial(_ret_kernel, has_s0=has_s0),
        grid=(batch, nc),
        in_specs=in_specs,
        out_specs=[pl.BlockSpec((cs, width), lambda b, c: (b * nc + c, 0)),
                   pl.BlockSpec((cs, width), lambda b, c: (b * nc + nc - 1 - c, 0)),
                   state_out, state_out],
        out_shape=[jax.ShapeDtypeStruct((t, width), F32),
                   jax.ShapeDtypeStruct((t, width), F32),
                   jax.ShapeDtypeStruct((batch, nh, RET_DK, RET_DV), F32),
                   jax.ShapeDtypeStruct((batch, nh, RET_DK, RET_DV), F32)],
        scratch_shapes=[pltpu.VMEM((nh, RET_DK, RET_DV), F32), pltpu.VMEM((nh, RET_DK, RET_DV), F32)],
        compiler_params=_params("arbitrary", "arbitrary"),
        name="retention",
    )(*args)
    return o_f, o_b, s_f, s_b


def _mix_kernel(x_ref, att_ref, of_ref, ob_ref, gr_ref, mod_ref, gnw_ref, wout_ref, n2w_ref, x1_ref, h2_ref):
    o = of_ref[...] + ob_ref[...]
    parts = []
    for hd in range(RET_HEADS):
        oh = o[:, hd * RET_DV:(hd + 1) * RET_DV]
        d = oh - jnp.mean(oh, axis=-1, keepdims=True)
        parts.append(d * lax.rsqrt(jnp.mean(d * d, axis=-1, keepdims=True) + EPS))
    g = gr_ref[...]
    ret = (g * jax.nn.sigmoid(g)) * (jnp.concatenate(parts, axis=-1) * gnw_ref[...])
    na = MLA_HEADS * MLA_V
    mixed = _mm(att_ref[...], wout_ref[:na, :]) + _mm(ret, wout_ref[na:, :])
    m = mod_ref[0]
    x1 = x_ref[...] + m[2:3] * mixed
    x1_ref[...] = x1
    h2_ref[...] = _rms(x1, n2w_ref[...]) * (1.0 + m[4:5]) + m[3:4]


def _mix(x, att, o_f, o_b, zr, mod, w, seq):
    t = x.shape[0]
    tm = ROW_TILE
    per_batch = seq // tm
    const = lambda i: (0, 0)
    row = lambda i: (i, 0)
    half = RET_HEADS * RET_DV
    return pl.pallas_call(
        _mix_kernel,
        grid=(t // tm,),
        in_specs=[pl.BlockSpec((tm, D_MODEL), row),
                  pl.BlockSpec((tm, half), row),
                  pl.BlockSpec((tm, half), row),
                  pl.BlockSpec((tm, half), row),
                  pl.BlockSpec((tm, half), lambda i: (i, 3)),
                  pl.BlockSpec((1, 6, D_MODEL), lambda i: (i // per_batch, 0, 0)),
                  pl.BlockSpec((1, half), const),
                  pl.BlockSpec((D_MODEL, D_MODEL), const),
                  pl.BlockSpec((1, D_MODEL), const)],
        out_specs=[pl.BlockSpec((tm, D_MODEL), row), pl.BlockSpec((tm, D_MODEL), row)],
        out_shape=[jax.ShapeDtypeStruct((t, D_MODEL), F32), jax.ShapeDtypeStruct((t, D_MODEL), F32)],
        compiler_params=_params("arbitrary"),
        name="mix",
    )(x, att, o_f, o_b, zr, mod, w["ret_gn_w"], w["w_out"], w["norm2_w"])


def _top16(jobs):
    def body(r, carry):
        for s_ref, vals_ref, pay_ref, payload in jobs:
            n, cols = s_ref.shape
            rows = lax.broadcasted_iota(jnp.int32, (n, cols), 0).astype(F32)
            s = s_ref[...]
            m = jnp.max(s, axis=0, keepdims=True)
            pos = jnp.min(jnp.where(s == m, rows, float(n)), axis=0, keepdims=True)
            hit = rows == pos
            vals_ref[pl.ds(r, 1), :] = m
            if payload is None:
                pay_ref[pl.ds(r, 1), :] = pos
            else:
                pay_ref[pl.ds(r, 1), :] = jnp.sum(jnp.where(hit, payload, 0.0), axis=0, keepdims=True)
            s_ref[...] = jnp.where(hit, -jnp.inf, s)
        return carry

    lax.fori_loop(0, PEER_TOPK, body, 0)


PAIR_COUNTS = tuple(PEER_TOPK // (a + 1) for a in range(PEER_TOPK))
NUM_PAIRS = sum(PAIR_COUNTS)
PAIR_ROWS = -(-NUM_PAIRS // 8) * 8


def _score_head(hbt, wpqt, k1, k2, g_out, e_out, scr):
    s1_scr, s2_scr, c_scr, p_scr, v1_scr, i1_scr, v2_scr, i2_scr, vt_scr, it_scr = scr
    qt = jnp.dot(wpqt, hbt, preferred_element_type=F32)
    s1_scr[...] = _mm(k1, qt[:PEER_HALF, :])
    s2_scr[...] = _mm(k2, qt[PEER_HALF:, :])
    _top16([(s1_scr, v1_scr, i1_scr, None), (s2_scr, v2_scr, i2_scr, None)])
    c_scr[...] = jnp.full(c_scr.shape, -jnp.inf, F32)
    p_scr[...] = jnp.zeros(p_scr.shape, F32)
    off = 0
    for a, nb in enumerate(PAIR_COUNTS):
        c_scr[off:off + nb, :] = v1_scr[a:a + 1, :] + v2_scr[0:nb, :]
        p_scr[off:off + nb, :] = i1_scr[a:a + 1, :] * float(N_KEYS) + i2_scr[0:nb, :]
        off += nb
    _top16([(c_scr, vt_scr, it_scr, p_scr[...])])
    top = vt_scr[...]
    p = jnp.exp(top - jnp.max(top, axis=0, keepdims=True))
    g_out[...] = p / jnp.sum(p, axis=0, keepdims=True)
    e_out[...] = it_scr[...]


def _score_scratch(tb):
    k = PEER_TOPK
    return ([pltpu.VMEM((N_KEYS, tb), F32)] * 2 + [pltpu.VMEM((PAIR_ROWS, tb), F32)] * 2
            + [pltpu.VMEM((k, tb), F32)] * 6)


def _peer_kernel(tab_hbm, h2_ref, h2n_ref, wpq_ref, k1_ref, k2_ref, x1_ref, mod_ref, *rest, ntiles, aliased):
    y_ref, idx_smem, g_scr, e_scr, ei_scr, hb_scr = rest[int(aliased):int(aliased) + 6]
    scratch = rest[int(aliased) + 6:]
    _peer_body(tab_hbm, h2_ref, h2n_ref, wpq_ref, k1_ref, k2_ref, x1_ref, mod_ref, y_ref,
               idx_smem, g_scr, e_scr, ei_scr, hb_scr, *scratch, ntiles=ntiles)


def _peer_body(tab_hbm, h2_ref, h2n_ref, wpq_ref, k1_ref, k2_ref, x1_ref, mod_ref, y_ref,
               idx_smem, g_scr, e_scr, ei_scr, hb_scr, *scratch, ntiles):
    bufs = scratch[:GATHER_SLOTS]
    sem_rows, sem_idx = scratch[GATHER_SLOTS:GATHER_SLOTS + 2]
    score_scr = scratch[GATHER_SLOTS + 2:]
    i = pl.program_id(0)
    tb = GATHER_TILE
    ns = GATHER_SLOTS
    ahead = GATHER_AHEAD
    ne = EXPERTS_PER_TOKEN
    slab = SLAB_ROWS
    topk = PEER_TOPK
    groups_per_head = tb // (ns * PEER_HEADS)
    cur = i % 2
    nxt = 1 - cur
    more = i + 1 < ntiles

    def score(src_ref, hd, slot):
        @pl.when(hd == 0)
        def _():
            hb_scr[...] = src_ref[...].T.astype(BF16)
        head_rows = pl.ds(pl.multiple_of(hd * topk, topk), topk)
        _score_head(hb_scr[...], wpq_ref[hd], k1_ref[...], k2_ref[...],
                    g_scr.at[slot, head_rows], e_scr.at[head_rows], score_scr)

    def publish(slot):
        ei_scr[...] = e_scr[...].T.astype(jnp.int32)
        copy = pltpu.make_async_copy(ei_scr, idx_smem.at[slot], sem_idx.at[0])
        copy.start()
        copy.wait()

    def issue(islot, tok, rslot):
        for kk in range(ne):
            ex = idx_smem[islot, tok, kk]
            pltpu.make_async_copy(tab_hbm.at[ex], bufs[rslot].at[:, pl.ds(kk * slab, slab), :],
                                  sem_rows.at[rslot]).start(priority=kk % 2)

    def wait_rows(rslot):
        pltpu.make_async_copy(bufs[rslot], bufs[rslot], sem_rows.at[rslot]).wait()

    @pl.when(i == 0)
    def _():
        def first(hd, carry):
            score(h2_ref, hd, 0)
            return carry
        lax.fori_loop(0, PEER_HEADS, first, 0)
        publish(0)
        for tok in range(ahead):
            issue(0, tok, tok)

    g2 = mod_ref[0][5:6]
    lane = lax.broadcasted_iota(jnp.int32, (ne, tb), 1)

    def compute(j, rslot):
        rows = bufs[rslot]
        x_row = h2_ref[pl.ds(j, 1), :]
        acc = None
        for c in range(slab):
            u_c = rows[0, pl.ds(c, ne, stride=slab), :]
            term = u_c * x_row[:, c * LANE:(c + 1) * LANE]
            acc = term if acc is None else acc + term
        pre = jnp.sum(acc, axis=-1, keepdims=True)
        act = 0.5 * pre * (1.0 + lax.erf(pre * (2.0 ** -0.5)))
        gate = jnp.sum(jnp.where(lane == j, g_scr[cur], 0.0), axis=-1, keepdims=True)
        wgt = gate * act
        outs = []
        for c in range(slab):
            v_c = rows[1, pl.ds(c, ne, stride=slab), :]
            outs.append(jnp.sum(v_c * wgt, axis=0, keepdims=True))
        out = jnp.concatenate(outs, axis=-1)
        y_ref[pl.ds(j, 1), :] = x1_ref[pl.ds(j, 1), :] + g2 * out

    def phase(hd, carry):
        score(h2n_ref, hd, nxt)

        @pl.when(hd == PEER_HEADS - 1)
        def _():
            publish(nxt)

        def group(gq, inner):
            g = hd * groups_per_head + gq
            for u in range(ns):
                j = g * ns + u
                wait_rows(u)
                jj = j + ahead
                over = jj >= tb
                issue(jnp.where(over, nxt, cur), jnp.where(over, jj - tb, jj), (u + ahead) % ns)
                compute(j, u)
            return inner

        lax.fori_loop(0, groups_per_head, group, 0)
        return carry

    lax.fori_loop(0, PEER_HEADS, phase, 0)

    @pl.when(jnp.logical_not(more))
    def _():
        for u in range(ahead):
            wait_rows(u)


def _peer(tab, h2, x1, mod, w, seq, tile0, ntiles, y_prev=None):
    t = h2.shape[0]
    tb = GATHER_TILE
    per_batch = seq // tb
    row = lambda i: (tile0 + i, 0)
    const = lambda i: (0, 0)
    in_specs = [pl.BlockSpec(memory_space=pl.ANY),
                pl.BlockSpec((tb, D_MODEL), row),
                pl.BlockSpec((tb, D_MODEL), lambda i: (tile0 + jnp.minimum(i + 1, ntiles - 1), 0)),
                pl.BlockSpec((PEER_HEADS, PEER_QDIM, D_MODEL), lambda i: (0, 0, 0)),
                pl.BlockSpec((N_KEYS, PEER_HALF), const),
                pl.BlockSpec((N_KEYS, PEER_HALF), const),
                pl.BlockSpec((tb, D_MODEL), row),
                pl.BlockSpec((1, 6, D_MODEL), lambda i: ((tile0 + i) // per_batch, 0, 0))]
    args = [tab, h2, h2, w["w_pq"], w["sub_keys1"], w["sub_keys2"], x1, mod]
    aliases = {}
    if y_prev is not None:
        in_specs.append(pl.BlockSpec(memory_space=pl.ANY))
        args.append(y_prev)
        aliases = {len(args) - 1: 0}
    return pl.pallas_call(
        functools.partial(_peer_kernel, ntiles=ntiles, aliased=y_prev is not None),
        grid=(ntiles,),
        in_specs=in_specs,
        input_output_aliases=aliases,
        out_specs=pl.BlockSpec((tb, D_MODEL), row),
        out_shape=jax.ShapeDtypeStruct((t, D_MODEL), F32),
        scratch_shapes=[pltpu.SMEM((2, tb, EXPERTS_PER_TOKEN), jnp.int32),
                        pltpu.VMEM((2, EXPERTS_PER_TOKEN, tb), F32),
                        pltpu.VMEM((EXPERTS_PER_TOKEN, tb), F32),
                        pltpu.VMEM((tb, EXPERTS_PER_TOKEN), jnp.int32),
                        pltpu.VMEM((D_MODEL, tb), BF16)]
                       + [pltpu.VMEM((2, EXPERTS_PER_TOKEN * SLAB_ROWS, LANE), F32)] * GATHER_SLOTS
                       + [pltpu.SemaphoreType.DMA((GATHER_SLOTS,)), pltpu.SemaphoreType.DMA((1,))]
                       + _score_scratch(tb),
        compiler_params=pltpu.CompilerParams(dimension_semantics=("arbitrary",),
                                             vmem_limit_bytes=VMEM_LIMIT,
                                             disable_bounds_checks=True),
        name="peer",
    )(*args)


def _score_kernel(h2_ref, wpq_ref, k1_ref, k2_ref, *rest, ordered):
    g_ref, e_ref, hb_scr = rest[int(ordered):int(ordered) + 3]
    score_scr = rest[int(ordered) + 3:]
    hb_scr[...] = h2_ref[...].T.astype(BF16)

    def head(hd, carry):
        head_rows = pl.ds(pl.multiple_of(hd * PEER_TOPK, PEER_TOPK), PEER_TOPK)
        _score_head(hb_scr[...], wpq_ref[hd], k1_ref[...], k2_ref[...], g_ref.at[head_rows], e_ref.at[head_rows],
                    score_scr)
        return carry

    lax.fori_loop(0, PEER_HEADS, head, 0)


def _peer_score(h2, w, tile0, ntiles, after=None):
    tb = GATHER_TILE
    const = lambda i: (0, 0)
    out = pl.BlockSpec((EXPERTS_PER_TOKEN, tb), lambda i: (0, i))
    ordering = [] if after is None else [after]
    return pl.pallas_call(
        functools.partial(_score_kernel, ordered=after is not None),
        grid=(ntiles,),
        in_specs=[pl.BlockSpec((tb, D_MODEL), lambda i: (tile0 + i, 0)),
                  pl.BlockSpec((PEER_HEADS, PEER_QDIM, D_MODEL), lambda i: (0, 0, 0)),
                  pl.BlockSpec((N_KEYS, PEER_HALF), const),
                  pl.BlockSpec((N_KEYS, PEER_HALF), const)] + [pl.BlockSpec(memory_space=pl.ANY)] * len(ordering),
        out_specs=[out, out],
        out_shape=[jax.ShapeDtypeStruct((EXPERTS_PER_TOKEN, ntiles * tb), F32)] * 2,
        scratch_shapes=[pltpu.VMEM((D_MODEL, tb), BF16)] + _score_scratch(tb),
        compiler_params=_params("arbitrary"),
        name="peer_score",
    )(h2, w["w_pq"], w["sub_keys1"], w["sub_keys2"], *ordering)


def _act_kernel(pre_ref, g_ref, after_ref, w_ref):
    del after_ref
    pre = pre_ref[...]
    w_ref[...] = g_ref[...] * (0.5 * pre * (1.0 + lax.erf(pre * (2.0 ** -0.5))))


def _peer_act(pre, gate, after):
    t, ne = pre.shape
    tm = 1024
    row = lambda i: (i, 0)
    return pl.pallas_call(
        _act_kernel,
        grid=(t // tm,),
        in_specs=[pl.BlockSpec((tm, ne), row), pl.BlockSpec((tm, ne), row), pl.BlockSpec(memory_space=pl.ANY)],
        out_specs=pl.BlockSpec((tm, ne), row),
        out_shape=jax.ShapeDtypeStruct((t, ne), F32),
        compiler_params=_params("arbitrary"),
        name="peer_act",
    )(pre, gate, after)


def _residual_kernel(x1_ref, out_ref, mod_ref, *rest):
    rest[-1][...] = x1_ref[...] + mod_ref[0][5:6] * out_ref[...]


def _peer_residual(x1, out, mod, seq, tile0, y_prev):
    tb = GATHER_TILE
    per_batch = seq // tb
    row = lambda i: (tile0 + i, 0)
    in_specs = [pl.BlockSpec((tb, D_MODEL), row),
                pl.BlockSpec((tb, D_MODEL), lambda i: (i, 0)),
                pl.BlockSpec((1, 6, D_MODEL), lambda i: ((tile0 + i) // per_batch, 0, 0))]
    args = [x1, out, mod]
    aliases = {}
    if y_prev is not None:
        in_specs.append(pl.BlockSpec(memory_space=pl.ANY))
        args.append(y_prev)
        aliases = {3: 0}
    return pl.pallas_call(
        _residual_kernel,
        grid=(out.shape[0] // tb,),
        in_specs=in_specs,
        input_output_aliases=aliases,
        out_specs=pl.BlockSpec((tb, D_MODEL), row),
        out_shape=jax.ShapeDtypeStruct(x1.shape, F32),
        compiler_params=_params("arbitrary"),
        name="peer_residual",
    )(*args)


SC_WORKERS = 32
SC_LANES = 16
SC_CHUNK = 16
SC_SHARE = (15, 32)
TC_FIRST = (13, 25)


def _sc_udot(u_table, idx, x):
    t, ne = idx.shape
    per_w = t // SC_WORKERS
    nchunk = ne // SC_CHUNK
    nl = D_MODEL // SC_LANES
    mesh = plsc.VectorSubcoreMesh(core_axis_name="c", subcore_axis_name="s")

    @functools.partial(
        pl.kernel, out_type=jax.ShapeDtypeStruct((t, ne), F32), mesh=mesh,
        scratch_types=[pltpu.VMEM((ne,), jnp.int32), pltpu.VMEM((D_MODEL,), F32),
                       pltpu.VMEM((2, SC_CHUNK, D_MODEL), F32), pltpu.VMEM((SC_CHUNK, SC_LANES), F32),
                       pltpu.VMEM((ne,), F32), pltpu.SemaphoreType.DMA((2,))],
        compiler_params=pltpu.CompilerParams(needs_layout_passes=False),
        name="sc_udot")
    def run(u_hbm, idx_hbm, x_hbm, pre_hbm, idx_v, x_v, rows, accs, pre_v, sems):
        wid = lax.axis_index("s") * 2 + lax.axis_index("c")
        lanes = lax.iota(jnp.int32, SC_LANES)

        def gather(c, b):
            return pltpu.make_async_copy(u_hbm.at[idx_v.at[pl.ds(c * SC_CHUNK, SC_CHUNK)]], rows.at[b], sems.at[b])

        @pl.loop(0, per_w)
        def _(tt):
            tok = wid * per_w + tt
            pltpu.sync_copy(idx_hbm.at[tok], idx_v)
            pltpu.sync_copy(x_hbm.at[tok], x_v)
            gather(0, 0).start()
            for c in range(nchunk):
                b = c % 2
                if c + 1 < nchunk:
                    gather(c + 1, 1 - b).start()
                gather(c, b).wait()
                for g in range(SC_CHUNK // 4):
                    def body(j, acc):
                        xv = x_v[pl.ds(j * SC_LANES, SC_LANES)]
                        return tuple(acc[q] + rows[b, g * 4 + q, pl.ds(j * SC_LANES, SC_LANES)] * xv
                                     for q in range(4))
                    acc = plsc.parallel_loop(0, nl, unroll=4,
                                             carry=tuple(jnp.zeros((SC_LANES,), F32) for _ in range(4)))(body)
                    for q in range(4):
                        accs[g * 4 + q, :] = acc[q]
                tot = jnp.zeros((SC_LANES,), F32)
                for l in range(SC_LANES):
                    tot = tot + plsc.load_gather(accs, [lanes, jnp.full((SC_LANES,), l, jnp.int32)])
                pre_v[pl.ds(c * SC_CHUNK, SC_CHUNK)] = tot
            pltpu.sync_copy(pre_v, pre_hbm.at[tok])

    return run(u_table, idx, x)


def _sc_vsum(v_table, idx, wgt):
    t, ne = idx.shape
    per_w = t // SC_WORKERS
    nchunk = ne // SC_CHUNK
    block = 16 * SC_LANES
    mesh = plsc.VectorSubcoreMesh(core_axis_name="c", subcore_axis_name="s")

    @functools.partial(
        pl.kernel, out_type=jax.ShapeDtypeStruct((t, D_MODEL), F32), mesh=mesh,
        scratch_types=[pltpu.VMEM((ne,), jnp.int32), pltpu.VMEM((ne,), F32),
                       pltpu.VMEM((2, SC_CHUNK, D_MODEL), F32), pltpu.VMEM((D_MODEL,), F32),
                       pltpu.SemaphoreType.DMA((2,))],
        compiler_params=pltpu.CompilerParams(needs_layout_passes=False),
        name="sc_vsum")
    def run(v_hbm, idx_hbm, w_hbm, out_hbm, idx_v, w_v, rows, out_v, sems):
        wid = lax.axis_index("s") * 2 + lax.axis_index("c")

        def gather(c, b):
            return pltpu.make_async_copy(v_hbm.at[idx_v.at[pl.ds(c * SC_CHUNK, SC_CHUNK)]], rows.at[b], sems.at[b])

        @pl.loop(0, per_w)
        def _(tt):
            tok = wid * per_w + tt
            pltpu.sync_copy(idx_hbm.at[tok], idx_v)
            pltpu.sync_copy(w_hbm.at[tok], w_v)
            gather(0, 0).start()
            for c in range(nchunk):
                b = c % 2
                if c + 1 < nchunk:
                    gather(c + 1, 1 - b).start()
                gather(c, b).wait()
                for d in range(D_MODEL // block):
                    def body(k, acc):
                        wk = plsc.load_gather(w_v, [jnp.full((SC_LANES,), c * SC_CHUNK, jnp.int32) + k])
                        return tuple(acc[j] + rows[b, k, pl.ds(d * block + j * SC_LANES, SC_LANES)] * wk
                                     for j in range(16))
                    if c == 0:
                        init = tuple(jnp.zeros((SC_LANES,), F32) for _ in range(16))
                    else:
                        init = tuple(out_v[pl.ds(d * block + j * SC_LANES, SC_LANES)] for j in range(16))
                    acc = plsc.parallel_loop(0, SC_CHUNK, carry=init)(body)
                    for j in range(16):
                        out_v[pl.ds(d * block + j * SC_LANES, SC_LANES)] = acc[j]
            pltpu.sync_copy(out_v, out_hbm.at[tok])

    return run(v_table, idx, wgt)


def _expert_slabs(u_table, v_table):
    n = u_table.shape[0]
    u = u_table.reshape(n, 1, SLAB_ROWS, LANE)
    v = v_table.reshape(n, 1, SLAB_ROWS, LANE)
    return jnp.concatenate([u, v], axis=1)


def _rope_tables(seq):
    rows = seq // GRID_W
    r = jnp.repeat(jnp.arange(rows, dtype=F32), GRID_W)
    col = jnp.tile(jnp.arange(GRID_W, dtype=F32), rows)
    nf = ROPE_AXIS // 2
    freqs = jnp.power(ROPE_BASE, -jnp.arange(nf, dtype=F32) / nf)
    ar, ac = r[:, None] * freqs, col[:, None] * freqs
    pad = jnp.zeros((seq, LANE - MLA_ROPE), F32)
    cos = jnp.concatenate([jnp.cos(ar), jnp.cos(ar), jnp.cos(ac), jnp.cos(ac), pad], axis=-1)
    sin = jnp.concatenate([-jnp.sin(ar), jnp.sin(ar), -jnp.sin(ac), jnp.sin(ac), pad], axis=-1)
    return cos, sin


def _prep_weights(norm1_w, w_in, q_norm_w, w_uq, kv_norm_w, w_ukv, q_head_norm_w, k_head_norm_w,
                  ret_gn_w, w_out, norm2_w, w_pq, sub_keys1, sub_keys2):
    cut = Q_RANK + KV_RANK + MLA_ROPE
    w_in_p = jnp.concatenate([w_in[:, :cut], jnp.zeros((D_MODEL, LANE - MLA_ROPE), F32), w_in[:, cut:]], axis=1)
    wq = w_uq.reshape(Q_RANK, MLA_HEADS, MLA_QK)
    wq = jnp.pad(wq, ((0, 0), (0, 0), (0, HEAD_PAD - MLA_QK))).reshape(Q_RANK, MLA_HEADS * HEAD_PAD)

    def head_w(v):
        return jnp.pad(v, (0, HEAD_PAD - MLA_QK)).reshape(1, HEAD_PAD)

    return {
        "norm1_w": norm1_w.reshape(1, -1), "w_in": w_in_p.astype(BF16),
        "q_norm_w": q_norm_w.reshape(1, -1), "kv_norm_w": kv_norm_w.reshape(1, -1),
        "w_uq": wq.astype(BF16), "w_ukv": w_ukv.astype(BF16),
        "q_head_norm_w": head_w(q_head_norm_w), "k_head_norm_w": head_w(k_head_norm_w),
        "ret_gn_w": ret_gn_w.reshape(1, -1), "w_out": w_out.astype(BF16), "norm2_w": norm2_w.reshape(1, -1),
        "w_pq": w_pq.astype(BF16).reshape(D_MODEL, PEER_HEADS, PEER_QDIM).transpose(1, 2, 0), "sub_keys1": sub_keys1.astype(BF16), "sub_keys2": sub_keys2.astype(BF16),
    }


def _trunk(x, mod, w, decay_logit, ctx, after=None):
    batch, seq, _ = x.shape
    x2 = x.reshape(batch * seq, D_MODEL)
    use_rope = ctx is not None
    if use_rope:
        cos, sin = _rope_tables(seq)
    else:
        cos = sin = jnp.zeros((ROW_TILE, LANE), F32)
    ckv, krope, q, k, v, zr = _in_proj(x2, mod, w, cos, sin, seq, use_rope, after)
    if use_rope:
        ckv_c, krope_c, state_c = ctx
        past = ckv_c.shape[1]
        kr_c = jnp.pad(krope_c.reshape(batch * past, MLA_ROPE), ((0, 0), (0, LANE - MLA_ROPE)))
        ctx_kv = _kv_up(ckv_c.reshape(batch * past, KV_RANK), kr_c, w)
        s0 = state_c.reshape(batch * 2, RET_HEADS, RET_DK, RET_DV)
    else:
        ctx_kv, s0 = None, None
    att = _attention(q, k, v, batch, seq, ctx_kv)
    o_f, o_b, s_f, s_b = _retention(zr, decay_logit, batch, seq, s0)
    x1, h2 = _mix(x2, att, o_f, o_b, zr, mod, w, seq)
    states = jnp.stack([s_f, s_b], axis=1)
    return x1, h2, ckv.reshape(batch, seq, KV_RANK), krope.reshape(batch, seq, MLA_ROPE), states


def _sc_select(h2, w, tile0, ntiles, after=None):
    gt, et = _peer_score(h2, w, tile0, ntiles, after)
    return et.T.astype(jnp.int32), gt.T


def _peer_split(ctx_sc, x_ctx, x_lat, mod_ctx, mod_lat, seq_ctx, seq_lat, w, tab):
    (x1c, h2c), (x1l, h2l) = x_ctx, x_lat
    tiles_l = h2l.shape[0] // GATHER_TILE
    sc_tiles = tiles_l * SC_SHARE[0] // SC_SHARE[1]
    tc_tiles = tiles_l - sc_tiles
    first = tc_tiles * TC_FIRST[0] // TC_FIRST[1]

    idx_c, gate_c, pre_c = ctx_sc
    wgt_c = _peer_act(pre_c, gate_c, h2l)
    out_c = _sc_vsum(w["v_table"], idx_c, wgt_c)

    idx_l, gate_l = _sc_select(h2l, w, tc_tiles, sc_tiles, wgt_c)
    pre_l = _sc_udot(w["u_table"], idx_l, h2l[tc_tiles * GATHER_TILE:])
    y = _peer(tab, h2l, x1l, mod_lat, w, seq_lat, 0, first)
    out_l = _sc_vsum(w["v_table"], idx_l, _peer_act(pre_l, gate_l, y))
    y = _peer(tab, h2l, x1l, mod_lat, w, seq_lat, first, tc_tiles - first, y_prev=y)
    y_lat = _peer_residual(x1l, out_l, mod_lat, seq_lat, tc_tiles, y)
    y_ctx = _peer_residual(x1c, out_c, mod_ctx, seq_ctx, 0, None)
    return y_ctx, y_lat


def kernel(x_prompt, x_sample, c, cache_ckv, cache_krope, state_ret, c_ctx, w_ada, b_ada, norm1_w, w_in,
           q_norm_w, w_uq, kv_norm_w, w_ukv, q_head_norm_w, k_head_norm_w, ret_decay_logit, ret_gn_w,
           w_out, norm2_w, w_pq, sub_keys1, sub_keys2, u_table, v_table):
    depth = w_ada.shape[0]
    nb_ctx = x_prompt.shape[0]
    nb_lat = x_sample.shape[0]
    y_prompt, y_sample = x_prompt, x_sample
    ckv_list, krope_list, ret_list = [], [], []
    for l in range(depth):
        cond_rows = -(-(nb_lat + 1) // 8) * 8
        cond = jnp.concatenate([c, c_ctx[None, :], jnp.zeros((cond_rows - nb_lat - 1, D_MODEL), F32)], axis=0)
        mod = _ada(cond, w_ada[l], b_ada[l])
        mod_lat = mod[:nb_lat].reshape(nb_lat, 6, D_MODEL)
        mod_ctx = jnp.broadcast_to(mod[nb_lat].reshape(1, 6, D_MODEL), (nb_ctx, 6, D_MODEL))
        w = _prep_weights(norm1_w[l], w_in[l], q_norm_w[l], w_uq[l], kv_norm_w[l], w_ukv[l], q_head_norm_w[l],
                          k_head_norm_w[l], ret_gn_w[l], w_out[l], norm2_w[l], w_pq[l], sub_keys1[l], sub_keys2[l])
        tab = _expert_slabs(u_table[l], v_table[l])
        w["u_table"], w["v_table"] = u_table[l], v_table[l]
        x1c, h2c, ckv_l, krope_l, ret_l = _trunk(y_prompt, mod_ctx, w, ret_decay_logit[l], None)
        ckv_list.append(ckv_l)
        krope_list.append(krope_l)
        ret_list.append(ret_l)
        idx_c, gate_c = _sc_select(h2c, w, 0, h2c.shape[0] // GATHER_TILE)
        pre_c = _sc_udot(w["u_table"], idx_c, h2c)
        x1l, h2l, _, _, _ = _trunk(y_sample, mod_lat, w, ret_decay_logit[l],
                                   (cache_ckv[:, l], cache_krope[:, l], state_ret[:, l]), after=idx_c)
        y_ctx, y_lat = _peer_split((idx_c, gate_c, pre_c), (x1c, h2c), (x1l, h2l), mod_ctx, mod_lat,
                                   y_prompt.shape[1], y_sample.shape[1], w, tab)
        y_prompt = y_ctx.reshape(y_prompt.shape)
        y_sample = y_lat.reshape(y_sample.shape)
    return (y_prompt, y_sample, jnp.stack(ckv_list, axis=1), jnp.stack(krope_list, axis=1),
            jnp.stack(ret_list, axis=1))
```

```python
import functools

import jax
import jax.numpy as jnp
from jax import lax
from jax.experimental import pallas as pl
from jax.experimental.pallas import tpu as pltpu
from jax.experimental.pallas import tpu_sc as plsc

F32 = jnp.float32
BF16 = jnp.bfloat16

D_MODEL = 1024
GRID_W = 64
MLA_HEADS = 4
MLA_NOPE = 128
MLA_ROPE = 64
MLA_QK = MLA_NOPE + MLA_ROPE
MLA_V = 128
Q_RANK = 512
KV_RANK = 256
ROPE_AXIS = MLA_ROPE // 2
ROPE_BASE = 10000.0
RET_HEADS = 4
RET_DK = 128
RET_DV = 128
RET_CHUNK = 128
PEER_HEADS = 8
PEER_QDIM = 256
PEER_HALF = PEER_QDIM // 2
N_KEYS = 128
PEER_TOPK = 16
EPS = 1e-6

LANE = 128
HEAD_PAD = 2 * LANE
RET_W = 4 * RET_HEADS * RET_DK
IN_PAD = Q_RANK + KV_RANK + LANE + RET_W
VMEM_LIMIT = 48 * 1024 * 1024

ROW_TILE = 256
Q_TILE = 256
GATHER_TILE = 128
GATHER_SLOTS = 4
GATHER_AHEAD = GATHER_SLOTS - 1
EXPERTS_PER_TOKEN = PEER_HEADS * PEER_TOPK
SLAB_ROWS = D_MODEL // LANE


def _params(*sem):
    return pltpu.CompilerParams(dimension_semantics=sem, vmem_limit_bytes=VMEM_LIMIT)


def _rms(x, w):
    return x * lax.rsqrt(jnp.mean(x * x, axis=-1, keepdims=True) + EPS) * w


def _mm(a, b):
    return jnp.dot(a.astype(BF16), b.astype(BF16), preferred_element_type=F32)


def _mm_nt(a, b):
    return lax.dot_general(a.astype(BF16), b.astype(BF16), (((1,), (1,)), ((), ())),
                           preferred_element_type=F32)


def _mm_tn(a, b):
    return lax.dot_general(a.astype(BF16), b.astype(BF16), (((0,), (0,)), ((), ())),
                           preferred_element_type=F32)


def _ada_kernel(c_ref, w_ref, b_ref, o_ref):
    c = c_ref[...]
    o_ref[...] = _mm(c * jax.nn.sigmoid(c), w_ref[...]) + b_ref[...]


def _ada(cond, w_ada, b_ada):
    rows, d = cond.shape
    n = w_ada.shape[1]
    tn = 1536
    return pl.pallas_call(
        _ada_kernel,
        grid=(n // tn,),
        in_specs=[pl.BlockSpec((rows, d), lambda j: (0, 0)),
                  pl.BlockSpec((d, tn), lambda j: (0, j)),
                  pl.BlockSpec((1, tn), lambda j: (0, j))],
        out_specs=pl.BlockSpec((rows, tn), lambda j: (0, j)),
        out_shape=jax.ShapeDtypeStruct((rows, n), F32),
        compiler_params=_params("arbitrary"),
        name="ada",
    )(cond, w_ada, b_ada.reshape(1, n))


def _rope_tile(x, cos, sin):
    lane = lax.broadcasted_iota(jnp.int32, x.shape, 1)
    partner = jnp.where((lane % 32) < 16, pltpu.roll(x, LANE - 16, 1), pltpu.roll(x, 16, 1))
    return x * cos + partner * sin


def _kv_heads(kv, kr, khw, cos, sin, use_rope):
    krw = kr * khw[:, LANE:]
    if use_rope:
        krw = _rope_tile(krw, cos, sin)
    ssq_r = jnp.sum(kr * kr, axis=-1, keepdims=True)
    ks, vs = [], []
    for hd in range(MLA_HEADS):
        kn = kv[:, hd * HEAD_PAD: hd * HEAD_PAD + LANE]
        r = lax.rsqrt((jnp.sum(kn * kn, axis=-1, keepdims=True) + ssq_r) / MLA_QK + EPS)
        ks += [kn * r * khw[:, :LANE], krw * r]
        vs.append(kv[:, hd * HEAD_PAD + LANE: (hd + 1) * HEAD_PAD])
    return jnp.concatenate(ks, axis=-1).astype(BF16), jnp.concatenate(vs, axis=-1).astype(BF16)


def _inproj_kernel(x_ref, mod_ref, n1w_ref, win_ref, qnw_ref, kvnw_ref, wuq_ref, wukv_ref, qhw_ref, khw_ref,
                   cos_ref, sin_ref, *rest, use_rope):
    ckv_ref, krope_ref, q_ref, k_ref, v_ref, zr_ref = rest[-6:]
    m = mod_ref[0]
    h = _rms(x_ref[...], n1w_ref[...]) * (1.0 + m[1:2]) + m[0:1]
    z = _mm(h, win_ref[...])
    kr = z[:, Q_RANK + KV_RANK: Q_RANK + KV_RANK + LANE]
    zr_ref[...] = z[:, Q_RANK + KV_RANK + LANE:]
    ckvn = _rms(z[:, Q_RANK: Q_RANK + KV_RANK], kvnw_ref[...])
    ckv_ref[...] = ckvn
    krope_ref[...] = kr[:, :MLA_ROPE]
    cos, sin = cos_ref[...], sin_ref[...]

    q = _mm(_rms(z[:, :Q_RANK], qnw_ref[...]), wuq_ref[...])
    qhw = qhw_ref[...]
    qs = []
    for hd in range(MLA_HEADS):
        qn = q[:, hd * HEAD_PAD: hd * HEAD_PAD + LANE]
        qr = q[:, hd * HEAD_PAD + LANE: (hd + 1) * HEAD_PAD]
        ssq = jnp.sum(qn * qn, axis=-1, keepdims=True) + jnp.sum(qr * qr, axis=-1, keepdims=True)
        r = lax.rsqrt(ssq / MLA_QK + EPS)
        qrw = qr * qhw[:, LANE:]
        if use_rope:
            qrw = _rope_tile(qrw, cos, sin)
        qs += [qn * r * qhw[:, :LANE], qrw * r]
    q_ref[...] = jnp.concatenate(qs, axis=-1).astype(BF16)

    k, v = _kv_heads(_mm(ckvn, wukv_ref[...]), kr, khw_ref[...], cos, sin, use_rope)
    k_ref[...] = k
    v_ref[...] = v


def _in_proj(x, mod, w, cos, sin, seq, use_rope, after=None):
    t = x.shape[0]
    tm = ROW_TILE
    per_batch = seq // tm
    const = lambda i: (0, 0)
    row = lambda i: (i, 0)
    if use_rope:
        pos = lambda i: (i % per_batch, 0)
    else:
        pos = const
    hp = MLA_HEADS * HEAD_PAD
    ordering = [] if after is None else [after]
    return pl.pallas_call(
        functools.partial(_inproj_kernel, use_rope=use_rope),
        grid=(t // tm,),
        in_specs=[pl.BlockSpec((tm, D_MODEL), row),
                  pl.BlockSpec((1, 6, D_MODEL), lambda i: (i // per_batch, 0, 0)),
                  pl.BlockSpec((1, D_MODEL), const),
                  pl.BlockSpec((D_MODEL, IN_PAD), const),
                  pl.BlockSpec((1, Q_RANK), const),
                  pl.BlockSpec((1, KV_RANK), const),
                  pl.BlockSpec((Q_RANK, hp), const),
                  pl.BlockSpec((KV_RANK, hp), const),
                  pl.BlockSpec((1, HEAD_PAD), const),
                  pl.BlockSpec((1, HEAD_PAD), const),
                  pl.BlockSpec((tm, LANE), pos),
                  pl.BlockSpec((tm, LANE), pos)] + [pl.BlockSpec(memory_space=pl.ANY)] * len(ordering),
        out_specs=[pl.BlockSpec((tm, KV_RANK), row),
                   pl.BlockSpec((tm, MLA_ROPE), row),
                   pl.BlockSpec((tm, hp), row),
                   pl.BlockSpec((tm, hp), row),
                   pl.BlockSpec((tm, MLA_HEADS * MLA_V), row),
                   pl.BlockSpec((tm, RET_W), row)],
        out_shape=[jax.ShapeDtypeStruct((t, KV_RANK), F32),
                   jax.ShapeDtypeStruct((t, MLA_ROPE), F32),
                   jax.ShapeDtypeStruct((t, hp), BF16),
                   jax.ShapeDtypeStruct((t, hp), BF16),
                   jax.ShapeDtypeStruct((t, MLA_HEADS * MLA_V), BF16),
                   jax.ShapeDtypeStruct((t, RET_W), F32)],
        compiler_params=_params("arbitrary"),
        name="in_proj",
    )(x, mod, w["norm1_w"], w["w_in"], w["q_norm_w"], w["kv_norm_w"], w["w_uq"], w["w_ukv"],
      w["q_head_norm_w"], w["k_head_norm_w"], cos, sin, *ordering)


def _kvup_kernel(ckv_ref, kr_ref, wukv_ref, khw_ref, k_ref, v_ref):
    k, v = _kv_heads(_mm(ckv_ref[...], wukv_ref[...]), kr_ref[...], khw_ref[...], None, None, False)
    k_ref[...] = k
    v_ref[...] = v


def _kv_up(ckv, kr, w):
    t = ckv.shape[0]
    tm = ROW_TILE
    hp = MLA_HEADS * HEAD_PAD
    const = lambda i: (0, 0)
    row = lambda i: (i, 0)
    return pl.pallas_call(
        _kvup_kernel,
        grid=(t // tm,),
        in_specs=[pl.BlockSpec((tm, KV_RANK), row),
                  pl.BlockSpec((tm, LANE), row),
                  pl.BlockSpec((KV_RANK, hp), const),
                  pl.BlockSpec((1, HEAD_PAD), const)],
        out_specs=[pl.BlockSpec((tm, hp), row),
                   pl.BlockSpec((tm, MLA_HEADS * MLA_V), row)],
        out_shape=[jax.ShapeDtypeStruct((t, hp), BF16),
                   jax.ShapeDtypeStruct((t, MLA_HEADS * MLA_V), BF16)],
        compiler_params=_params("arbitrary"),
        name="kv_up",
    )(ckv, kr, w["w_ukv"], w["k_head_norm_w"])


def _attn_kernel(*refs, has_ctx):
    if has_ctx:
        q_ref, k_ref, v_ref, kc_ref, vc_ref, o_ref = refs
    else:
        q_ref, k_ref, v_ref, o_ref = refs
    scale = MLA_QK ** -0.5
    q = q_ref[...]
    s = _mm_nt(q, k_ref[...]) * scale
    m = jnp.max(s, axis=-1, keepdims=True)
    if has_ctx:
        sc = _mm_nt(q, kc_ref[...]) * scale
        m = jnp.maximum(m, jnp.max(sc, axis=-1, keepdims=True))
    p = jnp.exp(s - m)
    den = jnp.sum(p, axis=-1, keepdims=True)
    o = _mm(p, v_ref[...])
    if has_ctx:
        pc = jnp.exp(sc - m)
        den = den + jnp.sum(pc, axis=-1, keepdims=True)
        o = o + _mm(pc, vc_ref[...])
    o_ref[...] = (o / den).astype(BF16)


def _attention(q, k, v, batch, seq, ctx_kv):
    tq = min(Q_TILE, seq)
    nq = seq // tq
    has_ctx = ctx_kv is not None
    qmap = lambda b, h, i: (b * nq + i, h)
    kvmap = lambda b, h, i: (b, h)
    in_specs = [pl.BlockSpec((tq, HEAD_PAD), qmap),
                pl.BlockSpec((seq, HEAD_PAD), kvmap),
                pl.BlockSpec((seq, MLA_V), kvmap)]
    args = [q, k, v]
    if has_ctx:
        kc, vc = ctx_kv
        past = kc.shape[0] // batch
        in_specs += [pl.BlockSpec((past, HEAD_PAD), kvmap), pl.BlockSpec((past, MLA_V), kvmap)]
        args += [kc, vc]
    return pl.pallas_call(
        functools.partial(_attn_kernel, has_ctx=has_ctx),
        grid=(batch, MLA_HEADS, nq),
        in_specs=in_specs,
        out_specs=pl.BlockSpec((tq, MLA_V), qmap),
        out_shape=jax.ShapeDtypeStruct((batch * seq, MLA_HEADS * MLA_V), BF16),
        compiler_params=_params("arbitrary", "arbitrary", "arbitrary"),
        name="attention",
    )(*args)


def _ret_kernel(*refs, has_s0):
    if has_s0:
        (lg_ref, qf_ref, kf_ref, vf_ref, qb_ref, kb_ref, vb_ref, s0f_ref, s0b_ref,
         of_ref, ob_ref, sf_out, sb_out, sf_scr, sb_scr) = refs
    else:
        (lg_ref, qf_ref, kf_ref, vf_ref, qb_ref, kb_ref, vb_ref,
         of_ref, ob_ref, sf_out, sb_out, sf_scr, sb_scr) = refs
    c = pl.program_id(1)
    cs = RET_CHUNK

    @pl.when(c == 0)
    def _():
        if has_s0:
            sf_scr[...] = s0f_ref[0]
            sb_scr[...] = s0b_ref[0]
        else:
            sf_scr[...] = jnp.zeros_like(sf_scr)
            sb_scr[...] = jnp.zeros_like(sb_scr)

    ii = lax.broadcasted_iota(jnp.int32, (cs, cs), 0).astype(F32)
    jj = lax.broadcasted_iota(jnp.int32, (cs, cs), 1).astype(F32)
    rel = ii - jj
    kscale = RET_DK ** -0.5

    def chunk(q, k, v, s, intra, qdec, kdec, cdec):
        a = _mm_nt(q, k) * intra
        o = _mm(a, v) + _mm(q, s) * qdec
        return o, s * cdec + _mm_tn(k * kdec, v)

    for hd in range(RET_HEADS):
        cols = slice(hd * RET_DK, (hd + 1) * RET_DK)
        lgf = jax.nn.log_sigmoid(jnp.full((1, LANE), lg_ref[0, hd], F32))
        lgb = jax.nn.log_sigmoid(jnp.full((1, LANE), lg_ref[1, hd], F32))

        intra_f = jnp.where(rel >= 0, jnp.exp(jnp.maximum(rel, 0.0) * lgf), 0.0)
        o_f, s_f = chunk(qf_ref[:, cols], kf_ref[:, cols] * kscale, vf_ref[:, cols], sf_scr[hd], intra_f,
                         jnp.exp((ii + 1.0) * lgf), jnp.exp((cs - 1.0 - ii) * lgf), jnp.exp(cs * lgf))
        of_ref[:, cols] = o_f
        sf_scr[hd] = s_f
        sf_out[0, hd] = s_f

        intra_b = jnp.where(rel <= 0, jnp.exp(jnp.maximum(-rel, 0.0) * lgb), 0.0)
        o_b, s_b = chunk(qb_ref[:, cols], kb_ref[:, cols] * kscale, vb_ref[:, cols], sb_scr[hd], intra_b,
                         jnp.exp((cs - ii) * lgb), jnp.exp(ii * lgb), jnp.exp(cs * lgb))
        ob_ref[:, cols] = o_b
        sb_scr[hd] = s_b
        sb_out[0, hd] = s_b


def _retention(zr, decay_logit, batch, seq, s0):
    cs = RET_CHUNK
    nc = seq // cs
    nh = RET_HEADS
    width = nh * RET_DK
    has_s0 = s0 is not None

    def fwd(col):
        return pl.BlockSpec((cs, width), lambda b, c: (b * nc + c, col))

    def bwd(col):
        return pl.BlockSpec((cs, width), lambda b, c: (b * nc + nc - 1 - c, col))

    def state(d):
        return pl.BlockSpec((1, nh, RET_DK, RET_DV), lambda b, c: (b * 2 + d, 0, 0, 0))

    state_out = pl.BlockSpec((1, nh, RET_DK, RET_DV), lambda b, c: (b, 0, 0, 0))
    in_specs = [pl.BlockSpec(memory_space=pltpu.SMEM), fwd(0), fwd(1), fwd(2), bwd(0), bwd(1), bwd(2)]
    args = [decay_logit, zr, zr, zr, zr, zr, zr]
    if has_s0:
        in_specs += [state(0), state(1)]
        args += [s0, s0]
    t = batch * seq
    o_f, o_b, s_f, s_b = pl.pallas_call(
        functools.partial(_ret_kernel, has_s0=has_s0),
        grid=(batch, nc),
        in_specs=in_specs,
        out_specs=[pl.BlockSpec((cs, width), lambda b, c: (b * nc + c, 0)),
                   pl.BlockSpec((cs, width), lambda b, c: (b * nc + nc - 1 - c, 0)),
                   state_out, state_out],
        out_shape=[jax.ShapeDtypeStruct((t, width), F32),
                   jax.ShapeDtypeStruct((t, width), F32),
                   jax.ShapeDtypeStruct((batch, nh, RET_DK, RET_DV), F32),
                   jax.ShapeDtypeStruct((batch, nh, RET_DK, RET_DV), F32)],
        scratch_shapes=[pltpu.VMEM((nh, RET_DK, RET_DV), F32), pltpu.VMEM((nh, RET_DK, RET_DV), F32)],
        compiler_params=_params("arbitrary", "arbitrary"),
        name="retention",
    )(*args)
    return o_f, o_b, s_f, s_b


def _mix_kernel(x_ref, att_ref, of_ref, ob_ref, gr_ref, mod_ref, gnw_ref, wout_ref, n2w_ref, x1_ref, h2_ref):
    o = of_ref[...] + ob_ref[...]
    parts = []
    for hd in range(RET_HEADS):
        oh = o[:, hd * RET_DV:(hd + 1) * RET_DV]
        d = oh - jnp.mean(oh, axis=-1, keepdims=True)
        parts.append(d * lax.rsqrt(jnp.mean(d * d, axis=-1, keepdims=True) + EPS))
    g = gr_ref[...]
    ret = (g * jax.nn.sigmoid(g)) * (jnp.concatenate(parts, axis=-1) * gnw_ref[...])
    na = MLA_HEADS * MLA_V
    mixed = _mm(att_ref[...], wout_ref[:na, :]) + _mm(ret, wout_ref[na:, :])
    m = mod_ref[0]
    x1 = x_ref[...] + m[2:3] * mixed
    x1_ref[...] = x1
    h2_ref[...] = _rms(x1, n2w_ref[...]) * (1.0 + m[4:5]) + m[3:4]


def _mix(x, att, o_f, o_b, zr, mod, w, seq):
    t = x.shape[0]
    tm = ROW_TILE
    per_batch = seq // tm
    const = lambda i: (0, 0)
    row = lambda i: (i, 0)
    half = RET_HEADS * RET_DV
    return pl.pallas_call(
        _mix_kernel,
        grid=(t // tm,),
        in_specs=[pl.BlockSpec((tm, D_MODEL), row),
                  pl.BlockSpec((tm, half), row),
                  pl.BlockSpec((tm, half), row),
                  pl.BlockSpec((tm, half), row),
                  pl.BlockSpec((tm, half), lambda i: (i, 3)),
                  pl.BlockSpec((1, 6, D_MODEL), lambda i: (i // per_batch, 0, 0)),
                  pl.BlockSpec((1, half), const),
                  pl.BlockSpec((D_MODEL, D_MODEL), const),
                  pl.BlockSpec((1, D_MODEL), const)],
        out_specs=[pl.BlockSpec((tm, D_MODEL), row), pl.BlockSpec((tm, D_MODEL), row)],
        out_shape=[jax.ShapeDtypeStruct((t, D_MODEL), F32), jax.ShapeDtypeStruct((t, D_MODEL), F32)],
        compiler_params=_params("arbitrary"),
        name="mix",
    )(x, att, o_f, o_b, zr, mod, w["ret_gn_w"], w["w_out"], w["norm2_w"])


def _top16(jobs):
    def body(r, carry):
        for s_ref, vals_ref, pay_ref, payload in jobs:
            n, cols = s_ref.shape
            rows = lax.broadcasted_iota(jnp.int32, (n, cols), 0).astype(F32)
            s = s_ref[...]
            m = jnp.max(s, axis=0, keepdims=True)
            pos = jnp.min(jnp.where(s == m, rows, float(n)), axis=0, keepdims=True)
            hit = rows == pos
            vals_ref[pl.ds(r, 1), :] = m
            if payload is None:
                pay_ref[pl.ds(r, 1), :] = pos
            else:
                pay_ref[pl.ds(r, 1), :] = jnp.sum(jnp.where(hit, payload, 0.0), axis=0, keepdims=True)
            s_ref[...] = jnp.where(hit, -jnp.inf, s)
        return carry

    lax.fori_loop(0, PEER_TOPK, body, 0)


PAIR_COUNTS = tuple(PEER_TOPK // (a + 1) for a in range(PEER_TOPK))
NUM_PAIRS = sum(PAIR_COUNTS)
PAIR_ROWS = -(-NUM_PAIRS // 8) * 8


def _score_head(hbt, wpqt, k1, k2, g_out, e_out, scr):
    s1_scr, s2_scr, c_scr, p_scr, v1_scr, i1_scr, v2_scr, i2_scr, vt_scr, it_scr = scr
    qt = jnp.dot(wpqt, hbt, preferred_element_type=F32)
    s1_scr[...] = _mm(k1, qt[:PEER_HALF, :])
    s2_scr[...] = _mm(k2, qt[PEER_HALF:, :])
    _top16([(s1_scr, v1_scr, i1_scr, None), (s2_scr, v2_scr, i2_scr, None)])
    c_scr[...] = jnp.full(c_scr.shape, -jnp.inf, F32)
    p_scr[...] = jnp.zeros(p_scr.shape, F32)
    off = 0
    for a, nb in enumerate(PAIR_COUNTS):
        c_scr[off:off + nb, :] = v1_scr[a:a + 1, :] + v2_scr[0:nb, :]
        p_scr[off:off + nb, :] = i1_scr[a:a + 1, :] * float(N_KEYS) + i2_scr[0:nb, :]
        off += nb
    _top16([(c_scr, vt_scr, it_scr, p_scr[...])])
    top = vt_scr[...]
    p = jnp.exp(top - jnp.max(top, axis=0, keepdims=True))
    g_out[...] = p / jnp.sum(p, axis=0, keepdims=True)
    e_out[...] = it_scr[...]


def _score_scratch(tb):
    k = PEER_TOPK
    return ([pltpu.VMEM((N_KEYS, tb), F32)] * 2 + [pltpu.VMEM((PAIR_ROWS, tb), F32)] * 2
            + [pltpu.VMEM((k, tb), F32)] * 6)


def _peer_kernel(tab_hbm, h2_ref, h2n_ref, wpq_ref, k1_ref, k2_ref, x1_ref, mod_ref, *rest, ntiles, aliased):
    y_ref, idx_smem, g_scr, e_scr, ei_scr, hb_scr = rest[int(aliased):int(aliased) + 6]
    scratch = rest[int(aliased) + 6:]
    _peer_body(tab_hbm, h2_ref, h2n_ref, wpq_ref, k1_ref, k2_ref, x1_ref, mod_ref, y_ref,
               idx_smem, g_scr, e_scr, ei_scr, hb_scr, *scratch, ntiles=ntiles)


def _peer_body(tab_hbm, h2_ref, h2n_ref, wpq_ref, k1_ref, k2_ref, x1_ref, mod_ref, y_ref,
               idx_smem, g_scr, e_scr, ei_scr, hb_scr, *scratch, ntiles):
    bufs = scratch[:GATHER_SLOTS]
    sem_rows, sem_idx = scratch[GATHER_SLOTS:GATHER_SLOTS + 2]
    score_scr = scratch[GATHER_SLOTS + 2:]
    i = pl.program_id(0)
    tb = GATHER_TILE
    ns = GATHER_SLOTS
    ahead = GATHER_AHEAD
    ne = EXPERTS_PER_TOKEN
    slab = SLAB_ROWS
    topk = PEER_TOPK
    groups_per_head = tb // (ns * PEER_HEADS)
    cur = i % 2
    nxt = 1 - cur
    more = i + 1 < ntiles

    def score(src_ref, hd, slot):
        @pl.when(hd == 0)
        def _():
            hb_scr[...] = src_ref[...].T.astype(BF16)
        head_rows = pl.ds(pl.multiple_of(hd * topk, topk), topk)
        _score_head(hb_scr[...], wpq_ref[hd], k1_ref[...], k2_ref[...],
                    g_scr.at[slot, head_rows], e_scr.at[head_rows], score_scr)

    def publish(slot):
        ei_scr[...] = e_scr[...].T.astype(jnp.int32)
        copy = pltpu.make_async_copy(ei_scr, idx_smem.at[slot], sem_idx.at[0])
        copy.start()
        copy.wait()

    def issue(islot, tok, rslot):
        for kk in range(ne):
            ex = idx_smem[islot, tok, kk]
            pltpu.make_async_copy(tab_hbm.at[ex], bufs[rslot].at[:, pl.ds(kk * slab, slab), :],
                                  sem_rows.at[rslot]).start(priority=kk % 2)

    def wait_rows(rslot):
        pltpu.make_async_copy(bufs[rslot], bufs[rslot], sem_rows.at[rslot]).wait()

    @pl.when(i == 0)
    def _():
        def first(hd, carry):
            score(h2_ref, hd, 0)
            return carry
        lax.fori_loop(0, PEER_HEADS, first, 0)
        publish(0)
        for tok in range(ahead):
            issue(0, tok, tok)

    g2 = mod_ref[0][5:6]
    lane = lax.broadcasted_iota(jnp.int32, (ne, tb), 1)

    def compute(j, rslot):
        rows = bufs[rslot]
        x_row = h2_ref[pl.ds(j, 1), :]
        acc = None
        for c in range(slab):
            u_c = rows[0, pl.ds(c, ne, stride=slab), :]
            term = u_c * x_row[:, c * LANE:(c + 1) * LANE]
            acc = term if acc is None else acc + term
        pre = jnp.sum(acc, axis=-1, keepdims=True)
        act = 0.5 * pre * (1.0 + lax.erf(pre * (2.0 ** -0.5)))
        gate = jnp.sum(jnp.where(lane == j, g_scr[cur], 0.0), axis=-1, keepdims=True)
        wgt = gate * act
        outs = []
        for c in range(slab):
            v_c = rows[1, pl.ds(c, ne, stride=slab), :]
            outs.append(jnp.sum(v_c * wgt, axis=0, keepdims=True))
        out = jnp.concatenate(outs, axis=-1)
        y_ref[pl.ds(j, 1), :] = x1_ref[pl.ds(j, 1), :] + g2 * out

    def phase(hd, carry):
        score(h2n_ref, hd, nxt)

        @pl.when(hd == PEER_HEADS - 1)
        def _():
            publish(nxt)

        def group(gq, inner):
            g = hd * groups_per_head + gq
            for u in range(ns):
                j = g * ns + u
                wait_rows(u)
                jj = j + ahead
                over = jj >= tb
                issue(jnp.where(over, nxt, cur), jnp.where(over, jj - tb, jj), (u + ahead) % ns)
                compute(j, u)
            return inner

        lax.fori_loop(0, groups_per_head, group, 0)
        return carry

    lax.fori_loop(0, PEER_HEADS, phase, 0)

    @pl.when(jnp.logical_not(more))
    def _():
        for u in range(ahead):
            wait_rows(u)


def _peer(tab, h2, x1, mod, w, seq, tile0, ntiles, y_prev=None):
    t = h2.shape[0]
    tb = GATHER_TILE
    per_batch = seq // tb
    row = lambda i: (tile0 + i, 0)
    const = lambda i: (0, 0)
    in_specs = [pl.BlockSpec(memory_space=pl.ANY),
                pl.BlockSpec((tb, D_MODEL), row),
                pl.BlockSpec((tb, D_MODEL), lambda i: (tile0 + jnp.minimum(i + 1, ntiles - 1), 0)),
                pl.BlockSpec((PEER_HEADS, PEER_QDIM, D_MODEL), lambda i: (0, 0, 0)),
                pl.BlockSpec((N_KEYS, PEER_HALF), const),
                pl.BlockSpec((N_KEYS, PEER_HALF), const),
                pl.BlockSpec((tb, D_MODEL), row),
                pl.BlockSpec((1, 6, D_MODEL), lambda i: ((tile0 + i) // per_batch, 0, 0))]
    args = [tab, h2, h2, w["w_pq"], w["sub_keys1"], w["sub_keys2"], x1, mod]
    aliases = {}
    if y_prev is not None:
        in_specs.append(pl.BlockSpec(memory_space=pl.ANY))
        args.append(y_prev)
        aliases = {len(args) - 1: 0}
    return pl.pallas_call(
        functools.partial(_peer_kernel, ntiles=ntiles, aliased=y_prev is not None),
        grid=(ntiles,),
        in_specs=in_specs,
        input_output_aliases=aliases,
        out_specs=pl.BlockSpec((tb, D_MODEL), row),
        out_shape=jax.ShapeDtypeStruct((t, D_MODEL), F32),
        scratch_shapes=[pltpu.SMEM((2, tb, EXPERTS_PER_TOKEN), jnp.int32),
                        pltpu.VMEM((2, EXPERTS_PER_TOKEN, tb), F32),
                        pltpu.VMEM((EXPERTS_PER_TOKEN, tb), F32),
                        pltpu.VMEM((tb, EXPERTS_PER_TOKEN), jnp.int32),
                        pltpu.VMEM((D_MODEL, tb), BF16)]
                       + [pltpu.VMEM((2, EXPERTS_PER_TOKEN * SLAB_ROWS, LANE), F32)] * GATHER_SLOTS
                       + [pltpu.SemaphoreType.DMA((GATHER_SLOTS,)), pltpu.SemaphoreType.DMA((1,))]
                       + _score_scratch(tb),
        compiler_params=pltpu.CompilerParams(dimension_semantics=("arbitrary",),
                                             vmem_limit_bytes=VMEM_LIMIT,
                                             disable_bounds_checks=True),
        name="peer",
    )(*args)


def _score_kernel(h2_ref, wpq_ref, k1_ref, k2_ref, *rest, ordered):
    g_ref, e_ref, hb_scr = rest[int(ordered):int(ordered) + 3]
    score_scr = rest[int(ordered) + 3:]
    hb_scr[...] = h2_ref[...].T.astype(BF16)

    def head(hd, carry):
        head_rows = pl.ds(pl.multiple_of(hd * PEER_TOPK, PEER_TOPK), PEER_TOPK)
        _score_head(hb_scr[...], wpq_ref[hd], k1_ref[...], k2_ref[...], g_ref.at[head_rows], e_ref.at[head_rows],
                    score_scr)
        return carry

    lax.fori_loop(0, PEER_HEADS, head, 0)


def _peer_score(h2, w, tile0, ntiles, after=None):
    tb = GATHER_TILE
    const = lambda i: (0, 0)
    out = pl.BlockSpec((EXPERTS_PER_TOKEN, tb), lambda i: (0, i))
    ordering = [] if after is None else [after]
    return pl.pallas_call(
        functools.partial(_score_kernel, ordered=after is not None),
        grid=(ntiles,),
        in_specs=[pl.BlockSpec((tb, D_MODEL), lambda i: (tile0 + i, 0)),
                  pl.BlockSpec((PEER_HEADS, PEER_QDIM, D_MODEL), lambda i: (0, 0, 0)),
                  pl.BlockSpec((N_KEYS, PEER_HALF), const),
                  pl.BlockSpec((N_KEYS, PEER_HALF), const)] + [pl.BlockSpec(memory_space=pl.ANY)] * len(ordering),
        out_specs=[out, out],
        out_shape=[jax.ShapeDtypeStruct((EXPERTS_PER_TOKEN, ntiles * tb), F32)] * 2,
        scratch_shapes=[pltpu.VMEM((D_MODEL, tb), BF16)] + _score_scratch(tb),
        compiler_params=_params("arbitrary"),
        name="peer_score",
    )(h2, w["w_pq"], w["sub_keys1"], w["sub_keys2"], *ordering)


def _act_kernel(pre_ref, g_ref, after_ref, w_ref):
    del after_ref
    pre = pre_ref[...]
    w_ref[...] = g_ref[...] * (0.5 * pre * (1.0 + lax.erf(pre * (2.0 ** -0.5))))


def _peer_act(pre, gate, after):
    t, ne = pre.shape
    tm = 1024
    row = lambda i: (i, 0)
    return pl.pallas_call(
        _act_kernel,
        grid=(t // tm,),
        in_specs=[pl.BlockSpec((tm, ne), row), pl.BlockSpec((tm, ne), row), pl.BlockSpec(memory_space=pl.ANY)],
        out_specs=pl.BlockSpec((tm, ne), row),
        out_shape=jax.ShapeDtypeStruct((t, ne), F32),
        compiler_params=_params("arbitrary"),
        name="peer_act",
    )(pre, gate, after)


def _residual_kernel(x1_ref, out_ref, mod_ref, *rest):
    rest[-1][...] = x1_ref[...] + mod_ref[0][5:6] * out_ref[...]


def _peer_residual(x1, out, mod, seq, tile0, y_prev):
    tb = GATHER_TILE
    per_batch = seq // tb
    row = lambda i: (tile0 + i, 0)
    in_specs = [pl.BlockSpec((tb, D_MODEL), row),
                pl.BlockSpec((tb, D_MODEL), lambda i: (i, 0)),
                pl.BlockSpec((1, 6, D_MODEL), lambda i: ((tile0 + i) // per_batch, 0, 0))]
    args = [x1, out, mod]
    aliases = {}
    if y_prev is not None:
        in_specs.append(pl.BlockSpec(memory_space=pl.ANY))
        args.append(y_prev)
        aliases = {3: 0}
    return pl.pallas_call(
        _residual_kernel,
        grid=(out.shape[0] // tb,),
        in_specs=in_specs,
        input_output_aliases=aliases,
        out_specs=pl.BlockSpec((tb, D_MODEL), row),
        out_shape=jax.ShapeDtypeStruct(x1.shape, F32),
        compiler_params=_params("arbitrary"),
        name="peer_residual",
    )(*args)


SC_WORKERS = 32
SC_LANES = 16
SC_CHUNK = 16
SC_SHARE = (15, 32)
TC_FIRST = (13, 25)


def _sc_udot(u_table, idx, x, after=None):
    t, ne = idx.shape
    per_w = t // SC_WORKERS
    nchunk = ne // SC_CHUNK
    nl = D_MODEL // SC_LANES
    mesh = plsc.VectorSubcoreMesh(core_axis_name="c", subcore_axis_name="s")

    @functools.partial(
        pl.kernel, out_type=jax.ShapeDtypeStruct((t, ne), F32), mesh=mesh,
        scratch_types=[pltpu.VMEM((ne,), jnp.int32), pltpu.VMEM((D_MODEL,), F32),
                       pltpu.VMEM((2, SC_CHUNK, D_MODEL), F32), pltpu.VMEM((SC_CHUNK, SC_LANES), F32),
                       pltpu.VMEM((ne,), F32), pltpu.SemaphoreType.DMA((2,))],
        compiler_params=pltpu.CompilerParams(needs_layout_passes=False),
        name="sc_udot")
    def run(u_hbm, idx_hbm, x_hbm, *rest):
        pre_hbm, idx_v, x_v, rows, accs, pre_v, sems = rest[-7:]
        wid = lax.axis_index("s") * 2 + lax.axis_index("c")
        lanes = lax.iota(jnp.int32, SC_LANES)

        def gather(c, b):
            return pltpu.make_async_copy(u_hbm.at[idx_v.at[pl.ds(c * SC_CHUNK, SC_CHUNK)]], rows.at[b], sems.at[b])

        @pl.loop(0, per_w)
        def _(tt):
            tok = wid * per_w + tt
            pltpu.sync_copy(idx_hbm.at[tok], idx_v)
            pltpu.sync_copy(x_hbm.at[tok], x_v)
            gather(0, 0).start()
            for c in range(nchunk):
                b = c % 2
                if c + 1 < nchunk:
                    gather(c + 1, 1 - b).start()
                gather(c, b).wait()
                for g in range(SC_CHUNK // 4):
                    def body(j, acc):
                        xv = x_v[pl.ds(j * SC_LANES, SC_LANES)]
                        return tuple(acc[q] + rows[b, g * 4 + q, pl.ds(j * SC_LANES, SC_LANES)] * xv
                                     for q in range(4))
                    acc = plsc.parallel_loop(0, nl, unroll=4,
                                             carry=tuple(jnp.zeros((SC_LANES,), F32) for _ in range(4)))(body)
                    for q in range(4):
                        accs[g * 4 + q, :] = acc[q]
                tot = jnp.zeros((SC_LANES,), F32)
                for l in range(SC_LANES):
                    tot = tot + plsc.load_gather(accs, [lanes, jnp.full((SC_LANES,), l, jnp.int32)])
                pre_v[pl.ds(c * SC_CHUNK, SC_CHUNK)] = tot
            pltpu.sync_copy(pre_v, pre_hbm.at[tok])

    return run(u_table, idx, x, *([] if after is None else [after]))


def _sc_vsum(v_table, idx, wgt):
    t, ne = idx.shape
    per_w = t // SC_WORKERS
    nchunk = ne // SC_CHUNK
    block = 16 * SC_LANES
    mesh = plsc.VectorSubcoreMesh(core_axis_name="c", subcore_axis_name="s")

    @functools.partial(
        pl.kernel, out_type=jax.ShapeDtypeStruct((t, D_MODEL), F32), mesh=mesh,
        scratch_types=[pltpu.VMEM((ne,), jnp.int32), pltpu.VMEM((ne,), F32),
                       pltpu.VMEM((2, SC_CHUNK, D_MODEL), F32), pltpu.VMEM((D_MODEL,), F32),
                       pltpu.SemaphoreType.DMA((2,))],
        compiler_params=pltpu.CompilerParams(needs_layout_passes=False),
        name="sc_vsum")
    def run(v_hbm, idx_hbm, w_hbm, out_hbm, idx_v, w_v, rows, out_v, sems):
        wid = lax.axis_index("s") * 2 + lax.axis_index("c")

        def gather(c, b):
            return pltpu.make_async_copy(v_hbm.at[idx_v.at[pl.ds(c * SC_CHUNK, SC_CHUNK)]], rows.at[b], sems.at[b])

        @pl.loop(0, per_w)
        def _(tt):
            tok = wid * per_w + tt
            pltpu.sync_copy(idx_hbm.at[tok], idx_v)
            pltpu.sync_copy(w_hbm.at[tok], w_v)
            gather(0, 0).start()
            for c in range(nchunk):
                b = c % 2
                if c + 1 < nchunk:
                    gather(c + 1, 1 - b).start()
                gather(c, b).wait()
                for d in range(D_MODEL // block):
                    def body(k, acc):
                        wk = plsc.load_gather(w_v, [jnp.full((SC_LANES,), c * SC_CHUNK, jnp.int32) + k])
                        return tuple(acc[j] + rows[b, k, pl.ds(d * block + j * SC_LANES, SC_LANES)] * wk
                                     for j in range(16))
                    if c == 0:
                        init = tuple(jnp.zeros((SC_LANES,), F32) for _ in range(16))
                    else:
                        init = tuple(out_v[pl.ds(d * block + j * SC_LANES, SC_LANES)] for j in range(16))
                    acc = plsc.parallel_loop(0, SC_CHUNK, carry=init)(body)
                    for j in range(16):
                        out_v[pl.ds(d * block + j * SC_LANES, SC_LANES)] = acc[j]
            pltpu.sync_copy(out_v, out_hbm.at[tok])

    return run(v_table, idx, wgt)


def _expert_slabs(u_table, v_table):
    n = u_table.shape[0]
    u = u_table.reshape(n, 1, SLAB_ROWS, LANE)
    v = v_table.reshape(n, 1, SLAB_ROWS, LANE)
    return jnp.concatenate([u, v], axis=1)


def _rope_tables(seq):
    rows = seq // GRID_W
    r = jnp.repeat(jnp.arange(rows, dtype=F32), GRID_W)
    col = jnp.tile(jnp.arange(GRID_W, dtype=F32), rows)
    nf = ROPE_AXIS // 2
    freqs = jnp.power(ROPE_BASE, -jnp.arange(nf, dtype=F32) / nf)
    ar, ac = r[:, None] * freqs, col[:, None] * freqs
    pad = jnp.zeros((seq, LANE - MLA_ROPE), F32)
    cos = jnp.concatenate([jnp.cos(ar), jnp.cos(ar), jnp.cos(ac), jnp.cos(ac), pad], axis=-1)
    sin = jnp.concatenate([-jnp.sin(ar), jnp.sin(ar), -jnp.sin(ac), jnp.sin(ac), pad], axis=-1)
    return cos, sin


def _prep_weights(norm1_w, w_in, q_norm_w, w_uq, kv_norm_w, w_ukv, q_head_norm_w, k_head_norm_w,
                  ret_gn_w, w_out, norm2_w, w_pq, sub_keys1, sub_keys2):
    cut = Q_RANK + KV_RANK + MLA_ROPE
    w_in_p = jnp.concatenate([w_in[:, :cut], jnp.zeros((D_MODEL, LANE - MLA_ROPE), F32), w_in[:, cut:]], axis=1)
    wq = w_uq.reshape(Q_RANK, MLA_HEADS, MLA_QK)
    wq = jnp.pad(wq, ((0, 0), (0, 0), (0, HEAD_PAD - MLA_QK))).reshape(Q_RANK, MLA_HEADS * HEAD_PAD)

    def head_w(v):
        return jnp.pad(v, (0, HEAD_PAD - MLA_QK)).reshape(1, HEAD_PAD)

    return {
        "norm1_w": norm1_w.reshape(1, -1), "w_in": w_in_p.astype(BF16),
        "q_norm_w": q_norm_w.reshape(1, -1), "kv_norm_w": kv_norm_w.reshape(1, -1),
        "w_uq": wq.astype(BF16), "w_ukv": w_ukv.astype(BF16),
        "q_head_norm_w": head_w(q_head_norm_w), "k_head_norm_w": head_w(k_head_norm_w),
        "ret_gn_w": ret_gn_w.reshape(1, -1), "w_out": w_out.astype(BF16), "norm2_w": norm2_w.reshape(1, -1),
        "w_pq": w_pq.astype(BF16).reshape(D_MODEL, PEER_HEADS, PEER_QDIM).transpose(1, 2, 0), "sub_keys1": sub_keys1.astype(BF16), "sub_keys2": sub_keys2.astype(BF16),
    }


def _trunk(x, mod, w, decay_logit, ctx, after=None):
    batch, seq, _ = x.shape
    x2 = x.reshape(batch * seq, D_MODEL)
    use_rope = ctx is not None
    if use_rope:
        cos, sin = _rope_tables(seq)
    else:
        cos = sin = jnp.zeros((ROW_TILE, LANE), F32)
    ckv, krope, q, k, v, zr = _in_proj(x2, mod, w, cos, sin, seq, use_rope, after)
    if use_rope:
        ckv_c, krope_c, state_c = ctx
        past = ckv_c.shape[1]
        kr_c = jnp.pad(krope_c.reshape(batch * past, MLA_ROPE), ((0, 0), (0, LANE - MLA_ROPE)))
        ctx_kv = _kv_up(ckv_c.reshape(batch * past, KV_RANK), kr_c, w)
        s0 = state_c.reshape(batch * 2, RET_HEADS, RET_DK, RET_DV)
    else:
        ctx_kv, s0 = None, None
    att = _attention(q, k, v, batch, seq, ctx_kv)
    o_f, o_b, s_f, s_b = _retention(zr, decay_logit, batch, seq, s0)
    x1, h2 = _mix(x2, att, o_f, o_b, zr, mod, w, seq)
    states = jnp.stack([s_f, s_b], axis=1)
    return x1, h2, ckv.reshape(batch, seq, KV_RANK), krope.reshape(batch, seq, MLA_ROPE), states


def _sc_select(h2, w, tile0, ntiles, after=None):
    gt, et = _peer_score(h2, w, tile0, ntiles, after)
    return et.T.astype(jnp.int32), gt.T


def _peer_split(ctx_sc, x_ctx, x_lat, mod_ctx, mod_lat, seq_ctx, seq_lat, w, tab):
    (x1c, h2c), (x1l, h2l) = x_ctx, x_lat
    tiles_l = h2l.shape[0] // GATHER_TILE
    sc_tiles = tiles_l * SC_SHARE[0] // SC_SHARE[1]
    tc_tiles = tiles_l - sc_tiles
    first = tc_tiles * TC_FIRST[0] // TC_FIRST[1]

    idx_c, gate_c, pre_c = ctx_sc
    wgt_c = _peer_act(pre_c, gate_c, h2l)
    out_c = _sc_vsum(w["v_table"], idx_c, wgt_c)

    idx_l, gate_l = _sc_select(h2l, w, tc_tiles, sc_tiles, wgt_c)
    pre_l = _sc_udot(w["u_table"], idx_l, h2l[tc_tiles * GATHER_TILE:], out_c)
    y = _peer(tab, h2l, x1l, mod_lat, w, seq_lat, 0, first)
    out_l = _sc_vsum(w["v_table"], idx_l, _peer_act(pre_l, gate_l, y))
    y = _peer(tab, h2l, x1l, mod_lat, w, seq_lat, first, tc_tiles - first, y_prev=y)
    y_lat = _peer_residual(x1l, out_l, mod_lat, seq_lat, tc_tiles, y)
    y_ctx = _peer_residual(x1c, out_c, mod_ctx, seq_ctx, 0, None)
    return y_ctx, y_lat


def kernel(x_prompt, x_sample, c, cache_ckv, cache_krope, state_ret, c_ctx, w_ada, b_ada, norm1_w, w_in,
           q_norm_w, w_uq, kv_norm_w, w_ukv, q_head_norm_w, k_head_norm_w, ret_decay_logit, ret_gn_w,
           w_out, norm2_w, w_pq, sub_keys1, sub_keys2, u_table, v_table):
    depth = w_ada.shape[0]
    nb_ctx = x_prompt.shape[0]
    nb_lat = x_sample.shape[0]
    y_prompt, y_sample = x_prompt, x_sample
    ckv_list, krope_list, ret_list = [], [], []
    for l in range(depth):
        cond_rows = -(-(nb_lat + 1) // 8) * 8
        cond = jnp.concatenate([c, c_ctx[None, :], jnp.zeros((cond_rows - nb_lat - 1, D_MODEL), F32)], axis=0)
        mod = _ada(cond, w_ada[l], b_ada[l])
        mod_lat = mod[:nb_lat].reshape(nb_lat, 6, D_MODEL)
        mod_ctx = jnp.broadcast_to(mod[nb_lat].reshape(1, 6, D_MODEL), (nb_ctx, 6, D_MODEL))
        w = _prep_weights(norm1_w[l], w_in[l], q_norm_w[l], w_uq[l], kv_norm_w[l], w_ukv[l], q_head_norm_w[l],
                          k_head_norm_w[l], ret_gn_w[l], w_out[l], norm2_w[l], w_pq[l], sub_keys1[l], sub_keys2[l])
        tab = _expert_slabs(u_table[l], v_table[l])
        w["u_table"], w["v_table"] = u_table[l], v_table[l]
        x1c, h2c, ckv_l, krope_l, ret_l = _trunk(y_prompt, mod_ctx, w, ret_decay_logit[l], None)
        ckv_list.append(ckv_l)
        krope_list.append(krope_l)
        ret_list.append(ret_l)
        idx_c, gate_c = _sc_select(h2c, w, 0, h2c.shape[0] // GATHER_TILE)
        pre_c = _sc_udot(w["u_table"], idx_c, h2c)
        x1l, h2l, _, _, _ = _trunk(y_sample, mod_lat, w, ret_decay_logit[l],
                                   (cache_ckv[:, l], cache_krope[:, l], state_ret[:, l]), after=idx_c)
        y_ctx, y_lat = _peer_split((idx_c, gate_c, pre_c), (x1c, h2c), (x1l, h2l), mod_ctx, mod_lat,
                                   y_prompt.shape[1], y_sample.shape[1], w, tab)
        y_prompt = y_ctx.reshape(y_prompt.shape)
        y_sample = y_lat.reshape(y_sample.shape)
    return (y_prompt, y_sample, jnp.stack(ckv_list, axis=1), jnp.stack(krope_list, axis=1),
            jnp.stack(ret_list, axis=1))
```

```python
import functools

import jax
import jax.numpy as jnp
from jax import lax
from jax.experimental import pallas as pl
from jax.experimental.pallas import tpu as pltpu
from jax.experimental.pallas import tpu_sc as plsc

F32 = jnp.float32
BF16 = jnp.bfloat16

D_MODEL = 1024
GRID_W = 64
MLA_HEADS = 4
MLA_NOPE = 128
MLA_ROPE = 64
MLA_QK = MLA_NOPE + MLA_ROPE
MLA_V = 128
Q_RANK = 512
KV_RANK = 256
ROPE_AXIS = MLA_ROPE // 2
ROPE_BASE = 10000.0
RET_HEADS = 4
RET_DK = 128
RET_DV = 128
RET_CHUNK = 128
PEER_HEADS = 8
PEER_QDIM = 256
PEER_HALF = PEER_QDIM // 2
N_KEYS = 128
PEER_TOPK = 16
EPS = 1e-6

LANE = 128
HEAD_PAD = 2 * LANE
RET_W = 4 * RET_HEADS * RET_DK
IN_PAD = Q_RANK + KV_RANK + LANE + RET_W
VMEM_LIMIT = 48 * 1024 * 1024

ROW_TILE = 256
Q_TILE = 256
GATHER_TILE = 128
GATHER_SLOTS = 4
GATHER_AHEAD = GATHER_SLOTS - 1
EXPERTS_PER_TOKEN = PEER_HEADS * PEER_TOPK
SLAB_ROWS = D_MODEL // LANE


def _params(*sem):
    return pltpu.CompilerParams(dimension_semantics=sem, vmem_limit_bytes=VMEM_LIMIT)


def _rms(x, w):
    return x * lax.rsqrt(jnp.mean(x * x, axis=-1, keepdims=True) + EPS) * w


def _mm(a, b):
    return jnp.dot(a.astype(BF16), b.astype(BF16), preferred_element_type=F32)


def _mm_nt(a, b):
    return lax.dot_general(a.astype(BF16), b.astype(BF16), (((1,), (1,)), ((), ())),
                           preferred_element_type=F32)


def _mm_tn(a, b):
    return lax.dot_general(a.astype(BF16), b.astype(BF16), (((0,), (0,)), ((), ())),
                           preferred_element_type=F32)


def _ada_kernel(c_ref, w_ref, b_ref, o_ref):
    c = c_ref[...]
    o_ref[...] = _mm(c * jax.nn.sigmoid(c), w_ref[...]) + b_ref[...]


def _ada(cond, w_ada, b_ada):
    rows, d = cond.shape
    n = w_ada.shape[1]
    tn = 1536
    return pl.pallas_call(
        _ada_kernel,
        grid=(n // tn,),
        in_specs=[pl.BlockSpec((rows, d), lambda j: (0, 0)),
                  pl.BlockSpec((d, tn), lambda j: (0, j)),
                  pl.BlockSpec((1, tn), lambda j: (0, j))],
        out_specs=pl.BlockSpec((rows, tn), lambda j: (0, j)),
        out_shape=jax.ShapeDtypeStruct((rows, n), F32),
        compiler_params=_params("arbitrary"),
        name="ada",
    )(cond, w_ada, b_ada.reshape(1, n))


def _rope_tile(x, cos, sin):
    lane = lax.broadcasted_iota(jnp.int32, x.shape, 1)
    partner = jnp.where((lane % 32) < 16, pltpu.roll(x, LANE - 16, 1), pltpu.roll(x, 16, 1))
    return x * cos + partner * sin


def _kv_heads(kv, kr, khw, cos, sin, use_rope):
    krw = kr * khw[:, LANE:]
    if use_rope:
        krw = _rope_tile(krw, cos, sin)
    ssq_r = jnp.sum(kr * kr, axis=-1, keepdims=True)
    ks, vs = [], []
    for hd in range(MLA_HEADS):
        kn = kv[:, hd * HEAD_PAD: hd * HEAD_PAD + LANE]
        r = lax.rsqrt((jnp.sum(kn * kn, axis=-1, keepdims=True) + ssq_r) / MLA_QK + EPS)
        ks += [kn * r * khw[:, :LANE], krw * r]
        vs.append(kv[:, hd * HEAD_PAD + LANE: (hd + 1) * HEAD_PAD])
    return jnp.concatenate(ks, axis=-1).astype(BF16), jnp.concatenate(vs, axis=-1).astype(BF16)


def _inproj_kernel(x_ref, mod_ref, n1w_ref, win_ref, qnw_ref, kvnw_ref, wuq_ref, wukv_ref, qhw_ref, khw_ref,
                   cos_ref, sin_ref, *rest, use_rope):
    ckv_ref, krope_ref, q_ref, k_ref, v_ref, zr_ref = rest[-6:]
    m = mod_ref[0]
    h = _rms(x_ref[...], n1w_ref[...]) * (1.0 + m[1:2]) + m[0:1]
    z = _mm(h, win_ref[...])
    kr = z[:, Q_RANK + KV_RANK: Q_RANK + KV_RANK + LANE]
    zr_ref[...] = z[:, Q_RANK + KV_RANK + LANE:]
    ckvn = _rms(z[:, Q_RANK: Q_RANK + KV_RANK], kvnw_ref[...])
    ckv_ref[...] = ckvn
    krope_ref[...] = kr[:, :MLA_ROPE]
    cos, sin = cos_ref[...], sin_ref[...]

    q = _mm(_rms(z[:, :Q_RANK], qnw_ref[...]), wuq_ref[...])
    qhw = qhw_ref[...]
    qs = []
    for hd in range(MLA_HEADS):
        qn = q[:, hd * HEAD_PAD: hd * HEAD_PAD + LANE]
        qr = q[:, hd * HEAD_PAD + LANE: (hd + 1) * HEAD_PAD]
        ssq = jnp.sum(qn * qn, axis=-1, keepdims=True) + jnp.sum(qr * qr, axis=-1, keepdims=True)
        r = lax.rsqrt(ssq / MLA_QK + EPS)
        qrw = qr * qhw[:, LANE:]
        if use_rope:
            qrw = _rope_tile(qrw, cos, sin)
        qs += [qn * r * qhw[:, :LANE], qrw * r]
    q_ref[...] = jnp.concatenate(qs, axis=-1).astype(BF16)

    k, v = _kv_heads(_mm(ckvn, wukv_ref[...]), kr, khw_ref[...], cos, sin, use_rope)
    k_ref[...] = k
    v_ref[...] = v


def _in_proj(x, mod, w, cos, sin, seq, use_rope, after=None):
    t = x.shape[0]
    tm = ROW_TILE
    per_batch = seq // tm
    const = lambda i: (0, 0)
    row = lambda i: (i, 0)
    if use_rope:
        pos = lambda i: (i % per_batch, 0)
    else:
        pos = const
    hp = MLA_HEADS * HEAD_PAD
    ordering = [] if after is None else [after]
    return pl.pallas_call(
        functools.partial(_inproj_kernel, use_rope=use_rope),
        grid=(t // tm,),
        in_specs=[pl.BlockSpec((tm, D_MODEL), row),
                  pl.BlockSpec((1, 6, D_MODEL), lambda i: (i // per_batch, 0, 0)),
                  pl.BlockSpec((1, D_MODEL), const),
                  pl.BlockSpec((D_MODEL, IN_PAD), const),
                  pl.BlockSpec((1, Q_RANK), const),
                  pl.BlockSpec((1, KV_RANK), const),
                  pl.BlockSpec((Q_RANK, hp), const),
                  pl.BlockSpec((KV_RANK, hp), const),
                  pl.BlockSpec((1, HEAD_PAD), const),
                  pl.BlockSpec((1, HEAD_PAD), const),
                  pl.BlockSpec((tm, LANE), pos),
                  pl.BlockSpec((tm, LANE), pos)] + [pl.BlockSpec(memory_space=pl.ANY)] * len(ordering),
        out_specs=[pl.BlockSpec((tm, KV_RANK), row),
                   pl.BlockSpec((tm, MLA_ROPE), row),
                   pl.BlockSpec((tm, hp), row),
                   pl.BlockSpec((tm, hp), row),
                   pl.BlockSpec((tm, MLA_HEADS * MLA_V), row),
                   pl.BlockSpec((tm, RET_W), row)],
        out_shape=[jax.ShapeDtypeStruct((t, KV_RANK), F32),
                   jax.ShapeDtypeStruct((t, MLA_ROPE), F32),
                   jax.ShapeDtypeStruct((t, hp), BF16),
                   jax.ShapeDtypeStruct((t, hp), BF16),
                   jax.ShapeDtypeStruct((t, MLA_HEADS * MLA_V), BF16),
                   jax.ShapeDtypeStruct((t, RET_W), F32)],
        compiler_params=_params("arbitrary"),
        name="in_proj",
    )(x, mod, w["norm1_w"], w["w_in"], w["q_norm_w"], w["kv_norm_w"], w["w_uq"], w["w_ukv"],
      w["q_head_norm_w"], w["k_head_norm_w"], cos, sin, *ordering)


def _kvup_kernel(ckv_ref, kr_ref, wukv_ref, khw_ref, k_ref, v_ref):
    k, v = _kv_heads(_mm(ckv_ref[...], wukv_ref[...]), kr_ref[...], khw_ref[...], None, None, False)
    k_ref[...] = k
    v_ref[...] = v


def _kv_up(ckv, kr, w):
    t = ckv.shape[0]
    tm = ROW_TILE
    hp = MLA_HEADS * HEAD_PAD
    const = lambda i: (0, 0)
    row = lambda i: (i, 0)
    return pl.pallas_call(
        _kvup_kernel,
        grid=(t // tm,),
        in_specs=[pl.BlockSpec((tm, KV_RANK), row),
                  pl.BlockSpec((tm, LANE), row),
                  pl.BlockSpec((KV_RANK, hp), const),
                  pl.BlockSpec((1, HEAD_PAD), const)],
        out_specs=[pl.BlockSpec((tm, hp), row),
                   pl.BlockSpec((tm, MLA_HEADS * MLA_V), row)],
        out_shape=[jax.ShapeDtypeStruct((t, hp), BF16),
                   jax.ShapeDtypeStruct((t, MLA_HEADS * MLA_V), BF16)],
        compiler_params=_params("arbitrary"),
        name="kv_up",
    )(ckv, kr, w["w_ukv"], w["k_head_norm_w"])


def _attn_kernel(*refs, has_ctx):
    if has_ctx:
        q_ref, k_ref, v_ref, kc_ref, vc_ref, o_ref = refs
    else:
        q_ref, k_ref, v_ref, o_ref = refs
    scale = MLA_QK ** -0.5
    q = q_ref[...]
    s = _mm_nt(q, k_ref[...]) * scale
    m = jnp.max(s, axis=-1, keepdims=True)
    if has_ctx:
        sc = _mm_nt(q, kc_ref[...]) * scale
        m = jnp.maximum(m, jnp.max(sc, axis=-1, keepdims=True))
    p = jnp.exp(s - m)
    den = jnp.sum(p, axis=-1, keepdims=True)
    o = _mm(p, v_ref[...])
    if has_ctx:
        pc = jnp.exp(sc - m)
        den = den + jnp.sum(pc, axis=-1, keepdims=True)
        o = o + _mm(pc, vc_ref[...])
    o_ref[...] = (o / den).astype(BF16)


def _attention(q, k, v, batch, seq, ctx_kv):
    tq = min(Q_TILE, seq)
    nq = seq // tq
    has_ctx = ctx_kv is not None
    qmap = lambda b, h, i: (b * nq + i, h)
    kvmap = lambda b, h, i: (b, h)
    in_specs = [pl.BlockSpec((tq, HEAD_PAD), qmap),
                pl.BlockSpec((seq, HEAD_PAD), kvmap),
                pl.BlockSpec((seq, MLA_V), kvmap)]
    args = [q, k, v]
    if has_ctx:
        kc, vc = ctx_kv
        past = kc.shape[0] // batch
        in_specs += [pl.BlockSpec((past, HEAD_PAD), kvmap), pl.BlockSpec((past, MLA_V), kvmap)]
        args += [kc, vc]
    return pl.pallas_call(
        functools.partial(_attn_kernel, has_ctx=has_ctx),
        grid=(batch, MLA_HEADS, nq),
        in_specs=in_specs,
        out_specs=pl.BlockSpec((tq, MLA_V), qmap),
        out_shape=jax.ShapeDtypeStruct((batch * seq, MLA_HEADS * MLA_V), BF16),
        compiler_params=_params("arbitrary", "arbitrary", "arbitrary"),
        name="attention",
    )(*args)


def _ret_kernel(*refs, has_s0):
    if has_s0:
        (lg_ref, qf_ref, kf_ref, vf_ref, qb_ref, kb_ref, vb_ref, s0f_ref, s0b_ref,
         of_ref, ob_ref, sf_out, sb_out, sf_scr, sb_scr) = refs
    else:
        (lg_ref, qf_ref, kf_ref, vf_ref, qb_ref, kb_ref, vb_ref,
         of_ref, ob_ref, sf_out, sb_out, sf_scr, sb_scr) = refs
    c = pl.program_id(1)
    cs = RET_CHUNK

    @pl.when(c == 0)
    def _():
        if has_s0:
            sf_scr[...] = s0f_ref[0]
            sb_scr[...] = s0b_ref[0]
        else:
            sf_scr[...] = jnp.zeros_like(sf_scr)
            sb_scr[...] = jnp.zeros_like(sb_scr)

    ii = lax.broadcasted_iota(jnp.int32, (cs, cs), 0).astype(F32)
    jj = lax.broadcasted_iota(jnp.int32, (cs, cs), 1).astype(F32)
    rel = ii - jj
    kscale = RET_DK ** -0.5

    def chunk(q, k, v, s, intra, qdec, kdec, cdec):
        a = _mm_nt(q, k) * intra
        o = _mm(a, v) + _mm(q, s) * qdec
        return o, s * cdec + _mm_tn(k * kdec, v)

    for hd in range(RET_HEADS):
        cols = slice(hd * RET_DK, (hd + 1) * RET_DK)
        lgf = jax.nn.log_sigmoid(jnp.full((1, LANE), lg_ref[0, hd], F32))
        lgb = jax.nn.log_sigmoid(jnp.full((1, LANE), lg_ref[1, hd], F32))

        intra_f = jnp.where(rel >= 0, jnp.exp(jnp.maximum(rel, 0.0) * lgf), 0.0)
        o_f, s_f = chunk(qf_ref[:, cols], kf_ref[:, cols] * kscale, vf_ref[:, cols], sf_scr[hd], intra_f,
                         jnp.exp((ii + 1.0) * lgf), jnp.exp((cs - 1.0 - ii) * lgf), jnp.exp(cs * lgf))
        of_ref[:, cols] = o_f
        sf_scr[hd] = s_f
        sf_out[0, hd] = s_f

        intra_b = jnp.where(rel <= 0, jnp.exp(jnp.maximum(-rel, 0.0) * lgb), 0.0)
        o_b, s_b = chunk(qb_ref[:, cols], kb_ref[:, cols] * kscale, vb_ref[:, cols], sb_scr[hd], intra_b,
                         jnp.exp((cs - ii) * lgb), jnp.exp(ii * lgb), jnp.exp(cs * lgb))
        ob_ref[:, cols] = o_b
        sb_scr[hd] = s_b
        sb_out[0, hd] = s_b


def _retention(zr, decay_logit, batch, seq, s0):
    cs = RET_CHUNK
    nc = seq // cs
    nh = RET_HEADS
    width = nh * RET_DK
    has_s0 = s0 is not None

    def fwd(col):
        return pl.BlockSpec((cs, width), lambda b, c: (b * nc + c, col))

    def bwd(col):
        return pl.BlockSpec((cs, width), lambda b, c: (b * nc + nc - 1 - c, col))

    def state(d):
        return pl.BlockSpec((1, nh, RET_DK, RET_DV), lambda b, c: (b * 2 + d, 0, 0, 0))

    state_out = pl.BlockSpec((1, nh, RET_DK, RET_DV), lambda b, c: (b, 0, 0, 0))
    in_specs = [pl.BlockSpec(memory_space=pltpu.SMEM), fwd(0), fwd(1), fwd(2), bwd(0), bwd(1), bwd(2)]
    args = [decay_logit, zr, zr, zr, zr, zr, zr]
    if has_s0:
        in_specs += [state(0), state(1)]
        args += [s0, s0]
    t = batch * seq
    o_f, o_b, s_f, s_b = pl.pallas_call(
        functools.partial(_ret_kernel, has_s0=has_s0),
        grid=(batch, nc),
        in_specs=in_specs,
        out_specs=[pl.BlockSpec((cs, width), lambda b, c: (b * nc + c, 0)),
                   pl.BlockSpec((cs, width), lambda b, c: (b * nc + nc - 1 - c, 0)),
                   state_out, state_out],
        out_shape=[jax.ShapeDtypeStruct((t, width), F32),
                   jax.ShapeDtypeStruct((t, width), F32),
                   jax.ShapeDtypeStruct((batch, nh, RET_DK, RET_DV), F32),
                   jax.ShapeDtypeStruct((batch, nh, RET_DK, RET_DV), F32)],
        scratch_shapes=[pltpu.VMEM((nh, RET_DK, RET_DV), F32), pltpu.VMEM((nh, RET_DK, RET_DV), F32)],
        compiler_params=_params("arbitrary", "arbitrary"),
        name="retention",
    )(*args)
    return o_f, o_b, s_f, s_b


def _mix_kernel(x_ref, att_ref, of_ref, ob_ref, gr_ref, mod_ref, gnw_ref, wout_ref, n2w_ref, x1_ref, h2_ref):
    o = of_ref[...] + ob_ref[...]
    parts = []
    for hd in range(RET_HEADS):
        oh = o[:, hd * RET_DV:(hd + 1) * RET_DV]
        d = oh - jnp.mean(oh, axis=-1, keepdims=True)
        parts.append(d * lax.rsqrt(jnp.mean(d * d, axis=-1, keepdims=True) + EPS))
    g = gr_ref[...]
    ret = (g * jax.nn.sigmoid(g)) * (jnp.concatenate(parts, axis=-1) * gnw_ref[...])
    na = MLA_HEADS * MLA_V
    mixed = _mm(att_ref[...], wout_ref[:na, :]) + _mm(ret, wout_ref[na:, :])
    m = mod_ref[0]
    x1 = x_ref[...] + m[2:3] * mixed
    x1_ref[...] = x1
    h2_ref[...] = _rms(x1, n2w_ref[...]) * (1.0 + m[4:5]) + m[3:4]


def _mix(x, att, o_f, o_b, zr, mod, w, seq):
    t = x.shape[0]
    tm = ROW_TILE
    per_batch = seq // tm
    const = lambda i: (0, 0)
    row = lambda i: (i, 0)
    half = RET_HEADS * RET_DV
    return pl.pallas_call(
        _mix_kernel,
        grid=(t // tm,),
        in_specs=[pl.BlockSpec((tm, D_MODEL), row),
                  pl.BlockSpec((tm, half), row),
                  pl.BlockSpec((tm, half), row),
                  pl.BlockSpec((tm, half), row),
                  pl.BlockSpec((tm, half), lambda i: (i, 3)),
                  pl.BlockSpec((1, 6, D_MODEL), lambda i: (i // per_batch, 0, 0)),
                  pl.BlockSpec((1, half), const),
                  pl.BlockSpec((D_MODEL, D_MODEL), const),
                  pl.BlockSpec((1, D_MODEL), const)],
        out_specs=[pl.BlockSpec((tm, D_MODEL), row), pl.BlockSpec((tm, D_MODEL), row)],
        out_shape=[jax.ShapeDtypeStruct((t, D_MODEL), F32), jax.ShapeDtypeStruct((t, D_MODEL), F32)],
        compiler_params=_params("arbitrary"),
        name="mix",
    )(x, att, o_f, o_b, zr, mod, w["ret_gn_w"], w["w_out"], w["norm2_w"])


def _top16(jobs):
    def body(r, carry):
        for s_ref, vals_ref, pay_ref, payload in jobs:
            n, cols = s_ref.shape
            rows = lax.broadcasted_iota(jnp.int32, (n, cols), 0).astype(F32)
            s = s_ref[...]
            m = jnp.max(s, axis=0, keepdims=True)
            pos = jnp.min(jnp.where(s == m, rows, float(n)), axis=0, keepdims=True)
            hit = rows == pos
            vals_ref[pl.ds(r, 1), :] = m
            if payload is None:
                pay_ref[pl.ds(r, 1), :] = pos
            else:
                pay_ref[pl.ds(r, 1), :] = jnp.sum(jnp.where(hit, payload, 0.0), axis=0, keepdims=True)
            s_ref[...] = jnp.where(hit, -jnp.inf, s)
        return carry

    lax.fori_loop(0, PEER_TOPK, body, 0)


PAIR_COUNTS = tuple(PEER_TOPK // (a + 1) for a in range(PEER_TOPK))
NUM_PAIRS = sum(PAIR_COUNTS)
PAIR_ROWS = -(-NUM_PAIRS // 8) * 8


def _score_head(hbt, wpqt, k1, k2, g_out, e_out, scr):
    s1_scr, s2_scr, c_scr, p_scr, v1_scr, i1_scr, v2_scr, i2_scr, vt_scr, it_scr = scr
    qt = jnp.dot(wpqt, hbt, preferred_element_type=F32)
    s1_scr[...] = _mm(k1, qt[:PEER_HALF, :])
    s2_scr[...] = _mm(k2, qt[PEER_HALF:, :])
    _top16([(s1_scr, v1_scr, i1_scr, None), (s2_scr, v2_scr, i2_scr, None)])
    c_scr[...] = jnp.full(c_scr.shape, -jnp.inf, F32)
    p_scr[...] = jnp.zeros(p_scr.shape, F32)
    off = 0
    for a, nb in enumerate(PAIR_COUNTS):
        c_scr[off:off + nb, :] = v1_scr[a:a + 1, :] + v2_scr[0:nb, :]
        p_scr[off:off + nb, :] = i1_scr[a:a + 1, :] * float(N_KEYS) + i2_scr[0:nb, :]
        off += nb
    _top16([(c_scr, vt_scr, it_scr, p_scr[...])])
    top = vt_scr[...]
    p = jnp.exp(top - jnp.max(top, axis=0, keepdims=True))
    g_out[...] = p / jnp.sum(p, axis=0, keepdims=True)
    e_out[...] = it_scr[...]


def _score_scratch(tb):
    k = PEER_TOPK
    return ([pltpu.VMEM((N_KEYS, tb), F32)] * 2 + [pltpu.VMEM((PAIR_ROWS, tb), F32)] * 2
            + [pltpu.VMEM((k, tb), F32)] * 6)


def _peer_kernel(tab_hbm, h2_ref, h2n_ref, wpq_ref, k1_ref, k2_ref, x1_ref, mod_ref, *rest, ntiles, aliased):
    y_ref, idx_smem, g_scr, e_scr, ei_scr, hb_scr = rest[int(aliased):int(aliased) + 6]
    scratch = rest[int(aliased) + 6:]
    _peer_body(tab_hbm, h2_ref, h2n_ref, wpq_ref, k1_ref, k2_ref, x1_ref, mod_ref, y_ref,
               idx_smem, g_scr, e_scr, ei_scr, hb_scr, *scratch, ntiles=ntiles)


def _peer_body(tab_hbm, h2_ref, h2n_ref, wpq_ref, k1_ref, k2_ref, x1_ref, mod_ref, y_ref,
               idx_smem, g_scr, e_scr, ei_scr, hb_scr, *scratch, ntiles):
    bufs = scratch[:GATHER_SLOTS]
    sem_rows, sem_idx = scratch[GATHER_SLOTS:GATHER_SLOTS + 2]
    score_scr = scratch[GATHER_SLOTS + 2:]
    i = pl.program_id(0)
    tb = GATHER_TILE
    ns = GATHER_SLOTS
    ahead = GATHER_AHEAD
    ne = EXPERTS_PER_TOKEN
    slab = SLAB_ROWS
    topk = PEER_TOPK
    groups_per_head = tb // (ns * PEER_HEADS)
    cur = i % 2
    nxt = 1 - cur
    more = i + 1 < ntiles

    def score(src_ref, hd, slot):
        @pl.when(hd == 0)
        def _():
            hb_scr[...] = src_ref[...].T.astype(BF16)
        head_rows = pl.ds(pl.multiple_of(hd * topk, topk), topk)
        _score_head(hb_scr[...], wpq_ref[hd], k1_ref[...], k2_ref[...],
                    g_scr.at[slot, head_rows], e_scr.at[head_rows], score_scr)

    def publish(slot):
        ei_scr[...] = e_scr[...].T.astype(jnp.int32)
        copy = pltpu.make_async_copy(ei_scr, idx_smem.at[slot], sem_idx.at[0])
        copy.start()
        copy.wait()

    def issue(islot, tok, rslot):
        for kk in range(ne):
            ex = idx_smem[islot, tok, kk]
            pltpu.make_async_copy(tab_hbm.at[ex], bufs[rslot].at[:, pl.ds(kk * slab, slab), :],
                                  sem_rows.at[rslot]).start(priority=kk % 2)

    def wait_rows(rslot):
        pltpu.make_async_copy(bufs[rslot], bufs[rslot], sem_rows.at[rslot]).wait()

    @pl.when(i == 0)
    def _():
        def first(hd, carry):
            score(h2_ref, hd, 0)
            return carry
        lax.fori_loop(0, PEER_HEADS, first, 0)
        publish(0)
        for tok in range(ahead):
            issue(0, tok, tok)

    g2 = mod_ref[0][5:6]
    lane = lax.broadcasted_iota(jnp.int32, (ne, tb), 1)

    def compute(j, rslot):
        rows = bufs[rslot]
        x_row = h2_ref[pl.ds(j, 1), :]
        acc = None
        for c in range(slab):
            u_c = rows[0, pl.ds(c, ne, stride=slab), :]
            term = u_c * x_row[:, c * LANE:(c + 1) * LANE]
            acc = term if acc is None else acc + term
        pre = jnp.sum(acc, axis=-1, keepdims=True)
        act = 0.5 * pre * (1.0 + lax.erf(pre * (2.0 ** -0.5)))
        gate = jnp.sum(jnp.where(lane == j, g_scr[cur], 0.0), axis=-1, keepdims=True)
        wgt = gate * act
        outs = []
        for c in range(slab):
            v_c = rows[1, pl.ds(c, ne, stride=slab), :]
            outs.append(jnp.sum(v_c * wgt, axis=0, keepdims=True))
        out = jnp.concatenate(outs, axis=-1)
        y_ref[pl.ds(j, 1), :] = x1_ref[pl.ds(j, 1), :] + g2 * out

    def phase(hd, carry):
        score(h2n_ref, hd, nxt)

        @pl.when(hd == PEER_HEADS - 1)
        def _():
            publish(nxt)

        def group(gq, inner):
            g = hd * groups_per_head + gq
            for u in range(ns):
                j = g * ns + u
                wait_rows(u)
                jj = j + ahead
                over = jj >= tb
                issue(jnp.where(over, nxt, cur), jnp.where(over, jj - tb, jj), (u + ahead) % ns)
                compute(j, u)
            return inner

        lax.fori_loop(0, groups_per_head, group, 0)
        return carry

    lax.fori_loop(0, PEER_HEADS, phase, 0)

    @pl.when(jnp.logical_not(more))
    def _():
        for u in range(ahead):
            wait_rows(u)


def _peer(tab, h2, x1, mod, w, seq, tile0, ntiles, y_prev=None):
    t = h2.shape[0]
    tb = GATHER_TILE
    per_batch = seq // tb
    row = lambda i: (tile0 + i, 0)
    const = lambda i: (0, 0)
    in_specs = [pl.BlockSpec(memory_space=pl.ANY),
                pl.BlockSpec((tb, D_MODEL), row),
                pl.BlockSpec((tb, D_MODEL), lambda i: (tile0 + jnp.minimum(i + 1, ntiles - 1), 0)),
                pl.BlockSpec((PEER_HEADS, PEER_QDIM, D_MODEL), lambda i: (0, 0, 0)),
                pl.BlockSpec((N_KEYS, PEER_HALF), const),
                pl.BlockSpec((N_KEYS, PEER_HALF), const),
                pl.BlockSpec((tb, D_MODEL), row),
                pl.BlockSpec((1, 6, D_MODEL), lambda i: ((tile0 + i) // per_batch, 0, 0))]
    args = [tab, h2, h2, w["w_pq"], w["sub_keys1"], w["sub_keys2"], x1, mod]
    aliases = {}
    if y_prev is not None:
        in_specs.append(pl.BlockSpec(memory_space=pl.ANY))
        args.append(y_prev)
        aliases = {len(args) - 1: 0}
    return pl.pallas_call(
        functools.partial(_peer_kernel, ntiles=ntiles, aliased=y_prev is not None),
        grid=(ntiles,),
        in_specs=in_specs,
        input_output_aliases=aliases,
        out_specs=pl.BlockSpec((tb, D_MODEL), row),
        out_shape=jax.ShapeDtypeStruct((t, D_MODEL), F32),
        scratch_shapes=[pltpu.SMEM((2, tb, EXPERTS_PER_TOKEN), jnp.int32),
                        pltpu.VMEM((2, EXPERTS_PER_TOKEN, tb), F32),
                        pltpu.VMEM((EXPERTS_PER_TOKEN, tb), F32),
                        pltpu.VMEM((tb, EXPERTS_PER_TOKEN), jnp.int32),
                        pltpu.VMEM((D_MODEL, tb), BF16)]
                       + [pltpu.VMEM((2, EXPERTS_PER_TOKEN * SLAB_ROWS, LANE), F32)] * GATHER_SLOTS
                       + [pltpu.SemaphoreType.DMA((GATHER_SLOTS,)), pltpu.SemaphoreType.DMA((1,))]
                       + _score_scratch(tb),
        compiler_params=pltpu.CompilerParams(dimension_semantics=("arbitrary",),
                                             vmem_limit_bytes=VMEM_LIMIT,
                                             disable_bounds_checks=True),
        name="peer",
    )(*args)


def _score_kernel(h2_ref, wpq_ref, k1_ref, k2_ref, *rest, ordered):
    g_ref, e_ref, hb_scr = rest[int(ordered):int(ordered) + 3]
    score_scr = rest[int(ordered) + 3:]
    hb_scr[...] = h2_ref[...].T.astype(BF16)

    def head(hd, carry):
        head_rows = pl.ds(pl.multiple_of(hd * PEER_TOPK, PEER_TOPK), PEER_TOPK)
        _score_head(hb_scr[...], wpq_ref[hd], k1_ref[...], k2_ref[...], g_ref.at[head_rows], e_ref.at[head_rows],
                    score_scr)
        return carry

    lax.fori_loop(0, PEER_HEADS, head, 0)


def _peer_score(h2, w, tile0, ntiles, after=None):
    tb = GATHER_TILE
    const = lambda i: (0, 0)
    out = pl.BlockSpec((EXPERTS_PER_TOKEN, tb), lambda i: (0, i))
    ordering = [] if after is None else [after]
    return pl.pallas_call(
        functools.partial(_score_kernel, ordered=after is not None),
        grid=(ntiles,),
        in_specs=[pl.BlockSpec((tb, D_MODEL), lambda i: (tile0 + i, 0)),
                  pl.BlockSpec((PEER_HEADS, PEER_QDIM, D_MODEL), lambda i: (0, 0, 0)),
                  pl.BlockSpec((N_KEYS, PEER_HALF), const),
                  pl.BlockSpec((N_KEYS, PEER_HALF), const)] + [pl.BlockSpec(memory_space=pl.ANY)] * len(ordering),
        out_specs=[out, out],
        out_shape=[jax.ShapeDtypeStruct((EXPERTS_PER_TOKEN, ntiles * tb), F32)] * 2,
        scratch_shapes=[pltpu.VMEM((D_MODEL, tb), BF16)] + _score_scratch(tb),
        compiler_params=_params("arbitrary"),
        name="peer_score",
    )(h2, w["w_pq"], w["sub_keys1"], w["sub_keys2"], *ordering)


def _act_kernel(pre_ref, g_ref, after_ref, w_ref):
    del after_ref
    pre = pre_ref[...]
    w_ref[...] = g_ref[...] * (0.5 * pre * (1.0 + lax.erf(pre * (2.0 ** -0.5))))


def _peer_act(pre, gate, after):
    t, ne = pre.shape
    tm = 1024
    row = lambda i: (i, 0)
    return pl.pallas_call(
        _act_kernel,
        grid=(t // tm,),
        in_specs=[pl.BlockSpec((tm, ne), row), pl.BlockSpec((tm, ne), row), pl.BlockSpec(memory_space=pl.ANY)],
        out_specs=pl.BlockSpec((tm, ne), row),
        out_shape=jax.ShapeDtypeStruct((t, ne), F32),
        compiler_params=_params("arbitrary"),
        name="peer_act",
    )(pre, gate, after)


def _residual_kernel(x1_ref, out_ref, mod_ref, *rest):
    rest[-1][...] = x1_ref[...] + mod_ref[0][5:6] * out_ref[...]


def _peer_residual(x1, out, mod, seq, tile0, ntiles, out_tile0, y_prev):
    tb = GATHER_TILE
    per_batch = seq // tb
    row = lambda i: (tile0 + i, 0)
    in_specs = [pl.BlockSpec((tb, D_MODEL), row),
                pl.BlockSpec((tb, D_MODEL), lambda i: (out_tile0 + i, 0)),
                pl.BlockSpec((1, 6, D_MODEL), lambda i: ((tile0 + i) // per_batch, 0, 0))]
    args = [x1, out, mod]
    aliases = {}
    if y_prev is not None:
        in_specs.append(pl.BlockSpec(memory_space=pl.ANY))
        args.append(y_prev)
        aliases = {3: 0}
    return pl.pallas_call(
        _residual_kernel,
        grid=(ntiles,),
        in_specs=in_specs,
        input_output_aliases=aliases,
        out_specs=pl.BlockSpec((tb, D_MODEL), row),
        out_shape=jax.ShapeDtypeStruct(x1.shape, F32),
        compiler_params=_params("arbitrary"),
        name="peer_residual",
    )(*args)


SC_WORKERS = 32
SC_LANES = 16
SC_CHUNK = 16
SC_SHARE = (7, 16)
TC_FIRST = (9, 20)


def _sc_udot(u_table, idx, x, after=None):
    t, ne = idx.shape
    per_w = t // SC_WORKERS
    nchunk = ne // SC_CHUNK
    nl = D_MODEL // SC_LANES
    mesh = plsc.VectorSubcoreMesh(core_axis_name="c", subcore_axis_name="s")

    @functools.partial(
        pl.kernel, out_type=jax.ShapeDtypeStruct((t, ne), F32), mesh=mesh,
        scratch_types=[pltpu.VMEM((ne,), jnp.int32), pltpu.VMEM((D_MODEL,), F32),
                       pltpu.VMEM((2, SC_CHUNK, D_MODEL), F32), pltpu.VMEM((SC_CHUNK, SC_LANES), F32),
                       pltpu.VMEM((ne,), F32), pltpu.SemaphoreType.DMA((2,))],
        compiler_params=pltpu.CompilerParams(needs_layout_passes=False),
        name="sc_udot")
    def run(u_hbm, idx_hbm, x_hbm, *rest):
        pre_hbm, idx_v, x_v, rows, accs, pre_v, sems = rest[-7:]
        wid = lax.axis_index("s") * 2 + lax.axis_index("c")
        lanes = lax.iota(jnp.int32, SC_LANES)

        def gather(c, b):
            return pltpu.make_async_copy(u_hbm.at[idx_v.at[pl.ds(c * SC_CHUNK, SC_CHUNK)]], rows.at[b], sems.at[b])

        @pl.loop(0, per_w)
        def _(tt):
            tok = wid * per_w + tt
            pltpu.sync_copy(idx_hbm.at[tok], idx_v)
            pltpu.sync_copy(x_hbm.at[tok], x_v)
            gather(0, 0).start()
            for c in range(nchunk):
                b = c % 2
                if c + 1 < nchunk:
                    gather(c + 1, 1 - b).start()
                gather(c, b).wait()
                for g in range(SC_CHUNK // 4):
                    def body(j, acc):
                        xv = x_v[pl.ds(j * SC_LANES, SC_LANES)]
                        return tuple(acc[q] + rows[b, g * 4 + q, pl.ds(j * SC_LANES, SC_LANES)] * xv
                                     for q in range(4))
                    acc = plsc.parallel_loop(0, nl, unroll=4,
                                             carry=tuple(jnp.zeros((SC_LANES,), F32) for _ in range(4)))(body)
                    for q in range(4):
                        accs[g * 4 + q, :] = acc[q]
                tot = jnp.zeros((SC_LANES,), F32)
                for l in range(SC_LANES):
                    tot = tot + plsc.load_gather(accs, [lanes, jnp.full((SC_LANES,), l, jnp.int32)])
                pre_v[pl.ds(c * SC_CHUNK, SC_CHUNK)] = tot
            pltpu.sync_copy(pre_v, pre_hbm.at[tok])

    return run(u_table, idx, x, *([] if after is None else [after]))


def _sc_vsum(v_table, idx, wgt):
    t, ne = idx.shape
    per_w = t // SC_WORKERS
    nchunk = ne // SC_CHUNK
    block = 16 * SC_LANES
    mesh = plsc.VectorSubcoreMesh(core_axis_name="c", subcore_axis_name="s")

    @functools.partial(
        pl.kernel, out_type=jax.ShapeDtypeStruct((t, D_MODEL), F32), mesh=mesh,
        scratch_types=[pltpu.VMEM((ne,), jnp.int32), pltpu.VMEM((ne,), F32),
                       pltpu.VMEM((2, SC_CHUNK, D_MODEL), F32), pltpu.VMEM((D_MODEL,), F32),
                       pltpu.SemaphoreType.DMA((2,))],
        compiler_params=pltpu.CompilerParams(needs_layout_passes=False),
        name="sc_vsum")
    def run(v_hbm, idx_hbm, w_hbm, out_hbm, idx_v, w_v, rows, out_v, sems):
        wid = lax.axis_index("s") * 2 + lax.axis_index("c")

        def gather(c, b):
            return pltpu.make_async_copy(v_hbm.at[idx_v.at[pl.ds(c * SC_CHUNK, SC_CHUNK)]], rows.at[b], sems.at[b])

        @pl.loop(0, per_w)
        def _(tt):
            tok = wid * per_w + tt
            pltpu.sync_copy(idx_hbm.at[tok], idx_v)
            pltpu.sync_copy(w_hbm.at[tok], w_v)
            gather(0, 0).start()
            for c in range(nchunk):
                b = c % 2
                if c + 1 < nchunk:
                    gather(c + 1, 1 - b).start()
                gather(c, b).wait()
                for d in range(D_MODEL // block):
                    def body(k, acc):
                        wk = plsc.load_gather(w_v, [jnp.full((SC_LANES,), c * SC_CHUNK, jnp.int32) + k])
                        return tuple(acc[j] + rows[b, k, pl.ds(d * block + j * SC_LANES, SC_LANES)] * wk
                                     for j in range(16))
                    if c == 0:
                        init = tuple(jnp.zeros((SC_LANES,), F32) for _ in range(16))
                    else:
                        init = tuple(out_v[pl.ds(d * block + j * SC_LANES, SC_LANES)] for j in range(16))
                    acc = plsc.parallel_loop(0, SC_CHUNK, carry=init)(body)
                    for j in range(16):
                        out_v[pl.ds(d * block + j * SC_LANES, SC_LANES)] = acc[j]
            pltpu.sync_copy(out_v, out_hbm.at[tok])

    return run(v_table, idx, wgt)


def _expert_slabs(u_table, v_table):
    n = u_table.shape[0]
    u = u_table.reshape(n, 1, SLAB_ROWS, LANE)
    v = v_table.reshape(n, 1, SLAB_ROWS, LANE)
    return jnp.concatenate([u, v], axis=1)


def _rope_tables(seq):
    rows = seq // GRID_W
    r = jnp.repeat(jnp.arange(rows, dtype=F32), GRID_W)
    col = jnp.tile(jnp.arange(GRID_W, dtype=F32), rows)
    nf = ROPE_AXIS // 2
    freqs = jnp.power(ROPE_BASE, -jnp.arange(nf, dtype=F32) / nf)
    ar, ac = r[:, None] * freqs, col[:, None] * freqs
    pad = jnp.zeros((seq, LANE - MLA_ROPE), F32)
    cos = jnp.concatenate([jnp.cos(ar), jnp.cos(ar), jnp.cos(ac), jnp.cos(ac), pad], axis=-1)
    sin = jnp.concatenate([-jnp.sin(ar), jnp.sin(ar), -jnp.sin(ac), jnp.sin(ac), pad], axis=-1)
    return cos, sin


def _prep_weights(norm1_w, w_in, q_norm_w, w_uq, kv_norm_w, w_ukv, q_head_norm_w, k_head_norm_w,
                  ret_gn_w, w_out, norm2_w, w_pq, sub_keys1, sub_keys2):
    cut = Q_RANK + KV_RANK + MLA_ROPE
    w_in_p = jnp.concatenate([w_in[:, :cut], jnp.zeros((D_MODEL, LANE - MLA_ROPE), F32), w_in[:, cut:]], axis=1)
    wq = w_uq.reshape(Q_RANK, MLA_HEADS, MLA_QK)
    wq = jnp.pad(wq, ((0, 0), (0, 0), (0, HEAD_PAD - MLA_QK))).reshape(Q_RANK, MLA_HEADS * HEAD_PAD)

    def head_w(v):
        return jnp.pad(v, (0, HEAD_PAD - MLA_QK)).reshape(1, HEAD_PAD)

    return {
        "norm1_w": norm1_w.reshape(1, -1), "w_in": w_in_p.astype(BF16),
        "q_norm_w": q_norm_w.reshape(1, -1), "kv_norm_w": kv_norm_w.reshape(1, -1),
        "w_uq": wq.astype(BF16), "w_ukv": w_ukv.astype(BF16),
        "q_head_norm_w": head_w(q_head_norm_w), "k_head_norm_w": head_w(k_head_norm_w),
        "ret_gn_w": ret_gn_w.reshape(1, -1), "w_out": w_out.astype(BF16), "norm2_w": norm2_w.reshape(1, -1),
        "w_pq": w_pq.astype(BF16).reshape(D_MODEL, PEER_HEADS, PEER_QDIM).transpose(1, 2, 0), "sub_keys1": sub_keys1.astype(BF16), "sub_keys2": sub_keys2.astype(BF16),
    }


def _trunk(x, mod, w, decay_logit, ctx, after=None):
    batch, seq, _ = x.shape
    x2 = x.reshape(batch * seq, D_MODEL)
    use_rope = ctx is not None
    if use_rope:
        cos, sin = _rope_tables(seq)
    else:
        cos = sin = jnp.zeros((ROW_TILE, LANE), F32)
    ckv, krope, q, k, v, zr = _in_proj(x2, mod, w, cos, sin, seq, use_rope, after)
    if use_rope:
        ckv_c, krope_c, state_c = ctx
        past = ckv_c.shape[1]
        kr_c = jnp.pad(krope_c.reshape(batch * past, MLA_ROPE), ((0, 0), (0, LANE - MLA_ROPE)))
        ctx_kv = _kv_up(ckv_c.reshape(batch * past, KV_RANK), kr_c, w)
        s0 = state_c.reshape(batch * 2, RET_HEADS, RET_DK, RET_DV)
    else:
        ctx_kv, s0 = None, None
    att = _attention(q, k, v, batch, seq, ctx_kv)
    o_f, o_b, s_f, s_b = _retention(zr, decay_logit, batch, seq, s0)
    x1, h2 = _mix(x2, att, o_f, o_b, zr, mod, w, seq)
    states = jnp.stack([s_f, s_b], axis=1)
    return x1, h2, ckv.reshape(batch, seq, KV_RANK), krope.reshape(batch, seq, MLA_ROPE), states


def _sc_select(h2, w, tile0, ntiles, after=None):
    gt, et = _peer_score(h2, w, tile0, ntiles, after)
    return et.T.astype(jnp.int32), gt.T


def _peer_split(ctx_sc, x_ctx, x_lat, mod_ctx, mod_lat, seq_ctx, seq_lat, w, tab):
    (x1c, h2c), (x1l, h2l) = x_ctx, x_lat
    tiles_c = h2c.shape[0] // GATHER_TILE
    tiles_l = h2l.shape[0] // GATHER_TILE
    sc_tiles = tiles_l * SC_SHARE[0] // SC_SHARE[1]
    tc_tiles = tiles_l - sc_tiles
    first = tc_tiles * TC_FIRST[0] // TC_FIRST[1]

    idx_c, gate_c, pre_c = ctx_sc
    idx_l, gate_l = _sc_select(h2l, w, tc_tiles, sc_tiles)
    pre_l = _sc_udot(w["u_table"], idx_l, h2l[tc_tiles * GATHER_TILE:])
    y = _peer(tab, h2l, x1l, mod_lat, w, seq_lat, 0, first)
    idx = jnp.concatenate([idx_c, idx_l], axis=0)
    wgt = _peer_act(jnp.concatenate([pre_c, pre_l], axis=0), jnp.concatenate([gate_c, gate_l], axis=0), y)
    out = _sc_vsum(w["v_table"], idx, wgt)
    y = _peer(tab, h2l, x1l, mod_lat, w, seq_lat, first, tc_tiles - first, y_prev=y)
    y_lat = _peer_residual(x1l, out, mod_lat, seq_lat, tc_tiles, sc_tiles, tiles_c, y)
    y_ctx = _peer_residual(x1c, out, mod_ctx, seq_ctx, 0, tiles_c, 0, None)
    return y_ctx, y_lat


def kernel(x_prompt, x_sample, c, cache_ckv, cache_krope, state_ret, c_ctx, w_ada, b_ada, norm1_w, w_in,
           q_norm_w, w_uq, kv_norm_w, w_ukv, q_head_norm_w, k_head_norm_w, ret_decay_logit, ret_gn_w,
           w_out, norm2_w, w_pq, sub_keys1, sub_keys2, u_table, v_table):
    depth = w_ada.shape[0]
    nb_ctx = x_prompt.shape[0]
    nb_lat = x_sample.shape[0]
    y_prompt, y_sample = x_prompt, x_sample
    ckv_list, krope_list, ret_list = [], [], []
    for l in range(depth):
        cond_rows = -(-(nb_lat + 1) // 8) * 8
        cond = jnp.concatenate([c, c_ctx[None, :], jnp.zeros((cond_rows - nb_lat - 1, D_MODEL), F32)], axis=0)
        mod = _ada(cond, w_ada[l], b_ada[l])
        mod_lat = mod[:nb_lat].reshape(nb_lat, 6, D_MODEL)
        mod_ctx = jnp.broadcast_to(mod[nb_lat].reshape(1, 6, D_MODEL), (nb_ctx, 6, D_MODEL))
        w = _prep_weights(norm1_w[l], w_in[l], q_norm_w[l], w_uq[l], kv_norm_w[l], w_ukv[l], q_head_norm_w[l],
                          k_head_norm_w[l], ret_gn_w[l], w_out[l], norm2_w[l], w_pq[l], sub_keys1[l], sub_keys2[l])
        tab = _expert_slabs(u_table[l], v_table[l])
        w["u_table"], w["v_table"] = u_table[l], v_table[l]
        x1c, h2c, ckv_l, krope_l, ret_l = _trunk(y_prompt, mod_ctx, w, ret_decay_logit[l], None)
        ckv_list.append(ckv_l)
        krope_list.append(krope_l)
        ret_list.append(ret_l)
        idx_c, gate_c = _sc_select(h2c, w, 0, h2c.shape[0] // GATHER_TILE)
        pre_c = _sc_udot(w["u_table"], idx_c, h2c)
        x1l, h2l, _, _, _ = _trunk(y_sample, mod_lat, w, ret_decay_logit[l],
                                   (cache_ckv[:, l], cache_krope[:, l], state_ret[:, l]), after=idx_c)
        y_ctx, y_lat = _peer_split((idx_c, gate_c, pre_c), (x1c, h2c), (x1l, h2l), mod_ctx, mod_lat,
                                   y_prompt.shape[1], y_sample.shape[1], w, tab)
        y_prompt = y_ctx.reshape(y_prompt.shape)
        y_sample = y_lat.reshape(y_sample.shape)
    return (y_prompt, y_sample, jnp.stack(ckv_list, axis=1), jnp.stack(krope_list, axis=1),
            jnp.stack(ret_list, axis=1))
```

```python
import functools
import math

import jax
import jax.numpy as jnp
from jax import lax
from jax.experimental import pallas as pl
from jax.experimental.pallas import tpu as pltpu
from jax.experimental.pallas import tpu_sc as plsc

F32 = jnp.float32
BF16 = jnp.bfloat16

D_MODEL = 1024
GRID_W = 64
MLA_HEADS = 4
MLA_NOPE = 128
MLA_ROPE = 64
MLA_QK = MLA_NOPE + MLA_ROPE
MLA_V = 128
Q_RANK = 512
KV_RANK = 256
ROPE_AXIS = MLA_ROPE // 2
ROPE_BASE = 10000.0
RET_HEADS = 4
RET_DK = 128
RET_DV = 128
RET_CHUNK = 128
PEER_HEADS = 8
PEER_QDIM = 256
PEER_HALF = PEER_QDIM // 2
N_KEYS = 128
PEER_TOPK = 16
EPS = 1e-6

LANE = 128
HEAD_PAD = 2 * LANE
RET_W = 4 * RET_HEADS * RET_DK
IN_PAD = Q_RANK + KV_RANK + LANE + RET_W
VMEM_LIMIT = 48 * 1024 * 1024

ROW_TILE = 256
Q_TILE = 256
GATHER_TILE = 128
GATHER_SLOTS = 4
GATHER_AHEAD = GATHER_SLOTS - 1
EXPERTS_PER_TOKEN = PEER_HEADS * PEER_TOPK
SLAB_ROWS = D_MODEL // LANE


def _params(*sem):
    return pltpu.CompilerParams(dimension_semantics=sem, vmem_limit_bytes=VMEM_LIMIT)


def _rms(x, w):
    return x * lax.rsqrt(jnp.mean(x * x, axis=-1, keepdims=True) + EPS) * w


def _mm(a, b):
    return jnp.dot(a.astype(BF16), b.astype(BF16), preferred_element_type=F32)


def _mm_nt(a, b):
    return lax.dot_general(a.astype(BF16), b.astype(BF16), (((1,), (1,)), ((), ())),
                           preferred_element_type=F32)


def _mm_tn(a, b):
    return lax.dot_general(a.astype(BF16), b.astype(BF16), (((0,), (0,)), ((), ())),
                           preferred_element_type=F32)


def _ada_kernel(c_ref, w_ref, b_ref, o_ref):
    c = c_ref[...]
    o_ref[...] = _mm(c * jax.nn.sigmoid(c), w_ref[...]) + b_ref[...]


def _ada(cond, w_ada, b_ada):
    rows, d = cond.shape
    n = w_ada.shape[1]
    tn = 1536
    return pl.pallas_call(
        _ada_kernel,
        grid=(n // tn,),
        in_specs=[pl.BlockSpec((rows, d), lambda j: (0, 0)),
                  pl.BlockSpec((d, tn), lambda j: (0, j)),
                  pl.BlockSpec((1, tn), lambda j: (0, j))],
        out_specs=pl.BlockSpec((rows, tn), lambda j: (0, j)),
        out_shape=jax.ShapeDtypeStruct((rows, n), F32),
        compiler_params=_params("arbitrary"),
        name="ada",
    )(cond, w_ada, b_ada.reshape(1, n))


def _rope_tile(x, cos, sin):
    lane = lax.broadcasted_iota(jnp.int32, x.shape, 1)
    partner = jnp.where((lane % 32) < 16, pltpu.roll(x, LANE - 16, 1), pltpu.roll(x, 16, 1))
    return x * cos + partner * sin


def _kv_heads(kv, kr, khw, cos, sin, use_rope):
    krw = kr * khw[:, LANE:]
    if use_rope:
        krw = _rope_tile(krw, cos, sin)
    ssq_r = jnp.sum(kr * kr, axis=-1, keepdims=True)
    ks, vs = [], []
    for hd in range(MLA_HEADS):
        kn = kv[:, hd * HEAD_PAD: hd * HEAD_PAD + LANE]
        r = lax.rsqrt((jnp.sum(kn * kn, axis=-1, keepdims=True) + ssq_r) / MLA_QK + EPS)
        ks += [kn * r * khw[:, :LANE], krw * r]
        vs.append(kv[:, hd * HEAD_PAD + LANE: (hd + 1) * HEAD_PAD])
    return jnp.concatenate(ks, axis=-1).astype(BF16), jnp.concatenate(vs, axis=-1).astype(BF16)


def _inproj_kernel(x_ref, mod_ref, n1w_ref, win_ref, qnw_ref, kvnw_ref, wuq_ref, wukv_ref, qhw_ref, khw_ref,
                   cos_ref, sin_ref, *rest, use_rope):
    ckv_ref, krope_ref, q_ref, k_ref, v_ref, zr_ref = rest[-6:]
    m = mod_ref[0]
    h = _rms(x_ref[...], n1w_ref[...]) * (1.0 + m[1:2]) + m[0:1]
    z = _mm(h, win_ref[...])
    kr = z[:, Q_RANK + KV_RANK: Q_RANK + KV_RANK + LANE]
    zr_ref[...] = z[:, Q_RANK + KV_RANK + LANE:]
    ckvn = _rms(z[:, Q_RANK: Q_RANK + KV_RANK], kvnw_ref[...])
    ckv_ref[...] = ckvn
    krope_ref[...] = kr[:, :MLA_ROPE]
    cos, sin = cos_ref[...], sin_ref[...]

    q = _mm(_rms(z[:, :Q_RANK], qnw_ref[...]), wuq_ref[...])
    qhw = qhw_ref[...]
    qs = []
    for hd in range(MLA_HEADS):
        qn = q[:, hd * HEAD_PAD: hd * HEAD_PAD + LANE]
        qr = q[:, hd * HEAD_PAD + LANE: (hd + 1) * HEAD_PAD]
        ssq = jnp.sum(qn * qn, axis=-1, keepdims=True) + jnp.sum(qr * qr, axis=-1, keepdims=True)
        r = lax.rsqrt(ssq / MLA_QK + EPS)
        qrw = qr * qhw[:, LANE:]
        if use_rope:
            qrw = _rope_tile(qrw, cos, sin)
        qs += [qn * r * qhw[:, :LANE], qrw * r]
    q_ref[...] = jnp.concatenate(qs, axis=-1).astype(BF16)

    k, v = _kv_heads(_mm(ckvn, wukv_ref[...]), kr, khw_ref[...], cos, sin, use_rope)
    k_ref[...] = k
    v_ref[...] = v


def _in_proj(x, mod, w, cos, sin, seq, use_rope, after=None):
    t = x.shape[0]
    tm = ROW_TILE
    per_batch = seq // tm
    const = lambda i: (0, 0)
    row = lambda i: (i, 0)
    if use_rope:
        pos = lambda i: (i % per_batch, 0)
    else:
        pos = const
    hp = MLA_HEADS * HEAD_PAD
    ordering = [] if after is None else [after]
    return pl.pallas_call(
        functools.partial(_inproj_kernel, use_rope=use_rope),
        grid=(t // tm,),
        in_specs=[pl.BlockSpec((tm, D_MODEL), row),
                  pl.BlockSpec((1, 6, D_MODEL), lambda i: (i // per_batch, 0, 0)),
                  pl.BlockSpec((1, D_MODEL), const),
                  pl.BlockSpec((D_MODEL, IN_PAD), const),
                  pl.BlockSpec((1, Q_RANK), const),
                  pl.BlockSpec((1, KV_RANK), const),
                  pl.BlockSpec((Q_RANK, hp), const),
                  pl.BlockSpec((KV_RANK, hp), const),
                  pl.BlockSpec((1, HEAD_PAD), const),
                  pl.BlockSpec((1, HEAD_PAD), const),
                  pl.BlockSpec((tm, LANE), pos),
                  pl.BlockSpec((tm, LANE), pos)] + [pl.BlockSpec(memory_space=pl.ANY)] * len(ordering),
        out_specs=[pl.BlockSpec((tm, KV_RANK), row),
                   pl.BlockSpec((tm, MLA_ROPE), row),
                   pl.BlockSpec((tm, hp), row),
                   pl.BlockSpec((tm, hp), row),
                   pl.BlockSpec((tm, MLA_HEADS * MLA_V), row),
                   pl.BlockSpec((tm, RET_W), row)],
        out_shape=[jax.ShapeDtypeStruct((t, KV_RANK), F32),
                   jax.ShapeDtypeStruct((t, MLA_ROPE), F32),
                   jax.ShapeDtypeStruct((t, hp), BF16),
                   jax.ShapeDtypeStruct((t, hp), BF16),
                   jax.ShapeDtypeStruct((t, MLA_HEADS * MLA_V), BF16),
                   jax.ShapeDtypeStruct((t, RET_W), F32)],
        compiler_params=_params("arbitrary"),
        name="in_proj",
    )(x, mod, w["norm1_w"], w["w_in"], w["q_norm_w"], w["kv_norm_w"], w["w_uq"], w["w_ukv"],
      w["q_head_norm_w"], w["k_head_norm_w"], cos, sin, *ordering)


def _kvup_kernel(ckv_ref, kr_ref, wukv_ref, khw_ref, k_ref, v_ref):
    k, v = _kv_heads(_mm(ckv_ref[...], wukv_ref[...]), kr_ref[...], khw_ref[...], None, None, False)
    k_ref[...] = k
    v_ref[...] = v


def _kv_up(ckv, kr, w):
    t = ckv.shape[0]
    tm = ROW_TILE
    hp = MLA_HEADS * HEAD_PAD
    const = lambda i: (0, 0)
    row = lambda i: (i, 0)
    return pl.pallas_call(
        _kvup_kernel,
        grid=(t // tm,),
        in_specs=[pl.BlockSpec((tm, KV_RANK), row),
                  pl.BlockSpec((tm, LANE), row),
                  pl.BlockSpec((KV_RANK, hp), const),
                  pl.BlockSpec((1, HEAD_PAD), const)],
        out_specs=[pl.BlockSpec((tm, hp), row),
                   pl.BlockSpec((tm, MLA_HEADS * MLA_V), row)],
        out_shape=[jax.ShapeDtypeStruct((t, hp), BF16),
                   jax.ShapeDtypeStruct((t, MLA_HEADS * MLA_V), BF16)],
        compiler_params=_params("arbitrary"),
        name="kv_up",
    )(ckv, kr, w["w_ukv"], w["k_head_norm_w"])


def _attn_kernel(*refs, has_ctx):
    if has_ctx:
        q_ref, k_ref, v_ref, kc_ref, vc_ref, o_ref = refs
    else:
        q_ref, k_ref, v_ref, o_ref = refs
    scale = MLA_QK ** -0.5
    q = q_ref[...]
    s = _mm_nt(q, k_ref[...]) * scale
    m = jnp.max(s, axis=-1, keepdims=True)
    if has_ctx:
        sc = _mm_nt(q, kc_ref[...]) * scale
        m = jnp.maximum(m, jnp.max(sc, axis=-1, keepdims=True))
    p = jnp.exp(s - m)
    den = jnp.sum(p, axis=-1, keepdims=True)
    o = _mm(p, v_ref[...])
    if has_ctx:
        pc = jnp.exp(sc - m)
        den = den + jnp.sum(pc, axis=-1, keepdims=True)
        o = o + _mm(pc, vc_ref[...])
    o_ref[...] = (o / den).astype(BF16)


def _attention(q, k, v, batch, seq, ctx_kv):
    tq = min(Q_TILE, seq)
    nq = seq // tq
    has_ctx = ctx_kv is not None
    qmap = lambda b, h, i: (b * nq + i, h)
    kvmap = lambda b, h, i: (b, h)
    in_specs = [pl.BlockSpec((tq, HEAD_PAD), qmap),
                pl.BlockSpec((seq, HEAD_PAD), kvmap),
                pl.BlockSpec((seq, MLA_V), kvmap)]
    args = [q, k, v]
    if has_ctx:
        kc, vc = ctx_kv
        past = kc.shape[0] // batch
        in_specs += [pl.BlockSpec((past, HEAD_PAD), kvmap), pl.BlockSpec((past, MLA_V), kvmap)]
        args += [kc, vc]
    return pl.pallas_call(
        functools.partial(_attn_kernel, has_ctx=has_ctx),
        grid=(batch, MLA_HEADS, nq),
        in_specs=in_specs,
        out_specs=pl.BlockSpec((tq, MLA_V), qmap),
        out_shape=jax.ShapeDtypeStruct((batch * seq, MLA_HEADS * MLA_V), BF16),
        compiler_params=_params("arbitrary", "arbitrary", "arbitrary"),
        name="attention",
    )(*args)


def _ret_kernel(*refs, has_s0):
    if has_s0:
        (lg_ref, qf_ref, kf_ref, vf_ref, qb_ref, kb_ref, vb_ref, s0f_ref, s0b_ref,
         of_ref, ob_ref, sf_out, sb_out, sf_scr, sb_scr) = refs
    else:
        (lg_ref, qf_ref, kf_ref, vf_ref, qb_ref, kb_ref, vb_ref,
         of_ref, ob_ref, sf_out, sb_out, sf_scr, sb_scr) = refs
    c = pl.program_id(1)
    cs = RET_CHUNK

    @pl.when(c == 0)
    def _():
        if has_s0:
            sf_scr[...] = s0f_ref[0]
            sb_scr[...] = s0b_ref[0]
        else:
            sf_scr[...] = jnp.zeros_like(sf_scr)
            sb_scr[...] = jnp.zeros_like(sb_scr)

    ii = lax.broadcasted_iota(jnp.int32, (cs, cs), 0).astype(F32)
    jj = lax.broadcasted_iota(jnp.int32, (cs, cs), 1).astype(F32)
    rel = ii - jj
    kscale = RET_DK ** -0.5

    def chunk(q, k, v, s, intra, qdec, kdec, cdec):
        a = _mm_nt(q, k) * intra
        o = _mm(a, v) + _mm(q, s) * qdec
        return o, s * cdec + _mm_tn(k * kdec, v)

    for hd in range(RET_HEADS):
        cols = slice(hd * RET_DK, (hd + 1) * RET_DK)
        lgf = jax.nn.log_sigmoid(jnp.full((1, LANE), lg_ref[0, hd], F32))
        lgb = jax.nn.log_sigmoid(jnp.full((1, LANE), lg_ref[1, hd], F32))

        intra_f = jnp.where(rel >= 0, jnp.exp(jnp.maximum(rel, 0.0) * lgf), 0.0)
        o_f, s_f = chunk(qf_ref[:, cols], kf_ref[:, cols] * kscale, vf_ref[:, cols], sf_scr[hd], intra_f,
                         jnp.exp((ii + 1.0) * lgf), jnp.exp((cs - 1.0 - ii) * lgf), jnp.exp(cs * lgf))
        of_ref[:, cols] = o_f
        sf_scr[hd] = s_f
        sf_out[0, hd] = s_f

        intra_b = jnp.where(rel <= 0, jnp.exp(jnp.maximum(-rel, 0.0) * lgb), 0.0)
        o_b, s_b = chunk(qb_ref[:, cols], kb_ref[:, cols] * kscale, vb_ref[:, cols], sb_scr[hd], intra_b,
                         jnp.exp((cs - ii) * lgb), jnp.exp(ii * lgb), jnp.exp(cs * lgb))
        ob_ref[:, cols] = o_b
        sb_scr[hd] = s_b
        sb_out[0, hd] = s_b


def _retention(zr, decay_logit, batch, seq, s0):
    cs = RET_CHUNK
    nc = seq // cs
    nh = RET_HEADS
    width = nh * RET_DK
    has_s0 = s0 is not None

    def fwd(col):
        return pl.BlockSpec((cs, width), lambda b, c: (b * nc + c, col))

    def bwd(col):
        return pl.BlockSpec((cs, width), lambda b, c: (b * nc + nc - 1 - c, col))

    def state(d):
        return pl.BlockSpec((1, nh, RET_DK, RET_DV), lambda b, c: (b * 2 + d, 0, 0, 0))

    state_out = pl.BlockSpec((1, nh, RET_DK, RET_DV), lambda b, c: (b, 0, 0, 0))
    in_specs = [pl.BlockSpec(memory_space=pltpu.SMEM), fwd(0), fwd(1), fwd(2), bwd(0), bwd(1), bwd(2)]
    args = [decay_logit, zr, zr, zr, zr, zr, zr]
    if has_s0:
        in_specs += [state(0), state(1)]
        args += [s0, s0]
    t = batch * seq
    o_f, o_b, s_f, s_b = pl.pallas_call(
        functools.partial(_ret_kernel, has_s0=has_s0),
        grid=(batch, nc),
        in_specs=in_specs,
        out_specs=[pl.BlockSpec((cs, width), lambda b, c: (b * nc + c, 0)),
                   pl.BlockSpec((cs, width), lambda b, c: (b * nc + nc - 1 - c, 0)),
                   state_out, state_out],
        out_shape=[jax.ShapeDtypeStruct((t, width), F32),
                   jax.ShapeDtypeStruct((t, width), F32),
                   jax.ShapeDtypeStruct((batch, nh, RET_DK, RET_DV), F32),
                   jax.ShapeDtypeStruct((batch, nh, RET_DK, RET_DV), F32)],
        scratch_shapes=[pltpu.VMEM((nh, RET_DK, RET_DV), F32), pltpu.VMEM((nh, RET_DK, RET_DV), F32)],
        compiler_params=_params("arbitrary", "arbitrary"),
        name="retention",
    )(*args)
    return o_f, o_b, s_f, s_b


def _mix_kernel(x_ref, att_ref, of_ref, ob_ref, gr_ref, mod_ref, gnw_ref, wout_ref, n2w_ref, x1_ref, h2_ref):
    o = of_ref[...] + ob_ref[...]
    parts = []
    for hd in range(RET_HEADS):
        oh = o[:, hd * RET_DV:(hd + 1) * RET_DV]
        d = oh - jnp.mean(oh, axis=-1, keepdims=True)
        parts.append(d * lax.rsqrt(jnp.mean(d * d, axis=-1, keepdims=True) + EPS))
    g = gr_ref[...]
    ret = (g * jax.nn.sigmoid(g)) * (jnp.concatenate(parts, axis=-1) * gnw_ref[...])
    na = MLA_HEADS * MLA_V
    mixed = _mm(att_ref[...], wout_ref[:na, :]) + _mm(ret, wout_ref[na:, :])
    m = mod_ref[0]
    x1 = x_ref[...] + m[2:3] * mixed
    x1_ref[...] = x1
    h2_ref[...] = _rms(x1, n2w_ref[...]) * (1.0 + m[4:5]) + m[3:4]


def _mix(x, att, o_f, o_b, zr, mod, w, seq):
    t = x.shape[0]
    tm = ROW_TILE
    per_batch = seq // tm
    const = lambda i: (0, 0)
    row = lambda i: (i, 0)
    half = RET_HEADS * RET_DV
    return pl.pallas_call(
        _mix_kernel,
        grid=(t // tm,),
        in_specs=[pl.BlockSpec((tm, D_MODEL), row),
                  pl.BlockSpec((tm, half), row),
                  pl.BlockSpec((tm, half), row),
                  pl.BlockSpec((tm, half), row),
                  pl.BlockSpec((tm, half), lambda i: (i, 3)),
                  pl.BlockSpec((1, 6, D_MODEL), lambda i: (i // per_batch, 0, 0)),
                  pl.BlockSpec((1, half), const),
                  pl.BlockSpec((D_MODEL, D_MODEL), const),
                  pl.BlockSpec((1, D_MODEL), const)],
        out_specs=[pl.BlockSpec((tm, D_MODEL), row), pl.BlockSpec((tm, D_MODEL), row)],
        out_shape=[jax.ShapeDtypeStruct((t, D_MODEL), F32), jax.ShapeDtypeStruct((t, D_MODEL), F32)],
        compiler_params=_params("arbitrary"),
        name="mix",
    )(x, att, o_f, o_b, zr, mod, w["ret_gn_w"], w["w_out"], w["norm2_w"])


def _top16(jobs):
    def body(r, carry):
        for s_ref, vals_ref, pay_ref, payload in jobs:
            n, cols = s_ref.shape
            rows = lax.broadcasted_iota(jnp.int32, (n, cols), 0).astype(F32)
            s = s_ref[...]
            m = jnp.max(s, axis=0, keepdims=True)
            pos = jnp.min(jnp.where(s == m, rows, float(n)), axis=0, keepdims=True)
            hit = rows == pos
            vals_ref[pl.ds(r, 1), :] = m
            if payload is None:
                pay_ref[pl.ds(r, 1), :] = pos
            else:
                pay_ref[pl.ds(r, 1), :] = jnp.sum(jnp.where(hit, payload, 0.0), axis=0, keepdims=True)
            s_ref[...] = jnp.where(hit, -jnp.inf, s)
        return carry

    lax.fori_loop(0, PEER_TOPK, body, 0)


PAIR_COUNTS = tuple(PEER_TOPK // (a + 1) for a in range(PEER_TOPK))
NUM_PAIRS = sum(PAIR_COUNTS)
PAIR_ROWS = -(-NUM_PAIRS // 8) * 8


HEAD_GROUP = 2
SCORE_REFS = 10


def _score_heads(hbt, wpqts, k1, k2, g_outs, e_outs, scr):
    heads = [scr[h * SCORE_REFS:(h + 1) * SCORE_REFS] for h in range(len(wpqts))]
    for wpqt, (s1_scr, s2_scr, *_) in zip(wpqts, heads):
        qt = jnp.dot(wpqt, hbt, preferred_element_type=F32)
        s1_scr[...] = _mm(k1, qt[:PEER_HALF, :])
        s2_scr[...] = _mm(k2, qt[PEER_HALF:, :])
    _top16([job for (s1, s2, _, _, v1, i1, v2, i2, _, _) in heads for job in ((s1, v1, i1, None), (s2, v2, i2, None))])
    for (_, _, c_scr, p_scr, v1_scr, i1_scr, v2_scr, i2_scr, _, _) in heads:
        c_scr[...] = jnp.full(c_scr.shape, -jnp.inf, F32)
        p_scr[...] = jnp.zeros(p_scr.shape, F32)
        off = 0
        for a, nb in enumerate(PAIR_COUNTS):
            c_scr[off:off + nb, :] = v1_scr[a:a + 1, :] + v2_scr[0:nb, :]
            p_scr[off:off + nb, :] = i1_scr[a:a + 1, :] * float(N_KEYS) + i2_scr[0:nb, :]
            off += nb
    _top16([(c_scr, vt_scr, it_scr, p_scr[...]) for (_, _, c_scr, p_scr, _, _, _, _, vt_scr, it_scr) in heads])
    for (*_, vt_scr, it_scr), g_out, e_out in zip(heads, g_outs, e_outs):
        top = vt_scr[...]
        p = jnp.exp(top - jnp.max(top, axis=0, keepdims=True))
        g_out[...] = p / jnp.sum(p, axis=0, keepdims=True)
        e_out[...] = it_scr[...]


def _score_scratch(tb):
    k = PEER_TOPK
    one = ([pltpu.VMEM((N_KEYS, tb), F32)] * 2 + [pltpu.VMEM((PAIR_ROWS, tb), F32)] * 2
           + [pltpu.VMEM((k, tb), F32)] * 6)
    assert len(one) == SCORE_REFS
    return one * HEAD_GROUP


def _score_group(hbt, wpq_ref, k1_ref, k2_ref, g_ref, e_ref, grp, scr):
    wpqts, g_outs, e_outs = [], [], []
    for h in range(HEAD_GROUP):
        hd = grp * HEAD_GROUP + h
        head_rows = pl.ds(pl.multiple_of(hd * PEER_TOPK, PEER_TOPK), PEER_TOPK)
        wpqts.append(wpq_ref[hd])
        g_outs.append(g_ref.at[head_rows])
        e_outs.append(e_ref.at[head_rows])
    _score_heads(hbt, wpqts, k1_ref[...], k2_ref[...], g_outs, e_outs, scr)


def _peer_kernel(tab_hbm, h2_ref, h2n_ref, wpq_ref, k1_ref, k2_ref, x1_ref, mod_ref, *rest, ntiles, aliased):
    y_ref, idx_smem, g_scr, e_scr, ei_scr, hb_scr = rest[int(aliased):int(aliased) + 6]
    scratch = rest[int(aliased) + 6:]
    _peer_body(tab_hbm, h2_ref, h2n_ref, wpq_ref, k1_ref, k2_ref, x1_ref, mod_ref, y_ref,
               idx_smem, g_scr, e_scr, ei_scr, hb_scr, *scratch, ntiles=ntiles)


def _peer_body(tab_hbm, h2_ref, h2n_ref, wpq_ref, k1_ref, k2_ref, x1_ref, mod_ref, y_ref,
               idx_smem, g_scr, e_scr, ei_scr, hb_scr, *scratch, ntiles):
    bufs = scratch[:GATHER_SLOTS]
    sem_rows, sem_idx = scratch[GATHER_SLOTS:GATHER_SLOTS + 2]
    score_scr = scratch[GATHER_SLOTS + 2:]
    i = pl.program_id(0)
    tb = GATHER_TILE
    ns = GATHER_SLOTS
    ahead = GATHER_AHEAD
    ne = EXPERTS_PER_TOKEN
    slab = SLAB_ROWS
    nphase = PEER_HEADS // HEAD_GROUP
    groups_per_phase = tb // (ns * nphase)
    cur = i % 2
    nxt = 1 - cur
    more = i + 1 < ntiles

    def score(src_ref, grp, slot):
        @pl.when(grp == 0)
        def _():
            hb_scr[...] = src_ref[...].T.astype(BF16)
        _score_group(hb_scr[...], wpq_ref, k1_ref, k2_ref, g_scr.at[slot], e_scr, grp, score_scr)

    def publish(slot):
        ei_scr[...] = e_scr[...].T.astype(jnp.int32)
        copy = pltpu.make_async_copy(ei_scr, idx_smem.at[slot], sem_idx.at[0])
        copy.start()
        copy.wait()

    def issue(islot, tok, rslot):
        for kk in range(ne):
            ex = idx_smem[islot, tok, kk]
            pltpu.make_async_copy(tab_hbm.at[ex], bufs[rslot].at[:, pl.ds(kk * slab, slab), :],
                                  sem_rows.at[rslot]).start(priority=kk % 2)

    def wait_rows(rslot):
        pltpu.make_async_copy(bufs[rslot], bufs[rslot], sem_rows.at[rslot]).wait()

    @pl.when(i == 0)
    def _():
        def first(grp, carry):
            score(h2_ref, grp, 0)
            return carry
        lax.fori_loop(0, nphase, first, 0)
        publish(0)
        for tok in range(ahead):
            issue(0, tok, tok)

    g2 = mod_ref[0][5:6]
    lane = lax.broadcasted_iota(jnp.int32, (ne, tb), 1)

    def compute(j, rslot):
        rows = bufs[rslot]
        x_row = h2_ref[pl.ds(j, 1), :]
        acc = None
        for c in range(slab):
            u_c = rows[0, pl.ds(c, ne, stride=slab), :]
            term = u_c * x_row[:, c * LANE:(c + 1) * LANE]
            acc = term if acc is None else acc + term
        pre = jnp.sum(acc, axis=-1, keepdims=True)
        act = 0.5 * pre * (1.0 + lax.erf(pre * (2.0 ** -0.5)))
        gate = jnp.sum(jnp.where(lane == j, g_scr[cur], 0.0), axis=-1, keepdims=True)
        wgt = gate * act
        outs = []
        for c in range(slab):
            v_c = rows[1, pl.ds(c, ne, stride=slab), :]
            outs.append(jnp.sum(v_c * wgt, axis=0, keepdims=True))
        out = jnp.concatenate(outs, axis=-1)
        y_ref[pl.ds(j, 1), :] = x1_ref[pl.ds(j, 1), :] + g2 * out

    def phase(grp, carry):
        score(h2n_ref, grp, nxt)

        @pl.when(grp == nphase - 1)
        def _():
            publish(nxt)

        def group(gq, inner):
            g = grp * groups_per_phase + gq
            for u in range(ns):
                j = g * ns + u
                wait_rows(u)
                jj = j + ahead
                over = jj >= tb
                issue(jnp.where(over, nxt, cur), jnp.where(over, jj - tb, jj), (u + ahead) % ns)
                compute(j, u)
            return inner

        lax.fori_loop(0, groups_per_phase, group, 0)
        return carry

    lax.fori_loop(0, nphase, phase, 0)

    @pl.when(jnp.logical_not(more))
    def _():
        for u in range(ahead):
            wait_rows(u)


def _peer(tab, h2, x1, mod, w, seq, tile0, ntiles, y_prev=None):
    t = h2.shape[0]
    tb = GATHER_TILE
    per_batch = seq // tb
    row = lambda i: (tile0 + i, 0)
    const = lambda i: (0, 0)
    in_specs = [pl.BlockSpec(memory_space=pl.ANY),
                pl.BlockSpec((tb, D_MODEL), row),
                pl.BlockSpec((tb, D_MODEL), lambda i: (tile0 + jnp.minimum(i + 1, ntiles - 1), 0)),
                pl.BlockSpec((PEER_HEADS, PEER_QDIM, D_MODEL), lambda i: (0, 0, 0)),
                pl.BlockSpec((N_KEYS, PEER_HALF), const),
                pl.BlockSpec((N_KEYS, PEER_HALF), const),
                pl.BlockSpec((tb, D_MODEL), row),
                pl.BlockSpec((1, 6, D_MODEL), lambda i: ((tile0 + i) // per_batch, 0, 0))]
    args = [tab, h2, h2, w["w_pq"], w["sub_keys1"], w["sub_keys2"], x1, mod]
    aliases = {}
    if y_prev is not None:
        in_specs.append(pl.BlockSpec(memory_space=pl.ANY))
        args.append(y_prev)
        aliases = {len(args) - 1: 0}
    return pl.pallas_call(
        functools.partial(_peer_kernel, ntiles=ntiles, aliased=y_prev is not None),
        grid=(ntiles,),
        in_specs=in_specs,
        input_output_aliases=aliases,
        out_specs=pl.BlockSpec((tb, D_MODEL), row),
        out_shape=jax.ShapeDtypeStruct((t, D_MODEL), F32),
        scratch_shapes=[pltpu.SMEM((2, tb, EXPERTS_PER_TOKEN), jnp.int32),
                        pltpu.VMEM((2, EXPERTS_PER_TOKEN, tb), F32),
                        pltpu.VMEM((EXPERTS_PER_TOKEN, tb), F32),
                        pltpu.VMEM((tb, EXPERTS_PER_TOKEN), jnp.int32),
                        pltpu.VMEM((D_MODEL, tb), BF16)]
                       + [pltpu.VMEM((2, EXPERTS_PER_TOKEN * SLAB_ROWS, LANE), F32)] * GATHER_SLOTS
                       + [pltpu.SemaphoreType.DMA((GATHER_SLOTS,)), pltpu.SemaphoreType.DMA((1,))]
                       + _score_scratch(tb),
        compiler_params=pltpu.CompilerParams(dimension_semantics=("arbitrary",),
                                             vmem_limit_bytes=VMEM_LIMIT,
                                             disable_bounds_checks=True),
        name="peer",
    )(*args)


def _score_kernel(h2_ref, wpq_ref, k1_ref, k2_ref, *rest, ordered):
    g_ref, e_ref, hb_scr = rest[int(ordered):int(ordered) + 3]
    score_scr = rest[int(ordered) + 3:]
    hb_scr[...] = h2_ref[...].T.astype(BF16)

    def group(grp, carry):
        _score_group(hb_scr[...], wpq_ref, k1_ref, k2_ref, g_ref, e_ref, grp, score_scr)
        return carry

    lax.fori_loop(0, PEER_HEADS // HEAD_GROUP, group, 0)


def _peer_score(h2, w, tile0, ntiles, after=None):
    tb = GATHER_TILE
    const = lambda i: (0, 0)
    out = pl.BlockSpec((EXPERTS_PER_TOKEN, tb), lambda i: (0, i))
    ordering = [] if after is None else [after]
    return pl.pallas_call(
        functools.partial(_score_kernel, ordered=after is not None),
        grid=(ntiles,),
        in_specs=[pl.BlockSpec((tb, D_MODEL), lambda i: (tile0 + i, 0)),
                  pl.BlockSpec((PEER_HEADS, PEER_QDIM, D_MODEL), lambda i: (0, 0, 0)),
                  pl.BlockSpec((N_KEYS, PEER_HALF), const),
                  pl.BlockSpec((N_KEYS, PEER_HALF), const)] + [pl.BlockSpec(memory_space=pl.ANY)] * len(ordering),
        out_specs=[out, out],
        out_shape=[jax.ShapeDtypeStruct((EXPERTS_PER_TOKEN, ntiles * tb), F32)] * 2,
        scratch_shapes=[pltpu.VMEM((D_MODEL, tb), BF16)] + _score_scratch(tb),
        compiler_params=_params("arbitrary"),
        name="peer_score",
    )(h2, w["w_pq"], w["sub_keys1"], w["sub_keys2"], *ordering)


def _act_kernel(pre_ref, g_ref, after_ref, w_ref):
    del after_ref
    pre = pre_ref[...]
    w_ref[...] = g_ref[...] * (0.5 * pre * (1.0 + lax.erf(pre * (2.0 ** -0.5))))


def _peer_act(pre, gate, after):
    t, ne = pre.shape
    tm = math.gcd(t, 1024)
    row = lambda i: (i, 0)
    return pl.pallas_call(
        _act_kernel,
        grid=(t // tm,),
        in_specs=[pl.BlockSpec((tm, ne), row), pl.BlockSpec((tm, ne), row), pl.BlockSpec(memory_space=pl.ANY)],
        out_specs=pl.BlockSpec((tm, ne), row),
        out_shape=jax.ShapeDtypeStruct((t, ne), F32),
        compiler_params=_params("arbitrary"),
        name="peer_act",
    )(pre, gate, after)


def _residual_kernel(x1_ref, out_ref, mod_ref, *rest):
    rest[-1][...] = x1_ref[...] + mod_ref[0][5:6] * out_ref[...]


def _peer_residual(x1, out, mod, seq, tile0, ntiles, out_tile0, y_prev):
    tb = GATHER_TILE
    per_batch = seq // tb
    row = lambda i: (tile0 + i, 0)
    in_specs = [pl.BlockSpec((tb, D_MODEL), row),
                pl.BlockSpec((tb, D_MODEL), lambda i: (out_tile0 + i, 0)),
                pl.BlockSpec((1, 6, D_MODEL), lambda i: ((tile0 + i) // per_batch, 0, 0))]
    args = [x1, out, mod]
    aliases = {}
    if y_prev is not None:
        in_specs.append(pl.BlockSpec(memory_space=pl.ANY))
        args.append(y_prev)
        aliases = {3: 0}
    return pl.pallas_call(
        _residual_kernel,
        grid=(ntiles,),
        in_specs=in_specs,
        input_output_aliases=aliases,
        out_specs=pl.BlockSpec((tb, D_MODEL), row),
        out_shape=jax.ShapeDtypeStruct(x1.shape, F32),
        compiler_params=_params("arbitrary"),
        name="peer_residual",
    )(*args)


SC_WORKERS = 32
SC_LANES = 16
SC_CHUNK = 16
SC_SHARE = (29, 64)
TC_FIRST = (31, 70)


def _sc_udot(u_table, idx, x, after=None):
    t, ne = idx.shape
    assert t % SC_WORKERS == 0 and ne % SC_CHUNK == 0, (t, ne)
    per_w = t // SC_WORKERS
    nchunk = ne // SC_CHUNK
    nl = D_MODEL // SC_LANES
    mesh = plsc.VectorSubcoreMesh(core_axis_name="c", subcore_axis_name="s")

    @functools.partial(
        pl.kernel, out_type=jax.ShapeDtypeStruct((t, ne), F32), mesh=mesh,
        scratch_types=[pltpu.VMEM((ne,), jnp.int32), pltpu.VMEM((D_MODEL,), F32),
                       pltpu.VMEM((2, SC_CHUNK, D_MODEL), F32), pltpu.VMEM((SC_CHUNK, SC_LANES), F32),
                       pltpu.VMEM((ne,), F32), pltpu.SemaphoreType.DMA((2,))],
        compiler_params=pltpu.CompilerParams(needs_layout_passes=False),
        name="sc_udot")
    def run(u_hbm, idx_hbm, x_hbm, *rest):
        pre_hbm, idx_v, x_v, rows, accs, pre_v, sems = rest[-7:]
        wid = lax.axis_index("s") * 2 + lax.axis_index("c")
        lanes = lax.iota(jnp.int32, SC_LANES)

        def gather(c, b):
            return pltpu.make_async_copy(u_hbm.at[idx_v.at[pl.ds(c * SC_CHUNK, SC_CHUNK)]], rows.at[b], sems.at[b])

        @pl.loop(0, per_w)
        def _(tt):
            tok = wid * per_w + tt
            pltpu.sync_copy(idx_hbm.at[tok], idx_v)
            pltpu.sync_copy(x_hbm.at[tok], x_v)
            gather(0, 0).start()
            for c in range(nchunk):
                b = c % 2
                if c + 1 < nchunk:
                    gather(c + 1, 1 - b).start()
                gather(c, b).wait()
                for g in range(SC_CHUNK // 4):
                    def body(j, acc):
                        xv = x_v[pl.ds(j * SC_LANES, SC_LANES)]
                        return tuple(acc[q] + rows[b, g * 4 + q, pl.ds(j * SC_LANES, SC_LANES)] * xv
                                     for q in range(4))
                    acc = plsc.parallel_loop(0, nl, unroll=4,
                                             carry=tuple(jnp.zeros((SC_LANES,), F32) for _ in range(4)))(body)
                    for q in range(4):
                        accs[g * 4 + q, :] = acc[q]
                tot = jnp.zeros((SC_LANES,), F32)
                for l in range(SC_LANES):
                    tot = tot + plsc.load_gather(accs, [lanes, jnp.full((SC_LANES,), l, jnp.int32)])
                pre_v[pl.ds(c * SC_CHUNK, SC_CHUNK)] = tot
            pltpu.sync_copy(pre_v, pre_hbm.at[tok])

    return run(u_table, idx, x, *([] if after is None else [after]))


def _sc_vsum(v_table, idx, wgt):
    t, ne = idx.shape
    assert t % SC_WORKERS == 0 and ne % SC_CHUNK == 0, (t, ne)
    per_w = t // SC_WORKERS
    nchunk = ne // SC_CHUNK
    block = 16 * SC_LANES
    mesh = plsc.VectorSubcoreMesh(core_axis_name="c", subcore_axis_name="s")

    @functools.partial(
        pl.kernel, out_type=jax.ShapeDtypeStruct((t, D_MODEL), F32), mesh=mesh,
        scratch_types=[pltpu.VMEM((ne,), jnp.int32), pltpu.VMEM((ne,), F32),
                       pltpu.VMEM((2, SC_CHUNK, D_MODEL), F32), pltpu.VMEM((D_MODEL,), F32),
                       pltpu.SemaphoreType.DMA((2,))],
        compiler_params=pltpu.CompilerParams(needs_layout_passes=False),
        name="sc_vsum")
    def run(v_hbm, idx_hbm, w_hbm, out_hbm, idx_v, w_v, rows, out_v, sems):
        wid = lax.axis_index("s") * 2 + lax.axis_index("c")

        def gather(c, b):
            return pltpu.make_async_copy(v_hbm.at[idx_v.at[pl.ds(c * SC_CHUNK, SC_CHUNK)]], rows.at[b], sems.at[b])

        @pl.loop(0, per_w)
        def _(tt):
            tok = wid * per_w + tt
            pltpu.sync_copy(idx_hbm.at[tok], idx_v)
            pltpu.sync_copy(w_hbm.at[tok], w_v)
            gather(0, 0).start()
            for c in range(nchunk):
                b = c % 2
                if c + 1 < nchunk:
                    gather(c + 1, 1 - b).start()
                gather(c, b).wait()
                for d in range(D_MODEL // block):
                    def body(k, acc):
                        wk = plsc.load_gather(w_v, [jnp.full((SC_LANES,), c * SC_CHUNK, jnp.int32) + k])
                        return tuple(acc[j] + rows[b, k, pl.ds(d * block + j * SC_LANES, SC_LANES)] * wk
                                     for j in range(16))
                    if c == 0:
                        init = tuple(jnp.zeros((SC_LANES,), F32) for _ in range(16))
                    else:
                        init = tuple(out_v[pl.ds(d * block + j * SC_LANES, SC_LANES)] for j in range(16))
                    acc = plsc.parallel_loop(0, SC_CHUNK, carry=init)(body)
                    for j in range(16):
                        out_v[pl.ds(d * block + j * SC_LANES, SC_LANES)] = acc[j]
            pltpu.sync_copy(out_v, out_hbm.at[tok])

    return run(v_table, idx, wgt)


def _expert_slabs(u_table, v_table):
    n = u_table.shape[0]
    u = u_table.reshape(n, 1, SLAB_ROWS, LANE)
    v = v_table.reshape(n, 1, SLAB_ROWS, LANE)
    return jnp.concatenate([u, v], axis=1)


def _rope_tables(seq):
    rows = seq // GRID_W
    r = jnp.repeat(jnp.arange(rows, dtype=F32), GRID_W)
    col = jnp.tile(jnp.arange(GRID_W, dtype=F32), rows)
    nf = ROPE_AXIS // 2
    freqs = jnp.power(ROPE_BASE, -jnp.arange(nf, dtype=F32) / nf)
    ar, ac = r[:, None] * freqs, col[:, None] * freqs
    pad = jnp.zeros((seq, LANE - MLA_ROPE), F32)
    cos = jnp.concatenate([jnp.cos(ar), jnp.cos(ar), jnp.cos(ac), jnp.cos(ac), pad], axis=-1)
    sin = jnp.concatenate([-jnp.sin(ar), jnp.sin(ar), -jnp.sin(ac), jnp.sin(ac), pad], axis=-1)
    return cos, sin


def _prep_weights(norm1_w, w_in, q_norm_w, w_uq, kv_norm_w, w_ukv, q_head_norm_w, k_head_norm_w,
                  ret_gn_w, w_out, norm2_w, w_pq, sub_keys1, sub_keys2):
    cut = Q_RANK + KV_RANK + MLA_ROPE
    w_in_p = jnp.concatenate([w_in[:, :cut], jnp.zeros((D_MODEL, LANE - MLA_ROPE), F32), w_in[:, cut:]], axis=1)
    wq = w_uq.reshape(Q_RANK, MLA_HEADS, MLA_QK)
    wq = jnp.pad(wq, ((0, 0), (0, 0), (0, HEAD_PAD - MLA_QK))).reshape(Q_RANK, MLA_HEADS * HEAD_PAD)

    def head_w(v):
        return jnp.pad(v, (0, HEAD_PAD - MLA_QK)).reshape(1, HEAD_PAD)

    return {
        "norm1_w": norm1_w.reshape(1, -1), "w_in": w_in_p.astype(BF16),
        "q_norm_w": q_norm_w.reshape(1, -1), "kv_norm_w": kv_norm_w.reshape(1, -1),
        "w_uq": wq.astype(BF16), "w_ukv": w_ukv.astype(BF16),
        "q_head_norm_w": head_w(q_head_norm_w), "k_head_norm_w": head_w(k_head_norm_w),
        "ret_gn_w": ret_gn_w.reshape(1, -1), "w_out": w_out.astype(BF16), "norm2_w": norm2_w.reshape(1, -1),
        "w_pq": w_pq.astype(BF16).reshape(D_MODEL, PEER_HEADS, PEER_QDIM).transpose(1, 2, 0), "sub_keys1": sub_keys1.astype(BF16), "sub_keys2": sub_keys2.astype(BF16),
    }


def _trunk(x, mod, w, decay_logit, ctx, after=None):
    batch, seq, _ = x.shape
    x2 = x.reshape(batch * seq, D_MODEL)
    use_rope = ctx is not None
    if use_rope:
        cos, sin = _rope_tables(seq)
    else:
        cos = sin = jnp.zeros((ROW_TILE, LANE), F32)
    ckv, krope, q, k, v, zr = _in_proj(x2, mod, w, cos, sin, seq, use_rope, after)
    if use_rope:
        ckv_c, krope_c, state_c = ctx
        past = ckv_c.shape[1]
        kr_c = jnp.pad(krope_c.reshape(batch * past, MLA_ROPE), ((0, 0), (0, LANE - MLA_ROPE)))
        ctx_kv = _kv_up(ckv_c.reshape(batch * past, KV_RANK), kr_c, w)
        s0 = state_c.reshape(batch * 2, RET_HEADS, RET_DK, RET_DV)
    else:
        ctx_kv, s0 = None, None
    att = _attention(q, k, v, batch, seq, ctx_kv)
    o_f, o_b, s_f, s_b = _retention(zr, decay_logit, batch, seq, s0)
    x1, h2 = _mix(x2, att, o_f, o_b, zr, mod, w, seq)
    states = jnp.stack([s_f, s_b], axis=1)
    return x1, h2, ckv.reshape(batch, seq, KV_RANK), krope.reshape(batch, seq, MLA_ROPE), states


def _sc_select(h2, w, tile0, ntiles, after=None):
    gt, et = _peer_score(h2, w, tile0, ntiles, after)
    return et.T.astype(jnp.int32), gt.T


def _peer_split(ctx_sc, x_ctx, x_lat, mod_ctx, mod_lat, seq_ctx, seq_lat, w, tab):
    (x1c, h2c), (x1l, h2l) = x_ctx, x_lat
    tiles_c = h2c.shape[0] // GATHER_TILE
    tiles_l = h2l.shape[0] // GATHER_TILE
    sc_tiles = tiles_l * SC_SHARE[0] // SC_SHARE[1]
    tc_tiles = tiles_l - sc_tiles
    first = tc_tiles * TC_FIRST[0] // TC_FIRST[1]

    idx_c, gate_c, pre_c = ctx_sc
    wgt_c = _peer_act(pre_c, gate_c, h2l)
    idx_l, gate_l = _sc_select(h2l, w, tc_tiles, sc_tiles, wgt_c)
    pre_l = _sc_udot(w["u_table"], idx_l, h2l[tc_tiles * GATHER_TILE:])
    y = _peer(tab, h2l, x1l, mod_lat, w, seq_lat, 0, first)
    idx = jnp.concatenate([idx_c, idx_l], axis=0)
    wgt = jnp.concatenate([wgt_c, _peer_act(pre_l, gate_l, y)], axis=0)
    out = _sc_vsum(w["v_table"], idx, wgt)
    y = _peer(tab, h2l, x1l, mod_lat, w, seq_lat, first, tc_tiles - first, y_prev=y)
    y_lat = _peer_residual(x1l, out, mod_lat, seq_lat, tc_tiles, sc_tiles, tiles_c, y)
    y_ctx = _peer_residual(x1c, out, mod_ctx, seq_ctx, 0, tiles_c, 0, None)
    return y_ctx, y_lat


def kernel(x_prompt, x_sample, c, cache_ckv, cache_krope, state_ret, c_ctx, w_ada, b_ada, norm1_w, w_in,
           q_norm_w, w_uq, kv_norm_w, w_ukv, q_head_norm_w, k_head_norm_w, ret_decay_logit, ret_gn_w,
           w_out, norm2_w, w_pq, sub_keys1, sub_keys2, u_table, v_table):
    depth = w_ada.shape[0]
    nb_ctx = x_prompt.shape[0]
    nb_lat = x_sample.shape[0]
    y_prompt, y_sample = x_prompt, x_sample
    ckv_list, krope_list, ret_list = [], [], []
    for l in range(depth):
        cond_rows = -(-(nb_lat + 1) // 8) * 8
        cond = jnp.concatenate([c, c_ctx[None, :], jnp.zeros((cond_rows - nb_lat - 1, D_MODEL), F32)], axis=0)
        mod = _ada(cond, w_ada[l], b_ada[l])
        mod_lat = mod[:nb_lat].reshape(nb_lat, 6, D_MODEL)
        mod_ctx = jnp.broadcast_to(mod[nb_lat].reshape(1, 6, D_MODEL), (nb_ctx, 6, D_MODEL))
        w = _prep_weights(norm1_w[l], w_in[l], q_norm_w[l], w_uq[l], kv_norm_w[l], w_ukv[l], q_head_norm_w[l],
                          k_head_norm_w[l], ret_gn_w[l], w_out[l], norm2_w[l], w_pq[l], sub_keys1[l], sub_keys2[l])
        tab = _expert_slabs(u_table[l], v_table[l])
        w["u_table"], w["v_table"] = u_table[l], v_table[l]
        x1c, h2c, ckv_l, krope_l, ret_l = _trunk(y_prompt, mod_ctx, w, ret_decay_logit[l], None)
        ckv_list.append(ckv_l)
        krope_list.append(krope_l)
        ret_list.append(ret_l)
        idx_c, gate_c = _sc_select(h2c, w, 0, h2c.shape[0] // GATHER_TILE)
        pre_c = _sc_udot(w["u_table"], idx_c, h2c)
        x1l, h2l, _, _, _ = _trunk(y_sample, mod_lat, w, ret_decay_logit[l],
                                   (cache_ckv[:, l], cache_krope[:, l], state_ret[:, l]), after=idx_c)
        y_ctx, y_lat = _peer_split((idx_c, gate_c, pre_c), (x1c, h2c), (x1l, h2l), mod_ctx, mod_lat,
                                   y_prompt.shape[1], y_sample.shape[1], w, tab)
        y_prompt = y_ctx.reshape(y_prompt.shape)
        y_sample = y_lat.reshape(y_sample.shape)
    return (y_prompt, y_sample, jnp.stack(ckv_list, axis=1), jnp.stack(krope_list, axis=1),
            jnp.stack(ret_list, axis=1))
```

```python
import functools
import math

import jax
import jax.numpy as jnp
from jax import lax
from jax.experimental import pallas as pl
from jax.experimental.pallas import tpu as pltpu
from jax.experimental.pallas import tpu_sc as plsc

F32 = jnp.float32
BF16 = jnp.bfloat16

D_MODEL = 1024
GRID_W = 64
MLA_HEADS = 4
MLA_NOPE = 128
MLA_ROPE = 64
MLA_QK = MLA_NOPE + MLA_ROPE
MLA_V = 128
Q_RANK = 512
KV_RANK = 256
ROPE_AXIS = MLA_ROPE // 2
ROPE_BASE = 10000.0
RET_HEADS = 4
RET_DK = 128
RET_DV = 128
RET_CHUNK = 128
PEER_HEADS = 8
PEER_QDIM = 256
PEER_HALF = PEER_QDIM // 2
N_KEYS = 128
PEER_TOPK = 16
EPS = 1e-6

LANE = 128
HEAD_PAD = 2 * LANE
RET_W = 4 * RET_HEADS * RET_DK
IN_PAD = Q_RANK + KV_RANK + LANE + RET_W
VMEM_LIMIT = 48 * 1024 * 1024

ROW_TILE = 256
Q_TILE = 256
GATHER_TILE = 128
GATHER_SLOTS = 4
GATHER_AHEAD = GATHER_SLOTS - 1
EXPERTS_PER_TOKEN = PEER_HEADS * PEER_TOPK
SLAB_ROWS = D_MODEL // LANE


def _params(*sem):
    return pltpu.CompilerParams(dimension_semantics=sem, vmem_limit_bytes=VMEM_LIMIT)


def _rms(x, w):
    return x * lax.rsqrt(jnp.mean(x * x, axis=-1, keepdims=True) + EPS) * w


def _mm(a, b):
    return jnp.dot(a.astype(BF16), b.astype(BF16), preferred_element_type=F32)


def _mm_nt(a, b):
    return lax.dot_general(a.astype(BF16), b.astype(BF16), (((1,), (1,)), ((), ())),
                           preferred_element_type=F32)


def _mm_tn(a, b):
    return lax.dot_general(a.astype(BF16), b.astype(BF16), (((0,), (0,)), ((), ())),
                           preferred_element_type=F32)


def _ada_kernel(c_ref, w_ref, b_ref, o_ref):
    c = c_ref[...]
    o_ref[...] = _mm(c * jax.nn.sigmoid(c), w_ref[...]) + b_ref[...]


def _ada(cond, w_ada, b_ada):
    rows, d = cond.shape
    n = w_ada.shape[1]
    tn = 1536
    return pl.pallas_call(
        _ada_kernel,
        grid=(n // tn,),
        in_specs=[pl.BlockSpec((rows, d), lambda j: (0, 0)),
                  pl.BlockSpec((d, tn), lambda j: (0, j)),
                  pl.BlockSpec((1, tn), lambda j: (0, j))],
        out_specs=pl.BlockSpec((rows, tn), lambda j: (0, j)),
        out_shape=jax.ShapeDtypeStruct((rows, n), F32),
        compiler_params=_params("arbitrary"),
        name="ada",
    )(cond, w_ada, b_ada.reshape(1, n))


def _rope_tile(x, cos, sin):
    lane = lax.broadcasted_iota(jnp.int32, x.shape, 1)
    partner = jnp.where((lane % 32) < 16, pltpu.roll(x, LANE - 16, 1), pltpu.roll(x, 16, 1))
    return x * cos + partner * sin


def _kv_heads(kv, kr, khw, cos, sin, use_rope):
    krw = kr * khw[:, LANE:]
    if use_rope:
        krw = _rope_tile(krw, cos, sin)
    ssq_r = jnp.sum(kr * kr, axis=-1, keepdims=True)
    ks, vs = [], []
    for hd in range(MLA_HEADS):
        kn = kv[:, hd * HEAD_PAD: hd * HEAD_PAD + LANE]
        r = lax.rsqrt((jnp.sum(kn * kn, axis=-1, keepdims=True) + ssq_r) / MLA_QK + EPS)
        ks += [kn * r * khw[:, :LANE], krw * r]
        vs.append(kv[:, hd * HEAD_PAD + LANE: (hd + 1) * HEAD_PAD])
    return jnp.concatenate(ks, axis=-1).astype(BF16), jnp.concatenate(vs, axis=-1).astype(BF16)


def _inproj_kernel(x_ref, mod_ref, n1w_ref, win_ref, qnw_ref, kvnw_ref, wuq_ref, wukv_ref, qhw_ref, khw_ref,
                   cos_ref, sin_ref, *rest, use_rope):
    ckv_ref, krope_ref, q_ref, k_ref, v_ref, zr_ref = rest[-6:]
    m = mod_ref[0]
    h = _rms(x_ref[...], n1w_ref[...]) * (1.0 + m[1:2]) + m[0:1]
    z = _mm(h, win_ref[...])
    kr = z[:, Q_RANK + KV_RANK: Q_RANK + KV_RANK + LANE]
    zr_ref[...] = z[:, Q_RANK + KV_RANK + LANE:]
    ckvn = _rms(z[:, Q_RANK: Q_RANK + KV_RANK], kvnw_ref[...])
    ckv_ref[...] = ckvn
    krope_ref[...] = kr[:, :MLA_ROPE]
    cos, sin = cos_ref[...], sin_ref[...]

    q = _mm(_rms(z[:, :Q_RANK], qnw_ref[...]), wuq_ref[...])
    qhw = qhw_ref[...]
    qs = []
    for hd in range(MLA_HEADS):
        qn = q[:, hd * HEAD_PAD: hd * HEAD_PAD + LANE]
        qr = q[:, hd * HEAD_PAD + LANE: (hd + 1) * HEAD_PAD]
        ssq = jnp.sum(qn * qn, axis=-1, keepdims=True) + jnp.sum(qr * qr, axis=-1, keepdims=True)
        r = lax.rsqrt(ssq / MLA_QK + EPS)
        qrw = qr * qhw[:, LANE:]
        if use_rope:
            qrw = _rope_tile(qrw, cos, sin)
        qs += [qn * r * qhw[:, :LANE], qrw * r]
    q_ref[...] = jnp.concatenate(qs, axis=-1).astype(BF16)

    k, v = _kv_heads(_mm(ckvn, wukv_ref[...]), kr, khw_ref[...], cos, sin, use_rope)
    k_ref[...] = k
    v_ref[...] = v


def _in_proj(x, mod, w, cos, sin, seq, use_rope, after=None):
    t = x.shape[0]
    tm = ROW_TILE
    per_batch = seq // tm
    const = lambda i: (0, 0)
    row = lambda i: (i, 0)
    if use_rope:
        pos = lambda i: (i % per_batch, 0)
    else:
        pos = const
    hp = MLA_HEADS * HEAD_PAD
    ordering = [] if after is None else [after]
    return pl.pallas_call(
        functools.partial(_inproj_kernel, use_rope=use_rope),
        grid=(t // tm,),
        in_specs=[pl.BlockSpec((tm, D_MODEL), row),
                  pl.BlockSpec((1, 6, D_MODEL), lambda i: (i // per_batch, 0, 0)),
                  pl.BlockSpec((1, D_MODEL), const),
                  pl.BlockSpec((D_MODEL, IN_PAD), const),
                  pl.BlockSpec((1, Q_RANK), const),
                  pl.BlockSpec((1, KV_RANK), const),
                  pl.BlockSpec((Q_RANK, hp), const),
                  pl.BlockSpec((KV_RANK, hp), const),
                  pl.BlockSpec((1, HEAD_PAD), const),
                  pl.BlockSpec((1, HEAD_PAD), const),
                  pl.BlockSpec((tm, LANE), pos),
                  pl.BlockSpec((tm, LANE), pos)] + [pl.BlockSpec(memory_space=pl.ANY)] * len(ordering),
        out_specs=[pl.BlockSpec((tm, KV_RANK), row),
                   pl.BlockSpec((tm, MLA_ROPE), row),
                   pl.BlockSpec((tm, hp), row),
                   pl.BlockSpec((tm, hp), row),
                   pl.BlockSpec((tm, MLA_HEADS * MLA_V), row),
                   pl.BlockSpec((tm, RET_W), row)],
        out_shape=[jax.ShapeDtypeStruct((t, KV_RANK), F32),
                   jax.ShapeDtypeStruct((t, MLA_ROPE), F32),
                   jax.ShapeDtypeStruct((t, hp), BF16),
                   jax.ShapeDtypeStruct((t, hp), BF16),
                   jax.ShapeDtypeStruct((t, MLA_HEADS * MLA_V), BF16),
                   jax.ShapeDtypeStruct((t, RET_W), F32)],
        compiler_params=_params("arbitrary"),
        name="in_proj",
    )(x, mod, w["norm1_w"], w["w_in"], w["q_norm_w"], w["kv_norm_w"], w["w_uq"], w["w_ukv"],
      w["q_head_norm_w"], w["k_head_norm_w"], cos, sin, *ordering)


def _kvup_kernel(ckv_ref, kr_ref, wukv_ref, khw_ref, k_ref, v_ref):
    k, v = _kv_heads(_mm(ckv_ref[...], wukv_ref[...]), kr_ref[...], khw_ref[...], None, None, False)
    k_ref[...] = k
    v_ref[...] = v


def _kv_up(ckv, kr, w):
    t = ckv.shape[0]
    tm = ROW_TILE
    hp = MLA_HEADS * HEAD_PAD
    const = lambda i: (0, 0)
    row = lambda i: (i, 0)
    return pl.pallas_call(
        _kvup_kernel,
        grid=(t // tm,),
        in_specs=[pl.BlockSpec((tm, KV_RANK), row),
                  pl.BlockSpec((tm, LANE), row),
                  pl.BlockSpec((KV_RANK, hp), const),
                  pl.BlockSpec((1, HEAD_PAD), const)],
        out_specs=[pl.BlockSpec((tm, hp), row),
                   pl.BlockSpec((tm, MLA_HEADS * MLA_V), row)],
        out_shape=[jax.ShapeDtypeStruct((t, hp), BF16),
                   jax.ShapeDtypeStruct((t, MLA_HEADS * MLA_V), BF16)],
        compiler_params=_params("arbitrary"),
        name="kv_up",
    )(ckv, kr, w["w_ukv"], w["k_head_norm_w"])


def _attn_kernel(*refs, has_ctx):
    if has_ctx:
        q_ref, k_ref, v_ref, kc_ref, vc_ref, o_ref = refs
    else:
        q_ref, k_ref, v_ref, o_ref = refs
    scale = MLA_QK ** -0.5
    q = q_ref[...]
    s = _mm_nt(q, k_ref[...]) * scale
    m = jnp.max(s, axis=-1, keepdims=True)
    if has_ctx:
        sc = _mm_nt(q, kc_ref[...]) * scale
        m = jnp.maximum(m, jnp.max(sc, axis=-1, keepdims=True))
    p = jnp.exp(s - m)
    den = jnp.sum(p, axis=-1, keepdims=True)
    o = _mm(p, v_ref[...])
    if has_ctx:
        pc = jnp.exp(sc - m)
        den = den + jnp.sum(pc, axis=-1, keepdims=True)
        o = o + _mm(pc, vc_ref[...])
    o_ref[...] = (o / den).astype(BF16)


def _attention(q, k, v, batch, seq, ctx_kv):
    tq = min(Q_TILE, seq)
    nq = seq // tq
    has_ctx = ctx_kv is not None
    qmap = lambda b, h, i: (b * nq + i, h)
    kvmap = lambda b, h, i: (b, h)
    in_specs = [pl.BlockSpec((tq, HEAD_PAD), qmap),
                pl.BlockSpec((seq, HEAD_PAD), kvmap),
                pl.BlockSpec((seq, MLA_V), kvmap)]
    args = [q, k, v]
    if has_ctx:
        kc, vc = ctx_kv
        past = kc.shape[0] // batch
        in_specs += [pl.BlockSpec((past, HEAD_PAD), kvmap), pl.BlockSpec((past, MLA_V), kvmap)]
        args += [kc, vc]
    return pl.pallas_call(
        functools.partial(_attn_kernel, has_ctx=has_ctx),
        grid=(batch, MLA_HEADS, nq),
        in_specs=in_specs,
        out_specs=pl.BlockSpec((tq, MLA_V), qmap),
        out_shape=jax.ShapeDtypeStruct((batch * seq, MLA_HEADS * MLA_V), BF16),
        compiler_params=_params("arbitrary", "arbitrary", "arbitrary"),
        name="attention",
    )(*args)


def _ret_kernel(*refs, has_s0):
    if has_s0:
        (lg_ref, qf_ref, kf_ref, vf_ref, qb_ref, kb_ref, vb_ref, s0f_ref, s0b_ref,
         of_ref, ob_ref, sf_out, sb_out, sf_scr, sb_scr) = refs
    else:
        (lg_ref, qf_ref, kf_ref, vf_ref, qb_ref, kb_ref, vb_ref,
         of_ref, ob_ref, sf_out, sb_out, sf_scr, sb_scr) = refs
    c = pl.program_id(1)
    cs = RET_CHUNK

    @pl.when(c == 0)
    def _():
        if has_s0:
            sf_scr[...] = s0f_ref[0]
            sb_scr[...] = s0b_ref[0]
        else:
            sf_scr[...] = jnp.zeros_like(sf_scr)
            sb_scr[...] = jnp.zeros_like(sb_scr)

    ii = lax.broadcasted_iota(jnp.int32, (cs, cs), 0).astype(F32)
    jj = lax.broadcasted_iota(jnp.int32, (cs, cs), 1).astype(F32)
    rel = ii - jj
    kscale = RET_DK ** -0.5

    def chunk(q, k, v, s, intra, qdec, kdec, cdec):
        a = _mm_nt(q, k) * intra
        o = _mm(a, v) + _mm(q, s) * qdec
        return o, s * cdec + _mm_tn(k * kdec, v)

    for hd in range(RET_HEADS):
        cols = slice(hd * RET_DK, (hd + 1) * RET_DK)
        lgf = jax.nn.log_sigmoid(jnp.full((1, LANE), lg_ref[0, hd], F32))
        lgb = jax.nn.log_sigmoid(jnp.full((1, LANE), lg_ref[1, hd], F32))

        intra_f = jnp.where(rel >= 0, jnp.exp(jnp.maximum(rel, 0.0) * lgf), 0.0)
        o_f, s_f = chunk(qf_ref[:, cols], kf_ref[:, cols] * kscale, vf_ref[:, cols], sf_scr[hd], intra_f,
                         jnp.exp((ii + 1.0) * lgf), jnp.exp((cs - 1.0 - ii) * lgf), jnp.exp(cs * lgf))
        of_ref[:, cols] = o_f
        sf_scr[hd] = s_f
        sf_out[0, hd] = s_f

        intra_b = jnp.where(rel <= 0, jnp.exp(jnp.maximum(-rel, 0.0) * lgb), 0.0)
        o_b, s_b = chunk(qb_ref[:, cols], kb_ref[:, cols] * kscale, vb_ref[:, cols], sb_scr[hd], intra_b,
                         jnp.exp((cs - ii) * lgb), jnp.exp(ii * lgb), jnp.exp(cs * lgb))
        ob_ref[:, cols] = o_b
        sb_scr[hd] = s_b
        sb_out[0, hd] = s_b


def _retention(zr, decay_logit, batch, seq, s0):
    cs = RET_CHUNK
    nc = seq // cs
    nh = RET_HEADS
    width = nh * RET_DK
    has_s0 = s0 is not None

    def fwd(col):
        return pl.BlockSpec((cs, width), lambda b, c: (b * nc + c, col))

    def bwd(col):
        return pl.BlockSpec((cs, width), lambda b, c: (b * nc + nc - 1 - c, col))

    def state(d):
        return pl.BlockSpec((1, nh, RET_DK, RET_DV), lambda b, c: (b * 2 + d, 0, 0, 0))

    state_out = pl.BlockSpec((1, nh, RET_DK, RET_DV), lambda b, c: (b, 0, 0, 0))
    in_specs = [pl.BlockSpec(memory_space=pltpu.SMEM), fwd(0), fwd(1), fwd(2), bwd(0), bwd(1), bwd(2)]
    args = [decay_logit, zr, zr, zr, zr, zr, zr]
    if has_s0:
        in_specs += [state(0), state(1)]
        args += [s0, s0]
    t = batch * seq
    o_f, o_b, s_f, s_b = pl.pallas_call(
        functools.partial(_ret_kernel, has_s0=has_s0),
        grid=(batch, nc),
        in_specs=in_specs,
        out_specs=[pl.BlockSpec((cs, width), lambda b, c: (b * nc + c, 0)),
                   pl.BlockSpec((cs, width), lambda b, c: (b * nc + nc - 1 - c, 0)),
                   state_out, state_out],
        out_shape=[jax.ShapeDtypeStruct((t, width), F32),
                   jax.ShapeDtypeStruct((t, width), F32),
                   jax.ShapeDtypeStruct((batch, nh, RET_DK, RET_DV), F32),
                   jax.ShapeDtypeStruct((batch, nh, RET_DK, RET_DV), F32)],
        scratch_shapes=[pltpu.VMEM((nh, RET_DK, RET_DV), F32), pltpu.VMEM((nh, RET_DK, RET_DV), F32)],
        compiler_params=_params("arbitrary", "arbitrary"),
        name="retention",
    )(*args)
    return o_f, o_b, s_f, s_b


def _mix_kernel(x_ref, att_ref, of_ref, ob_ref, gr_ref, mod_ref, gnw_ref, wout_ref, n2w_ref, x1_ref, h2_ref):
    o = of_ref[...] + ob_ref[...]
    parts = []
    for hd in range(RET_HEADS):
        oh = o[:, hd * RET_DV:(hd + 1) * RET_DV]
        d = oh - jnp.mean(oh, axis=-1, keepdims=True)
        parts.append(d * lax.rsqrt(jnp.mean(d * d, axis=-1, keepdims=True) + EPS))
    g = gr_ref[...]
    ret = (g * jax.nn.sigmoid(g)) * (jnp.concatenate(parts, axis=-1) * gnw_ref[...])
    na = MLA_HEADS * MLA_V
    mixed = _mm(att_ref[...], wout_ref[:na, :]) + _mm(ret, wout_ref[na:, :])
    m = mod_ref[0]
    x1 = x_ref[...] + m[2:3] * mixed
    x1_ref[...] = x1
    h2_ref[...] = _rms(x1, n2w_ref[...]) * (1.0 + m[4:5]) + m[3:4]


def _mix(x, att, o_f, o_b, zr, mod, w, seq):
    t = x.shape[0]
    tm = ROW_TILE
    per_batch = seq // tm
    const = lambda i: (0, 0)
    row = lambda i: (i, 0)
    half = RET_HEADS * RET_DV
    return pl.pallas_call(
        _mix_kernel,
        grid=(t // tm,),
        in_specs=[pl.BlockSpec((tm, D_MODEL), row),
                  pl.BlockSpec((tm, half), row),
                  pl.BlockSpec((tm, half), row),
                  pl.BlockSpec((tm, half), row),
                  pl.BlockSpec((tm, half), lambda i: (i, 3)),
                  pl.BlockSpec((1, 6, D_MODEL), lambda i: (i // per_batch, 0, 0)),
                  pl.BlockSpec((1, half), const),
                  pl.BlockSpec((D_MODEL, D_MODEL), const),
                  pl.BlockSpec((1, D_MODEL), const)],
        out_specs=[pl.BlockSpec((tm, D_MODEL), row), pl.BlockSpec((tm, D_MODEL), row)],
        out_shape=[jax.ShapeDtypeStruct((t, D_MODEL), F32), jax.ShapeDtypeStruct((t, D_MODEL), F32)],
        compiler_params=_params("arbitrary"),
        name="mix",
    )(x, att, o_f, o_b, zr, mod, w["ret_gn_w"], w["w_out"], w["norm2_w"])


def _top16(jobs):
    def body(r, carry):
        for s_ref, vals_ref, pay_ref, payload in jobs:
            n, cols = s_ref.shape
            rows = lax.broadcasted_iota(jnp.int32, (n, cols), 0).astype(F32)
            s = s_ref[...]
            m = jnp.max(s, axis=0, keepdims=True)
            pos = jnp.min(jnp.where(s == m, rows, float(n)), axis=0, keepdims=True)
            hit = rows == pos
            vals_ref[pl.ds(r, 1), :] = m
            if payload is None:
                pay_ref[pl.ds(r, 1), :] = pos
            else:
                pay_ref[pl.ds(r, 1), :] = jnp.sum(jnp.where(hit, payload, 0.0), axis=0, keepdims=True)
            s_ref[...] = jnp.where(hit, -jnp.inf, s)
        return carry

    lax.fori_loop(0, PEER_TOPK, body, 0)


PAIR_COUNTS = tuple(PEER_TOPK // (a + 1) for a in range(PEER_TOPK))
NUM_PAIRS = sum(PAIR_COUNTS)
PAIR_ROWS = -(-NUM_PAIRS // 8) * 8


HEAD_GROUP = 2
SCORE_REFS = 10


def _score_heads(hbt, wpqts, k1, k2, g_outs, e_outs, scr):
    heads = [scr[h * SCORE_REFS:(h + 1) * SCORE_REFS] for h in range(len(wpqts))]
    for wpqt, (s1_scr, s2_scr, *_) in zip(wpqts, heads):
        qt = jnp.dot(wpqt, hbt, preferred_element_type=F32)
        s1_scr[...] = _mm(k1, qt[:PEER_HALF, :])
        s2_scr[...] = _mm(k2, qt[PEER_HALF:, :])
    _top16([job for (s1, s2, _, _, v1, i1, v2, i2, _, _) in heads for job in ((s1, v1, i1, None), (s2, v2, i2, None))])
    for (_, _, c_scr, p_scr, v1_scr, i1_scr, v2_scr, i2_scr, _, _) in heads:
        c_scr[...] = jnp.full(c_scr.shape, -jnp.inf, F32)
        p_scr[...] = jnp.zeros(p_scr.shape, F32)
        off = 0
        for a, nb in enumerate(PAIR_COUNTS):
            c_scr[off:off + nb, :] = v1_scr[a:a + 1, :] + v2_scr[0:nb, :]
            p_scr[off:off + nb, :] = i1_scr[a:a + 1, :] * float(N_KEYS) + i2_scr[0:nb, :]
            off += nb
    _top16([(c_scr, vt_scr, it_scr, p_scr[...]) for (_, _, c_scr, p_scr, _, _, _, _, vt_scr, it_scr) in heads])
    for (*_, vt_scr, it_scr), g_out, e_out in zip(heads, g_outs, e_outs):
        top = vt_scr[...]
        p = jnp.exp(top - jnp.max(top, axis=0, keepdims=True))
        g_out[...] = p / jnp.sum(p, axis=0, keepdims=True)
        e_out[...] = it_scr[...]


def _score_scratch(tb):
    k = PEER_TOPK
    one = ([pltpu.VMEM((N_KEYS, tb), F32)] * 2 + [pltpu.VMEM((PAIR_ROWS, tb), F32)] * 2
           + [pltpu.VMEM((k, tb), F32)] * 6)
    assert len(one) == SCORE_REFS
    return one * HEAD_GROUP


def _score_group(hbt, wpq_ref, k1_ref, k2_ref, g_ref, e_ref, grp, scr):
    wpqts, g_outs, e_outs = [], [], []
    for h in range(HEAD_GROUP):
        hd = grp * HEAD_GROUP + h
        head_rows = pl.ds(pl.multiple_of(hd * PEER_TOPK, PEER_TOPK), PEER_TOPK)
        wpqts.append(wpq_ref[hd])
        g_outs.append(g_ref.at[head_rows])
        e_outs.append(e_ref.at[head_rows])
    _score_heads(hbt, wpqts, k1_ref[...], k2_ref[...], g_outs, e_outs, scr)


def _peer_kernel(tab_hbm, h2_ref, h2n_ref, wpq_ref, k1_ref, k2_ref, x1_ref, mod_ref, *rest, ntiles, aliased):
    y_ref, idx_smem, g_scr, e_scr, ei_scr, hb_scr = rest[int(aliased):int(aliased) + 6]
    scratch = rest[int(aliased) + 6:]
    _peer_body(tab_hbm, h2_ref, h2n_ref, wpq_ref, k1_ref, k2_ref, x1_ref, mod_ref, y_ref,
               idx_smem, g_scr, e_scr, ei_scr, hb_scr, *scratch, ntiles=ntiles)


def _peer_body(tab_hbm, h2_ref, h2n_ref, wpq_ref, k1_ref, k2_ref, x1_ref, mod_ref, y_ref,
               idx_smem, g_scr, e_scr, ei_scr, hb_scr, *scratch, ntiles):
    bufs = scratch[:GATHER_SLOTS]
    sem_rows, sem_idx = scratch[GATHER_SLOTS:GATHER_SLOTS + 2]
    score_scr = scratch[GATHER_SLOTS + 2:]
    i = pl.program_id(0)
    tb = GATHER_TILE
    ns = GATHER_SLOTS
    ahead = GATHER_AHEAD
    ne = EXPERTS_PER_TOKEN
    slab = SLAB_ROWS
    nphase = PEER_HEADS // HEAD_GROUP
    groups_per_phase = tb // (ns * nphase)
    cur = i % 2
    nxt = 1 - cur
    more = i + 1 < ntiles

    def score(src_ref, grp, slot):
        @pl.when(grp == 0)
        def _():
            hb_scr[...] = src_ref[...].T.astype(BF16)
        _score_group(hb_scr[...], wpq_ref, k1_ref, k2_ref, g_scr.at[slot], e_scr, grp, score_scr)

    def publish(slot):
        ei_scr[...] = e_scr[...].T.astype(jnp.int32)
        copy = pltpu.make_async_copy(ei_scr, idx_smem.at[slot], sem_idx.at[0])
        copy.start()
        copy.wait()

    def issue(islot, tok, rslot):
        for kk in range(ne):
            ex = idx_smem[islot, tok, kk]
            pltpu.make_async_copy(tab_hbm.at[ex], bufs[rslot].at[:, pl.ds(kk * slab, slab), :],
                                  sem_rows.at[rslot]).start(priority=kk % 2)

    def wait_rows(rslot):
        pltpu.make_async_copy(bufs[rslot], bufs[rslot], sem_rows.at[rslot]).wait()

    @pl.when(i == 0)
    def _():
        def first(grp, carry):
            score(h2_ref, grp, 0)
            return carry
        lax.fori_loop(0, nphase, first, 0)
        publish(0)
        for tok in range(ahead):
            issue(0, tok, tok)

    g2 = mod_ref[0][5:6]
    lane = lax.broadcasted_iota(jnp.int32, (ne, tb), 1)

    def compute(j, rslot):
        rows = bufs[rslot]
        x_row = h2_ref[pl.ds(j, 1), :]
        acc = None
        for c in range(slab):
            u_c = rows[0, pl.ds(c, ne, stride=slab), :]
            term = u_c * x_row[:, c * LANE:(c + 1) * LANE]
            acc = term if acc is None else acc + term
        pre = jnp.sum(acc, axis=-1, keepdims=True)
        act = 0.5 * pre * (1.0 + lax.erf(pre * (2.0 ** -0.5)))
        gate = jnp.sum(jnp.where(lane == j, g_scr[cur], 0.0), axis=-1, keepdims=True)
        wgt = gate * act
        outs = []
        for c in range(slab):
            v_c = rows[1, pl.ds(c, ne, stride=slab), :]
            outs.append(jnp.sum(v_c * wgt, axis=0, keepdims=True))
        out = jnp.concatenate(outs, axis=-1)
        y_ref[pl.ds(j, 1), :] = x1_ref[pl.ds(j, 1), :] + g2 * out

    def phase(grp, carry):
        score(h2n_ref, grp, nxt)

        @pl.when(grp == nphase - 1)
        def _():
            publish(nxt)

        def group(gq, inner):
            g = grp * groups_per_phase + gq
            for u in range(ns):
                j = g * ns + u
                wait_rows(u)
                jj = j + ahead
                over = jj >= tb
                issue(jnp.where(over, nxt, cur), jnp.where(over, jj - tb, jj), (u + ahead) % ns)
                compute(j, u)
            return inner

        lax.fori_loop(0, groups_per_phase, group, 0)
        return carry

    lax.fori_loop(0, nphase, phase, 0)

    @pl.when(jnp.logical_not(more))
    def _():
        for u in range(ahead):
            wait_rows(u)


def _peer(tab, h2, x1, mod, w, seq, tile0, ntiles, y_prev=None):
    t = h2.shape[0]
    tb = GATHER_TILE
    per_batch = seq // tb
    row = lambda i: (tile0 + i, 0)
    const = lambda i: (0, 0)
    in_specs = [pl.BlockSpec(memory_space=pl.ANY),
                pl.BlockSpec((tb, D_MODEL), row),
                pl.BlockSpec((tb, D_MODEL), lambda i: (tile0 + jnp.minimum(i + 1, ntiles - 1), 0)),
                pl.BlockSpec((PEER_HEADS, PEER_QDIM, D_MODEL), lambda i: (0, 0, 0)),
                pl.BlockSpec((N_KEYS, PEER_HALF), const),
                pl.BlockSpec((N_KEYS, PEER_HALF), const),
                pl.BlockSpec((tb, D_MODEL), row),
                pl.BlockSpec((1, 6, D_MODEL), lambda i: ((tile0 + i) // per_batch, 0, 0))]
    args = [tab, h2, h2, w["w_pq"], w["sub_keys1"], w["sub_keys2"], x1, mod]
    aliases = {}
    if y_prev is not None:
        in_specs.append(pl.BlockSpec(memory_space=pl.ANY))
        args.append(y_prev)
        aliases = {len(args) - 1: 0}
    return pl.pallas_call(
        functools.partial(_peer_kernel, ntiles=ntiles, aliased=y_prev is not None),
        grid=(ntiles,),
        in_specs=in_specs,
        input_output_aliases=aliases,
        out_specs=pl.BlockSpec((tb, D_MODEL), row),
        out_shape=jax.ShapeDtypeStruct((t, D_MODEL), F32),
        scratch_shapes=[pltpu.SMEM((2, tb, EXPERTS_PER_TOKEN), jnp.int32),
                        pltpu.VMEM((2, EXPERTS_PER_TOKEN, tb), F32),
                        pltpu.VMEM((EXPERTS_PER_TOKEN, tb), F32),
                        pltpu.VMEM((tb, EXPERTS_PER_TOKEN), jnp.int32),
                        pltpu.VMEM((D_MODEL, tb), BF16)]
                       + [pltpu.VMEM((2, EXPERTS_PER_TOKEN * SLAB_ROWS, LANE), F32)] * GATHER_SLOTS
                       + [pltpu.SemaphoreType.DMA((GATHER_SLOTS,)), pltpu.SemaphoreType.DMA((1,))]
                       + _score_scratch(tb),
        compiler_params=pltpu.CompilerParams(dimension_semantics=("arbitrary",),
                                             vmem_limit_bytes=VMEM_LIMIT,
                                             disable_bounds_checks=True),
        name="peer",
    )(*args)


def _score_kernel(h2_ref, wpq_ref, k1_ref, k2_ref, *rest, ordered):
    g_ref, e_ref, hb_scr = rest[int(ordered):int(ordered) + 3]
    score_scr = rest[int(ordered) + 3:]
    hb_scr[...] = h2_ref[...].T.astype(BF16)

    def group(grp, carry):
        _score_group(hb_scr[...], wpq_ref, k1_ref, k2_ref, g_ref, e_ref, grp, score_scr)
        return carry

    lax.fori_loop(0, PEER_HEADS // HEAD_GROUP, group, 0)


def _peer_score(h2, w, tile0, ntiles, after=None):
    tb = GATHER_TILE
    const = lambda i: (0, 0)
    out = pl.BlockSpec((EXPERTS_PER_TOKEN, tb), lambda i: (0, i))
    ordering = [] if after is None else [after]
    return pl.pallas_call(
        functools.partial(_score_kernel, ordered=after is not None),
        grid=(ntiles,),
        in_specs=[pl.BlockSpec((tb, D_MODEL), lambda i: (tile0 + i, 0)),
                  pl.BlockSpec((PEER_HEADS, PEER_QDIM, D_MODEL), lambda i: (0, 0, 0)),
                  pl.BlockSpec((N_KEYS, PEER_HALF), const),
                  pl.BlockSpec((N_KEYS, PEER_HALF), const)] + [pl.BlockSpec(memory_space=pl.ANY)] * len(ordering),
        out_specs=[out, out],
        out_shape=[jax.ShapeDtypeStruct((EXPERTS_PER_TOKEN, ntiles * tb), F32)] * 2,
        scratch_shapes=[pltpu.VMEM((D_MODEL, tb), BF16)] + _score_scratch(tb),
        compiler_params=_params("arbitrary"),
        name="peer_score",
    )(h2, w["w_pq"], w["sub_keys1"], w["sub_keys2"], *ordering)


def _act_kernel(pre_ref, g_ref, after_ref, w_ref):
    del after_ref
    pre = pre_ref[...]
    w_ref[...] = g_ref[...] * (0.5 * pre * (1.0 + lax.erf(pre * (2.0 ** -0.5))))


def _peer_act(pre, gate, after):
    t, ne = pre.shape
    tm = math.gcd(t, 1024)
    row = lambda i: (i, 0)
    return pl.pallas_call(
        _act_kernel,
        grid=(t // tm,),
        in_specs=[pl.BlockSpec((tm, ne), row), pl.BlockSpec((tm, ne), row), pl.BlockSpec(memory_space=pl.ANY)],
        out_specs=pl.BlockSpec((tm, ne), row),
        out_shape=jax.ShapeDtypeStruct((t, ne), F32),
        compiler_params=_params("arbitrary"),
        name="peer_act",
    )(pre, gate, after)


def _residual_kernel(x1_ref, out_ref, mod_ref, *rest):
    rest[-1][...] = x1_ref[...] + mod_ref[0][5:6] * out_ref[...]


def _peer_residual(x1, out, mod, seq, tile0, ntiles, out_tile0, y_prev):
    tb = GATHER_TILE
    per_batch = seq // tb
    row = lambda i: (tile0 + i, 0)
    in_specs = [pl.BlockSpec((tb, D_MODEL), row),
                pl.BlockSpec((tb, D_MODEL), lambda i: (out_tile0 + i, 0)),
                pl.BlockSpec((1, 6, D_MODEL), lambda i: ((tile0 + i) // per_batch, 0, 0))]
    args = [x1, out, mod]
    aliases = {}
    if y_prev is not None:
        in_specs.append(pl.BlockSpec(memory_space=pl.ANY))
        args.append(y_prev)
        aliases = {3: 0}
    return pl.pallas_call(
        _residual_kernel,
        grid=(ntiles,),
        in_specs=in_specs,
        input_output_aliases=aliases,
        out_specs=pl.BlockSpec((tb, D_MODEL), row),
        out_shape=jax.ShapeDtypeStruct(x1.shape, F32),
        compiler_params=_params("arbitrary"),
        name="peer_residual",
    )(*args)


SC_WORKERS = 32
SC_LANES = 16
SC_CHUNK = 16
SC_SHARE = (57, 128)
TC_FIRST = (65, 142)


def _sc_udot(u_table, idx, x, after=None):
    t, ne = idx.shape
    assert t % SC_WORKERS == 0 and ne % SC_CHUNK == 0, (t, ne)
    per_w = t // SC_WORKERS
    nchunk = ne // SC_CHUNK
    nl = D_MODEL // SC_LANES
    mesh = plsc.VectorSubcoreMesh(core_axis_name="c", subcore_axis_name="s")

    @functools.partial(
        pl.kernel, out_type=jax.ShapeDtypeStruct((t, ne), F32), mesh=mesh,
        scratch_types=[pltpu.VMEM((ne,), jnp.int32), pltpu.VMEM((D_MODEL,), F32),
                       pltpu.VMEM((2, SC_CHUNK, D_MODEL), F32), pltpu.VMEM((SC_CHUNK, SC_LANES), F32),
                       pltpu.VMEM((ne,), F32), pltpu.SemaphoreType.DMA((2,))],
        compiler_params=pltpu.CompilerParams(needs_layout_passes=False),
        name="sc_udot")
    def run(u_hbm, idx_hbm, x_hbm, *rest):
        pre_hbm, idx_v, x_v, rows, accs, pre_v, sems = rest[-7:]
        wid = lax.axis_index("s") * 2 + lax.axis_index("c")
        lanes = lax.iota(jnp.int32, SC_LANES)

        def gather(c, b):
            return pltpu.make_async_copy(u_hbm.at[idx_v.at[pl.ds(c * SC_CHUNK, SC_CHUNK)]], rows.at[b], sems.at[b])

        @pl.loop(0, per_w)
        def _(tt):
            tok = wid * per_w + tt
            pltpu.sync_copy(idx_hbm.at[tok], idx_v)
            pltpu.sync_copy(x_hbm.at[tok], x_v)
            gather(0, 0).start()
            for c in range(nchunk):
                b = c % 2
                if c + 1 < nchunk:
                    gather(c + 1, 1 - b).start()
                gather(c, b).wait()
                for g in range(SC_CHUNK // 4):
                    def body(j, acc):
                        xv = x_v[pl.ds(j * SC_LANES, SC_LANES)]
                        return tuple(acc[q] + rows[b, g * 4 + q, pl.ds(j * SC_LANES, SC_LANES)] * xv
                                     for q in range(4))
                    acc = plsc.parallel_loop(0, nl, unroll=4,
                                             carry=tuple(jnp.zeros((SC_LANES,), F32) for _ in range(4)))(body)
                    for q in range(4):
                        accs[g * 4 + q, :] = acc[q]
                tot = jnp.zeros((SC_LANES,), F32)
                for l in range(SC_LANES):
                    tot = tot + plsc.load_gather(accs, [lanes, jnp.full((SC_LANES,), l, jnp.int32)])
                pre_v[pl.ds(c * SC_CHUNK, SC_CHUNK)] = tot
            pltpu.sync_copy(pre_v, pre_hbm.at[tok])

    return run(u_table, idx, x, *([] if after is None else [after]))


def _sc_vsum(v_table, idx, wgt):
    t, ne = idx.shape
    assert t % SC_WORKERS == 0 and ne % SC_CHUNK == 0, (t, ne)
    per_w = t // SC_WORKERS
    nchunk = ne // SC_CHUNK
    block = 16 * SC_LANES
    mesh = plsc.VectorSubcoreMesh(core_axis_name="c", subcore_axis_name="s")

    @functools.partial(
        pl.kernel, out_type=jax.ShapeDtypeStruct((t, D_MODEL), F32), mesh=mesh,
        scratch_types=[pltpu.VMEM((ne,), jnp.int32), pltpu.VMEM((ne,), F32),
                       pltpu.VMEM((2, SC_CHUNK, D_MODEL), F32), pltpu.VMEM((D_MODEL,), F32),
                       pltpu.SemaphoreType.DMA((2,))],
        compiler_params=pltpu.CompilerParams(needs_layout_passes=False),
        name="sc_vsum")
    def run(v_hbm, idx_hbm, w_hbm, out_hbm, idx_v, w_v, rows, out_v, sems):
        wid = lax.axis_index("s") * 2 + lax.axis_index("c")

        def gather(c, b):
            return pltpu.make_async_copy(v_hbm.at[idx_v.at[pl.ds(c * SC_CHUNK, SC_CHUNK)]], rows.at[b], sems.at[b])

        @pl.loop(0, per_w)
        def _(tt):
            tok = wid * per_w + tt
            pltpu.sync_copy(idx_hbm.at[tok], idx_v)
            pltpu.sync_copy(w_hbm.at[tok], w_v)
            gather(0, 0).start()
            for c in range(nchunk):
                b = c % 2
                if c + 1 < nchunk:
                    gather(c + 1, 1 - b).start()
                gather(c, b).wait()
                for d in range(D_MODEL // block):
                    def body(k, acc):
                        wk = plsc.load_gather(w_v, [jnp.full((SC_LANES,), c * SC_CHUNK, jnp.int32) + k])
                        return tuple(acc[j] + rows[b, k, pl.ds(d * block + j * SC_LANES, SC_LANES)] * wk
                                     for j in range(16))
                    if c == 0:
                        init = tuple(jnp.zeros((SC_LANES,), F32) for _ in range(16))
                    else:
                        init = tuple(out_v[pl.ds(d * block + j * SC_LANES, SC_LANES)] for j in range(16))
                    acc = plsc.parallel_loop(0, SC_CHUNK, carry=init)(body)
                    for j in range(16):
                        out_v[pl.ds(d * block + j * SC_LANES, SC_LANES)] = acc[j]
            pltpu.sync_copy(out_v, out_hbm.at[tok])

    return run(v_table, idx, wgt)


def _expert_slabs(u_table, v_table):
    n = u_table.shape[0]
    u = u_table.reshape(n, 1, SLAB_ROWS, LANE)
    v = v_table.reshape(n, 1, SLAB_ROWS, LANE)
    return jnp.concatenate([u, v], axis=1)


def _rope_tables(seq):
    rows = seq // GRID_W
    r = jnp.repeat(jnp.arange(rows, dtype=F32), GRID_W)
    col = jnp.tile(jnp.arange(GRID_W, dtype=F32), rows)
    nf = ROPE_AXIS // 2
    freqs = jnp.power(ROPE_BASE, -jnp.arange(nf, dtype=F32) / nf)
    ar, ac = r[:, None] * freqs, col[:, None] * freqs
    pad = jnp.zeros((seq, LANE - MLA_ROPE), F32)
    cos = jnp.concatenate([jnp.cos(ar), jnp.cos(ar), jnp.cos(ac), jnp.cos(ac), pad], axis=-1)
    sin = jnp.concatenate([-jnp.sin(ar), jnp.sin(ar), -jnp.sin(ac), jnp.sin(ac), pad], axis=-1)
    return cos, sin


def _prep_weights(norm1_w, w_in, q_norm_w, w_uq, kv_norm_w, w_ukv, q_head_norm_w, k_head_norm_w,
                  ret_gn_w, w_out, norm2_w, w_pq, sub_keys1, sub_keys2):
    cut = Q_RANK + KV_RANK + MLA_ROPE
    w_in_p = jnp.concatenate([w_in[:, :cut], jnp.zeros((D_MODEL, LANE - MLA_ROPE), F32), w_in[:, cut:]], axis=1)
    wq = w_uq.reshape(Q_RANK, MLA_HEADS, MLA_QK)
    wq = jnp.pad(wq, ((0, 0), (0, 0), (0, HEAD_PAD - MLA_QK))).reshape(Q_RANK, MLA_HEADS * HEAD_PAD)

    def head_w(v):
        return jnp.pad(v, (0, HEAD_PAD - MLA_QK)).reshape(1, HEAD_PAD)

    return {
        "norm1_w": norm1_w.reshape(1, -1), "w_in": w_in_p.astype(BF16),
        "q_norm_w": q_norm_w.reshape(1, -1), "kv_norm_w": kv_norm_w.reshape(1, -1),
        "w_uq": wq.astype(BF16), "w_ukv": w_ukv.astype(BF16),
        "q_head_norm_w": head_w(q_head_norm_w), "k_head_norm_w": head_w(k_head_norm_w),
        "ret_gn_w": ret_gn_w.reshape(1, -1), "w_out": w_out.astype(BF16), "norm2_w": norm2_w.reshape(1, -1),
        "w_pq": w_pq.astype(BF16).reshape(D_MODEL, PEER_HEADS, PEER_QDIM).transpose(1, 2, 0), "sub_keys1": sub_keys1.astype(BF16), "sub_keys2": sub_keys2.astype(BF16),
    }


def _trunk(x, mod, w, decay_logit, ctx, after=None):
    batch, seq, _ = x.shape
    x2 = x.reshape(batch * seq, D_MODEL)
    use_rope = ctx is not None
    if use_rope:
        cos, sin = _rope_tables(seq)
    else:
        cos = sin = jnp.zeros((ROW_TILE, LANE), F32)
    ckv, krope, q, k, v, zr = _in_proj(x2, mod, w, cos, sin, seq, use_rope, after)
    if use_rope:
        ckv_c, krope_c, state_c = ctx
        past = ckv_c.shape[1]
        kr_c = jnp.pad(krope_c.reshape(batch * past, MLA_ROPE), ((0, 0), (0, LANE - MLA_ROPE)))
        ctx_kv = _kv_up(ckv_c.reshape(batch * past, KV_RANK), kr_c, w)
        s0 = state_c.reshape(batch * 2, RET_HEADS, RET_DK, RET_DV)
    else:
        ctx_kv, s0 = None, None
    att = _attention(q, k, v, batch, seq, ctx_kv)
    o_f, o_b, s_f, s_b = _retention(zr, decay_logit, batch, seq, s0)
    x1, h2 = _mix(x2, att, o_f, o_b, zr, mod, w, seq)
    states = jnp.stack([s_f, s_b], axis=1)
    return x1, h2, ckv.reshape(batch, seq, KV_RANK), krope.reshape(batch, seq, MLA_ROPE), states


def _sc_select(h2, w, tile0, ntiles, after=None):
    gt, et = _peer_score(h2, w, tile0, ntiles, after)
    return et.T.astype(jnp.int32), gt.T


def _peer_split(ctx_sc, x_ctx, x_lat, mod_ctx, mod_lat, seq_ctx, seq_lat, w, tab):
    (x1c, h2c), (x1l, h2l) = x_ctx, x_lat
    tiles_c = h2c.shape[0] // GATHER_TILE
    tiles_l = h2l.shape[0] // GATHER_TILE
    sc_tiles = tiles_l * SC_SHARE[0] // SC_SHARE[1]
    tc_tiles = tiles_l - sc_tiles
    first = tc_tiles * TC_FIRST[0] // TC_FIRST[1]

    idx_c, gate_c, pre_c = ctx_sc
    wgt_c = _peer_act(pre_c, gate_c, h2l)
    idx_l, gate_l = _sc_select(h2l, w, tc_tiles, sc_tiles, wgt_c)
    pre_l = _sc_udot(w["u_table"], idx_l, h2l[tc_tiles * GATHER_TILE:])
    y = _peer(tab, h2l, x1l, mod_lat, w, seq_lat, 0, first)
    idx = jnp.concatenate([idx_c, idx_l], axis=0)
    wgt = jnp.concatenate([wgt_c, _peer_act(pre_l, gate_l, y)], axis=0)
    out = _sc_vsum(w["v_table"], idx, wgt)
    y = _peer(tab, h2l, x1l, mod_lat, w, seq_lat, first, tc_tiles - first, y_prev=y)
    y_lat = _peer_residual(x1l, out, mod_lat, seq_lat, tc_tiles, sc_tiles, tiles_c, y)
    y_ctx = _peer_residual(x1c, out, mod_ctx, seq_ctx, 0, tiles_c, 0, None)
    return y_ctx, y_lat


def kernel(x_prompt, x_sample, c, cache_ckv, cache_krope, state_ret, c_ctx, w_ada, b_ada, norm1_w, w_in,
           q_norm_w, w_uq, kv_norm_w, w_ukv, q_head_norm_w, k_head_norm_w, ret_decay_logit, ret_gn_w,
           w_out, norm2_w, w_pq, sub_keys1, sub_keys2, u_table, v_table):
    depth = w_ada.shape[0]
    nb_ctx = x_prompt.shape[0]
    nb_lat = x_sample.shape[0]
    y_prompt, y_sample = x_prompt, x_sample
    ckv_list, krope_list, ret_list = [], [], []
    for l in range(depth):
        cond_rows = -(-(nb_lat + 1) // 8) * 8
        cond = jnp.concatenate([c, c_ctx[None, :], jnp.zeros((cond_rows - nb_lat - 1, D_MODEL), F32)], axis=0)
        mod = _ada(cond, w_ada[l], b_ada[l])
        mod_lat = mod[:nb_lat].reshape(nb_lat, 6, D_MODEL)
        mod_ctx = jnp.broadcast_to(mod[nb_lat].reshape(1, 6, D_MODEL), (nb_ctx, 6, D_MODEL))
        w = _prep_weights(norm1_w[l], w_in[l], q_norm_w[l], w_uq[l], kv_norm_w[l], w_ukv[l], q_head_norm_w[l],
                          k_head_norm_w[l], ret_gn_w[l], w_out[l], norm2_w[l], w_pq[l], sub_keys1[l], sub_keys2[l])
        tab = _expert_slabs(u_table[l], v_table[l])
        w["u_table"], w["v_table"] = u_table[l], v_table[l]
        x1c, h2c, ckv_l, krope_l, ret_l = _trunk(y_prompt, mod_ctx, w, ret_decay_logit[l], None)
        ckv_list.append(ckv_l)
        krope_list.append(krope_l)
        ret_list.append(ret_l)
        idx_c, gate_c = _sc_select(h2c, w, 0, h2c.shape[0] // GATHER_TILE)
        pre_c = _sc_udot(w["u_table"], idx_c, h2c)
        x1l, h2l, _, _, _ = _trunk(y_sample, mod_lat, w, ret_decay_logit[l],
                                   (cache_ckv[:, l], cache_krope[:, l], state_ret[:, l]), after=idx_c)
        y_ctx, y_lat = _peer_split((idx_c, gate_c, pre_c), (x1c, h2c), (x1l, h2l), mod_ctx, mod_lat,
                                   y_prompt.shape[1], y_sample.shape[1], w, tab)
        y_prompt = y_ctx.reshape(y_prompt.shape)
        y_sample = y_lat.reshape(y_sample.shape)
    return (y_prompt, y_sample, jnp.stack(ckv_list, axis=1), jnp.stack(krope_list, axis=1),
            jnp.stack(ret_list, axis=1))
```

```python
import functools
import math

import jax
import jax.numpy as jnp
from jax import lax
from jax.experimental import pallas as pl
from jax.experimental.pallas import tpu as pltpu
from jax.experimental.pallas import tpu_sc as plsc

F32 = jnp.float32
BF16 = jnp.bfloat16

D_MODEL = 1024
GRID_W = 64
MLA_HEADS = 4
MLA_NOPE = 128
MLA_ROPE = 64
MLA_QK = MLA_NOPE + MLA_ROPE
MLA_V = 128
Q_RANK = 512
KV_RANK = 256
ROPE_AXIS = MLA_ROPE // 2
ROPE_BASE = 10000.0
RET_HEADS = 4
RET_DK = 128
RET_DV = 128
RET_CHUNK = 128
PEER_HEADS = 8
PEER_QDIM = 256
PEER_HALF = PEER_QDIM // 2
N_KEYS = 128
PEER_TOPK = 16
EPS = 1e-6

LANE = 128
HEAD_PAD = 2 * LANE
RET_W = 4 * RET_HEADS * RET_DK
IN_PAD = Q_RANK + KV_RANK + LANE + RET_W
VMEM_LIMIT = 48 * 1024 * 1024

ROW_TILE = 256
Q_TILE = 256
GATHER_TILE = 128
GATHER_SLOTS = 4
GATHER_AHEAD = GATHER_SLOTS - 1
EXPERTS_PER_TOKEN = PEER_HEADS * PEER_TOPK
SLAB_ROWS = D_MODEL // LANE


def _params(*sem):
    return pltpu.CompilerParams(dimension_semantics=sem, vmem_limit_bytes=VMEM_LIMIT)


def _rms(x, w):
    return x * lax.rsqrt(jnp.mean(x * x, axis=-1, keepdims=True) + EPS) * w


def _mm(a, b):
    return jnp.dot(a.astype(BF16), b.astype(BF16), preferred_element_type=F32)


def _mm_nt(a, b):
    return lax.dot_general(a.astype(BF16), b.astype(BF16), (((1,), (1,)), ((), ())),
                           preferred_element_type=F32)


def _mm_tn(a, b):
    return lax.dot_general(a.astype(BF16), b.astype(BF16), (((0,), (0,)), ((), ())),
                           preferred_element_type=F32)


def _ada_kernel(c_ref, w_ref, b_ref, o_ref):
    c = c_ref[...]
    o_ref[...] = _mm(c * jax.nn.sigmoid(c), w_ref[...]) + b_ref[...]


def _ada(cond, w_ada, b_ada):
    rows, d = cond.shape
    n = w_ada.shape[1]
    tn = 1536
    return pl.pallas_call(
        _ada_kernel,
        grid=(n // tn,),
        in_specs=[pl.BlockSpec((rows, d), lambda j: (0, 0)),
                  pl.BlockSpec((d, tn), lambda j: (0, j)),
                  pl.BlockSpec((1, tn), lambda j: (0, j))],
        out_specs=pl.BlockSpec((rows, tn), lambda j: (0, j)),
        out_shape=jax.ShapeDtypeStruct((rows, n), F32),
        compiler_params=_params("arbitrary"),
        name="ada",
    )(cond, w_ada, b_ada.reshape(1, n))


def _rope_tile(x, cos, sin):
    lane = lax.broadcasted_iota(jnp.int32, x.shape, 1)
    partner = jnp.where((lane % 32) < 16, pltpu.roll(x, LANE - 16, 1), pltpu.roll(x, 16, 1))
    return x * cos + partner * sin


def _kv_heads(kv, kr, khw, cos, sin, use_rope):
    krw = kr * khw[:, LANE:]
    if use_rope:
        krw = _rope_tile(krw, cos, sin)
    ssq_r = jnp.sum(kr * kr, axis=-1, keepdims=True)
    ks, vs = [], []
    for hd in range(MLA_HEADS):
        kn = kv[:, hd * HEAD_PAD: hd * HEAD_PAD + LANE]
        r = lax.rsqrt((jnp.sum(kn * kn, axis=-1, keepdims=True) + ssq_r) / MLA_QK + EPS)
        ks += [kn * r * khw[:, :LANE], krw * r]
        vs.append(kv[:, hd * HEAD_PAD + LANE: (hd + 1) * HEAD_PAD])
    return jnp.concatenate(ks, axis=-1).astype(BF16), jnp.concatenate(vs, axis=-1).astype(BF16)


def _inproj_kernel(x_ref, mod_ref, n1w_ref, win_ref, qnw_ref, kvnw_ref, wuq_ref, wukv_ref, qhw_ref, khw_ref,
                   cos_ref, sin_ref, *rest, use_rope):
    ckv_ref, krope_ref, q_ref, k_ref, v_ref, zr_ref = rest[-6:]
    m = mod_ref[0]
    h = _rms(x_ref[...], n1w_ref[...]) * (1.0 + m[1:2]) + m[0:1]
    z = _mm(h, win_ref[...])
    kr = z[:, Q_RANK + KV_RANK: Q_RANK + KV_RANK + LANE]
    zr_ref[...] = z[:, Q_RANK + KV_RANK + LANE:]
    ckvn = _rms(z[:, Q_RANK: Q_RANK + KV_RANK], kvnw_ref[...])
    ckv_ref[...] = ckvn
    krope_ref[...] = kr[:, :MLA_ROPE]
    cos, sin = cos_ref[...], sin_ref[...]

    q = _mm(_rms(z[:, :Q_RANK], qnw_ref[...]), wuq_ref[...])
    qhw = qhw_ref[...]
    qs = []
    for hd in range(MLA_HEADS):
        qn = q[:, hd * HEAD_PAD: hd * HEAD_PAD + LANE]
        qr = q[:, hd * HEAD_PAD + LANE: (hd + 1) * HEAD_PAD]
        ssq = jnp.sum(qn * qn, axis=-1, keepdims=True) + jnp.sum(qr * qr, axis=-1, keepdims=True)
        r = lax.rsqrt(ssq / MLA_QK + EPS)
        qrw = qr * qhw[:, LANE:]
        if use_rope:
            qrw = _rope_tile(qrw, cos, sin)
        qs += [qn * r * qhw[:, :LANE], qrw * r]
    q_ref[...] = jnp.concatenate(qs, axis=-1).astype(BF16)

    k, v = _kv_heads(_mm(ckvn, wukv_ref[...]), kr, khw_ref[...], cos, sin, use_rope)
    k_ref[...] = k
    v_ref[...] = v


def _in_proj(x, mod, w, cos, sin, seq, use_rope, after=None):
    t = x.shape[0]
    tm = ROW_TILE
    per_batch = seq // tm
    const = lambda i: (0, 0)
    row = lambda i: (i, 0)
    if use_rope:
        pos = lambda i: (i % per_batch, 0)
    else:
        pos = const
    hp = MLA_HEADS * HEAD_PAD
    ordering = [] if after is None else [after]
    return pl.pallas_call(
        functools.partial(_inproj_kernel, use_rope=use_rope),
        grid=(t // tm,),
        in_specs=[pl.BlockSpec((tm, D_MODEL), row),
                  pl.BlockSpec((1, 6, D_MODEL), lambda i: (i // per_batch, 0, 0)),
                  pl.BlockSpec((1, D_MODEL), const),
                  pl.BlockSpec((D_MODEL, IN_PAD), const),
                  pl.BlockSpec((1, Q_RANK), const),
                  pl.BlockSpec((1, KV_RANK), const),
                  pl.BlockSpec((Q_RANK, hp), const),
                  pl.BlockSpec((KV_RANK, hp), const),
                  pl.BlockSpec((1, HEAD_PAD), const),
                  pl.BlockSpec((1, HEAD_PAD), const),
                  pl.BlockSpec((tm, LANE), pos),
                  pl.BlockSpec((tm, LANE), pos)] + [pl.BlockSpec(memory_space=pl.ANY)] * len(ordering),
        out_specs=[pl.BlockSpec((tm, KV_RANK), row),
                   pl.BlockSpec((tm, MLA_ROPE), row),
                   pl.BlockSpec((tm, hp), row),
                   pl.BlockSpec((tm, hp), row),
                   pl.BlockSpec((tm, MLA_HEADS * MLA_V), row),
                   pl.BlockSpec((tm, RET_W), row)],
        out_shape=[jax.ShapeDtypeStruct((t, KV_RANK), F32),
                   jax.ShapeDtypeStruct((t, MLA_ROPE), F32),
                   jax.ShapeDtypeStruct((t, hp), BF16),
                   jax.ShapeDtypeStruct((t, hp), BF16),
                   jax.ShapeDtypeStruct((t, MLA_HEADS * MLA_V), BF16),
                   jax.ShapeDtypeStruct((t, RET_W), F32)],
        compiler_params=_params("arbitrary"),
        name="in_proj",
    )(x, mod, w["norm1_w"], w["w_in"], w["q_norm_w"], w["kv_norm_w"], w["w_uq"], w["w_ukv"],
      w["q_head_norm_w"], w["k_head_norm_w"], cos, sin, *ordering)


def _kvup_kernel(ckv_ref, kr_ref, wukv_ref, khw_ref, k_ref, v_ref):
    k, v = _kv_heads(_mm(ckv_ref[...], wukv_ref[...]), kr_ref[...], khw_ref[...], None, None, False)
    k_ref[...] = k
    v_ref[...] = v


def _kv_up(ckv, kr, w):
    t = ckv.shape[0]
    tm = ROW_TILE
    hp = MLA_HEADS * HEAD_PAD
    const = lambda i: (0, 0)
    row = lambda i: (i, 0)
    return pl.pallas_call(
        _kvup_kernel,
        grid=(t // tm,),
        in_specs=[pl.BlockSpec((tm, KV_RANK), row),
                  pl.BlockSpec((tm, LANE), row),
                  pl.BlockSpec((KV_RANK, hp), const),
                  pl.BlockSpec((1, HEAD_PAD), const)],
        out_specs=[pl.BlockSpec((tm, hp), row),
                   pl.BlockSpec((tm, MLA_HEADS * MLA_V), row)],
        out_shape=[jax.ShapeDtypeStruct((t, hp), BF16),
                   jax.ShapeDtypeStruct((t, MLA_HEADS * MLA_V), BF16)],
        compiler_params=_params("arbitrary"),
        name="kv_up",
    )(ckv, kr, w["w_ukv"], w["k_head_norm_w"])


def _attn_kernel(*refs, has_ctx):
    if has_ctx:
        q_ref, k_ref, v_ref, kc_ref, vc_ref, o_ref = refs
    else:
        q_ref, k_ref, v_ref, o_ref = refs
    scale = MLA_QK ** -0.5
    q = q_ref[...]
    s = _mm_nt(q, k_ref[...]) * scale
    m = jnp.max(s, axis=-1, keepdims=True)
    if has_ctx:
        sc = _mm_nt(q, kc_ref[...]) * scale
        m = jnp.maximum(m, jnp.max(sc, axis=-1, keepdims=True))
    p = jnp.exp(s - m)
    den = jnp.sum(p, axis=-1, keepdims=True)
    o = _mm(p, v_ref[...])
    if has_ctx:
        pc = jnp.exp(sc - m)
        den = den + jnp.sum(pc, axis=-1, keepdims=True)
        o = o + _mm(pc, vc_ref[...])
    o_ref[...] = (o / den).astype(BF16)


def _attention(q, k, v, batch, seq, ctx_kv):
    tq = min(Q_TILE, seq)
    nq = seq // tq
    has_ctx = ctx_kv is not None
    qmap = lambda b, h, i: (b * nq + i, h)
    kvmap = lambda b, h, i: (b, h)
    in_specs = [pl.BlockSpec((tq, HEAD_PAD), qmap),
                pl.BlockSpec((seq, HEAD_PAD), kvmap),
                pl.BlockSpec((seq, MLA_V), kvmap)]
    args = [q, k, v]
    if has_ctx:
        kc, vc = ctx_kv
        past = kc.shape[0] // batch
        in_specs += [pl.BlockSpec((past, HEAD_PAD), kvmap), pl.BlockSpec((past, MLA_V), kvmap)]
        args += [kc, vc]
    return pl.pallas_call(
        functools.partial(_attn_kernel, has_ctx=has_ctx),
        grid=(batch, MLA_HEADS, nq),
        in_specs=in_specs,
        out_specs=pl.BlockSpec((tq, MLA_V), qmap),
        out_shape=jax.ShapeDtypeStruct((batch * seq, MLA_HEADS * MLA_V), BF16),
        compiler_params=_params("arbitrary", "arbitrary", "arbitrary"),
        name="attention",
    )(*args)


def _ret_kernel(*refs, has_s0):
    if has_s0:
        (lg_ref, qf_ref, kf_ref, vf_ref, qb_ref, kb_ref, vb_ref, s0f_ref, s0b_ref,
         of_ref, ob_ref, sf_out, sb_out, sf_scr, sb_scr) = refs
    else:
        (lg_ref, qf_ref, kf_ref, vf_ref, qb_ref, kb_ref, vb_ref,
         of_ref, ob_ref, sf_out, sb_out, sf_scr, sb_scr) = refs
    c = pl.program_id(1)
    cs = RET_CHUNK

    @pl.when(c == 0)
    def _():
        if has_s0:
            sf_scr[...] = s0f_ref[0]
            sb_scr[...] = s0b_ref[0]
        else:
            sf_scr[...] = jnp.zeros_like(sf_scr)
            sb_scr[...] = jnp.zeros_like(sb_scr)

    ii = lax.broadcasted_iota(jnp.int32, (cs, cs), 0).astype(F32)
    jj = lax.broadcasted_iota(jnp.int32, (cs, cs), 1).astype(F32)
    rel = ii - jj
    kscale = RET_DK ** -0.5

    def chunk(q, k, v, s, intra, qdec, kdec, cdec):
        a = _mm_nt(q, k) * intra
        o = _mm(a, v) + _mm(q, s) * qdec
        return o, s * cdec + _mm_tn(k * kdec, v)

    for hd in range(RET_HEADS):
        cols = slice(hd * RET_DK, (hd + 1) * RET_DK)
        lgf = jax.nn.log_sigmoid(jnp.full((1, LANE), lg_ref[0, hd], F32))
        lgb = jax.nn.log_sigmoid(jnp.full((1, LANE), lg_ref[1, hd], F32))

        intra_f = jnp.where(rel >= 0, jnp.exp(jnp.maximum(rel, 0.0) * lgf), 0.0)
        o_f, s_f = chunk(qf_ref[:, cols], kf_ref[:, cols] * kscale, vf_ref[:, cols], sf_scr[hd], intra_f,
                         jnp.exp((ii + 1.0) * lgf), jnp.exp((cs - 1.0 - ii) * lgf), jnp.exp(cs * lgf))
        of_ref[:, cols] = o_f
        sf_scr[hd] = s_f
        sf_out[0, hd] = s_f

        intra_b = jnp.where(rel <= 0, jnp.exp(jnp.maximum(-rel, 0.0) * lgb), 0.0)
        o_b, s_b = chunk(qb_ref[:, cols], kb_ref[:, cols] * kscale, vb_ref[:, cols], sb_scr[hd], intra_b,
                         jnp.exp((cs - ii) * lgb), jnp.exp(ii * lgb), jnp.exp(cs * lgb))
        ob_ref[:, cols] = o_b
        sb_scr[hd] = s_b
        sb_out[0, hd] = s_b


def _retention(zr, decay_logit, batch, seq, s0):
    cs = RET_CHUNK
    nc = seq // cs
    nh = RET_HEADS
    width = nh * RET_DK
    has_s0 = s0 is not None

    def fwd(col):
        return pl.BlockSpec((cs, width), lambda b, c: (b * nc + c, col))

    def bwd(col):
        return pl.BlockSpec((cs, width), lambda b, c: (b * nc + nc - 1 - c, col))

    def state(d):
        return pl.BlockSpec((1, nh, RET_DK, RET_DV), lambda b, c: (b * 2 + d, 0, 0, 0))

    state_out = pl.BlockSpec((1, nh, RET_DK, RET_DV), lambda b, c: (b, 0, 0, 0))
    in_specs = [pl.BlockSpec(memory_space=pltpu.SMEM), fwd(0), fwd(1), fwd(2), bwd(0), bwd(1), bwd(2)]
    args = [decay_logit, zr, zr, zr, zr, zr, zr]
    if has_s0:
        in_specs += [state(0), state(1)]
        args += [s0, s0]
    t = batch * seq
    o_f, o_b, s_f, s_b = pl.pallas_call(
        functools.partial(_ret_kernel, has_s0=has_s0),
        grid=(batch, nc),
        in_specs=in_specs,
        out_specs=[pl.BlockSpec((cs, width), lambda b, c: (b * nc + c, 0)),
                   pl.BlockSpec((cs, width), lambda b, c: (b * nc + nc - 1 - c, 0)),
                   state_out, state_out],
        out_shape=[jax.ShapeDtypeStruct((t, width), F32),
                   jax.ShapeDtypeStruct((t, width), F32),
                   jax.ShapeDtypeStruct((batch, nh, RET_DK, RET_DV), F32),
                   jax.ShapeDtypeStruct((batch, nh, RET_DK, RET_DV), F32)],
        scratch_shapes=[pltpu.VMEM((nh, RET_DK, RET_DV), F32), pltpu.VMEM((nh, RET_DK, RET_DV), F32)],
        compiler_params=_params("arbitrary", "arbitrary"),
        name="retention",
    )(*args)
    return o_f, o_b, s_f, s_b


def _mix_kernel(x_ref, att_ref, of_ref, ob_ref, gr_ref, mod_ref, gnw_ref, wout_ref, n2w_ref, x1_ref, h2_ref):
    o = of_ref[...] + ob_ref[...]
    parts = []
    for hd in range(RET_HEADS):
        oh = o[:, hd * RET_DV:(hd + 1) * RET_DV]
        d = oh - jnp.mean(oh, axis=-1, keepdims=True)
        parts.append(d * lax.rsqrt(jnp.mean(d * d, axis=-1, keepdims=True) + EPS))
    g = gr_ref[...]
    ret = (g * jax.nn.sigmoid(g)) * (jnp.concatenate(parts, axis=-1) * gnw_ref[...])
    na = MLA_HEADS * MLA_V
    mixed = _mm(att_ref[...], wout_ref[:na, :]) + _mm(ret, wout_ref[na:, :])
    m = mod_ref[0]
    x1 = x_ref[...] + m[2:3] * mixed
    x1_ref[...] = x1
    h2_ref[...] = _rms(x1, n2w_ref[...]) * (1.0 + m[4:5]) + m[3:4]


def _mix(x, att, o_f, o_b, zr, mod, w, seq):
    t = x.shape[0]
    tm = ROW_TILE
    per_batch = seq // tm
    const = lambda i: (0, 0)
    row = lambda i: (i, 0)
    half = RET_HEADS * RET_DV
    return pl.pallas_call(
        _mix_kernel,
        grid=(t // tm,),
        in_specs=[pl.BlockSpec((tm, D_MODEL), row),
                  pl.BlockSpec((tm, half), row),
                  pl.BlockSpec((tm, half), row),
                  pl.BlockSpec((tm, half), row),
                  pl.BlockSpec((tm, half), lambda i: (i, 3)),
                  pl.BlockSpec((1, 6, D_MODEL), lambda i: (i // per_batch, 0, 0)),
                  pl.BlockSpec((1, half), const),
                  pl.BlockSpec((D_MODEL, D_MODEL), const),
                  pl.BlockSpec((1, D_MODEL), const)],
        out_specs=[pl.BlockSpec((tm, D_MODEL), row), pl.BlockSpec((tm, D_MODEL), row)],
        out_shape=[jax.ShapeDtypeStruct((t, D_MODEL), F32), jax.ShapeDtypeStruct((t, D_MODEL), F32)],
        compiler_params=_params("arbitrary"),
        name="mix",
    )(x, att, o_f, o_b, zr, mod, w["ret_gn_w"], w["w_out"], w["norm2_w"])


def _top16(jobs):
    def body(r, carry):
        for s_ref, vals_ref, pay_ref, payload in jobs:
            n, cols = s_ref.shape
            rows = lax.broadcasted_iota(jnp.int32, (n, cols), 0).astype(F32)
            s = s_ref[...]
            m = jnp.max(s, axis=0, keepdims=True)
            pos = jnp.min(jnp.where(s == m, rows, float(n)), axis=0, keepdims=True)
            hit = rows == pos
            vals_ref[pl.ds(r, 1), :] = m
            if payload is None:
                pay_ref[pl.ds(r, 1), :] = pos
            else:
                pay_ref[pl.ds(r, 1), :] = jnp.sum(jnp.where(hit, payload, 0.0), axis=0, keepdims=True)
            s_ref[...] = jnp.where(hit, -jnp.inf, s)
        return carry

    lax.fori_loop(0, PEER_TOPK, body, 0)


PAIR_COUNTS = tuple(PEER_TOPK // (a + 1) for a in range(PEER_TOPK))
NUM_PAIRS = sum(PAIR_COUNTS)
PAIR_ROWS = -(-NUM_PAIRS // 8) * 8


HEAD_GROUP = 2
SCORE_REFS = 10


def _score_heads(hbt, wpqts, k1, k2, g_outs, e_outs, scr):
    heads = [scr[h * SCORE_REFS:(h + 1) * SCORE_REFS] for h in range(len(wpqts))]
    for wpqt, (s1_scr, s2_scr, *_) in zip(wpqts, heads):
        qt = jnp.dot(wpqt, hbt, preferred_element_type=F32)
        s1_scr[...] = _mm(k1, qt[:PEER_HALF, :])
        s2_scr[...] = _mm(k2, qt[PEER_HALF:, :])
    _top16([job for (s1, s2, _, _, v1, i1, v2, i2, _, _) in heads for job in ((s1, v1, i1, None), (s2, v2, i2, None))])
    for (_, _, c_scr, p_scr, v1_scr, i1_scr, v2_scr, i2_scr, _, _) in heads:
        c_scr[...] = jnp.full(c_scr.shape, -jnp.inf, F32)
        p_scr[...] = jnp.zeros(p_scr.shape, F32)
        off = 0
        for a, nb in enumerate(PAIR_COUNTS):
            c_scr[off:off + nb, :] = v1_scr[a:a + 1, :] + v2_scr[0:nb, :]
            p_scr[off:off + nb, :] = i1_scr[a:a + 1, :] * float(N_KEYS) + i2_scr[0:nb, :]
            off += nb
    _top16([(c_scr, vt_scr, it_scr, p_scr[...]) for (_, _, c_scr, p_scr, _, _, _, _, vt_scr, it_scr) in heads])
    for (*_, vt_scr, it_scr), g_out, e_out in zip(heads, g_outs, e_outs):
        top = vt_scr[...]
        p = jnp.exp(top - jnp.max(top, axis=0, keepdims=True))
        g_out[...] = p / jnp.sum(p, axis=0, keepdims=True)
        e_out[...] = it_scr[...]


def _score_scratch(tb):
    k = PEER_TOPK
    one = ([pltpu.VMEM((N_KEYS, tb), F32)] * 2 + [pltpu.VMEM((PAIR_ROWS, tb), F32)] * 2
           + [pltpu.VMEM((k, tb), F32)] * 6)
    assert len(one) == SCORE_REFS
    return one * HEAD_GROUP


def _score_group(hbt, wpq_ref, k1_ref, k2_ref, g_ref, e_ref, grp, scr):
    wpqts, g_outs, e_outs = [], [], []
    for h in range(HEAD_GROUP):
        hd = grp * HEAD_GROUP + h
        head_rows = pl.ds(pl.multiple_of(hd * PEER_TOPK, PEER_TOPK), PEER_TOPK)
        wpqts.append(wpq_ref[hd])
        g_outs.append(g_ref.at[head_rows])
        e_outs.append(e_ref.at[head_rows])
    _score_heads(hbt, wpqts, k1_ref[...], k2_ref[...], g_outs, e_outs, scr)


def _peer_kernel(tab_hbm, h2_ref, h2n_ref, wpq_ref, k1_ref, k2_ref, x1_ref, mod_ref, *rest, ntiles, aliased):
    y_ref, idx_smem, g_scr, e_scr, ei_scr, hb_scr = rest[int(aliased):int(aliased) + 6]
    scratch = rest[int(aliased) + 6:]
    _peer_body(tab_hbm, h2_ref, h2n_ref, wpq_ref, k1_ref, k2_ref, x1_ref, mod_ref, y_ref,
               idx_smem, g_scr, e_scr, ei_scr, hb_scr, *scratch, ntiles=ntiles)


def _peer_body(tab_hbm, h2_ref, h2n_ref, wpq_ref, k1_ref, k2_ref, x1_ref, mod_ref, y_ref,
               idx_smem, g_scr, e_scr, ei_scr, hb_scr, *scratch, ntiles):
    bufs = scratch[:GATHER_SLOTS]
    sem_rows, sem_idx = scratch[GATHER_SLOTS:GATHER_SLOTS + 2]
    score_scr = scratch[GATHER_SLOTS + 2:]
    i = pl.program_id(0)
    tb = GATHER_TILE
    ns = GATHER_SLOTS
    ahead = GATHER_AHEAD
    ne = EXPERTS_PER_TOKEN
    slab = SLAB_ROWS
    nphase = PEER_HEADS // HEAD_GROUP
    groups_per_phase = tb // (ns * nphase)
    cur = i % 2
    nxt = 1 - cur
    more = i + 1 < ntiles

    def score(src_ref, grp, slot):
        @pl.when(grp == 0)
        def _():
            hb_scr[...] = src_ref[...].T.astype(BF16)
        _score_group(hb_scr[...], wpq_ref, k1_ref, k2_ref, g_scr.at[slot], e_scr, grp, score_scr)

    def publish(slot):
        ei_scr[...] = e_scr[...].T.astype(jnp.int32)
        copy = pltpu.make_async_copy(ei_scr, idx_smem.at[slot], sem_idx.at[0])
        copy.start()
        copy.wait()

    def issue(islot, tok, rslot):
        for kk in range(ne):
            ex = idx_smem[islot, tok, kk]
            pltpu.make_async_copy(tab_hbm.at[ex], bufs[rslot].at[:, pl.ds(kk * slab, slab), :],
                                  sem_rows.at[rslot]).start(priority=kk % 2)

    def wait_rows(rslot):
        pltpu.make_async_copy(bufs[rslot], bufs[rslot], sem_rows.at[rslot]).wait()

    @pl.when(i == 0)
    def _():
        def first(grp, carry):
            score(h2_ref, grp, 0)
            return carry
        lax.fori_loop(0, nphase, first, 0)
        publish(0)
        for tok in range(ahead):
            issue(0, tok, tok)

    g2 = mod_ref[0][5:6]
    lane = lax.broadcasted_iota(jnp.int32, (ne, tb), 1)

    def compute(j, rslot):
        rows = bufs[rslot]
        x_row = h2_ref[pl.ds(j, 1), :]
        acc = None
        for c in range(slab):
            u_c = rows[0, pl.ds(c, ne, stride=slab), :]
            term = u_c * x_row[:, c * LANE:(c + 1) * LANE]
            acc = term if acc is None else acc + term
        pre = jnp.sum(acc, axis=-1, keepdims=True)
        act = 0.5 * pre * (1.0 + lax.erf(pre * (2.0 ** -0.5)))
        gate = jnp.sum(jnp.where(lane == j, g_scr[cur], 0.0), axis=-1, keepdims=True)
        wgt = gate * act
        outs = []
        for c in range(slab):
            v_c = rows[1, pl.ds(c, ne, stride=slab), :]
            outs.append(jnp.sum(v_c * wgt, axis=0, keepdims=True))
        out = jnp.concatenate(outs, axis=-1)
        y_ref[pl.ds(j, 1), :] = x1_ref[pl.ds(j, 1), :] + g2 * out

    def phase(grp, carry):
        score(h2n_ref, grp, nxt)

        @pl.when(grp == nphase - 1)
        def _():
            publish(nxt)

        def group(gq, inner):
            g = grp * groups_per_phase + gq
            for u in range(ns):
                j = g * ns + u
                wait_rows(u)
                jj = j + ahead
                over = jj >= tb
                issue(jnp.where(over, nxt, cur), jnp.where(over, jj - tb, jj), (u + ahead) % ns)
                compute(j, u)
            return inner

        lax.fori_loop(0, groups_per_phase, group, 0)
        return carry

    lax.fori_loop(0, nphase, phase, 0)

    @pl.when(jnp.logical_not(more))
    def _():
        for u in range(ahead):
            wait_rows(u)


def _peer(tab, h2, x1, mod, w, seq, tile0, ntiles, y_prev=None):
    t = h2.shape[0]
    tb = GATHER_TILE
    per_batch = seq // tb
    row = lambda i: (tile0 + i, 0)
    const = lambda i: (0, 0)
    in_specs = [pl.BlockSpec(memory_space=pl.ANY),
                pl.BlockSpec((tb, D_MODEL), row),
                pl.BlockSpec((tb, D_MODEL), lambda i: (tile0 + jnp.minimum(i + 1, ntiles - 1), 0)),
                pl.BlockSpec((PEER_HEADS, PEER_QDIM, D_MODEL), lambda i: (0, 0, 0)),
                pl.BlockSpec((N_KEYS, PEER_HALF), const),
                pl.BlockSpec((N_KEYS, PEER_HALF), const),
                pl.BlockSpec((tb, D_MODEL), row),
                pl.BlockSpec((1, 6, D_MODEL), lambda i: ((tile0 + i) // per_batch, 0, 0))]
    args = [tab, h2, h2, w["w_pq"], w["sub_keys1"], w["sub_keys2"], x1, mod]
    aliases = {}
    if y_prev is not None:
        in_specs.append(pl.BlockSpec(memory_space=pl.ANY))
        args.append(y_prev)
        aliases = {len(args) - 1: 0}
    return pl.pallas_call(
        functools.partial(_peer_kernel, ntiles=ntiles, aliased=y_prev is not None),
        grid=(ntiles,),
        in_specs=in_specs,
        input_output_aliases=aliases,
        out_specs=pl.BlockSpec((tb, D_MODEL), row),
        out_shape=jax.ShapeDtypeStruct((t, D_MODEL), F32),
        scratch_shapes=[pltpu.SMEM((2, tb, EXPERTS_PER_TOKEN), jnp.int32),
                        pltpu.VMEM((2, EXPERTS_PER_TOKEN, tb), F32),
                        pltpu.VMEM((EXPERTS_PER_TOKEN, tb), F32),
                        pltpu.VMEM((tb, EXPERTS_PER_TOKEN), jnp.int32),
                        pltpu.VMEM((D_MODEL, tb), BF16)]
                       + [pltpu.VMEM((2, EXPERTS_PER_TOKEN * SLAB_ROWS, LANE), F32)] * GATHER_SLOTS
                       + [pltpu.SemaphoreType.DMA((GATHER_SLOTS,)), pltpu.SemaphoreType.DMA((1,))]
                       + _score_scratch(tb),
        compiler_params=pltpu.CompilerParams(dimension_semantics=("arbitrary",),
                                             vmem_limit_bytes=VMEM_LIMIT,
                                             disable_bounds_checks=True),
        name="peer",
    )(*args)


def _score_kernel(h2_ref, wpq_ref, k1_ref, k2_ref, *rest, ordered):
    g_ref, e_ref, hb_scr = rest[int(ordered):int(ordered) + 3]
    score_scr = rest[int(ordered) + 3:]
    hb_scr[...] = h2_ref[...].T.astype(BF16)

    def group(grp, carry):
        _score_group(hb_scr[...], wpq_ref, k1_ref, k2_ref, g_ref, e_ref, grp, score_scr)
        return carry

    lax.fori_loop(0, PEER_HEADS // HEAD_GROUP, group, 0)


def _peer_score(h2, w, tile0, ntiles, after=None):
    tb = GATHER_TILE
    const = lambda i: (0, 0)
    out = pl.BlockSpec((EXPERTS_PER_TOKEN, tb), lambda i: (0, i))
    ordering = [] if after is None else [after]
    return pl.pallas_call(
        functools.partial(_score_kernel, ordered=after is not None),
        grid=(ntiles,),
        in_specs=[pl.BlockSpec((tb, D_MODEL), lambda i: (tile0 + i, 0)),
                  pl.BlockSpec((PEER_HEADS, PEER_QDIM, D_MODEL), lambda i: (0, 0, 0)),
                  pl.BlockSpec((N_KEYS, PEER_HALF), const),
                  pl.BlockSpec((N_KEYS, PEER_HALF), const)] + [pl.BlockSpec(memory_space=pl.ANY)] * len(ordering),
        out_specs=[out, out],
        out_shape=[jax.ShapeDtypeStruct((EXPERTS_PER_TOKEN, ntiles * tb), F32)] * 2,
        scratch_shapes=[pltpu.VMEM((D_MODEL, tb), BF16)] + _score_scratch(tb),
        compiler_params=_params("arbitrary"),
        name="peer_score",
    )(h2, w["w_pq"], w["sub_keys1"], w["sub_keys2"], *ordering)


def _act_kernel(pre_ref, g_ref, after_ref, w_ref):
    del after_ref
    pre = pre_ref[...]
    w_ref[...] = g_ref[...] * (0.5 * pre * (1.0 + lax.erf(pre * (2.0 ** -0.5))))


def _peer_act(pre, gate, after):
    t, ne = pre.shape
    tm = math.gcd(t, 1024)
    row = lambda i: (i, 0)
    return pl.pallas_call(
        _act_kernel,
        grid=(t // tm,),
        in_specs=[pl.BlockSpec((tm, ne), row), pl.BlockSpec((tm, ne), row), pl.BlockSpec(memory_space=pl.ANY)],
        out_specs=pl.BlockSpec((tm, ne), row),
        out_shape=jax.ShapeDtypeStruct((t, ne), F32),
        compiler_params=_params("arbitrary"),
        name="peer_act",
    )(pre, gate, after)


def _residual_kernel(x1_ref, out_ref, mod_ref, *rest):
    rest[-1][...] = x1_ref[...] + mod_ref[0][5:6] * out_ref[...]


def _peer_residual(x1, out, mod, seq, tile0, ntiles, out_tile0, y_prev):
    tb = GATHER_TILE
    per_batch = seq // tb
    row = lambda i: (tile0 + i, 0)
    in_specs = [pl.BlockSpec((tb, D_MODEL), row),
                pl.BlockSpec((tb, D_MODEL), lambda i: (out_tile0 + i, 0)),
                pl.BlockSpec((1, 6, D_MODEL), lambda i: ((tile0 + i) // per_batch, 0, 0))]
    args = [x1, out, mod]
    aliases = {}
    if y_prev is not None:
        in_specs.append(pl.BlockSpec(memory_space=pl.ANY))
        args.append(y_prev)
        aliases = {3: 0}
    return pl.pallas_call(
        _residual_kernel,
        grid=(ntiles,),
        in_specs=in_specs,
        input_output_aliases=aliases,
        out_specs=pl.BlockSpec((tb, D_MODEL), row),
        out_shape=jax.ShapeDtypeStruct(x1.shape, F32),
        compiler_params=_params("arbitrary"),
        name="peer_residual",
    )(*args)


SC_WORKERS = 32
SC_LANES = 16
SC_CHUNK = 32
SC_SHARE = (57, 128)
TC_FIRST = (65, 142)


def _sc_udot(u_table, idx, x, after=None):
    t, ne = idx.shape
    assert t % SC_WORKERS == 0 and ne % SC_CHUNK == 0, (t, ne)
    per_w = t // SC_WORKERS
    nchunk = ne // SC_CHUNK
    nl = D_MODEL // SC_LANES
    mesh = plsc.VectorSubcoreMesh(core_axis_name="c", subcore_axis_name="s")

    @functools.partial(
        pl.kernel, out_type=jax.ShapeDtypeStruct((t, ne), F32), mesh=mesh,
        scratch_types=[pltpu.VMEM((ne,), jnp.int32), pltpu.VMEM((D_MODEL,), F32),
                       pltpu.VMEM((2, SC_CHUNK, D_MODEL), F32), pltpu.VMEM((SC_CHUNK, SC_LANES), F32),
                       pltpu.VMEM((ne,), F32), pltpu.SemaphoreType.DMA((2,))],
        compiler_params=pltpu.CompilerParams(needs_layout_passes=False),
        name="sc_udot")
    def run(u_hbm, idx_hbm, x_hbm, *rest):
        pre_hbm, idx_v, x_v, rows, accs, pre_v, sems = rest[-7:]
        wid = lax.axis_index("s") * 2 + lax.axis_index("c")
        lanes = lax.iota(jnp.int32, SC_LANES)

        def gather(c, b):
            return pltpu.make_async_copy(u_hbm.at[idx_v.at[pl.ds(c * SC_CHUNK, SC_CHUNK)]], rows.at[b], sems.at[b])

        @pl.loop(0, per_w)
        def _(tt):
            tok = wid * per_w + tt
            pltpu.sync_copy(idx_hbm.at[tok], idx_v)
            pltpu.sync_copy(x_hbm.at[tok], x_v)
            gather(0, 0).start()
            for c in range(nchunk):
                b = c % 2
                if c + 1 < nchunk:
                    gather(c + 1, 1 - b).start()
                gather(c, b).wait()
                for g in range(SC_CHUNK // 4):
                    def body(j, acc):
                        xv = x_v[pl.ds(j * SC_LANES, SC_LANES)]
                        return tuple(acc[q] + rows[b, g * 4 + q, pl.ds(j * SC_LANES, SC_LANES)] * xv
                                     for q in range(4))
                    acc = plsc.parallel_loop(0, nl, unroll=4,
                                             carry=tuple(jnp.zeros((SC_LANES,), F32) for _ in range(4)))(body)
                    for q in range(4):
                        accs[g * 4 + q, :] = acc[q]
                for part in range(SC_CHUNK // SC_LANES):
                    tot = jnp.zeros((SC_LANES,), F32)
                    for l in range(SC_LANES):
                        tot = tot + plsc.load_gather(accs, [lanes + part * SC_LANES,
                                                            jnp.full((SC_LANES,), l, jnp.int32)])
                    pre_v[pl.ds(c * SC_CHUNK + part * SC_LANES, SC_LANES)] = tot
            pltpu.sync_copy(pre_v, pre_hbm.at[tok])

    return run(u_table, idx, x, *([] if after is None else [after]))


def _sc_vsum(v_table, idx, wgt):
    t, ne = idx.shape
    assert t % SC_WORKERS == 0 and ne % SC_CHUNK == 0, (t, ne)
    per_w = t // SC_WORKERS
    nchunk = ne // SC_CHUNK
    block = 16 * SC_LANES
    mesh = plsc.VectorSubcoreMesh(core_axis_name="c", subcore_axis_name="s")

    @functools.partial(
        pl.kernel, out_type=jax.ShapeDtypeStruct((t, D_MODEL), F32), mesh=mesh,
        scratch_types=[pltpu.VMEM((ne,), jnp.int32), pltpu.VMEM((ne,), F32),
                       pltpu.VMEM((2, SC_CHUNK, D_MODEL), F32), pltpu.VMEM((D_MODEL,), F32),
                       pltpu.SemaphoreType.DMA((2,))],
        compiler_params=pltpu.CompilerParams(needs_layout_passes=False),
        name="sc_vsum")
    def run(v_hbm, idx_hbm, w_hbm, out_hbm, idx_v, w_v, rows, out_v, sems):
        wid = lax.axis_index("s") * 2 + lax.axis_index("c")

        def gather(c, b):
            return pltpu.make_async_copy(v_hbm.at[idx_v.at[pl.ds(c * SC_CHUNK, SC_CHUNK)]], rows.at[b], sems.at[b])

        @pl.loop(0, per_w)
        def _(tt):
            tok = wid * per_w + tt
            pltpu.sync_copy(idx_hbm.at[tok], idx_v)
            pltpu.sync_copy(w_hbm.at[tok], w_v)
            gather(0, 0).start()
            for c in range(nchunk):
                b = c % 2
                if c + 1 < nchunk:
                    gather(c + 1, 1 - b).start()
                gather(c, b).wait()
                for d in range(D_MODEL // block):
                    def body(k, acc):
                        wk = plsc.load_gather(w_v, [jnp.full((SC_LANES,), c * SC_CHUNK, jnp.int32) + k])
                        return tuple(acc[j] + rows[b, k, pl.ds(d * block + j * SC_LANES, SC_LANES)] * wk
                                     for j in range(16))
                    if c == 0:
                        init = tuple(jnp.zeros((SC_LANES,), F32) for _ in range(16))
                    else:
                        init = tuple(out_v[pl.ds(d * block + j * SC_LANES, SC_LANES)] for j in range(16))
                    acc = plsc.parallel_loop(0, SC_CHUNK, carry=init)(body)
                    for j in range(16):
                        out_v[pl.ds(d * block + j * SC_LANES, SC_LANES)] = acc[j]
            pltpu.sync_copy(out_v, out_hbm.at[tok])

    return run(v_table, idx, wgt)


def _expert_slabs(u_table, v_table):
    n = u_table.shape[0]
    u = u_table.reshape(n, 1, SLAB_ROWS, LANE)
    v = v_table.reshape(n, 1, SLAB_ROWS, LANE)
    return jnp.concatenate([u, v], axis=1)


def _rope_tables(seq):
    rows = seq // GRID_W
    r = jnp.repeat(jnp.arange(rows, dtype=F32), GRID_W)
    col = jnp.tile(jnp.arange(GRID_W, dtype=F32), rows)
    nf = ROPE_AXIS // 2
    freqs = jnp.power(ROPE_BASE, -jnp.arange(nf, dtype=F32) / nf)
    ar, ac = r[:, None] * freqs, col[:, None] * freqs
    pad = jnp.zeros((seq, LANE - MLA_ROPE), F32)
    cos = jnp.concatenate([jnp.cos(ar), jnp.cos(ar), jnp.cos(ac), jnp.cos(ac), pad], axis=-1)
    sin = jnp.concatenate([-jnp.sin(ar), jnp.sin(ar), -jnp.sin(ac), jnp.sin(ac), pad], axis=-1)
    return cos, sin


def _prep_weights(norm1_w, w_in, q_norm_w, w_uq, kv_norm_w, w_ukv, q_head_norm_w, k_head_norm_w,
                  ret_gn_w, w_out, norm2_w, w_pq, sub_keys1, sub_keys2):
    cut = Q_RANK + KV_RANK + MLA_ROPE
    w_in_p = jnp.concatenate([w_in[:, :cut], jnp.zeros((D_MODEL, LANE - MLA_ROPE), F32), w_in[:, cut:]], axis=1)
    wq = w_uq.reshape(Q_RANK, MLA_HEADS, MLA_QK)
    wq = jnp.pad(wq, ((0, 0), (0, 0), (0, HEAD_PAD - MLA_QK))).reshape(Q_RANK, MLA_HEADS * HEAD_PAD)

    def head_w(v):
        return jnp.pad(v, (0, HEAD_PAD - MLA_QK)).reshape(1, HEAD_PAD)

    return {
        "norm1_w": norm1_w.reshape(1, -1), "w_in": w_in_p.astype(BF16),
        "q_norm_w": q_norm_w.reshape(1, -1), "kv_norm_w": kv_norm_w.reshape(1, -1),
        "w_uq": wq.astype(BF16), "w_ukv": w_ukv.astype(BF16),
        "q_head_norm_w": head_w(q_head_norm_w), "k_head_norm_w": head_w(k_head_norm_w),
        "ret_gn_w": ret_gn_w.reshape(1, -1), "w_out": w_out.astype(BF16), "norm2_w": norm2_w.reshape(1, -1),
        "w_pq": w_pq.astype(BF16).reshape(D_MODEL, PEER_HEADS, PEER_QDIM).transpose(1, 2, 0), "sub_keys1": sub_keys1.astype(BF16), "sub_keys2": sub_keys2.astype(BF16),
    }


def _trunk(x, mod, w, decay_logit, ctx, after=None):
    batch, seq, _ = x.shape
    x2 = x.reshape(batch * seq, D_MODEL)
    use_rope = ctx is not None
    if use_rope:
        cos, sin = _rope_tables(seq)
    else:
        cos = sin = jnp.zeros((ROW_TILE, LANE), F32)
    ckv, krope, q, k, v, zr = _in_proj(x2, mod, w, cos, sin, seq, use_rope, after)
    if use_rope:
        ckv_c, krope_c, state_c = ctx
        past = ckv_c.shape[1]
        kr_c = jnp.pad(krope_c.reshape(batch * past, MLA_ROPE), ((0, 0), (0, LANE - MLA_ROPE)))
        ctx_kv = _kv_up(ckv_c.reshape(batch * past, KV_RANK), kr_c, w)
        s0 = state_c.reshape(batch * 2, RET_HEADS, RET_DK, RET_DV)
    else:
        ctx_kv, s0 = None, None
    att = _attention(q, k, v, batch, seq, ctx_kv)
    o_f, o_b, s_f, s_b = _retention(zr, decay_logit, batch, seq, s0)
    x1, h2 = _mix(x2, att, o_f, o_b, zr, mod, w, seq)
    states = jnp.stack([s_f, s_b], axis=1)
    return x1, h2, ckv.reshape(batch, seq, KV_RANK), krope.reshape(batch, seq, MLA_ROPE), states


def _sc_select(h2, w, tile0, ntiles, after=None):
    gt, et = _peer_score(h2, w, tile0, ntiles, after)
    return et.T.astype(jnp.int32), gt.T


def _peer_split(ctx_sc, x_ctx, x_lat, mod_ctx, mod_lat, seq_ctx, seq_lat, w, tab):
    (x1c, h2c), (x1l, h2l) = x_ctx, x_lat
    tiles_c = h2c.shape[0] // GATHER_TILE
    tiles_l = h2l.shape[0] // GATHER_TILE
    sc_tiles = tiles_l * SC_SHARE[0] // SC_SHARE[1]
    tc_tiles = tiles_l - sc_tiles
    first = tc_tiles * TC_FIRST[0] // TC_FIRST[1]

    idx_c, gate_c, pre_c = ctx_sc
    wgt_c = _peer_act(pre_c, gate_c, h2l)
    idx_l, gate_l = _sc_select(h2l, w, tc_tiles, sc_tiles, wgt_c)
    pre_l = _sc_udot(w["u_table"], idx_l, h2l[tc_tiles * GATHER_TILE:])
    y = _peer(tab, h2l, x1l, mod_lat, w, seq_lat, 0, first)
    idx = jnp.concatenate([idx_c, idx_l], axis=0)
    wgt = jnp.concatenate([wgt_c, _peer_act(pre_l, gate_l, y)], axis=0)
    out = _sc_vsum(w["v_table"], idx, wgt)
    y = _peer(tab, h2l, x1l, mod_lat, w, seq_lat, first, tc_tiles - first, y_prev=y)
    y_lat = _peer_residual(x1l, out, mod_lat, seq_lat, tc_tiles, sc_tiles, tiles_c, y)
    y_ctx = _peer_residual(x1c, out, mod_ctx, seq_ctx, 0, tiles_c, 0, None)
    return y_ctx, y_lat


def kernel(x_prompt, x_sample, c, cache_ckv, cache_krope, state_ret, c_ctx, w_ada, b_ada, norm1_w, w_in,
           q_norm_w, w_uq, kv_norm_w, w_ukv, q_head_norm_w, k_head_norm_w, ret_decay_logit, ret_gn_w,
           w_out, norm2_w, w_pq, sub_keys1, sub_keys2, u_table, v_table):
    depth = w_ada.shape[0]
    nb_ctx = x_prompt.shape[0]
    nb_lat = x_sample.shape[0]
    y_prompt, y_sample = x_prompt, x_sample
    ckv_list, krope_list, ret_list = [], [], []
    for l in range(depth):
        cond_rows = -(-(nb_lat + 1) // 8) * 8
        cond = jnp.concatenate([c, c_ctx[None, :], jnp.zeros((cond_rows - nb_lat - 1, D_MODEL), F32)], axis=0)
        mod = _ada(cond, w_ada[l], b_ada[l])
        mod_lat = mod[:nb_lat].reshape(nb_lat, 6, D_MODEL)
        mod_ctx = jnp.broadcast_to(mod[nb_lat].reshape(1, 6, D_MODEL), (nb_ctx, 6, D_MODEL))
        w = _prep_weights(norm1_w[l], w_in[l], q_norm_w[l], w_uq[l], kv_norm_w[l], w_ukv[l], q_head_norm_w[l],
                          k_head_norm_w[l], ret_gn_w[l], w_out[l], norm2_w[l], w_pq[l], sub_keys1[l], sub_keys2[l])
        tab = _expert_slabs(u_table[l], v_table[l])
        w["u_table"], w["v_table"] = u_table[l], v_table[l]
        x1c, h2c, ckv_l, krope_l, ret_l = _trunk(y_prompt, mod_ctx, w, ret_decay_logit[l], None)
        ckv_list.append(ckv_l)
        krope_list.append(krope_l)
        ret_list.append(ret_l)
        idx_c, gate_c = _sc_select(h2c, w, 0, h2c.shape[0] // GATHER_TILE)
        pre_c = _sc_udot(w["u_table"], idx_c, h2c)
        x1l, h2l, _, _, _ = _trunk(y_sample, mod_lat, w, ret_decay_logit[l],
                                   (cache_ckv[:, l], cache_krope[:, l], state_ret[:, l]), after=idx_c)
        y_ctx, y_lat = _peer_split((idx_c, gate_c, pre_c), (x1c, h2c), (x1l, h2l), mod_ctx, mod_lat,
                                   y_prompt.shape[1], y_sample.shape[1], w, tab)
        y_prompt = y_ctx.reshape(y_prompt.shape)
        y_sample = y_lat.reshape(y_sample.shape)
    return (y_prompt, y_sample, jnp.stack(ckv_list, axis=1), jnp.stack(krope_list, axis=1),
            jnp.stack(ret_list, axis=1))
```

```python
import functools
import math

import jax
import jax.numpy as jnp
from jax import lax
from jax.experimental import pallas as pl
from jax.experimental.pallas import tpu as pltpu
from jax.experimental.pallas import tpu_sc as plsc

F32 = jnp.float32
BF16 = jnp.bfloat16

D_MODEL = 1024
GRID_W = 64
MLA_HEADS = 4
MLA_NOPE = 128
MLA_ROPE = 64
MLA_QK = MLA_NOPE + MLA_ROPE
MLA_V = 128
Q_RANK = 512
KV_RANK = 256
ROPE_AXIS = MLA_ROPE // 2
ROPE_BASE = 10000.0
RET_HEADS = 4
RET_DK = 128
RET_DV = 128
RET_CHUNK = 128
PEER_HEADS = 8
PEER_QDIM = 256
PEER_HALF = PEER_QDIM // 2
N_KEYS = 128
PEER_TOPK = 16
EPS = 1e-6

LANE = 128
HEAD_PAD = 2 * LANE
RET_W = 4 * RET_HEADS * RET_DK
IN_PAD = Q_RANK + KV_RANK + LANE + RET_W
VMEM_LIMIT = 48 * 1024 * 1024

ROW_TILE = 256
Q_TILE = 256
GATHER_TILE = 128
GATHER_SLOTS = 4
GATHER_AHEAD = GATHER_SLOTS - 1
EXPERTS_PER_TOKEN = PEER_HEADS * PEER_TOPK
SLAB_ROWS = D_MODEL // LANE


def _params(*sem):
    return pltpu.CompilerParams(dimension_semantics=sem, vmem_limit_bytes=VMEM_LIMIT)


def _rms(x, w):
    return x * lax.rsqrt(jnp.mean(x * x, axis=-1, keepdims=True) + EPS) * w


def _mm(a, b):
    return jnp.dot(a.astype(BF16), b.astype(BF16), preferred_element_type=F32)


def _mm_nt(a, b):
    return lax.dot_general(a.astype(BF16), b.astype(BF16), (((1,), (1,)), ((), ())),
                           preferred_element_type=F32)


def _mm_tn(a, b):
    return lax.dot_general(a.astype(BF16), b.astype(BF16), (((0,), (0,)), ((), ())),
                           preferred_element_type=F32)


def _ada_kernel(c_ref, w_ref, b_ref, o_ref):
    c = c_ref[...]
    o_ref[...] = _mm(c * jax.nn.sigmoid(c), w_ref[...]) + b_ref[...]


def _ada(cond, w_ada, b_ada):
    rows, d = cond.shape
    n = w_ada.shape[1]
    tn = 1536
    return pl.pallas_call(
        _ada_kernel,
        grid=(n // tn,),
        in_specs=[pl.BlockSpec((rows, d), lambda j: (0, 0)),
                  pl.BlockSpec((d, tn), lambda j: (0, j)),
                  pl.BlockSpec((1, tn), lambda j: (0, j))],
        out_specs=pl.BlockSpec((rows, tn), lambda j: (0, j)),
        out_shape=jax.ShapeDtypeStruct((rows, n), F32),
        compiler_params=_params("arbitrary"),
        name="ada",
    )(cond, w_ada, b_ada.reshape(1, n))


def _rope_tile(x, cos, sin):
    lane = lax.broadcasted_iota(jnp.int32, x.shape, 1)
    partner = jnp.where((lane % 32) < 16, pltpu.roll(x, LANE - 16, 1), pltpu.roll(x, 16, 1))
    return x * cos + partner * sin


def _kv_heads(kv, kr, khw, cos, sin, use_rope):
    krw = kr * khw[:, LANE:]
    if use_rope:
        krw = _rope_tile(krw, cos, sin)
    ssq_r = jnp.sum(kr * kr, axis=-1, keepdims=True)
    ks, vs = [], []
    for hd in range(MLA_HEADS):
        kn = kv[:, hd * HEAD_PAD: hd * HEAD_PAD + LANE]
        r = lax.rsqrt((jnp.sum(kn * kn, axis=-1, keepdims=True) + ssq_r) / MLA_QK + EPS)
        ks += [kn * r * khw[:, :LANE], krw * r]
        vs.append(kv[:, hd * HEAD_PAD + LANE: (hd + 1) * HEAD_PAD])
    return jnp.concatenate(ks, axis=-1).astype(BF16), jnp.concatenate(vs, axis=-1).astype(BF16)


def _inproj_kernel(x_ref, mod_ref, n1w_ref, win_ref, qnw_ref, kvnw_ref, wuq_ref, wukv_ref, qhw_ref, khw_ref,
                   cos_ref, sin_ref, *rest, use_rope):
    ckv_ref, krope_ref, q_ref, k_ref, v_ref, zr_ref = rest[-6:]
    m = mod_ref[0]
    h = _rms(x_ref[...], n1w_ref[...]) * (1.0 + m[1:2]) + m[0:1]
    z = _mm(h, win_ref[...])
    kr = z[:, Q_RANK + KV_RANK: Q_RANK + KV_RANK + LANE]
    zr_ref[...] = z[:, Q_RANK + KV_RANK + LANE:]
    ckvn = _rms(z[:, Q_RANK: Q_RANK + KV_RANK], kvnw_ref[...])
    ckv_ref[...] = ckvn
    krope_ref[...] = kr[:, :MLA_ROPE]
    cos, sin = cos_ref[...], sin_ref[...]

    q = _mm(_rms(z[:, :Q_RANK], qnw_ref[...]), wuq_ref[...])
    qhw = qhw_ref[...]
    qs = []
    for hd in range(MLA_HEADS):
        qn = q[:, hd * HEAD_PAD: hd * HEAD_PAD + LANE]
        qr = q[:, hd * HEAD_PAD + LANE: (hd + 1) * HEAD_PAD]
        ssq = jnp.sum(qn * qn, axis=-1, keepdims=True) + jnp.sum(qr * qr, axis=-1, keepdims=True)
        r = lax.rsqrt(ssq / MLA_QK + EPS)
        qrw = qr * qhw[:, LANE:]
        if use_rope:
            qrw = _rope_tile(qrw, cos, sin)
        qs += [qn * r * qhw[:, :LANE], qrw * r]
    q_ref[...] = jnp.concatenate(qs, axis=-1).astype(BF16)

    k, v = _kv_heads(_mm(ckvn, wukv_ref[...]), kr, khw_ref[...], cos, sin, use_rope)
    k_ref[...] = k
    v_ref[...] = v


def _in_proj(x, mod, w, cos, sin, seq, use_rope, after=None):
    t = x.shape[0]
    tm = ROW_TILE
    per_batch = seq // tm
    const = lambda i: (0, 0)
    row = lambda i: (i, 0)
    if use_rope:
        pos = lambda i: (i % per_batch, 0)
    else:
        pos = const
    hp = MLA_HEADS * HEAD_PAD
    ordering = [] if after is None else [after]
    return pl.pallas_call(
        functools.partial(_inproj_kernel, use_rope=use_rope),
        grid=(t // tm,),
        in_specs=[pl.BlockSpec((tm, D_MODEL), row),
                  pl.BlockSpec((1, 6, D_MODEL), lambda i: (i // per_batch, 0, 0)),
                  pl.BlockSpec((1, D_MODEL), const),
                  pl.BlockSpec((D_MODEL, IN_PAD), const),
                  pl.BlockSpec((1, Q_RANK), const),
                  pl.BlockSpec((1, KV_RANK), const),
                  pl.BlockSpec((Q_RANK, hp), const),
                  pl.BlockSpec((KV_RANK, hp), const),
                  pl.BlockSpec((1, HEAD_PAD), const),
                  pl.BlockSpec((1, HEAD_PAD), const),
                  pl.BlockSpec((tm, LANE), pos),
                  pl.BlockSpec((tm, LANE), pos)] + [pl.BlockSpec(memory_space=pl.ANY)] * len(ordering),
        out_specs=[pl.BlockSpec((tm, KV_RANK), row),
                   pl.BlockSpec((tm, MLA_ROPE), row),
                   pl.BlockSpec((tm, hp), row),
                   pl.BlockSpec((tm, hp), row),
                   pl.BlockSpec((tm, MLA_HEADS * MLA_V), row),
                   pl.BlockSpec((tm, RET_W), row)],
        out_shape=[jax.ShapeDtypeStruct((t, KV_RANK), F32),
                   jax.ShapeDtypeStruct((t, MLA_ROPE), F32),
                   jax.ShapeDtypeStruct((t, hp), BF16),
                   jax.ShapeDtypeStruct((t, hp), BF16),
                   jax.ShapeDtypeStruct((t, MLA_HEADS * MLA_V), BF16),
                   jax.ShapeDtypeStruct((t, RET_W), F32)],
        compiler_params=_params("arbitrary"),
        name="in_proj",
    )(x, mod, w["norm1_w"], w["w_in"], w["q_norm_w"], w["kv_norm_w"], w["w_uq"], w["w_ukv"],
      w["q_head_norm_w"], w["k_head_norm_w"], cos, sin, *ordering)


def _kvup_kernel(ckv_ref, kr_ref, wukv_ref, khw_ref, k_ref, v_ref):
    k, v = _kv_heads(_mm(ckv_ref[...], wukv_ref[...]), kr_ref[...], khw_ref[...], None, None, False)
    k_ref[...] = k
    v_ref[...] = v


def _kv_up(ckv, kr, w):
    t = ckv.shape[0]
    tm = ROW_TILE
    hp = MLA_HEADS * HEAD_PAD
    const = lambda i: (0, 0)
    row = lambda i: (i, 0)
    return pl.pallas_call(
        _kvup_kernel,
        grid=(t // tm,),
        in_specs=[pl.BlockSpec((tm, KV_RANK), row),
                  pl.BlockSpec((tm, LANE), row),
                  pl.BlockSpec((KV_RANK, hp), const),
                  pl.BlockSpec((1, HEAD_PAD), const)],
        out_specs=[pl.BlockSpec((tm, hp), row),
                   pl.BlockSpec((tm, MLA_HEADS * MLA_V), row)],
        out_shape=[jax.ShapeDtypeStruct((t, hp), BF16),
                   jax.ShapeDtypeStruct((t, MLA_HEADS * MLA_V), BF16)],
        compiler_params=_params("arbitrary"),
        name="kv_up",
    )(ckv, kr, w["w_ukv"], w["k_head_norm_w"])


def _attn_kernel(*refs, has_ctx):
    if has_ctx:
        q_ref, k_ref, v_ref, kc_ref, vc_ref, o_ref = refs
    else:
        q_ref, k_ref, v_ref, o_ref = refs
    scale = MLA_QK ** -0.5
    q = q_ref[...]
    s = _mm_nt(q, k_ref[...]) * scale
    m = jnp.max(s, axis=-1, keepdims=True)
    if has_ctx:
        sc = _mm_nt(q, kc_ref[...]) * scale
        m = jnp.maximum(m, jnp.max(sc, axis=-1, keepdims=True))
    p = jnp.exp(s - m)
    den = jnp.sum(p, axis=-1, keepdims=True)
    o = _mm(p, v_ref[...])
    if has_ctx:
        pc = jnp.exp(sc - m)
        den = den + jnp.sum(pc, axis=-1, keepdims=True)
        o = o + _mm(pc, vc_ref[...])
    o_ref[...] = (o / den).astype(BF16)


def _attention(q, k, v, batch, seq, ctx_kv):
    tq = min(Q_TILE, seq)
    nq = seq // tq
    has_ctx = ctx_kv is not None
    qmap = lambda b, h, i: (b * nq + i, h)
    kvmap = lambda b, h, i: (b, h)
    in_specs = [pl.BlockSpec((tq, HEAD_PAD), qmap),
                pl.BlockSpec((seq, HEAD_PAD), kvmap),
                pl.BlockSpec((seq, MLA_V), kvmap)]
    args = [q, k, v]
    if has_ctx:
        kc, vc = ctx_kv
        past = kc.shape[0] // batch
        in_specs += [pl.BlockSpec((past, HEAD_PAD), kvmap), pl.BlockSpec((past, MLA_V), kvmap)]
        args += [kc, vc]
    return pl.pallas_call(
        functools.partial(_attn_kernel, has_ctx=has_ctx),
        grid=(batch, MLA_HEADS, nq),
        in_specs=in_specs,
        out_specs=pl.BlockSpec((tq, MLA_V), qmap),
        out_shape=jax.ShapeDtypeStruct((batch * seq, MLA_HEADS * MLA_V), BF16),
        compiler_params=_params("arbitrary", "arbitrary", "arbitrary"),
        name="attention",
    )(*args)


def _ret_kernel(*refs, has_s0):
    if has_s0:
        (lg_ref, qf_ref, kf_ref, vf_ref, qb_ref, kb_ref, vb_ref, s0f_ref, s0b_ref,
         of_ref, ob_ref, sf_out, sb_out, sf_scr, sb_scr) = refs
    else:
        (lg_ref, qf_ref, kf_ref, vf_ref, qb_ref, kb_ref, vb_ref,
         of_ref, ob_ref, sf_out, sb_out, sf_scr, sb_scr) = refs
    c = pl.program_id(1)
    cs = RET_CHUNK

    @pl.when(c == 0)
    def _():
        if has_s0:
            sf_scr[...] = s0f_ref[0]
            sb_scr[...] = s0b_ref[0]
        else:
            sf_scr[...] = jnp.zeros_like(sf_scr)
            sb_scr[...] = jnp.zeros_like(sb_scr)

    ii = lax.broadcasted_iota(jnp.int32, (cs, cs), 0).astype(F32)
    jj = lax.broadcasted_iota(jnp.int32, (cs, cs), 1).astype(F32)
    rel = ii - jj
    kscale = RET_DK ** -0.5

    def chunk(q, k, v, s, intra, qdec, kdec, cdec):
        a = _mm_nt(q, k) * intra
        o = _mm(a, v) + _mm(q, s) * qdec
        return o, s * cdec + _mm_tn(k * kdec, v)

    for hd in range(RET_HEADS):
        cols = slice(hd * RET_DK, (hd + 1) * RET_DK)
        lgf = jax.nn.log_sigmoid(jnp.full((1, LANE), lg_ref[0, hd], F32))
        lgb = jax.nn.log_sigmoid(jnp.full((1, LANE), lg_ref[1, hd], F32))

        intra_f = jnp.where(rel >= 0, jnp.exp(jnp.maximum(rel, 0.0) * lgf), 0.0)
        o_f, s_f = chunk(qf_ref[:, cols], kf_ref[:, cols] * kscale, vf_ref[:, cols], sf_scr[hd], intra_f,
                         jnp.exp((ii + 1.0) * lgf), jnp.exp((cs - 1.0 - ii) * lgf), jnp.exp(cs * lgf))
        of_ref[:, cols] = o_f
        sf_scr[hd] = s_f
        sf_out[0, hd] = s_f

        intra_b = jnp.where(rel <= 0, jnp.exp(jnp.maximum(-rel, 0.0) * lgb), 0.0)
        o_b, s_b = chunk(qb_ref[:, cols], kb_ref[:, cols] * kscale, vb_ref[:, cols], sb_scr[hd], intra_b,
                         jnp.exp((cs - ii) * lgb), jnp.exp(ii * lgb), jnp.exp(cs * lgb))
        ob_ref[:, cols] = o_b
        sb_scr[hd] = s_b
        sb_out[0, hd] = s_b


def _retention(zr, decay_logit, batch, seq, s0):
    cs = RET_CHUNK
    nc = seq // cs
    nh = RET_HEADS
    width = nh * RET_DK
    has_s0 = s0 is not None

    def fwd(col):
        return pl.BlockSpec((cs, width), lambda b, c: (b * nc + c, col))

    def bwd(col):
        return pl.BlockSpec((cs, width), lambda b, c: (b * nc + nc - 1 - c, col))

    def state(d):
        return pl.BlockSpec((1, nh, RET_DK, RET_DV), lambda b, c: (b * 2 + d, 0, 0, 0))

    state_out = pl.BlockSpec((1, nh, RET_DK, RET_DV), lambda b, c: (b, 0, 0, 0))
    in_specs = [pl.BlockSpec(memory_space=pltpu.SMEM), fwd(0), fwd(1), fwd(2), bwd(0), bwd(1), bwd(2)]
    args = [decay_logit, zr, zr, zr, zr, zr, zr]
    if has_s0:
        in_specs += [state(0), state(1)]
        args += [s0, s0]
    t = batch * seq
    o_f, o_b, s_f, s_b = pl.pallas_call(
        functools.partial(_ret_kernel, has_s0=has_s0),
        grid=(batch, nc),
        in_specs=in_specs,
        out_specs=[pl.BlockSpec((cs, width), lambda b, c: (b * nc + c, 0)),
                   pl.BlockSpec((cs, width), lambda b, c: (b * nc + nc - 1 - c, 0)),
                   state_out, state_out],
        out_shape=[jax.ShapeDtypeStruct((t, width), F32),
                   jax.ShapeDtypeStruct((t, width), F32),
                   jax.ShapeDtypeStruct((batch, nh, RET_DK, RET_DV), F32),
                   jax.ShapeDtypeStruct((batch, nh, RET_DK, RET_DV), F32)],
        scratch_shapes=[pltpu.VMEM((nh, RET_DK, RET_DV), F32), pltpu.VMEM((nh, RET_DK, RET_DV), F32)],
        compiler_params=_params("arbitrary", "arbitrary"),
        name="retention",
    )(*args)
    return o_f, o_b, s_f, s_b


def _mix_kernel(x_ref, att_ref, of_ref, ob_ref, gr_ref, mod_ref, gnw_ref, wout_ref, n2w_ref, x1_ref, h2_ref):
    o = of_ref[...] + ob_ref[...]
    parts = []
    for hd in range(RET_HEADS):
        oh = o[:, hd * RET_DV:(hd + 1) * RET_DV]
        d = oh - jnp.mean(oh, axis=-1, keepdims=True)
        parts.append(d * lax.rsqrt(jnp.mean(d * d, axis=-1, keepdims=True) + EPS))
    g = gr_ref[...]
    ret = (g * jax.nn.sigmoid(g)) * (jnp.concatenate(parts, axis=-1) * gnw_ref[...])
    na = MLA_HEADS * MLA_V
    mixed = _mm(att_ref[...], wout_ref[:na, :]) + _mm(ret, wout_ref[na:, :])
    m = mod_ref[0]
    x1 = x_ref[...] + m[2:3] * mixed
    x1_ref[...] = x1
    h2_ref[...] = _rms(x1, n2w_ref[...]) * (1.0 + m[4:5]) + m[3:4]


def _mix(x, att, o_f, o_b, zr, mod, w, seq):
    t = x.shape[0]
    tm = ROW_TILE
    per_batch = seq // tm
    const = lambda i: (0, 0)
    row = lambda i: (i, 0)
    half = RET_HEADS * RET_DV
    return pl.pallas_call(
        _mix_kernel,
        grid=(t // tm,),
        in_specs=[pl.BlockSpec((tm, D_MODEL), row),
                  pl.BlockSpec((tm, half), row),
                  pl.BlockSpec((tm, half), row),
                  pl.BlockSpec((tm, half), row),
                  pl.BlockSpec((tm, half), lambda i: (i, 3)),
                  pl.BlockSpec((1, 6, D_MODEL), lambda i: (i // per_batch, 0, 0)),
                  pl.BlockSpec((1, half), const),
                  pl.BlockSpec((D_MODEL, D_MODEL), const),
                  pl.BlockSpec((1, D_MODEL), const)],
        out_specs=[pl.BlockSpec((tm, D_MODEL), row), pl.BlockSpec((tm, D_MODEL), row)],
        out_shape=[jax.ShapeDtypeStruct((t, D_MODEL), F32), jax.ShapeDtypeStruct((t, D_MODEL), F32)],
        compiler_params=_params("arbitrary"),
        name="mix",
    )(x, att, o_f, o_b, zr, mod, w["ret_gn_w"], w["w_out"], w["norm2_w"])


def _top16(jobs):
    def body(r, carry):
        for s_ref, vals_ref, pay_ref, payload in jobs:
            n, cols = s_ref.shape
            rows = lax.broadcasted_iota(jnp.int32, (n, cols), 0).astype(F32)
            s = s_ref[...]
            m = jnp.max(s, axis=0, keepdims=True)
            pos = jnp.min(jnp.where(s == m, rows, float(n)), axis=0, keepdims=True)
            hit = rows == pos
            vals_ref[pl.ds(r, 1), :] = m
            if payload is None:
                pay_ref[pl.ds(r, 1), :] = pos
            else:
                pay_ref[pl.ds(r, 1), :] = jnp.sum(jnp.where(hit, payload, 0.0), axis=0, keepdims=True)
            s_ref[...] = jnp.where(hit, -jnp.inf, s)
        return carry

    lax.fori_loop(0, PEER_TOPK, body, 0)


PAIR_COUNTS = tuple(PEER_TOPK // (a + 1) for a in range(PEER_TOPK))
NUM_PAIRS = sum(PAIR_COUNTS)
PAIR_ROWS = -(-NUM_PAIRS // 8) * 8


HEAD_GROUP = 2
SCORE_REFS = 10


def _score_heads(hbt, wpqts, k1, k2, g_outs, e_outs, scr):
    heads = [scr[h * SCORE_REFS:(h + 1) * SCORE_REFS] for h in range(len(wpqts))]
    for wpqt, (s1_scr, s2_scr, *_) in zip(wpqts, heads):
        qt = jnp.dot(wpqt, hbt, preferred_element_type=F32)
        s1_scr[...] = _mm(k1, qt[:PEER_HALF, :])
        s2_scr[...] = _mm(k2, qt[PEER_HALF:, :])
    _top16([job for (s1, s2, _, _, v1, i1, v2, i2, _, _) in heads for job in ((s1, v1, i1, None), (s2, v2, i2, None))])
    for (_, _, c_scr, p_scr, v1_scr, i1_scr, v2_scr, i2_scr, _, _) in heads:
        c_scr[...] = jnp.full(c_scr.shape, -jnp.inf, F32)
        p_scr[...] = jnp.zeros(p_scr.shape, F32)
        off = 0
        for a, nb in enumerate(PAIR_COUNTS):
            c_scr[off:off + nb, :] = v1_scr[a:a + 1, :] + v2_scr[0:nb, :]
            p_scr[off:off + nb, :] = i1_scr[a:a + 1, :] * float(N_KEYS) + i2_scr[0:nb, :]
            off += nb
    _top16([(c_scr, vt_scr, it_scr, p_scr[...]) for (_, _, c_scr, p_scr, _, _, _, _, vt_scr, it_scr) in heads])
    for (*_, vt_scr, it_scr), g_out, e_out in zip(heads, g_outs, e_outs):
        top = vt_scr[...]
        p = jnp.exp(top - jnp.max(top, axis=0, keepdims=True))
        g_out[...] = p / jnp.sum(p, axis=0, keepdims=True)
        e_out[...] = it_scr[...]


def _score_scratch(tb):
    k = PEER_TOPK
    one = ([pltpu.VMEM((N_KEYS, tb), F32)] * 2 + [pltpu.VMEM((PAIR_ROWS, tb), F32)] * 2
           + [pltpu.VMEM((k, tb), F32)] * 6)
    assert len(one) == SCORE_REFS
    return one * HEAD_GROUP


def _score_group(hbt, wpq_ref, k1_ref, k2_ref, g_ref, e_ref, grp, scr):
    wpqts, g_outs, e_outs = [], [], []
    for h in range(HEAD_GROUP):
        hd = grp * HEAD_GROUP + h
        head_rows = pl.ds(pl.multiple_of(hd * PEER_TOPK, PEER_TOPK), PEER_TOPK)
        wpqts.append(wpq_ref[hd])
        g_outs.append(g_ref.at[head_rows])
        e_outs.append(e_ref.at[head_rows])
    _score_heads(hbt, wpqts, k1_ref[...], k2_ref[...], g_outs, e_outs, scr)


def _peer_kernel(tab_hbm, h2_ref, h2n_ref, wpq_ref, k1_ref, k2_ref, x1_ref, mod_ref, *rest, ntiles, aliased):
    y_ref, idx_smem, g_scr, e_scr, ei_scr, hb_scr = rest[int(aliased):int(aliased) + 6]
    scratch = rest[int(aliased) + 6:]
    _peer_body(tab_hbm, h2_ref, h2n_ref, wpq_ref, k1_ref, k2_ref, x1_ref, mod_ref, y_ref,
               idx_smem, g_scr, e_scr, ei_scr, hb_scr, *scratch, ntiles=ntiles)


def _peer_body(tab_hbm, h2_ref, h2n_ref, wpq_ref, k1_ref, k2_ref, x1_ref, mod_ref, y_ref,
               idx_smem, g_scr, e_scr, ei_scr, hb_scr, *scratch, ntiles):
    bufs = scratch[:GATHER_SLOTS]
    sem_rows, sem_idx = scratch[GATHER_SLOTS:GATHER_SLOTS + 2]
    score_scr = scratch[GATHER_SLOTS + 2:]
    i = pl.program_id(0)
    tb = GATHER_TILE
    ns = GATHER_SLOTS
    ahead = GATHER_AHEAD
    ne = EXPERTS_PER_TOKEN
    slab = SLAB_ROWS
    nphase = PEER_HEADS // HEAD_GROUP
    groups_per_phase = tb // (ns * nphase)
    cur = i % 2
    nxt = 1 - cur
    more = i + 1 < ntiles

    def score(src_ref, grp, slot):
        @pl.when(grp == 0)
        def _():
            hb_scr[...] = src_ref[...].T.astype(BF16)
        _score_group(hb_scr[...], wpq_ref, k1_ref, k2_ref, g_scr.at[slot], e_scr, grp, score_scr)

    def publish(slot):
        ei_scr[...] = e_scr[...].T.astype(jnp.int32)
        copy = pltpu.make_async_copy(ei_scr, idx_smem.at[slot], sem_idx.at[0])
        copy.start()
        copy.wait()

    def issue(islot, tok, rslot):
        for kk in range(ne):
            ex = idx_smem[islot, tok, kk]
            pltpu.make_async_copy(tab_hbm.at[ex], bufs[rslot].at[:, pl.ds(kk * slab, slab), :],
                                  sem_rows.at[rslot]).start(priority=kk % 2)

    def wait_rows(rslot):
        pltpu.make_async_copy(bufs[rslot], bufs[rslot], sem_rows.at[rslot]).wait()

    @pl.when(i == 0)
    def _():
        def first(grp, carry):
            score(h2_ref, grp, 0)
            return carry
        lax.fori_loop(0, nphase, first, 0)
        publish(0)
        for tok in range(ahead):
            issue(0, tok, tok)

    g2 = mod_ref[0][5:6]
    lane = lax.broadcasted_iota(jnp.int32, (ne, tb), 1)

    def compute(j, rslot):
        rows = bufs[rslot]
        x_row = h2_ref[pl.ds(j, 1), :]
        acc = None
        for c in range(slab):
            u_c = rows[0, pl.ds(c, ne, stride=slab), :]
            term = u_c * x_row[:, c * LANE:(c + 1) * LANE]
            acc = term if acc is None else acc + term
        pre = jnp.sum(acc, axis=-1, keepdims=True)
        act = 0.5 * pre * (1.0 + lax.erf(pre * (2.0 ** -0.5)))
        gate = jnp.sum(jnp.where(lane == j, g_scr[cur], 0.0), axis=-1, keepdims=True)
        wgt = gate * act
        outs = []
        for c in range(slab):
            v_c = rows[1, pl.ds(c, ne, stride=slab), :]
            outs.append(jnp.sum(v_c * wgt, axis=0, keepdims=True))
        out = jnp.concatenate(outs, axis=-1)
        y_ref[pl.ds(j, 1), :] = x1_ref[pl.ds(j, 1), :] + g2 * out

    def phase(grp, carry):
        score(h2n_ref, grp, nxt)

        @pl.when(grp == nphase - 1)
        def _():
            publish(nxt)

        def group(gq, inner):
            g = grp * groups_per_phase + gq
            for u in range(ns):
                j = g * ns + u
                wait_rows(u)
                jj = j + ahead
                over = jj >= tb
                issue(jnp.where(over, nxt, cur), jnp.where(over, jj - tb, jj), (u + ahead) % ns)
                compute(j, u)
            return inner

        lax.fori_loop(0, groups_per_phase, group, 0)
        return carry

    lax.fori_loop(0, nphase, phase, 0)

    @pl.when(jnp.logical_not(more))
    def _():
        for u in range(ahead):
            wait_rows(u)


def _peer(tab, h2, x1, mod, w, seq, tile0, ntiles, y_prev=None):
    t = h2.shape[0]
    tb = GATHER_TILE
    per_batch = seq // tb
    row = lambda i: (tile0 + i, 0)
    const = lambda i: (0, 0)
    in_specs = [pl.BlockSpec(memory_space=pl.ANY),
                pl.BlockSpec((tb, D_MODEL), row),
                pl.BlockSpec((tb, D_MODEL), lambda i: (tile0 + jnp.minimum(i + 1, ntiles - 1), 0)),
                pl.BlockSpec((PEER_HEADS, PEER_QDIM, D_MODEL), lambda i: (0, 0, 0)),
                pl.BlockSpec((N_KEYS, PEER_HALF), const),
                pl.BlockSpec((N_KEYS, PEER_HALF), const),
                pl.BlockSpec((tb, D_MODEL), row),
                pl.BlockSpec((1, 6, D_MODEL), lambda i: ((tile0 + i) // per_batch, 0, 0))]
    args = [tab, h2, h2, w["w_pq"], w["sub_keys1"], w["sub_keys2"], x1, mod]
    aliases = {}
    if y_prev is not None:
        in_specs.append(pl.BlockSpec(memory_space=pl.ANY))
        args.append(y_prev)
        aliases = {len(args) - 1: 0}
    return pl.pallas_call(
        functools.partial(_peer_kernel, ntiles=ntiles, aliased=y_prev is not None),
        grid=(ntiles,),
        in_specs=in_specs,
        input_output_aliases=aliases,
        out_specs=pl.BlockSpec((tb, D_MODEL), row),
        out_shape=jax.ShapeDtypeStruct((t, D_MODEL), F32),
        scratch_shapes=[pltpu.SMEM((2, tb, EXPERTS_PER_TOKEN), jnp.int32),
                        pltpu.VMEM((2, EXPERTS_PER_TOKEN, tb), F32),
                        pltpu.VMEM((EXPERTS_PER_TOKEN, tb), F32),
                        pltpu.VMEM((tb, EXPERTS_PER_TOKEN), jnp.int32),
                        pltpu.VMEM((D_MODEL, tb), BF16)]
                       + [pltpu.VMEM((2, EXPERTS_PER_TOKEN * SLAB_ROWS, LANE), F32)] * GATHER_SLOTS
                       + [pltpu.SemaphoreType.DMA((GATHER_SLOTS,)), pltpu.SemaphoreType.DMA((1,))]
                       + _score_scratch(tb),
        compiler_params=pltpu.CompilerParams(dimension_semantics=("arbitrary",),
                                             vmem_limit_bytes=VMEM_LIMIT,
                                             disable_bounds_checks=True),
        name="peer",
    )(*args)


def _score_kernel(h2_ref, wpq_ref, k1_ref, k2_ref, *rest, ordered):
    g_ref, e_ref, hb_scr = rest[int(ordered):int(ordered) + 3]
    score_scr = rest[int(ordered) + 3:]
    hb_scr[...] = h2_ref[...].T.astype(BF16)

    def group(grp, carry):
        _score_group(hb_scr[...], wpq_ref, k1_ref, k2_ref, g_ref, e_ref, grp, score_scr)
        return carry

    lax.fori_loop(0, PEER_HEADS // HEAD_GROUP, group, 0)


def _peer_score(h2, w, tile0, ntiles, after=None):
    tb = GATHER_TILE
    const = lambda i: (0, 0)
    out = pl.BlockSpec((EXPERTS_PER_TOKEN, tb), lambda i: (0, i))
    ordering = [] if after is None else [after]
    return pl.pallas_call(
        functools.partial(_score_kernel, ordered=after is not None),
        grid=(ntiles,),
        in_specs=[pl.BlockSpec((tb, D_MODEL), lambda i: (tile0 + i, 0)),
                  pl.BlockSpec((PEER_HEADS, PEER_QDIM, D_MODEL), lambda i: (0, 0, 0)),
                  pl.BlockSpec((N_KEYS, PEER_HALF), const),
                  pl.BlockSpec((N_KEYS, PEER_HALF), const)] + [pl.BlockSpec(memory_space=pl.ANY)] * len(ordering),
        out_specs=[out, out],
        out_shape=[jax.ShapeDtypeStruct((EXPERTS_PER_TOKEN, ntiles * tb), F32)] * 2,
        scratch_shapes=[pltpu.VMEM((D_MODEL, tb), BF16)] + _score_scratch(tb),
        compiler_params=_params("arbitrary"),
        name="peer_score",
    )(h2, w["w_pq"], w["sub_keys1"], w["sub_keys2"], *ordering)


def _act_kernel(pre_ref, g_ref, after_ref, w_ref):
    del after_ref
    pre = pre_ref[...]
    w_ref[...] = g_ref[...] * (0.5 * pre * (1.0 + lax.erf(pre * (2.0 ** -0.5))))


def _peer_act(pre, gate, after):
    t, ne = pre.shape
    tm = math.gcd(t, 1024)
    row = lambda i: (i, 0)
    return pl.pallas_call(
        _act_kernel,
        grid=(t // tm,),
        in_specs=[pl.BlockSpec((tm, ne), row), pl.BlockSpec((tm, ne), row), pl.BlockSpec(memory_space=pl.ANY)],
        out_specs=pl.BlockSpec((tm, ne), row),
        out_shape=jax.ShapeDtypeStruct((t, ne), F32),
        compiler_params=_params("arbitrary"),
        name="peer_act",
    )(pre, gate, after)


def _residual_kernel(x1_ref, out_ref, mod_ref, *rest):
    rest[-1][...] = x1_ref[...] + mod_ref[0][5:6] * out_ref[...]


def _peer_residual(x1, out, mod, seq, tile0, ntiles, out_tile0, y_prev):
    tb = GATHER_TILE
    per_batch = seq // tb
    row = lambda i: (tile0 + i, 0)
    in_specs = [pl.BlockSpec((tb, D_MODEL), row),
                pl.BlockSpec((tb, D_MODEL), lambda i: (out_tile0 + i, 0)),
                pl.BlockSpec((1, 6, D_MODEL), lambda i: ((tile0 + i) // per_batch, 0, 0))]
    args = [x1, out, mod]
    aliases = {}
    if y_prev is not None:
        in_specs.append(pl.BlockSpec(memory_space=pl.ANY))
        args.append(y_prev)
        aliases = {3: 0}
    return pl.pallas_call(
        _residual_kernel,
        grid=(ntiles,),
        in_specs=in_specs,
        input_output_aliases=aliases,
        out_specs=pl.BlockSpec((tb, D_MODEL), row),
        out_shape=jax.ShapeDtypeStruct(x1.shape, F32),
        compiler_params=_params("arbitrary"),
        name="peer_residual",
    )(*args)


SC_WORKERS = 32
SC_LANES = 16
SC_CHUNK = 32
SC_SHARE = (29, 64)
TC_FIRST = (13, 28)


def _sc_udot(u_table, idx, x, row0=0):
    t, ne = idx.shape
    assert t % SC_WORKERS == 0 and ne % SC_CHUNK == 0, (t, ne)
    per_w = t // SC_WORKERS
    nchunk = ne // SC_CHUNK
    nl = D_MODEL // SC_LANES
    mesh = plsc.VectorSubcoreMesh(core_axis_name="c", subcore_axis_name="s")

    @functools.partial(
        pl.kernel, out_type=jax.ShapeDtypeStruct((t, ne), F32), mesh=mesh,
        scratch_types=[pltpu.VMEM((ne,), jnp.int32), pltpu.VMEM((D_MODEL,), F32),
                       pltpu.VMEM((2, SC_CHUNK, D_MODEL), F32), pltpu.VMEM((SC_CHUNK, SC_LANES), F32),
                       pltpu.VMEM((ne,), F32), pltpu.SemaphoreType.DMA((2,))],
        compiler_params=pltpu.CompilerParams(needs_layout_passes=False),
        name="sc_udot")
    def run(u_hbm, idx_hbm, x_hbm, pre_hbm, idx_v, x_v, rows, accs, pre_v, sems):
        wid = lax.axis_index("s") * 2 + lax.axis_index("c")
        lanes = lax.iota(jnp.int32, SC_LANES)

        def gather(c, b):
            return pltpu.make_async_copy(u_hbm.at[idx_v.at[pl.ds(c * SC_CHUNK, SC_CHUNK)]], rows.at[b], sems.at[b])

        @pl.loop(0, per_w)
        def _(tt):
            tok = wid * per_w + tt
            pltpu.sync_copy(idx_hbm.at[tok], idx_v)
            pltpu.sync_copy(x_hbm.at[row0 + tok], x_v)
            gather(0, 0).start()
            for c in range(nchunk):
                b = c % 2
                if c + 1 < nchunk:
                    gather(c + 1, 1 - b).start()
                gather(c, b).wait()
                for g in range(SC_CHUNK // 4):
                    def body(j, acc):
                        xv = x_v[pl.ds(j * SC_LANES, SC_LANES)]
                        return tuple(acc[q] + rows[b, g * 4 + q, pl.ds(j * SC_LANES, SC_LANES)] * xv
                                     for q in range(4))
                    acc = plsc.parallel_loop(0, nl, unroll=4,
                                             carry=tuple(jnp.zeros((SC_LANES,), F32) for _ in range(4)))(body)
                    for q in range(4):
                        accs[g * 4 + q, :] = acc[q]
                for part in range(SC_CHUNK // SC_LANES):
                    tot = jnp.zeros((SC_LANES,), F32)
                    for l in range(SC_LANES):
                        tot = tot + plsc.load_gather(accs, [lanes + part * SC_LANES,
                                                            jnp.full((SC_LANES,), l, jnp.int32)])
                    pre_v[pl.ds(c * SC_CHUNK + part * SC_LANES, SC_LANES)] = tot
            pltpu.sync_copy(pre_v, pre_hbm.at[tok])

    return run(u_table, idx, x)


def _sc_vsum(v_table, idx, wgt):
    t, ne = idx.shape
    assert t % SC_WORKERS == 0 and ne % SC_CHUNK == 0, (t, ne)
    per_w = t // SC_WORKERS
    nchunk = ne // SC_CHUNK
    block = 16 * SC_LANES
    mesh = plsc.VectorSubcoreMesh(core_axis_name="c", subcore_axis_name="s")

    @functools.partial(
        pl.kernel, out_type=jax.ShapeDtypeStruct((t, D_MODEL), F32), mesh=mesh,
        scratch_types=[pltpu.VMEM((ne,), jnp.int32), pltpu.VMEM((ne,), F32),
                       pltpu.VMEM((2, SC_CHUNK, D_MODEL), F32), pltpu.VMEM((D_MODEL,), F32),
                       pltpu.SemaphoreType.DMA((2,))],
        compiler_params=pltpu.CompilerParams(needs_layout_passes=False),
        name="sc_vsum")
    def run(v_hbm, idx_hbm, w_hbm, out_hbm, idx_v, w_v, rows, out_v, sems):
        wid = lax.axis_index("s") * 2 + lax.axis_index("c")

        def gather(c, b):
            return pltpu.make_async_copy(v_hbm.at[idx_v.at[pl.ds(c * SC_CHUNK, SC_CHUNK)]], rows.at[b], sems.at[b])

        @pl.loop(0, per_w)
        def _(tt):
            tok = wid * per_w + tt
            pltpu.sync_copy(idx_hbm.at[tok], idx_v)
            pltpu.sync_copy(w_hbm.at[tok], w_v)
            gather(0, 0).start()
            for c in range(nchunk):
                b = c % 2
                if c + 1 < nchunk:
                    gather(c + 1, 1 - b).start()
                gather(c, b).wait()
                for d in range(D_MODEL // block):
                    def body(k, acc):
                        wk = plsc.load_gather(w_v, [jnp.full((SC_LANES,), c * SC_CHUNK, jnp.int32) + k])
                        return tuple(acc[j] + rows[b, k, pl.ds(d * block + j * SC_LANES, SC_LANES)] * wk
                                     for j in range(16))
                    if c == 0:
                        init = tuple(jnp.zeros((SC_LANES,), F32) for _ in range(16))
                    else:
                        init = tuple(out_v[pl.ds(d * block + j * SC_LANES, SC_LANES)] for j in range(16))
                    acc = plsc.parallel_loop(0, SC_CHUNK, carry=init)(body)
                    for j in range(16):
                        out_v[pl.ds(d * block + j * SC_LANES, SC_LANES)] = acc[j]
            pltpu.sync_copy(out_v, out_hbm.at[tok])

    return run(v_table, idx, wgt)


def _expert_slabs(u_table, v_table):
    n = u_table.shape[0]
    u = u_table.reshape(n, 1, SLAB_ROWS, LANE)
    v = v_table.reshape(n, 1, SLAB_ROWS, LANE)
    return jnp.concatenate([u, v], axis=1)


def _rope_tables(seq):
    rows = seq // GRID_W
    r = jnp.repeat(jnp.arange(rows, dtype=F32), GRID_W)
    col = jnp.tile(jnp.arange(GRID_W, dtype=F32), rows)
    nf = ROPE_AXIS // 2
    freqs = jnp.power(ROPE_BASE, -jnp.arange(nf, dtype=F32) / nf)
    ar, ac = r[:, None] * freqs, col[:, None] * freqs
    pad = jnp.zeros((seq, LANE - MLA_ROPE), F32)
    cos = jnp.concatenate([jnp.cos(ar), jnp.cos(ar), jnp.cos(ac), jnp.cos(ac), pad], axis=-1)
    sin = jnp.concatenate([-jnp.sin(ar), jnp.sin(ar), -jnp.sin(ac), jnp.sin(ac), pad], axis=-1)
    return cos, sin


def _prep_weights(norm1_w, w_in, q_norm_w, w_uq, kv_norm_w, w_ukv, q_head_norm_w, k_head_norm_w,
                  ret_gn_w, w_out, norm2_w, w_pq, sub_keys1, sub_keys2):
    cut = Q_RANK + KV_RANK + MLA_ROPE
    w_in_p = jnp.concatenate([w_in[:, :cut], jnp.zeros((D_MODEL, LANE - MLA_ROPE), F32), w_in[:, cut:]], axis=1)
    wq = w_uq.reshape(Q_RANK, MLA_HEADS, MLA_QK)
    wq = jnp.pad(wq, ((0, 0), (0, 0), (0, HEAD_PAD - MLA_QK))).reshape(Q_RANK, MLA_HEADS * HEAD_PAD)

    def head_w(v):
        return jnp.pad(v, (0, HEAD_PAD - MLA_QK)).reshape(1, HEAD_PAD)

    return {
        "norm1_w": norm1_w.reshape(1, -1), "w_in": w_in_p.astype(BF16),
        "q_norm_w": q_norm_w.reshape(1, -1), "kv_norm_w": kv_norm_w.reshape(1, -1),
        "w_uq": wq.astype(BF16), "w_ukv": w_ukv.astype(BF16),
        "q_head_norm_w": head_w(q_head_norm_w), "k_head_norm_w": head_w(k_head_norm_w),
        "ret_gn_w": ret_gn_w.reshape(1, -1), "w_out": w_out.astype(BF16), "norm2_w": norm2_w.reshape(1, -1),
        "w_pq": w_pq.astype(BF16).reshape(D_MODEL, PEER_HEADS, PEER_QDIM).transpose(1, 2, 0), "sub_keys1": sub_keys1.astype(BF16), "sub_keys2": sub_keys2.astype(BF16),
    }


def _trunk(x, mod, w, decay_logit, ctx, after=None):
    batch, seq, _ = x.shape
    x2 = x.reshape(batch * seq, D_MODEL)
    use_rope = ctx is not None
    if use_rope:
        cos, sin = _rope_tables(seq)
    else:
        cos = sin = jnp.zeros((ROW_TILE, LANE), F32)
    ckv, krope, q, k, v, zr = _in_proj(x2, mod, w, cos, sin, seq, use_rope, after)
    if use_rope:
        ckv_c, krope_c, state_c = ctx
        past = ckv_c.shape[1]
        kr_c = jnp.pad(krope_c.reshape(batch * past, MLA_ROPE), ((0, 0), (0, LANE - MLA_ROPE)))
        ctx_kv = _kv_up(ckv_c.reshape(batch * past, KV_RANK), kr_c, w)
        s0 = state_c.reshape(batch * 2, RET_HEADS, RET_DK, RET_DV)
    else:
        ctx_kv, s0 = None, None
    att = _attention(q, k, v, batch, seq, ctx_kv)
    o_f, o_b, s_f, s_b = _retention(zr, decay_logit, batch, seq, s0)
    x1, h2 = _mix(x2, att, o_f, o_b, zr, mod, w, seq)
    states = jnp.stack([s_f, s_b], axis=1)
    return x1, h2, ckv.reshape(batch, seq, KV_RANK), krope.reshape(batch, seq, MLA_ROPE), states


def _sc_select(h2, w, tile0, ntiles, after=None):
    gt, et = _peer_score(h2, w, tile0, ntiles, after)
    return et.T.astype(jnp.int32), gt.T


def _peer_split(ctx_sc, x_ctx, x_lat, mod_ctx, mod_lat, seq_ctx, seq_lat, w, tab):
    (x1c, h2c), (x1l, h2l) = x_ctx, x_lat
    tiles_c = h2c.shape[0] // GATHER_TILE
    tiles_l = h2l.shape[0] // GATHER_TILE
    sc_tiles = tiles_l * SC_SHARE[0] // SC_SHARE[1]
    tc_tiles = tiles_l - sc_tiles
    first = tc_tiles * TC_FIRST[0] // TC_FIRST[1]

    idx_c, gate_c, pre_c = ctx_sc
    wgt_c = _peer_act(pre_c, gate_c, h2l)
    idx_l, gate_l = _sc_select(h2l, w, tc_tiles, sc_tiles, wgt_c)
    pre_l = _sc_udot(w["u_table"], idx_l, h2l, tc_tiles * GATHER_TILE)
    y = _peer(tab, h2l, x1l, mod_lat, w, seq_lat, 0, first)
    idx = jnp.concatenate([idx_c, idx_l], axis=0)
    wgt = jnp.concatenate([wgt_c, _peer_act(pre_l, gate_l, y)], axis=0)
    out = _sc_vsum(w["v_table"], idx, wgt)
    y = _peer(tab, h2l, x1l, mod_lat, w, seq_lat, first, tc_tiles - first, y_prev=y)
    y_lat = _peer_residual(x1l, out, mod_lat, seq_lat, tc_tiles, sc_tiles, tiles_c, y)
    y_ctx = _peer_residual(x1c, out, mod_ctx, seq_ctx, 0, tiles_c, 0, None)
    return y_ctx, y_lat


def kernel(x_prompt, x_sample, c, cache_ckv, cache_krope, state_ret, c_ctx, w_ada, b_ada, norm1_w, w_in,
           q_norm_w, w_uq, kv_norm_w, w_ukv, q_head_norm_w, k_head_norm_w, ret_decay_logit, ret_gn_w,
           w_out, norm2_w, w_pq, sub_keys1, sub_keys2, u_table, v_table):
    depth = w_ada.shape[0]
    nb_ctx = x_prompt.shape[0]
    nb_lat = x_sample.shape[0]
    y_prompt, y_sample = x_prompt, x_sample
    ckv_list, krope_list, ret_list = [], [], []
    for l in range(depth):
        cond_rows = -(-(nb_lat + 1) // 8) * 8
        cond = jnp.concatenate([c, c_ctx[None, :], jnp.zeros((cond_rows - nb_lat - 1, D_MODEL), F32)], axis=0)
        mod = _ada(cond, w_ada[l], b_ada[l])
        mod_lat = mod[:nb_lat].reshape(nb_lat, 6, D_MODEL)
        mod_ctx = jnp.broadcast_to(mod[nb_lat].reshape(1, 6, D_MODEL), (nb_ctx, 6, D_MODEL))
        w = _prep_weights(norm1_w[l], w_in[l], q_norm_w[l], w_uq[l], kv_norm_w[l], w_ukv[l], q_head_norm_w[l],
                          k_head_norm_w[l], ret_gn_w[l], w_out[l], norm2_w[l], w_pq[l], sub_keys1[l], sub_keys2[l])
        tab = _expert_slabs(u_table[l], v_table[l])
        w["u_table"], w["v_table"] = u_table[l], v_table[l]
        x1c, h2c, ckv_l, krope_l, ret_l = _trunk(y_prompt, mod_ctx, w, ret_decay_logit[l], None)
        ckv_list.append(ckv_l)
        krope_list.append(krope_l)
        ret_list.append(ret_l)
        idx_c, gate_c = _sc_select(h2c, w, 0, h2c.shape[0] // GATHER_TILE)
        pre_c = _sc_udot(w["u_table"], idx_c, h2c)
        x1l, h2l, _, _, _ = _trunk(y_sample, mod_lat, w, ret_decay_logit[l],
                                   (cache_ckv[:, l], cache_krope[:, l], state_ret[:, l]), after=idx_c)
        y_ctx, y_lat = _peer_split((idx_c, gate_c, pre_c), (x1c, h2c), (x1l, h2l), mod_ctx, mod_lat,
                                   y_prompt.shape[1], y_sample.shape[1], w, tab)
        y_prompt = y_ctx.reshape(y_prompt.shape)
        y_sample = y_lat.reshape(y_sample.shape)
    return (y_prompt, y_sample, jnp.stack(ckv_list, axis=1), jnp.stack(krope_list, axis=1),
            jnp.stack(ret_list, axis=1))
```

```python
import functools
import math

import jax
import jax.numpy as jnp
from jax import lax
from jax.experimental import pallas as pl
from jax.experimental.pallas import tpu as pltpu
from jax.experimental.pallas import tpu_sc as plsc

F32 = jnp.float32
BF16 = jnp.bfloat16

D_MODEL = 1024
GRID_W = 64
MLA_HEADS = 4
MLA_NOPE = 128
MLA_ROPE = 64
MLA_QK = MLA_NOPE + MLA_ROPE
MLA_V = 128
Q_RANK = 512
KV_RANK = 256
ROPE_AXIS = MLA_ROPE // 2
ROPE_BASE = 10000.0
RET_HEADS = 4
RET_DK = 128
RET_DV = 128
RET_CHUNK = 128
PEER_HEADS = 8
PEER_QDIM = 256
PEER_HALF = PEER_QDIM // 2
N_KEYS = 128
PEER_TOPK = 16
EPS = 1e-6

LANE = 128
HEAD_PAD = 2 * LANE
RET_W = 4 * RET_HEADS * RET_DK
IN_PAD = Q_RANK + KV_RANK + LANE + RET_W
VMEM_LIMIT = 48 * 1024 * 1024

ROW_TILE = 256
Q_TILE = 256
GATHER_TILE = 128
GATHER_SLOTS = 4
GATHER_AHEAD = GATHER_SLOTS - 1
EXPERTS_PER_TOKEN = PEER_HEADS * PEER_TOPK
SLAB_ROWS = D_MODEL // LANE


def _params(*sem):
    return pltpu.CompilerParams(dimension_semantics=sem, vmem_limit_bytes=VMEM_LIMIT)


def _rms(x, w):
    return x * lax.rsqrt(jnp.mean(x * x, axis=-1, keepdims=True) + EPS) * w


def _mm(a, b):
    return jnp.dot(a.astype(BF16), b.astype(BF16), preferred_element_type=F32)


def _mm_nt(a, b):
    return lax.dot_general(a.astype(BF16), b.astype(BF16), (((1,), (1,)), ((), ())),
                           preferred_element_type=F32)


def _mm_tn(a, b):
    return lax.dot_general(a.astype(BF16), b.astype(BF16), (((0,), (0,)), ((), ())),
                           preferred_element_type=F32)


def _ada_kernel(c_ref, w_ref, b_ref, o_ref):
    c = c_ref[...]
    o_ref[...] = _mm(c * jax.nn.sigmoid(c), w_ref[...]) + b_ref[...]


def _ada(cond, w_ada, b_ada):
    rows, d = cond.shape
    n = w_ada.shape[1]
    tn = 1536
    return pl.pallas_call(
        _ada_kernel,
        grid=(n // tn,),
        in_specs=[pl.BlockSpec((rows, d), lambda j: (0, 0)),
                  pl.BlockSpec((d, tn), lambda j: (0, j)),
                  pl.BlockSpec((1, tn), lambda j: (0, j))],
        out_specs=pl.BlockSpec((rows, tn), lambda j: (0, j)),
        out_shape=jax.ShapeDtypeStruct((rows, n), F32),
        compiler_params=_params("arbitrary"),
        name="ada",
    )(cond, w_ada, b_ada.reshape(1, n))


def _rope_tile(x, cos, sin):
    lane = lax.broadcasted_iota(jnp.int32, x.shape, 1)
    partner = jnp.where((lane % 32) < 16, pltpu.roll(x, LANE - 16, 1), pltpu.roll(x, 16, 1))
    return x * cos + partner * sin


def _kv_heads(kv, kr, khw, cos, sin, use_rope):
    krw = kr * khw[:, LANE:]
    if use_rope:
        krw = _rope_tile(krw, cos, sin)
    ssq_r = jnp.sum(kr * kr, axis=-1, keepdims=True)
    ks, vs = [], []
    for hd in range(MLA_HEADS):
        kn = kv[:, hd * HEAD_PAD: hd * HEAD_PAD + LANE]
        r = lax.rsqrt((jnp.sum(kn * kn, axis=-1, keepdims=True) + ssq_r) / MLA_QK + EPS)
        ks += [kn * r * khw[:, :LANE], krw * r]
        vs.append(kv[:, hd * HEAD_PAD + LANE: (hd + 1) * HEAD_PAD])
    return jnp.concatenate(ks, axis=-1).astype(BF16), jnp.concatenate(vs, axis=-1).astype(BF16)


def _inproj_kernel(x_ref, mod_ref, n1w_ref, win_ref, qnw_ref, kvnw_ref, wuq_ref, wukv_ref, qhw_ref, khw_ref,
                   cos_ref, sin_ref, *rest, use_rope):
    ckv_ref, krope_ref, q_ref, k_ref, v_ref, zr_ref = rest[-6:]
    m = mod_ref[0]
    h = _rms(x_ref[...], n1w_ref[...]) * (1.0 + m[1:2]) + m[0:1]
    z = _mm(h, win_ref[...])
    kr = z[:, Q_RANK + KV_RANK: Q_RANK + KV_RANK + LANE]
    zr_ref[...] = z[:, Q_RANK + KV_RANK + LANE:]
    ckvn = _rms(z[:, Q_RANK: Q_RANK + KV_RANK], kvnw_ref[...])
    ckv_ref[...] = ckvn
    krope_ref[...] = kr[:, :MLA_ROPE]
    cos, sin = cos_ref[...], sin_ref[...]

    q = _mm(_rms(z[:, :Q_RANK], qnw_ref[...]), wuq_ref[...])
    qhw = qhw_ref[...]
    qs = []
    for hd in range(MLA_HEADS):
        qn = q[:, hd * HEAD_PAD: hd * HEAD_PAD + LANE]
        qr = q[:, hd * HEAD_PAD + LANE: (hd + 1) * HEAD_PAD]
        ssq = jnp.sum(qn * qn, axis=-1, keepdims=True) + jnp.sum(qr * qr, axis=-1, keepdims=True)
        r = lax.rsqrt(ssq / MLA_QK + EPS)
        qrw = qr * qhw[:, LANE:]
        if use_rope:
            qrw = _rope_tile(qrw, cos, sin)
        qs += [qn * r * qhw[:, :LANE], qrw * r]
    q_ref[...] = jnp.concatenate(qs, axis=-1).astype(BF16)

    k, v = _kv_heads(_mm(ckvn, wukv_ref[...]), kr, khw_ref[...], cos, sin, use_rope)
    k_ref[...] = k
    v_ref[...] = v


def _in_proj(x, mod, w, cos, sin, seq, use_rope, after=None):
    t = x.shape[0]
    tm = ROW_TILE
    per_batch = seq // tm
    const = lambda i: (0, 0)
    row = lambda i: (i, 0)
    if use_rope:
        pos = lambda i: (i % per_batch, 0)
    else:
        pos = const
    hp = MLA_HEADS * HEAD_PAD
    ordering = [] if after is None else [after]
    return pl.pallas_call(
        functools.partial(_inproj_kernel, use_rope=use_rope),
        grid=(t // tm,),
        in_specs=[pl.BlockSpec((tm, D_MODEL), row),
                  pl.BlockSpec((1, 6, D_MODEL), lambda i: (i // per_batch, 0, 0)),
                  pl.BlockSpec((1, D_MODEL), const),
                  pl.BlockSpec((D_MODEL, IN_PAD), const),
                  pl.BlockSpec((1, Q_RANK), const),
                  pl.BlockSpec((1, KV_RANK), const),
                  pl.BlockSpec((Q_RANK, hp), const),
                  pl.BlockSpec((KV_RANK, hp), const),
                  pl.BlockSpec((1, HEAD_PAD), const),
                  pl.BlockSpec((1, HEAD_PAD), const),
                  pl.BlockSpec((tm, LANE), pos),
                  pl.BlockSpec((tm, LANE), pos)] + [pl.BlockSpec(memory_space=pl.ANY)] * len(ordering),
        out_specs=[pl.BlockSpec((tm, KV_RANK), row),
                   pl.BlockSpec((tm, MLA_ROPE), row),
                   pl.BlockSpec((tm, hp), row),
                   pl.BlockSpec((tm, hp), row),
                   pl.BlockSpec((tm, MLA_HEADS * MLA_V), row),
                   pl.BlockSpec((tm, RET_W), row)],
        out_shape=[jax.ShapeDtypeStruct((t, KV_RANK), F32),
                   jax.ShapeDtypeStruct((t, MLA_ROPE), F32),
                   jax.ShapeDtypeStruct((t, hp), BF16),
                   jax.ShapeDtypeStruct((t, hp), BF16),
                   jax.ShapeDtypeStruct((t, MLA_HEADS * MLA_V), BF16),
                   jax.ShapeDtypeStruct((t, RET_W), F32)],
        compiler_params=_params("arbitrary"),
        name="in_proj",
    )(x, mod, w["norm1_w"], w["w_in"], w["q_norm_w"], w["kv_norm_w"], w["w_uq"], w["w_ukv"],
      w["q_head_norm_w"], w["k_head_norm_w"], cos, sin, *ordering)


def _kvup_kernel(ckv_ref, kr_ref, wukv_ref, khw_ref, k_ref, v_ref):
    k, v = _kv_heads(_mm(ckv_ref[...], wukv_ref[...]), kr_ref[...], khw_ref[...], None, None, False)
    k_ref[...] = k
    v_ref[...] = v


def _kv_up(ckv, kr, w):
    t = ckv.shape[0]
    tm = ROW_TILE
    hp = MLA_HEADS * HEAD_PAD
    const = lambda i: (0, 0)
    row = lambda i: (i, 0)
    return pl.pallas_call(
        _kvup_kernel,
        grid=(t // tm,),
        in_specs=[pl.BlockSpec((tm, KV_RANK), row),
                  pl.BlockSpec((tm, LANE), row),
                  pl.BlockSpec((KV_RANK, hp), const),
                  pl.BlockSpec((1, HEAD_PAD), const)],
        out_specs=[pl.BlockSpec((tm, hp), row),
                   pl.BlockSpec((tm, MLA_HEADS * MLA_V), row)],
        out_shape=[jax.ShapeDtypeStruct((t, hp), BF16),
                   jax.ShapeDtypeStruct((t, MLA_HEADS * MLA_V), BF16)],
        compiler_params=_params("arbitrary"),
        name="kv_up",
    )(ckv, kr, w["w_ukv"], w["k_head_norm_w"])


def _attn_kernel(*refs, has_ctx):
    if has_ctx:
        q_ref, k_ref, v_ref, kc_ref, vc_ref, o_ref = refs
    else:
        q_ref, k_ref, v_ref, o_ref = refs
    scale = MLA_QK ** -0.5
    q = q_ref[...]
    s = _mm_nt(q, k_ref[...]) * scale
    m = jnp.max(s, axis=-1, keepdims=True)
    if has_ctx:
        sc = _mm_nt(q, kc_ref[...]) * scale
        m = jnp.maximum(m, jnp.max(sc, axis=-1, keepdims=True))
    p = jnp.exp(s - m)
    den = jnp.sum(p, axis=-1, keepdims=True)
    o = _mm(p, v_ref[...])
    if has_ctx:
        pc = jnp.exp(sc - m)
        den = den + jnp.sum(pc, axis=-1, keepdims=True)
        o = o + _mm(pc, vc_ref[...])
    o_ref[...] = (o / den).astype(BF16)


def _attention(q, k, v, batch, seq, ctx_kv):
    tq = min(Q_TILE, seq)
    nq = seq // tq
    has_ctx = ctx_kv is not None
    qmap = lambda b, h, i: (b * nq + i, h)
    kvmap = lambda b, h, i: (b, h)
    in_specs = [pl.BlockSpec((tq, HEAD_PAD), qmap),
                pl.BlockSpec((seq, HEAD_PAD), kvmap),
                pl.BlockSpec((seq, MLA_V), kvmap)]
    args = [q, k, v]
    if has_ctx:
        kc, vc = ctx_kv
        past = kc.shape[0] // batch
        in_specs += [pl.BlockSpec((past, HEAD_PAD), kvmap), pl.BlockSpec((past, MLA_V), kvmap)]
        args += [kc, vc]
    return pl.pallas_call(
        functools.partial(_attn_kernel, has_ctx=has_ctx),
        grid=(batch, MLA_HEADS, nq),
        in_specs=in_specs,
        out_specs=pl.BlockSpec((tq, MLA_V), qmap),
        out_shape=jax.ShapeDtypeStruct((batch * seq, MLA_HEADS * MLA_V), BF16),
        compiler_params=_params("arbitrary", "arbitrary", "arbitrary"),
        name="attention",
    )(*args)


def _ret_kernel(*refs, has_s0):
    if has_s0:
        (lg_ref, qf_ref, kf_ref, vf_ref, qb_ref, kb_ref, vb_ref, s0f_ref, s0b_ref,
         of_ref, ob_ref, sf_out, sb_out, sf_scr, sb_scr) = refs
    else:
        (lg_ref, qf_ref, kf_ref, vf_ref, qb_ref, kb_ref, vb_ref,
         of_ref, ob_ref, sf_out, sb_out, sf_scr, sb_scr) = refs
    c = pl.program_id(1)
    cs = RET_CHUNK

    @pl.when(c == 0)
    def _():
        if has_s0:
            sf_scr[...] = s0f_ref[0]
            sb_scr[...] = s0b_ref[0]
        else:
            sf_scr[...] = jnp.zeros_like(sf_scr)
            sb_scr[...] = jnp.zeros_like(sb_scr)

    ii = lax.broadcasted_iota(jnp.int32, (cs, cs), 0).astype(F32)
    jj = lax.broadcasted_iota(jnp.int32, (cs, cs), 1).astype(F32)
    rel = ii - jj
    kscale = RET_DK ** -0.5

    def chunk(q, k, v, s, intra, qdec, kdec, cdec):
        a = _mm_nt(q, k) * intra
        o = _mm(a, v) + _mm(q, s) * qdec
        return o, s * cdec + _mm_tn(k * kdec, v)

    for hd in range(RET_HEADS):
        cols = slice(hd * RET_DK, (hd + 1) * RET_DK)
        lgf = jax.nn.log_sigmoid(jnp.full((1, LANE), lg_ref[0, hd], F32))
        lgb = jax.nn.log_sigmoid(jnp.full((1, LANE), lg_ref[1, hd], F32))

        intra_f = jnp.where(rel >= 0, jnp.exp(jnp.maximum(rel, 0.0) * lgf), 0.0)
        o_f, s_f = chunk(qf_ref[:, cols], kf_ref[:, cols] * kscale, vf_ref[:, cols], sf_scr[hd], intra_f,
                         jnp.exp((ii + 1.0) * lgf), jnp.exp((cs - 1.0 - ii) * lgf), jnp.exp(cs * lgf))
        of_ref[:, cols] = o_f
        sf_scr[hd] = s_f
        sf_out[0, hd] = s_f

        intra_b = jnp.where(rel <= 0, jnp.exp(jnp.maximum(-rel, 0.0) * lgb), 0.0)
        o_b, s_b = chunk(qb_ref[:, cols], kb_ref[:, cols] * kscale, vb_ref[:, cols], sb_scr[hd], intra_b,
                         jnp.exp((cs - ii) * lgb), jnp.exp(ii * lgb), jnp.exp(cs * lgb))
        ob_ref[:, cols] = o_b
        sb_scr[hd] = s_b
        sb_out[0, hd] = s_b


def _retention(zr, decay_logit, batch, seq, s0):
    cs = RET_CHUNK
    nc = seq // cs
    nh = RET_HEADS
    width = nh * RET_DK
    has_s0 = s0 is not None

    def fwd(col):
        return pl.BlockSpec((cs, width), lambda b, c: (b * nc + c, col))

    def bwd(col):
        return pl.BlockSpec((cs, width), lambda b, c: (b * nc + nc - 1 - c, col))

    def state(d):
        return pl.BlockSpec((1, nh, RET_DK, RET_DV), lambda b, c: (b * 2 + d, 0, 0, 0))

    state_out = pl.BlockSpec((1, nh, RET_DK, RET_DV), lambda b, c: (b, 0, 0, 0))
    in_specs = [pl.BlockSpec(memory_space=pltpu.SMEM), fwd(0), fwd(1), fwd(2), bwd(0), bwd(1), bwd(2)]
    args = [decay_logit, zr, zr, zr, zr, zr, zr]
    if has_s0:
        in_specs += [state(0), state(1)]
        args += [s0, s0]
    t = batch * seq
    o_f, o_b, s_f, s_b = pl.pallas_call(
        functools.partial(_ret_kernel, has_s0=has_s0),
        grid=(batch, nc),
        in_specs=in_specs,
        out_specs=[pl.BlockSpec((cs, width), lambda b, c: (b * nc + c, 0)),
                   pl.BlockSpec((cs, width), lambda b, c: (b * nc + nc - 1 - c, 0)),
                   state_out, state_out],
        out_shape=[jax.ShapeDtypeStruct((t, width), F32),
                   jax.ShapeDtypeStruct((t, width), F32),
                   jax.ShapeDtypeStruct((batch, nh, RET_DK, RET_DV), F32),
                   jax.ShapeDtypeStruct((batch, nh, RET_DK, RET_DV), F32)],
        scratch_shapes=[pltpu.VMEM((nh, RET_DK, RET_DV), F32), pltpu.VMEM((nh, RET_DK, RET_DV), F32)],
        compiler_params=_params("arbitrary", "arbitrary"),
        name="retention",
    )(*args)
    return o_f, o_b, s_f, s_b


def _mix_kernel(x_ref, att_ref, of_ref, ob_ref, gr_ref, mod_ref, gnw_ref, wout_ref, n2w_ref, x1_ref, h2_ref):
    o = of_ref[...] + ob_ref[...]
    parts = []
    for hd in range(RET_HEADS):
        oh = o[:, hd * RET_DV:(hd + 1) * RET_DV]
        d = oh - jnp.mean(oh, axis=-1, keepdims=True)
        parts.append(d * lax.rsqrt(jnp.mean(d * d, axis=-1, keepdims=True) + EPS))
    g = gr_ref[...]
    ret = (g * jax.nn.sigmoid(g)) * (jnp.concatenate(parts, axis=-1) * gnw_ref[...])
    na = MLA_HEADS * MLA_V
    mixed = _mm(att_ref[...], wout_ref[:na, :]) + _mm(ret, wout_ref[na:, :])
    m = mod_ref[0]
    x1 = x_ref[...] + m[2:3] * mixed
    x1_ref[...] = x1
    h2_ref[...] = _rms(x1, n2w_ref[...]) * (1.0 + m[4:5]) + m[3:4]


def _mix(x, att, o_f, o_b, zr, mod, w, seq):
    t = x.shape[0]
    tm = ROW_TILE
    per_batch = seq // tm
    const = lambda i: (0, 0)
    row = lambda i: (i, 0)
    half = RET_HEADS * RET_DV
    return pl.pallas_call(
        _mix_kernel,
        grid=(t // tm,),
        in_specs=[pl.BlockSpec((tm, D_MODEL), row),
                  pl.BlockSpec((tm, half), row),
                  pl.BlockSpec((tm, half), row),
                  pl.BlockSpec((tm, half), row),
                  pl.BlockSpec((tm, half), lambda i: (i, 3)),
                  pl.BlockSpec((1, 6, D_MODEL), lambda i: (i // per_batch, 0, 0)),
                  pl.BlockSpec((1, half), const),
                  pl.BlockSpec((D_MODEL, D_MODEL), const),
                  pl.BlockSpec((1, D_MODEL), const)],
        out_specs=[pl.BlockSpec((tm, D_MODEL), row), pl.BlockSpec((tm, D_MODEL), row)],
        out_shape=[jax.ShapeDtypeStruct((t, D_MODEL), F32), jax.ShapeDtypeStruct((t, D_MODEL), F32)],
        compiler_params=_params("arbitrary"),
        name="mix",
    )(x, att, o_f, o_b, zr, mod, w["ret_gn_w"], w["w_out"], w["norm2_w"])


def _top16(jobs):
    def body(r, carry):
        for s_ref, vals_ref, pay_ref, payload in jobs:
            n, cols = s_ref.shape
            rows = lax.broadcasted_iota(jnp.int32, (n, cols), 0).astype(F32)
            s = s_ref[...]
            m = jnp.max(s, axis=0, keepdims=True)
            pos = jnp.min(jnp.where(s == m, rows, float(n)), axis=0, keepdims=True)
            hit = rows == pos
            vals_ref[pl.ds(r, 1), :] = m
            if payload is None:
                pay_ref[pl.ds(r, 1), :] = pos
            else:
                pay_ref[pl.ds(r, 1), :] = jnp.sum(jnp.where(hit, payload, 0.0), axis=0, keepdims=True)
            s_ref[...] = jnp.where(hit, -jnp.inf, s)
        return carry

    lax.fori_loop(0, PEER_TOPK, body, 0)


PAIR_COUNTS = tuple(PEER_TOPK // (a + 1) for a in range(PEER_TOPK))
NUM_PAIRS = sum(PAIR_COUNTS)
PAIR_ROWS = -(-NUM_PAIRS // 8) * 8


HEAD_GROUP = 2
SCORE_REFS = 10


def _score_heads(hbt, wpqts, k1, k2, g_outs, e_outs, scr):
    heads = [scr[h * SCORE_REFS:(h + 1) * SCORE_REFS] for h in range(len(wpqts))]
    for wpqt, (s1_scr, s2_scr, *_) in zip(wpqts, heads):
        qt = jnp.dot(wpqt, hbt, preferred_element_type=F32)
        s1_scr[...] = _mm(k1, qt[:PEER_HALF, :])
        s2_scr[...] = _mm(k2, qt[PEER_HALF:, :])
    _top16([job for (s1, s2, _, _, v1, i1, v2, i2, _, _) in heads for job in ((s1, v1, i1, None), (s2, v2, i2, None))])
    for (_, _, c_scr, p_scr, v1_scr, i1_scr, v2_scr, i2_scr, _, _) in heads:
        c_scr[...] = jnp.full(c_scr.shape, -jnp.inf, F32)
        p_scr[...] = jnp.zeros(p_scr.shape, F32)
        off = 0
        for a, nb in enumerate(PAIR_COUNTS):
            c_scr[off:off + nb, :] = v1_scr[a:a + 1, :] + v2_scr[0:nb, :]
            p_scr[off:off + nb, :] = i1_scr[a:a + 1, :] * float(N_KEYS) + i2_scr[0:nb, :]
            off += nb
    _top16([(c_scr, vt_scr, it_scr, p_scr[...]) for (_, _, c_scr, p_scr, _, _, _, _, vt_scr, it_scr) in heads])
    for (*_, vt_scr, it_scr), g_out, e_out in zip(heads, g_outs, e_outs):
        top = vt_scr[...]
        p = jnp.exp(top - jnp.max(top, axis=0, keepdims=True))
        g_out[...] = p / jnp.sum(p, axis=0, keepdims=True)
        e_out[...] = it_scr[...]


def _score_scratch(tb):
    k = PEER_TOPK
    one = ([pltpu.VMEM((N_KEYS, tb), F32)] * 2 + [pltpu.VMEM((PAIR_ROWS, tb), F32)] * 2
           + [pltpu.VMEM((k, tb), F32)] * 6)
    assert len(one) == SCORE_REFS
    return one * HEAD_GROUP


def _score_group(hbt, wpq_ref, k1_ref, k2_ref, g_ref, e_ref, grp, scr):
    wpqts, g_outs, e_outs = [], [], []
    for h in range(HEAD_GROUP):
        hd = grp * HEAD_GROUP + h
        head_rows = pl.ds(pl.multiple_of(hd * PEER_TOPK, PEER_TOPK), PEER_TOPK)
        wpqts.append(wpq_ref[hd])
        g_outs.append(g_ref.at[head_rows])
        e_outs.append(e_ref.at[head_rows])
    _score_heads(hbt, wpqts, k1_ref[...], k2_ref[...], g_outs, e_outs, scr)


def _peer_kernel(tab_hbm, h2_ref, h2n_ref, wpq_ref, k1_ref, k2_ref, x1_ref, mod_ref, *rest, ntiles, aliased):
    y_ref, idx_smem, g_scr, e_scr, ei_scr, hb_scr = rest[int(aliased):int(aliased) + 6]
    scratch = rest[int(aliased) + 6:]
    _peer_body(tab_hbm, h2_ref, h2n_ref, wpq_ref, k1_ref, k2_ref, x1_ref, mod_ref, y_ref,
               idx_smem, g_scr, e_scr, ei_scr, hb_scr, *scratch, ntiles=ntiles)


def _peer_body(tab_hbm, h2_ref, h2n_ref, wpq_ref, k1_ref, k2_ref, x1_ref, mod_ref, y_ref,
               idx_smem, g_scr, e_scr, ei_scr, hb_scr, *scratch, ntiles):
    bufs = scratch[:GATHER_SLOTS]
    sem_rows, sem_idx = scratch[GATHER_SLOTS:GATHER_SLOTS + 2]
    score_scr = scratch[GATHER_SLOTS + 2:]
    i = pl.program_id(0)
    tb = GATHER_TILE
    ns = GATHER_SLOTS
    ahead = GATHER_AHEAD
    ne = EXPERTS_PER_TOKEN
    slab = SLAB_ROWS
    nphase = PEER_HEADS // HEAD_GROUP
    groups_per_phase = tb // (ns * nphase)
    cur = i % 2
    nxt = 1 - cur
    more = i + 1 < ntiles

    def score(src_ref, grp, slot):
        @pl.when(grp == 0)
        def _():
            hb_scr[...] = src_ref[...].T.astype(BF16)
        _score_group(hb_scr[...], wpq_ref, k1_ref, k2_ref, g_scr.at[slot], e_scr, grp, score_scr)

    def publish(slot):
        ei_scr[...] = e_scr[...].T.astype(jnp.int32)
        copy = pltpu.make_async_copy(ei_scr, idx_smem.at[slot], sem_idx.at[0])
        copy.start()
        copy.wait()

    def issue(islot, tok, rslot):
        for kk in range(ne):
            ex = idx_smem[islot, tok, kk]
            pltpu.make_async_copy(tab_hbm.at[ex], bufs[rslot].at[pl.ds(kk * slab, slab), :],
                                  sem_rows.at[rslot]).start(priority=kk % 2)

    def wait_rows(rslot):
        pltpu.make_async_copy(bufs[rslot], bufs[rslot], sem_rows.at[rslot]).wait()

    @pl.when(i == 0)
    def _():
        def first(grp, carry):
            score(h2_ref, grp, 0)
            return carry
        lax.fori_loop(0, nphase, first, 0)
        publish(0)
        for tok in range(ahead):
            issue(0, tok, tok)

    g2 = mod_ref[0][5:6]
    lane = lax.broadcasted_iota(jnp.int32, (ne, tb), 1)

    def compute(j, rslot):
        rows = bufs[rslot]
        x_row = h2_ref[pl.ds(j, 1), :]
        acc = None
        for c in range(slab):
            u_c = _unpack_u(rows[pl.ds(c, ne, stride=slab), :])
            term = u_c * x_row[:, c * LANE:(c + 1) * LANE]
            acc = term if acc is None else acc + term
        pre = jnp.sum(acc, axis=-1, keepdims=True)
        act = 0.5 * pre * (1.0 + lax.erf(pre * (2.0 ** -0.5)))
        gate = jnp.sum(jnp.where(lane == j, g_scr[cur], 0.0), axis=-1, keepdims=True)
        wgt = gate * act
        outs = []
        for c in range(slab):
            v_c = _unpack_v(rows[pl.ds(c, ne, stride=slab), :])
            outs.append(jnp.sum(v_c * wgt, axis=0, keepdims=True))
        out = jnp.concatenate(outs, axis=-1)
        y_ref[pl.ds(j, 1), :] = x1_ref[pl.ds(j, 1), :] + g2 * out

    def phase(grp, carry):
        score(h2n_ref, grp, nxt)

        @pl.when(grp == nphase - 1)
        def _():
            publish(nxt)

        def group(gq, inner):
            g = grp * groups_per_phase + gq
            for u in range(ns):
                j = g * ns + u
                wait_rows(u)
                jj = j + ahead
                over = jj >= tb
                issue(jnp.where(over, nxt, cur), jnp.where(over, jj - tb, jj), (u + ahead) % ns)
                compute(j, u)
            return inner

        lax.fori_loop(0, groups_per_phase, group, 0)
        return carry

    lax.fori_loop(0, nphase, phase, 0)

    @pl.when(jnp.logical_not(more))
    def _():
        for u in range(ahead):
            wait_rows(u)


def _peer(tab, h2, x1, mod, w, seq, tile0, ntiles, y_prev=None):
    t = h2.shape[0]
    tb = GATHER_TILE
    per_batch = seq // tb
    row = lambda i: (tile0 + i, 0)
    const = lambda i: (0, 0)
    in_specs = [pl.BlockSpec(memory_space=pl.ANY),
                pl.BlockSpec((tb, D_MODEL), row),
                pl.BlockSpec((tb, D_MODEL), lambda i: (tile0 + jnp.minimum(i + 1, ntiles - 1), 0)),
                pl.BlockSpec((PEER_HEADS, PEER_QDIM, D_MODEL), lambda i: (0, 0, 0)),
                pl.BlockSpec((N_KEYS, PEER_HALF), const),
                pl.BlockSpec((N_KEYS, PEER_HALF), const),
                pl.BlockSpec((tb, D_MODEL), row),
                pl.BlockSpec((1, 6, D_MODEL), lambda i: ((tile0 + i) // per_batch, 0, 0))]
    args = [tab, h2, h2, w["w_pq"], w["sub_keys1"], w["sub_keys2"], x1, mod]
    aliases = {}
    if y_prev is not None:
        in_specs.append(pl.BlockSpec(memory_space=pl.ANY))
        args.append(y_prev)
        aliases = {len(args) - 1: 0}
    return pl.pallas_call(
        functools.partial(_peer_kernel, ntiles=ntiles, aliased=y_prev is not None),
        grid=(ntiles,),
        in_specs=in_specs,
        input_output_aliases=aliases,
        out_specs=pl.BlockSpec((tb, D_MODEL), row),
        out_shape=jax.ShapeDtypeStruct((t, D_MODEL), F32),
        scratch_shapes=[pltpu.SMEM((2, tb, EXPERTS_PER_TOKEN), jnp.int32),
                        pltpu.VMEM((2, EXPERTS_PER_TOKEN, tb), F32),
                        pltpu.VMEM((EXPERTS_PER_TOKEN, tb), F32),
                        pltpu.VMEM((tb, EXPERTS_PER_TOKEN), jnp.int32),
                        pltpu.VMEM((D_MODEL, tb), BF16)]
                       + [pltpu.VMEM((EXPERTS_PER_TOKEN * SLAB_ROWS, LANE), jnp.uint32)] * GATHER_SLOTS
                       + [pltpu.SemaphoreType.DMA((GATHER_SLOTS,)), pltpu.SemaphoreType.DMA((1,))]
                       + _score_scratch(tb),
        compiler_params=pltpu.CompilerParams(dimension_semantics=("arbitrary",),
                                             vmem_limit_bytes=VMEM_LIMIT,
                                             disable_bounds_checks=True),
        name="peer",
    )(*args)


def _score_kernel(h2_ref, wpq_ref, k1_ref, k2_ref, *rest, ordered):
    g_ref, e_ref, hb_scr = rest[int(ordered):int(ordered) + 3]
    score_scr = rest[int(ordered) + 3:]
    hb_scr[...] = h2_ref[...].T.astype(BF16)

    def group(grp, carry):
        _score_group(hb_scr[...], wpq_ref, k1_ref, k2_ref, g_ref, e_ref, grp, score_scr)
        return carry

    lax.fori_loop(0, PEER_HEADS // HEAD_GROUP, group, 0)


def _peer_score(h2, w, tile0, ntiles, after=None):
    tb = GATHER_TILE
    const = lambda i: (0, 0)
    out = pl.BlockSpec((EXPERTS_PER_TOKEN, tb), lambda i: (0, i))
    ordering = [] if after is None else [after]
    return pl.pallas_call(
        functools.partial(_score_kernel, ordered=after is not None),
        grid=(ntiles,),
        in_specs=[pl.BlockSpec((tb, D_MODEL), lambda i: (tile0 + i, 0)),
                  pl.BlockSpec((PEER_HEADS, PEER_QDIM, D_MODEL), lambda i: (0, 0, 0)),
                  pl.BlockSpec((N_KEYS, PEER_HALF), const),
                  pl.BlockSpec((N_KEYS, PEER_HALF), const)] + [pl.BlockSpec(memory_space=pl.ANY)] * len(ordering),
        out_specs=[out, out],
        out_shape=[jax.ShapeDtypeStruct((EXPERTS_PER_TOKEN, ntiles * tb), F32)] * 2,
        scratch_shapes=[pltpu.VMEM((D_MODEL, tb), BF16)] + _score_scratch(tb),
        compiler_params=_params("arbitrary"),
        name="peer_score",
    )(h2, w["w_pq"], w["sub_keys1"], w["sub_keys2"], *ordering)


def _act_kernel(pre_ref, g_ref, after_ref, w_ref):
    del after_ref
    pre = pre_ref[...]
    w_ref[...] = g_ref[...] * (0.5 * pre * (1.0 + lax.erf(pre * (2.0 ** -0.5))))


def _peer_act(pre, gate, after):
    t, ne = pre.shape
    tm = math.gcd(t, 1024)
    row = lambda i: (i, 0)
    return pl.pallas_call(
        _act_kernel,
        grid=(t // tm,),
        in_specs=[pl.BlockSpec((tm, ne), row), pl.BlockSpec((tm, ne), row), pl.BlockSpec(memory_space=pl.ANY)],
        out_specs=pl.BlockSpec((tm, ne), row),
        out_shape=jax.ShapeDtypeStruct((t, ne), F32),
        compiler_params=_params("arbitrary"),
        name="peer_act",
    )(pre, gate, after)


def _residual_kernel(x1_ref, out_ref, mod_ref, *rest):
    rest[-1][...] = x1_ref[...] + mod_ref[0][5:6] * out_ref[...]


def _peer_residual(x1, out, mod, seq, tile0, ntiles, out_tile0, y_prev):
    tb = GATHER_TILE
    per_batch = seq // tb
    row = lambda i: (tile0 + i, 0)
    in_specs = [pl.BlockSpec((tb, D_MODEL), row),
                pl.BlockSpec((tb, D_MODEL), lambda i: (out_tile0 + i, 0)),
                pl.BlockSpec((1, 6, D_MODEL), lambda i: ((tile0 + i) // per_batch, 0, 0))]
    args = [x1, out, mod]
    aliases = {}
    if y_prev is not None:
        in_specs.append(pl.BlockSpec(memory_space=pl.ANY))
        args.append(y_prev)
        aliases = {3: 0}
    return pl.pallas_call(
        _residual_kernel,
        grid=(ntiles,),
        in_specs=in_specs,
        input_output_aliases=aliases,
        out_specs=pl.BlockSpec((tb, D_MODEL), row),
        out_shape=jax.ShapeDtypeStruct(x1.shape, F32),
        compiler_params=_params("arbitrary"),
        name="peer_residual",
    )(*args)


SC_WORKERS = 32
SC_LANES = 16
SC_CHUNK = 32
SC_SHARE = (29, 64)
TC_FIRST = (13, 28)


def _sc_udot(u_table, idx, x, row0=0):
    t, ne = idx.shape
    assert t % SC_WORKERS == 0 and ne % SC_CHUNK == 0, (t, ne)
    per_w = t // SC_WORKERS
    nchunk = ne // SC_CHUNK
    nl = D_MODEL // SC_LANES
    mesh = plsc.VectorSubcoreMesh(core_axis_name="c", subcore_axis_name="s")

    @functools.partial(
        pl.kernel, out_type=jax.ShapeDtypeStruct((t, ne), F32), mesh=mesh,
        scratch_types=[pltpu.VMEM((ne,), jnp.int32), pltpu.VMEM((D_MODEL,), F32),
                       pltpu.VMEM((2, SC_CHUNK, D_MODEL), F32), pltpu.VMEM((SC_CHUNK, SC_LANES), F32),
                       pltpu.VMEM((ne,), F32), pltpu.SemaphoreType.DMA((2,))],
        compiler_params=pltpu.CompilerParams(needs_layout_passes=False),
        name="sc_udot")
    def run(u_hbm, idx_hbm, x_hbm, pre_hbm, idx_v, x_v, rows, accs, pre_v, sems):
        wid = lax.axis_index("s") * 2 + lax.axis_index("c")
        lanes = lax.iota(jnp.int32, SC_LANES)

        def gather(c, b):
            return pltpu.make_async_copy(u_hbm.at[idx_v.at[pl.ds(c * SC_CHUNK, SC_CHUNK)]], rows.at[b], sems.at[b])

        @pl.loop(0, per_w)
        def _(tt):
            tok = wid * per_w + tt
            pltpu.sync_copy(idx_hbm.at[tok], idx_v)
            pltpu.sync_copy(x_hbm.at[row0 + tok], x_v)
            gather(0, 0).start()
            for c in range(nchunk):
                b = c % 2
                if c + 1 < nchunk:
                    gather(c + 1, 1 - b).start()
                gather(c, b).wait()
                for g in range(SC_CHUNK // 4):
                    def body(j, acc):
                        xv = x_v[pl.ds(j * SC_LANES, SC_LANES)]
                        return tuple(acc[q] + rows[b, g * 4 + q, pl.ds(j * SC_LANES, SC_LANES)] * xv
                                     for q in range(4))
                    acc = plsc.parallel_loop(0, nl, unroll=4,
                                             carry=tuple(jnp.zeros((SC_LANES,), F32) for _ in range(4)))(body)
                    for q in range(4):
                        accs[g * 4 + q, :] = acc[q]
                for part in range(SC_CHUNK // SC_LANES):
                    tot = jnp.zeros((SC_LANES,), F32)
                    for l in range(SC_LANES):
                        tot = tot + plsc.load_gather(accs, [lanes + part * SC_LANES,
                                                            jnp.full((SC_LANES,), l, jnp.int32)])
                    pre_v[pl.ds(c * SC_CHUNK + part * SC_LANES, SC_LANES)] = tot
            pltpu.sync_copy(pre_v, pre_hbm.at[tok])

    return run(u_table, idx, x)


def _sc_vsum(v_table, idx, wgt):
    t, ne = idx.shape
    assert t % SC_WORKERS == 0 and ne % SC_CHUNK == 0, (t, ne)
    per_w = t // SC_WORKERS
    nchunk = ne // SC_CHUNK
    block = 16 * SC_LANES
    mesh = plsc.VectorSubcoreMesh(core_axis_name="c", subcore_axis_name="s")

    @functools.partial(
        pl.kernel, out_type=jax.ShapeDtypeStruct((t, D_MODEL), F32), mesh=mesh,
        scratch_types=[pltpu.VMEM((ne,), jnp.int32), pltpu.VMEM((ne,), F32),
                       pltpu.VMEM((2, SC_CHUNK, D_MODEL), F32), pltpu.VMEM((D_MODEL,), F32),
                       pltpu.SemaphoreType.DMA((2,))],
        compiler_params=pltpu.CompilerParams(needs_layout_passes=False),
        name="sc_vsum")
    def run(v_hbm, idx_hbm, w_hbm, out_hbm, idx_v, w_v, rows, out_v, sems):
        wid = lax.axis_index("s") * 2 + lax.axis_index("c")

        def gather(c, b):
            return pltpu.make_async_copy(v_hbm.at[idx_v.at[pl.ds(c * SC_CHUNK, SC_CHUNK)]], rows.at[b], sems.at[b])

        @pl.loop(0, per_w)
        def _(tt):
            tok = wid * per_w + tt
            pltpu.sync_copy(idx_hbm.at[tok], idx_v)
            pltpu.sync_copy(w_hbm.at[tok], w_v)
            gather(0, 0).start()
            for c in range(nchunk):
                b = c % 2
                if c + 1 < nchunk:
                    gather(c + 1, 1 - b).start()
                gather(c, b).wait()
                for d in range(D_MODEL // block):
                    def body(k, acc):
                        wk = plsc.load_gather(w_v, [jnp.full((SC_LANES,), c * SC_CHUNK, jnp.int32) + k])
                        return tuple(acc[j] + rows[b, k, pl.ds(d * block + j * SC_LANES, SC_LANES)] * wk
                                     for j in range(16))
                    if c == 0:
                        init = tuple(jnp.zeros((SC_LANES,), F32) for _ in range(16))
                    else:
                        init = tuple(out_v[pl.ds(d * block + j * SC_LANES, SC_LANES)] for j in range(16))
                    acc = plsc.parallel_loop(0, SC_CHUNK, carry=init)(body)
                    for j in range(16):
                        out_v[pl.ds(d * block + j * SC_LANES, SC_LANES)] = acc[j]
            pltpu.sync_copy(out_v, out_hbm.at[tok])

    return run(v_table, idx, wgt)


def _expert_slabs(u_table, v_table):
    n = u_table.shape[0]
    hi = lax.bitcast_convert_type(u_table.astype(BF16), jnp.uint16).astype(jnp.uint32) << 16
    lo = lax.bitcast_convert_type(v_table.astype(BF16), jnp.uint16).astype(jnp.uint32)
    return (hi | lo).reshape(n, SLAB_ROWS, LANE)


def _unpack_u(words):
    return lax.bitcast_convert_type(words & jnp.uint32(0xFFFF0000), F32)


def _unpack_v(words):
    return lax.bitcast_convert_type(words << 16, F32)


def _rope_tables(seq):
    rows = seq // GRID_W
    r = jnp.repeat(jnp.arange(rows, dtype=F32), GRID_W)
    col = jnp.tile(jnp.arange(GRID_W, dtype=F32), rows)
    nf = ROPE_AXIS // 2
    freqs = jnp.power(ROPE_BASE, -jnp.arange(nf, dtype=F32) / nf)
    ar, ac = r[:, None] * freqs, col[:, None] * freqs
    pad = jnp.zeros((seq, LANE - MLA_ROPE), F32)
    cos = jnp.concatenate([jnp.cos(ar), jnp.cos(ar), jnp.cos(ac), jnp.cos(ac), pad], axis=-1)
    sin = jnp.concatenate([-jnp.sin(ar), jnp.sin(ar), -jnp.sin(ac), jnp.sin(ac), pad], axis=-1)
    return cos, sin


def _prep_weights(norm1_w, w_in, q_norm_w, w_uq, kv_norm_w, w_ukv, q_head_norm_w, k_head_norm_w,
                  ret_gn_w, w_out, norm2_w, w_pq, sub_keys1, sub_keys2):
    cut = Q_RANK + KV_RANK + MLA_ROPE
    w_in_p = jnp.concatenate([w_in[:, :cut], jnp.zeros((D_MODEL, LANE - MLA_ROPE), F32), w_in[:, cut:]], axis=1)
    wq = w_uq.reshape(Q_RANK, MLA_HEADS, MLA_QK)
    wq = jnp.pad(wq, ((0, 0), (0, 0), (0, HEAD_PAD - MLA_QK))).reshape(Q_RANK, MLA_HEADS * HEAD_PAD)

    def head_w(v):
        return jnp.pad(v, (0, HEAD_PAD - MLA_QK)).reshape(1, HEAD_PAD)

    return {
        "norm1_w": norm1_w.reshape(1, -1), "w_in": w_in_p.astype(BF16),
        "q_norm_w": q_norm_w.reshape(1, -1), "kv_norm_w": kv_norm_w.reshape(1, -1),
        "w_uq": wq.astype(BF16), "w_ukv": w_ukv.astype(BF16),
        "q_head_norm_w": head_w(q_head_norm_w), "k_head_norm_w": head_w(k_head_norm_w),
        "ret_gn_w": ret_gn_w.reshape(1, -1), "w_out": w_out.astype(BF16), "norm2_w": norm2_w.reshape(1, -1),
        "w_pq": w_pq.astype(BF16).reshape(D_MODEL, PEER_HEADS, PEER_QDIM).transpose(1, 2, 0), "sub_keys1": sub_keys1.astype(BF16), "sub_keys2": sub_keys2.astype(BF16),
    }


def _trunk(x, mod, w, decay_logit, ctx, after=None):
    batch, seq, _ = x.shape
    x2 = x.reshape(batch * seq, D_MODEL)
    use_rope = ctx is not None
    if use_rope:
        cos, sin = _rope_tables(seq)
    else:
        cos = sin = jnp.zeros((ROW_TILE, LANE), F32)
    ckv, krope, q, k, v, zr = _in_proj(x2, mod, w, cos, sin, seq, use_rope, after)
    if use_rope:
        ckv_c, krope_c, state_c = ctx
        past = ckv_c.shape[1]
        kr_c = jnp.pad(krope_c.reshape(batch * past, MLA_ROPE), ((0, 0), (0, LANE - MLA_ROPE)))
        ctx_kv = _kv_up(ckv_c.reshape(batch * past, KV_RANK), kr_c, w)
        s0 = state_c.reshape(batch * 2, RET_HEADS, RET_DK, RET_DV)
    else:
        ctx_kv, s0 = None, None
    att = _attention(q, k, v, batch, seq, ctx_kv)
    o_f, o_b, s_f, s_b = _retention(zr, decay_logit, batch, seq, s0)
    x1, h2 = _mix(x2, att, o_f, o_b, zr, mod, w, seq)
    states = jnp.stack([s_f, s_b], axis=1)
    return x1, h2, ckv.reshape(batch, seq, KV_RANK), krope.reshape(batch, seq, MLA_ROPE), states


def _sc_select(h2, w, tile0, ntiles, after=None):
    gt, et = _peer_score(h2, w, tile0, ntiles, after)
    return et.T.astype(jnp.int32), gt.T


def _peer_split(ctx_sc, x_ctx, x_lat, mod_ctx, mod_lat, seq_ctx, seq_lat, w, tab):
    (x1c, h2c), (x1l, h2l) = x_ctx, x_lat
    tiles_c = h2c.shape[0] // GATHER_TILE
    tiles_l = h2l.shape[0] // GATHER_TILE
    sc_tiles = tiles_l * SC_SHARE[0] // SC_SHARE[1]
    tc_tiles = tiles_l - sc_tiles
    first = tc_tiles * TC_FIRST[0] // TC_FIRST[1]

    idx_c, gate_c, pre_c = ctx_sc
    wgt_c = _peer_act(pre_c, gate_c, h2l)
    idx_l, gate_l = _sc_select(h2l, w, tc_tiles, sc_tiles, wgt_c)
    pre_l = _sc_udot(w["u_table"], idx_l, h2l, tc_tiles * GATHER_TILE)
    y = _peer(tab, h2l, x1l, mod_lat, w, seq_lat, 0, first)
    idx = jnp.concatenate([idx_c, idx_l], axis=0)
    wgt = jnp.concatenate([wgt_c, _peer_act(pre_l, gate_l, y)], axis=0)
    out = _sc_vsum(w["v_table"], idx, wgt)
    y = _peer(tab, h2l, x1l, mod_lat, w, seq_lat, first, tc_tiles - first, y_prev=y)
    y_lat = _peer_residual(x1l, out, mod_lat, seq_lat, tc_tiles, sc_tiles, tiles_c, y)
    y_ctx = _peer_residual(x1c, out, mod_ctx, seq_ctx, 0, tiles_c, 0, None)
    return y_ctx, y_lat


def kernel(x_prompt, x_sample, c, cache_ckv, cache_krope, state_ret, c_ctx, w_ada, b_ada, norm1_w, w_in,
           q_norm_w, w_uq, kv_norm_w, w_ukv, q_head_norm_w, k_head_norm_w, ret_decay_logit, ret_gn_w,
           w_out, norm2_w, w_pq, sub_keys1, sub_keys2, u_table, v_table):
    depth = w_ada.shape[0]
    nb_ctx = x_prompt.shape[0]
    nb_lat = x_sample.shape[0]
    y_prompt, y_sample = x_prompt, x_sample
    ckv_list, krope_list, ret_list = [], [], []
    for l in range(depth):
        cond_rows = -(-(nb_lat + 1) // 8) * 8
        cond = jnp.concatenate([c, c_ctx[None, :], jnp.zeros((cond_rows - nb_lat - 1, D_MODEL), F32)], axis=0)
        mod = _ada(cond, w_ada[l], b_ada[l])
        mod_lat = mod[:nb_lat].reshape(nb_lat, 6, D_MODEL)
        mod_ctx = jnp.broadcast_to(mod[nb_lat].reshape(1, 6, D_MODEL), (nb_ctx, 6, D_MODEL))
        w = _prep_weights(norm1_w[l], w_in[l], q_norm_w[l], w_uq[l], kv_norm_w[l], w_ukv[l], q_head_norm_w[l],
                          k_head_norm_w[l], ret_gn_w[l], w_out[l], norm2_w[l], w_pq[l], sub_keys1[l], sub_keys2[l])
        tab = _expert_slabs(u_table[l], v_table[l])
        w["u_table"], w["v_table"] = u_table[l], v_table[l]
        x1c, h2c, ckv_l, krope_l, ret_l = _trunk(y_prompt, mod_ctx, w, ret_decay_logit[l], None)
        ckv_list.append(ckv_l)
        krope_list.append(krope_l)
        ret_list.append(ret_l)
        idx_c, gate_c = _sc_select(h2c, w, 0, h2c.shape[0] // GATHER_TILE)
        pre_c = _sc_udot(w["u_table"], idx_c, h2c)
        x1l, h2l, _, _, _ = _trunk(y_sample, mod_lat, w, ret_decay_logit[l],
                                   (cache_ckv[:, l], cache_krope[:, l], state_ret[:, l]), after=idx_c)
        y_ctx, y_lat = _peer_split((idx_c, gate_c, pre_c), (x1c, h2c), (x1l, h2l), mod_ctx, mod_lat,
                                   y_prompt.shape[1], y_sample.shape[1], w, tab)
        y_prompt = y_ctx.reshape(y_prompt.shape)
        y_sample = y_lat.reshape(y_sample.shape)
    return (y_prompt, y_sample, jnp.stack(ckv_list, axis=1), jnp.stack(krope_list, axis=1),
            jnp.stack(ret_list, axis=1))
```

```python
import functools
import math

import jax
import jax.numpy as jnp
from jax import lax
from jax.experimental import pallas as pl
from jax.experimental.pallas import tpu as pltpu
from jax.experimental.pallas import tpu_sc as plsc

F32 = jnp.float32
BF16 = jnp.bfloat16

D_MODEL = 1024
GRID_W = 64
MLA_HEADS = 4
MLA_NOPE = 128
MLA_ROPE = 64
MLA_QK = MLA_NOPE + MLA_ROPE
MLA_V = 128
Q_RANK = 512
KV_RANK = 256
ROPE_AXIS = MLA_ROPE // 2
ROPE_BASE = 10000.0
RET_HEADS = 4
RET_DK = 128
RET_DV = 128
RET_CHUNK = 128
PEER_HEADS = 8
PEER_QDIM = 256
PEER_HALF = PEER_QDIM // 2
N_KEYS = 128
PEER_TOPK = 16
EPS = 1e-6

LANE = 128
HEAD_PAD = 2 * LANE
RET_W = 4 * RET_HEADS * RET_DK
IN_PAD = Q_RANK + KV_RANK + LANE + RET_W
VMEM_LIMIT = 48 * 1024 * 1024

ROW_TILE = 256
Q_TILE = 256
GATHER_TILE = 128
GATHER_SLOTS = 4
GATHER_AHEAD = GATHER_SLOTS - 1
EXPERTS_PER_TOKEN = PEER_HEADS * PEER_TOPK
SLAB_ROWS = D_MODEL // LANE


def _params(*sem):
    return pltpu.CompilerParams(dimension_semantics=sem, vmem_limit_bytes=VMEM_LIMIT)


def _rms(x, w):
    return x * lax.rsqrt(jnp.mean(x * x, axis=-1, keepdims=True) + EPS) * w


def _mm(a, b):
    return jnp.dot(a.astype(BF16), b.astype(BF16), preferred_element_type=F32)


def _mm_nt(a, b):
    return lax.dot_general(a.astype(BF16), b.astype(BF16), (((1,), (1,)), ((), ())),
                           preferred_element_type=F32)


def _mm_tn(a, b):
    return lax.dot_general(a.astype(BF16), b.astype(BF16), (((0,), (0,)), ((), ())),
                           preferred_element_type=F32)


def _ada_kernel(c_ref, w_ref, b_ref, o_ref):
    c = c_ref[...]
    o_ref[...] = _mm(c * jax.nn.sigmoid(c), w_ref[...]) + b_ref[...]


def _ada(cond, w_ada, b_ada):
    rows, d = cond.shape
    n = w_ada.shape[1]
    tn = 1536
    return pl.pallas_call(
        _ada_kernel,
        grid=(n // tn,),
        in_specs=[pl.BlockSpec((rows, d), lambda j: (0, 0)),
                  pl.BlockSpec((d, tn), lambda j: (0, j)),
                  pl.BlockSpec((1, tn), lambda j: (0, j))],
        out_specs=pl.BlockSpec((rows, tn), lambda j: (0, j)),
        out_shape=jax.ShapeDtypeStruct((rows, n), F32),
        compiler_params=_params("arbitrary"),
        name="ada",
    )(cond, w_ada, b_ada.reshape(1, n))


def _rope_tile(x, cos, sin):
    lane = lax.broadcasted_iota(jnp.int32, x.shape, 1)
    partner = jnp.where((lane % 32) < 16, pltpu.roll(x, LANE - 16, 1), pltpu.roll(x, 16, 1))
    return x * cos + partner * sin


def _kv_heads(kv, kr, khw, cos, sin, use_rope):
    krw = kr * khw[:, LANE:]
    if use_rope:
        krw = _rope_tile(krw, cos, sin)
    ssq_r = jnp.sum(kr * kr, axis=-1, keepdims=True)
    ks, vs = [], []
    for hd in range(MLA_HEADS):
        kn = kv[:, hd * HEAD_PAD: hd * HEAD_PAD + LANE]
        r = lax.rsqrt((jnp.sum(kn * kn, axis=-1, keepdims=True) + ssq_r) / MLA_QK + EPS)
        ks += [kn * r * khw[:, :LANE], krw * r]
        vs.append(kv[:, hd * HEAD_PAD + LANE: (hd + 1) * HEAD_PAD])
    return jnp.concatenate(ks, axis=-1).astype(BF16), jnp.concatenate(vs, axis=-1).astype(BF16)


def _inproj_kernel(x_ref, mod_ref, n1w_ref, win_ref, qnw_ref, kvnw_ref, wuq_ref, wukv_ref, qhw_ref, khw_ref,
                   cos_ref, sin_ref, *rest, use_rope):
    ckv_ref, krope_ref, q_ref, k_ref, v_ref, zr_ref = rest[-6:]
    m = mod_ref[0]
    h = _rms(x_ref[...], n1w_ref[...]) * (1.0 + m[1:2]) + m[0:1]
    z = _mm(h, win_ref[...])
    kr = z[:, Q_RANK + KV_RANK: Q_RANK + KV_RANK + LANE]
    zr_ref[...] = z[:, Q_RANK + KV_RANK + LANE:]
    ckvn = _rms(z[:, Q_RANK: Q_RANK + KV_RANK], kvnw_ref[...])
    ckv_ref[...] = ckvn
    krope_ref[...] = kr[:, :MLA_ROPE]
    cos, sin = cos_ref[...], sin_ref[...]

    q = _mm(_rms(z[:, :Q_RANK], qnw_ref[...]), wuq_ref[...])
    qhw = qhw_ref[...]
    qs = []
    for hd in range(MLA_HEADS):
        qn = q[:, hd * HEAD_PAD: hd * HEAD_PAD + LANE]
        qr = q[:, hd * HEAD_PAD + LANE: (hd + 1) * HEAD_PAD]
        ssq = jnp.sum(qn * qn, axis=-1, keepdims=True) + jnp.sum(qr * qr, axis=-1, keepdims=True)
        r = lax.rsqrt(ssq / MLA_QK + EPS)
        qrw = qr * qhw[:, LANE:]
        if use_rope:
            qrw = _rope_tile(qrw, cos, sin)
        qs += [qn * r * qhw[:, :LANE], qrw * r]
    q_ref[...] = jnp.concatenate(qs, axis=-1).astype(BF16)

    k, v = _kv_heads(_mm(ckvn, wukv_ref[...]), kr, khw_ref[...], cos, sin, use_rope)
    k_ref[...] = k
    v_ref[...] = v


def _in_proj(x, mod, w, cos, sin, seq, use_rope, after=None):
    t = x.shape[0]
    tm = ROW_TILE
    per_batch = seq // tm
    const = lambda i: (0, 0)
    row = lambda i: (i, 0)
    if use_rope:
        pos = lambda i: (i % per_batch, 0)
    else:
        pos = const
    hp = MLA_HEADS * HEAD_PAD
    ordering = [] if after is None else [after]
    return pl.pallas_call(
        functools.partial(_inproj_kernel, use_rope=use_rope),
        grid=(t // tm,),
        in_specs=[pl.BlockSpec((tm, D_MODEL), row),
                  pl.BlockSpec((1, 6, D_MODEL), lambda i: (i // per_batch, 0, 0)),
                  pl.BlockSpec((1, D_MODEL), const),
                  pl.BlockSpec((D_MODEL, IN_PAD), const),
                  pl.BlockSpec((1, Q_RANK), const),
                  pl.BlockSpec((1, KV_RANK), const),
                  pl.BlockSpec((Q_RANK, hp), const),
                  pl.BlockSpec((KV_RANK, hp), const),
                  pl.BlockSpec((1, HEAD_PAD), const),
                  pl.BlockSpec((1, HEAD_PAD), const),
                  pl.BlockSpec((tm, LANE), pos),
                  pl.BlockSpec((tm, LANE), pos)] + [pl.BlockSpec(memory_space=pl.ANY)] * len(ordering),
        out_specs=[pl.BlockSpec((tm, KV_RANK), row),
                   pl.BlockSpec((tm, MLA_ROPE), row),
                   pl.BlockSpec((tm, hp), row),
                   pl.BlockSpec((tm, hp), row),
                   pl.BlockSpec((tm, MLA_HEADS * MLA_V), row),
                   pl.BlockSpec((tm, RET_W), row)],
        out_shape=[jax.ShapeDtypeStruct((t, KV_RANK), F32),
                   jax.ShapeDtypeStruct((t, MLA_ROPE), F32),
                   jax.ShapeDtypeStruct((t, hp), BF16),
                   jax.ShapeDtypeStruct((t, hp), BF16),
                   jax.ShapeDtypeStruct((t, MLA_HEADS * MLA_V), BF16),
                   jax.ShapeDtypeStruct((t, RET_W), F32)],
        compiler_params=_params("arbitrary"),
        name="in_proj",
    )(x, mod, w["norm1_w"], w["w_in"], w["q_norm_w"], w["kv_norm_w"], w["w_uq"], w["w_ukv"],
      w["q_head_norm_w"], w["k_head_norm_w"], cos, sin, *ordering)


def _kvup_kernel(ckv_ref, kr_ref, wukv_ref, khw_ref, k_ref, v_ref):
    k, v = _kv_heads(_mm(ckv_ref[...], wukv_ref[...]), kr_ref[...], khw_ref[...], None, None, False)
    k_ref[...] = k
    v_ref[...] = v


def _kv_up(ckv, kr, w):
    t = ckv.shape[0]
    tm = ROW_TILE
    hp = MLA_HEADS * HEAD_PAD
    const = lambda i: (0, 0)
    row = lambda i: (i, 0)
    return pl.pallas_call(
        _kvup_kernel,
        grid=(t // tm,),
        in_specs=[pl.BlockSpec((tm, KV_RANK), row),
                  pl.BlockSpec((tm, LANE), row),
                  pl.BlockSpec((KV_RANK, hp), const),
                  pl.BlockSpec((1, HEAD_PAD), const)],
        out_specs=[pl.BlockSpec((tm, hp), row),
                   pl.BlockSpec((tm, MLA_HEADS * MLA_V), row)],
        out_shape=[jax.ShapeDtypeStruct((t, hp), BF16),
                   jax.ShapeDtypeStruct((t, MLA_HEADS * MLA_V), BF16)],
        compiler_params=_params("arbitrary"),
        name="kv_up",
    )(ckv, kr, w["w_ukv"], w["k_head_norm_w"])


def _attn_kernel(*refs, has_ctx):
    if has_ctx:
        q_ref, k_ref, v_ref, kc_ref, vc_ref, o_ref = refs
    else:
        q_ref, k_ref, v_ref, o_ref = refs
    scale = MLA_QK ** -0.5
    q = q_ref[...]
    s = _mm_nt(q, k_ref[...]) * scale
    m = jnp.max(s, axis=-1, keepdims=True)
    if has_ctx:
        sc = _mm_nt(q, kc_ref[...]) * scale
        m = jnp.maximum(m, jnp.max(sc, axis=-1, keepdims=True))
    p = jnp.exp(s - m)
    den = jnp.sum(p, axis=-1, keepdims=True)
    o = _mm(p, v_ref[...])
    if has_ctx:
        pc = jnp.exp(sc - m)
        den = den + jnp.sum(pc, axis=-1, keepdims=True)
        o = o + _mm(pc, vc_ref[...])
    o_ref[...] = (o / den).astype(BF16)


def _attention(q, k, v, batch, seq, ctx_kv):
    tq = min(Q_TILE, seq)
    nq = seq // tq
    has_ctx = ctx_kv is not None
    qmap = lambda b, h, i: (b * nq + i, h)
    kvmap = lambda b, h, i: (b, h)
    in_specs = [pl.BlockSpec((tq, HEAD_PAD), qmap),
                pl.BlockSpec((seq, HEAD_PAD), kvmap),
                pl.BlockSpec((seq, MLA_V), kvmap)]
    args = [q, k, v]
    if has_ctx:
        kc, vc = ctx_kv
        past = kc.shape[0] // batch
        in_specs += [pl.BlockSpec((past, HEAD_PAD), kvmap), pl.BlockSpec((past, MLA_V), kvmap)]
        args += [kc, vc]
    return pl.pallas_call(
        functools.partial(_attn_kernel, has_ctx=has_ctx),
        grid=(batch, MLA_HEADS, nq),
        in_specs=in_specs,
        out_specs=pl.BlockSpec((tq, MLA_V), qmap),
        out_shape=jax.ShapeDtypeStruct((batch * seq, MLA_HEADS * MLA_V), BF16),
        compiler_params=_params("arbitrary", "arbitrary", "arbitrary"),
        name="attention",
    )(*args)


def _ret_kernel(*refs, has_s0):
    if has_s0:
        (lg_ref, qf_ref, kf_ref, vf_ref, qb_ref, kb_ref, vb_ref, s0f_ref, s0b_ref,
         of_ref, ob_ref, sf_out, sb_out, sf_scr, sb_scr) = refs
    else:
        (lg_ref, qf_ref, kf_ref, vf_ref, qb_ref, kb_ref, vb_ref,
         of_ref, ob_ref, sf_out, sb_out, sf_scr, sb_scr) = refs
    c = pl.program_id(1)
    cs = RET_CHUNK

    @pl.when(c == 0)
    def _():
        if has_s0:
            sf_scr[...] = s0f_ref[0]
            sb_scr[...] = s0b_ref[0]
        else:
            sf_scr[...] = jnp.zeros_like(sf_scr)
            sb_scr[...] = jnp.zeros_like(sb_scr)

    ii = lax.broadcasted_iota(jnp.int32, (cs, cs), 0).astype(F32)
    jj = lax.broadcasted_iota(jnp.int32, (cs, cs), 1).astype(F32)
    rel = ii - jj
    kscale = RET_DK ** -0.5

    def chunk(q, k, v, s, intra, qdec, kdec, cdec):
        a = _mm_nt(q, k) * intra
        o = _mm(a, v) + _mm(q, s) * qdec
        return o, s * cdec + _mm_tn(k * kdec, v)

    for hd in range(RET_HEADS):
        cols = slice(hd * RET_DK, (hd + 1) * RET_DK)
        lgf = jax.nn.log_sigmoid(jnp.full((1, LANE), lg_ref[0, hd], F32))
        lgb = jax.nn.log_sigmoid(jnp.full((1, LANE), lg_ref[1, hd], F32))

        intra_f = jnp.where(rel >= 0, jnp.exp(jnp.maximum(rel, 0.0) * lgf), 0.0)
        o_f, s_f = chunk(qf_ref[:, cols], kf_ref[:, cols] * kscale, vf_ref[:, cols], sf_scr[hd], intra_f,
                         jnp.exp((ii + 1.0) * lgf), jnp.exp((cs - 1.0 - ii) * lgf), jnp.exp(cs * lgf))
        of_ref[:, cols] = o_f
        sf_scr[hd] = s_f
        sf_out[0, hd] = s_f

        intra_b = jnp.where(rel <= 0, jnp.exp(jnp.maximum(-rel, 0.0) * lgb), 0.0)
        o_b, s_b = chunk(qb_ref[:, cols], kb_ref[:, cols] * kscale, vb_ref[:, cols], sb_scr[hd], intra_b,
                         jnp.exp((cs - ii) * lgb), jnp.exp(ii * lgb), jnp.exp(cs * lgb))
        ob_ref[:, cols] = o_b
        sb_scr[hd] = s_b
        sb_out[0, hd] = s_b


def _retention(zr, decay_logit, batch, seq, s0):
    cs = RET_CHUNK
    nc = seq // cs
    nh = RET_HEADS
    width = nh * RET_DK
    has_s0 = s0 is not None

    def fwd(col):
        return pl.BlockSpec((cs, width), lambda b, c: (b * nc + c, col))

    def bwd(col):
        return pl.BlockSpec((cs, width), lambda b, c: (b * nc + nc - 1 - c, col))

    def state(d):
        return pl.BlockSpec((1, nh, RET_DK, RET_DV), lambda b, c: (b * 2 + d, 0, 0, 0))

    state_out = pl.BlockSpec((1, nh, RET_DK, RET_DV), lambda b, c: (b, 0, 0, 0))
    in_specs = [pl.BlockSpec(memory_space=pltpu.SMEM), fwd(0), fwd(1), fwd(2), bwd(0), bwd(1), bwd(2)]
    args = [decay_logit, zr, zr, zr, zr, zr, zr]
    if has_s0:
        in_specs += [state(0), state(1)]
        args += [s0, s0]
    t = batch * seq
    o_f, o_b, s_f, s_b = pl.pallas_call(
        functools.partial(_ret_kernel, has_s0=has_s0),
        grid=(batch, nc),
        in_specs=in_specs,
        out_specs=[pl.BlockSpec((cs, width), lambda b, c: (b * nc + c, 0)),
                   pl.BlockSpec((cs, width), lambda b, c: (b * nc + nc - 1 - c, 0)),
                   state_out, state_out],
        out_shape=[jax.ShapeDtypeStruct((t, width), F32),
                   jax.ShapeDtypeStruct((t, width), F32),
                   jax.ShapeDtypeStruct((batch, nh, RET_DK, RET_DV), F32),
                   jax.ShapeDtypeStruct((batch, nh, RET_DK, RET_DV), F32)],
        scratch_shapes=[pltpu.VMEM((nh, RET_DK, RET_DV), F32), pltpu.VMEM((nh, RET_DK, RET_DV), F32)],
        compiler_params=_params("arbitrary", "arbitrary"),
        name="retention",
    )(*args)
    return o_f, o_b, s_f, s_b


def _mix_kernel(x_ref, att_ref, of_ref, ob_ref, gr_ref, mod_ref, gnw_ref, wout_ref, n2w_ref, x1_ref, h2_ref):
    o = of_ref[...] + ob_ref[...]
    parts = []
    for hd in range(RET_HEADS):
        oh = o[:, hd * RET_DV:(hd + 1) * RET_DV]
        d = oh - jnp.mean(oh, axis=-1, keepdims=True)
        parts.append(d * lax.rsqrt(jnp.mean(d * d, axis=-1, keepdims=True) + EPS))
    g = gr_ref[...]
    ret = (g * jax.nn.sigmoid(g)) * (jnp.concatenate(parts, axis=-1) * gnw_ref[...])
    na = MLA_HEADS * MLA_V
    mixed = _mm(att_ref[...], wout_ref[:na, :]) + _mm(ret, wout_ref[na:, :])
    m = mod_ref[0]
    x1 = x_ref[...] + m[2:3] * mixed
    x1_ref[...] = x1
    h2_ref[...] = _rms(x1, n2w_ref[...]) * (1.0 + m[4:5]) + m[3:4]


def _mix(x, att, o_f, o_b, zr, mod, w, seq):
    t = x.shape[0]
    tm = ROW_TILE
    per_batch = seq // tm
    const = lambda i: (0, 0)
    row = lambda i: (i, 0)
    half = RET_HEADS * RET_DV
    return pl.pallas_call(
        _mix_kernel,
        grid=(t // tm,),
        in_specs=[pl.BlockSpec((tm, D_MODEL), row),
                  pl.BlockSpec((tm, half), row),
                  pl.BlockSpec((tm, half), row),
                  pl.BlockSpec((tm, half), row),
                  pl.BlockSpec((tm, half), lambda i: (i, 3)),
                  pl.BlockSpec((1, 6, D_MODEL), lambda i: (i // per_batch, 0, 0)),
                  pl.BlockSpec((1, half), const),
                  pl.BlockSpec((D_MODEL, D_MODEL), const),
                  pl.BlockSpec((1, D_MODEL), const)],
        out_specs=[pl.BlockSpec((tm, D_MODEL), row), pl.BlockSpec((tm, D_MODEL), row)],
        out_shape=[jax.ShapeDtypeStruct((t, D_MODEL), F32), jax.ShapeDtypeStruct((t, D_MODEL), F32)],
        compiler_params=_params("arbitrary"),
        name="mix",
    )(x, att, o_f, o_b, zr, mod, w["ret_gn_w"], w["w_out"], w["norm2_w"])


def _top16(jobs):
    def body(r, carry):
        for s_ref, vals_ref, pay_ref, payload in jobs:
            n, cols = s_ref.shape
            rows = lax.broadcasted_iota(jnp.int32, (n, cols), 0).astype(F32)
            s = s_ref[...]
            m = jnp.max(s, axis=0, keepdims=True)
            pos = jnp.min(jnp.where(s == m, rows, float(n)), axis=0, keepdims=True)
            hit = rows == pos
            vals_ref[pl.ds(r, 1), :] = m
            if payload is None:
                pay_ref[pl.ds(r, 1), :] = pos
            else:
                pay_ref[pl.ds(r, 1), :] = jnp.sum(jnp.where(hit, payload, 0.0), axis=0, keepdims=True)
            s_ref[...] = jnp.where(hit, -jnp.inf, s)
        return carry

    lax.fori_loop(0, PEER_TOPK, body, 0)


PAIR_COUNTS = tuple(PEER_TOPK // (a + 1) for a in range(PEER_TOPK))
NUM_PAIRS = sum(PAIR_COUNTS)
PAIR_ROWS = -(-NUM_PAIRS // 8) * 8


HEAD_GROUP = 2
SCORE_REFS = 10


def _score_heads(hbt, wpqts, k1, k2, g_outs, e_outs, scr):
    heads = [scr[h * SCORE_REFS:(h + 1) * SCORE_REFS] for h in range(len(wpqts))]
    for wpqt, (s1_scr, s2_scr, *_) in zip(wpqts, heads):
        qt = jnp.dot(wpqt, hbt, preferred_element_type=F32)
        s1_scr[...] = _mm(k1, qt[:PEER_HALF, :])
        s2_scr[...] = _mm(k2, qt[PEER_HALF:, :])
    _top16([job for (s1, s2, _, _, v1, i1, v2, i2, _, _) in heads for job in ((s1, v1, i1, None), (s2, v2, i2, None))])
    for (_, _, c_scr, p_scr, v1_scr, i1_scr, v2_scr, i2_scr, _, _) in heads:
        c_scr[...] = jnp.full(c_scr.shape, -jnp.inf, F32)
        p_scr[...] = jnp.zeros(p_scr.shape, F32)
        off = 0
        for a, nb in enumerate(PAIR_COUNTS):
            c_scr[off:off + nb, :] = v1_scr[a:a + 1, :] + v2_scr[0:nb, :]
            p_scr[off:off + nb, :] = i1_scr[a:a + 1, :] * float(N_KEYS) + i2_scr[0:nb, :]
            off += nb
    _top16([(c_scr, vt_scr, it_scr, p_scr[...]) for (_, _, c_scr, p_scr, _, _, _, _, vt_scr, it_scr) in heads])
    for (*_, vt_scr, it_scr), g_out, e_out in zip(heads, g_outs, e_outs):
        top = vt_scr[...]
        p = jnp.exp(top - jnp.max(top, axis=0, keepdims=True))
        g_out[...] = p / jnp.sum(p, axis=0, keepdims=True)
        e_out[...] = it_scr[...]


def _score_scratch(tb):
    k = PEER_TOPK
    one = ([pltpu.VMEM((N_KEYS, tb), F32)] * 2 + [pltpu.VMEM((PAIR_ROWS, tb), F32)] * 2
           + [pltpu.VMEM((k, tb), F32)] * 6)
    assert len(one) == SCORE_REFS
    return one * HEAD_GROUP


def _score_group(hbt, wpq_ref, k1_ref, k2_ref, g_ref, e_ref, grp, scr):
    wpqts, g_outs, e_outs = [], [], []
    for h in range(HEAD_GROUP):
        hd = grp * HEAD_GROUP + h
        head_rows = pl.ds(pl.multiple_of(hd * PEER_TOPK, PEER_TOPK), PEER_TOPK)
        wpqts.append(wpq_ref[hd])
        g_outs.append(g_ref.at[head_rows])
        e_outs.append(e_ref.at[head_rows])
    _score_heads(hbt, wpqts, k1_ref[...], k2_ref[...], g_outs, e_outs, scr)


def _peer_kernel(tab_hbm, h2_ref, h2n_ref, wpq_ref, k1_ref, k2_ref, x1_ref, mod_ref, *rest, ntiles, aliased):
    y_ref, idx_smem, g_scr, e_scr, ei_scr, hb_scr = rest[int(aliased):int(aliased) + 6]
    scratch = rest[int(aliased) + 6:]
    _peer_body(tab_hbm, h2_ref, h2n_ref, wpq_ref, k1_ref, k2_ref, x1_ref, mod_ref, y_ref,
               idx_smem, g_scr, e_scr, ei_scr, hb_scr, *scratch, ntiles=ntiles)


def _peer_body(tab_hbm, h2_ref, h2n_ref, wpq_ref, k1_ref, k2_ref, x1_ref, mod_ref, y_ref,
               idx_smem, g_scr, e_scr, ei_scr, hb_scr, *scratch, ntiles):
    bufs = scratch[:GATHER_SLOTS]
    sem_rows, sem_idx = scratch[GATHER_SLOTS:GATHER_SLOTS + 2]
    score_scr = scratch[GATHER_SLOTS + 2:]
    i = pl.program_id(0)
    tb = GATHER_TILE
    ns = GATHER_SLOTS
    ahead = GATHER_AHEAD
    ne = EXPERTS_PER_TOKEN
    slab = SLAB_ROWS
    nphase = PEER_HEADS // HEAD_GROUP
    groups_per_phase = tb // (ns * nphase)
    cur = i % 2
    nxt = 1 - cur
    more = i + 1 < ntiles

    def score(src_ref, grp, slot):
        @pl.when(grp == 0)
        def _():
            hb_scr[...] = src_ref[...].T.astype(BF16)
        _score_group(hb_scr[...], wpq_ref, k1_ref, k2_ref, g_scr.at[slot], e_scr, grp, score_scr)

    def publish(slot):
        ei_scr[...] = e_scr[...].T.astype(jnp.int32)
        copy = pltpu.make_async_copy(ei_scr, idx_smem.at[slot], sem_idx.at[0])
        copy.start()
        copy.wait()

    def issue(islot, tok, rslot):
        for kk in range(ne):
            ex = idx_smem[islot, tok, kk]
            pltpu.make_async_copy(tab_hbm.at[ex], bufs[rslot].at[pl.ds(kk * slab, slab), :],
                                  sem_rows.at[rslot]).start(priority=kk % 2)

    def wait_rows(rslot):
        pltpu.make_async_copy(bufs[rslot], bufs[rslot], sem_rows.at[rslot]).wait()

    @pl.when(i == 0)
    def _():
        def first(grp, carry):
            score(h2_ref, grp, 0)
            return carry
        lax.fori_loop(0, nphase, first, 0)
        publish(0)
        for tok in range(ahead):
            issue(0, tok, tok)

    g2 = mod_ref[0][5:6]
    lane = lax.broadcasted_iota(jnp.int32, (ne, tb), 1)

    def compute(j, rslot):
        rows = bufs[rslot]
        x_row = h2_ref[pl.ds(j, 1), :]
        acc = None
        for c in range(slab):
            u_c = _unpack_u(rows[pl.ds(c, ne, stride=slab), :])
            term = u_c * x_row[:, c * LANE:(c + 1) * LANE]
            acc = term if acc is None else acc + term
        pre = jnp.sum(acc, axis=-1, keepdims=True)
        act = 0.5 * pre * (1.0 + lax.erf(pre * (2.0 ** -0.5)))
        gate = jnp.sum(jnp.where(lane == j, g_scr[cur], 0.0), axis=-1, keepdims=True)
        wgt = gate * act
        outs = []
        for c in range(slab):
            v_c = _unpack_v(rows[pl.ds(c, ne, stride=slab), :])
            outs.append(jnp.sum(v_c * wgt, axis=0, keepdims=True))
        out = jnp.concatenate(outs, axis=-1)
        y_ref[pl.ds(j, 1), :] = x1_ref[pl.ds(j, 1), :] + g2 * out

    def phase(grp, carry):
        score(h2n_ref, grp, nxt)

        @pl.when(grp == nphase - 1)
        def _():
            publish(nxt)

        def group(gq, inner):
            g = grp * groups_per_phase + gq
            for u in range(ns):
                j = g * ns + u
                wait_rows(u)
                jj = j + ahead
                over = jj >= tb
                issue(jnp.where(over, nxt, cur), jnp.where(over, jj - tb, jj), (u + ahead) % ns)
                compute(j, u)
            return inner

        lax.fori_loop(0, groups_per_phase, group, 0)
        return carry

    lax.fori_loop(0, nphase, phase, 0)

    @pl.when(jnp.logical_not(more))
    def _():
        for u in range(ahead):
            wait_rows(u)


def _peer(tab, h2, x1, mod, w, seq, tile0, ntiles, y_prev=None):
    t = h2.shape[0]
    tb = GATHER_TILE
    per_batch = seq // tb
    row = lambda i: (tile0 + i, 0)
    const = lambda i: (0, 0)
    in_specs = [pl.BlockSpec(memory_space=pl.ANY),
                pl.BlockSpec((tb, D_MODEL), row),
                pl.BlockSpec((tb, D_MODEL), lambda i: (tile0 + jnp.minimum(i + 1, ntiles - 1), 0)),
                pl.BlockSpec((PEER_HEADS, PEER_QDIM, D_MODEL), lambda i: (0, 0, 0)),
                pl.BlockSpec((N_KEYS, PEER_HALF), const),
                pl.BlockSpec((N_KEYS, PEER_HALF), const),
                pl.BlockSpec((tb, D_MODEL), row),
                pl.BlockSpec((1, 6, D_MODEL), lambda i: ((tile0 + i) // per_batch, 0, 0))]
    args = [tab, h2, h2, w["w_pq"], w["sub_keys1"], w["sub_keys2"], x1, mod]
    aliases = {}
    if y_prev is not None:
        in_specs.append(pl.BlockSpec(memory_space=pl.ANY))
        args.append(y_prev)
        aliases = {len(args) - 1: 0}
    return pl.pallas_call(
        functools.partial(_peer_kernel, ntiles=ntiles, aliased=y_prev is not None),
        grid=(ntiles,),
        in_specs=in_specs,
        input_output_aliases=aliases,
        out_specs=pl.BlockSpec((tb, D_MODEL), row),
        out_shape=jax.ShapeDtypeStruct((t, D_MODEL), F32),
        scratch_shapes=[pltpu.SMEM((2, tb, EXPERTS_PER_TOKEN), jnp.int32),
                        pltpu.VMEM((2, EXPERTS_PER_TOKEN, tb), F32),
                        pltpu.VMEM((EXPERTS_PER_TOKEN, tb), F32),
                        pltpu.VMEM((tb, EXPERTS_PER_TOKEN), jnp.int32),
                        pltpu.VMEM((D_MODEL, tb), BF16)]
                       + [pltpu.VMEM((EXPERTS_PER_TOKEN * SLAB_ROWS, LANE), jnp.uint32)] * GATHER_SLOTS
                       + [pltpu.SemaphoreType.DMA((GATHER_SLOTS,)), pltpu.SemaphoreType.DMA((1,))]
                       + _score_scratch(tb),
        compiler_params=pltpu.CompilerParams(dimension_semantics=("arbitrary",),
                                             vmem_limit_bytes=VMEM_LIMIT,
                                             disable_bounds_checks=True),
        name="peer",
    )(*args)


def _score_kernel(h2_ref, wpq_ref, k1_ref, k2_ref, *rest, ordered):
    g_ref, e_ref, hb_scr = rest[int(ordered):int(ordered) + 3]
    score_scr = rest[int(ordered) + 3:]
    hb_scr[...] = h2_ref[...].T.astype(BF16)

    def group(grp, carry):
        _score_group(hb_scr[...], wpq_ref, k1_ref, k2_ref, g_ref, e_ref, grp, score_scr)
        return carry

    lax.fori_loop(0, PEER_HEADS // HEAD_GROUP, group, 0)


def _peer_score(h2, w, tile0, ntiles, after=None):
    tb = GATHER_TILE
    const = lambda i: (0, 0)
    out = pl.BlockSpec((EXPERTS_PER_TOKEN, tb), lambda i: (0, i))
    ordering = [] if after is None else [after]
    return pl.pallas_call(
        functools.partial(_score_kernel, ordered=after is not None),
        grid=(ntiles,),
        in_specs=[pl.BlockSpec((tb, D_MODEL), lambda i: (tile0 + i, 0)),
                  pl.BlockSpec((PEER_HEADS, PEER_QDIM, D_MODEL), lambda i: (0, 0, 0)),
                  pl.BlockSpec((N_KEYS, PEER_HALF), const),
                  pl.BlockSpec((N_KEYS, PEER_HALF), const)] + [pl.BlockSpec(memory_space=pl.ANY)] * len(ordering),
        out_specs=[out, out],
        out_shape=[jax.ShapeDtypeStruct((EXPERTS_PER_TOKEN, ntiles * tb), F32)] * 2,
        scratch_shapes=[pltpu.VMEM((D_MODEL, tb), BF16)] + _score_scratch(tb),
        compiler_params=_params("arbitrary"),
        name="peer_score",
    )(h2, w["w_pq"], w["sub_keys1"], w["sub_keys2"], *ordering)


def _act_kernel(pre_ref, g_ref, after_ref, w_ref):
    del after_ref
    pre = pre_ref[...]
    w_ref[...] = g_ref[...] * (0.5 * pre * (1.0 + lax.erf(pre * (2.0 ** -0.5))))


def _peer_act(pre, gate, after):
    t, ne = pre.shape
    tm = math.gcd(t, 1024)
    row = lambda i: (i, 0)
    return pl.pallas_call(
        _act_kernel,
        grid=(t // tm,),
        in_specs=[pl.BlockSpec((tm, ne), row), pl.BlockSpec((tm, ne), row), pl.BlockSpec(memory_space=pl.ANY)],
        out_specs=pl.BlockSpec((tm, ne), row),
        out_shape=jax.ShapeDtypeStruct((t, ne), F32),
        compiler_params=_params("arbitrary"),
        name="peer_act",
    )(pre, gate, after)


def _residual_kernel(x1_ref, out_ref, mod_ref, *rest):
    rest[-1][...] = x1_ref[...] + mod_ref[0][5:6] * out_ref[...]


def _peer_residual(x1, out, mod, seq, tile0, ntiles, out_tile0, y_prev):
    tb = GATHER_TILE
    per_batch = seq // tb
    row = lambda i: (tile0 + i, 0)
    in_specs = [pl.BlockSpec((tb, D_MODEL), row),
                pl.BlockSpec((tb, D_MODEL), lambda i: (out_tile0 + i, 0)),
                pl.BlockSpec((1, 6, D_MODEL), lambda i: ((tile0 + i) // per_batch, 0, 0))]
    args = [x1, out, mod]
    aliases = {}
    if y_prev is not None:
        in_specs.append(pl.BlockSpec(memory_space=pl.ANY))
        args.append(y_prev)
        aliases = {3: 0}
    return pl.pallas_call(
        _residual_kernel,
        grid=(ntiles,),
        in_specs=in_specs,
        input_output_aliases=aliases,
        out_specs=pl.BlockSpec((tb, D_MODEL), row),
        out_shape=jax.ShapeDtypeStruct(x1.shape, F32),
        compiler_params=_params("arbitrary"),
        name="peer_residual",
    )(*args)


SC_WORKERS = 32
SC_LANES = 16
SC_CHUNK = 32
SC_SHARE = (107, 256)
TC_FIRST = (68, 149)


def _sc_udot(u_table, idx, x, row0=0):
    t, ne = idx.shape
    assert t % SC_WORKERS == 0 and ne % SC_CHUNK == 0, (t, ne)
    per_w = t // SC_WORKERS
    nchunk = ne // SC_CHUNK
    nl = D_MODEL // SC_LANES
    mesh = plsc.VectorSubcoreMesh(core_axis_name="c", subcore_axis_name="s")

    @functools.partial(
        pl.kernel, out_type=jax.ShapeDtypeStruct((t, ne), F32), mesh=mesh,
        scratch_types=[pltpu.VMEM((ne,), jnp.int32), pltpu.VMEM((D_MODEL,), F32),
                       pltpu.VMEM((2, SC_CHUNK, D_MODEL), F32), pltpu.VMEM((SC_CHUNK, SC_LANES), F32),
                       pltpu.VMEM((ne,), F32), pltpu.SemaphoreType.DMA((2,))],
        compiler_params=pltpu.CompilerParams(needs_layout_passes=False),
        name="sc_udot")
    def run(u_hbm, idx_hbm, x_hbm, pre_hbm, idx_v, x_v, rows, accs, pre_v, sems):
        wid = lax.axis_index("s") * 2 + lax.axis_index("c")
        lanes = lax.iota(jnp.int32, SC_LANES)

        def gather(c, b):
            return pltpu.make_async_copy(u_hbm.at[idx_v.at[pl.ds(c * SC_CHUNK, SC_CHUNK)]], rows.at[b], sems.at[b])

        @pl.loop(0, per_w)
        def _(tt):
            tok = wid * per_w + tt
            pltpu.sync_copy(idx_hbm.at[tok], idx_v)
            pltpu.sync_copy(x_hbm.at[row0 + tok], x_v)
            gather(0, 0).start()
            for c in range(nchunk):
                b = c % 2
                if c + 1 < nchunk:
                    gather(c + 1, 1 - b).start()
                gather(c, b).wait()
                for g in range(SC_CHUNK // 4):
                    def body(j, acc):
                        xv = x_v[pl.ds(j * SC_LANES, SC_LANES)]
                        return tuple(acc[q] + rows[b, g * 4 + q, pl.ds(j * SC_LANES, SC_LANES)] * xv
                                     for q in range(4))
                    acc = plsc.parallel_loop(0, nl, unroll=4,
                                             carry=tuple(jnp.zeros((SC_LANES,), F32) for _ in range(4)))(body)
                    for q in range(4):
                        accs[g * 4 + q, :] = acc[q]
                for part in range(SC_CHUNK // SC_LANES):
                    tot = jnp.zeros((SC_LANES,), F32)
                    for l in range(SC_LANES):
                        tot = tot + plsc.load_gather(accs, [lanes + part * SC_LANES,
                                                            jnp.full((SC_LANES,), l, jnp.int32)])
                    pre_v[pl.ds(c * SC_CHUNK + part * SC_LANES, SC_LANES)] = tot
            pltpu.sync_copy(pre_v, pre_hbm.at[tok])

    return run(u_table, idx, x)


def _sc_vsum(v_table, idx, wgt):
    t, ne = idx.shape
    assert t % SC_WORKERS == 0 and ne % SC_CHUNK == 0, (t, ne)
    per_w = t // SC_WORKERS
    nchunk = ne // SC_CHUNK
    block = 16 * SC_LANES
    mesh = plsc.VectorSubcoreMesh(core_axis_name="c", subcore_axis_name="s")

    @functools.partial(
        pl.kernel, out_type=jax.ShapeDtypeStruct((t, D_MODEL), F32), mesh=mesh,
        scratch_types=[pltpu.VMEM((ne,), jnp.int32), pltpu.VMEM((ne,), F32),
                       pltpu.VMEM((2, SC_CHUNK, D_MODEL), F32), pltpu.VMEM((D_MODEL,), F32),
                       pltpu.SemaphoreType.DMA((2,))],
        compiler_params=pltpu.CompilerParams(needs_layout_passes=False),
        name="sc_vsum")
    def run(v_hbm, idx_hbm, w_hbm, out_hbm, idx_v, w_v, rows, out_v, sems):
        wid = lax.axis_index("s") * 2 + lax.axis_index("c")

        def gather(c, b):
            return pltpu.make_async_copy(v_hbm.at[idx_v.at[pl.ds(c * SC_CHUNK, SC_CHUNK)]], rows.at[b], sems.at[b])

        @pl.loop(0, per_w)
        def _(tt):
            tok = wid * per_w + tt
            pltpu.sync_copy(idx_hbm.at[tok], idx_v)
            pltpu.sync_copy(w_hbm.at[tok], w_v)
            gather(0, 0).start()
            for c in range(nchunk):
                b = c % 2
                if c + 1 < nchunk:
                    gather(c + 1, 1 - b).start()
                gather(c, b).wait()
                for d in range(D_MODEL // block):
                    def body(k, acc):
                        wk = plsc.load_gather(w_v, [jnp.full((SC_LANES,), c * SC_CHUNK, jnp.int32) + k])
                        return tuple(acc[j] + rows[b, k, pl.ds(d * block + j * SC_LANES, SC_LANES)] * wk
                                     for j in range(16))
                    if c == 0:
                        init = tuple(jnp.zeros((SC_LANES,), F32) for _ in range(16))
                    else:
                        init = tuple(out_v[pl.ds(d * block + j * SC_LANES, SC_LANES)] for j in range(16))
                    acc = plsc.parallel_loop(0, SC_CHUNK, carry=init)(body)
                    for j in range(16):
                        out_v[pl.ds(d * block + j * SC_LANES, SC_LANES)] = acc[j]
            pltpu.sync_copy(out_v, out_hbm.at[tok])

    return run(v_table, idx, wgt)


def _expert_slabs(u_table, v_table):
    n = u_table.shape[0]
    hi = lax.bitcast_convert_type(u_table.astype(BF16), jnp.uint16).astype(jnp.uint32) << 16
    lo = lax.bitcast_convert_type(v_table.astype(BF16), jnp.uint16).astype(jnp.uint32)
    return (hi | lo).reshape(n, SLAB_ROWS, LANE)


def _unpack_u(words):
    return lax.bitcast_convert_type(words & jnp.uint32(0xFFFF0000), F32)


def _unpack_v(words):
    return lax.bitcast_convert_type(words << 16, F32)


def _rope_tables(seq):
    rows = seq // GRID_W
    r = jnp.repeat(jnp.arange(rows, dtype=F32), GRID_W)
    col = jnp.tile(jnp.arange(GRID_W, dtype=F32), rows)
    nf = ROPE_AXIS // 2
    freqs = jnp.power(ROPE_BASE, -jnp.arange(nf, dtype=F32) / nf)
    ar, ac = r[:, None] * freqs, col[:, None] * freqs
    pad = jnp.zeros((seq, LANE - MLA_ROPE), F32)
    cos = jnp.concatenate([jnp.cos(ar), jnp.cos(ar), jnp.cos(ac), jnp.cos(ac), pad], axis=-1)
    sin = jnp.concatenate([-jnp.sin(ar), jnp.sin(ar), -jnp.sin(ac), jnp.sin(ac), pad], axis=-1)
    return cos, sin


def _prep_weights(norm1_w, w_in, q_norm_w, w_uq, kv_norm_w, w_ukv, q_head_norm_w, k_head_norm_w,
                  ret_gn_w, w_out, norm2_w, w_pq, sub_keys1, sub_keys2):
    cut = Q_RANK + KV_RANK + MLA_ROPE
    w_in_p = jnp.concatenate([w_in[:, :cut], jnp.zeros((D_MODEL, LANE - MLA_ROPE), F32), w_in[:, cut:]], axis=1)
    wq = w_uq.reshape(Q_RANK, MLA_HEADS, MLA_QK)
    wq = jnp.pad(wq, ((0, 0), (0, 0), (0, HEAD_PAD - MLA_QK))).reshape(Q_RANK, MLA_HEADS * HEAD_PAD)

    def head_w(v):
        return jnp.pad(v, (0, HEAD_PAD - MLA_QK)).reshape(1, HEAD_PAD)

    return {
        "norm1_w": norm1_w.reshape(1, -1), "w_in": w_in_p.astype(BF16),
        "q_norm_w": q_norm_w.reshape(1, -1), "kv_norm_w": kv_norm_w.reshape(1, -1),
        "w_uq": wq.astype(BF16), "w_ukv": w_ukv.astype(BF16),
        "q_head_norm_w": head_w(q_head_norm_w), "k_head_norm_w": head_w(k_head_norm_w),
        "ret_gn_w": ret_gn_w.reshape(1, -1), "w_out": w_out.astype(BF16), "norm2_w": norm2_w.reshape(1, -1),
        "w_pq": w_pq.astype(BF16).reshape(D_MODEL, PEER_HEADS, PEER_QDIM).transpose(1, 2, 0), "sub_keys1": sub_keys1.astype(BF16), "sub_keys2": sub_keys2.astype(BF16),
    }


def _trunk(x, mod, w, decay_logit, ctx, after=None):
    batch, seq, _ = x.shape
    x2 = x.reshape(batch * seq, D_MODEL)
    use_rope = ctx is not None
    if use_rope:
        cos, sin = _rope_tables(seq)
    else:
        cos = sin = jnp.zeros((ROW_TILE, LANE), F32)
    ckv, krope, q, k, v, zr = _in_proj(x2, mod, w, cos, sin, seq, use_rope, after)
    if use_rope:
        ckv_c, krope_c, state_c = ctx
        past = ckv_c.shape[1]
        kr_c = jnp.pad(krope_c.reshape(batch * past, MLA_ROPE), ((0, 0), (0, LANE - MLA_ROPE)))
        ctx_kv = _kv_up(ckv_c.reshape(batch * past, KV_RANK), kr_c, w)
        s0 = state_c.reshape(batch * 2, RET_HEADS, RET_DK, RET_DV)
    else:
        ctx_kv, s0 = None, None
    att = _attention(q, k, v, batch, seq, ctx_kv)
    o_f, o_b, s_f, s_b = _retention(zr, decay_logit, batch, seq, s0)
    x1, h2 = _mix(x2, att, o_f, o_b, zr, mod, w, seq)
    states = jnp.stack([s_f, s_b], axis=1)
    return x1, h2, ckv.reshape(batch, seq, KV_RANK), krope.reshape(batch, seq, MLA_ROPE), states


def _sc_select(h2, w, tile0, ntiles, after=None):
    gt, et = _peer_score(h2, w, tile0, ntiles, after)
    return et.T.astype(jnp.int32), gt.T


def _peer_split(ctx_sc, x_ctx, x_lat, mod_ctx, mod_lat, seq_ctx, seq_lat, w, tab):
    (x1c, h2c), (x1l, h2l) = x_ctx, x_lat
    tiles_c = h2c.shape[0] // GATHER_TILE
    tiles_l = h2l.shape[0] // GATHER_TILE
    sc_tiles = tiles_l * SC_SHARE[0] // SC_SHARE[1]
    tc_tiles = tiles_l - sc_tiles
    first = tc_tiles * TC_FIRST[0] // TC_FIRST[1]

    idx_c, gate_c, pre_c = ctx_sc
    wgt_c = _peer_act(pre_c, gate_c, h2l)
    idx_l, gate_l = _sc_select(h2l, w, tc_tiles, sc_tiles, wgt_c)
    pre_l = _sc_udot(w["u_table"], idx_l, h2l, tc_tiles * GATHER_TILE)
    y = _peer(tab, h2l, x1l, mod_lat, w, seq_lat, 0, first)
    idx = jnp.concatenate([idx_c, idx_l], axis=0)
    wgt = jnp.concatenate([wgt_c, _peer_act(pre_l, gate_l, y)], axis=0)
    out = _sc_vsum(w["v_table"], idx, wgt)
    y = _peer(tab, h2l, x1l, mod_lat, w, seq_lat, first, tc_tiles - first, y_prev=y)
    y_lat = _peer_residual(x1l, out, mod_lat, seq_lat, tc_tiles, sc_tiles, tiles_c, y)
    y_ctx = _peer_residual(x1c, out, mod_ctx, seq_ctx, 0, tiles_c, 0, None)
    return y_ctx, y_lat


def kernel(x_prompt, x_sample, c, cache_ckv, cache_krope, state_ret, c_ctx, w_ada, b_ada, norm1_w, w_in,
           q_norm_w, w_uq, kv_norm_w, w_ukv, q_head_norm_w, k_head_norm_w, ret_decay_logit, ret_gn_w,
           w_out, norm2_w, w_pq, sub_keys1, sub_keys2, u_table, v_table):
    depth = w_ada.shape[0]
    nb_ctx = x_prompt.shape[0]
    nb_lat = x_sample.shape[0]
    y_prompt, y_sample = x_prompt, x_sample
    ckv_list, krope_list, ret_list = [], [], []
    for l in range(depth):
        cond_rows = -(-(nb_lat + 1) // 8) * 8
        cond = jnp.concatenate([c, c_ctx[None, :], jnp.zeros((cond_rows - nb_lat - 1, D_MODEL), F32)], axis=0)
        mod = _ada(cond, w_ada[l], b_ada[l])
        mod_lat = mod[:nb_lat].reshape(nb_lat, 6, D_MODEL)
        mod_ctx = jnp.broadcast_to(mod[nb_lat].reshape(1, 6, D_MODEL), (nb_ctx, 6, D_MODEL))
        w = _prep_weights(norm1_w[l], w_in[l], q_norm_w[l], w_uq[l], kv_norm_w[l], w_ukv[l], q_head_norm_w[l],
                          k_head_norm_w[l], ret_gn_w[l], w_out[l], norm2_w[l], w_pq[l], sub_keys1[l], sub_keys2[l])
        tab = _expert_slabs(u_table[l], v_table[l])
        w["u_table"], w["v_table"] = u_table[l], v_table[l]
        x1c, h2c, ckv_l, krope_l, ret_l = _trunk(y_prompt, mod_ctx, w, ret_decay_logit[l], None)
        ckv_list.append(ckv_l)
        krope_list.append(krope_l)
        ret_list.append(ret_l)
        idx_c, gate_c = _sc_select(h2c, w, 0, h2c.shape[0] // GATHER_TILE)
        pre_c = _sc_udot(w["u_table"], idx_c, h2c)
        x1l, h2l, _, _, _ = _trunk(y_sample, mod_lat, w, ret_decay_logit[l],
                                   (cache_ckv[:, l], cache_krope[:, l], state_ret[:, l]), after=idx_c)
        y_ctx, y_lat = _peer_split((idx_c, gate_c, pre_c), (x1c, h2c), (x1l, h2l), mod_ctx, mod_lat,
                                   y_prompt.shape[1], y_sample.shape[1], w, tab)
        y_prompt = y_ctx.reshape(y_prompt.shape)
        y_sample = y_lat.reshape(y_sample.shape)
    return (y_prompt, y_sample, jnp.stack(ckv_list, axis=1), jnp.stack(krope_list, axis=1),
            jnp.stack(ret_list, axis=1))
```

```python
import functools
import math

import jax
import jax.numpy as jnp
from jax import lax
from jax.experimental import pallas as pl
from jax.experimental.pallas import tpu as pltpu
from jax.experimental.pallas import tpu_sc as plsc

F32 = jnp.float32
BF16 = jnp.bfloat16

D_MODEL = 1024
GRID_W = 64
MLA_HEADS = 4
MLA_NOPE = 128
MLA_ROPE = 64
MLA_QK = MLA_NOPE + MLA_ROPE
MLA_V = 128
Q_RANK = 512
KV_RANK = 256
ROPE_AXIS = MLA_ROPE // 2
ROPE_BASE = 10000.0
RET_HEADS = 4
RET_DK = 128
RET_DV = 128
RET_CHUNK = 128
PEER_HEADS = 8
PEER_QDIM = 256
PEER_HALF = PEER_QDIM // 2
N_KEYS = 128
PEER_TOPK = 16
EPS = 1e-6

LANE = 128
HEAD_PAD = 2 * LANE
RET_W = 4 * RET_HEADS * RET_DK
IN_PAD = Q_RANK + KV_RANK + LANE + RET_W
VMEM_LIMIT = 48 * 1024 * 1024

ROW_TILE = 256
Q_TILE = 256
GATHER_TILE = 128
GATHER_SLOTS = 4
GATHER_AHEAD = GATHER_SLOTS - 1
EXPERTS_PER_TOKEN = PEER_HEADS * PEER_TOPK
SLAB_ROWS = D_MODEL // LANE


def _params(*sem):
    return pltpu.CompilerParams(dimension_semantics=sem, vmem_limit_bytes=VMEM_LIMIT)


def _rms(x, w):
    return x * lax.rsqrt(jnp.mean(x * x, axis=-1, keepdims=True) + EPS) * w


def _mm(a, b):
    return jnp.dot(a.astype(BF16), b.astype(BF16), preferred_element_type=F32)


def _mm_nt(a, b):
    return lax.dot_general(a.astype(BF16), b.astype(BF16), (((1,), (1,)), ((), ())),
                           preferred_element_type=F32)


def _mm_tn(a, b):
    return lax.dot_general(a.astype(BF16), b.astype(BF16), (((0,), (0,)), ((), ())),
                           preferred_element_type=F32)


def _ada_kernel(c_ref, w_ref, b_ref, o_ref):
    c = c_ref[...]
    o_ref[...] = _mm(c * jax.nn.sigmoid(c), w_ref[...]) + b_ref[...]


def _ada(cond, w_ada, b_ada):
    rows, d = cond.shape
    n = w_ada.shape[1]
    tn = 1536
    return pl.pallas_call(
        _ada_kernel,
        grid=(n // tn,),
        in_specs=[pl.BlockSpec((rows, d), lambda j: (0, 0)),
                  pl.BlockSpec((d, tn), lambda j: (0, j)),
                  pl.BlockSpec((1, tn), lambda j: (0, j))],
        out_specs=pl.BlockSpec((rows, tn), lambda j: (0, j)),
        out_shape=jax.ShapeDtypeStruct((rows, n), F32),
        compiler_params=_params("arbitrary"),
        name="ada",
    )(cond, w_ada, b_ada.reshape(1, n))


def _rope_tile(x, cos, sin):
    lane = lax.broadcasted_iota(jnp.int32, x.shape, 1)
    partner = jnp.where((lane % 32) < 16, pltpu.roll(x, LANE - 16, 1), pltpu.roll(x, 16, 1))
    return x * cos + partner * sin


def _kv_heads(kv, kr, khw, cos, sin, use_rope):
    krw = kr * khw[:, LANE:]
    if use_rope:
        krw = _rope_tile(krw, cos, sin)
    ssq_r = jnp.sum(kr * kr, axis=-1, keepdims=True)
    ks, vs = [], []
    for hd in range(MLA_HEADS):
        kn = kv[:, hd * HEAD_PAD: hd * HEAD_PAD + LANE]
        r = lax.rsqrt((jnp.sum(kn * kn, axis=-1, keepdims=True) + ssq_r) / MLA_QK + EPS)
        ks += [kn * r * khw[:, :LANE], krw * r]
        vs.append(kv[:, hd * HEAD_PAD + LANE: (hd + 1) * HEAD_PAD])
    return jnp.concatenate(ks, axis=-1).astype(BF16), jnp.concatenate(vs, axis=-1).astype(BF16)


def _inproj_kernel(x_ref, mod_ref, n1w_ref, win_ref, qnw_ref, kvnw_ref, wuq_ref, wukv_ref, qhw_ref, khw_ref,
                   cos_ref, sin_ref, *rest, use_rope):
    ckv_ref, krope_ref, q_ref, k_ref, v_ref, zr_ref = rest[-6:]
    m = mod_ref[0]
    h = _rms(x_ref[...], n1w_ref[...]) * (1.0 + m[1:2]) + m[0:1]
    z = _mm(h, win_ref[...])
    kr = z[:, Q_RANK + KV_RANK: Q_RANK + KV_RANK + LANE]
    zr_ref[...] = z[:, Q_RANK + KV_RANK + LANE:]
    ckvn = _rms(z[:, Q_RANK: Q_RANK + KV_RANK], kvnw_ref[...])
    ckv_ref[...] = ckvn
    krope_ref[...] = kr[:, :MLA_ROPE]
    cos, sin = cos_ref[...], sin_ref[...]

    q = _mm(_rms(z[:, :Q_RANK], qnw_ref[...]), wuq_ref[...])
    qhw = qhw_ref[...]
    qs = []
    for hd in range(MLA_HEADS):
        qn = q[:, hd * HEAD_PAD: hd * HEAD_PAD + LANE]
        qr = q[:, hd * HEAD_PAD + LANE: (hd + 1) * HEAD_PAD]
        ssq = jnp.sum(qn * qn, axis=-1, keepdims=True) + jnp.sum(qr * qr, axis=-1, keepdims=True)
        r = lax.rsqrt(ssq / MLA_QK + EPS)
        qrw = qr * qhw[:, LANE:]
        if use_rope:
            qrw = _rope_tile(qrw, cos, sin)
        qs += [qn * r * qhw[:, :LANE], qrw * r]
    q_ref[...] = jnp.concatenate(qs, axis=-1).astype(BF16)

    k, v = _kv_heads(_mm(ckvn, wukv_ref[...]), kr, khw_ref[...], cos, sin, use_rope)
    k_ref[...] = k
    v_ref[...] = v


def _in_proj(x, mod, w, cos, sin, seq, use_rope, after=None):
    t = x.shape[0]
    tm = ROW_TILE
    per_batch = seq // tm
    const = lambda i: (0, 0)
    row = lambda i: (i, 0)
    if use_rope:
        pos = lambda i: (i % per_batch, 0)
    else:
        pos = const
    hp = MLA_HEADS * HEAD_PAD
    ordering = [] if after is None else [after]
    return pl.pallas_call(
        functools.partial(_inproj_kernel, use_rope=use_rope),
        grid=(t // tm,),
        in_specs=[pl.BlockSpec((tm, D_MODEL), row),
                  pl.BlockSpec((1, 6, D_MODEL), lambda i: (i // per_batch, 0, 0)),
                  pl.BlockSpec((1, D_MODEL), const),
                  pl.BlockSpec((D_MODEL, IN_PAD), const),
                  pl.BlockSpec((1, Q_RANK), const),
                  pl.BlockSpec((1, KV_RANK), const),
                  pl.BlockSpec((Q_RANK, hp), const),
                  pl.BlockSpec((KV_RANK, hp), const),
                  pl.BlockSpec((1, HEAD_PAD), const),
                  pl.BlockSpec((1, HEAD_PAD), const),
                  pl.BlockSpec((tm, LANE), pos),
                  pl.BlockSpec((tm, LANE), pos)] + [pl.BlockSpec(memory_space=pl.ANY)] * len(ordering),
        out_specs=[pl.BlockSpec((tm, KV_RANK), row),
                   pl.BlockSpec((tm, MLA_ROPE), row),
                   pl.BlockSpec((tm, hp), row),
                   pl.BlockSpec((tm, hp), row),
                   pl.BlockSpec((tm, MLA_HEADS * MLA_V), row),
                   pl.BlockSpec((tm, RET_W), row)],
        out_shape=[jax.ShapeDtypeStruct((t, KV_RANK), F32),
                   jax.ShapeDtypeStruct((t, MLA_ROPE), F32),
                   jax.ShapeDtypeStruct((t, hp), BF16),
                   jax.ShapeDtypeStruct((t, hp), BF16),
                   jax.ShapeDtypeStruct((t, MLA_HEADS * MLA_V), BF16),
                   jax.ShapeDtypeStruct((t, RET_W), F32)],
        compiler_params=_params("arbitrary"),
        name="in_proj",
    )(x, mod, w["norm1_w"], w["w_in"], w["q_norm_w"], w["kv_norm_w"], w["w_uq"], w["w_ukv"],
      w["q_head_norm_w"], w["k_head_norm_w"], cos, sin, *ordering)


def _kvup_kernel(ckv_ref, kr_ref, wukv_ref, khw_ref, k_ref, v_ref):
    k, v = _kv_heads(_mm(ckv_ref[...], wukv_ref[...]), kr_ref[...], khw_ref[...], None, None, False)
    k_ref[...] = k
    v_ref[...] = v


def _kv_up(ckv, kr, w):
    t = ckv.shape[0]
    tm = ROW_TILE
    hp = MLA_HEADS * HEAD_PAD
    const = lambda i: (0, 0)
    row = lambda i: (i, 0)
    return pl.pallas_call(
        _kvup_kernel,
        grid=(t // tm,),
        in_specs=[pl.BlockSpec((tm, KV_RANK), row),
                  pl.BlockSpec((tm, LANE), row),
                  pl.BlockSpec((KV_RANK, hp), const),
                  pl.BlockSpec((1, HEAD_PAD), const)],
        out_specs=[pl.BlockSpec((tm, hp), row),
                   pl.BlockSpec((tm, MLA_HEADS * MLA_V), row)],
        out_shape=[jax.ShapeDtypeStruct((t, hp), BF16),
                   jax.ShapeDtypeStruct((t, MLA_HEADS * MLA_V), BF16)],
        compiler_params=_params("arbitrary"),
        name="kv_up",
    )(ckv, kr, w["w_ukv"], w["k_head_norm_w"])


def _attn_kernel(*refs, has_ctx):
    if has_ctx:
        q_ref, k_ref, v_ref, kc_ref, vc_ref, o_ref = refs
    else:
        q_ref, k_ref, v_ref, o_ref = refs
    scale = MLA_QK ** -0.5
    q = q_ref[...]
    s = _mm_nt(q, k_ref[...]) * scale
    m = jnp.max(s, axis=-1, keepdims=True)
    if has_ctx:
        sc = _mm_nt(q, kc_ref[...]) * scale
        m = jnp.maximum(m, jnp.max(sc, axis=-1, keepdims=True))
    p = jnp.exp(s - m)
    den = jnp.sum(p, axis=-1, keepdims=True)
    o = _mm(p, v_ref[...])
    if has_ctx:
        pc = jnp.exp(sc - m)
        den = den + jnp.sum(pc, axis=-1, keepdims=True)
        o = o + _mm(pc, vc_ref[...])
    o_ref[...] = (o / den).astype(BF16)


def _attention(q, k, v, batch, seq, ctx_kv):
    tq = min(Q_TILE, seq)
    nq = seq // tq
    has_ctx = ctx_kv is not None
    qmap = lambda b, h, i: (b * nq + i, h)
    kvmap = lambda b, h, i: (b, h)
    in_specs = [pl.BlockSpec((tq, HEAD_PAD), qmap),
                pl.BlockSpec((seq, HEAD_PAD), kvmap),
                pl.BlockSpec((seq, MLA_V), kvmap)]
    args = [q, k, v]
    if has_ctx:
        kc, vc = ctx_kv
        past = kc.shape[0] // batch
        in_specs += [pl.BlockSpec((past, HEAD_PAD), kvmap), pl.BlockSpec((past, MLA_V), kvmap)]
        args += [kc, vc]
    return pl.pallas_call(
        functools.partial(_attn_kernel, has_ctx=has_ctx),
        grid=(batch, MLA_HEADS, nq),
        in_specs=in_specs,
        out_specs=pl.BlockSpec((tq, MLA_V), qmap),
        out_shape=jax.ShapeDtypeStruct((batch * seq, MLA_HEADS * MLA_V), BF16),
        compiler_params=_params("arbitrary", "arbitrary", "arbitrary"),
        name="attention",
    )(*args)


def _ret_kernel(*refs, has_s0):
    if has_s0:
        (lg_ref, qf_ref, kf_ref, vf_ref, qb_ref, kb_ref, vb_ref, s0f_ref, s0b_ref,
         of_ref, ob_ref, sf_out, sb_out, sf_scr, sb_scr) = refs
    else:
        (lg_ref, qf_ref, kf_ref, vf_ref, qb_ref, kb_ref, vb_ref,
         of_ref, ob_ref, sf_out, sb_out, sf_scr, sb_scr) = refs
    c = pl.program_id(1)
    cs = RET_CHUNK

    @pl.when(c == 0)
    def _():
        if has_s0:
            sf_scr[...] = s0f_ref[0]
            sb_scr[...] = s0b_ref[0]
        else:
            sf_scr[...] = jnp.zeros_like(sf_scr)
            sb_scr[...] = jnp.zeros_like(sb_scr)

    ii = lax.broadcasted_iota(jnp.int32, (cs, cs), 0).astype(F32)
    jj = lax.broadcasted_iota(jnp.int32, (cs, cs), 1).astype(F32)
    rel = ii - jj
    kscale = RET_DK ** -0.5

    def chunk(q, k, v, s, intra, qdec, kdec, cdec):
        a = _mm_nt(q, k) * intra
        o = _mm(a, v) + _mm(q, s) * qdec
        return o, s * cdec + _mm_tn(k * kdec, v)

    for hd in range(RET_HEADS):
        cols = slice(hd * RET_DK, (hd + 1) * RET_DK)
        lgf = jax.nn.log_sigmoid(jnp.full((1, LANE), lg_ref[0, hd], F32))
        lgb = jax.nn.log_sigmoid(jnp.full((1, LANE), lg_ref[1, hd], F32))

        intra_f = jnp.where(rel >= 0, jnp.exp(jnp.maximum(rel, 0.0) * lgf), 0.0)
        o_f, s_f = chunk(qf_ref[:, cols], kf_ref[:, cols] * kscale, vf_ref[:, cols], sf_scr[hd], intra_f,
                         jnp.exp((ii + 1.0) * lgf), jnp.exp((cs - 1.0 - ii) * lgf), jnp.exp(cs * lgf))
        of_ref[:, cols] = o_f
        sf_scr[hd] = s_f
        sf_out[0, hd] = s_f

        intra_b = jnp.where(rel <= 0, jnp.exp(jnp.maximum(-rel, 0.0) * lgb), 0.0)
        o_b, s_b = chunk(qb_ref[:, cols], kb_ref[:, cols] * kscale, vb_ref[:, cols], sb_scr[hd], intra_b,
                         jnp.exp((cs - ii) * lgb), jnp.exp(ii * lgb), jnp.exp(cs * lgb))
        ob_ref[:, cols] = o_b
        sb_scr[hd] = s_b
        sb_out[0, hd] = s_b


def _retention(zr, decay_logit, batch, seq, s0):
    cs = RET_CHUNK
    nc = seq // cs
    nh = RET_HEADS
    width = nh * RET_DK
    has_s0 = s0 is not None

    def fwd(col):
        return pl.BlockSpec((cs, width), lambda b, c: (b * nc + c, col))

    def bwd(col):
        return pl.BlockSpec((cs, width), lambda b, c: (b * nc + nc - 1 - c, col))

    def state(d):
        return pl.BlockSpec((1, nh, RET_DK, RET_DV), lambda b, c: (b * 2 + d, 0, 0, 0))

    state_out = pl.BlockSpec((1, nh, RET_DK, RET_DV), lambda b, c: (b, 0, 0, 0))
    in_specs = [pl.BlockSpec(memory_space=pltpu.SMEM), fwd(0), fwd(1), fwd(2), bwd(0), bwd(1), bwd(2)]
    args = [decay_logit, zr, zr, zr, zr, zr, zr]
    if has_s0:
        in_specs += [state(0), state(1)]
        args += [s0, s0]
    t = batch * seq
    o_f, o_b, s_f, s_b = pl.pallas_call(
        functools.partial(_ret_kernel, has_s0=has_s0),
        grid=(batch, nc),
        in_specs=in_specs,
        out_specs=[pl.BlockSpec((cs, width), lambda b, c: (b * nc + c, 0)),
                   pl.BlockSpec((cs, width), lambda b, c: (b * nc + nc - 1 - c, 0)),
                   state_out, state_out],
        out_shape=[jax.ShapeDtypeStruct((t, width), F32),
                   jax.ShapeDtypeStruct((t, width), F32),
                   jax.ShapeDtypeStruct((batch, nh, RET_DK, RET_DV), F32),
                   jax.ShapeDtypeStruct((batch, nh, RET_DK, RET_DV), F32)],
        scratch_shapes=[pltpu.VMEM((nh, RET_DK, RET_DV), F32), pltpu.VMEM((nh, RET_DK, RET_DV), F32)],
        compiler_params=_params("arbitrary", "arbitrary"),
        name="retention",
    )(*args)
    return o_f, o_b, s_f, s_b


def _mix_kernel(x_ref, att_ref, of_ref, ob_ref, gr_ref, mod_ref, gnw_ref, wout_ref, n2w_ref, x1_ref, h2_ref):
    o = of_ref[...] + ob_ref[...]
    parts = []
    for hd in range(RET_HEADS):
        oh = o[:, hd * RET_DV:(hd + 1) * RET_DV]
        d = oh - jnp.mean(oh, axis=-1, keepdims=True)
        parts.append(d * lax.rsqrt(jnp.mean(d * d, axis=-1, keepdims=True) + EPS))
    g = gr_ref[...]
    ret = (g * jax.nn.sigmoid(g)) * (jnp.concatenate(parts, axis=-1) * gnw_ref[...])
    na = MLA_HEADS * MLA_V
    mixed = _mm(att_ref[...], wout_ref[:na, :]) + _mm(ret, wout_ref[na:, :])
    m = mod_ref[0]
    x1 = x_ref[...] + m[2:3] * mixed
    x1_ref[...] = x1
    h2_ref[...] = _rms(x1, n2w_ref[...]) * (1.0 + m[4:5]) + m[3:4]


def _mix(x, att, o_f, o_b, zr, mod, w, seq):
    t = x.shape[0]
    tm = ROW_TILE
    per_batch = seq // tm
    const = lambda i: (0, 0)
    row = lambda i: (i, 0)
    half = RET_HEADS * RET_DV
    return pl.pallas_call(
        _mix_kernel,
        grid=(t // tm,),
        in_specs=[pl.BlockSpec((tm, D_MODEL), row),
                  pl.BlockSpec((tm, half), row),
                  pl.BlockSpec((tm, half), row),
                  pl.BlockSpec((tm, half), row),
                  pl.BlockSpec((tm, half), lambda i: (i, 3)),
                  pl.BlockSpec((1, 6, D_MODEL), lambda i: (i // per_batch, 0, 0)),
                  pl.BlockSpec((1, half), const),
                  pl.BlockSpec((D_MODEL, D_MODEL), const),
                  pl.BlockSpec((1, D_MODEL), const)],
        out_specs=[pl.BlockSpec((tm, D_MODEL), row), pl.BlockSpec((tm, D_MODEL), row)],
        out_shape=[jax.ShapeDtypeStruct((t, D_MODEL), F32), jax.ShapeDtypeStruct((t, D_MODEL), F32)],
        compiler_params=_params("arbitrary"),
        name="mix",
    )(x, att, o_f, o_b, zr, mod, w["ret_gn_w"], w["w_out"], w["norm2_w"])


def _top16(jobs):
    def body(r, carry):
        for s_ref, vals_ref, pay_ref, payload in jobs:
            n, cols = s_ref.shape
            rows = lax.broadcasted_iota(jnp.int32, (n, cols), 0).astype(F32)
            s = s_ref[...]
            m = jnp.max(s, axis=0, keepdims=True)
            pos = jnp.min(jnp.where(s == m, rows, float(n)), axis=0, keepdims=True)
            hit = rows == pos
            vals_ref[pl.ds(r, 1), :] = m
            if payload is None:
                pay_ref[pl.ds(r, 1), :] = pos
            else:
                pay_ref[pl.ds(r, 1), :] = jnp.sum(jnp.where(hit, payload, 0.0), axis=0, keepdims=True)
            s_ref[...] = jnp.where(hit, -jnp.inf, s)
        return carry

    lax.fori_loop(0, PEER_TOPK, body, 0)


PAIR_COUNTS = tuple(PEER_TOPK // (a + 1) for a in range(PEER_TOPK))
NUM_PAIRS = sum(PAIR_COUNTS)
PAIR_ROWS = -(-NUM_PAIRS // 8) * 8


HEAD_GROUP = 2
SCORE_REFS = 10


def _score_heads(hbt, wpqts, k1, k2, g_outs, e_outs, scr):
    heads = [scr[h * SCORE_REFS:(h + 1) * SCORE_REFS] for h in range(len(wpqts))]
    for wpqt, (s1_scr, s2_scr, *_) in zip(wpqts, heads):
        qt = jnp.dot(wpqt, hbt, preferred_element_type=F32)
        s1_scr[...] = _mm(k1, qt[:PEER_HALF, :])
        s2_scr[...] = _mm(k2, qt[PEER_HALF:, :])
    _top16([job for (s1, s2, _, _, v1, i1, v2, i2, _, _) in heads for job in ((s1, v1, i1, None), (s2, v2, i2, None))])
    for (_, _, c_scr, p_scr, v1_scr, i1_scr, v2_scr, i2_scr, _, _) in heads:
        c_scr[...] = jnp.full(c_scr.shape, -jnp.inf, F32)
        p_scr[...] = jnp.zeros(p_scr.shape, F32)
        off = 0
        for a, nb in enumerate(PAIR_COUNTS):
            c_scr[off:off + nb, :] = v1_scr[a:a + 1, :] + v2_scr[0:nb, :]
            p_scr[off:off + nb, :] = i1_scr[a:a + 1, :] * float(N_KEYS) + i2_scr[0:nb, :]
            off += nb
    _top16([(c_scr, vt_scr, it_scr, p_scr[...]) for (_, _, c_scr, p_scr, _, _, _, _, vt_scr, it_scr) in heads])
    for (*_, vt_scr, it_scr), g_out, e_out in zip(heads, g_outs, e_outs):
        top = vt_scr[...]
        p = jnp.exp(top - jnp.max(top, axis=0, keepdims=True))
        g_out[...] = p / jnp.sum(p, axis=0, keepdims=True)
        e_out[...] = it_scr[...]


def _score_scratch(tb):
    k = PEER_TOPK
    one = ([pltpu.VMEM((N_KEYS, tb), F32)] * 2 + [pltpu.VMEM((PAIR_ROWS, tb), F32)] * 2
           + [pltpu.VMEM((k, tb), F32)] * 6)
    assert len(one) == SCORE_REFS
    return one * HEAD_GROUP


def _score_group(hbt, wpq_ref, k1_ref, k2_ref, g_ref, e_ref, grp, scr):
    wpqts, g_outs, e_outs = [], [], []
    for h in range(HEAD_GROUP):
        hd = grp * HEAD_GROUP + h
        head_rows = pl.ds(pl.multiple_of(hd * PEER_TOPK, PEER_TOPK), PEER_TOPK)
        wpqts.append(wpq_ref[hd])
        g_outs.append(g_ref.at[head_rows])
        e_outs.append(e_ref.at[head_rows])
    _score_heads(hbt, wpqts, k1_ref[...], k2_ref[...], g_outs, e_outs, scr)


def _peer_kernel(tab_hbm, h2_ref, h2n_ref, wpq_ref, k1_ref, k2_ref, x1_ref, mod_ref, *rest, ntiles, aliased):
    y_ref, idx_smem, g_scr, e_scr, ei_scr, hb_scr = rest[int(aliased):int(aliased) + 6]
    scratch = rest[int(aliased) + 6:]
    _peer_body(tab_hbm, h2_ref, h2n_ref, wpq_ref, k1_ref, k2_ref, x1_ref, mod_ref, y_ref,
               idx_smem, g_scr, e_scr, ei_scr, hb_scr, *scratch, ntiles=ntiles)


def _peer_body(tab_hbm, h2_ref, h2n_ref, wpq_ref, k1_ref, k2_ref, x1_ref, mod_ref, y_ref,
               idx_smem, g_scr, e_scr, ei_scr, hb_scr, *scratch, ntiles):
    bufs = scratch[:GATHER_SLOTS]
    sem_rows, sem_idx = scratch[GATHER_SLOTS:GATHER_SLOTS + 2]
    score_scr = scratch[GATHER_SLOTS + 2:]
    i = pl.program_id(0)
    tb = GATHER_TILE
    ns = GATHER_SLOTS
    ahead = GATHER_AHEAD
    ne = EXPERTS_PER_TOKEN
    slab = SLAB_ROWS
    nphase = PEER_HEADS // HEAD_GROUP
    groups_per_phase = tb // (ns * nphase)
    cur = i % 2
    nxt = 1 - cur
    more = i + 1 < ntiles

    def score(src_ref, grp, slot):
        @pl.when(grp == 0)
        def _():
            hb_scr[...] = src_ref[...].T.astype(BF16)
        _score_group(hb_scr[...], wpq_ref, k1_ref, k2_ref, g_scr.at[slot], e_scr, grp, score_scr)

    def publish(slot):
        ei_scr[...] = e_scr[...].T.astype(jnp.int32)
        copy = pltpu.make_async_copy(ei_scr, idx_smem.at[slot], sem_idx.at[0])
        copy.start()
        copy.wait()

    def issue(islot, tok, rslot):
        for kk in range(ne):
            ex = idx_smem[islot, tok, kk]
            pltpu.make_async_copy(tab_hbm.at[ex], bufs[rslot].at[pl.ds(kk * slab, slab), :],
                                  sem_rows.at[rslot]).start(priority=kk % 2)

    def wait_rows(rslot):
        pltpu.make_async_copy(bufs[rslot], bufs[rslot], sem_rows.at[rslot]).wait()

    @pl.when(i == 0)
    def _():
        def first(grp, carry):
            score(h2_ref, grp, 0)
            return carry
        lax.fori_loop(0, nphase, first, 0)
        publish(0)
        for tok in range(ahead):
            issue(0, tok, tok)

    g2 = mod_ref[0][5:6]
    lane = lax.broadcasted_iota(jnp.int32, (ne, tb), 1)

    def compute(j, rslot):
        rows = bufs[rslot]
        x_row = h2_ref[pl.ds(j, 1), :]
        acc = None
        for c in range(slab):
            u_c = _unpack_u(rows[pl.ds(c, ne, stride=slab), :])
            term = u_c * x_row[:, c * LANE:(c + 1) * LANE]
            acc = term if acc is None else acc + term
        pre = jnp.sum(acc, axis=-1, keepdims=True)
        act = 0.5 * pre * (1.0 + lax.erf(pre * (2.0 ** -0.5)))
        gate = jnp.sum(jnp.where(lane == j, g_scr[cur], 0.0), axis=-1, keepdims=True)
        wgt = gate * act
        outs = []
        for c in range(slab):
            v_c = _unpack_v(rows[pl.ds(c, ne, stride=slab), :])
            outs.append(jnp.sum(v_c * wgt, axis=0, keepdims=True))
        out = jnp.concatenate(outs, axis=-1)
        y_ref[pl.ds(j, 1), :] = x1_ref[pl.ds(j, 1), :] + g2 * out

    def phase(grp, carry):
        score(h2n_ref, grp, nxt)

        @pl.when(grp == nphase - 1)
        def _():
            publish(nxt)

        def group(gq, inner):
            g = grp * groups_per_phase + gq
            for u in range(ns):
                j = g * ns + u
                wait_rows(u)
                jj = j + ahead
                over = jj >= tb
                issue(jnp.where(over, nxt, cur), jnp.where(over, jj - tb, jj), (u + ahead) % ns)
                compute(j, u)
            return inner

        lax.fori_loop(0, groups_per_phase, group, 0)
        return carry

    lax.fori_loop(0, nphase, phase, 0)

    @pl.when(jnp.logical_not(more))
    def _():
        for u in range(ahead):
            wait_rows(u)


def _peer(tab, h2, x1, mod, w, seq, tile0, ntiles, y_prev=None):
    t = h2.shape[0]
    tb = GATHER_TILE
    per_batch = seq // tb
    row = lambda i: (tile0 + i, 0)
    const = lambda i: (0, 0)
    in_specs = [pl.BlockSpec(memory_space=pl.ANY),
                pl.BlockSpec((tb, D_MODEL), row),
                pl.BlockSpec((tb, D_MODEL), lambda i: (tile0 + jnp.minimum(i + 1, ntiles - 1), 0)),
                pl.BlockSpec((PEER_HEADS, PEER_QDIM, D_MODEL), lambda i: (0, 0, 0)),
                pl.BlockSpec((N_KEYS, PEER_HALF), const),
                pl.BlockSpec((N_KEYS, PEER_HALF), const),
                pl.BlockSpec((tb, D_MODEL), row),
                pl.BlockSpec((1, 6, D_MODEL), lambda i: ((tile0 + i) // per_batch, 0, 0))]
    args = [tab, h2, h2, w["w_pq"], w["sub_keys1"], w["sub_keys2"], x1, mod]
    aliases = {}
    if y_prev is not None:
        in_specs.append(pl.BlockSpec(memory_space=pl.ANY))
        args.append(y_prev)
        aliases = {len(args) - 1: 0}
    return pl.pallas_call(
        functools.partial(_peer_kernel, ntiles=ntiles, aliased=y_prev is not None),
        grid=(ntiles,),
        in_specs=in_specs,
        input_output_aliases=aliases,
        out_specs=pl.BlockSpec((tb, D_MODEL), row),
        out_shape=jax.ShapeDtypeStruct((t, D_MODEL), F32),
        scratch_shapes=[pltpu.SMEM((2, tb, EXPERTS_PER_TOKEN), jnp.int32),
                        pltpu.VMEM((2, EXPERTS_PER_TOKEN, tb), F32),
                        pltpu.VMEM((EXPERTS_PER_TOKEN, tb), F32),
                        pltpu.VMEM((tb, EXPERTS_PER_TOKEN), jnp.int32),
                        pltpu.VMEM((D_MODEL, tb), BF16)]
                       + [pltpu.VMEM((EXPERTS_PER_TOKEN * SLAB_ROWS, LANE), jnp.uint32)] * GATHER_SLOTS
                       + [pltpu.SemaphoreType.DMA((GATHER_SLOTS,)), pltpu.SemaphoreType.DMA((1,))]
                       + _score_scratch(tb),
        compiler_params=pltpu.CompilerParams(dimension_semantics=("arbitrary",),
                                             vmem_limit_bytes=VMEM_LIMIT,
                                             disable_bounds_checks=True),
        name="peer",
    )(*args)


def _score_kernel(h2_ref, wpq_ref, k1_ref, k2_ref, *rest, ordered):
    g_ref, e_ref, hb_scr = rest[int(ordered):int(ordered) + 3]
    score_scr = rest[int(ordered) + 3:]
    hb_scr[...] = h2_ref[...].T.astype(BF16)

    def group(grp, carry):
        _score_group(hb_scr[...], wpq_ref, k1_ref, k2_ref, g_ref, e_ref, grp, score_scr)
        return carry

    lax.fori_loop(0, PEER_HEADS // HEAD_GROUP, group, 0)


def _peer_score(h2, w, tile0, ntiles, after=None):
    tb = GATHER_TILE
    const = lambda i: (0, 0)
    out = pl.BlockSpec((EXPERTS_PER_TOKEN, tb), lambda i: (0, i))
    ordering = [] if after is None else [after]
    return pl.pallas_call(
        functools.partial(_score_kernel, ordered=after is not None),
        grid=(ntiles,),
        in_specs=[pl.BlockSpec((tb, D_MODEL), lambda i: (tile0 + i, 0)),
                  pl.BlockSpec((PEER_HEADS, PEER_QDIM, D_MODEL), lambda i: (0, 0, 0)),
                  pl.BlockSpec((N_KEYS, PEER_HALF), const),
                  pl.BlockSpec((N_KEYS, PEER_HALF), const)] + [pl.BlockSpec(memory_space=pl.ANY)] * len(ordering),
        out_specs=[out, out],
        out_shape=[jax.ShapeDtypeStruct((EXPERTS_PER_TOKEN, ntiles * tb), F32)] * 2,
        scratch_shapes=[pltpu.VMEM((D_MODEL, tb), BF16)] + _score_scratch(tb),
        compiler_params=_params("arbitrary"),
        name="peer_score",
    )(h2, w["w_pq"], w["sub_keys1"], w["sub_keys2"], *ordering)


def _act_kernel(pre_ref, g_ref, after_ref, w_ref):
    del after_ref
    pre = pre_ref[...]
    w_ref[...] = g_ref[...] * (0.5 * pre * (1.0 + lax.erf(pre * (2.0 ** -0.5))))


def _peer_act(pre, gate, after):
    t, ne = pre.shape
    tm = math.gcd(t, 1024)
    row = lambda i: (i, 0)
    return pl.pallas_call(
        _act_kernel,
        grid=(t // tm,),
        in_specs=[pl.BlockSpec((tm, ne), row), pl.BlockSpec((tm, ne), row), pl.BlockSpec(memory_space=pl.ANY)],
        out_specs=pl.BlockSpec((tm, ne), row),
        out_shape=jax.ShapeDtypeStruct((t, ne), F32),
        compiler_params=_params("arbitrary"),
        name="peer_act",
    )(pre, gate, after)


def _residual_kernel(x1_ref, out_ref, mod_ref, *rest):
    rest[-1][...] = x1_ref[...] + mod_ref[0][5:6] * out_ref[...]


def _peer_residual(x1, out, mod, seq, tile0, ntiles, out_tile0, y_prev):
    tb = GATHER_TILE
    per_batch = seq // tb
    row = lambda i: (tile0 + i, 0)
    in_specs = [pl.BlockSpec((tb, D_MODEL), row),
                pl.BlockSpec((tb, D_MODEL), lambda i: (out_tile0 + i, 0)),
                pl.BlockSpec((1, 6, D_MODEL), lambda i: ((tile0 + i) // per_batch, 0, 0))]
    args = [x1, out, mod]
    aliases = {}
    if y_prev is not None:
        in_specs.append(pl.BlockSpec(memory_space=pl.ANY))
        args.append(y_prev)
        aliases = {3: 0}
    return pl.pallas_call(
        _residual_kernel,
        grid=(ntiles,),
        in_specs=in_specs,
        input_output_aliases=aliases,
        out_specs=pl.BlockSpec((tb, D_MODEL), row),
        out_shape=jax.ShapeDtypeStruct(x1.shape, F32),
        compiler_params=_params("arbitrary"),
        name="peer_residual",
    )(*args)


SC_WORKERS = 32
SC_LANES = 16
SC_CHUNK = 32
SC_ROWS = 8
SC_SHARE = (107, 256)
TC_FIRST = (68, 149)


def _sc_udot(u_table, idx, x, row0=0):
    t, ne = idx.shape
    assert t % SC_WORKERS == 0 and ne % SC_CHUNK == 0, (t, ne)
    per_w = t // SC_WORKERS
    nchunk = ne // SC_CHUNK
    nl = D_MODEL // SC_LANES
    mesh = plsc.VectorSubcoreMesh(core_axis_name="c", subcore_axis_name="s")

    @functools.partial(
        pl.kernel, out_type=jax.ShapeDtypeStruct((t, ne), F32), mesh=mesh,
        scratch_types=[pltpu.VMEM((ne,), jnp.int32), pltpu.VMEM((D_MODEL,), F32),
                       pltpu.VMEM((2, SC_CHUNK, D_MODEL), F32), pltpu.VMEM((SC_CHUNK, SC_LANES), F32),
                       pltpu.VMEM((ne,), F32), pltpu.SemaphoreType.DMA((2,))],
        compiler_params=pltpu.CompilerParams(needs_layout_passes=False),
        name="sc_udot")
    def run(u_hbm, idx_hbm, x_hbm, pre_hbm, idx_v, x_v, rows, accs, pre_v, sems):
        wid = lax.axis_index("s") * 2 + lax.axis_index("c")
        lanes = lax.iota(jnp.int32, SC_LANES)

        def gather(c, b):
            return pltpu.make_async_copy(u_hbm.at[idx_v.at[pl.ds(c * SC_CHUNK, SC_CHUNK)]], rows.at[b], sems.at[b])

        @pl.loop(0, per_w)
        def _(tt):
            tok = wid * per_w + tt
            pltpu.sync_copy(idx_hbm.at[tok], idx_v)
            pltpu.sync_copy(x_hbm.at[row0 + tok], x_v)
            gather(0, 0).start()
            for c in range(nchunk):
                b = c % 2
                if c + 1 < nchunk:
                    gather(c + 1, 1 - b).start()
                gather(c, b).wait()
                for g in range(SC_CHUNK // SC_ROWS):
                    def body(j, acc):
                        xv = x_v[pl.ds(j * SC_LANES, SC_LANES)]
                        return tuple(acc[q] + rows[b, g * SC_ROWS + q, pl.ds(j * SC_LANES, SC_LANES)] * xv
                                     for q in range(SC_ROWS))
                    acc = plsc.parallel_loop(0, nl, unroll=2,
                                             carry=tuple(jnp.zeros((SC_LANES,), F32) for _ in range(SC_ROWS)))(body)
                    for q in range(SC_ROWS):
                        accs[g * SC_ROWS + q, :] = acc[q]
                for part in range(SC_CHUNK // SC_LANES):
                    tot = jnp.zeros((SC_LANES,), F32)
                    for l in range(SC_LANES):
                        tot = tot + plsc.load_gather(accs, [lanes + part * SC_LANES,
                                                            jnp.full((SC_LANES,), l, jnp.int32)])
                    pre_v[pl.ds(c * SC_CHUNK + part * SC_LANES, SC_LANES)] = tot
            pltpu.sync_copy(pre_v, pre_hbm.at[tok])

    return run(u_table, idx, x)


def _sc_vsum(v_table, idx, wgt):
    t, ne = idx.shape
    assert t % SC_WORKERS == 0 and ne % SC_CHUNK == 0, (t, ne)
    per_w = t // SC_WORKERS
    nchunk = ne // SC_CHUNK
    block = 16 * SC_LANES
    mesh = plsc.VectorSubcoreMesh(core_axis_name="c", subcore_axis_name="s")

    @functools.partial(
        pl.kernel, out_type=jax.ShapeDtypeStruct((t, D_MODEL), F32), mesh=mesh,
        scratch_types=[pltpu.VMEM((ne,), jnp.int32), pltpu.VMEM((ne,), F32),
                       pltpu.VMEM((2, SC_CHUNK, D_MODEL), F32), pltpu.VMEM((D_MODEL,), F32),
                       pltpu.SemaphoreType.DMA((2,))],
        compiler_params=pltpu.CompilerParams(needs_layout_passes=False),
        name="sc_vsum")
    def run(v_hbm, idx_hbm, w_hbm, out_hbm, idx_v, w_v, rows, out_v, sems):
        wid = lax.axis_index("s") * 2 + lax.axis_index("c")

        def gather(c, b):
            return pltpu.make_async_copy(v_hbm.at[idx_v.at[pl.ds(c * SC_CHUNK, SC_CHUNK)]], rows.at[b], sems.at[b])

        @pl.loop(0, per_w)
        def _(tt):
            tok = wid * per_w + tt
            pltpu.sync_copy(idx_hbm.at[tok], idx_v)
            pltpu.sync_copy(w_hbm.at[tok], w_v)
            gather(0, 0).start()
            for c in range(nchunk):
                b = c % 2
                if c + 1 < nchunk:
                    gather(c + 1, 1 - b).start()
                gather(c, b).wait()
                for d in range(D_MODEL // block):
                    def body(k, acc):
                        wk = plsc.load_gather(w_v, [jnp.full((SC_LANES,), c * SC_CHUNK, jnp.int32) + k])
                        return tuple(acc[j] + rows[b, k, pl.ds(d * block + j * SC_LANES, SC_LANES)] * wk
                                     for j in range(16))
                    if c == 0:
                        init = tuple(jnp.zeros((SC_LANES,), F32) for _ in range(16))
                    else:
                        init = tuple(out_v[pl.ds(d * block + j * SC_LANES, SC_LANES)] for j in range(16))
                    acc = plsc.parallel_loop(0, SC_CHUNK, carry=init)(body)
                    for j in range(16):
                        out_v[pl.ds(d * block + j * SC_LANES, SC_LANES)] = acc[j]
            pltpu.sync_copy(out_v, out_hbm.at[tok])

    return run(v_table, idx, wgt)


def _expert_slabs(u_table, v_table):
    n = u_table.shape[0]
    hi = lax.bitcast_convert_type(u_table.astype(BF16), jnp.uint16).astype(jnp.uint32) << 16
    lo = lax.bitcast_convert_type(v_table.astype(BF16), jnp.uint16).astype(jnp.uint32)
    return (hi | lo).reshape(n, SLAB_ROWS, LANE)


def _unpack_u(words):
    return lax.bitcast_convert_type(words & jnp.uint32(0xFFFF0000), F32)


def _unpack_v(words):
    return lax.bitcast_convert_type(words << 16, F32)


def _rope_tables(seq):
    rows = seq // GRID_W
    r = jnp.repeat(jnp.arange(rows, dtype=F32), GRID_W)
    col = jnp.tile(jnp.arange(GRID_W, dtype=F32), rows)
    nf = ROPE_AXIS // 2
    freqs = jnp.power(ROPE_BASE, -jnp.arange(nf, dtype=F32) / nf)
    ar, ac = r[:, None] * freqs, col[:, None] * freqs
    pad = jnp.zeros((seq, LANE - MLA_ROPE), F32)
    cos = jnp.concatenate([jnp.cos(ar), jnp.cos(ar), jnp.cos(ac), jnp.cos(ac), pad], axis=-1)
    sin = jnp.concatenate([-jnp.sin(ar), jnp.sin(ar), -jnp.sin(ac), jnp.sin(ac), pad], axis=-1)
    return cos, sin


def _prep_weights(norm1_w, w_in, q_norm_w, w_uq, kv_norm_w, w_ukv, q_head_norm_w, k_head_norm_w,
                  ret_gn_w, w_out, norm2_w, w_pq, sub_keys1, sub_keys2):
    cut = Q_RANK + KV_RANK + MLA_ROPE
    w_in_p = jnp.concatenate([w_in[:, :cut], jnp.zeros((D_MODEL, LANE - MLA_ROPE), F32), w_in[:, cut:]], axis=1)
    wq = w_uq.reshape(Q_RANK, MLA_HEADS, MLA_QK)
    wq = jnp.pad(wq, ((0, 0), (0, 0), (0, HEAD_PAD - MLA_QK))).reshape(Q_RANK, MLA_HEADS * HEAD_PAD)

    def head_w(v):
        return jnp.pad(v, (0, HEAD_PAD - MLA_QK)).reshape(1, HEAD_PAD)

    return {
        "norm1_w": norm1_w.reshape(1, -1), "w_in": w_in_p.astype(BF16),
        "q_norm_w": q_norm_w.reshape(1, -1), "kv_norm_w": kv_norm_w.reshape(1, -1),
        "w_uq": wq.astype(BF16), "w_ukv": w_ukv.astype(BF16),
        "q_head_norm_w": head_w(q_head_norm_w), "k_head_norm_w": head_w(k_head_norm_w),
        "ret_gn_w": ret_gn_w.reshape(1, -1), "w_out": w_out.astype(BF16), "norm2_w": norm2_w.reshape(1, -1),
        "w_pq": w_pq.astype(BF16).reshape(D_MODEL, PEER_HEADS, PEER_QDIM).transpose(1, 2, 0), "sub_keys1": sub_keys1.astype(BF16), "sub_keys2": sub_keys2.astype(BF16),
    }


def _trunk(x, mod, w, decay_logit, ctx, after=None):
    batch, seq, _ = x.shape
    x2 = x.reshape(batch * seq, D_MODEL)
    use_rope = ctx is not None
    if use_rope:
        cos, sin = _rope_tables(seq)
    else:
        cos = sin = jnp.zeros((ROW_TILE, LANE), F32)
    ckv, krope, q, k, v, zr = _in_proj(x2, mod, w, cos, sin, seq, use_rope, after)
    if use_rope:
        ckv_c, krope_c, state_c = ctx
        past = ckv_c.shape[1]
        kr_c = jnp.pad(krope_c.reshape(batch * past, MLA_ROPE), ((0, 0), (0, LANE - MLA_ROPE)))
        ctx_kv = _kv_up(ckv_c.reshape(batch * past, KV_RANK), kr_c, w)
        s0 = state_c.reshape(batch * 2, RET_HEADS, RET_DK, RET_DV)
    else:
        ctx_kv, s0 = None, None
    att = _attention(q, k, v, batch, seq, ctx_kv)
    o_f, o_b, s_f, s_b = _retention(zr, decay_logit, batch, seq, s0)
    x1, h2 = _mix(x2, att, o_f, o_b, zr, mod, w, seq)
    states = jnp.stack([s_f, s_b], axis=1)
    return x1, h2, ckv.reshape(batch, seq, KV_RANK), krope.reshape(batch, seq, MLA_ROPE), states


def _sc_select(h2, w, tile0, ntiles, after=None):
    gt, et = _peer_score(h2, w, tile0, ntiles, after)
    return et.T.astype(jnp.int32), gt.T


def _peer_split(ctx_sc, x_ctx, x_lat, mod_ctx, mod_lat, seq_ctx, seq_lat, w, tab):
    (x1c, h2c), (x1l, h2l) = x_ctx, x_lat
    tiles_c = h2c.shape[0] // GATHER_TILE
    tiles_l = h2l.shape[0] // GATHER_TILE
    sc_tiles = tiles_l * SC_SHARE[0] // SC_SHARE[1]
    tc_tiles = tiles_l - sc_tiles
    first = tc_tiles * TC_FIRST[0] // TC_FIRST[1]

    idx_c, gate_c, pre_c = ctx_sc
    wgt_c = _peer_act(pre_c, gate_c, h2l)
    idx_l, gate_l = _sc_select(h2l, w, tc_tiles, sc_tiles, wgt_c)
    pre_l = _sc_udot(w["u_table"], idx_l, h2l, tc_tiles * GATHER_TILE)
    y = _peer(tab, h2l, x1l, mod_lat, w, seq_lat, 0, first)
    idx = jnp.concatenate([idx_c, idx_l], axis=0)
    wgt = jnp.concatenate([wgt_c, _peer_act(pre_l, gate_l, y)], axis=0)
    out = _sc_vsum(w["v_table"], idx, wgt)
    y = _peer(tab, h2l, x1l, mod_lat, w, seq_lat, first, tc_tiles - first, y_prev=y)
    y_lat = _peer_residual(x1l, out, mod_lat, seq_lat, tc_tiles, sc_tiles, tiles_c, y)
    y_ctx = _peer_residual(x1c, out, mod_ctx, seq_ctx, 0, tiles_c, 0, None)
    return y_ctx, y_lat


def kernel(x_prompt, x_sample, c, cache_ckv, cache_krope, state_ret, c_ctx, w_ada, b_ada, norm1_w, w_in,
           q_norm_w, w_uq, kv_norm_w, w_ukv, q_head_norm_w, k_head_norm_w, ret_decay_logit, ret_gn_w,
           w_out, norm2_w, w_pq, sub_keys1, sub_keys2, u_table, v_table):
    depth = w_ada.shape[0]
    nb_ctx = x_prompt.shape[0]
    nb_lat = x_sample.shape[0]
    y_prompt, y_sample = x_prompt, x_sample
    ckv_list, krope_list, ret_list = [], [], []
    for l in range(depth):
        cond_rows = -(-(nb_lat + 1) // 8) * 8
        cond = jnp.concatenate([c, c_ctx[None, :], jnp.zeros((cond_rows - nb_lat - 1, D_MODEL), F32)], axis=0)
        mod = _ada(cond, w_ada[l], b_ada[l])
        mod_lat = mod[:nb_lat].reshape(nb_lat, 6, D_MODEL)
        mod_ctx = jnp.broadcast_to(mod[nb_lat].reshape(1, 6, D_MODEL), (nb_ctx, 6, D_MODEL))
        w = _prep_weights(norm1_w[l], w_in[l], q_norm_w[l], w_uq[l], kv_norm_w[l], w_ukv[l], q_head_norm_w[l],
                          k_head_norm_w[l], ret_gn_w[l], w_out[l], norm2_w[l], w_pq[l], sub_keys1[l], sub_keys2[l])
        tab = _expert_slabs(u_table[l], v_table[l])
        w["u_table"], w["v_table"] = u_table[l], v_table[l]
        x1c, h2c, ckv_l, krope_l, ret_l = _trunk(y_prompt, mod_ctx, w, ret_decay_logit[l], None)
        ckv_list.append(ckv_l)
        krope_list.append(krope_l)
        ret_list.append(ret_l)
        idx_c, gate_c = _sc_select(h2c, w, 0, h2c.shape[0] // GATHER_TILE)
        pre_c = _sc_udot(w["u_table"], idx_c, h2c)
        x1l, h2l, _, _, _ = _trunk(y_sample, mod_lat, w, ret_decay_logit[l],
                                   (cache_ckv[:, l], cache_krope[:, l], state_ret[:, l]), after=idx_c)
        y_ctx, y_lat = _peer_split((idx_c, gate_c, pre_c), (x1c, h2c), (x1l, h2l), mod_ctx, mod_lat,
                                   y_prompt.shape[1], y_sample.shape[1], w, tab)
        y_prompt = y_ctx.reshape(y_prompt.shape)
        y_sample = y_lat.reshape(y_sample.shape)
    return (y_prompt, y_sample, jnp.stack(ckv_list, axis=1), jnp.stack(krope_list, axis=1),
            jnp.stack(ret_list, axis=1))
```

```python
import functools
import math

import jax
import jax.numpy as jnp
from jax import lax
from jax.experimental import pallas as pl
from jax.experimental.pallas import tpu as pltpu
from jax.experimental.pallas import tpu_sc as plsc

F32 = jnp.float32
BF16 = jnp.bfloat16

D_MODEL = 1024
GRID_W = 64
MLA_HEADS = 4
MLA_NOPE = 128
MLA_ROPE = 64
MLA_QK = MLA_NOPE + MLA_ROPE
MLA_V = 128
Q_RANK = 512
KV_RANK = 256
ROPE_AXIS = MLA_ROPE // 2
ROPE_BASE = 10000.0
RET_HEADS = 4
RET_DK = 128
RET_DV = 128
RET_CHUNK = 128
PEER_HEADS = 8
PEER_QDIM = 256
PEER_HALF = PEER_QDIM // 2
N_KEYS = 128
PEER_TOPK = 16
EPS = 1e-6

LANE = 128
HEAD_PAD = 2 * LANE
RET_W = 4 * RET_HEADS * RET_DK
IN_PAD = Q_RANK + KV_RANK + LANE + RET_W
VMEM_LIMIT = 48 * 1024 * 1024

ROW_TILE = 256
Q_TILE = 256
GATHER_TILE = 128
GATHER_SLOTS = 4
GATHER_AHEAD = GATHER_SLOTS - 1
EXPERTS_PER_TOKEN = PEER_HEADS * PEER_TOPK
SLAB_ROWS = D_MODEL // LANE


def _params(*sem):
    return pltpu.CompilerParams(dimension_semantics=sem, vmem_limit_bytes=VMEM_LIMIT)


def _rms(x, w):
    return x * lax.rsqrt(jnp.mean(x * x, axis=-1, keepdims=True) + EPS) * w


def _mm(a, b):
    return jnp.dot(a.astype(BF16), b.astype(BF16), preferred_element_type=F32)


def _mm_nt(a, b):
    return lax.dot_general(a.astype(BF16), b.astype(BF16), (((1,), (1,)), ((), ())),
                           preferred_element_type=F32)


def _mm_tn(a, b):
    return lax.dot_general(a.astype(BF16), b.astype(BF16), (((0,), (0,)), ((), ())),
                           preferred_element_type=F32)


def _ada_kernel(c_ref, w_ref, b_ref, o_ref):
    c = c_ref[...]
    o_ref[...] = _mm(c * jax.nn.sigmoid(c), w_ref[...]) + b_ref[...]


def _ada(cond, w_ada, b_ada):
    rows, d = cond.shape
    n = w_ada.shape[1]
    tn = 1536
    return pl.pallas_call(
        _ada_kernel,
        grid=(n // tn,),
        in_specs=[pl.BlockSpec((rows, d), lambda j: (0, 0)),
                  pl.BlockSpec((d, tn), lambda j: (0, j)),
                  pl.BlockSpec((1, tn), lambda j: (0, j))],
        out_specs=pl.BlockSpec((rows, tn), lambda j: (0, j)),
        out_shape=jax.ShapeDtypeStruct((rows, n), F32),
        compiler_params=_params("arbitrary"),
        name="ada",
    )(cond, w_ada, b_ada.reshape(1, n))


def _rope_tile(x, cos, sin):
    lane = lax.broadcasted_iota(jnp.int32, x.shape, 1)
    partner = jnp.where((lane % 32) < 16, pltpu.roll(x, LANE - 16, 1), pltpu.roll(x, 16, 1))
    return x * cos + partner * sin


def _kv_heads(kv, kr, khw, cos, sin, use_rope):
    krw = kr * khw[:, LANE:]
    if use_rope:
        krw = _rope_tile(krw, cos, sin)
    ssq_r = jnp.sum(kr * kr, axis=-1, keepdims=True)
    ks, vs = [], []
    for hd in range(MLA_HEADS):
        kn = kv[:, hd * HEAD_PAD: hd * HEAD_PAD + LANE]
        r = lax.rsqrt((jnp.sum(kn * kn, axis=-1, keepdims=True) + ssq_r) / MLA_QK + EPS)
        ks += [kn * r * khw[:, :LANE], krw * r]
        vs.append(kv[:, hd * HEAD_PAD + LANE: (hd + 1) * HEAD_PAD])
    return jnp.concatenate(ks, axis=-1).astype(BF16), jnp.concatenate(vs, axis=-1).astype(BF16)


def _inproj_kernel(x_ref, mod_ref, n1w_ref, win_ref, qnw_ref, kvnw_ref, wuq_ref, wukv_ref, qhw_ref, khw_ref,
                   cos_ref, sin_ref, *rest, use_rope):
    ckv_ref, krope_ref, q_ref, k_ref, v_ref, zr_ref = rest[-6:]
    m = mod_ref[0]
    h = _rms(x_ref[...], n1w_ref[...]) * (1.0 + m[1:2]) + m[0:1]
    z = _mm(h, win_ref[...])
    kr = z[:, Q_RANK + KV_RANK: Q_RANK + KV_RANK + LANE]
    zr_ref[...] = z[:, Q_RANK + KV_RANK + LANE:]
    ckvn = _rms(z[:, Q_RANK: Q_RANK + KV_RANK], kvnw_ref[...])
    ckv_ref[...] = ckvn
    krope_ref[...] = kr[:, :MLA_ROPE]
    cos, sin = cos_ref[...], sin_ref[...]

    q = _mm(_rms(z[:, :Q_RANK], qnw_ref[...]), wuq_ref[...])
    qhw = qhw_ref[...]
    qs = []
    for hd in range(MLA_HEADS):
        qn = q[:, hd * HEAD_PAD: hd * HEAD_PAD + LANE]
        qr = q[:, hd * HEAD_PAD + LANE: (hd + 1) * HEAD_PAD]
        ssq = jnp.sum(qn * qn, axis=-1, keepdims=True) + jnp.sum(qr * qr, axis=-1, keepdims=True)
        r = lax.rsqrt(ssq / MLA_QK + EPS)
        qrw = qr * qhw[:, LANE:]
        if use_rope:
            qrw = _rope_tile(qrw, cos, sin)
        qs += [qn * r * qhw[:, :LANE], qrw * r]
    q_ref[...] = jnp.concatenate(qs, axis=-1).astype(BF16)

    k, v = _kv_heads(_mm(ckvn, wukv_ref[...]), kr, khw_ref[...], cos, sin, use_rope)
    k_ref[...] = k
    v_ref[...] = v


def _in_proj(x, mod, w, cos, sin, seq, use_rope, after=None):
    t = x.shape[0]
    tm = ROW_TILE
    per_batch = seq // tm
    const = lambda i: (0, 0)
    row = lambda i: (i, 0)
    if use_rope:
        pos = lambda i: (i % per_batch, 0)
    else:
        pos = const
    hp = MLA_HEADS * HEAD_PAD
    ordering = [] if after is None else [after]
    return pl.pallas_call(
        functools.partial(_inproj_kernel, use_rope=use_rope),
        grid=(t // tm,),
        in_specs=[pl.BlockSpec((tm, D_MODEL), row),
                  pl.BlockSpec((1, 6, D_MODEL), lambda i: (i // per_batch, 0, 0)),
                  pl.BlockSpec((1, D_MODEL), const),
                  pl.BlockSpec((D_MODEL, IN_PAD), const),
                  pl.BlockSpec((1, Q_RANK), const),
                  pl.BlockSpec((1, KV_RANK), const),
                  pl.BlockSpec((Q_RANK, hp), const),
                  pl.BlockSpec((KV_RANK, hp), const),
                  pl.BlockSpec((1, HEAD_PAD), const),
                  pl.BlockSpec((1, HEAD_PAD), const),
                  pl.BlockSpec((tm, LANE), pos),
                  pl.BlockSpec((tm, LANE), pos)] + [pl.BlockSpec(memory_space=pl.ANY)] * len(ordering),
        out_specs=[pl.BlockSpec((tm, KV_RANK), row),
                   pl.BlockSpec((tm, MLA_ROPE), row),
                   pl.BlockSpec((tm, hp), row),
                   pl.BlockSpec((tm, hp), row),
                   pl.BlockSpec((tm, MLA_HEADS * MLA_V), row),
                   pl.BlockSpec((tm, RET_W), row)],
        out_shape=[jax.ShapeDtypeStruct((t, KV_RANK), F32),
                   jax.ShapeDtypeStruct((t, MLA_ROPE), F32),
                   jax.ShapeDtypeStruct((t, hp), BF16),
                   jax.ShapeDtypeStruct((t, hp), BF16),
                   jax.ShapeDtypeStruct((t, MLA_HEADS * MLA_V), BF16),
                   jax.ShapeDtypeStruct((t, RET_W), F32)],
        compiler_params=_params("arbitrary"),
        name="in_proj",
    )(x, mod, w["norm1_w"], w["w_in"], w["q_norm_w"], w["kv_norm_w"], w["w_uq"], w["w_ukv"],
      w["q_head_norm_w"], w["k_head_norm_w"], cos, sin, *ordering)


def _kvup_kernel(ckv_ref, kr_ref, wukv_ref, khw_ref, k_ref, v_ref):
    k, v = _kv_heads(_mm(ckv_ref[...], wukv_ref[...]), kr_ref[...], khw_ref[...], None, None, False)
    k_ref[...] = k
    v_ref[...] = v


def _kv_up(ckv, kr, w):
    t = ckv.shape[0]
    tm = ROW_TILE
    hp = MLA_HEADS * HEAD_PAD
    const = lambda i: (0, 0)
    row = lambda i: (i, 0)
    return pl.pallas_call(
        _kvup_kernel,
        grid=(t // tm,),
        in_specs=[pl.BlockSpec((tm, KV_RANK), row),
                  pl.BlockSpec((tm, LANE), row),
                  pl.BlockSpec((KV_RANK, hp), const),
                  pl.BlockSpec((1, HEAD_PAD), const)],
        out_specs=[pl.BlockSpec((tm, hp), row),
                   pl.BlockSpec((tm, MLA_HEADS * MLA_V), row)],
        out_shape=[jax.ShapeDtypeStruct((t, hp), BF16),
                   jax.ShapeDtypeStruct((t, MLA_HEADS * MLA_V), BF16)],
        compiler_params=_params("arbitrary"),
        name="kv_up",
    )(ckv, kr, w["w_ukv"], w["k_head_norm_w"])


def _attn_kernel(*refs, has_ctx):
    if has_ctx:
        q_ref, k_ref, v_ref, kc_ref, vc_ref, o_ref = refs
    else:
        q_ref, k_ref, v_ref, o_ref = refs
    scale = MLA_QK ** -0.5
    q = q_ref[...]
    s = _mm_nt(q, k_ref[...]) * scale
    m = jnp.max(s, axis=-1, keepdims=True)
    if has_ctx:
        sc = _mm_nt(q, kc_ref[...]) * scale
        m = jnp.maximum(m, jnp.max(sc, axis=-1, keepdims=True))
    p = jnp.exp(s - m)
    den = jnp.sum(p, axis=-1, keepdims=True)
    o = _mm(p, v_ref[...])
    if has_ctx:
        pc = jnp.exp(sc - m)
        den = den + jnp.sum(pc, axis=-1, keepdims=True)
        o = o + _mm(pc, vc_ref[...])
    o_ref[...] = (o / den).astype(BF16)


def _attention(q, k, v, batch, seq, ctx_kv):
    tq = min(Q_TILE, seq)
    nq = seq // tq
    has_ctx = ctx_kv is not None
    qmap = lambda b, h, i: (b * nq + i, h)
    kvmap = lambda b, h, i: (b, h)
    in_specs = [pl.BlockSpec((tq, HEAD_PAD), qmap),
                pl.BlockSpec((seq, HEAD_PAD), kvmap),
                pl.BlockSpec((seq, MLA_V), kvmap)]
    args = [q, k, v]
    if has_ctx:
        kc, vc = ctx_kv
        past = kc.shape[0] // batch
        in_specs += [pl.BlockSpec((past, HEAD_PAD), kvmap), pl.BlockSpec((past, MLA_V), kvmap)]
        args += [kc, vc]
    return pl.pallas_call(
        functools.partial(_attn_kernel, has_ctx=has_ctx),
        grid=(batch, MLA_HEADS, nq),
        in_specs=in_specs,
        out_specs=pl.BlockSpec((tq, MLA_V), qmap),
        out_shape=jax.ShapeDtypeStruct((batch * seq, MLA_HEADS * MLA_V), BF16),
        compiler_params=_params("arbitrary", "arbitrary", "arbitrary"),
        name="attention",
    )(*args)


def _ret_kernel(*refs, has_s0):
    if has_s0:
        (lg_ref, qf_ref, kf_ref, vf_ref, qb_ref, kb_ref, vb_ref, s0f_ref, s0b_ref,
         of_ref, ob_ref, sf_out, sb_out, sf_scr, sb_scr) = refs
    else:
        (lg_ref, qf_ref, kf_ref, vf_ref, qb_ref, kb_ref, vb_ref,
         of_ref, ob_ref, sf_out, sb_out, sf_scr, sb_scr) = refs
    c = pl.program_id(1)
    cs = RET_CHUNK

    @pl.when(c == 0)
    def _():
        if has_s0:
            sf_scr[...] = s0f_ref[0]
            sb_scr[...] = s0b_ref[0]
        else:
            sf_scr[...] = jnp.zeros_like(sf_scr)
            sb_scr[...] = jnp.zeros_like(sb_scr)

    ii = lax.broadcasted_iota(jnp.int32, (cs, cs), 0).astype(F32)
    jj = lax.broadcasted_iota(jnp.int32, (cs, cs), 1).astype(F32)
    rel = ii - jj
    kscale = RET_DK ** -0.5

    def chunk(q, k, v, s, intra, qdec, kdec, cdec):
        a = _mm_nt(q, k) * intra
        o = _mm(a, v) + _mm(q, s) * qdec
        return o, s * cdec + _mm_tn(k * kdec, v)

    for hd in range(RET_HEADS):
        cols = slice(hd * RET_DK, (hd + 1) * RET_DK)
        lgf = jax.nn.log_sigmoid(jnp.full((1, LANE), lg_ref[0, hd], F32))
        lgb = jax.nn.log_sigmoid(jnp.full((1, LANE), lg_ref[1, hd], F32))

        intra_f = jnp.where(rel >= 0, jnp.exp(jnp.maximum(rel, 0.0) * lgf), 0.0)
        o_f, s_f = chunk(qf_ref[:, cols], kf_ref[:, cols] * kscale, vf_ref[:, cols], sf_scr[hd], intra_f,
                         jnp.exp((ii + 1.0) * lgf), jnp.exp((cs - 1.0 - ii) * lgf), jnp.exp(cs * lgf))
        of_ref[:, cols] = o_f
        sf_scr[hd] = s_f
        sf_out[0, hd] = s_f

        intra_b = jnp.where(rel <= 0, jnp.exp(jnp.maximum(-rel, 0.0) * lgb), 0.0)
        o_b, s_b = chunk(qb_ref[:, cols], kb_ref[:, cols] * kscale, vb_ref[:, cols], sb_scr[hd], intra_b,
                         jnp.exp((cs - ii) * lgb), jnp.exp(ii * lgb), jnp.exp(cs * lgb))
        ob_ref[:, cols] = o_b
        sb_scr[hd] = s_b
        sb_out[0, hd] = s_b


def _retention(zr, decay_logit, batch, seq, s0):
    cs = RET_CHUNK
    nc = seq // cs
    nh = RET_HEADS
    width = nh * RET_DK
    has_s0 = s0 is not None

    def fwd(col):
        return pl.BlockSpec((cs, width), lambda b, c: (b * nc + c, col))

    def bwd(col):
        return pl.BlockSpec((cs, width), lambda b, c: (b * nc + nc - 1 - c, col))

    def state(d):
        return pl.BlockSpec((1, nh, RET_DK, RET_DV), lambda b, c: (b * 2 + d, 0, 0, 0))

    state_out = pl.BlockSpec((1, nh, RET_DK, RET_DV), lambda b, c: (b, 0, 0, 0))
    in_specs = [pl.BlockSpec(memory_space=pltpu.SMEM), fwd(0), fwd(1), fwd(2), bwd(0), bwd(1), bwd(2)]
    args = [decay_logit, zr, zr, zr, zr, zr, zr]
    if has_s0:
        in_specs += [state(0), state(1)]
        args += [s0, s0]
    t = batch * seq
    o_f, o_b, s_f, s_b = pl.pallas_call(
        functools.partial(_ret_kernel, has_s0=has_s0),
        grid=(batch, nc),
        in_specs=in_specs,
        out_specs=[pl.BlockSpec((cs, width), lambda b, c: (b * nc + c, 0)),
                   pl.BlockSpec((cs, width), lambda b, c: (b * nc + nc - 1 - c, 0)),
                   state_out, state_out],
        out_shape=[jax.ShapeDtypeStruct((t, width), F32),
                   jax.ShapeDtypeStruct((t, width), F32),
                   jax.ShapeDtypeStruct((batch, nh, RET_DK, RET_DV), F32),
                   jax.ShapeDtypeStruct((batch, nh, RET_DK, RET_DV), F32)],
        scratch_shapes=[pltpu.VMEM((nh, RET_DK, RET_DV), F32), pltpu.VMEM((nh, RET_DK, RET_DV), F32)],
        compiler_params=_params("arbitrary", "arbitrary"),
        name="retention",
    )(*args)
    return o_f, o_b, s_f, s_b


def _mix_kernel(x_ref, att_ref, of_ref, ob_ref, gr_ref, mod_ref, gnw_ref, wout_ref, n2w_ref, x1_ref, h2_ref):
    o = of_ref[...] + ob_ref[...]
    parts = []
    for hd in range(RET_HEADS):
        oh = o[:, hd * RET_DV:(hd + 1) * RET_DV]
        d = oh - jnp.mean(oh, axis=-1, keepdims=True)
        parts.append(d * lax.rsqrt(jnp.mean(d * d, axis=-1, keepdims=True) + EPS))
    g = gr_ref[...]
    ret = (g * jax.nn.sigmoid(g)) * (jnp.concatenate(parts, axis=-1) * gnw_ref[...])
    na = MLA_HEADS * MLA_V
    mixed = _mm(att_ref[...], wout_ref[:na, :]) + _mm(ret, wout_ref[na:, :])
    m = mod_ref[0]
    x1 = x_ref[...] + m[2:3] * mixed
    x1_ref[...] = x1
    h2_ref[...] = _rms(x1, n2w_ref[...]) * (1.0 + m[4:5]) + m[3:4]


def _mix(x, att, o_f, o_b, zr, mod, w, seq):
    t = x.shape[0]
    tm = ROW_TILE
    per_batch = seq // tm
    const = lambda i: (0, 0)
    row = lambda i: (i, 0)
    half = RET_HEADS * RET_DV
    return pl.pallas_call(
        _mix_kernel,
        grid=(t // tm,),
        in_specs=[pl.BlockSpec((tm, D_MODEL), row),
                  pl.BlockSpec((tm, half), row),
                  pl.BlockSpec((tm, half), row),
                  pl.BlockSpec((tm, half), row),
                  pl.BlockSpec((tm, half), lambda i: (i, 3)),
                  pl.BlockSpec((1, 6, D_MODEL), lambda i: (i // per_batch, 0, 0)),
                  pl.BlockSpec((1, half), const),
                  pl.BlockSpec((D_MODEL, D_MODEL), const),
                  pl.BlockSpec((1, D_MODEL), const)],
        out_specs=[pl.BlockSpec((tm, D_MODEL), row), pl.BlockSpec((tm, D_MODEL), row)],
        out_shape=[jax.ShapeDtypeStruct((t, D_MODEL), F32), jax.ShapeDtypeStruct((t, D_MODEL), F32)],
        compiler_params=_params("arbitrary"),
        name="mix",
    )(x, att, o_f, o_b, zr, mod, w["ret_gn_w"], w["w_out"], w["norm2_w"])


def _top16(jobs):
    def body(r, carry):
        for s_ref, vals_ref, pay_ref, payload in jobs:
            n, cols = s_ref.shape
            rows = lax.broadcasted_iota(jnp.int32, (n, cols), 0).astype(F32)
            s = s_ref[...]
            m = jnp.max(s, axis=0, keepdims=True)
            pos = jnp.min(jnp.where(s == m, rows, float(n)), axis=0, keepdims=True)
            hit = rows == pos
            vals_ref[pl.ds(r, 1), :] = m
            if payload is None:
                pay_ref[pl.ds(r, 1), :] = pos
            else:
                pay_ref[pl.ds(r, 1), :] = jnp.sum(jnp.where(hit, payload, 0.0), axis=0, keepdims=True)
            s_ref[...] = jnp.where(hit, -jnp.inf, s)
        return carry

    lax.fori_loop(0, PEER_TOPK, body, 0)


PAIR_COUNTS = tuple(PEER_TOPK // (a + 1) for a in range(PEER_TOPK))
NUM_PAIRS = sum(PAIR_COUNTS)
PAIR_ROWS = -(-NUM_PAIRS // 8) * 8


HEAD_GROUP = 2
SCORE_REFS = 10


def _score_heads(hbt, wpqts, k1, k2, g_outs, e_outs, scr):
    heads = [scr[h * SCORE_REFS:(h + 1) * SCORE_REFS] for h in range(len(wpqts))]
    for wpqt, (s1_scr, s2_scr, *_) in zip(wpqts, heads):
        qt = jnp.dot(wpqt, hbt, preferred_element_type=F32)
        s1_scr[...] = _mm(k1, qt[:PEER_HALF, :])
        s2_scr[...] = _mm(k2, qt[PEER_HALF:, :])
    _top16([job for (s1, s2, _, _, v1, i1, v2, i2, _, _) in heads for job in ((s1, v1, i1, None), (s2, v2, i2, None))])
    for (_, _, c_scr, p_scr, v1_scr, i1_scr, v2_scr, i2_scr, _, _) in heads:
        c_scr[...] = jnp.full(c_scr.shape, -jnp.inf, F32)
        p_scr[...] = jnp.zeros(p_scr.shape, F32)
        off = 0
        for a, nb in enumerate(PAIR_COUNTS):
            c_scr[off:off + nb, :] = v1_scr[a:a + 1, :] + v2_scr[0:nb, :]
            p_scr[off:off + nb, :] = i1_scr[a:a + 1, :] * float(N_KEYS) + i2_scr[0:nb, :]
            off += nb
    _top16([(c_scr, vt_scr, it_scr, p_scr[...]) for (_, _, c_scr, p_scr, _, _, _, _, vt_scr, it_scr) in heads])
    for (*_, vt_scr, it_scr), g_out, e_out in zip(heads, g_outs, e_outs):
        top = vt_scr[...]
        p = jnp.exp(top - jnp.max(top, axis=0, keepdims=True))
        g_out[...] = p / jnp.sum(p, axis=0, keepdims=True)
        e_out[...] = it_scr[...]


def _score_scratch(tb):
    k = PEER_TOPK
    one = ([pltpu.VMEM((N_KEYS, tb), F32)] * 2 + [pltpu.VMEM((PAIR_ROWS, tb), F32)] * 2
           + [pltpu.VMEM((k, tb), F32)] * 6)
    assert len(one) == SCORE_REFS
    return one * HEAD_GROUP


def _score_group(hbt, wpq_ref, k1_ref, k2_ref, g_ref, e_ref, grp, scr):
    wpqts, g_outs, e_outs = [], [], []
    for h in range(HEAD_GROUP):
        hd = grp * HEAD_GROUP + h
        head_rows = pl.ds(pl.multiple_of(hd * PEER_TOPK, PEER_TOPK), PEER_TOPK)
        wpqts.append(wpq_ref[hd])
        g_outs.append(g_ref.at[head_rows])
        e_outs.append(e_ref.at[head_rows])
    _score_heads(hbt, wpqts, k1_ref[...], k2_ref[...], g_outs, e_outs, scr)


def _peer_kernel(tab_hbm, h2_ref, h2n_ref, wpq_ref, k1_ref, k2_ref, x1_ref, mod_ref, *rest, ntiles, aliased):
    y_ref, idx_smem, g_scr, e_scr, ei_scr, hb_scr = rest[int(aliased):int(aliased) + 6]
    scratch = rest[int(aliased) + 6:]
    _peer_body(tab_hbm, h2_ref, h2n_ref, wpq_ref, k1_ref, k2_ref, x1_ref, mod_ref, y_ref,
               idx_smem, g_scr, e_scr, ei_scr, hb_scr, *scratch, ntiles=ntiles)


def _peer_body(tab_hbm, h2_ref, h2n_ref, wpq_ref, k1_ref, k2_ref, x1_ref, mod_ref, y_ref,
               idx_smem, g_scr, e_scr, ei_scr, hb_scr, *scratch, ntiles):
    bufs = scratch[:GATHER_SLOTS]
    sem_rows, sem_idx = scratch[GATHER_SLOTS:GATHER_SLOTS + 2]
    score_scr = scratch[GATHER_SLOTS + 2:]
    i = pl.program_id(0)
    tb = GATHER_TILE
    ns = GATHER_SLOTS
    ahead = GATHER_AHEAD
    ne = EXPERTS_PER_TOKEN
    slab = SLAB_ROWS
    nphase = PEER_HEADS // HEAD_GROUP
    groups_per_phase = tb // (ns * nphase)
    cur = i % 2
    nxt = 1 - cur
    more = i + 1 < ntiles

    def score(src_ref, grp, slot):
        @pl.when(grp == 0)
        def _():
            hb_scr[...] = src_ref[...].T.astype(BF16)
        _score_group(hb_scr[...], wpq_ref, k1_ref, k2_ref, g_scr.at[slot], e_scr, grp, score_scr)

    def publish(slot):
        ei_scr[...] = e_scr[...].T.astype(jnp.int32)
        copy = pltpu.make_async_copy(ei_scr, idx_smem.at[slot], sem_idx.at[0])
        copy.start()
        copy.wait()

    def issue(islot, tok, rslot):
        for kk in range(ne):
            ex = idx_smem[islot, tok, kk]
            pltpu.make_async_copy(tab_hbm.at[ex], bufs[rslot].at[pl.ds(kk * slab, slab), :],
                                  sem_rows.at[rslot]).start(priority=kk % 2)

    def wait_rows(rslot):
        pltpu.make_async_copy(bufs[rslot], bufs[rslot], sem_rows.at[rslot]).wait()

    @pl.when(i == 0)
    def _():
        def first(grp, carry):
            score(h2_ref, grp, 0)
            return carry
        lax.fori_loop(0, nphase, first, 0)
        publish(0)
        for tok in range(ahead):
            issue(0, tok, tok)

    g2 = mod_ref[0][5:6]
    lane = lax.broadcasted_iota(jnp.int32, (ne, tb), 1)

    def compute(j, rslot):
        rows = bufs[rslot]
        x_row = h2_ref[pl.ds(j, 1), :]
        acc = None
        for c in range(slab):
            u_c = _unpack_u(rows[pl.ds(c, ne, stride=slab), :])
            term = u_c * x_row[:, c * LANE:(c + 1) * LANE]
            acc = term if acc is None else acc + term
        pre = jnp.sum(acc, axis=-1, keepdims=True)
        act = 0.5 * pre * (1.0 + lax.erf(pre * (2.0 ** -0.5)))
        gate = jnp.sum(jnp.where(lane == j, g_scr[cur], 0.0), axis=-1, keepdims=True)
        wgt = gate * act
        outs = []
        for c in range(slab):
            v_c = _unpack_v(rows[pl.ds(c, ne, stride=slab), :])
            outs.append(jnp.sum(v_c * wgt, axis=0, keepdims=True))
        out = jnp.concatenate(outs, axis=-1)
        y_ref[pl.ds(j, 1), :] = x1_ref[pl.ds(j, 1), :] + g2 * out

    def phase(grp, carry):
        score(h2n_ref, grp, nxt)

        @pl.when(grp == nphase - 1)
        def _():
            publish(nxt)

        def group(gq, inner):
            g = grp * groups_per_phase + gq
            for u in range(ns):
                j = g * ns + u
                wait_rows(u)
                jj = j + ahead
                over = jj >= tb
                issue(jnp.where(over, nxt, cur), jnp.where(over, jj - tb, jj), (u + ahead) % ns)
                compute(j, u)
            return inner

        lax.fori_loop(0, groups_per_phase, group, 0)
        return carry

    lax.fori_loop(0, nphase, phase, 0)

    @pl.when(jnp.logical_not(more))
    def _():
        for u in range(ahead):
            wait_rows(u)


def _peer(tab, h2, x1, mod, w, seq, tile0, ntiles, y_prev=None):
    t = h2.shape[0]
    tb = GATHER_TILE
    per_batch = seq // tb
    row = lambda i: (tile0 + i, 0)
    const = lambda i: (0, 0)
    in_specs = [pl.BlockSpec(memory_space=pl.ANY),
                pl.BlockSpec((tb, D_MODEL), row),
                pl.BlockSpec((tb, D_MODEL), lambda i: (tile0 + jnp.minimum(i + 1, ntiles - 1), 0)),
                pl.BlockSpec((PEER_HEADS, PEER_QDIM, D_MODEL), lambda i: (0, 0, 0)),
                pl.BlockSpec((N_KEYS, PEER_HALF), const),
                pl.BlockSpec((N_KEYS, PEER_HALF), const),
                pl.BlockSpec((tb, D_MODEL), row),
                pl.BlockSpec((1, 6, D_MODEL), lambda i: ((tile0 + i) // per_batch, 0, 0))]
    args = [tab, h2, h2, w["w_pq"], w["sub_keys1"], w["sub_keys2"], x1, mod]
    aliases = {}
    if y_prev is not None:
        in_specs.append(pl.BlockSpec(memory_space=pl.ANY))
        args.append(y_prev)
        aliases = {len(args) - 1: 0}
    return pl.pallas_call(
        functools.partial(_peer_kernel, ntiles=ntiles, aliased=y_prev is not None),
        grid=(ntiles,),
        in_specs=in_specs,
        input_output_aliases=aliases,
        out_specs=pl.BlockSpec((tb, D_MODEL), row),
        out_shape=jax.ShapeDtypeStruct((t, D_MODEL), F32),
        scratch_shapes=[pltpu.SMEM((2, tb, EXPERTS_PER_TOKEN), jnp.int32),
                        pltpu.VMEM((2, EXPERTS_PER_TOKEN, tb), F32),
                        pltpu.VMEM((EXPERTS_PER_TOKEN, tb), F32),
                        pltpu.VMEM((tb, EXPERTS_PER_TOKEN), jnp.int32),
                        pltpu.VMEM((D_MODEL, tb), BF16)]
                       + [pltpu.VMEM((EXPERTS_PER_TOKEN * SLAB_ROWS, LANE), jnp.uint32)] * GATHER_SLOTS
                       + [pltpu.SemaphoreType.DMA((GATHER_SLOTS,)), pltpu.SemaphoreType.DMA((1,))]
                       + _score_scratch(tb),
        compiler_params=pltpu.CompilerParams(dimension_semantics=("arbitrary",),
                                             vmem_limit_bytes=VMEM_LIMIT,
                                             disable_bounds_checks=True),
        name="peer",
    )(*args)


def _score_kernel(h2_ref, wpq_ref, k1_ref, k2_ref, *rest, ordered):
    g_ref, e_ref, hb_scr = rest[int(ordered):int(ordered) + 3]
    score_scr = rest[int(ordered) + 3:]
    hb_scr[...] = h2_ref[...].T.astype(BF16)

    def group(grp, carry):
        _score_group(hb_scr[...], wpq_ref, k1_ref, k2_ref, g_ref, e_ref, grp, score_scr)
        return carry

    lax.fori_loop(0, PEER_HEADS // HEAD_GROUP, group, 0)


def _peer_score(h2, w, tile0, ntiles, after=None):
    tb = GATHER_TILE
    const = lambda i: (0, 0)
    out = pl.BlockSpec((EXPERTS_PER_TOKEN, tb), lambda i: (0, i))
    ordering = [] if after is None else [after]
    return pl.pallas_call(
        functools.partial(_score_kernel, ordered=after is not None),
        grid=(ntiles,),
        in_specs=[pl.BlockSpec((tb, D_MODEL), lambda i: (tile0 + i, 0)),
                  pl.BlockSpec((PEER_HEADS, PEER_QDIM, D_MODEL), lambda i: (0, 0, 0)),
                  pl.BlockSpec((N_KEYS, PEER_HALF), const),
                  pl.BlockSpec((N_KEYS, PEER_HALF), const)] + [pl.BlockSpec(memory_space=pl.ANY)] * len(ordering),
        out_specs=[out, out],
        out_shape=[jax.ShapeDtypeStruct((EXPERTS_PER_TOKEN, ntiles * tb), F32)] * 2,
        scratch_shapes=[pltpu.VMEM((D_MODEL, tb), BF16)] + _score_scratch(tb),
        compiler_params=_params("arbitrary"),
        name="peer_score",
    )(h2, w["w_pq"], w["sub_keys1"], w["sub_keys2"], *ordering)


def _act_kernel(pre_ref, g_ref, after_ref, w_ref):
    del after_ref
    pre = pre_ref[...]
    w_ref[...] = g_ref[...] * (0.5 * pre * (1.0 + lax.erf(pre * (2.0 ** -0.5))))


def _peer_act(pre, gate, after):
    t, ne = pre.shape
    tm = math.gcd(t, 1024)
    row = lambda i: (i, 0)
    return pl.pallas_call(
        _act_kernel,
        grid=(t // tm,),
        in_specs=[pl.BlockSpec((tm, ne), row), pl.BlockSpec((tm, ne), row), pl.BlockSpec(memory_space=pl.ANY)],
        out_specs=pl.BlockSpec((tm, ne), row),
        out_shape=jax.ShapeDtypeStruct((t, ne), F32),
        compiler_params=_params("arbitrary"),
        name="peer_act",
    )(pre, gate, after)


def _residual_kernel(x1_ref, out_ref, mod_ref, *rest):
    rest[-1][...] = x1_ref[...] + mod_ref[0][5:6] * out_ref[...]


def _peer_residual(x1, out, mod, seq, tile0, ntiles, out_tile0, y_prev):
    tb = GATHER_TILE
    per_batch = seq // tb
    row = lambda i: (tile0 + i, 0)
    in_specs = [pl.BlockSpec((tb, D_MODEL), row),
                pl.BlockSpec((tb, D_MODEL), lambda i: (out_tile0 + i, 0)),
                pl.BlockSpec((1, 6, D_MODEL), lambda i: ((tile0 + i) // per_batch, 0, 0))]
    args = [x1, out, mod]
    aliases = {}
    if y_prev is not None:
        in_specs.append(pl.BlockSpec(memory_space=pl.ANY))
        args.append(y_prev)
        aliases = {3: 0}
    return pl.pallas_call(
        _residual_kernel,
        grid=(ntiles,),
        in_specs=in_specs,
        input_output_aliases=aliases,
        out_specs=pl.BlockSpec((tb, D_MODEL), row),
        out_shape=jax.ShapeDtypeStruct(x1.shape, F32),
        compiler_params=_params("arbitrary"),
        name="peer_residual",
    )(*args)


SC_WORKERS = 32
SC_LANES = 16
SC_CHUNK = 32
SC_ROWS = 8
SC_SHARE = (107, 256)
TC_FIRST = (68, 149)


def _sc_udot(u_table, idx, x, row0=0):
    t, ne = idx.shape
    assert t % SC_WORKERS == 0 and ne % SC_CHUNK == 0, (t, ne)
    per_w = t // SC_WORKERS
    nchunk = ne // SC_CHUNK
    nl = D_MODEL // SC_LANES
    mesh = plsc.VectorSubcoreMesh(core_axis_name="c", subcore_axis_name="s")

    @functools.partial(
        pl.kernel, out_type=jax.ShapeDtypeStruct((t, ne), F32), mesh=mesh,
        scratch_types=[pltpu.VMEM((ne,), jnp.int32), pltpu.VMEM((D_MODEL,), F32),
                       pltpu.VMEM((2, SC_CHUNK, D_MODEL), F32), pltpu.VMEM((SC_CHUNK, SC_LANES), F32),
                       pltpu.VMEM((ne,), F32), pltpu.SemaphoreType.DMA((2,))],
        compiler_params=pltpu.CompilerParams(needs_layout_passes=False),
        name="sc_udot")
    def run(u_hbm, idx_hbm, x_hbm, pre_hbm, idx_v, x_v, rows, accs, pre_v, sems):
        wid = lax.axis_index("s") * 2 + lax.axis_index("c")
        lanes = lax.iota(jnp.int32, SC_LANES)

        def gather(c, b):
            return pltpu.make_async_copy(u_hbm.at[idx_v.at[pl.ds(c * SC_CHUNK, SC_CHUNK)]], rows.at[b], sems.at[b])

        @pl.loop(0, per_w)
        def _(tt):
            tok = wid * per_w + tt
            pltpu.sync_copy(idx_hbm.at[tok], idx_v)
            pltpu.sync_copy(x_hbm.at[row0 + tok], x_v)
            gather(0, 0).start()
            for c in range(nchunk):
                b = c % 2
                if c + 1 < nchunk:
                    gather(c + 1, 1 - b).start()
                gather(c, b).wait()
                for g in range(SC_CHUNK // SC_ROWS):
                    def body(j, acc):
                        xv = x_v[pl.ds(j * SC_LANES, SC_LANES)]
                        return tuple(acc[q] + rows[b, g * SC_ROWS + q, pl.ds(j * SC_LANES, SC_LANES)] * xv
                                     for q in range(SC_ROWS))
                    acc = plsc.parallel_loop(0, nl, unroll=2,
                                             carry=tuple(jnp.zeros((SC_LANES,), F32) for _ in range(SC_ROWS)))(body)
                    for q in range(SC_ROWS):
                        accs[g * SC_ROWS + q, :] = acc[q]
                for part in range(SC_CHUNK // SC_LANES):
                    tot = jnp.zeros((SC_LANES,), F32)
                    for l in range(SC_LANES):
                        tot = tot + plsc.load_gather(accs, [lanes + part * SC_LANES,
                                                            jnp.full((SC_LANES,), l, jnp.int32)])
                    pre_v[pl.ds(c * SC_CHUNK + part * SC_LANES, SC_LANES)] = tot
            pltpu.sync_copy(pre_v, pre_hbm.at[tok])

    return run(u_table, idx, x)


def _sc_vsum(v_table, idx, wgt):
    t, ne = idx.shape
    assert t % SC_WORKERS == 0 and ne % SC_CHUNK == 0, (t, ne)
    per_w = t // SC_WORKERS
    nchunk = ne // SC_CHUNK
    nacc = 32
    block = nacc * SC_LANES
    mesh = plsc.VectorSubcoreMesh(core_axis_name="c", subcore_axis_name="s")

    @functools.partial(
        pl.kernel, out_type=jax.ShapeDtypeStruct((t, D_MODEL), F32), mesh=mesh,
        scratch_types=[pltpu.VMEM((ne,), jnp.int32), pltpu.VMEM((ne,), F32),
                       pltpu.VMEM((2, SC_CHUNK, D_MODEL), F32), pltpu.VMEM((D_MODEL,), F32),
                       pltpu.SemaphoreType.DMA((2,))],
        compiler_params=pltpu.CompilerParams(needs_layout_passes=False),
        name="sc_vsum")
    def run(v_hbm, idx_hbm, w_hbm, out_hbm, idx_v, w_v, rows, out_v, sems):
        wid = lax.axis_index("s") * 2 + lax.axis_index("c")

        def gather(c, b):
            return pltpu.make_async_copy(v_hbm.at[idx_v.at[pl.ds(c * SC_CHUNK, SC_CHUNK)]], rows.at[b], sems.at[b])

        @pl.loop(0, per_w)
        def _(tt):
            tok = wid * per_w + tt
            pltpu.sync_copy(idx_hbm.at[tok], idx_v)
            pltpu.sync_copy(w_hbm.at[tok], w_v)
            gather(0, 0).start()
            for c in range(nchunk):
                b = c % 2
                if c + 1 < nchunk:
                    gather(c + 1, 1 - b).start()
                gather(c, b).wait()
                for d in range(D_MODEL // block):
                    def body(k, acc):
                        wk = plsc.load_gather(w_v, [jnp.full((SC_LANES,), c * SC_CHUNK, jnp.int32) + k])
                        return tuple(acc[j] + rows[b, k, pl.ds(d * block + j * SC_LANES, SC_LANES)] * wk
                                     for j in range(nacc))
                    if c == 0:
                        init = tuple(jnp.zeros((SC_LANES,), F32) for _ in range(nacc))
                    else:
                        init = tuple(out_v[pl.ds(d * block + j * SC_LANES, SC_LANES)] for j in range(nacc))
                    acc = plsc.parallel_loop(0, SC_CHUNK, carry=init)(body)
                    for j in range(nacc):
                        out_v[pl.ds(d * block + j * SC_LANES, SC_LANES)] = acc[j]
            pltpu.sync_copy(out_v, out_hbm.at[tok])

    return run(v_table, idx, wgt)


def _expert_slabs(u_table, v_table):
    n = u_table.shape[0]
    hi = lax.bitcast_convert_type(u_table.astype(BF16), jnp.uint16).astype(jnp.uint32) << 16
    lo = lax.bitcast_convert_type(v_table.astype(BF16), jnp.uint16).astype(jnp.uint32)
    return (hi | lo).reshape(n, SLAB_ROWS, LANE)


def _unpack_u(words):
    return lax.bitcast_convert_type(words & jnp.uint32(0xFFFF0000), F32)


def _unpack_v(words):
    return lax.bitcast_convert_type(words << 16, F32)


def _rope_tables(seq):
    rows = seq // GRID_W
    r = jnp.repeat(jnp.arange(rows, dtype=F32), GRID_W)
    col = jnp.tile(jnp.arange(GRID_W, dtype=F32), rows)
    nf = ROPE_AXIS // 2
    freqs = jnp.power(ROPE_BASE, -jnp.arange(nf, dtype=F32) / nf)
    ar, ac = r[:, None] * freqs, col[:, None] * freqs
    pad = jnp.zeros((seq, LANE - MLA_ROPE), F32)
    cos = jnp.concatenate([jnp.cos(ar), jnp.cos(ar), jnp.cos(ac), jnp.cos(ac), pad], axis=-1)
    sin = jnp.concatenate([-jnp.sin(ar), jnp.sin(ar), -jnp.sin(ac), jnp.sin(ac), pad], axis=-1)
    return cos, sin


def _prep_weights(norm1_w, w_in, q_norm_w, w_uq, kv_norm_w, w_ukv, q_head_norm_w, k_head_norm_w,
                  ret_gn_w, w_out, norm2_w, w_pq, sub_keys1, sub_keys2):
    cut = Q_RANK + KV_RANK + MLA_ROPE
    w_in_p = jnp.concatenate([w_in[:, :cut], jnp.zeros((D_MODEL, LANE - MLA_ROPE), F32), w_in[:, cut:]], axis=1)
    wq = w_uq.reshape(Q_RANK, MLA_HEADS, MLA_QK)
    wq = jnp.pad(wq, ((0, 0), (0, 0), (0, HEAD_PAD - MLA_QK))).reshape(Q_RANK, MLA_HEADS * HEAD_PAD)

    def head_w(v):
        return jnp.pad(v, (0, HEAD_PAD - MLA_QK)).reshape(1, HEAD_PAD)

    return {
        "norm1_w": norm1_w.reshape(1, -1), "w_in": w_in_p.astype(BF16),
        "q_norm_w": q_norm_w.reshape(1, -1), "kv_norm_w": kv_norm_w.reshape(1, -1),
        "w_uq": wq.astype(BF16), "w_ukv": w_ukv.astype(BF16),
        "q_head_norm_w": head_w(q_head_norm_w), "k_head_norm_w": head_w(k_head_norm_w),
        "ret_gn_w": ret_gn_w.reshape(1, -1), "w_out": w_out.astype(BF16), "norm2_w": norm2_w.reshape(1, -1),
        "w_pq": w_pq.astype(BF16).reshape(D_MODEL, PEER_HEADS, PEER_QDIM).transpose(1, 2, 0), "sub_keys1": sub_keys1.astype(BF16), "sub_keys2": sub_keys2.astype(BF16),
    }


def _trunk(x, mod, w, decay_logit, ctx, after=None):
    batch, seq, _ = x.shape
    x2 = x.reshape(batch * seq, D_MODEL)
    use_rope = ctx is not None
    if use_rope:
        cos, sin = _rope_tables(seq)
    else:
        cos = sin = jnp.zeros((ROW_TILE, LANE), F32)
    ckv, krope, q, k, v, zr = _in_proj(x2, mod, w, cos, sin, seq, use_rope, after)
    if use_rope:
        ckv_c, krope_c, state_c = ctx
        past = ckv_c.shape[1]
        kr_c = jnp.pad(krope_c.reshape(batch * past, MLA_ROPE), ((0, 0), (0, LANE - MLA_ROPE)))
        ctx_kv = _kv_up(ckv_c.reshape(batch * past, KV_RANK), kr_c, w)
        s0 = state_c.reshape(batch * 2, RET_HEADS, RET_DK, RET_DV)
    else:
        ctx_kv, s0 = None, None
    att = _attention(q, k, v, batch, seq, ctx_kv)
    o_f, o_b, s_f, s_b = _retention(zr, decay_logit, batch, seq, s0)
    x1, h2 = _mix(x2, att, o_f, o_b, zr, mod, w, seq)
    states = jnp.stack([s_f, s_b], axis=1)
    return x1, h2, ckv.reshape(batch, seq, KV_RANK), krope.reshape(batch, seq, MLA_ROPE), states


def _sc_select(h2, w, tile0, ntiles, after=None):
    gt, et = _peer_score(h2, w, tile0, ntiles, after)
    return et.T.astype(jnp.int32), gt.T


def _peer_split(ctx_sc, x_ctx, x_lat, mod_ctx, mod_lat, seq_ctx, seq_lat, w, tab):
    (x1c, h2c), (x1l, h2l) = x_ctx, x_lat
    tiles_c = h2c.shape[0] // GATHER_TILE
    tiles_l = h2l.shape[0] // GATHER_TILE
    sc_tiles = tiles_l * SC_SHARE[0] // SC_SHARE[1]
    tc_tiles = tiles_l - sc_tiles
    first = tc_tiles * TC_FIRST[0] // TC_FIRST[1]

    idx_c, gate_c, pre_c = ctx_sc
    wgt_c = _peer_act(pre_c, gate_c, h2l)
    idx_l, gate_l = _sc_select(h2l, w, tc_tiles, sc_tiles, wgt_c)
    pre_l = _sc_udot(w["u_table"], idx_l, h2l, tc_tiles * GATHER_TILE)
    y = _peer(tab, h2l, x1l, mod_lat, w, seq_lat, 0, first)
    idx = jnp.concatenate([idx_c, idx_l], axis=0)
    wgt = jnp.concatenate([wgt_c, _peer_act(pre_l, gate_l, y)], axis=0)
    out = _sc_vsum(w["v_table"], idx, wgt)
    y = _peer(tab, h2l, x1l, mod_lat, w, seq_lat, first, tc_tiles - first, y_prev=y)
    y_lat = _peer_residual(x1l, out, mod_lat, seq_lat, tc_tiles, sc_tiles, tiles_c, y)
    y_ctx = _peer_residual(x1c, out, mod_ctx, seq_ctx, 0, tiles_c, 0, None)
    return y_ctx, y_lat


def kernel(x_prompt, x_sample, c, cache_ckv, cache_krope, state_ret, c_ctx, w_ada, b_ada, norm1_w, w_in,
           q_norm_w, w_uq, kv_norm_w, w_ukv, q_head_norm_w, k_head_norm_w, ret_decay_logit, ret_gn_w,
           w_out, norm2_w, w_pq, sub_keys1, sub_keys2, u_table, v_table):
    depth = w_ada.shape[0]
    nb_ctx = x_prompt.shape[0]
    nb_lat = x_sample.shape[0]
    y_prompt, y_sample = x_prompt, x_sample
    ckv_list, krope_list, ret_list = [], [], []
    for l in range(depth):
        cond_rows = -(-(nb_lat + 1) // 8) * 8
        cond = jnp.concatenate([c, c_ctx[None, :], jnp.zeros((cond_rows - nb_lat - 1, D_MODEL), F32)], axis=0)
        mod = _ada(cond, w_ada[l], b_ada[l])
        mod_lat = mod[:nb_lat].reshape(nb_lat, 6, D_MODEL)
        mod_ctx = jnp.broadcast_to(mod[nb_lat].reshape(1, 6, D_MODEL), (nb_ctx, 6, D_MODEL))
        w = _prep_weights(norm1_w[l], w_in[l], q_norm_w[l], w_uq[l], kv_norm_w[l], w_ukv[l], q_head_norm_w[l],
                          k_head_norm_w[l], ret_gn_w[l], w_out[l], norm2_w[l], w_pq[l], sub_keys1[l], sub_keys2[l])
        tab = _expert_slabs(u_table[l], v_table[l])
        w["u_table"], w["v_table"] = u_table[l], v_table[l]
        x1c, h2c, ckv_l, krope_l, ret_l = _trunk(y_prompt, mod_ctx, w, ret_decay_logit[l], None)
        ckv_list.append(ckv_l)
        krope_list.append(krope_l)
        ret_list.append(ret_l)
        idx_c, gate_c = _sc_select(h2c, w, 0, h2c.shape[0] // GATHER_TILE)
        pre_c = _sc_udot(w["u_table"], idx_c, h2c)
        x1l, h2l, _, _, _ = _trunk(y_sample, mod_lat, w, ret_decay_logit[l],
                                   (cache_ckv[:, l], cache_krope[:, l], state_ret[:, l]), after=idx_c)
        y_ctx, y_lat = _peer_split((idx_c, gate_c, pre_c), (x1c, h2c), (x1l, h2l), mod_ctx, mod_lat,
                                   y_prompt.shape[1], y_sample.shape[1], w, tab)
        y_prompt = y_ctx.reshape(y_prompt.shape)
        y_sample = y_lat.reshape(y_sample.shape)
    return (y_prompt, y_sample, jnp.stack(ckv_list, axis=1), jnp.stack(krope_list, axis=1),
            jnp.stack(ret_list, axis=1))
```

```python
import functools
import math

import jax
import jax.numpy as jnp
from jax import lax
from jax.experimental import pallas as pl
from jax.experimental.pallas import tpu as pltpu
from jax.experimental.pallas import tpu_sc as plsc

F32 = jnp.float32
BF16 = jnp.bfloat16

D_MODEL = 1024
GRID_W = 64
MLA_HEADS = 4
MLA_NOPE = 128
MLA_ROPE = 64
MLA_QK = MLA_NOPE + MLA_ROPE
MLA_V = 128
Q_RANK = 512
KV_RANK = 256
ROPE_AXIS = MLA_ROPE // 2
ROPE_BASE = 10000.0
RET_HEADS = 4
RET_DK = 128
RET_DV = 128
RET_CHUNK = 128
PEER_HEADS = 8
PEER_QDIM = 256
PEER_HALF = PEER_QDIM // 2
N_KEYS = 128
PEER_TOPK = 16
EPS = 1e-6

LANE = 128
HEAD_PAD = 2 * LANE
RET_W = 4 * RET_HEADS * RET_DK
IN_PAD = Q_RANK + KV_RANK + LANE + RET_W
VMEM_LIMIT = 48 * 1024 * 1024

ROW_TILE = 256
Q_TILE = 256
KEY_PARTS = 2
MIN_PART = 512
GATHER_TILE = 128
GATHER_SLOTS = 4
GATHER_AHEAD = GATHER_SLOTS - 1
EXPERTS_PER_TOKEN = PEER_HEADS * PEER_TOPK
SLAB_ROWS = D_MODEL // LANE


def _params(*sem):
    return pltpu.CompilerParams(dimension_semantics=sem, vmem_limit_bytes=VMEM_LIMIT)


def _rms(x, w):
    return x * lax.rsqrt(jnp.mean(x * x, axis=-1, keepdims=True) + EPS) * w


def _mm(a, b):
    return jnp.dot(a.astype(BF16), b.astype(BF16), preferred_element_type=F32)


def _mm_nt(a, b):
    return lax.dot_general(a.astype(BF16), b.astype(BF16), (((1,), (1,)), ((), ())),
                           preferred_element_type=F32)


def _mm_tn(a, b):
    return lax.dot_general(a.astype(BF16), b.astype(BF16), (((0,), (0,)), ((), ())),
                           preferred_element_type=F32)


def _ada_kernel(c_ref, w_ref, b_ref, o_ref):
    c = c_ref[...]
    o_ref[...] = _mm(c * jax.nn.sigmoid(c), w_ref[...]) + b_ref[...]


def _ada(cond, w_ada, b_ada):
    rows, d = cond.shape
    n = w_ada.shape[1]
    tn = 1536
    return pl.pallas_call(
        _ada_kernel,
        grid=(n // tn,),
        in_specs=[pl.BlockSpec((rows, d), lambda j: (0, 0)),
                  pl.BlockSpec((d, tn), lambda j: (0, j)),
                  pl.BlockSpec((1, tn), lambda j: (0, j))],
        out_specs=pl.BlockSpec((rows, tn), lambda j: (0, j)),
        out_shape=jax.ShapeDtypeStruct((rows, n), F32),
        compiler_params=_params("arbitrary"),
        name="ada",
    )(cond, w_ada, b_ada.reshape(1, n))


def _rope_tile(x, cos, sin):
    lane = lax.broadcasted_iota(jnp.int32, x.shape, 1)
    partner = jnp.where((lane % 32) < 16, pltpu.roll(x, LANE - 16, 1), pltpu.roll(x, 16, 1))
    return x * cos + partner * sin


def _kv_heads(kv, kr, khw, cos, sin, use_rope):
    krw = kr * khw[:, LANE:]
    if use_rope:
        krw = _rope_tile(krw, cos, sin)
    ssq_r = jnp.sum(kr * kr, axis=-1, keepdims=True)
    ks, vs = [], []
    for hd in range(MLA_HEADS):
        kn = kv[:, hd * HEAD_PAD: hd * HEAD_PAD + LANE]
        r = lax.rsqrt((jnp.sum(kn * kn, axis=-1, keepdims=True) + ssq_r) / MLA_QK + EPS)
        ks += [kn * r * khw[:, :LANE], krw * r]
        vs.append(kv[:, hd * HEAD_PAD + LANE: (hd + 1) * HEAD_PAD])
    return jnp.concatenate(ks, axis=-1).astype(BF16), jnp.concatenate(vs, axis=-1).astype(BF16)


def _inproj_kernel(x_ref, mod_ref, n1w_ref, win_ref, qnw_ref, kvnw_ref, wuq_ref, wukv_ref, qhw_ref, khw_ref,
                   cos_ref, sin_ref, *rest, use_rope):
    ckv_ref, krope_ref, q_ref, k_ref, v_ref, zr_ref = rest[-6:]
    m = mod_ref[0]
    h = _rms(x_ref[...], n1w_ref[...]) * (1.0 + m[1:2]) + m[0:1]
    z = _mm(h, win_ref[...])
    kr = z[:, Q_RANK + KV_RANK: Q_RANK + KV_RANK + LANE]
    zr_ref[...] = z[:, Q_RANK + KV_RANK + LANE:]
    ckvn = _rms(z[:, Q_RANK: Q_RANK + KV_RANK], kvnw_ref[...])
    ckv_ref[...] = ckvn
    krope_ref[...] = kr[:, :MLA_ROPE]
    cos, sin = cos_ref[...], sin_ref[...]

    q = _mm(_rms(z[:, :Q_RANK], qnw_ref[...]), wuq_ref[...])
    qhw = qhw_ref[...]
    qs = []
    for hd in range(MLA_HEADS):
        qn = q[:, hd * HEAD_PAD: hd * HEAD_PAD + LANE]
        qr = q[:, hd * HEAD_PAD + LANE: (hd + 1) * HEAD_PAD]
        ssq = jnp.sum(qn * qn, axis=-1, keepdims=True) + jnp.sum(qr * qr, axis=-1, keepdims=True)
        r = lax.rsqrt(ssq / MLA_QK + EPS)
        qrw = qr * qhw[:, LANE:]
        if use_rope:
            qrw = _rope_tile(qrw, cos, sin)
        qs += [qn * r * qhw[:, :LANE], qrw * r]
    q_ref[...] = jnp.concatenate(qs, axis=-1).astype(BF16)

    k, v = _kv_heads(_mm(ckvn, wukv_ref[...]), kr, khw_ref[...], cos, sin, use_rope)
    k_ref[...] = k
    v_ref[...] = v


def _in_proj(x, mod, w, cos, sin, seq, use_rope, after=None):
    t = x.shape[0]
    tm = ROW_TILE
    per_batch = seq // tm
    const = lambda i: (0, 0)
    row = lambda i: (i, 0)
    if use_rope:
        pos = lambda i: (i % per_batch, 0)
    else:
        pos = const
    hp = MLA_HEADS * HEAD_PAD
    ordering = [] if after is None else [after]
    return pl.pallas_call(
        functools.partial(_inproj_kernel, use_rope=use_rope),
        grid=(t // tm,),
        in_specs=[pl.BlockSpec((tm, D_MODEL), row),
                  pl.BlockSpec((1, 6, D_MODEL), lambda i: (i // per_batch, 0, 0)),
                  pl.BlockSpec((1, D_MODEL), const),
                  pl.BlockSpec((D_MODEL, IN_PAD), const),
                  pl.BlockSpec((1, Q_RANK), const),
                  pl.BlockSpec((1, KV_RANK), const),
                  pl.BlockSpec((Q_RANK, hp), const),
                  pl.BlockSpec((KV_RANK, hp), const),
                  pl.BlockSpec((1, HEAD_PAD), const),
                  pl.BlockSpec((1, HEAD_PAD), const),
                  pl.BlockSpec((tm, LANE), pos),
                  pl.BlockSpec((tm, LANE), pos)] + [pl.BlockSpec(memory_space=pl.ANY)] * len(ordering),
        out_specs=[pl.BlockSpec((tm, KV_RANK), row),
                   pl.BlockSpec((tm, MLA_ROPE), row),
                   pl.BlockSpec((tm, hp), row),
                   pl.BlockSpec((tm, hp), row),
                   pl.BlockSpec((tm, MLA_HEADS * MLA_V), row),
                   pl.BlockSpec((tm, RET_W), row)],
        out_shape=[jax.ShapeDtypeStruct((t, KV_RANK), F32),
                   jax.ShapeDtypeStruct((t, MLA_ROPE), F32),
                   jax.ShapeDtypeStruct((t, hp), BF16),
                   jax.ShapeDtypeStruct((t, hp), BF16),
                   jax.ShapeDtypeStruct((t, MLA_HEADS * MLA_V), BF16),
                   jax.ShapeDtypeStruct((t, RET_W), F32)],
        compiler_params=_params("arbitrary"),
        name="in_proj",
    )(x, mod, w["norm1_w"], w["w_in"], w["q_norm_w"], w["kv_norm_w"], w["w_uq"], w["w_ukv"],
      w["q_head_norm_w"], w["k_head_norm_w"], cos, sin, *ordering)


def _kvup_kernel(ckv_ref, kr_ref, wukv_ref, khw_ref, k_ref, v_ref):
    k, v = _kv_heads(_mm(ckv_ref[...], wukv_ref[...]), kr_ref[...], khw_ref[...], None, None, False)
    k_ref[...] = k
    v_ref[...] = v


def _kv_up(ckv, kr, w):
    t = ckv.shape[0]
    tm = ROW_TILE
    hp = MLA_HEADS * HEAD_PAD
    const = lambda i: (0, 0)
    row = lambda i: (i, 0)
    return pl.pallas_call(
        _kvup_kernel,
        grid=(t // tm,),
        in_specs=[pl.BlockSpec((tm, KV_RANK), row),
                  pl.BlockSpec((tm, LANE), row),
                  pl.BlockSpec((KV_RANK, hp), const),
                  pl.BlockSpec((1, HEAD_PAD), const)],
        out_specs=[pl.BlockSpec((tm, hp), row),
                   pl.BlockSpec((tm, MLA_HEADS * MLA_V), row)],
        out_shape=[jax.ShapeDtypeStruct((t, hp), BF16),
                   jax.ShapeDtypeStruct((t, MLA_HEADS * MLA_V), BF16)],
        compiler_params=_params("arbitrary"),
        name="kv_up",
    )(ckv, kr, w["w_ukv"], w["k_head_norm_w"])


def _attn_kernel(*refs, has_ctx):
    if has_ctx:
        q_ref, k_ref, v_ref, kc_ref, vc_ref, o_ref = refs
    else:
        q_ref, k_ref, v_ref, o_ref = refs
    scale = MLA_QK ** -0.5
    q = q_ref[...]
    rows = k_ref.shape[0]
    parts = KEY_PARTS if rows % (KEY_PARTS * MIN_PART) == 0 else 1
    step = rows // parts
    blocks = [(k_ref[i * step:(i + 1) * step, :], v_ref[i * step:(i + 1) * step, :]) for i in range(parts)]
    if has_ctx:
        blocks.insert(0, (kc_ref[...], vc_ref[...]))
    m = den = acc = None
    for kb, vb in blocks:
        s = _mm_nt(q, kb) * scale
        bm = jnp.max(s, axis=-1, keepdims=True)
        if m is None:
            m = bm
            p = jnp.exp(s - m)
            den = jnp.sum(p, axis=-1, keepdims=True)
            acc = _mm(p, vb)
        else:
            m_new = jnp.maximum(m, bm)
            keep = jnp.exp(m - m_new)
            p = jnp.exp(s - m_new)
            den = keep * den + jnp.sum(p, axis=-1, keepdims=True)
            acc = keep * acc + _mm(p, vb)
            m = m_new
    o_ref[...] = (acc / den).astype(BF16)


def _attention(q, k, v, batch, seq, ctx_kv):
    tq = min(Q_TILE, seq)
    nq = seq // tq
    has_ctx = ctx_kv is not None
    qmap = lambda b, h, i: (b * nq + i, h)
    kvmap = lambda b, h, i: (b, h)
    in_specs = [pl.BlockSpec((tq, HEAD_PAD), qmap),
                pl.BlockSpec((seq, HEAD_PAD), kvmap),
                pl.BlockSpec((seq, MLA_V), kvmap)]
    args = [q, k, v]
    if has_ctx:
        kc, vc = ctx_kv
        past = kc.shape[0] // batch
        in_specs += [pl.BlockSpec((past, HEAD_PAD), kvmap), pl.BlockSpec((past, MLA_V), kvmap)]
        args += [kc, vc]
    return pl.pallas_call(
        functools.partial(_attn_kernel, has_ctx=has_ctx),
        grid=(batch, MLA_HEADS, nq),
        in_specs=in_specs,
        out_specs=pl.BlockSpec((tq, MLA_V), qmap),
        out_shape=jax.ShapeDtypeStruct((batch * seq, MLA_HEADS * MLA_V), BF16),
        compiler_params=_params("arbitrary", "arbitrary", "arbitrary"),
        name="attention",
    )(*args)


def _ret_kernel(*refs, has_s0):
    if has_s0:
        (lg_ref, qf_ref, kf_ref, vf_ref, qb_ref, kb_ref, vb_ref, s0f_ref, s0b_ref,
         of_ref, ob_ref, sf_out, sb_out, sf_scr, sb_scr) = refs
    else:
        (lg_ref, qf_ref, kf_ref, vf_ref, qb_ref, kb_ref, vb_ref,
         of_ref, ob_ref, sf_out, sb_out, sf_scr, sb_scr) = refs
    c = pl.program_id(1)
    cs = RET_CHUNK

    @pl.when(c == 0)
    def _():
        if has_s0:
            sf_scr[...] = s0f_ref[0]
            sb_scr[...] = s0b_ref[0]
        else:
            sf_scr[...] = jnp.zeros_like(sf_scr)
            sb_scr[...] = jnp.zeros_like(sb_scr)

    ii = lax.broadcasted_iota(jnp.int32, (cs, cs), 0).astype(F32)
    jj = lax.broadcasted_iota(jnp.int32, (cs, cs), 1).astype(F32)
    rel = ii - jj
    kscale = RET_DK ** -0.5

    def chunk(q, k, v, s, intra, qdec, kdec, cdec):
        a = _mm_nt(q, k) * intra
        o = _mm(a, v) + _mm(q, s) * qdec
        return o, s * cdec + _mm_tn(k * kdec, v)

    for hd in range(RET_HEADS):
        cols = slice(hd * RET_DK, (hd + 1) * RET_DK)
        lgf = jax.nn.log_sigmoid(jnp.full((1, LANE), lg_ref[0, hd], F32))
        lgb = jax.nn.log_sigmoid(jnp.full((1, LANE), lg_ref[1, hd], F32))

        intra_f = jnp.where(rel >= 0, jnp.exp(jnp.maximum(rel, 0.0) * lgf), 0.0)
        o_f, s_f = chunk(qf_ref[:, cols], kf_ref[:, cols] * kscale, vf_ref[:, cols], sf_scr[hd], intra_f,
                         jnp.exp((ii + 1.0) * lgf), jnp.exp((cs - 1.0 - ii) * lgf), jnp.exp(cs * lgf))
        of_ref[:, cols] = o_f
        sf_scr[hd] = s_f
        sf_out[0, hd] = s_f

        intra_b = jnp.where(rel <= 0, jnp.exp(jnp.maximum(-rel, 0.0) * lgb), 0.0)
        o_b, s_b = chunk(qb_ref[:, cols], kb_ref[:, cols] * kscale, vb_ref[:, cols], sb_scr[hd], intra_b,
                         jnp.exp((cs - ii) * lgb), jnp.exp(ii * lgb), jnp.exp(cs * lgb))
        ob_ref[:, cols] = o_b
        sb_scr[hd] = s_b
        sb_out[0, hd] = s_b


def _retention(zr, decay_logit, batch, seq, s0):
    cs = RET_CHUNK
    nc = seq // cs
    nh = RET_HEADS
    width = nh * RET_DK
    has_s0 = s0 is not None

    def fwd(col):
        return pl.BlockSpec((cs, width), lambda b, c: (b * nc + c, col))

    def bwd(col):
        return pl.BlockSpec((cs, width), lambda b, c: (b * nc + nc - 1 - c, col))

    def state(d):
        return pl.BlockSpec((1, nh, RET_DK, RET_DV), lambda b, c: (b * 2 + d, 0, 0, 0))

    state_out = pl.BlockSpec((1, nh, RET_DK, RET_DV), lambda b, c: (b, 0, 0, 0))
    in_specs = [pl.BlockSpec(memory_space=pltpu.SMEM), fwd(0), fwd(1), fwd(2), bwd(0), bwd(1), bwd(2)]
    args = [decay_logit, zr, zr, zr, zr, zr, zr]
    if has_s0:
        in_specs += [state(0), state(1)]
        args += [s0, s0]
    t = batch * seq
    o_f, o_b, s_f, s_b = pl.pallas_call(
        functools.partial(_ret_kernel, has_s0=has_s0),
        grid=(batch, nc),
        in_specs=in_specs,
        out_specs=[pl.BlockSpec((cs, width), lambda b, c: (b * nc + c, 0)),
                   pl.BlockSpec((cs, width), lambda b, c: (b * nc + nc - 1 - c, 0)),
                   state_out, state_out],
        out_shape=[jax.ShapeDtypeStruct((t, width), F32),
                   jax.ShapeDtypeStruct((t, width), F32),
                   jax.ShapeDtypeStruct((batch, nh, RET_DK, RET_DV), F32),
                   jax.ShapeDtypeStruct((batch, nh, RET_DK, RET_DV), F32)],
        scratch_shapes=[pltpu.VMEM((nh, RET_DK, RET_DV), F32), pltpu.VMEM((nh, RET_DK, RET_DV), F32)],
        compiler_params=_params("arbitrary", "arbitrary"),
        name="retention",
    )(*args)
    return o_f, o_b, s_f, s_b


def _mix_kernel(x_ref, att_ref, of_ref, ob_ref, gr_ref, mod_ref, gnw_ref, wout_ref, n2w_ref, x1_ref, h2_ref):
    o = of_ref[...] + ob_ref[...]
    parts = []
    for hd in range(RET_HEADS):
        oh = o[:, hd * RET_DV:(hd + 1) * RET_DV]
        d = oh - jnp.mean(oh, axis=-1, keepdims=True)
        parts.append(d * lax.rsqrt(jnp.mean(d * d, axis=-1, keepdims=True) + EPS))
    g = gr_ref[...]
    ret = (g * jax.nn.sigmoid(g)) * (jnp.concatenate(parts, axis=-1) * gnw_ref[...])
    na = MLA_HEADS * MLA_V
    mixed = _mm(att_ref[...], wout_ref[:na, :]) + _mm(ret, wout_ref[na:, :])
    m = mod_ref[0]
    x1 = x_ref[...] + m[2:3] * mixed
    x1_ref[...] = x1
    h2_ref[...] = _rms(x1, n2w_ref[...]) * (1.0 + m[4:5]) + m[3:4]


def _mix(x, att, o_f, o_b, zr, mod, w, seq):
    t = x.shape[0]
    tm = ROW_TILE
    per_batch = seq // tm
    const = lambda i: (0, 0)
    row = lambda i: (i, 0)
    half = RET_HEADS * RET_DV
    return pl.pallas_call(
        _mix_kernel,
        grid=(t // tm,),
        in_specs=[pl.BlockSpec((tm, D_MODEL), row),
                  pl.BlockSpec((tm, half), row),
                  pl.BlockSpec((tm, half), row),
                  pl.BlockSpec((tm, half), row),
                  pl.BlockSpec((tm, half), lambda i: (i, 3)),
                  pl.BlockSpec((1, 6, D_MODEL), lambda i: (i // per_batch, 0, 0)),
                  pl.BlockSpec((1, half), const),
                  pl.BlockSpec((D_MODEL, D_MODEL), const),
                  pl.BlockSpec((1, D_MODEL), const)],
        out_specs=[pl.BlockSpec((tm, D_MODEL), row), pl.BlockSpec((tm, D_MODEL), row)],
        out_shape=[jax.ShapeDtypeStruct((t, D_MODEL), F32), jax.ShapeDtypeStruct((t, D_MODEL), F32)],
        compiler_params=_params("arbitrary"),
        name="mix",
    )(x, att, o_f, o_b, zr, mod, w["ret_gn_w"], w["w_out"], w["norm2_w"])


def _top16(jobs):
    def body(r, carry):
        for s_ref, vals_ref, pay_ref, payload in jobs:
            n, cols = s_ref.shape
            rows = lax.broadcasted_iota(jnp.int32, (n, cols), 0).astype(F32)
            s = s_ref[...]
            m = jnp.max(s, axis=0, keepdims=True)
            pos = jnp.min(jnp.where(s == m, rows, float(n)), axis=0, keepdims=True)
            hit = rows == pos
            vals_ref[pl.ds(r, 1), :] = m
            if payload is None:
                pay_ref[pl.ds(r, 1), :] = pos
            else:
                pay_ref[pl.ds(r, 1), :] = jnp.sum(jnp.where(hit, payload, 0.0), axis=0, keepdims=True)
            s_ref[...] = jnp.where(hit, -jnp.inf, s)
        return carry

    lax.fori_loop(0, PEER_TOPK, body, 0)


PAIR_COUNTS = tuple(PEER_TOPK // (a + 1) for a in range(PEER_TOPK))
NUM_PAIRS = sum(PAIR_COUNTS)
PAIR_ROWS = -(-NUM_PAIRS // 8) * 8


HEAD_GROUP = 2
SCORE_REFS = 10


def _score_heads(hbt, wpqts, k1, k2, g_outs, e_outs, scr):
    heads = [scr[h * SCORE_REFS:(h + 1) * SCORE_REFS] for h in range(len(wpqts))]
    for wpqt, (s1_scr, s2_scr, *_) in zip(wpqts, heads):
        qt = jnp.dot(wpqt, hbt, preferred_element_type=F32)
        s1_scr[...] = _mm(k1, qt[:PEER_HALF, :])
        s2_scr[...] = _mm(k2, qt[PEER_HALF:, :])
    _top16([job for (s1, s2, _, _, v1, i1, v2, i2, _, _) in heads for job in ((s1, v1, i1, None), (s2, v2, i2, None))])
    for (_, _, c_scr, p_scr, v1_scr, i1_scr, v2_scr, i2_scr, _, _) in heads:
        c_scr[...] = jnp.full(c_scr.shape, -jnp.inf, F32)
        p_scr[...] = jnp.zeros(p_scr.shape, F32)
        off = 0
        for a, nb in enumerate(PAIR_COUNTS):
            c_scr[off:off + nb, :] = v1_scr[a:a + 1, :] + v2_scr[0:nb, :]
            p_scr[off:off + nb, :] = i1_scr[a:a + 1, :] * float(N_KEYS) + i2_scr[0:nb, :]
            off += nb
    _top16([(c_scr, vt_scr, it_scr, p_scr[...]) for (_, _, c_scr, p_scr, _, _, _, _, vt_scr, it_scr) in heads])
    for (*_, vt_scr, it_scr), g_out, e_out in zip(heads, g_outs, e_outs):
        top = vt_scr[...]
        p = jnp.exp(top - jnp.max(top, axis=0, keepdims=True))
        g_out[...] = p / jnp.sum(p, axis=0, keepdims=True)
        e_out[...] = it_scr[...]


def _score_scratch(tb):
    k = PEER_TOPK
    one = ([pltpu.VMEM((N_KEYS, tb), F32)] * 2 + [pltpu.VMEM((PAIR_ROWS, tb), F32)] * 2
           + [pltpu.VMEM((k, tb), F32)] * 6)
    assert len(one) == SCORE_REFS
    return one * HEAD_GROUP


def _score_group(hbt, wpq_ref, k1_ref, k2_ref, g_ref, e_ref, grp, scr):
    wpqts, g_outs, e_outs = [], [], []
    for h in range(HEAD_GROUP):
        hd = grp * HEAD_GROUP + h
        head_rows = pl.ds(pl.multiple_of(hd * PEER_TOPK, PEER_TOPK), PEER_TOPK)
        wpqts.append(wpq_ref[hd])
        g_outs.append(g_ref.at[head_rows])
        e_outs.append(e_ref.at[head_rows])
    _score_heads(hbt, wpqts, k1_ref[...], k2_ref[...], g_outs, e_outs, scr)


def _peer_kernel(tab_hbm, h2_ref, h2n_ref, wpq_ref, k1_ref, k2_ref, x1_ref, mod_ref, *rest, ntiles, aliased):
    y_ref, idx_smem, g_scr, e_scr, ei_scr, hb_scr = rest[int(aliased):int(aliased) + 6]
    scratch = rest[int(aliased) + 6:]
    _peer_body(tab_hbm, h2_ref, h2n_ref, wpq_ref, k1_ref, k2_ref, x1_ref, mod_ref, y_ref,
               idx_smem, g_scr, e_scr, ei_scr, hb_scr, *scratch, ntiles=ntiles)


def _peer_body(tab_hbm, h2_ref, h2n_ref, wpq_ref, k1_ref, k2_ref, x1_ref, mod_ref, y_ref,
               idx_smem, g_scr, e_scr, ei_scr, hb_scr, *scratch, ntiles):
    bufs = scratch[:GATHER_SLOTS]
    sem_rows, sem_idx = scratch[GATHER_SLOTS:GATHER_SLOTS + 2]
    score_scr = scratch[GATHER_SLOTS + 2:]
    i = pl.program_id(0)
    tb = GATHER_TILE
    ns = GATHER_SLOTS
    ahead = GATHER_AHEAD
    ne = EXPERTS_PER_TOKEN
    slab = SLAB_ROWS
    nphase = PEER_HEADS // HEAD_GROUP
    groups_per_phase = tb // (ns * nphase)
    cur = i % 2
    nxt = 1 - cur
    more = i + 1 < ntiles

    def score(src_ref, grp, slot):
        @pl.when(grp == 0)
        def _():
            hb_scr[...] = src_ref[...].T.astype(BF16)
        _score_group(hb_scr[...], wpq_ref, k1_ref, k2_ref, g_scr.at[slot], e_scr, grp, score_scr)

    def publish(slot):
        ei_scr[...] = e_scr[...].T.astype(jnp.int32)
        copy = pltpu.make_async_copy(ei_scr, idx_smem.at[slot], sem_idx.at[0])
        copy.start()
        copy.wait()

    def issue(islot, tok, rslot):
        for kk in range(ne):
            ex = idx_smem[islot, tok, kk]
            pltpu.make_async_copy(tab_hbm.at[ex], bufs[rslot].at[pl.ds(kk * slab, slab), :],
                                  sem_rows.at[rslot]).start(priority=kk % 2)

    def wait_rows(rslot):
        pltpu.make_async_copy(bufs[rslot], bufs[rslot], sem_rows.at[rslot]).wait()

    @pl.when(i == 0)
    def _():
        def first(grp, carry):
            score(h2_ref, grp, 0)
            return carry
        lax.fori_loop(0, nphase, first, 0)
        publish(0)
        for tok in range(ahead):
            issue(0, tok, tok)

    g2 = mod_ref[0][5:6]
    lane = lax.broadcasted_iota(jnp.int32, (ne, tb), 1)

    def compute(j, rslot):
        rows = bufs[rslot]
        x_row = h2_ref[pl.ds(j, 1), :]
        acc = None
        for c in range(slab):
            u_c = _unpack_u(rows[pl.ds(c, ne, stride=slab), :])
            term = u_c * x_row[:, c * LANE:(c + 1) * LANE]
            acc = term if acc is None else acc + term
        pre = jnp.sum(acc, axis=-1, keepdims=True)
        act = 0.5 * pre * (1.0 + lax.erf(pre * (2.0 ** -0.5)))
        gate = jnp.sum(jnp.where(lane == j, g_scr[cur], 0.0), axis=-1, keepdims=True)
        wgt = gate * act
        outs = []
        for c in range(slab):
            v_c = _unpack_v(rows[pl.ds(c, ne, stride=slab), :])
            outs.append(jnp.sum(v_c * wgt, axis=0, keepdims=True))
        out = jnp.concatenate(outs, axis=-1)
        y_ref[pl.ds(j, 1), :] = x1_ref[pl.ds(j, 1), :] + g2 * out

    def phase(grp, carry):
        score(h2n_ref, grp, nxt)

        @pl.when(grp == nphase - 1)
        def _():
            publish(nxt)

        def group(gq, inner):
            g = grp * groups_per_phase + gq
            for u in range(ns):
                j = g * ns + u
                wait_rows(u)
                jj = j + ahead
                over = jj >= tb
                issue(jnp.where(over, nxt, cur), jnp.where(over, jj - tb, jj), (u + ahead) % ns)
                compute(j, u)
            return inner

        lax.fori_loop(0, groups_per_phase, group, 0)
        return carry

    lax.fori_loop(0, nphase, phase, 0)

    @pl.when(jnp.logical_not(more))
    def _():
        for u in range(ahead):
            wait_rows(u)


def _peer(tab, h2, x1, mod, w, seq, tile0, ntiles, y_prev=None):
    t = h2.shape[0]
    tb = GATHER_TILE
    per_batch = seq // tb
    row = lambda i: (tile0 + i, 0)
    const = lambda i: (0, 0)
    in_specs = [pl.BlockSpec(memory_space=pl.ANY),
                pl.BlockSpec((tb, D_MODEL), row),
                pl.BlockSpec((tb, D_MODEL), lambda i: (tile0 + jnp.minimum(i + 1, ntiles - 1), 0)),
                pl.BlockSpec((PEER_HEADS, PEER_QDIM, D_MODEL), lambda i: (0, 0, 0)),
                pl.BlockSpec((N_KEYS, PEER_HALF), const),
                pl.BlockSpec((N_KEYS, PEER_HALF), const),
                pl.BlockSpec((tb, D_MODEL), row),
                pl.BlockSpec((1, 6, D_MODEL), lambda i: ((tile0 + i) // per_batch, 0, 0))]
    args = [tab, h2, h2, w["w_pq"], w["sub_keys1"], w["sub_keys2"], x1, mod]
    aliases = {}
    if y_prev is not None:
        in_specs.append(pl.BlockSpec(memory_space=pl.ANY))
        args.append(y_prev)
        aliases = {len(args) - 1: 0}
    return pl.pallas_call(
        functools.partial(_peer_kernel, ntiles=ntiles, aliased=y_prev is not None),
        grid=(ntiles,),
        in_specs=in_specs,
        input_output_aliases=aliases,
        out_specs=pl.BlockSpec((tb, D_MODEL), row),
        out_shape=jax.ShapeDtypeStruct((t, D_MODEL), F32),
        scratch_shapes=[pltpu.SMEM((2, tb, EXPERTS_PER_TOKEN), jnp.int32),
                        pltpu.VMEM((2, EXPERTS_PER_TOKEN, tb), F32),
                        pltpu.VMEM((EXPERTS_PER_TOKEN, tb), F32),
                        pltpu.VMEM((tb, EXPERTS_PER_TOKEN), jnp.int32),
                        pltpu.VMEM((D_MODEL, tb), BF16)]
                       + [pltpu.VMEM((EXPERTS_PER_TOKEN * SLAB_ROWS, LANE), jnp.uint32)] * GATHER_SLOTS
                       + [pltpu.SemaphoreType.DMA((GATHER_SLOTS,)), pltpu.SemaphoreType.DMA((1,))]
                       + _score_scratch(tb),
        compiler_params=pltpu.CompilerParams(dimension_semantics=("arbitrary",),
                                             vmem_limit_bytes=VMEM_LIMIT,
                                             disable_bounds_checks=True),
        name="peer",
    )(*args)


def _score_kernel(h2_ref, wpq_ref, k1_ref, k2_ref, *rest, ordered):
    g_ref, e_ref, hb_scr = rest[int(ordered):int(ordered) + 3]
    score_scr = rest[int(ordered) + 3:]
    hb_scr[...] = h2_ref[...].T.astype(BF16)

    def group(grp, carry):
        _score_group(hb_scr[...], wpq_ref, k1_ref, k2_ref, g_ref, e_ref, grp, score_scr)
        return carry

    lax.fori_loop(0, PEER_HEADS // HEAD_GROUP, group, 0)


def _peer_score(h2, w, tile0, ntiles, after=None):
    tb = GATHER_TILE
    const = lambda i: (0, 0)
    out = pl.BlockSpec((EXPERTS_PER_TOKEN, tb), lambda i: (0, i))
    ordering = [] if after is None else [after]
    return pl.pallas_call(
        functools.partial(_score_kernel, ordered=after is not None),
        grid=(ntiles,),
        in_specs=[pl.BlockSpec((tb, D_MODEL), lambda i: (tile0 + i, 0)),
                  pl.BlockSpec((PEER_HEADS, PEER_QDIM, D_MODEL), lambda i: (0, 0, 0)),
                  pl.BlockSpec((N_KEYS, PEER_HALF), const),
                  pl.BlockSpec((N_KEYS, PEER_HALF), const)] + [pl.BlockSpec(memory_space=pl.ANY)] * len(ordering),
        out_specs=[out, out],
        out_shape=[jax.ShapeDtypeStruct((EXPERTS_PER_TOKEN, ntiles * tb), F32)] * 2,
        scratch_shapes=[pltpu.VMEM((D_MODEL, tb), BF16)] + _score_scratch(tb),
        compiler_params=_params("arbitrary"),
        name="peer_score",
    )(h2, w["w_pq"], w["sub_keys1"], w["sub_keys2"], *ordering)


def _act_kernel(pre_ref, g_ref, after_ref, w_ref):
    del after_ref
    pre = pre_ref[...]
    w_ref[...] = g_ref[...] * (0.5 * pre * (1.0 + lax.erf(pre * (2.0 ** -0.5))))


def _peer_act(pre, gate, after):
    t, ne = pre.shape
    tm = math.gcd(t, 1024)
    row = lambda i: (i, 0)
    return pl.pallas_call(
        _act_kernel,
        grid=(t // tm,),
        in_specs=[pl.BlockSpec((tm, ne), row), pl.BlockSpec((tm, ne), row), pl.BlockSpec(memory_space=pl.ANY)],
        out_specs=pl.BlockSpec((tm, ne), row),
        out_shape=jax.ShapeDtypeStruct((t, ne), F32),
        compiler_params=_params("arbitrary"),
        name="peer_act",
    )(pre, gate, after)


def _residual_kernel(x1_ref, out_ref, mod_ref, *rest):
    rest[-1][...] = x1_ref[...] + mod_ref[0][5:6] * out_ref[...]


def _peer_residual(x1, out, mod, seq, tile0, ntiles, out_tile0, y_prev):
    tb = GATHER_TILE
    per_batch = seq // tb
    row = lambda i: (tile0 + i, 0)
    in_specs = [pl.BlockSpec((tb, D_MODEL), row),
                pl.BlockSpec((tb, D_MODEL), lambda i: (out_tile0 + i, 0)),
                pl.BlockSpec((1, 6, D_MODEL), lambda i: ((tile0 + i) // per_batch, 0, 0))]
    args = [x1, out, mod]
    aliases = {}
    if y_prev is not None:
        in_specs.append(pl.BlockSpec(memory_space=pl.ANY))
        args.append(y_prev)
        aliases = {3: 0}
    return pl.pallas_call(
        _residual_kernel,
        grid=(ntiles,),
        in_specs=in_specs,
        input_output_aliases=aliases,
        out_specs=pl.BlockSpec((tb, D_MODEL), row),
        out_shape=jax.ShapeDtypeStruct(x1.shape, F32),
        compiler_params=_params("arbitrary"),
        name="peer_residual",
    )(*args)


SC_WORKERS = 32
SC_LANES = 16
SC_CHUNK = 32
SC_ROWS = 8
SC_SHARE = (107, 256)
TC_FIRST = (68, 149)


def _sc_udot(u_table, idx, x, row0=0):
    t, ne = idx.shape
    assert t % SC_WORKERS == 0 and ne % SC_CHUNK == 0, (t, ne)
    per_w = t // SC_WORKERS
    nchunk = ne // SC_CHUNK
    nl = D_MODEL // SC_LANES
    mesh = plsc.VectorSubcoreMesh(core_axis_name="c", subcore_axis_name="s")

    @functools.partial(
        pl.kernel, out_type=jax.ShapeDtypeStruct((t, ne), F32), mesh=mesh,
        scratch_types=[pltpu.VMEM((ne,), jnp.int32), pltpu.VMEM((D_MODEL,), F32),
                       pltpu.VMEM((2, SC_CHUNK, D_MODEL), F32), pltpu.VMEM((SC_CHUNK, SC_LANES), F32),
                       pltpu.VMEM((ne,), F32), pltpu.SemaphoreType.DMA((2,))],
        compiler_params=pltpu.CompilerParams(needs_layout_passes=False),
        name="sc_udot")
    def run(u_hbm, idx_hbm, x_hbm, pre_hbm, idx_v, x_v, rows, accs, pre_v, sems):
        wid = lax.axis_index("s") * 2 + lax.axis_index("c")
        lanes = lax.iota(jnp.int32, SC_LANES)

        def gather(c, b):
            return pltpu.make_async_copy(u_hbm.at[idx_v.at[pl.ds(c * SC_CHUNK, SC_CHUNK)]], rows.at[b], sems.at[b])

        @pl.loop(0, per_w)
        def _(tt):
            tok = wid * per_w + tt
            pltpu.sync_copy(idx_hbm.at[tok], idx_v)
            pltpu.sync_copy(x_hbm.at[row0 + tok], x_v)
            gather(0, 0).start()
            for c in range(nchunk):
                b = c % 2
                if c + 1 < nchunk:
                    gather(c + 1, 1 - b).start()
                gather(c, b).wait()
                for g in range(SC_CHUNK // SC_ROWS):
                    def body(j, acc):
                        xv = x_v[pl.ds(j * SC_LANES, SC_LANES)]
                        return tuple(acc[q] + rows[b, g * SC_ROWS + q, pl.ds(j * SC_LANES, SC_LANES)] * xv
                                     for q in range(SC_ROWS))
                    acc = plsc.parallel_loop(0, nl, unroll=2,
                                             carry=tuple(jnp.zeros((SC_LANES,), F32) for _ in range(SC_ROWS)))(body)
                    for q in range(SC_ROWS):
                        accs[g * SC_ROWS + q, :] = acc[q]
                for part in range(SC_CHUNK // SC_LANES):
                    tot = jnp.zeros((SC_LANES,), F32)
                    for l in range(SC_LANES):
                        tot = tot + plsc.load_gather(accs, [lanes + part * SC_LANES,
                                                            jnp.full((SC_LANES,), l, jnp.int32)])
                    pre_v[pl.ds(c * SC_CHUNK + part * SC_LANES, SC_LANES)] = tot
            pltpu.sync_copy(pre_v, pre_hbm.at[tok])

    return run(u_table, idx, x)


def _sc_vsum(v_table, idx, wgt):
    t, ne = idx.shape
    assert t % SC_WORKERS == 0 and ne % SC_CHUNK == 0, (t, ne)
    per_w = t // SC_WORKERS
    nchunk = ne // SC_CHUNK
    block = 16 * SC_LANES
    mesh = plsc.VectorSubcoreMesh(core_axis_name="c", subcore_axis_name="s")

    @functools.partial(
        pl.kernel, out_type=jax.ShapeDtypeStruct((t, D_MODEL), F32), mesh=mesh,
        scratch_types=[pltpu.VMEM((ne,), jnp.int32), pltpu.VMEM((ne,), F32),
                       pltpu.VMEM((2, SC_CHUNK, D_MODEL), F32), pltpu.VMEM((D_MODEL,), F32),
                       pltpu.SemaphoreType.DMA((2,))],
        compiler_params=pltpu.CompilerParams(needs_layout_passes=False),
        name="sc_vsum")
    def run(v_hbm, idx_hbm, w_hbm, out_hbm, idx_v, w_v, rows, out_v, sems):
        wid = lax.axis_index("s") * 2 + lax.axis_index("c")

        def gather(c, b):
            return pltpu.make_async_copy(v_hbm.at[idx_v.at[pl.ds(c * SC_CHUNK, SC_CHUNK)]], rows.at[b], sems.at[b])

        @pl.loop(0, per_w)
        def _(tt):
            tok = wid * per_w + tt
            pltpu.sync_copy(idx_hbm.at[tok], idx_v)
            pltpu.sync_copy(w_hbm.at[tok], w_v)
            gather(0, 0).start()
            for c in range(nchunk):
                b = c % 2
                if c + 1 < nchunk:
                    gather(c + 1, 1 - b).start()
                gather(c, b).wait()
                for d in range(D_MODEL // block):
                    def body(k, acc):
                        wk = plsc.load_gather(w_v, [jnp.full((SC_LANES,), c * SC_CHUNK, jnp.int32) + k])
                        return tuple(acc[j] + rows[b, k, pl.ds(d * block + j * SC_LANES, SC_LANES)] * wk
                                     for j in range(16))
                    if c == 0:
                        init = tuple(jnp.zeros((SC_LANES,), F32) for _ in range(16))
                    else:
                        init = tuple(out_v[pl.ds(d * block + j * SC_LANES, SC_LANES)] for j in range(16))
                    acc = plsc.parallel_loop(0, SC_CHUNK, carry=init)(body)
                    for j in range(16):
                        out_v[pl.ds(d * block + j * SC_LANES, SC_LANES)] = acc[j]
            pltpu.sync_copy(out_v, out_hbm.at[tok])

    return run(v_table, idx, wgt)


def _expert_slabs(u_table, v_table):
    n = u_table.shape[0]
    hi = lax.bitcast_convert_type(u_table.astype(BF16), jnp.uint16).astype(jnp.uint32) << 16
    lo = lax.bitcast_convert_type(v_table.astype(BF16), jnp.uint16).astype(jnp.uint32)
    return (hi | lo).reshape(n, SLAB_ROWS, LANE)


def _unpack_u(words):
    return lax.bitcast_convert_type(words & jnp.uint32(0xFFFF0000), F32)


def _unpack_v(words):
    return lax.bitcast_convert_type(words << 16, F32)


def _rope_tables(seq):
    rows = seq // GRID_W
    r = jnp.repeat(jnp.arange(rows, dtype=F32), GRID_W)
    col = jnp.tile(jnp.arange(GRID_W, dtype=F32), rows)
    nf = ROPE_AXIS // 2
    freqs = jnp.power(ROPE_BASE, -jnp.arange(nf, dtype=F32) / nf)
    ar, ac = r[:, None] * freqs, col[:, None] * freqs
    pad = jnp.zeros((seq, LANE - MLA_ROPE), F32)
    cos = jnp.concatenate([jnp.cos(ar), jnp.cos(ar), jnp.cos(ac), jnp.cos(ac), pad], axis=-1)
    sin = jnp.concatenate([-jnp.sin(ar), jnp.sin(ar), -jnp.sin(ac), jnp.sin(ac), pad], axis=-1)
    return cos, sin


def _prep_weights(norm1_w, w_in, q_norm_w, w_uq, kv_norm_w, w_ukv, q_head_norm_w, k_head_norm_w,
                  ret_gn_w, w_out, norm2_w, w_pq, sub_keys1, sub_keys2):
    cut = Q_RANK + KV_RANK + MLA_ROPE
    w_in_p = jnp.concatenate([w_in[:, :cut], jnp.zeros((D_MODEL, LANE - MLA_ROPE), F32), w_in[:, cut:]], axis=1)
    wq = w_uq.reshape(Q_RANK, MLA_HEADS, MLA_QK)
    wq = jnp.pad(wq, ((0, 0), (0, 0), (0, HEAD_PAD - MLA_QK))).reshape(Q_RANK, MLA_HEADS * HEAD_PAD)

    def head_w(v):
        return jnp.pad(v, (0, HEAD_PAD - MLA_QK)).reshape(1, HEAD_PAD)

    return {
        "norm1_w": norm1_w.reshape(1, -1), "w_in": w_in_p.astype(BF16),
        "q_norm_w": q_norm_w.reshape(1, -1), "kv_norm_w": kv_norm_w.reshape(1, -1),
        "w_uq": wq.astype(BF16), "w_ukv": w_ukv.astype(BF16),
        "q_head_norm_w": head_w(q_head_norm_w), "k_head_norm_w": head_w(k_head_norm_w),
        "ret_gn_w": ret_gn_w.reshape(1, -1), "w_out": w_out.astype(BF16), "norm2_w": norm2_w.reshape(1, -1),
        "w_pq": w_pq.astype(BF16).reshape(D_MODEL, PEER_HEADS, PEER_QDIM).transpose(1, 2, 0), "sub_keys1": sub_keys1.astype(BF16), "sub_keys2": sub_keys2.astype(BF16),
    }


def _trunk(x, mod, w, decay_logit, ctx, after=None):
    batch, seq, _ = x.shape
    x2 = x.reshape(batch * seq, D_MODEL)
    use_rope = ctx is not None
    if use_rope:
        cos, sin = _rope_tables(seq)
    else:
        cos = sin = jnp.zeros((ROW_TILE, LANE), F32)
    ckv, krope, q, k, v, zr = _in_proj(x2, mod, w, cos, sin, seq, use_rope, after)
    if use_rope:
        ckv_c, krope_c, state_c = ctx
        past = ckv_c.shape[1]
        kr_c = jnp.pad(krope_c.reshape(batch * past, MLA_ROPE), ((0, 0), (0, LANE - MLA_ROPE)))
        ctx_kv = _kv_up(ckv_c.reshape(batch * past, KV_RANK), kr_c, w)
        s0 = state_c.reshape(batch * 2, RET_HEADS, RET_DK, RET_DV)
    else:
        ctx_kv, s0 = None, None
    att = _attention(q, k, v, batch, seq, ctx_kv)
    o_f, o_b, s_f, s_b = _retention(zr, decay_logit, batch, seq, s0)
    x1, h2 = _mix(x2, att, o_f, o_b, zr, mod, w, seq)
    states = jnp.stack([s_f, s_b], axis=1)
    return x1, h2, ckv.reshape(batch, seq, KV_RANK), krope.reshape(batch, seq, MLA_ROPE), states


def _sc_select(h2, w, tile0, ntiles, after=None):
    gt, et = _peer_score(h2, w, tile0, ntiles, after)
    return et.T.astype(jnp.int32), gt.T


def _peer_split(ctx_sc, x_ctx, x_lat, mod_ctx, mod_lat, seq_ctx, seq_lat, w, tab):
    (x1c, h2c), (x1l, h2l) = x_ctx, x_lat
    tiles_c = h2c.shape[0] // GATHER_TILE
    tiles_l = h2l.shape[0] // GATHER_TILE
    sc_tiles = tiles_l * SC_SHARE[0] // SC_SHARE[1]
    tc_tiles = tiles_l - sc_tiles
    first = tc_tiles * TC_FIRST[0] // TC_FIRST[1]

    idx_c, gate_c, pre_c = ctx_sc
    wgt_c = _peer_act(pre_c, gate_c, h2l)
    idx_l, gate_l = _sc_select(h2l, w, tc_tiles, sc_tiles, wgt_c)
    pre_l = _sc_udot(w["u_table"], idx_l, h2l, tc_tiles * GATHER_TILE)
    y = _peer(tab, h2l, x1l, mod_lat, w, seq_lat, 0, first)
    idx = jnp.concatenate([idx_c, idx_l], axis=0)
    wgt = jnp.concatenate([wgt_c, _peer_act(pre_l, gate_l, y)], axis=0)
    out = _sc_vsum(w["v_table"], idx, wgt)
    y = _peer(tab, h2l, x1l, mod_lat, w, seq_lat, first, tc_tiles - first, y_prev=y)
    y_lat = _peer_residual(x1l, out, mod_lat, seq_lat, tc_tiles, sc_tiles, tiles_c, y)
    y_ctx = _peer_residual(x1c, out, mod_ctx, seq_ctx, 0, tiles_c, 0, None)
    return y_ctx, y_lat


def kernel(x_prompt, x_sample, c, cache_ckv, cache_krope, state_ret, c_ctx, w_ada, b_ada, norm1_w, w_in,
           q_norm_w, w_uq, kv_norm_w, w_ukv, q_head_norm_w, k_head_norm_w, ret_decay_logit, ret_gn_w,
           w_out, norm2_w, w_pq, sub_keys1, sub_keys2, u_table, v_table):
    depth = w_ada.shape[0]
    nb_ctx = x_prompt.shape[0]
    nb_lat = x_sample.shape[0]
    y_prompt, y_sample = x_prompt, x_sample
    ckv_list, krope_list, ret_list = [], [], []
    for l in range(depth):
        cond_rows = -(-(nb_lat + 1) // 8) * 8
        cond = jnp.concatenate([c, c_ctx[None, :], jnp.zeros((cond_rows - nb_lat - 1, D_MODEL), F32)], axis=0)
        mod = _ada(cond, w_ada[l], b_ada[l])
        mod_lat = mod[:nb_lat].reshape(nb_lat, 6, D_MODEL)
        mod_ctx = jnp.broadcast_to(mod[nb_lat].reshape(1, 6, D_MODEL), (nb_ctx, 6, D_MODEL))
        w = _prep_weights(norm1_w[l], w_in[l], q_norm_w[l], w_uq[l], kv_norm_w[l], w_ukv[l], q_head_norm_w[l],
                          k_head_norm_w[l], ret_gn_w[l], w_out[l], norm2_w[l], w_pq[l], sub_keys1[l], sub_keys2[l])
        tab = _expert_slabs(u_table[l], v_table[l])
        w["u_table"], w["v_table"] = u_table[l], v_table[l]
        x1c, h2c, ckv_l, krope_l, ret_l = _trunk(y_prompt, mod_ctx, w, ret_decay_logit[l], None)
        ckv_list.append(ckv_l)
        krope_list.append(krope_l)
        ret_list.append(ret_l)
        idx_c, gate_c = _sc_select(h2c, w, 0, h2c.shape[0] // GATHER_TILE)
        pre_c = _sc_udot(w["u_table"], idx_c, h2c)
        x1l, h2l, _, _, _ = _trunk(y_sample, mod_lat, w, ret_decay_logit[l],
                                   (cache_ckv[:, l], cache_krope[:, l], state_ret[:, l]), after=idx_c)
        y_ctx, y_lat = _peer_split((idx_c, gate_c, pre_c), (x1c, h2c), (x1l, h2l), mod_ctx, mod_lat,
                                   y_prompt.shape[1], y_sample.shape[1], w, tab)
        y_prompt = y_ctx.reshape(y_prompt.shape)
        y_sample = y_lat.reshape(y_sample.shape)
    return (y_prompt, y_sample, jnp.stack(ckv_list, axis=1), jnp.stack(krope_list, axis=1),
            jnp.stack(ret_list, axis=1))
```

```python
import functools
import math

import jax
import jax.numpy as jnp
from jax import lax
from jax.experimental import pallas as pl
from jax.experimental.pallas import tpu as pltpu
from jax.experimental.pallas import tpu_sc as plsc

F32 = jnp.float32
BF16 = jnp.bfloat16

D_MODEL = 1024
GRID_W = 64
MLA_HEADS = 4
MLA_NOPE = 128
MLA_ROPE = 64
MLA_QK = MLA_NOPE + MLA_ROPE
MLA_V = 128
Q_RANK = 512
KV_RANK = 256
ROPE_AXIS = MLA_ROPE // 2
ROPE_BASE = 10000.0
RET_HEADS = 4
RET_DK = 128
RET_DV = 128
RET_CHUNK = 128
PEER_HEADS = 8
PEER_QDIM = 256
PEER_HALF = PEER_QDIM // 2
N_KEYS = 128
PEER_TOPK = 16
EPS = 1e-6

LANE = 128
HEAD_PAD = 2 * LANE
RET_W = 4 * RET_HEADS * RET_DK
IN_PAD = Q_RANK + KV_RANK + LANE + RET_W
VMEM_LIMIT = 48 * 1024 * 1024

ROW_TILE = 256
Q_TILE = 256
KEY_PARTS = 2
MIN_PART = 512
GATHER_TILE = 128
GATHER_SLOTS = 4
GATHER_AHEAD = GATHER_SLOTS - 1
EXPERTS_PER_TOKEN = PEER_HEADS * PEER_TOPK
SLAB_ROWS = D_MODEL // LANE


def _params(*sem):
    return pltpu.CompilerParams(dimension_semantics=sem, vmem_limit_bytes=VMEM_LIMIT)


def _rms(x, w):
    return x * lax.rsqrt(jnp.mean(x * x, axis=-1, keepdims=True) + EPS) * w


def _mm(a, b):
    return jnp.dot(a.astype(BF16), b.astype(BF16), preferred_element_type=F32)


def _mm_nt(a, b):
    return lax.dot_general(a.astype(BF16), b.astype(BF16), (((1,), (1,)), ((), ())),
                           preferred_element_type=F32)


def _mm_tn(a, b):
    return lax.dot_general(a.astype(BF16), b.astype(BF16), (((0,), (0,)), ((), ())),
                           preferred_element_type=F32)


def _ada_kernel(c_ref, w_ref, b_ref, o_ref):
    c = c_ref[...]
    o_ref[...] = _mm(c * jax.nn.sigmoid(c), w_ref[...]) + b_ref[...]


def _ada(cond, w_ada, b_ada):
    rows, d = cond.shape
    n = w_ada.shape[1]
    tn = 1536
    return pl.pallas_call(
        _ada_kernel,
        grid=(n // tn,),
        in_specs=[pl.BlockSpec((rows, d), lambda j: (0, 0)),
                  pl.BlockSpec((d, tn), lambda j: (0, j)),
                  pl.BlockSpec((1, tn), lambda j: (0, j))],
        out_specs=pl.BlockSpec((rows, tn), lambda j: (0, j)),
        out_shape=jax.ShapeDtypeStruct((rows, n), F32),
        compiler_params=_params("arbitrary"),
        name="ada",
    )(cond, w_ada, b_ada.reshape(1, n))


def _rope_tile(x, cos, sin):
    lane = lax.broadcasted_iota(jnp.int32, x.shape, 1)
    partner = jnp.where((lane % 32) < 16, pltpu.roll(x, LANE - 16, 1), pltpu.roll(x, 16, 1))
    return x * cos + partner * sin


def _kv_heads(kv, kr, khw, cos, sin, use_rope):
    krw = kr * khw[:, LANE:]
    if use_rope:
        krw = _rope_tile(krw, cos, sin)
    ssq_r = jnp.sum(kr * kr, axis=-1, keepdims=True)
    ks, vs = [], []
    for hd in range(MLA_HEADS):
        kn = kv[:, hd * HEAD_PAD: hd * HEAD_PAD + LANE]
        r = lax.rsqrt((jnp.sum(kn * kn, axis=-1, keepdims=True) + ssq_r) / MLA_QK + EPS)
        ks += [kn * r * khw[:, :LANE], krw * r]
        vs.append(kv[:, hd * HEAD_PAD + LANE: (hd + 1) * HEAD_PAD])
    return jnp.concatenate(ks, axis=-1).astype(BF16), jnp.concatenate(vs, axis=-1).astype(BF16)


def _inproj_kernel(x_ref, mod_ref, n1w_ref, win_ref, qnw_ref, kvnw_ref, wuq_ref, wukv_ref, qhw_ref, khw_ref,
                   cos_ref, sin_ref, *rest, use_rope):
    ckv_ref, krope_ref, q_ref, k_ref, v_ref, zr_ref = rest[-6:]
    m = mod_ref[0]
    h = _rms(x_ref[...], n1w_ref[...]) * (1.0 + m[1:2]) + m[0:1]
    z = _mm(h, win_ref[...])
    kr = z[:, Q_RANK + KV_RANK: Q_RANK + KV_RANK + LANE]
    zr_ref[...] = z[:, Q_RANK + KV_RANK + LANE:]
    ckvn = _rms(z[:, Q_RANK: Q_RANK + KV_RANK], kvnw_ref[...])
    ckv_ref[...] = ckvn
    krope_ref[...] = kr[:, :MLA_ROPE]
    cos, sin = cos_ref[...], sin_ref[...]

    q = _mm(_rms(z[:, :Q_RANK], qnw_ref[...]), wuq_ref[...])
    qhw = qhw_ref[...]
    qs = []
    for hd in range(MLA_HEADS):
        qn = q[:, hd * HEAD_PAD: hd * HEAD_PAD + LANE]
        qr = q[:, hd * HEAD_PAD + LANE: (hd + 1) * HEAD_PAD]
        ssq = jnp.sum(qn * qn, axis=-1, keepdims=True) + jnp.sum(qr * qr, axis=-1, keepdims=True)
        r = lax.rsqrt(ssq / MLA_QK + EPS)
        qrw = qr * qhw[:, LANE:]
        if use_rope:
            qrw = _rope_tile(qrw, cos, sin)
        qs += [qn * r * qhw[:, :LANE], qrw * r]
    q_ref[...] = (jnp.concatenate(qs, axis=-1) * (MLA_QK ** -0.5)).astype(BF16)

    k, v = _kv_heads(_mm(ckvn, wukv_ref[...]), kr, khw_ref[...], cos, sin, use_rope)
    k_ref[...] = k
    v_ref[...] = v


def _in_proj(x, mod, w, cos, sin, seq, use_rope, after=None):
    t = x.shape[0]
    tm = ROW_TILE
    per_batch = seq // tm
    const = lambda i: (0, 0)
    row = lambda i: (i, 0)
    if use_rope:
        pos = lambda i: (i % per_batch, 0)
    else:
        pos = const
    hp = MLA_HEADS * HEAD_PAD
    ordering = [] if after is None else [after]
    return pl.pallas_call(
        functools.partial(_inproj_kernel, use_rope=use_rope),
        grid=(t // tm,),
        in_specs=[pl.BlockSpec((tm, D_MODEL), row),
                  pl.BlockSpec((1, 6, D_MODEL), lambda i: (i // per_batch, 0, 0)),
                  pl.BlockSpec((1, D_MODEL), const),
                  pl.BlockSpec((D_MODEL, IN_PAD), const),
                  pl.BlockSpec((1, Q_RANK), const),
                  pl.BlockSpec((1, KV_RANK), const),
                  pl.BlockSpec((Q_RANK, hp), const),
                  pl.BlockSpec((KV_RANK, hp), const),
                  pl.BlockSpec((1, HEAD_PAD), const),
                  pl.BlockSpec((1, HEAD_PAD), const),
                  pl.BlockSpec((tm, LANE), pos),
                  pl.BlockSpec((tm, LANE), pos)] + [pl.BlockSpec(memory_space=pl.ANY)] * len(ordering),
        out_specs=[pl.BlockSpec((tm, KV_RANK), row),
                   pl.BlockSpec((tm, MLA_ROPE), row),
                   pl.BlockSpec((tm, hp), row),
                   pl.BlockSpec((tm, hp), row),
                   pl.BlockSpec((tm, MLA_HEADS * MLA_V), row),
                   pl.BlockSpec((tm, RET_W), row)],
        out_shape=[jax.ShapeDtypeStruct((t, KV_RANK), F32),
                   jax.ShapeDtypeStruct((t, MLA_ROPE), F32),
                   jax.ShapeDtypeStruct((t, hp), BF16),
                   jax.ShapeDtypeStruct((t, hp), BF16),
                   jax.ShapeDtypeStruct((t, MLA_HEADS * MLA_V), BF16),
                   jax.ShapeDtypeStruct((t, RET_W), F32)],
        compiler_params=_params("arbitrary"),
        name="in_proj",
    )(x, mod, w["norm1_w"], w["w_in"], w["q_norm_w"], w["kv_norm_w"], w["w_uq"], w["w_ukv"],
      w["q_head_norm_w"], w["k_head_norm_w"], cos, sin, *ordering)


def _kvup_kernel(ckv_ref, kr_ref, wukv_ref, khw_ref, k_ref, v_ref):
    k, v = _kv_heads(_mm(ckv_ref[...], wukv_ref[...]), kr_ref[...], khw_ref[...], None, None, False)
    k_ref[...] = k
    v_ref[...] = v


def _kv_up(ckv, kr, w):
    t = ckv.shape[0]
    tm = ROW_TILE
    hp = MLA_HEADS * HEAD_PAD
    const = lambda i: (0, 0)
    row = lambda i: (i, 0)
    return pl.pallas_call(
        _kvup_kernel,
        grid=(t // tm,),
        in_specs=[pl.BlockSpec((tm, KV_RANK), row),
                  pl.BlockSpec((tm, LANE), row),
                  pl.BlockSpec((KV_RANK, hp), const),
                  pl.BlockSpec((1, HEAD_PAD), const)],
        out_specs=[pl.BlockSpec((tm, hp), row),
                   pl.BlockSpec((tm, MLA_HEADS * MLA_V), row)],
        out_shape=[jax.ShapeDtypeStruct((t, hp), BF16),
                   jax.ShapeDtypeStruct((t, MLA_HEADS * MLA_V), BF16)],
        compiler_params=_params("arbitrary"),
        name="kv_up",
    )(ckv, kr, w["w_ukv"], w["k_head_norm_w"])


def _attn_kernel(*refs, has_ctx):
    if has_ctx:
        q_ref, k_ref, v_ref, kc_ref, vc_ref, o_ref = refs
    else:
        q_ref, k_ref, v_ref, o_ref = refs
    q = q_ref[...]
    rows = k_ref.shape[0]
    parts = KEY_PARTS if rows % (KEY_PARTS * MIN_PART) == 0 else 1
    step = rows // parts
    blocks = [(k_ref[i * step:(i + 1) * step, :], v_ref[i * step:(i + 1) * step, :]) for i in range(parts)]
    if has_ctx:
        blocks.insert(0, (kc_ref[...], vc_ref[...]))
    m = den = acc = None
    for kb, vb in blocks:
        s = _mm_nt(q, kb)
        bm = jnp.max(s, axis=-1, keepdims=True)
        if m is None:
            m = bm
            p = jnp.exp(s - m)
            den = jnp.sum(p, axis=-1, keepdims=True)
            acc = _mm(p, vb)
        else:
            m_new = jnp.maximum(m, bm)
            keep = jnp.exp(m - m_new)
            p = jnp.exp(s - m_new)
            den = keep * den + jnp.sum(p, axis=-1, keepdims=True)
            acc = keep * acc + _mm(p, vb)
            m = m_new
    o_ref[...] = (acc / den).astype(BF16)


def _attention(q, k, v, batch, seq, ctx_kv):
    tq = min(Q_TILE, seq)
    nq = seq // tq
    has_ctx = ctx_kv is not None
    qmap = lambda b, h, i: (b * nq + i, h)
    kvmap = lambda b, h, i: (b, h)
    in_specs = [pl.BlockSpec((tq, HEAD_PAD), qmap),
                pl.BlockSpec((seq, HEAD_PAD), kvmap),
                pl.BlockSpec((seq, MLA_V), kvmap)]
    args = [q, k, v]
    if has_ctx:
        kc, vc = ctx_kv
        past = kc.shape[0] // batch
        in_specs += [pl.BlockSpec((past, HEAD_PAD), kvmap), pl.BlockSpec((past, MLA_V), kvmap)]
        args += [kc, vc]
    return pl.pallas_call(
        functools.partial(_attn_kernel, has_ctx=has_ctx),
        grid=(batch, MLA_HEADS, nq),
        in_specs=in_specs,
        out_specs=pl.BlockSpec((tq, MLA_V), qmap),
        out_shape=jax.ShapeDtypeStruct((batch * seq, MLA_HEADS * MLA_V), BF16),
        compiler_params=_params("arbitrary", "arbitrary", "arbitrary"),
        name="attention",
    )(*args)


def _ret_kernel(*refs, has_s0):
    if has_s0:
        (lg_ref, qf_ref, kf_ref, vf_ref, qb_ref, kb_ref, vb_ref, s0f_ref, s0b_ref,
         of_ref, ob_ref, sf_out, sb_out, sf_scr, sb_scr) = refs
    else:
        (lg_ref, qf_ref, kf_ref, vf_ref, qb_ref, kb_ref, vb_ref,
         of_ref, ob_ref, sf_out, sb_out, sf_scr, sb_scr) = refs
    c = pl.program_id(1)
    cs = RET_CHUNK

    @pl.when(c == 0)
    def _():
        if has_s0:
            sf_scr[...] = s0f_ref[0]
            sb_scr[...] = s0b_ref[0]
        else:
            sf_scr[...] = jnp.zeros_like(sf_scr)
            sb_scr[...] = jnp.zeros_like(sb_scr)

    ii = lax.broadcasted_iota(jnp.int32, (cs, cs), 0).astype(F32)
    jj = lax.broadcasted_iota(jnp.int32, (cs, cs), 1).astype(F32)
    rel = ii - jj
    kscale = RET_DK ** -0.5

    def chunk(q, k, v, s, intra, qdec, kdec, cdec):
        a = _mm_nt(q, k) * intra
        o = _mm(a, v) + _mm(q, s) * qdec
        return o, s * cdec + _mm_tn(k * kdec, v)

    for hd in range(RET_HEADS):
        cols = slice(hd * RET_DK, (hd + 1) * RET_DK)
        lgf = jax.nn.log_sigmoid(jnp.full((1, LANE), lg_ref[0, hd], F32))
        lgb = jax.nn.log_sigmoid(jnp.full((1, LANE), lg_ref[1, hd], F32))

        intra_f = jnp.where(rel >= 0, jnp.exp(jnp.maximum(rel, 0.0) * lgf), 0.0)
        o_f, s_f = chunk(qf_ref[:, cols], kf_ref[:, cols] * kscale, vf_ref[:, cols], sf_scr[hd], intra_f,
                         jnp.exp((ii + 1.0) * lgf), jnp.exp((cs - 1.0 - ii) * lgf), jnp.exp(cs * lgf))
        of_ref[:, cols] = o_f
        sf_scr[hd] = s_f
        sf_out[0, hd] = s_f

        intra_b = jnp.where(rel <= 0, jnp.exp(jnp.maximum(-rel, 0.0) * lgb), 0.0)
        o_b, s_b = chunk(qb_ref[:, cols], kb_ref[:, cols] * kscale, vb_ref[:, cols], sb_scr[hd], intra_b,
                         jnp.exp((cs - ii) * lgb), jnp.exp(ii * lgb), jnp.exp(cs * lgb))
        ob_ref[:, cols] = o_b
        sb_scr[hd] = s_b
        sb_out[0, hd] = s_b


def _retention(zr, decay_logit, batch, seq, s0):
    cs = RET_CHUNK
    nc = seq // cs
    nh = RET_HEADS
    width = nh * RET_DK
    has_s0 = s0 is not None

    def fwd(col):
        return pl.BlockSpec((cs, width), lambda b, c: (b * nc + c, col))

    def bwd(col):
        return pl.BlockSpec((cs, width), lambda b, c: (b * nc + nc - 1 - c, col))

    def state(d):
        return pl.BlockSpec((1, nh, RET_DK, RET_DV), lambda b, c: (b * 2 + d, 0, 0, 0))

    state_out = pl.BlockSpec((1, nh, RET_DK, RET_DV), lambda b, c: (b, 0, 0, 0))
    in_specs = [pl.BlockSpec(memory_space=pltpu.SMEM), fwd(0), fwd(1), fwd(2), bwd(0), bwd(1), bwd(2)]
    args = [decay_logit, zr, zr, zr, zr, zr, zr]
    if has_s0:
        in_specs += [state(0), state(1)]
        args += [s0, s0]
    t = batch * seq
    o_f, o_b, s_f, s_b = pl.pallas_call(
        functools.partial(_ret_kernel, has_s0=has_s0),
        grid=(batch, nc),
        in_specs=in_specs,
        out_specs=[pl.BlockSpec((cs, width), lambda b, c: (b * nc + c, 0)),
                   pl.BlockSpec((cs, width), lambda b, c: (b * nc + nc - 1 - c, 0)),
                   state_out, state_out],
        out_shape=[jax.ShapeDtypeStruct((t, width), F32),
                   jax.ShapeDtypeStruct((t, width), F32),
                   jax.ShapeDtypeStruct((batch, nh, RET_DK, RET_DV), F32),
                   jax.ShapeDtypeStruct((batch, nh, RET_DK, RET_DV), F32)],
        scratch_shapes=[pltpu.VMEM((nh, RET_DK, RET_DV), F32), pltpu.VMEM((nh, RET_DK, RET_DV), F32)],
        compiler_params=_params("arbitrary", "arbitrary"),
        name="retention",
    )(*args)
    return o_f, o_b, s_f, s_b


def _mix_kernel(x_ref, att_ref, of_ref, ob_ref, gr_ref, mod_ref, gnw_ref, wout_ref, n2w_ref, x1_ref, h2_ref):
    o = of_ref[...] + ob_ref[...]
    parts = []
    for hd in range(RET_HEADS):
        oh = o[:, hd * RET_DV:(hd + 1) * RET_DV]
        d = oh - jnp.mean(oh, axis=-1, keepdims=True)
        parts.append(d * lax.rsqrt(jnp.mean(d * d, axis=-1, keepdims=True) + EPS))
    g = gr_ref[...]
    ret = (g * jax.nn.sigmoid(g)) * (jnp.concatenate(parts, axis=-1) * gnw_ref[...])
    na = MLA_HEADS * MLA_V
    mixed = _mm(att_ref[...], wout_ref[:na, :]) + _mm(ret, wout_ref[na:, :])
    m = mod_ref[0]
    x1 = x_ref[...] + m[2:3] * mixed
    x1_ref[...] = x1
    h2_ref[...] = _rms(x1, n2w_ref[...]) * (1.0 + m[4:5]) + m[3:4]


def _mix(x, att, o_f, o_b, zr, mod, w, seq):
    t = x.shape[0]
    tm = ROW_TILE
    per_batch = seq // tm
    const = lambda i: (0, 0)
    row = lambda i: (i, 0)
    half = RET_HEADS * RET_DV
    return pl.pallas_call(
        _mix_kernel,
        grid=(t // tm,),
        in_specs=[pl.BlockSpec((tm, D_MODEL), row),
                  pl.BlockSpec((tm, half), row),
                  pl.BlockSpec((tm, half), row),
                  pl.BlockSpec((tm, half), row),
                  pl.BlockSpec((tm, half), lambda i: (i, 3)),
                  pl.BlockSpec((1, 6, D_MODEL), lambda i: (i // per_batch, 0, 0)),
                  pl.BlockSpec((1, half), const),
                  pl.BlockSpec((D_MODEL, D_MODEL), const),
                  pl.BlockSpec((1, D_MODEL), const)],
        out_specs=[pl.BlockSpec((tm, D_MODEL), row), pl.BlockSpec((tm, D_MODEL), row)],
        out_shape=[jax.ShapeDtypeStruct((t, D_MODEL), F32), jax.ShapeDtypeStruct((t, D_MODEL), F32)],
        compiler_params=_params("arbitrary"),
        name="mix",
    )(x, att, o_f, o_b, zr, mod, w["ret_gn_w"], w["w_out"], w["norm2_w"])


def _top16(jobs):
    def body(r, carry):
        for s_ref, vals_ref, pay_ref, payload in jobs:
            n, cols = s_ref.shape
            rows = lax.broadcasted_iota(jnp.int32, (n, cols), 0).astype(F32)
            s = s_ref[...]
            m = jnp.max(s, axis=0, keepdims=True)
            pos = jnp.min(jnp.where(s == m, rows, float(n)), axis=0, keepdims=True)
            hit = rows == pos
            vals_ref[pl.ds(r, 1), :] = m
            if payload is None:
                pay_ref[pl.ds(r, 1), :] = pos
            else:
                pay_ref[pl.ds(r, 1), :] = jnp.sum(jnp.where(hit, payload, 0.0), axis=0, keepdims=True)
            s_ref[...] = jnp.where(hit, -jnp.inf, s)
        return carry

    lax.fori_loop(0, PEER_TOPK, body, 0)


PAIR_COUNTS = tuple(PEER_TOPK // (a + 1) for a in range(PEER_TOPK))
NUM_PAIRS = sum(PAIR_COUNTS)
PAIR_ROWS = -(-NUM_PAIRS // 8) * 8


HEAD_GROUP = 2
SCORE_REFS = 10


def _score_heads(hbt, wpqts, k1, k2, g_outs, e_outs, scr):
    heads = [scr[h * SCORE_REFS:(h + 1) * SCORE_REFS] for h in range(len(wpqts))]
    for wpqt, (s1_scr, s2_scr, *_) in zip(wpqts, heads):
        qt = jnp.dot(wpqt, hbt, preferred_element_type=F32)
        s1_scr[...] = _mm(k1, qt[:PEER_HALF, :])
        s2_scr[...] = _mm(k2, qt[PEER_HALF:, :])
    _top16([job for (s1, s2, _, _, v1, i1, v2, i2, _, _) in heads for job in ((s1, v1, i1, None), (s2, v2, i2, None))])
    for (_, _, c_scr, p_scr, v1_scr, i1_scr, v2_scr, i2_scr, _, _) in heads:
        c_scr[...] = jnp.full(c_scr.shape, -jnp.inf, F32)
        p_scr[...] = jnp.zeros(p_scr.shape, F32)
        off = 0
        for a, nb in enumerate(PAIR_COUNTS):
            c_scr[off:off + nb, :] = v1_scr[a:a + 1, :] + v2_scr[0:nb, :]
            p_scr[off:off + nb, :] = i1_scr[a:a + 1, :] * float(N_KEYS) + i2_scr[0:nb, :]
            off += nb
    _top16([(c_scr, vt_scr, it_scr, p_scr[...]) for (_, _, c_scr, p_scr, _, _, _, _, vt_scr, it_scr) in heads])
    for (*_, vt_scr, it_scr), g_out, e_out in zip(heads, g_outs, e_outs):
        top = vt_scr[...]
        p = jnp.exp(top - jnp.max(top, axis=0, keepdims=True))
        g_out[...] = p / jnp.sum(p, axis=0, keepdims=True)
        e_out[...] = it_scr[...]


def _score_scratch(tb):
    k = PEER_TOPK
    one = ([pltpu.VMEM((N_KEYS, tb), F32)] * 2 + [pltpu.VMEM((PAIR_ROWS, tb), F32)] * 2
           + [pltpu.VMEM((k, tb), F32)] * 6)
    assert len(one) == SCORE_REFS
    return one * HEAD_GROUP


def _score_group(hbt, wpq_ref, k1_ref, k2_ref, g_ref, e_ref, grp, scr):
    wpqts, g_outs, e_outs = [], [], []
    for h in range(HEAD_GROUP):
        hd = grp * HEAD_GROUP + h
        head_rows = pl.ds(pl.multiple_of(hd * PEER_TOPK, PEER_TOPK), PEER_TOPK)
        wpqts.append(wpq_ref[hd])
        g_outs.append(g_ref.at[head_rows])
        e_outs.append(e_ref.at[head_rows])
    _score_heads(hbt, wpqts, k1_ref[...], k2_ref[...], g_outs, e_outs, scr)


def _peer_kernel(tab_hbm, h2_ref, h2n_ref, wpq_ref, k1_ref, k2_ref, x1_ref, mod_ref, *rest, ntiles, aliased):
    y_ref, idx_smem, g_scr, e_scr, ei_scr, hb_scr = rest[int(aliased):int(aliased) + 6]
    scratch = rest[int(aliased) + 6:]
    _peer_body(tab_hbm, h2_ref, h2n_ref, wpq_ref, k1_ref, k2_ref, x1_ref, mod_ref, y_ref,
               idx_smem, g_scr, e_scr, ei_scr, hb_scr, *scratch, ntiles=ntiles)


def _peer_body(tab_hbm, h2_ref, h2n_ref, wpq_ref, k1_ref, k2_ref, x1_ref, mod_ref, y_ref,
               idx_smem, g_scr, e_scr, ei_scr, hb_scr, *scratch, ntiles):
    bufs = scratch[:GATHER_SLOTS]
    sem_rows, sem_idx = scratch[GATHER_SLOTS:GATHER_SLOTS + 2]
    score_scr = scratch[GATHER_SLOTS + 2:]
    i = pl.program_id(0)
    tb = GATHER_TILE
    ns = GATHER_SLOTS
    ahead = GATHER_AHEAD
    ne = EXPERTS_PER_TOKEN
    slab = SLAB_ROWS
    nphase = PEER_HEADS // HEAD_GROUP
    groups_per_phase = tb // (ns * nphase)
    cur = i % 2
    nxt = 1 - cur
    more = i + 1 < ntiles

    def score(src_ref, grp, slot):
        @pl.when(grp == 0)
        def _():
            hb_scr[...] = src_ref[...].T.astype(BF16)
        _score_group(hb_scr[...], wpq_ref, k1_ref, k2_ref, g_scr.at[slot], e_scr, grp, score_scr)

    def publish(slot):
        ei_scr[...] = e_scr[...].T.astype(jnp.int32)
        copy = pltpu.make_async_copy(ei_scr, idx_smem.at[slot], sem_idx.at[0])
        copy.start()
        copy.wait()

    def issue(islot, tok, rslot):
        for kk in range(ne):
            ex = idx_smem[islot, tok, kk]
            pltpu.make_async_copy(tab_hbm.at[ex], bufs[rslot].at[pl.ds(kk * slab, slab), :],
                                  sem_rows.at[rslot]).start(priority=kk % 2)

    def wait_rows(rslot):
        pltpu.make_async_copy(bufs[rslot], bufs[rslot], sem_rows.at[rslot]).wait()

    @pl.when(i == 0)
    def _():
        def first(grp, carry):
            score(h2_ref, grp, 0)
            return carry
        lax.fori_loop(0, nphase, first, 0)
        publish(0)
        for tok in range(ahead):
            issue(0, tok, tok)

    g2 = mod_ref[0][5:6]
    lane = lax.broadcasted_iota(jnp.int32, (ne, tb), 1)

    def compute(j, rslot):
        rows = bufs[rslot]
        x_row = h2_ref[pl.ds(j, 1), :]
        acc = None
        for c in range(slab):
            u_c = _unpack_u(rows[pl.ds(c, ne, stride=slab), :])
            term = u_c * x_row[:, c * LANE:(c + 1) * LANE]
            acc = term if acc is None else acc + term
        pre = jnp.sum(acc, axis=-1, keepdims=True)
        act = 0.5 * pre * (1.0 + lax.erf(pre * (2.0 ** -0.5)))
        gate = jnp.sum(jnp.where(lane == j, g_scr[cur], 0.0), axis=-1, keepdims=True)
        wgt = gate * act
        outs = []
        for c in range(slab):
            v_c = _unpack_v(rows[pl.ds(c, ne, stride=slab), :])
            outs.append(jnp.sum(v_c * wgt, axis=0, keepdims=True))
        out = jnp.concatenate(outs, axis=-1)
        y_ref[pl.ds(j, 1), :] = x1_ref[pl.ds(j, 1), :] + g2 * out

    def phase(grp, carry):
        score(h2n_ref, grp, nxt)

        @pl.when(grp == nphase - 1)
        def _():
            publish(nxt)

        def group(gq, inner):
            g = grp * groups_per_phase + gq
            for u in range(ns):
                j = g * ns + u
                wait_rows(u)
                jj = j + ahead
                over = jj >= tb
                issue(jnp.where(over, nxt, cur), jnp.where(over, jj - tb, jj), (u + ahead) % ns)
                compute(j, u)
            return inner

        lax.fori_loop(0, groups_per_phase, group, 0)
        return carry

    lax.fori_loop(0, nphase, phase, 0)

    @pl.when(jnp.logical_not(more))
    def _():
        for u in range(ahead):
            wait_rows(u)


def _peer(tab, h2, x1, mod, w, seq, tile0, ntiles, y_prev=None):
    t = h2.shape[0]
    tb = GATHER_TILE
    per_batch = seq // tb
    row = lambda i: (tile0 + i, 0)
    const = lambda i: (0, 0)
    in_specs = [pl.BlockSpec(memory_space=pl.ANY),
                pl.BlockSpec((tb, D_MODEL), row),
                pl.BlockSpec((tb, D_MODEL), lambda i: (tile0 + jnp.minimum(i + 1, ntiles - 1), 0)),
                pl.BlockSpec((PEER_HEADS, PEER_QDIM, D_MODEL), lambda i: (0, 0, 0)),
                pl.BlockSpec((N_KEYS, PEER_HALF), const),
                pl.BlockSpec((N_KEYS, PEER_HALF), const),
                pl.BlockSpec((tb, D_MODEL), row),
                pl.BlockSpec((1, 6, D_MODEL), lambda i: ((tile0 + i) // per_batch, 0, 0))]
    args = [tab, h2, h2, w["w_pq"], w["sub_keys1"], w["sub_keys2"], x1, mod]
    aliases = {}
    if y_prev is not None:
        in_specs.append(pl.BlockSpec(memory_space=pl.ANY))
        args.append(y_prev)
        aliases = {len(args) - 1: 0}
    return pl.pallas_call(
        functools.partial(_peer_kernel, ntiles=ntiles, aliased=y_prev is not None),
        grid=(ntiles,),
        in_specs=in_specs,
        input_output_aliases=aliases,
        out_specs=pl.BlockSpec((tb, D_MODEL), row),
        out_shape=jax.ShapeDtypeStruct((t, D_MODEL), F32),
        scratch_shapes=[pltpu.SMEM((2, tb, EXPERTS_PER_TOKEN), jnp.int32),
                        pltpu.VMEM((2, EXPERTS_PER_TOKEN, tb), F32),
                        pltpu.VMEM((EXPERTS_PER_TOKEN, tb), F32),
                        pltpu.VMEM((tb, EXPERTS_PER_TOKEN), jnp.int32),
                        pltpu.VMEM((D_MODEL, tb), BF16)]
                       + [pltpu.VMEM((EXPERTS_PER_TOKEN * SLAB_ROWS, LANE), jnp.uint32)] * GATHER_SLOTS
                       + [pltpu.SemaphoreType.DMA((GATHER_SLOTS,)), pltpu.SemaphoreType.DMA((1,))]
                       + _score_scratch(tb),
        compiler_params=pltpu.CompilerParams(dimension_semantics=("arbitrary",),
                                             vmem_limit_bytes=VMEM_LIMIT,
                                             disable_bounds_checks=True),
        name="peer",
    )(*args)


def _score_kernel(h2_ref, wpq_ref, k1_ref, k2_ref, *rest, ordered):
    g_ref, e_ref, hb_scr = rest[int(ordered):int(ordered) + 3]
    score_scr = rest[int(ordered) + 3:]
    hb_scr[...] = h2_ref[...].T.astype(BF16)

    def group(grp, carry):
        _score_group(hb_scr[...], wpq_ref, k1_ref, k2_ref, g_ref, e_ref, grp, score_scr)
        return carry

    lax.fori_loop(0, PEER_HEADS // HEAD_GROUP, group, 0)


def _peer_score(h2, w, tile0, ntiles, after=None):
    tb = GATHER_TILE
    const = lambda i: (0, 0)
    out = pl.BlockSpec((EXPERTS_PER_TOKEN, tb), lambda i: (0, i))
    ordering = [] if after is None else [after]
    return pl.pallas_call(
        functools.partial(_score_kernel, ordered=after is not None),
        grid=(ntiles,),
        in_specs=[pl.BlockSpec((tb, D_MODEL), lambda i: (tile0 + i, 0)),
                  pl.BlockSpec((PEER_HEADS, PEER_QDIM, D_MODEL), lambda i: (0, 0, 0)),
                  pl.BlockSpec((N_KEYS, PEER_HALF), const),
                  pl.BlockSpec((N_KEYS, PEER_HALF), const)] + [pl.BlockSpec(memory_space=pl.ANY)] * len(ordering),
        out_specs=[out, out],
        out_shape=[jax.ShapeDtypeStruct((EXPERTS_PER_TOKEN, ntiles * tb), F32)] * 2,
        scratch_shapes=[pltpu.VMEM((D_MODEL, tb), BF16)] + _score_scratch(tb),
        compiler_params=_params("arbitrary"),
        name="peer_score",
    )(h2, w["w_pq"], w["sub_keys1"], w["sub_keys2"], *ordering)


def _act_kernel(pre_ref, g_ref, after_ref, w_ref):
    del after_ref
    pre = pre_ref[...]
    w_ref[...] = g_ref[...] * (0.5 * pre * (1.0 + lax.erf(pre * (2.0 ** -0.5))))


def _peer_act(pre, gate, after):
    t, ne = pre.shape
    tm = math.gcd(t, 1024)
    row = lambda i: (i, 0)
    return pl.pallas_call(
        _act_kernel,
        grid=(t // tm,),
        in_specs=[pl.BlockSpec((tm, ne), row), pl.BlockSpec((tm, ne), row), pl.BlockSpec(memory_space=pl.ANY)],
        out_specs=pl.BlockSpec((tm, ne), row),
        out_shape=jax.ShapeDtypeStruct((t, ne), F32),
        compiler_params=_params("arbitrary"),
        name="peer_act",
    )(pre, gate, after)


def _residual_kernel(x1_ref, out_ref, mod_ref, *rest):
    rest[-1][...] = x1_ref[...] + mod_ref[0][5:6] * out_ref[...]


def _peer_residual(x1, out, mod, seq, tile0, ntiles, out_tile0, y_prev):
    tb = GATHER_TILE
    per_batch = seq // tb
    row = lambda i: (tile0 + i, 0)
    in_specs = [pl.BlockSpec((tb, D_MODEL), row),
                pl.BlockSpec((tb, D_MODEL), lambda i: (out_tile0 + i, 0)),
                pl.BlockSpec((1, 6, D_MODEL), lambda i: ((tile0 + i) // per_batch, 0, 0))]
    args = [x1, out, mod]
    aliases = {}
    if y_prev is not None:
        in_specs.append(pl.BlockSpec(memory_space=pl.ANY))
        args.append(y_prev)
        aliases = {3: 0}
    return pl.pallas_call(
        _residual_kernel,
        grid=(ntiles,),
        in_specs=in_specs,
        input_output_aliases=aliases,
        out_specs=pl.BlockSpec((tb, D_MODEL), row),
        out_shape=jax.ShapeDtypeStruct(x1.shape, F32),
        compiler_params=_params("arbitrary"),
        name="peer_residual",
    )(*args)


SC_WORKERS = 32
SC_LANES = 16
SC_CHUNK = 32
SC_ROWS = 8
SC_SHARE = (107, 256)
TC_FIRST = (68, 149)


def _sc_udot(u_table, idx, x, row0=0):
    t, ne = idx.shape
    assert t % SC_WORKERS == 0 and ne % SC_CHUNK == 0, (t, ne)
    per_w = t // SC_WORKERS
    nchunk = ne // SC_CHUNK
    nl = D_MODEL // SC_LANES
    mesh = plsc.VectorSubcoreMesh(core_axis_name="c", subcore_axis_name="s")

    @functools.partial(
        pl.kernel, out_type=jax.ShapeDtypeStruct((t, ne), F32), mesh=mesh,
        scratch_types=[pltpu.VMEM((ne,), jnp.int32), pltpu.VMEM((D_MODEL,), F32),
                       pltpu.VMEM((2, SC_CHUNK, D_MODEL), F32), pltpu.VMEM((SC_CHUNK, SC_LANES), F32),
                       pltpu.VMEM((ne,), F32), pltpu.SemaphoreType.DMA((2,))],
        compiler_params=pltpu.CompilerParams(needs_layout_passes=False),
        name="sc_udot")
    def run(u_hbm, idx_hbm, x_hbm, pre_hbm, idx_v, x_v, rows, accs, pre_v, sems):
        wid = lax.axis_index("s") * 2 + lax.axis_index("c")
        lanes = lax.iota(jnp.int32, SC_LANES)

        def gather(c, b):
            return pltpu.make_async_copy(u_hbm.at[idx_v.at[pl.ds(c * SC_CHUNK, SC_CHUNK)]], rows.at[b], sems.at[b])

        @pl.loop(0, per_w)
        def _(tt):
            tok = wid * per_w + tt
            pltpu.sync_copy(idx_hbm.at[tok], idx_v)
            pltpu.sync_copy(x_hbm.at[row0 + tok], x_v)
            gather(0, 0).start()
            for c in range(nchunk):
                b = c % 2
                if c + 1 < nchunk:
                    gather(c + 1, 1 - b).start()
                gather(c, b).wait()
                for g in range(SC_CHUNK // SC_ROWS):
                    def body(j, acc):
                        xv = x_v[pl.ds(j * SC_LANES, SC_LANES)]
                        return tuple(acc[q] + rows[b, g * SC_ROWS + q, pl.ds(j * SC_LANES, SC_LANES)] * xv
                                     for q in range(SC_ROWS))
                    acc = plsc.parallel_loop(0, nl, unroll=2,
                                             carry=tuple(jnp.zeros((SC_LANES,), F32) for _ in range(SC_ROWS)))(body)
                    for q in range(SC_ROWS):
                        accs[g * SC_ROWS + q, :] = acc[q]
                for part in range(SC_CHUNK // SC_LANES):
                    tot = jnp.zeros((SC_LANES,), F32)
                    for l in range(SC_LANES):
                        tot = tot + plsc.load_gather(accs, [lanes + part * SC_LANES,
                                                            jnp.full((SC_LANES,), l, jnp.int32)])
                    pre_v[pl.ds(c * SC_CHUNK + part * SC_LANES, SC_LANES)] = tot
            pltpu.sync_copy(pre_v, pre_hbm.at[tok])

    return run(u_table, idx, x)


def _sc_vsum(v_table, idx, wgt):
    t, ne = idx.shape
    assert t % SC_WORKERS == 0 and ne % SC_CHUNK == 0, (t, ne)
    per_w = t // SC_WORKERS
    nchunk = ne // SC_CHUNK
    block = 16 * SC_LANES
    mesh = plsc.VectorSubcoreMesh(core_axis_name="c", subcore_axis_name="s")

    @functools.partial(
        pl.kernel, out_type=jax.ShapeDtypeStruct((t, D_MODEL), F32), mesh=mesh,
        scratch_types=[pltpu.VMEM((ne,), jnp.int32), pltpu.VMEM((ne,), F32),
                       pltpu.VMEM((2, SC_CHUNK, D_MODEL), F32), pltpu.VMEM((D_MODEL,), F32),
                       pltpu.SemaphoreType.DMA((2,))],
        compiler_params=pltpu.CompilerParams(needs_layout_passes=False),
        name="sc_vsum")
    def run(v_hbm, idx_hbm, w_hbm, out_hbm, idx_v, w_v, rows, out_v, sems):
        wid = lax.axis_index("s") * 2 + lax.axis_index("c")

        def gather(c, b):
            return pltpu.make_async_copy(v_hbm.at[idx_v.at[pl.ds(c * SC_CHUNK, SC_CHUNK)]], rows.at[b], sems.at[b])

        @pl.loop(0, per_w)
        def _(tt):
            tok = wid * per_w + tt
            pltpu.sync_copy(idx_hbm.at[tok], idx_v)
            pltpu.sync_copy(w_hbm.at[tok], w_v)
            gather(0, 0).start()
            for c in range(nchunk):
                b = c % 2
                if c + 1 < nchunk:
                    gather(c + 1, 1 - b).start()
                gather(c, b).wait()
                for d in range(D_MODEL // block):
                    def body(k, acc):
                        wk = plsc.load_gather(w_v, [jnp.full((SC_LANES,), c * SC_CHUNK, jnp.int32) + k])
                        return tuple(acc[j] + rows[b, k, pl.ds(d * block + j * SC_LANES, SC_LANES)] * wk
                                     for j in range(16))
                    if c == 0:
                        init = tuple(jnp.zeros((SC_LANES,), F32) for _ in range(16))
                    else:
                        init = tuple(out_v[pl.ds(d * block + j * SC_LANES, SC_LANES)] for j in range(16))
                    acc = plsc.parallel_loop(0, SC_CHUNK, carry=init)(body)
                    for j in range(16):
                        out_v[pl.ds(d * block + j * SC_LANES, SC_LANES)] = acc[j]
            pltpu.sync_copy(out_v, out_hbm.at[tok])

    return run(v_table, idx, wgt)


def _expert_slabs(u_table, v_table):
    n = u_table.shape[0]
    hi = lax.bitcast_convert_type(u_table.astype(BF16), jnp.uint16).astype(jnp.uint32) << 16
    lo = lax.bitcast_convert_type(v_table.astype(BF16), jnp.uint16).astype(jnp.uint32)
    return (hi | lo).reshape(n, SLAB_ROWS, LANE)


def _unpack_u(words):
    return lax.bitcast_convert_type(words & jnp.uint32(0xFFFF0000), F32)


def _unpack_v(words):
    return lax.bitcast_convert_type(words << 16, F32)


def _rope_tables(seq):
    rows = seq // GRID_W
    r = jnp.repeat(jnp.arange(rows, dtype=F32), GRID_W)
    col = jnp.tile(jnp.arange(GRID_W, dtype=F32), rows)
    nf = ROPE_AXIS // 2
    freqs = jnp.power(ROPE_BASE, -jnp.arange(nf, dtype=F32) / nf)
    ar, ac = r[:, None] * freqs, col[:, None] * freqs
    pad = jnp.zeros((seq, LANE - MLA_ROPE), F32)
    cos = jnp.concatenate([jnp.cos(ar), jnp.cos(ar), jnp.cos(ac), jnp.cos(ac), pad], axis=-1)
    sin = jnp.concatenate([-jnp.sin(ar), jnp.sin(ar), -jnp.sin(ac), jnp.sin(ac), pad], axis=-1)
    return cos, sin


def _prep_weights(norm1_w, w_in, q_norm_w, w_uq, kv_norm_w, w_ukv, q_head_norm_w, k_head_norm_w,
                  ret_gn_w, w_out, norm2_w, w_pq, sub_keys1, sub_keys2):
    cut = Q_RANK + KV_RANK + MLA_ROPE
    w_in_p = jnp.concatenate([w_in[:, :cut], jnp.zeros((D_MODEL, LANE - MLA_ROPE), F32), w_in[:, cut:]], axis=1)
    wq = w_uq.reshape(Q_RANK, MLA_HEADS, MLA_QK)
    wq = jnp.pad(wq, ((0, 0), (0, 0), (0, HEAD_PAD - MLA_QK))).reshape(Q_RANK, MLA_HEADS * HEAD_PAD)

    def head_w(v):
        return jnp.pad(v, (0, HEAD_PAD - MLA_QK)).reshape(1, HEAD_PAD)

    return {
        "norm1_w": norm1_w.reshape(1, -1), "w_in": w_in_p.astype(BF16),
        "q_norm_w": q_norm_w.reshape(1, -1), "kv_norm_w": kv_norm_w.reshape(1, -1),
        "w_uq": wq.astype(BF16), "w_ukv": w_ukv.astype(BF16),
        "q_head_norm_w": head_w(q_head_norm_w), "k_head_norm_w": head_w(k_head_norm_w),
        "ret_gn_w": ret_gn_w.reshape(1, -1), "w_out": w_out.astype(BF16), "norm2_w": norm2_w.reshape(1, -1),
        "w_pq": w_pq.astype(BF16).reshape(D_MODEL, PEER_HEADS, PEER_QDIM).transpose(1, 2, 0), "sub_keys1": sub_keys1.astype(BF16), "sub_keys2": sub_keys2.astype(BF16),
    }


def _trunk(x, mod, w, decay_logit, ctx, after=None):
    batch, seq, _ = x.shape
    x2 = x.reshape(batch * seq, D_MODEL)
    use_rope = ctx is not None
    if use_rope:
        cos, sin = _rope_tables(seq)
    else:
        cos = sin = jnp.zeros((ROW_TILE, LANE), F32)
    ckv, krope, q, k, v, zr = _in_proj(x2, mod, w, cos, sin, seq, use_rope, after)
    if use_rope:
        ckv_c, krope_c, state_c = ctx
        past = ckv_c.shape[1]
        kr_c = jnp.pad(krope_c.reshape(batch * past, MLA_ROPE), ((0, 0), (0, LANE - MLA_ROPE)))
        ctx_kv = _kv_up(ckv_c.reshape(batch * past, KV_RANK), kr_c, w)
        s0 = state_c.reshape(batch * 2, RET_HEADS, RET_DK, RET_DV)
    else:
        ctx_kv, s0 = None, None
    att = _attention(q, k, v, batch, seq, ctx_kv)
    o_f, o_b, s_f, s_b = _retention(zr, decay_logit, batch, seq, s0)
    x1, h2 = _mix(x2, att, o_f, o_b, zr, mod, w, seq)
    states = jnp.stack([s_f, s_b], axis=1)
    return x1, h2, ckv.reshape(batch, seq, KV_RANK), krope.reshape(batch, seq, MLA_ROPE), states


def _sc_select(h2, w, tile0, ntiles, after=None):
    gt, et = _peer_score(h2, w, tile0, ntiles, after)
    return et.T.astype(jnp.int32), gt.T


def _peer_split(ctx_sc, x_ctx, x_lat, mod_ctx, mod_lat, seq_ctx, seq_lat, w, tab):
    (x1c, h2c), (x1l, h2l) = x_ctx, x_lat
    tiles_c = h2c.shape[0] // GATHER_TILE
    tiles_l = h2l.shape[0] // GATHER_TILE
    sc_tiles = tiles_l * SC_SHARE[0] // SC_SHARE[1]
    tc_tiles = tiles_l - sc_tiles
    first = tc_tiles * TC_FIRST[0] // TC_FIRST[1]

    idx_c, gate_c, pre_c = ctx_sc
    wgt_c = _peer_act(pre_c, gate_c, h2l)
    idx_l, gate_l = _sc_select(h2l, w, tc_tiles, sc_tiles, wgt_c)
    pre_l = _sc_udot(w["u_table"], idx_l, h2l, tc_tiles * GATHER_TILE)
    y = _peer(tab, h2l, x1l, mod_lat, w, seq_lat, 0, first)
    idx = jnp.concatenate([idx_c, idx_l], axis=0)
    wgt = jnp.concatenate([wgt_c, _peer_act(pre_l, gate_l, y)], axis=0)
    out = _sc_vsum(w["v_table"], idx, wgt)
    y = _peer(tab, h2l, x1l, mod_lat, w, seq_lat, first, tc_tiles - first, y_prev=y)
    y_lat = _peer_residual(x1l, out, mod_lat, seq_lat, tc_tiles, sc_tiles, tiles_c, y)
    y_ctx = _peer_residual(x1c, out, mod_ctx, seq_ctx, 0, tiles_c, 0, None)
    return y_ctx, y_lat


def kernel(x_prompt, x_sample, c, cache_ckv, cache_krope, state_ret, c_ctx, w_ada, b_ada, norm1_w, w_in,
           q_norm_w, w_uq, kv_norm_w, w_ukv, q_head_norm_w, k_head_norm_w, ret_decay_logit, ret_gn_w,
           w_out, norm2_w, w_pq, sub_keys1, sub_keys2, u_table, v_table):
    depth = w_ada.shape[0]
    nb_ctx = x_prompt.shape[0]
    nb_lat = x_sample.shape[0]
    y_prompt, y_sample = x_prompt, x_sample
    ckv_list, krope_list, ret_list = [], [], []
    for l in range(depth):
        cond_rows = -(-(nb_lat + 1) // 8) * 8
        cond = jnp.concatenate([c, c_ctx[None, :], jnp.zeros((cond_rows - nb_lat - 1, D_MODEL), F32)], axis=0)
        mod = _ada(cond, w_ada[l], b_ada[l])
        mod_lat = mod[:nb_lat].reshape(nb_lat, 6, D_MODEL)
        mod_ctx = jnp.broadcast_to(mod[nb_lat].reshape(1, 6, D_MODEL), (nb_ctx, 6, D_MODEL))
        w = _prep_weights(norm1_w[l], w_in[l], q_norm_w[l], w_uq[l], kv_norm_w[l], w_ukv[l], q_head_norm_w[l],
                          k_head_norm_w[l], ret_gn_w[l], w_out[l], norm2_w[l], w_pq[l], sub_keys1[l], sub_keys2[l])
        tab = _expert_slabs(u_table[l], v_table[l])
        w["u_table"], w["v_table"] = u_table[l], v_table[l]
        x1c, h2c, ckv_l, krope_l, ret_l = _trunk(y_prompt, mod_ctx, w, ret_decay_logit[l], None)
        ckv_list.append(ckv_l)
        krope_list.append(krope_l)
        ret_list.append(ret_l)
        idx_c, gate_c = _sc_select(h2c, w, 0, h2c.shape[0] // GATHER_TILE)
        pre_c = _sc_udot(w["u_table"], idx_c, h2c)
        x1l, h2l, _, _, _ = _trunk(y_sample, mod_lat, w, ret_decay_logit[l],
                                   (cache_ckv[:, l], cache_krope[:, l], state_ret[:, l]), after=idx_c)
        y_ctx, y_lat = _peer_split((idx_c, gate_c, pre_c), (x1c, h2c), (x1l, h2l), mod_ctx, mod_lat,
                                   y_prompt.shape[1], y_sample.shape[1], w, tab)
        y_prompt = y_ctx.reshape(y_prompt.shape)
        y_sample = y_lat.reshape(y_sample.shape)
    return (y_prompt, y_sample, jnp.stack(ckv_list, axis=1), jnp.stack(krope_list, axis=1),
            jnp.stack(ret_list, axis=1))
```
